```python
import math
import jax, jax.numpy as jnp
from jax import lax
import numpy as np

D_MODEL = 1024
BATCH = 16
SEQ = 2048
DEPTH = 2

D_MIX = 1024
HEAD_DIM = 64
ATT_HEADS = 6
ATT_WIDTH = ATT_HEADS * HEAD_DIM
DILATED_PAIRS = ((128, 1), (512, 4), (2048, 16))
ATT_BLOCK = 128
SSD_HEADS = 6
SSD_HEADDIM = 64
SSD_WIDTH = SSD_HEADS * SSD_HEADDIM
SSD_GROUPS = 2
SSD_STATE = 128
SSD_CONV = 4
SSD_CHUNK = 128
SSD_CONV_DIM = SSD_WIDTH + 2 * SSD_GROUPS * SSD_STATE
SGU_GROUPS = 4
SGU_GROUP_DIM = 64
SGU_WIDTH = SGU_GROUPS * SGU_GROUP_DIM
SGU_CHUNK = 128
D_FF = 2816
D_IN = 3 * ATT_WIDTH + SSD_WIDTH + SSD_CONV_DIM + SSD_HEADS + 2 * SGU_WIDTH
RMS_EPS = 1e-6
LN_EPS = 1e-5

kernel_name = 'hybrid_dilated_ssd_sgu_macaron'


def _rmsnorm(x, g):
    xf = x.astype(jnp.float32)
    y = xf * lax.rsqrt(jnp.mean(xf * xf, axis=-1, keepdims=True) + RMS_EPS)
    return (y * g.astype(jnp.float32)).astype(x.dtype)


def _swiglu(x, w_gate, w_up, w_down):
    return (jax.nn.silu(x @ w_gate) * (x @ w_up)) @ w_down


def _to_blocks(t, dil, n_blocks):
    b, s, h, e = t.shape
    L = s // dil
    t = t.reshape(b, L, dil, h, e).transpose(0, 2, 1, 3, 4)
    t = jnp.pad(t, ((0, 0), (0, 0), (0, n_blocks * ATT_BLOCK - L), (0, 0), (0, 0)))
    return t.reshape(b, dil, n_blocks, ATT_BLOCK, h, e)


def _with_prev_block(t):
    prev = jnp.pad(t, ((0, 0), (0, 0), (1, 0), (0, 0), (0, 0), (0, 0)))[:, :, :-1]
    return jnp.concatenate([prev, t], axis=3)


def _dilated_branch(q, k, v, window, dil):
    b, s, h, e = q.shape
    L = s // dil
    nb = -(-L // ATT_BLOCK)
    span = window // dil
    qb = _to_blocks(q, dil, nb)
    kk = _with_prev_block(_to_blocks(k, dil, nb))
    vv = _with_prev_block(_to_blocks(v, dil, nb))
    scores = jnp.einsum('bdnqhe,bdnkhe->bdnhqk', qb, kk,
                        preferred_element_type=jnp.float32) * (e ** -0.5)
    qi = jnp.arange(ATT_BLOCK)[:, None]
    kj = jnp.arange(2 * ATT_BLOCK)[None, :]
    dist = qi + ATT_BLOCK - kj
    band = (dist >= 0) & (dist <= span)
    valid_key = (jnp.arange(nb)[:, None, None] > 0) | (kj[None] >= ATT_BLOCK)
    mask = (band[None] & valid_key)[:, None]
    scores = jnp.where(mask, scores, -jnp.inf)
    m = jnp.max(scores, axis=-1, keepdims=True)
    p = jnp.exp(scores - m)
    den = jnp.sum(p, axis=-1)
    o = jnp.einsum('bdnhqk,bdnkhe->bdnqhe', p.astype(v.dtype), vv)
    o = o / den.transpose(0, 1, 2, 4, 3)[..., None]
    lse = (m[..., 0] + jnp.log(den)).transpose(0, 1, 2, 4, 3)
    o = o.reshape(b, dil, nb * ATT_BLOCK, h, e)[:, :, :L].transpose(0, 2, 1, 3, 4).reshape(b, s, h, e)
    lse = lse.reshape(b, dil, nb * ATT_BLOCK, h)[:, :, :L].transpose(0, 2, 1, 3).reshape(b, s, h)
    return o, lse


def _dilated_attention(q, k, v):
    outs, lses = [], []
    for window, dil in DILATED_PAIRS:
        o, lse = _dilated_branch(q, k, v, window, dil)
        outs.append(o)
        lses.append(lse)
    w = jax.nn.softmax(jnp.stack(lses, axis=0), axis=0)
    out = w[0][..., None] * outs[0]
    for i in range(1, len(outs)):
        out = out + w[i][..., None] * outs[i]
    return out.astype(q.dtype)


def _ssd_mixer(z, xbc, dt_raw, conv_w, conv_b, dt_bias, a_log, d_skip, norm_g):
    b, s, _ = xbc.shape
    G, J, P, N = SSD_GROUPS, SSD_HEADS // SSD_GROUPS, SSD_HEADDIM, SSD_STATE
    c, l = s // SSD_CHUNK, SSD_CHUNK
    xbc = lax.conv_general_dilated(xbc, conv_w[:, None, :].astype(xbc.dtype), window_strides=(1,),
                                   padding=[(SSD_CONV - 1, 0)],
                                   dimension_numbers=('NWC', 'WIO', 'NWC'),
                                   feature_group_count=SSD_CONV_DIM)
    xbc = jax.nn.silu(xbc + conv_b)
    xs, bm, cm = jnp.split(xbc, [SSD_WIDTH, SSD_WIDTH + G * N], axis=-1)
    dt = jax.nn.softplus(dt_raw.astype(jnp.float32) + dt_bias.astype(jnp.float32))
    a = dt * (-jnp.exp(a_log.astype(jnp.float32)))
    xs_h = xs.astype(jnp.float32).reshape(b, s, SSD_HEADS, P)
    X = (xs_h * dt[..., None]).reshape(b, c, l, G, J, P)
    Bc = bm.astype(jnp.float32).reshape(b, c, l, G, N)
    Cc = cm.astype(jnp.float32).reshape(b, c, l, G, N)
    a_cs = jnp.cumsum(a.reshape(b, c, l, G, J), axis=2)
    causal = jnp.tril(jnp.ones((l, l), dtype=bool))[:, :, None, None]
    seg = a_cs[:, :, :, None] - a_cs[:, :, None, :]
    decay_in = jnp.exp(jnp.where(causal, seg, -jnp.inf))
    cb = jnp.einsum('bclgn,bcsgn->bclsg', Cc, Bc)
    y_diag = jnp.einsum('bclsg,bclsgj,bcsgjp->bclgjp', cb, decay_in, X)
    decay_to_end = jnp.exp(a_cs[:, :, -1:] - a_cs)
    chunk_states = jnp.einsum('bclgn,bclgj,bclgjp->bcgjpn', Bc, decay_to_end, X)
    chunk_decay = jnp.exp(a_cs[:, :, -1])

    def step(h, inp):
        st, dec = inp
        return h * dec[..., None, None] + st, h

    h0 = jnp.zeros((b, G, J, P, N), jnp.float32)
    _, prev = lax.scan(step, h0, (chunk_states.transpose(1, 0, 2, 3, 4, 5),
                                  chunk_decay.transpose(1, 0, 2, 3)))
    prev = prev.transpose(1, 0, 2, 3, 4, 5)
    y_off = jnp.einsum('bclgn,bcgjpn,bclgj->bclgjp', Cc, prev, jnp.exp(a_cs))
    y = (y_diag + y_off).reshape(b, s, SSD_HEADS, P) + d_skip.astype(jnp.float32)[:, None] * xs_h
    y = y.reshape(b, s, G, J * P) * jax.nn.silu(z.astype(jnp.float32)).reshape(b, s, G, J * P)
    y = y * lax.rsqrt(jnp.mean(y * y, axis=-1, keepdims=True) + RMS_EPS)
    y = y.reshape(b, s, SSD_WIDTH) * norm_g.astype(jnp.float32)
    return y.astype(z.dtype)


def _sgu_mixer(uv, ln_g, ln_b, w_s, b_s):
    b, s, _ = uv.shape
    uv = jax.nn.gelu(uv, approximate=False)
    u, v = jnp.split(uv, 2, axis=-1)
    vf = v.astype(jnp.float32)
    mu = jnp.mean(vf, axis=-1, keepdims=True)
    var = jnp.mean(jnp.square(vf - mu), axis=-1, keepdims=True)
    vn = (vf - mu) * lax.rsqrt(var + LN_EPS) * ln_g.astype(jnp.float32) + ln_b.astype(jnp.float32)
    vc = vn.reshape(b, s // SGU_CHUNK, SGU_CHUNK, SGU_GROUPS, SGU_GROUP_DIM)
    tri = jnp.tril(jnp.ones((SGU_CHUNK, SGU_CHUNK), dtype=bool))[None]
    w_causal = jnp.where(tri, w_s.astype(jnp.float32), 0.0)
    mixed = jnp.einsum('gts,bnsgc->bntgc', w_causal, vc) + b_s.astype(jnp.float32).T[:, :, None]
    return (u.astype(jnp.float32) * mixed.reshape(b, s, SGU_WIDTH)).astype(uv.dtype)


def _fwd_setup_inputs(seed: int = 0) -> dict:
    key = jax.random.key(seed)
    ks = jax.random.split(key, 32)
    f32 = jnp.float32
    nrm = lambda k, shape, scale: jax.random.normal(k, shape, f32) * scale
    gain = lambda k, shape: jnp.ones(shape, f32) + 0.02 * jax.random.normal(k, shape, f32)
    dt0 = jnp.exp(jax.random.uniform(ks[9], (DEPTH, SSD_HEADS), f32, math.log(1e-3), math.log(1e-1)))
    return {
        'x': jax.random.normal(ks[0], (BATCH, SEQ, D_MODEL), f32),
        'ffn1_norm': gain(ks[1], (DEPTH, D_MODEL)),
        'ffn1_w_gate': nrm(ks[2], (DEPTH, D_MODEL, D_FF), D_MODEL ** -0.5),
        'ffn1_w_up': nrm(ks[3], (DEPTH, D_MODEL, D_FF), D_MODEL ** -0.5),
        'ffn1_w_down': nrm(ks[4], (DEPTH, D_FF, D_MODEL), D_FF ** -0.5),
        'mix_norm': gain(ks[5], (DEPTH, D_MODEL)),
        'w_in': nrm(ks[6], (DEPTH, D_MODEL, D_IN), D_MODEL ** -0.5),
        'conv_w': nrm(ks[7], (DEPTH, SSD_CONV, SSD_CONV_DIM), SSD_CONV ** -0.5),
        'conv_b': nrm(ks[8], (DEPTH, SSD_CONV_DIM), 0.02),
        'dt_bias': dt0 + jnp.log(-jnp.expm1(-dt0)),
        'a_log': jnp.log(jax.random.uniform(ks[10], (DEPTH, SSD_HEADS), f32, 1.0, 16.0)),
        'd_skip': jnp.ones((DEPTH, SSD_HEADS), f32) + 0.1 * jax.random.normal(ks[11], (DEPTH, SSD_HEADS), f32),
        'ssd_norm': gain(ks[12], (DEPTH, SSD_WIDTH)),
        'sgu_ln_g': gain(ks[13], (DEPTH, SGU_WIDTH)),
        'sgu_ln_b': nrm(ks[14], (DEPTH, SGU_WIDTH), 0.02),
        'sgu_w': nrm(ks[15], (DEPTH, SGU_GROUPS, SGU_CHUNK, SGU_CHUNK), SGU_CHUNK ** -0.5),
        'sgu_b': jnp.ones((DEPTH, SGU_GROUPS, SGU_CHUNK), f32) + 0.1 * jax.random.normal(ks[16], (DEPTH, SGU_GROUPS, SGU_CHUNK), f32),
        'w_out': nrm(ks[17], (DEPTH, D_MIX, D_MODEL), D_MIX ** -0.5),
        'ffn2_norm': gain(ks[18], (DEPTH, D_MODEL)),
        'ffn2_w_gate': nrm(ks[19], (DEPTH, D_MODEL, D_FF), D_MODEL ** -0.5),
        'ffn2_w_up': nrm(ks[20], (DEPTH, D_MODEL, D_FF), D_MODEL ** -0.5),
        'ffn2_w_down': nrm(ks[21], (DEPTH, D_FF, D_MODEL), D_FF ** -0.5),
        'final_norm': gain(ks[22], (D_MODEL,)),
    }


def _fwd_reference(x, ffn1_norm, ffn1_w_gate, ffn1_w_up, ffn1_w_down, mix_norm, w_in, conv_w, conv_b,
              dt_bias, a_log, d_skip, ssd_norm, sgu_ln_g, sgu_ln_b, sgu_w, sgu_b, w_out,
              ffn2_norm, ffn2_w_gate, ffn2_w_up, ffn2_w_down, final_norm):
    b, s, _ = x.shape
    widths = [ATT_WIDTH, ATT_WIDTH, ATT_WIDTH, SSD_WIDTH, SSD_CONV_DIM, SSD_HEADS]
    offsets = []
    acc = 0
    for wdt in widths:
        acc += wdt
        offsets.append(acc)
    for i in range(DEPTH):
        x = x + 0.5 * _swiglu(_rmsnorm(x, ffn1_norm[i]), ffn1_w_gate[i], ffn1_w_up[i], ffn1_w_down[i])
        h = _rmsnorm(x, mix_norm[i])
        proj = h @ w_in[i]
        q, k, v, z, xbc, dt_raw, uv = jnp.split(proj, offsets, axis=-1)
        hs = (b, s, ATT_HEADS, HEAD_DIM)
        y_att = _dilated_attention(q.reshape(hs), k.reshape(hs), v.reshape(hs)).reshape(b, s, ATT_WIDTH)
        y_ssd = _ssd_mixer(z, xbc, dt_raw, conv_w[i], conv_b[i], dt_bias[i], a_log[i], d_skip[i], ssd_norm[i])
        y_sgu = _sgu_mixer(uv, sgu_ln_g[i], sgu_ln_b[i], sgu_w[i], sgu_b[i])
        x = x + jnp.concatenate([y_att, y_ssd, y_sgu], axis=-1) @ w_out[i]
        x = x + 0.5 * _swiglu(_rmsnorm(x, ffn2_norm[i]), ffn2_w_gate[i], ffn2_w_up[i], ffn2_w_down[i])
    return _rmsnorm(x, final_norm)


import jax as _jax
import jax.numpy as _jnp

TWIN_FORMAT = 'train_step'
FWD_PARAMS = ['x', 'ffn1_norm', 'ffn1_w_gate', 'ffn1_w_up', 'ffn1_w_down', 'mix_norm', 'w_in', 'conv_w', 'conv_b', 'dt_bias', 'a_log', 'd_skip', 'ssd_norm', 'sgu_ln_g', 'sgu_ln_b', 'sgu_w', 'sgu_b', 'w_out', 'ffn2_norm', 'ffn2_w_gate', 'ffn2_w_up', 'ffn2_w_down', 'final_norm']
TWIN_WEIGHTS = ['ffn1_norm', 'ffn1_w_gate', 'ffn1_w_up', 'ffn1_w_down', 'mix_norm', 'w_in', 'conv_w', 'conv_b', 'dt_bias', 'a_log', 'd_skip', 'ssd_norm', 'sgu_ln_g', 'sgu_ln_b', 'sgu_w', 'sgu_b', 'w_out', 'ffn2_norm', 'ffn2_w_gate', 'ffn2_w_up', 'ffn2_w_down', 'final_norm']
TWIN_DIFF_INPUT = 'x'
TWIN_INPUTS = ['x', 'ffn1_norm', 'ffn1_w_gate', 'ffn1_w_up', 'ffn1_w_down', 'mix_norm', 'w_in', 'conv_w', 'conv_b', 'dt_bias', 'a_log', 'd_skip', 'ssd_norm', 'sgu_ln_g', 'sgu_ln_b', 'sgu_w', 'sgu_b', 'w_out', 'ffn2_norm', 'ffn2_w_gate', 'ffn2_w_up', 'ffn2_w_down', 'final_norm', 'loss_target', 'm_ffn1_norm', 'm_ffn1_w_gate', 'm_ffn1_w_up', 'm_ffn1_w_down', 'm_mix_norm', 'm_w_in', 'm_conv_w', 'm_conv_b', 'm_dt_bias', 'm_a_log', 'm_d_skip', 'm_ssd_norm', 'm_sgu_ln_g', 'm_sgu_ln_b', 'm_sgu_w', 'm_sgu_b', 'm_w_out', 'm_ffn2_norm', 'm_ffn2_w_gate', 'm_ffn2_w_up', 'm_ffn2_w_down', 'm_final_norm', 'v_ffn1_norm', 'v_ffn1_w_gate', 'v_ffn1_w_up', 'v_ffn1_w_down', 'v_mix_norm', 'v_w_in', 'v_conv_w', 'v_conv_b', 'v_dt_bias', 'v_a_log', 'v_d_skip', 'v_ssd_norm', 'v_sgu_ln_g', 'v_sgu_ln_b', 'v_sgu_w', 'v_sgu_b', 'v_w_out', 'v_ffn2_norm', 'v_ffn2_w_gate', 'v_ffn2_w_up', 'v_ffn2_w_down', 'v_final_norm']
TWIN_OUTPUTS = ['loss', 'grad_x', 'grad_ffn1_norm', 'grad_ffn1_w_gate', 'grad_ffn1_w_up', 'grad_ffn1_w_down', 'grad_mix_norm', 'grad_w_in', 'grad_conv_w', 'grad_conv_b', 'grad_dt_bias', 'grad_a_log', 'grad_d_skip', 'grad_ssd_norm', 'grad_sgu_ln_g', 'grad_sgu_ln_b', 'grad_sgu_w', 'grad_sgu_b', 'grad_w_out', 'grad_ffn2_norm', 'grad_ffn2_w_gate', 'grad_ffn2_w_up', 'grad_ffn2_w_down', 'grad_final_norm', 'delta_ffn1_norm', 'delta_ffn1_w_gate', 'delta_ffn1_w_up', 'delta_ffn1_w_down', 'delta_mix_norm', 'delta_w_in', 'delta_conv_w', 'delta_conv_b', 'delta_dt_bias', 'delta_a_log', 'delta_d_skip', 'delta_ssd_norm', 'delta_sgu_ln_g', 'delta_sgu_ln_b', 'delta_sgu_w', 'delta_sgu_b', 'delta_w_out', 'delta_ffn2_norm', 'delta_ffn2_w_gate', 'delta_ffn2_w_up', 'delta_ffn2_w_down', 'delta_final_norm', 'new_m_ffn1_norm', 'new_m_ffn1_w_gate', 'new_m_ffn1_w_up', 'new_m_ffn1_w_down', 'new_m_mix_norm', 'new_m_w_in', 'new_m_conv_w', 'new_m_conv_b', 'new_m_dt_bias', 'new_m_a_log', 'new_m_d_skip', 'new_m_ssd_norm', 'new_m_sgu_ln_g', 'new_m_sgu_ln_b', 'new_m_sgu_w', 'new_m_sgu_b', 'new_m_w_out', 'new_m_ffn2_norm', 'new_m_ffn2_w_gate', 'new_m_ffn2_w_up', 'new_m_ffn2_w_down', 'new_m_final_norm', 'new_v_ffn1_norm', 'new_v_ffn1_w_gate', 'new_v_ffn1_w_up', 'new_v_ffn1_w_down', 'new_v_mix_norm', 'new_v_w_in', 'new_v_conv_w', 'new_v_conv_b', 'new_v_dt_bias', 'new_v_a_log', 'new_v_d_skip', 'new_v_ssd_norm', 'new_v_sgu_ln_g', 'new_v_sgu_ln_b', 'new_v_sgu_w', 'new_v_sgu_b', 'new_v_w_out', 'new_v_ffn2_norm', 'new_v_ffn2_w_gate', 'new_v_ffn2_w_up', 'new_v_ffn2_w_down', 'new_v_final_norm']
TWIN_LEAF_KINDS = {'loss': 'loss', 'grad_x': 'grad_x', 'grad_ffn1_norm': 'grad_w', 'grad_ffn1_w_gate': 'grad_w', 'grad_ffn1_w_up': 'grad_w', 'grad_ffn1_w_down': 'grad_w', 'grad_mix_norm': 'grad_w', 'grad_w_in': 'grad_w', 'grad_conv_w': 'grad_w', 'grad_conv_b': 'grad_w', 'grad_dt_bias': 'grad_w', 'grad_a_log': 'grad_w', 'grad_d_skip': 'grad_w', 'grad_ssd_norm': 'grad_w', 'grad_sgu_ln_g': 'grad_w', 'grad_sgu_ln_b': 'grad_w', 'grad_sgu_w': 'grad_w', 'grad_sgu_b': 'grad_w', 'grad_w_out': 'grad_w', 'grad_ffn2_norm': 'grad_w', 'grad_ffn2_w_gate': 'grad_w', 'grad_ffn2_w_up': 'grad_w', 'grad_ffn2_w_down': 'grad_w', 'grad_final_norm': 'grad_w', 'delta_ffn1_norm': 'delta_w', 'delta_ffn1_w_gate': 'delta_w', 'delta_ffn1_w_up': 'delta_w', 'delta_ffn1_w_down': 'delta_w', 'delta_mix_norm': 'delta_w', 'delta_w_in': 'delta_w', 'delta_conv_w': 'delta_w', 'delta_conv_b': 'delta_w', 'delta_dt_bias': 'delta_w', 'delta_a_log': 'delta_w', 'delta_d_skip': 'delta_w', 'delta_ssd_norm': 'delta_w', 'delta_sgu_ln_g': 'delta_w', 'delta_sgu_ln_b': 'delta_w', 'delta_sgu_w': 'delta_w', 'delta_sgu_b': 'delta_w', 'delta_w_out': 'delta_w', 'delta_ffn2_norm': 'delta_w', 'delta_ffn2_w_gate': 'delta_w', 'delta_ffn2_w_up': 'delta_w', 'delta_ffn2_w_down': 'delta_w', 'delta_final_norm': 'delta_w', 'new_m_ffn1_norm': 'new_m', 'new_m_ffn1_w_gate': 'new_m', 'new_m_ffn1_w_up': 'new_m', 'new_m_ffn1_w_down': 'new_m', 'new_m_mix_norm': 'new_m', 'new_m_w_in': 'new_m', 'new_m_conv_w': 'new_m', 'new_m_conv_b': 'new_m', 'new_m_dt_bias': 'new_m', 'new_m_a_log': 'new_m', 'new_m_d_skip': 'new_m', 'new_m_ssd_norm': 'new_m', 'new_m_sgu_ln_g': 'new_m', 'new_m_sgu_ln_b': 'new_m', 'new_m_sgu_w': 'new_m', 'new_m_sgu_b': 'new_m', 'new_m_w_out': 'new_m', 'new_m_ffn2_norm': 'new_m', 'new_m_ffn2_w_gate': 'new_m', 'new_m_ffn2_w_up': 'new_m', 'new_m_ffn2_w_down': 'new_m', 'new_m_final_norm': 'new_m', 'new_v_ffn1_norm': 'new_v', 'new_v_ffn1_w_gate': 'new_v', 'new_v_ffn1_w_up': 'new_v', 'new_v_ffn1_w_down': 'new_v', 'new_v_mix_norm': 'new_v', 'new_v_w_in': 'new_v', 'new_v_conv_w': 'new_v', 'new_v_conv_b': 'new_v', 'new_v_dt_bias': 'new_v', 'new_v_a_log': 'new_v', 'new_v_d_skip': 'new_v', 'new_v_ssd_norm': 'new_v', 'new_v_sgu_ln_g': 'new_v', 'new_v_sgu_ln_b': 'new_v', 'new_v_sgu_w': 'new_v', 'new_v_sgu_b': 'new_v', 'new_v_w_out': 'new_v', 'new_v_ffn2_norm': 'new_v', 'new_v_ffn2_w_gate': 'new_v', 'new_v_ffn2_w_up': 'new_v', 'new_v_ffn2_w_down': 'new_v', 'new_v_final_norm': 'new_v'}


def _forward(args):
    return _fwd_reference(*[args[k] for k in FWD_PARAMS])


def _output_shape():
    out = _jax.eval_shape(lambda: _forward(_fwd_setup_inputs(0)))
    return out.shape, out.dtype

N_MICROBATCH = 1
ADAM_LR = 0.001
ADAM_B1 = 0.9
ADAM_B2 = 0.999
ADAM_EPS = 1e-08
ADAM_WD = 0.01
ADAM_STEP = 10
PER_EXAMPLE_BATCH_AXIS = {'x': 0, 'loss_target': 0}
SHARED_INPUTS = []
_WEIGHT_DTYPES = {'ffn1_norm': _jnp.float32, 'ffn1_w_gate': _jnp.float32, 'ffn1_w_up': _jnp.float32, 'ffn1_w_down': _jnp.float32, 'mix_norm': _jnp.float32, 'w_in': _jnp.float32, 'conv_w': _jnp.float32, 'conv_b': _jnp.float32, 'dt_bias': _jnp.float32, 'a_log': _jnp.float32, 'd_skip': _jnp.float32, 'ssd_norm': _jnp.float32, 'sgu_ln_g': _jnp.float32, 'sgu_ln_b': _jnp.float32, 'sgu_w': _jnp.float32, 'sgu_b': _jnp.float32, 'w_out': _jnp.float32, 'ffn2_norm': _jnp.float32, 'ffn2_w_gate': _jnp.float32, 'ffn2_w_up': _jnp.float32, 'ffn2_w_down': _jnp.float32, 'final_norm': _jnp.float32}
MOMENT_SCALE = {'ffn1_norm': 8.673170e-02, 'ffn1_w_gate': 3.689775e-02, 'ffn1_w_up': 3.573422e-02, 'ffn1_w_down': 5.928289e-02, 'mix_norm': 1.515401e-01, 'w_in': 9.127920e-02, 'conv_w': 1.011174e-01, 'conv_b': 1.335760e-01, 'dt_bias': 2.706947e-01, 'a_log': 2.281585e-01, 'd_skip': 4.347749e-01, 'ssd_norm': 1.427856e-01, 'sgu_ln_g': 7.379029e-02, 'sgu_ln_b': 6.931040e-02, 'sgu_w': 4.625175e-02, 'sgu_b': 7.134524e-02, 'w_out': 1.076903e-01, 'ffn2_norm': 6.078898e-02, 'ffn2_w_gate': 2.552132e-02, 'ffn2_w_up': 2.480071e-02, 'ffn2_w_down': 4.111765e-02, 'final_norm': 3.196185e+01}


def _to_microbatches(a, axis):
    t = _jnp.moveaxis(a, axis, 0)
    t = t.reshape((N_MICROBATCH, t.shape[0] // N_MICROBATCH) + t.shape[1:])
    return _jnp.moveaxis(t, 1, axis + 1)


def setup_inputs(seed: int = 0) -> dict:
    inp = _fwd_setup_inputs(seed)
    key = _jax.random.fold_in(_jax.random.key(seed), 7919)
    shape, _ = _output_shape()
    out = dict(inp)
    out["loss_target"] = _jax.random.normal(_jax.random.fold_in(key, 0), shape, _jnp.float32)
    for i, name in enumerate(TWIN_WEIGHTS):
        w = inp[name].astype(_jnp.float32)
        if MOMENT_SCALE is None:
            s = _jnp.sqrt(_jnp.mean(_jnp.square(w)) + 1e-30)
        else:
            s = MOMENT_SCALE[name]
        km, kv = _jax.random.split(_jax.random.fold_in(key, i + 1))
        out[name] = w
        out["m_" + name] = s * _jax.random.normal(km, w.shape, _jnp.float32)
        out["v_" + name] = (s * s) * _jax.random.uniform(kv, w.shape, _jnp.float32, 0.5, 1.5)
    if N_MICROBATCH > 1:
        for name, axis in PER_EXAMPLE_BATCH_AXIS.items():
            out[name] = _to_microbatches(out[name], axis)
    return {'x': out['x'], 'ffn1_norm': out['ffn1_norm'], 'ffn1_w_gate': out['ffn1_w_gate'], 'ffn1_w_up': out['ffn1_w_up'], 'ffn1_w_down': out['ffn1_w_down'], 'mix_norm': out['mix_norm'], 'w_in': out['w_in'], 'conv_w': out['conv_w'], 'conv_b': out['conv_b'], 'dt_bias': out['dt_bias'], 'a_log': out['a_log'], 'd_skip': out['d_skip'], 'ssd_norm': out['ssd_norm'], 'sgu_ln_g': out['sgu_ln_g'], 'sgu_ln_b': out['sgu_ln_b'], 'sgu_w': out['sgu_w'], 'sgu_b': out['sgu_b'], 'w_out': out['w_out'], 'ffn2_norm': out['ffn2_norm'], 'ffn2_w_gate': out['ffn2_w_gate'], 'ffn2_w_up': out['ffn2_w_up'], 'ffn2_w_down': out['ffn2_w_down'], 'final_norm': out['final_norm'], 'loss_target': out['loss_target'], 'm_ffn1_norm': out['m_ffn1_norm'], 'm_ffn1_w_gate': out['m_ffn1_w_gate'], 'm_ffn1_w_up': out['m_ffn1_w_up'], 'm_ffn1_w_down': out['m_ffn1_w_down'], 'm_mix_norm': out['m_mix_norm'], 'm_w_in': out['m_w_in'], 'm_conv_w': out['m_conv_w'], 'm_conv_b': out['m_conv_b'], 'm_dt_bias': out['m_dt_bias'], 'm_a_log': out['m_a_log'], 'm_d_skip': out['m_d_skip'], 'm_ssd_norm': out['m_ssd_norm'], 'm_sgu_ln_g': out['m_sgu_ln_g'], 'm_sgu_ln_b': out['m_sgu_ln_b'], 'm_sgu_w': out['m_sgu_w'], 'm_sgu_b': out['m_sgu_b'], 'm_w_out': out['m_w_out'], 'm_ffn2_norm': out['m_ffn2_norm'], 'm_ffn2_w_gate': out['m_ffn2_w_gate'], 'm_ffn2_w_up': out['m_ffn2_w_up'], 'm_ffn2_w_down': out['m_ffn2_w_down'], 'm_final_norm': out['m_final_norm'], 'v_ffn1_norm': out['v_ffn1_norm'], 'v_ffn1_w_gate': out['v_ffn1_w_gate'], 'v_ffn1_w_up': out['v_ffn1_w_up'], 'v_ffn1_w_down': out['v_ffn1_w_down'], 'v_mix_norm': out['v_mix_norm'], 'v_w_in': out['v_w_in'], 'v_conv_w': out['v_conv_w'], 'v_conv_b': out['v_conv_b'], 'v_dt_bias': out['v_dt_bias'], 'v_a_log': out['v_a_log'], 'v_d_skip': out['v_d_skip'], 'v_ssd_norm': out['v_ssd_norm'], 'v_sgu_ln_g': out['v_sgu_ln_g'], 'v_sgu_ln_b': out['v_sgu_ln_b'], 'v_sgu_w': out['v_sgu_w'], 'v_sgu_b': out['v_sgu_b'], 'v_w_out': out['v_w_out'], 'v_ffn2_norm': out['v_ffn2_norm'], 'v_ffn2_w_gate': out['v_ffn2_w_gate'], 'v_ffn2_w_up': out['v_ffn2_w_up'], 'v_ffn2_w_down': out['v_ffn2_w_down'], 'v_final_norm': out['v_final_norm']}


def _loss(weights, diff, rest, loss_target):
    with _jax.named_scope("forward"):
        args = {**rest, TWIN_DIFF_INPUT: diff, **{k: w.astype(_WEIGHT_DTYPES[k]) for k, w in weights.items()}}
        y = _forward(args)
    with _jax.named_scope("loss_head"):
        err = _jnp.square(y.astype(_jnp.float32) - loss_target)
        return 0.5 * _jnp.sum(_jnp.mean(err, axis=-1)) if err.ndim else 0.5 * err


def _adamw(w, g, m, v):
    m = ADAM_B1 * m + (1.0 - ADAM_B1) * g
    v = ADAM_B2 * v + (1.0 - ADAM_B2) * _jnp.square(g)
    m_hat = m / (1.0 - ADAM_B1 ** ADAM_STEP)
    v_hat = v / (1.0 - ADAM_B2 ** ADAM_STEP)
    delta = -ADAM_LR * (m_hat / (_jnp.sqrt(v_hat) + ADAM_EPS) + ADAM_WD * w)
    return delta, m, v


def reference(x, ffn1_norm, ffn1_w_gate, ffn1_w_up, ffn1_w_down, mix_norm, w_in, conv_w, conv_b, dt_bias, a_log, d_skip, ssd_norm, sgu_ln_g, sgu_ln_b, sgu_w, sgu_b, w_out, ffn2_norm, ffn2_w_gate, ffn2_w_up, ffn2_w_down, final_norm, loss_target, m_ffn1_norm, m_ffn1_w_gate, m_ffn1_w_up, m_ffn1_w_down, m_mix_norm, m_w_in, m_conv_w, m_conv_b, m_dt_bias, m_a_log, m_d_skip, m_ssd_norm, m_sgu_ln_g, m_sgu_ln_b, m_sgu_w, m_sgu_b, m_w_out, m_ffn2_norm, m_ffn2_w_gate, m_ffn2_w_up, m_ffn2_w_down, m_final_norm, v_ffn1_norm, v_ffn1_w_gate, v_ffn1_w_up, v_ffn1_w_down, v_mix_norm, v_w_in, v_conv_w, v_conv_b, v_dt_bias, v_a_log, v_d_skip, v_ssd_norm, v_sgu_ln_g, v_sgu_ln_b, v_sgu_w, v_sgu_b, v_w_out, v_ffn2_norm, v_ffn2_w_gate, v_ffn2_w_up, v_ffn2_w_down, v_final_norm):
    given = dict(x=x, ffn1_norm=ffn1_norm, ffn1_w_gate=ffn1_w_gate, ffn1_w_up=ffn1_w_up, ffn1_w_down=ffn1_w_down, mix_norm=mix_norm, w_in=w_in, conv_w=conv_w, conv_b=conv_b, dt_bias=dt_bias, a_log=a_log, d_skip=d_skip, ssd_norm=ssd_norm, sgu_ln_g=sgu_ln_g, sgu_ln_b=sgu_ln_b, sgu_w=sgu_w, sgu_b=sgu_b, w_out=w_out, ffn2_norm=ffn2_norm, ffn2_w_gate=ffn2_w_gate, ffn2_w_up=ffn2_w_up, ffn2_w_down=ffn2_w_down, final_norm=final_norm, loss_target=loss_target, m_ffn1_norm=m_ffn1_norm, m_ffn1_w_gate=m_ffn1_w_gate, m_ffn1_w_up=m_ffn1_w_up, m_ffn1_w_down=m_ffn1_w_down, m_mix_norm=m_mix_norm, m_w_in=m_w_in, m_conv_w=m_conv_w, m_conv_b=m_conv_b, m_dt_bias=m_dt_bias, m_a_log=m_a_log, m_d_skip=m_d_skip, m_ssd_norm=m_ssd_norm, m_sgu_ln_g=m_sgu_ln_g, m_sgu_ln_b=m_sgu_ln_b, m_sgu_w=m_sgu_w, m_sgu_b=m_sgu_b, m_w_out=m_w_out, m_ffn2_norm=m_ffn2_norm, m_ffn2_w_gate=m_ffn2_w_gate, m_ffn2_w_up=m_ffn2_w_up, m_ffn2_w_down=m_ffn2_w_down, m_final_norm=m_final_norm, v_ffn1_norm=v_ffn1_norm, v_ffn1_w_gate=v_ffn1_w_gate, v_ffn1_w_up=v_ffn1_w_up, v_ffn1_w_down=v_ffn1_w_down, v_mix_norm=v_mix_norm, v_w_in=v_w_in, v_conv_w=v_conv_w, v_conv_b=v_conv_b, v_dt_bias=v_dt_bias, v_a_log=v_a_log, v_d_skip=v_d_skip, v_ssd_norm=v_ssd_norm, v_sgu_ln_g=v_sgu_ln_g, v_sgu_ln_b=v_sgu_ln_b, v_sgu_w=v_sgu_w, v_sgu_b=v_sgu_b, v_w_out=v_w_out, v_ffn2_norm=v_ffn2_norm, v_ffn2_w_gate=v_ffn2_w_gate, v_ffn2_w_up=v_ffn2_w_up, v_ffn2_w_down=v_ffn2_w_down, v_final_norm=v_final_norm)
    weights = {n: given[n] for n in TWIN_WEIGHTS}
    shared = {n: given[n] for n in SHARED_INPUTS}
    per_example = {n: given[n] for n in ['x']}
    grad_fn = _jax.value_and_grad(_loss, argnums=(0, 1))

    def one_microbatch(ex, loss_target):
        ex = dict(ex)
        diff = ex.pop(TWIN_DIFF_INPUT)
        return grad_fn(weights, diff, {**shared, **ex}, loss_target)

    if N_MICROBATCH == 1:
        loss, (grad_w, grad_x) = one_microbatch(per_example, given["loss_target"])
    else:
        def body(carry, xs):
            loss_sum, grad_sum = carry
            l_k, (gw_k, gx_k) = one_microbatch(xs[0], xs[1])
            with _jax.named_scope("update"):
                return (loss_sum + l_k, _jax.tree.map(_jnp.add, grad_sum, gw_k)), gx_k

        init = (_jnp.zeros((), _jnp.float32), _jax.tree.map(_jnp.zeros_like, weights))
        (loss, grad_w), grad_x = _jax.lax.scan(body, init, (per_example, given["loss_target"]))
    with _jax.named_scope("update"):
        delta_w, new_m, new_v = {}, {}, {}
        for n in TWIN_WEIGHTS:
            delta_w[n], new_m[n], new_v[n] = _adamw(weights[n], grad_w[n], given["m_" + n], given["v_" + n])
    return (loss, grad_x, *[grad_w[n] for n in TWIN_WEIGHTS], *[delta_w[n] for n in TWIN_WEIGHTS],
            *[new_m[n] for n in TWIN_WEIGHTS], *[new_v[n] for n in TWIN_WEIGHTS])
```

```python
import functools

import numpy as np
import jax
import jax.numpy as jnp
from jax import lax
from jax.experimental import pallas as pl
from jax.experimental.pallas import tpu as pltpu

F32, BF16 = jnp.float32, jnp.bfloat16
HI = lax.Precision.HIGHEST
MESH = pl.DeviceIdType.MESH
N_DEV = 8
VMEM_LIMIT_BYTES = 56 * 1024 * 1024
LANE, SUBLANE = 128, 8

HEAD_DIM = 64
ATT_W = 384
SSD_W = 384
SSD_HEADS = 6
SSD_STATE = 128
SSD_CONV = 4
CHUNK = 128
SSD_CONV_DIM = 896
SGU_W = 256
SGU_GROUPS = 4
D_IN = 2950
RMS_EPS = 1e-6
LN_EPS = 1e-5
NEG = -1e30

PW = 3072
Q0, K0, V0, Z0, U0, XBC0, DT0, VS0 = 0, 384, 768, 1152, 1536, 1792, 2688, 2816

ADAM_LR, ADAM_B1, ADAM_B2, ADAM_EPS, ADAM_WD, ADAM_STEP = 0.001, 0.9, 0.999, 1e-08, 0.01, 10


def _cp(sem=None):
    return pltpu.CompilerParams(dimension_semantics=sem, vmem_limit_bytes=VMEM_LIMIT_BYTES)


def _tile(n, cap, mult=LANE):
    best = None
    t = mult
    while t <= min(n, cap):
        if n % t == 0:
            best = t
        t += mult
    return best if best is not None else n


def _dot(a, b, prec=None):
    return jnp.dot(a, b, preferred_element_type=F32, precision=prec)


def _dot_nt(a, b, prec=None):
    return lax.dot_general(a, b, (((1,), (1,)), ((), ())), preferred_element_type=F32, precision=prec)


def _dot_tn(a, b, prec=None):
    return lax.dot_general(a, b, (((0,), (0,)), ((), ())), preferred_element_type=F32, precision=prec)


def _sigmoid(x):
    return 1.0 / (1.0 + jnp.exp(-x))


def _silu(x):
    return x * _sigmoid(x)


def _gelu(x):
    return 0.5 * x * (1.0 + lax.erf(x * 0.7071067811865476))


def _softplus(x):
    return jnp.maximum(x, 0.0) + jnp.log(1.0 + jnp.exp(-jnp.abs(x)))


def _rms_fwd(x, g):
    rstd = lax.rsqrt(jnp.mean(x * x, axis=-1, keepdims=True) + RMS_EPS)
    xhat = x * rstd
    return xhat * g, xhat, rstd


def _rms_bwd(dy, xhat, rstd, g):
    dxhat = dy * g
    dx = rstd * (dxhat - xhat * jnp.mean(dxhat * xhat, axis=-1, keepdims=True))
    return dx, dy * xhat


def _mm(a, b, mode, name, out_dtype=F32, residual=None, tm_cap=512, tn_cap=512, tk_cap=1024):
    if mode == "nn":
        (M, K), (_, N) = a.shape, b.shape
    elif mode == "nt":
        (M, K), (N, _) = a.shape, b.shape
    else:
        (K, M), (_, N) = a.shape, b.shape
    tm, tn, tk = _tile(M, tm_cap), _tile(N, tn_cap), _tile(K, tk_cap)
    nk = K // tk
    if mode == "tn":
        a_spec = pl.BlockSpec((tk, tm), lambda i, j, k: (k, i))
    else:
        a_spec = pl.BlockSpec((tm, tk), lambda i, j, k: (i, k))
    if mode == "nt":
        b_spec = pl.BlockSpec((tn, tk), lambda i, j, k: (j, k))
    else:
        b_spec = pl.BlockSpec((tk, tn), lambda i, j, k: (k, j))
    o_spec = pl.BlockSpec((tm, tn), lambda i, j, k: (i, j))
    has_res = residual is not None

    def body(*refs):
        if has_res:
            a_ref, b_ref, r_ref, o_ref, acc = refs
        else:
            a_ref, b_ref, o_ref, acc = refs
        k = pl.program_id(2)

        @pl.when(k == 0)
        def _():
            acc[...] = jnp.zeros_like(acc)

        av = a_ref[...].astype(BF16)
        bv = b_ref[...].astype(BF16)
        if mode == "nn":
            acc[...] += _dot(av, bv)
        elif mode == "nt":
            acc[...] += _dot_nt(av, bv)
        else:
            acc[...] += _dot_tn(av, bv)

        @pl.when(k == nk - 1)
        def _():
            o = acc[...]
            if has_res:
                o = r_ref[...] + o
            o_ref[...] = o.astype(out_dtype)

    ins = [a, b] + ([residual] if has_res else [])
    in_specs = [a_spec, b_spec] + ([o_spec] if has_res else [])
    return pl.pallas_call(
        body, name=name, grid=(M // tm, N // tn, nk),
        in_specs=in_specs, out_specs=o_spec,
        out_shape=jax.ShapeDtypeStruct((M, N), out_dtype),
        scratch_shapes=[pltpu.VMEM((tm, tn), F32)],
        compiler_params=_cp(("parallel", "parallel", "arbitrary")),
    )(*ins)


def _ffn_fwd(x, g, wg, wu, wd, name):
    T, D = x.shape
    F = wg.shape[1]
    tm, th = _tile(T, 512), _tile(F, 256)
    nj = F // th

    def body(x_ref, g_ref, wg_ref, wu_ref, wd_ref, out_ref, gate_ref, up_ref, xn_s, acc_s):
        j = pl.program_id(1)

        @pl.when(j == 0)
        def _():
            xn, _, _ = _rms_fwd(x_ref[...], g_ref[...])
            xn_s[...] = xn.astype(BF16)
            acc_s[...] = jnp.zeros_like(acc_s)

        xn = xn_s[...]
        gate = _dot(xn, wg_ref[...])
        up = _dot(xn, wu_ref[...])
        gate_ref[...] = gate.astype(BF16)
        up_ref[...] = up.astype(BF16)
        act = _silu(gate) * up
        acc_s[...] += _dot(act.astype(BF16), wd_ref[...])

        @pl.when(j == nj - 1)
        def _():
            out_ref[...] = x_ref[...] + 0.5 * acc_s[...]

    return pl.pallas_call(
        body, name=name, grid=(T // tm, nj),
        in_specs=[pl.BlockSpec((tm, D), lambda i, j: (i, 0)),
                  pl.BlockSpec((1, D), lambda i, j: (0, 0)),
                  pl.BlockSpec((D, th), lambda i, j: (0, j)),
                  pl.BlockSpec((D, th), lambda i, j: (0, j)),
                  pl.BlockSpec((th, D), lambda i, j: (j, 0))],
        out_specs=[pl.BlockSpec((tm, D), lambda i, j: (i, 0)),
                   pl.BlockSpec((tm, th), lambda i, j: (i, j)),
                   pl.BlockSpec((tm, th), lambda i, j: (i, j))],
        out_shape=[jax.ShapeDtypeStruct((T, D), F32),
                   jax.ShapeDtypeStruct((T, F), BF16),
                   jax.ShapeDtypeStruct((T, F), BF16)],
        scratch_shapes=[pltpu.VMEM((tm, D), BF16), pltpu.VMEM((tm, D), F32)],
        compiler_params=_cp(("parallel", "arbitrary")),
    )(x, g, wg, wu, wd)


def _ffn_bwd_dx(dout, x, g, gate, up, wg, wu, wd, name):
    T, D = x.shape
    F = wg.shape[1]
    tm, th = _tile(T, 512), _tile(F, 256)
    nj = F // th

    def body(dout_ref, x_ref, g_ref, gate_ref, up_ref, wg_ref, wu_ref, wd_ref,
             dx_ref, dgate_ref, dup_ref, act_ref, xn_ref, dacc_ref, dg_ref, dxn_s):
        i, j = pl.program_id(0), pl.program_id(1)

        @pl.when(j == 0)
        def _():
            xn, _, _ = _rms_fwd(x_ref[...], g_ref[...])
            xn_ref[...] = xn.astype(BF16)
            dacc_ref[...] = (0.5 * dout_ref[...]).astype(BF16)
            dxn_s[...] = jnp.zeros_like(dxn_s)

        @pl.when((i == 0) & (j == 0))
        def _():
            dg_ref[...] = jnp.zeros_like(dg_ref)

        dact = _dot_nt(dacc_ref[...], wd_ref[...])
        gt = gate_ref[...].astype(F32)
        u = up_ref[...].astype(F32)
        sig = _sigmoid(gt)
        sl = gt * sig
        dgate = (dact * u * (sig * (1.0 + gt * (1.0 - sig)))).astype(BF16)
        dup = (dact * sl).astype(BF16)
        dgate_ref[...] = dgate
        dup_ref[...] = dup
        act_ref[...] = (sl * u).astype(BF16)
        dxn_s[...] += _dot_nt(dgate, wg_ref[...]) + _dot_nt(dup, wu_ref[...])

        @pl.when(j == nj - 1)
        def _():
            gv = g_ref[...]
            _, xhat, rstd = _rms_fwd(x_ref[...], gv)
            dx, dgrow = _rms_bwd(dxn_s[...], xhat, rstd, gv)
            dx_ref[...] = dout_ref[...] + dx
            dg_ref[...] += jnp.sum(dgrow, axis=0, keepdims=True)

    row = lambda i, j: (i, 0)
    tile = lambda i, j: (i, j)
    return pl.pallas_call(
        body, name=name, grid=(T // tm, nj),
        in_specs=[pl.BlockSpec((tm, D), row), pl.BlockSpec((tm, D), row),
                  pl.BlockSpec((1, D), lambda i, j: (0, 0)),
                  pl.BlockSpec((tm, th), tile), pl.BlockSpec((tm, th), tile),
                  pl.BlockSpec((D, th), lambda i, j: (0, j)),
                  pl.BlockSpec((D, th), lambda i, j: (0, j)),
                  pl.BlockSpec((th, D), lambda i, j: (j, 0))],
        out_specs=[pl.BlockSpec((tm, D), row),
                   pl.BlockSpec((tm, th), tile), pl.BlockSpec((tm, th), tile), pl.BlockSpec((tm, th), tile),
                   pl.BlockSpec((tm, D), row), pl.BlockSpec((tm, D), row),
                   pl.BlockSpec((1, D), lambda i, j: (0, 0))],
        out_shape=[jax.ShapeDtypeStruct((T, D), F32),
                   jax.ShapeDtypeStruct((T, F), BF16), jax.ShapeDtypeStruct((T, F), BF16),
                   jax.ShapeDtypeStruct((T, F), BF16),
                   jax.ShapeDtypeStruct((T, D), BF16), jax.ShapeDtypeStruct((T, D), BF16),
                   jax.ShapeDtypeStruct((1, D), F32)],
        scratch_shapes=[pltpu.VMEM((tm, D), F32)],
        compiler_params=_cp(("arbitrary", "arbitrary")),
    )(dout, x, g, gate, up, wg, wu, wd)


def _norm_mm(x, g, w, name):
    T, D = x.shape
    N = w.shape[1]
    tm, tn = _tile(T, 512), _tile(N, 512)

    def body(x_ref, g_ref, w_ref, o_ref, h_ref):
        @pl.when(pl.program_id(1) == 0)
        def _():
            xn, _, _ = _rms_fwd(x_ref[...], g_ref[...])
            h_ref[...] = xn.astype(BF16)

        o_ref[...] = _dot(h_ref[...], w_ref[...])

    return pl.pallas_call(
        body, name=name, grid=(T // tm, N // tn),
        in_specs=[pl.BlockSpec((tm, D), lambda i, j: (i, 0)),
                  pl.BlockSpec((1, D), lambda i, j: (0, 0)),
                  pl.BlockSpec((D, tn), lambda i, j: (0, j))],
        out_specs=[pl.BlockSpec((tm, tn), lambda i, j: (i, j)),
                   pl.BlockSpec((tm, D), lambda i, j: (i, 0))],
        out_shape=[jax.ShapeDtypeStruct((T, N), F32), jax.ShapeDtypeStruct((T, D), BF16)],
        compiler_params=_cp(("parallel", "arbitrary")),
    )(x, g, w)


def _rms_bwd_call(dh, x, g, dres, name):
    T, D = x.shape
    tm = _tile(T, 512)

    def body(dh_ref, x_ref, g_ref, dres_ref, dx_ref, dg_ref):
        @pl.when(pl.program_id(0) == 0)
        def _():
            dg_ref[...] = jnp.zeros_like(dg_ref)

        gv = g_ref[...]
        _, xhat, rstd = _rms_fwd(x_ref[...], gv)
        dx, dgrow = _rms_bwd(dh_ref[...], xhat, rstd, gv)
        dx_ref[...] = dres_ref[...] + dx
        dg_ref[...] += jnp.sum(dgrow, axis=0, keepdims=True)

    row = pl.BlockSpec((tm, D), lambda i: (i, 0))
    one = pl.BlockSpec((1, D), lambda i: (0, 0))
    return pl.pallas_call(
        body, name=name, grid=(T // tm,),
        in_specs=[row, row, one, row], out_specs=[row, one],
        out_shape=[jax.ShapeDtypeStruct((T, D), F32), jax.ShapeDtypeStruct((1, D), F32)],
        compiler_params=_cp(("arbitrary",)),
    )(dh, x, g, dres)


def _final_loss(x, g, target, name):
    T, D = x.shape
    tm = _tile(T, 512)

    def body(x_ref, g_ref, t_ref, loss_ref, dx_ref, dg_ref):
        @pl.when(pl.program_id(0) == 0)
        def _():
            dg_ref[...] = jnp.zeros_like(dg_ref)
            loss_ref[...] = jnp.zeros_like(loss_ref)

        gv = g_ref[...]
        y, xhat, rstd = _rms_fwd(x_ref[...], gv)
        err = y - t_ref[...]
        part = 0.5 * jnp.sum(jnp.mean(err * err, axis=-1, keepdims=True), axis=0, keepdims=True)
        loss_ref[...] += jnp.broadcast_to(part, loss_ref.shape)
        dy = err * (1.0 / D)
        dx, dgrow = _rms_bwd(dy, xhat, rstd, gv)
        dx_ref[...] = dx
        dg_ref[...] += jnp.sum(dgrow, axis=0, keepdims=True)

    row = pl.BlockSpec((tm, D), lambda i: (i, 0))
    one = pl.BlockSpec((1, D), lambda i: (0, 0))
    return pl.pallas_call(
        body, name=name, grid=(T // tm,),
        in_specs=[row, one, row],
        out_specs=[pl.BlockSpec((1, LANE), lambda i: (0, 0)), row, one],
        out_shape=[jax.ShapeDtypeStruct((1, LANE), F32), jax.ShapeDtypeStruct((T, D), F32),
                   jax.ShapeDtypeStruct((1, D), F32)],
        compiler_params=_cp(("arbitrary",)),
    )(x, g, target)


def _attn_bias(S, bq):
    nb = S // bq
    d = (jnp.arange(nb)[:, None, None] * bq + jnp.arange(bq)[None, :, None]
         - jnp.arange(bq)[None, None, :])
    ok = d >= 0
    mult = ((ok & (d <= 128)).astype(F32) + (ok & (d % 4 == 0) & (d <= 512)).astype(F32)
            + (ok & (d % 16 == 0) & (d <= 2048)).astype(F32))
    return jnp.where(mult > 0, jnp.log(jnp.maximum(mult, 1.0)), NEG).astype(F32)


def _attn_fwd(proj, bias, B, S, name):
    T = B * S
    nb, bq, _ = bias.shape
    qcol, kcol, vcol = Q0 // LANE, K0 // LANE, V0 // LANE

    def body(q_ref, k_ref, v_ref, bias_ref, o_ref, lse_ref, qs, ks, vs):
        for hh in range(2):
            sl = slice(HEAD_DIM * hh, HEAD_DIM * (hh + 1))
            qs[hh] = (q_ref[:, sl] * 0.125).astype(BF16)
            ks[hh] = k_ref[:, sl].astype(BF16)
            vs[hh] = v_ref[:, sl].astype(BF16)
        for hh in range(2):
            sl = slice(HEAD_DIM * hh, HEAD_DIM * (hh + 1))

            def qloop(qb, carry):
                q0 = pl.multiple_of(qb * bq, bq)
                q = qs[hh, pl.ds(q0, bq), :]

                def kloop(kb, c):
                    m, l, acc = c
                    k0 = pl.multiple_of(kb * bq, bq)
                    s = _dot_nt(q, ks[hh, pl.ds(k0, bq), :]) + bias_ref[qb - kb]
                    mn = jnp.maximum(m, jnp.max(s, axis=-1, keepdims=True))
                    p = jnp.exp(s - mn)
                    a = jnp.exp(m - mn)
                    l = a * l + jnp.sum(p, axis=-1, keepdims=True)
                    acc = a * acc + _dot(p.astype(BF16), vs[hh, pl.ds(k0, bq), :])
                    return mn, l, acc

                init = (jnp.full((bq, 1), NEG, F32), jnp.zeros((bq, 1), F32), jnp.zeros((bq, HEAD_DIM), F32))
                m, l, acc = lax.fori_loop(0, qb + 1, kloop, init)
                o_ref[pl.ds(q0, bq), sl] = acc / l
                lse_ref[pl.ds(q0, bq), hh:hh + 1] = m + jnp.log(l)
                return carry

            lax.fori_loop(0, nb, qloop, 0)

    blk = lambda c0: pl.BlockSpec((S, LANE), lambda b, p: (b, c0 + p))
    return pl.pallas_call(
        body, name=name, grid=(B, ATT_W // LANE),
        in_specs=[blk(qcol), blk(kcol), blk(vcol),
                  pl.BlockSpec((nb, bq, bq), lambda b, p: (0, 0, 0))],
        out_specs=[pl.BlockSpec((S, LANE), lambda b, p: (b, p)),
                   pl.BlockSpec((None, None, S, 2), lambda b, p: (b, p, 0, 0))],
        out_shape=[jax.ShapeDtypeStruct((T, ATT_W), F32),
                   jax.ShapeDtypeStruct((B, ATT_W // LANE, S, 2), F32)],
        scratch_shapes=[pltpu.VMEM((2, S, HEAD_DIM), BF16)] * 3,
        compiler_params=_cp(("parallel", "parallel")),
    )(proj, proj, proj, bias)


def _attn_bwd(proj, o, lse, dy, bias, B, S, name):
    T = B * S
    nb, bq, _ = bias.shape
    qcol, kcol, vcol = Q0 // LANE, K0 // LANE, V0 // LANE

    def body(q_ref, k_ref, v_ref, o_ref, lse_ref, do_ref, bias_ref, dq_ref, dk_ref, dv_ref,
             qs, ks, vs, dos, dl, dqs):
        for hh in range(2):
            sl = slice(HEAD_DIM * hh, HEAD_DIM * (hh + 1))
            qs[hh] = (q_ref[:, sl] * 0.125).astype(BF16)
            ks[hh] = k_ref[:, sl].astype(BF16)
            vs[hh] = v_ref[:, sl].astype(BF16)
            do = do_ref[:, sl]
            dos[hh] = do.astype(BF16)
            dl[:, hh:hh + 1] = jnp.sum(do * o_ref[:, sl], axis=-1, keepdims=True)
        dqs[...] = jnp.zeros_like(dqs)
        for hh in range(2):
            sl = slice(HEAD_DIM * hh, HEAD_DIM * (hh + 1))

            def kloop(kb, carry):
                k0 = pl.multiple_of(kb * bq, bq)
                k = ks[hh, pl.ds(k0, bq), :]
                v = vs[hh, pl.ds(k0, bq), :]

                def qloop(qb, c):
                    dk, dv = c
                    q0 = pl.multiple_of(qb * bq, bq)
                    q = qs[hh, pl.ds(q0, bq), :]
                    do = dos[hh, pl.ds(q0, bq), :]
                    s = _dot_nt(q, k) + bias_ref[qb - kb]
                    p = jnp.exp(s - lse_ref[pl.ds(q0, bq), hh:hh + 1])
                    dv = dv + _dot_tn(p.astype(BF16), do)
                    dp = _dot_nt(do, v)
                    ds = (p * (dp - dl[pl.ds(q0, bq), hh:hh + 1])).astype(BF16)
                    dk = dk + _dot_tn(ds, q)
                    dqs[hh, pl.ds(q0, bq), :] += _dot(ds, k)
                    return dk, dv

                z = jnp.zeros((bq, HEAD_DIM), F32)
                dk, dv = lax.fori_loop(kb, nb, qloop, (z, z))
                dk_ref[pl.ds(k0, bq), sl] = dk
                dv_ref[pl.ds(k0, bq), sl] = dv
                return carry

            lax.fori_loop(0, nb, kloop, 0)
            dq_ref[:, sl] = dqs[hh] * 0.125

    blk = lambda c0: pl.BlockSpec((S, LANE), lambda b, p: (b, c0 + p))
    own = pl.BlockSpec((S, LANE), lambda b, p: (b, p))
    return pl.pallas_call(
        body, name=name, grid=(B, ATT_W // LANE),
        in_specs=[blk(qcol), blk(kcol), blk(vcol), own,
                  pl.BlockSpec((None, None, S, 2), lambda b, p: (b, p, 0, 0)), own,
                  pl.BlockSpec((nb, bq, bq), lambda b, p: (0, 0, 0))],
        out_specs=[own, own, own],
        out_shape=[jax.ShapeDtypeStruct((T, ATT_W), F32)] * 3,
        scratch_shapes=[pltpu.VMEM((2, S, HEAD_DIM), BF16)] * 4
        + [pltpu.VMEM((S, 2), F32), pltpu.VMEM((2, S, HEAD_DIM), F32)],
        compiler_params=_cp(("parallel", "parallel")),
    )(proj, proj, proj, o, lse, dy, bias)


def _conv_fwd(proj, cw, cb, B, S, name):
    T = B * S
    nc = SSD_CONV_DIM // LANE
    c0 = XBC0 // LANE

    def body(x_ref, w_ref, b_ref, o_ref):
        x = x_ref[...]
        t = lax.broadcasted_iota(jnp.int32, (S, 1), 0)
        acc = b_ref[...] + w_ref[SSD_CONV - 1:SSD_CONV, :] * x
        for k in range(SSD_CONV - 1):
            sh = SSD_CONV - 1 - k
            xs = jnp.where(t >= sh, pltpu.roll(x, sh, 0), 0.0)
            acc = acc + w_ref[k:k + 1, :] * xs
        o_ref[...] = acc

    return pl.pallas_call(
        body, name=name, grid=(B, nc),
        in_specs=[pl.BlockSpec((S, LANE), lambda b, j: (b, c0 + j)),
                  pl.BlockSpec((SUBLANE, LANE), lambda b, j: (0, j)),
                  pl.BlockSpec((1, LANE), lambda b, j: (0, j))],
        out_specs=pl.BlockSpec((S, LANE), lambda b, j: (b, j)),
        out_shape=jax.ShapeDtypeStruct((T, SSD_CONV_DIM), F32),
        compiler_params=_cp(("parallel", "parallel")),
    )(proj, cw, cb)


def _conv_bwd(dpre, proj, cw, B, S, name):
    T = B * S
    nc = SSD_CONV_DIM // LANE
    c0 = XBC0 // LANE

    def body(d_ref, x_ref, w_ref, dx_ref, dwb_ref):
        @pl.when(pl.program_id(1) == 0)
        def _():
            dwb_ref[...] = jnp.zeros_like(dwb_ref)

        d = d_ref[...]
        x = x_ref[...]
        t = lax.broadcasted_iota(jnp.int32, (S, 1), 0)
        dx = w_ref[SSD_CONV - 1:SSD_CONV, :] * d
        rows = [None] * SUBLANE
        rows[SSD_CONV - 1] = jnp.sum(d * x, axis=0, keepdims=True)
        for k in range(SSD_CONV - 1):
            sh = SSD_CONV - 1 - k
            dx = dx + w_ref[k:k + 1, :] * jnp.where(t < S - sh, pltpu.roll(d, S - sh, 0), 0.0)
            xs = jnp.where(t >= sh, pltpu.roll(x, sh, 0), 0.0)
            rows[k] = jnp.sum(d * xs, axis=0, keepdims=True)
        rows[SSD_CONV] = jnp.sum(d, axis=0, keepdims=True)
        dx_ref[...] = dx
        r = lax.broadcasted_iota(jnp.int32, (SUBLANE, LANE), 0)
        upd = jnp.zeros((SUBLANE, LANE), F32)
        for k in range(SSD_CONV + 1):
            upd = upd + jnp.where(r == k, rows[k], 0.0)
        dwb_ref[...] += upd

    return pl.pallas_call(
        body, name=name, grid=(nc, B),
        in_specs=[pl.BlockSpec((S, LANE), lambda j, b: (b, j)),
                  pl.BlockSpec((S, LANE), lambda j, b: (b, c0 + j)),
                  pl.BlockSpec((SUBLANE, LANE), lambda j, b: (0, j))],
        out_specs=[pl.BlockSpec((S, LANE), lambda j, b: (b, j)),
                   pl.BlockSpec((SUBLANE, LANE), lambda j, b: (0, j))],
        out_shape=[jax.ShapeDtypeStruct((T, SSD_CONV_DIM), F32),
                   jax.ShapeDtypeStruct((SUBLANE, SSD_CONV_DIM), F32)],
        compiler_params=_cp(("parallel", "arbitrary")),
    )(dpre, proj, cw)


def _ssd_consts():
    e = np.zeros((LANE, SSD_W), np.float32)
    p = np.zeros((SUBLANE, SSD_W), np.float32)
    for h in range(SSD_HEADS):
        e[h, HEAD_DIM * h:HEAD_DIM * (h + 1)] = 1.0
        p[h, HEAD_DIM * h] = 1.0
    return jnp.asarray(e), jnp.asarray(p)


def _ssd_chunk(pre, z, dtr, sprev, par, e_mat, psel):
    L = CHUNK
    xc = _silu(pre)
    xs, bm, cm = xc[:, :SSD_W], xc[:, SSD_W:SSD_W + 2 * SSD_STATE], xc[:, SSD_W + 2 * SSD_STATE:]
    dtb, alog, dskip, ng = par[0:1], par[1:2], par[2:3], par[3:4]
    dt = _softplus(_dot(dtr, e_mat, HI) + dtb)
    a = dt * (-jnp.exp(alog))
    X = xs * dt
    ri = lax.broadcasted_iota(jnp.int32, (L, L), 0)
    ci = lax.broadcasted_iota(jnp.int32, (L, L), 1)
    tril = ri >= ci
    acs = _dot(tril.astype(F32), a, HI)
    acs_t = _dot_nt(psel, acs, HI)
    ecs = jnp.exp(acs)
    alast = acs[L - 1:L, :]
    xd = (X * jnp.exp(alast - acs)).astype(BF16)
    xb = X.astype(BF16)
    col = lax.broadcasted_iota(jnp.int32, (1, SSD_W), 1)
    sb = sprev.astype(BF16)
    y = dskip * xs
    snew = sprev * jnp.exp(alast)
    for g in range(2):
        gmask = (col >= g * (SSD_W // 2)) & (col < (g + 1) * (SSD_W // 2))
        bg = bm[:, SSD_STATE * g:SSD_STATE * (g + 1)].astype(BF16)
        cg = cm[:, SSD_STATE * g:SSD_STATE * (g + 1)].astype(BF16)
        cb = _dot_nt(cg, bg)
        for j in range(3):
            h = 3 * g + j
            seg = acs[:, HEAD_DIM * h:HEAD_DIM * h + 1] - acs_t[h:h + 1, :]
            dec = jnp.exp(jnp.where(tril, seg, NEG))
            yh = _dot((cb * dec).astype(BF16), xb)
            hmask = (col >= HEAD_DIM * h) & (col < HEAD_DIM * (h + 1))
            y = y + jnp.where(hmask, yh, 0.0)
        y = y + jnp.where(gmask, _dot(cg, sb) * ecs, 0.0)
        snew = snew + jnp.where(gmask, _dot_tn(bg, xd), 0.0)
    yg = y * _silu(z)
    sq = yg * yg
    g0 = col < SSD_W // 2
    ms0 = jnp.sum(jnp.where(g0, sq, 0.0), axis=-1, keepdims=True) * (2.0 / SSD_W)
    ms1 = jnp.sum(jnp.where(g0, 0.0, sq), axis=-1, keepdims=True) * (2.0 / SSD_W)
    r = jnp.where(g0, lax.rsqrt(ms0 + RMS_EPS), lax.rsqrt(ms1 + RMS_EPS))
    return yg * r * ng, snew


def _ssd_fwd(pre, proj, par, B, S, name):
    T = B * S
    nc = S // CHUNK
    e_mat, psel = _ssd_consts()

    def body(pre_ref, z_ref, dt_ref, par_ref, e_ref, p_ref, y_ref, sall_ref, st):
        @pl.when(pl.program_id(1) == 0)
        def _():
            st[...] = jnp.zeros_like(st)

        sprev = st[...]
        sall_ref[...] = sprev
        y, snew = _ssd_chunk(pre_ref[...], z_ref[...], dt_ref[...], sprev, par_ref[...], e_ref[...], p_ref[...])
        y_ref[...] = y.astype(BF16)
        st[...] = snew

    row = lambda b, c: b * nc + c
    full = lambda shp: pl.BlockSpec(shp, lambda b, c: (0, 0))
    return pl.pallas_call(
        body, name=name, grid=(B, nc),
        in_specs=[pl.BlockSpec((CHUNK, SSD_CONV_DIM), lambda b, c: (row(b, c), 0)),
                  pl.BlockSpec((CHUNK, SSD_W), lambda b, c: (row(b, c), Z0 // SSD_W)),
                  pl.BlockSpec((CHUNK, LANE), lambda b, c: (row(b, c), DT0 // LANE)),
                  full((SUBLANE, SSD_W)), full((LANE, SSD_W)), full((SUBLANE, SSD_W))],
        out_specs=[pl.BlockSpec((CHUNK, SSD_W), lambda b, c: (row(b, c), 0)),
                   pl.BlockSpec((None, SSD_STATE, SSD_W), lambda b, c: (row(b, c), 0, 0))],
        out_shape=[jax.ShapeDtypeStruct((T, SSD_W), BF16),
                   jax.ShapeDtypeStruct((B * nc, SSD_STATE, SSD_W), F32)],
        scratch_shapes=[pltpu.VMEM((SSD_STATE, SSD_W), F32)],
        compiler_params=_cp(("parallel", "arbitrary")),
    )(pre, proj, proj, par, e_mat, psel)


def _ssd_bwd(pre, proj, sall, dy, par, B, S, name):
    T = B * S
    nc = S // CHUNK
    e_mat, psel = _ssd_consts()

    def body(pre_ref, z_ref, dt_ref, sall_ref, dy_ref, par_ref, e_ref, p_ref,
             dpre_ref, dz_ref, ddt_ref, dpar_ref, ds):
        b, c = pl.program_id(0), pl.program_id(1)

        @pl.when(c == 0)
        def _():
            ds[...] = jnp.zeros_like(ds)

        @pl.when((b == 0) & (c == 0))
        def _():
            dpar_ref[...] = jnp.zeros_like(dpar_ref)

        e_v, p_v = e_ref[...], p_ref[...]
        fn = lambda pre, z, dtr, sprev, par: _ssd_chunk(pre, z, dtr, sprev, par, e_v, p_v)
        _, vjp = jax.vjp(fn, pre_ref[...], z_ref[...], dt_ref[...], sall_ref[...], par_ref[...])
        dpre, dz, ddt, dsp, dpar = vjp((dy_ref[...], ds[...]))
        dpre_ref[...] = dpre
        dz_ref[...] = dz
        ddt_ref[...] = ddt
        dpar_ref[...] += dpar
        ds[...] = dsp

    row = lambda b, c: b * nc + (nc - 1 - c)
    full = lambda shp: pl.BlockSpec(shp, lambda b, c: (0, 0))
    return pl.pallas_call(
        body, name=name, grid=(B, nc),
        in_specs=[pl.BlockSpec((CHUNK, SSD_CONV_DIM), lambda b, c: (row(b, c), 0)),
                  pl.BlockSpec((CHUNK, SSD_W), lambda b, c: (row(b, c), Z0 // SSD_W)),
                  pl.BlockSpec((CHUNK, LANE), lambda b, c: (row(b, c), DT0 // LANE)),
                  pl.BlockSpec((None, SSD_STATE, SSD_W), lambda b, c: (row(b, c), 0, 0)),
                  pl.BlockSpec((CHUNK, SSD_W), lambda b, c: (row(b, c), ATT_W // SSD_W)),
                  full((SUBLANE, SSD_W)), full((LANE, SSD_W)), full((SUBLANE, SSD_W))],
        out_specs=[pl.BlockSpec((CHUNK, SSD_CONV_DIM), lambda b, c: (row(b, c), 0)),
                   pl.BlockSpec((CHUNK, SSD_W), lambda b, c: (row(b, c), 0)),
                   pl.BlockSpec((CHUNK, LANE), lambda b, c: (row(b, c), 0)),
                   full((SUBLANE, SSD_W))],
        out_shape=[jax.ShapeDtypeStruct((T, SSD_CONV_DIM), F32),
                   jax.ShapeDtypeStruct((T, SSD_W), F32),
                   jax.ShapeDtypeStruct((T, LANE), F32),
                   jax.ShapeDtypeStruct((SUBLANE, SSD_W), F32)],
        scratch_shapes=[pltpu.VMEM((SSD_STATE, SSD_W), F32)],
        compiler_params=_cp(("arbitrary", "arbitrary")),
    )(pre, proj, proj, sall, dy, par, e_mat, psel)


def _sgu_consts():
    e = np.zeros((SUBLANE, SGU_W), np.float32)
    for g in range(SGU_GROUPS):
        e[g, HEAD_DIM * g:HEAD_DIM * (g + 1)] = 1.0
    return jnp.asarray(e)


def _sgu_chunk(u_raw, v_raw, ln, w, bst, e4):
    L = CHUNK
    u = _gelu(u_raw)
    v = _gelu(v_raw)
    mu = jnp.mean(v, axis=-1, keepdims=True)
    vc = v - mu
    var = jnp.mean(vc * vc, axis=-1, keepdims=True)
    vn = vc * lax.rsqrt(var + LN_EPS) * ln[0:1] + ln[1:2]
    vb = vn.astype(BF16)
    ri = lax.broadcasted_iota(jnp.int32, (L, L), 0)
    ci = lax.broadcasted_iota(jnp.int32, (L, L), 1)
    tril = ri >= ci
    col = lax.broadcasted_iota(jnp.int32, (1, SGU_W), 1)
    mixed = _dot(bst, e4, HI)
    for g in range(SGU_GROUPS):
        wc = jnp.where(tril, w[g], 0.0).astype(BF16)
        gm = (col >= HEAD_DIM * g) & (col < HEAD_DIM * (g + 1))
        mixed = mixed + jnp.where(gm, _dot(wc, vb), 0.0)
    return u * mixed


def _sgu_fwd(proj, ln, w, bst, B, S, name):
    T = B * S
    nc = S // CHUNK
    e4 = _sgu_consts()

    def body(u_ref, v_ref, ln_ref, w_ref, b_ref, e_ref, y_ref):
        y_ref[...] = _sgu_chunk(u_ref[...], v_ref[...], ln_ref[...], w_ref[...], b_ref[...], e_ref[...]).astype(BF16)

    return pl.pallas_call(
        body, name=name, grid=(T // CHUNK,),
        in_specs=[pl.BlockSpec((CHUNK, SGU_W), lambda i: (i, U0 // SGU_W)),
                  pl.BlockSpec((CHUNK, SGU_W), lambda i: (i, VS0 // SGU_W)),
                  pl.BlockSpec((SUBLANE, SGU_W), lambda i: (0, 0)),
                  pl.BlockSpec((SGU_GROUPS, CHUNK, CHUNK), lambda i: (0, 0, 0)),
                  pl.BlockSpec((CHUNK, SUBLANE), lambda i: (0, 0)),
                  pl.BlockSpec((SUBLANE, SGU_W), lambda i: (0, 0))],
        out_specs=pl.BlockSpec((CHUNK, SGU_W), lambda i: (i, 0)),
        out_shape=jax.ShapeDtypeStruct((T, SGU_W), BF16),
        compiler_params=_cp(("parallel",)),
    )(proj, proj, ln, w, bst, e4)


def _sgu_bwd(proj, dy, ln, w, bst, B, S, name):
    T = B * S
    e4 = _sgu_consts()
    ycol = (ATT_W + SSD_W) // SGU_W

    def body(u_ref, v_ref, dy_ref, ln_ref, w_ref, b_ref, e_ref, du_ref, dv_ref, dln_ref, dw_ref, db_ref):
        @pl.when(pl.program_id(0) == 0)
        def _():
            dln_ref[...] = jnp.zeros_like(dln_ref)
            dw_ref[...] = jnp.zeros_like(dw_ref)
            db_ref[...] = jnp.zeros_like(db_ref)

        e_v = e_ref[...]
        fn = lambda u, v, ln, w, b: _sgu_chunk(u, v, ln, w, b, e_v)
        _, vjp = jax.vjp(fn, u_ref[...], v_ref[...], ln_ref[...], w_ref[...], b_ref[...])
        du, dv, dln, dw, db = vjp(dy_ref[...])
        du_ref[...] = du
        dv_ref[...] = dv
        dln_ref[...] += dln
        dw_ref[...] += dw
        db_ref[...] += db

    c_ln = pl.BlockSpec((SUBLANE, SGU_W), lambda i: (0, 0))
    c_w = pl.BlockSpec((SGU_GROUPS, CHUNK, CHUNK), lambda i: (0, 0, 0))
    c_b = pl.BlockSpec((CHUNK, SUBLANE), lambda i: (0, 0))
    return pl.pallas_call(
        body, name=name, grid=(T // CHUNK,),
        in_specs=[pl.BlockSpec((CHUNK, SGU_W), lambda i: (i, U0 // SGU_W)),
                  pl.BlockSpec((CHUNK, SGU_W), lambda i: (i, VS0 // SGU_W)),
                  pl.BlockSpec((CHUNK, SGU_W), lambda i: (i, ycol)),
                  c_ln, c_w, c_b, pl.BlockSpec((SUBLANE, SGU_W), lambda i: (0, 0))],
        out_specs=[pl.BlockSpec((CHUNK, SGU_W), lambda i: (i, 0)),
                   pl.BlockSpec((CHUNK, SGU_W), lambda i: (i, 0)), c_ln, c_w, c_b],
        out_shape=[jax.ShapeDtypeStruct((T, SGU_W), F32), jax.ShapeDtypeStruct((T, SGU_W), F32),
                   jax.ShapeDtypeStruct((SUBLANE, SGU_W), F32),
                   jax.ShapeDtypeStruct((SGU_GROUPS, CHUNK, CHUNK), F32),
                   jax.ShapeDtypeStruct((CHUNK, SUBLANE), F32)],
        compiler_params=_cp(("arbitrary",)),
    )(proj, proj, dy, ln, w, bst, e4)


def _exchange(xs, a2a, name):
    n = len(xs)

    def body(*refs):
        ins, outs = refs[:n], refs[n:2 * n]
        send_sems, recv_sems, loc_sems = refs[2 * n:]
        x, y, c = lax.axis_index("x"), lax.axis_index("y"), lax.axis_index("c")
        me = 4 * x + 2 * y + c
        local = []
        for t in range(n):
            src = ins[t].at[me] if a2a[t] else ins[t]
            cp = pltpu.make_async_copy(src, outs[t].at[me], loc_sems.at[t])
            cp.start()
            local.append(cp)
        sent = []
        for p in range(1, N_DEV):
            px, py, pc = x ^ ((p >> 2) & 1), y ^ ((p >> 1) & 1), c ^ (p & 1)
            peer = 4 * px + 2 * py + pc
            for t in range(n):
                src = ins[t].at[peer] if a2a[t] else ins[t]
                cp = pltpu.make_async_remote_copy(
                    src_ref=src, dst_ref=outs[t].at[me],
                    send_sem=send_sems.at[p - 1, t], recv_sem=recv_sems.at[p - 1, t],
                    device_id=(px, py, pc), device_id_type=MESH)
                cp.start()
                landed = pltpu.make_async_remote_copy(
                    src_ref=src, dst_ref=outs[t].at[peer],
                    send_sem=send_sems.at[p - 1, t], recv_sem=recv_sems.at[p - 1, t],
                    device_id=(px, py, pc), device_id_type=MESH)
                sent.append((cp, landed))
        for cp, landed in sent:
            landed.wait_recv()
        for cp, landed in sent:
            cp.wait_send()
        for cp in local:
            cp.wait()

    out_shape = [jax.ShapeDtypeStruct(a.shape if f else (N_DEV,) + a.shape, a.dtype) for a, f in zip(xs, a2a)]
    anyspec = pl.BlockSpec(memory_space=pl.ANY)
    return pl.pallas_call(
        body, name=name,
        in_specs=[anyspec] * n, out_specs=[anyspec] * n, out_shape=out_shape,
        scratch_shapes=[pltpu.SemaphoreType.DMA((N_DEV - 1, n)), pltpu.SemaphoreType.DMA((N_DEV - 1, n)),
                        pltpu.SemaphoreType.DMA((n,))],
    )(*xs)


def _adamw(w, m, v, parts, name):
    R, C = w.shape
    P = parts.shape[0]
    tr = R
    t = 16
    while t <= R:
        if R % t == 0 and t * C <= 131072:
            tr = t
        t += 16
    if tr == R and R * C > 131072 and R % 16 == 0:
        tr = 16

    def body(w_ref, m_ref, v_ref, p_ref, g_ref, d_ref, mo_ref, vo_ref):
        g = p_ref[0].astype(F32)
        for p in range(1, P):
            g = g + p_ref[p].astype(F32)
        mn = ADAM_B1 * m_ref[...] + (1.0 - ADAM_B1) * g
        vn = ADAM_B2 * v_ref[...] + (1.0 - ADAM_B2) * (g * g)
        m_hat = mn / (1.0 - ADAM_B1 ** ADAM_STEP)
        v_hat = vn / (1.0 - ADAM_B2 ** ADAM_STEP)
        g_ref[...] = g
        d_ref[...] = -ADAM_LR * (m_hat / (jnp.sqrt(v_hat) + ADAM_EPS) + ADAM_WD * w_ref[...])
        mo_ref[...] = mn
        vo_ref[...] = vn

    blk = pl.BlockSpec((tr, C), lambda i: (i, 0))
    return pl.pallas_call(
        body, name=name, grid=(R // tr,),
        in_specs=[blk, blk, blk, pl.BlockSpec((P, tr, C), lambda i: (0, i, 0))],
        out_specs=[blk] * 4, out_shape=[jax.ShapeDtypeStruct((R, C), F32)] * 4,
        compiler_params=_cp(("parallel",)),
    )(w, m, v, parts)


def _perm_cols(w):
    pad = jnp.zeros((w.shape[0], LANE - SSD_HEADS), w.dtype)
    return jnp.concatenate([w[:, 0:1536], w[:, 2438:2694], w[:, 1536:2432], w[:, 2432:2438], pad,
                            w[:, 2694:2950]], axis=1)


def _unperm_cols(w):
    return jnp.concatenate([w[:, 0:1536], w[:, XBC0:XBC0 + SSD_CONV_DIM], w[:, DT0:DT0 + SSD_HEADS],
                            w[:, U0:U0 + SGU_W], w[:, VS0:VS0 + SGU_W]], axis=1)


_SMALL = ("ffn1_norm", "mix_norm", "conv_w", "conv_b", "dt_bias", "a_log", "d_skip", "ssd_norm",
          "sgu_ln_g", "sgu_ln_b", "sgu_w", "sgu_b", "ffn2_norm", "final_norm")


def _pack(d):
    v = jnp.concatenate([d[k].astype(F32).reshape(-1) for k in _SMALL])
    n = v.shape[0]
    npad = -(-n // (LANE * 16)) * (LANE * 16)
    return jnp.pad(v, (0, npad - n)).reshape(npad // LANE, LANE)


def _unpack(p, shapes):
    v = p.reshape(-1)
    out, o = {}, 0
    for k in _SMALL:
        n = int(np.prod(shapes[k]))
        out[k] = v[o:o + n].reshape(shapes[k])
        o += n
    return out


def kernel(x, ffn1_norm, ffn1_w_gate, ffn1_w_up, ffn1_w_down, mix_norm, w_in, conv_w, conv_b, dt_bias, a_log, d_skip, ssd_norm, sgu_ln_g, sgu_ln_b, sgu_w, sgu_b, w_out, ffn2_norm, ffn2_w_gate, ffn2_w_up, ffn2_w_down, final_norm, loss_target, m_ffn1_norm, m_ffn1_w_gate, m_ffn1_w_up, m_ffn1_w_down, m_mix_norm, m_w_in, m_conv_w, m_conv_b, m_dt_bias, m_a_log, m_d_skip, m_ssd_norm, m_sgu_ln_g, m_sgu_ln_b, m_sgu_w, m_sgu_b, m_w_out, m_ffn2_norm, m_ffn2_w_gate, m_ffn2_w_up, m_ffn2_w_down, m_final_norm, v_ffn1_norm, v_ffn1_w_gate, v_ffn1_w_up, v_ffn1_w_down, v_mix_norm, v_w_in, v_conv_w, v_conv_b, v_dt_bias, v_a_log, v_d_skip, v_ssd_norm, v_sgu_ln_g, v_sgu_ln_b, v_sgu_w, v_sgu_b, v_w_out, v_ffn2_norm, v_ffn2_w_gate, v_ffn2_w_up, v_ffn2_w_down, v_final_norm):
    B, S, D = x.shape
    T = B * S
    L = ffn1_norm.shape[0]
    me = 4 * lax.axis_index("x") + 2 * lax.axis_index("y") + lax.axis_index("c")
    cs = conv_w.shape[2]
    W = dict(ffn1_norm=ffn1_norm, ffn1_w_gate=ffn1_w_gate, ffn1_w_up=ffn1_w_up, ffn1_w_down=ffn1_w_down,
             mix_norm=mix_norm, w_in=w_in, conv_w=conv_w, conv_b=conv_b, dt_bias=dt_bias, a_log=a_log,
             d_skip=d_skip, ssd_norm=ssd_norm, sgu_ln_g=sgu_ln_g, sgu_ln_b=sgu_ln_b, sgu_w=sgu_w, sgu_b=sgu_b,
             w_out=w_out, ffn2_norm=ffn2_norm, ffn2_w_gate=ffn2_w_gate, ffn2_w_up=ffn2_w_up,
             ffn2_w_down=ffn2_w_down, final_norm=final_norm)
    M = dict(ffn1_norm=m_ffn1_norm, ffn1_w_gate=m_ffn1_w_gate, ffn1_w_up=m_ffn1_w_up, ffn1_w_down=m_ffn1_w_down,
             mix_norm=m_mix_norm, w_in=m_w_in, conv_w=m_conv_w, conv_b=m_conv_b, dt_bias=m_dt_bias, a_log=m_a_log,
             d_skip=m_d_skip, ssd_norm=m_ssd_norm, sgu_ln_g=m_sgu_ln_g, sgu_ln_b=m_sgu_ln_b, sgu_w=m_sgu_w,
             sgu_b=m_sgu_b, w_out=m_w_out, ffn2_norm=m_ffn2_norm, ffn2_w_gate=m_ffn2_w_gate,
             ffn2_w_up=m_ffn2_w_up, ffn2_w_down=m_ffn2_w_down, final_norm=m_final_norm)
    V = dict(ffn1_norm=v_ffn1_norm, ffn1_w_gate=v_ffn1_w_gate, ffn1_w_up=v_ffn1_w_up, ffn1_w_down=v_ffn1_w_down,
             mix_norm=v_mix_norm, w_in=v_w_in, conv_w=v_conv_w, conv_b=v_conv_b, dt_bias=v_dt_bias, a_log=v_a_log,
             d_skip=v_d_skip, ssd_norm=v_ssd_norm, sgu_ln_g=v_sgu_ln_g, sgu_ln_b=v_sgu_ln_b, sgu_w=v_sgu_w,
             sgu_b=v_sgu_b, w_out=v_w_out, ffn2_norm=v_ffn2_norm, ffn2_w_gate=v_ffn2_w_gate,
             ffn2_w_up=v_ffn2_w_up, ffn2_w_down=v_ffn2_w_down, final_norm=v_final_norm)
    big = ("ffn1_w_gate", "ffn1_w_up", "ffn1_w_down", "w_in", "w_out", "ffn2_w_gate", "ffn2_w_up", "ffn2_w_down")

    gathered = _exchange([W[k].astype(BF16) for k in big] + [conv_w], [False] * (len(big) + 1), "gather_weights")
    G = dict(zip(big + ("conv_w",), gathered))

    def cols(k, l):
        a = G[k][:, l]
        return jnp.transpose(a, (1, 0, 2)).reshape(a.shape[1], -1)

    def rows(k, l):
        a = G[k][:, l]
        return a.reshape(-1, a.shape[-1])

    bias = _attn_bias(S, min(256, S))
    row1 = lambda a: a.reshape(1, -1)
    lay = []
    for l in range(L):
        cw = jnp.transpose(G["conv_w"][:, l], (1, 0, 2)).reshape(SSD_CONV, -1)
        lay.append(dict(
            g1=row1(ffn1_norm[l]), wg1=cols("ffn1_w_gate", l), wu1=cols("ffn1_w_up", l), wd1=rows("ffn1_w_down", l),
            gm=row1(mix_norm[l]), win=_perm_cols(rows("w_in", l)), wout=rows("w_out", l),
            g2=row1(ffn2_norm[l]), wg2=cols("ffn2_w_gate", l), wu2=cols("ffn2_w_up", l), wd2=rows("ffn2_w_down", l),
            cw=jnp.pad(cw, ((0, SUBLANE - SSD_CONV), (0, 0))), cb=row1(conv_b[l]),
            par=jnp.pad(jnp.stack([jnp.repeat(dt_bias[l], HEAD_DIM), jnp.repeat(a_log[l], HEAD_DIM),
                                   jnp.repeat(d_skip[l], HEAD_DIM), ssd_norm[l]]), ((0, SUBLANE - 4), (0, 0))),
            ln=jnp.pad(jnp.stack([sgu_ln_g[l], sgu_ln_b[l]]), ((0, SUBLANE - 2), (0, 0))),
            sw=sgu_w[l], bst=jnp.pad(sgu_b[l].T, ((0, 0), (0, SUBLANE - SGU_GROUPS)))))

    xc = x.reshape(T, D)
    saved = []
    for l in range(L):
        p = lay[l]
        x1, gate1, up1 = _ffn_fwd(xc, p["g1"], p["wg1"], p["wu1"], p["wd1"], f"ffn1_fwd_{l}")
        proj, h = _norm_mm(x1, p["gm"], p["win"], f"in_proj_{l}")
        o_att, lse = _attn_fwd(proj, bias, B, S, f"attn_fwd_{l}")
        pre = _conv_fwd(proj, p["cw"], p["cb"], B, S, f"conv_fwd_{l}")
        y_ssd, sall = _ssd_fwd(pre, proj, p["par"], B, S, f"ssd_fwd_{l}")
        y_sgu = _sgu_fwd(proj, p["ln"], p["sw"], p["bst"], B, S, f"sgu_fwd_{l}")
        ycat = jnp.concatenate([o_att.astype(BF16), y_ssd, y_sgu], axis=1)
        x2 = _mm(ycat, p["wout"], "nn", f"out_proj_{l}", residual=x1)
        x3, gate2, up2 = _ffn_fwd(x2, p["g2"], p["wg2"], p["wu2"], p["wd2"], f"ffn2_fwd_{l}")
        saved.append(dict(x0=xc, gate1=gate1, up1=up1, x1=x1, h=h, proj=proj, o_att=o_att, lse=lse, pre=pre,
                          sall=sall, ycat=ycat, x2=x2, gate2=gate2, up2=up2))
        xc = x3
    loss_part, dx, dgf = _final_loss(xc, row1(final_norm), loss_target.reshape(T, D), "final_loss")
    loss = lax.psum(loss_part[0, 0], ("x", "y", "c"))

    gl = [dict() for _ in range(L)]
    for l in reversed(range(L)):
        p, s, g = lay[l], saved[l], gl[l]
        dx2, dgate, dup, act, xn, dacc, g["ffn2_norm"] = _ffn_bwd_dx(
            dx, s["x2"], p["g2"], s["gate2"], s["up2"], p["wg2"], p["wu2"], p["wd2"], f"ffn2_bwd_{l}")
        g["ffn2_w_gate"] = _mm(xn, dgate, "tn", f"ffn2_dwg_{l}")
        g["ffn2_w_up"] = _mm(xn, dup, "tn", f"ffn2_dwu_{l}")
        g["ffn2_w_down"] = _mm(act, dacc, "tn", f"ffn2_dwd_{l}")
        dycat = _mm(dx2, p["wout"], "nt", f"out_proj_dx_{l}")
        g["w_out"] = _mm(s["ycat"], dx2, "tn", f"out_proj_dw_{l}")
        dq, dk, dv = _attn_bwd(s["proj"], s["o_att"], s["lse"], dycat, bias, B, S, f"attn_bwd_{l}")
        dpre, dz, ddt, dpar = _ssd_bwd(s["pre"], s["proj"], s["sall"], dycat, p["par"], B, S, f"ssd_bwd_{l}")
        dxbc, dwb = _conv_bwd(dpre, s["proj"], p["cw"], B, S, f"conv_bwd_{l}")
        du, dvs, dln, dsw, dbst = _sgu_bwd(s["proj"], dycat, p["ln"], p["sw"], p["bst"], B, S, f"sgu_bwd_{l}")
        dproj = jnp.concatenate([dq, dk, dv, dz, du, dxbc, ddt, dvs], axis=1)
        dh = _mm(dproj, p["win"], "nt", f"in_proj_dx_{l}")
        g["w_in"] = _unperm_cols(_mm(s["h"], dproj, "tn", f"in_proj_dw_{l}"))
        dx1, g["mix_norm"] = _rms_bwd_call(dh, s["x1"], p["gm"], dx2, f"mix_norm_bwd_{l}")
        dx, dgate, dup, act, xn, dacc, g["ffn1_norm"] = _ffn_bwd_dx(
            dx1, s["x0"], p["g1"], s["gate1"], s["up1"], p["wg1"], p["wu1"], p["wd1"], f"ffn1_bwd_{l}")
        g["ffn1_w_gate"] = _mm(xn, dgate, "tn", f"ffn1_dwg_{l}")
        g["ffn1_w_up"] = _mm(xn, dup, "tn", f"ffn1_dwu_{l}")
        g["ffn1_w_down"] = _mm(act, dacc, "tn", f"ffn1_dwd_{l}")
        hsum = lambda r: r.reshape(SSD_HEADS, HEAD_DIM).sum(-1)
        g["conv_w"], g["conv_b"] = dwb[:SSD_CONV], dwb[SSD_CONV]
        g["dt_bias"], g["a_log"], g["d_skip"], g["ssd_norm"] = hsum(dpar[0]), hsum(dpar[1]), hsum(dpar[2]), dpar[3]
        g["sgu_ln_g"], g["sgu_ln_b"], g["sgu_w"], g["sgu_b"] = dln[0], dln[1], dsw, dbst[:, :SGU_GROUPS].T
    grad_x = dx.reshape(B, S, D)

    stack = lambda k: jnp.stack([gl[l][k] for l in range(L)])

    def to_blocks(k):
        a = stack(k)
        if k.endswith("w_gate") or k.endswith("w_up"):
            a = a.reshape(L, a.shape[1], N_DEV, -1).transpose(2, 0, 1, 3)
        else:
            a = a.reshape(L, N_DEV, -1, a.shape[-1]).transpose(1, 0, 2, 3)
        return a.astype(BF16)

    small = {k: (stack(k) if k != "final_norm" else dgf.reshape(-1)) for k in _SMALL}
    small = {k: small[k].reshape((L,) + W[k].shape[1:]) if k not in ("final_norm", "conv_w") else small[k]
             for k in _SMALL}
    full_shapes = {k: (W[k].shape if k != "conv_w" else (L, SSD_CONV, SSD_CONV_DIM)) for k in _SMALL}
    parts = _exchange([to_blocks(k) for k in big] + [_pack(small)], [True] * len(big) + [False], "exchange_grads")

    grads, deltas, new_m, new_v = {}, {}, {}, {}
    for k, pk in zip(big, parts[:-1]):
        shp = W[k].shape
        two = lambda a: a.reshape(-1, shp[-1])
        outs = _adamw(two(W[k]), two(M[k]), two(V[k]), pk.reshape(N_DEV, -1, shp[-1]), f"adamw_{k}")
        grads[k], deltas[k], new_m[k], new_v[k] = [o.reshape(shp) for o in outs]

    def embed(a, k):
        if k != "conv_w":
            return a
        return lax.dynamic_update_slice(jnp.zeros(full_shapes[k], F32), a, (0, 0, me * cs))

    outs = _adamw(_pack({k: embed(W[k], k) for k in _SMALL}), _pack({k: embed(M[k], k) for k in _SMALL}),
                  _pack({k: embed(V[k], k) for k in _SMALL}), parts[-1], "adamw_small")
    for d, o in zip((grads, deltas, new_m, new_v), outs):
        u = _unpack(o, full_shapes)
        u["conv_w"] = lax.dynamic_slice(u["conv_w"], (0, 0, me * cs), (L, SSD_CONV, cs))
        d.update(u)

    names = ("ffn1_norm", "ffn1_w_gate", "ffn1_w_up", "ffn1_w_down", "mix_norm", "w_in", "conv_w", "conv_b",
             "dt_bias", "a_log", "d_skip", "ssd_norm", "sgu_ln_g", "sgu_ln_b", "sgu_w", "sgu_b", "w_out",
             "ffn2_norm", "ffn2_w_gate", "ffn2_w_up", "ffn2_w_down", "final_norm")
    return (loss, grad_x, *[grads[n] for n in names], *[deltas[n] for n in names],
            *[new_m[n] for n in names], *[new_v[n] for n in names])
```

```python
import functools

import numpy as np
import jax
import jax.numpy as jnp
from jax import lax
from jax.experimental import pallas as pl
from jax.experimental.pallas import tpu as pltpu

F32, BF16 = jnp.float32, jnp.bfloat16
HI = lax.Precision.HIGHEST
MESH = pl.DeviceIdType.MESH
N_DEV = 8
VMEM_LIMIT_BYTES = 56 * 1024 * 1024
LANE, SUBLANE = 128, 8

HEAD_DIM = 64
ATT_W = 384
SSD_W = 384
SSD_HEADS = 6
SSD_STATE = 128
SSD_CONV = 4
CHUNK = 128
SSD_CONV_DIM = 896
SGU_W = 256
SGU_GROUPS = 4
D_IN = 2950
RMS_EPS = 1e-6
LN_EPS = 1e-5
NEG = -1e30

PW = 3072
Q0, K0, V0, Z0, U0, XBC0, DT0, VS0 = 0, 384, 768, 1152, 1536, 1792, 2688, 2816

ADAM_LR, ADAM_B1, ADAM_B2, ADAM_EPS, ADAM_WD, ADAM_STEP = 0.001, 0.9, 0.999, 1e-08, 0.01, 10


def _cp(sem=None):
    return pltpu.CompilerParams(dimension_semantics=sem, vmem_limit_bytes=VMEM_LIMIT_BYTES)


def _tile(n, cap, mult=LANE):
    best = None
    t = mult
    while t <= min(n, cap):
        if n % t == 0:
            best = t
        t += mult
    return best if best is not None else n


def _dot(a, b, prec=None):
    return jnp.dot(a, b, preferred_element_type=F32, precision=prec)


def _dot_nt(a, b, prec=None):
    return lax.dot_general(a, b, (((1,), (1,)), ((), ())), preferred_element_type=F32, precision=prec)


def _dot_tn(a, b, prec=None):
    return lax.dot_general(a, b, (((0,), (0,)), ((), ())), preferred_element_type=F32, precision=prec)


def _sigmoid(x):
    return 1.0 / (1.0 + jnp.exp(-x))


def _silu(x):
    return x * _sigmoid(x)


def _gelu(x):
    return 0.5 * x * (1.0 + lax.erf(x * 0.7071067811865476))


def _softplus(x):
    return jnp.maximum(x, 0.0) + jnp.log(1.0 + jnp.exp(-jnp.abs(x)))


def _rms_fwd(x, g):
    rstd = lax.rsqrt(jnp.mean(x * x, axis=-1, keepdims=True) + RMS_EPS)
    xhat = x * rstd
    return xhat * g, xhat, rstd


def _rms_bwd(dy, xhat, rstd, g):
    dxhat = dy * g
    dx = rstd * (dxhat - xhat * jnp.mean(dxhat * xhat, axis=-1, keepdims=True))
    return dx, dy * xhat


def _resident(shape):
    return pl.BlockSpec(shape, lambda *_: (0,) * len(shape), pipeline_mode=pl.Buffered(1))


def _mm(a, b, mode, name, out_dtype=F32, residual=None, tm_cap=512, tn_cap=1024, tk_cap=1024):
    if mode == "nn":
        (M, K), (_, N) = a.shape, b.shape
    elif mode == "nt":
        (M, K), (N, _) = a.shape, b.shape
    else:
        (K, M), (_, N) = a.shape, b.shape
    tm, tn, tk = _tile(M, tm_cap), _tile(N, tn_cap), _tile(K, tk_cap)
    nk = K // tk
    if mode == "tn":
        a_spec = pl.BlockSpec((tk, tm), lambda i, j, k: (k, i))
    else:
        a_spec = pl.BlockSpec((tm, tk), lambda i, j, k: (i, k))
    if mode == "nt":
        b_spec = pl.BlockSpec((tn, tk), lambda i, j, k: (j, k))
    else:
        b_spec = pl.BlockSpec((tk, tn), lambda i, j, k: (k, j))
    o_spec = pl.BlockSpec((tm, tn), lambda i, j, k: (i, j))
    has_res = residual is not None

    def prod(a_ref, b_ref):
        av = a_ref[...].astype(BF16)
        bv = b_ref[...].astype(BF16)
        if mode == "nn":
            return _dot(av, bv)
        if mode == "nt":
            return _dot_nt(av, bv)
        return _dot_tn(av, bv)

    def body(*refs):
        a_ref, b_ref = refs[:2]
        r_ref = refs[2] if has_res else None
        o_ref = refs[2 + has_res]
        if nk == 1:
            o = prod(a_ref, b_ref)
            if has_res:
                o = r_ref[...] + o
            o_ref[...] = o.astype(out_dtype)
            return
        acc = refs[3 + has_res]
        k = pl.program_id(2)

        @pl.when(k == 0)
        def _():
            acc[...] = jnp.zeros_like(acc)

        acc[...] += prod(a_ref, b_ref)

        @pl.when(k == nk - 1)
        def _():
            o = acc[...]
            if has_res:
                o = r_ref[...] + o
            o_ref[...] = o.astype(out_dtype)

    ins = [a, b] + ([residual] if has_res else [])
    in_specs = [a_spec, b_spec] + ([o_spec] if has_res else [])
    return pl.pallas_call(
        body, name=name, grid=(M // tm, N // tn, nk),
        in_specs=in_specs, out_specs=o_spec,
        out_shape=jax.ShapeDtypeStruct((M, N), out_dtype),
        scratch_shapes=[pltpu.VMEM((tm, tn), F32)] if nk > 1 else [],
        compiler_params=_cp(("parallel", "parallel", "arbitrary")),
    )(*ins)


def _ffn_fwd(x, g, wg, wu, wd, name):
    T, D = x.shape
    F = wg.shape[1]
    tm = _tile(T, 512)

    def body(x_ref, g_ref, wg_ref, wu_ref, wd_ref, out_ref, gate_ref, up_ref):
        xv = x_ref[...]
        xn = _rms_fwd(xv, g_ref[...])[0].astype(BF16)
        gate = _dot(xn, wg_ref[...])
        up = _dot(xn, wu_ref[...])
        gate_ref[...] = gate.astype(BF16)
        up_ref[...] = up.astype(BF16)
        act = (_silu(gate) * up).astype(BF16)
        out_ref[...] = xv + 0.5 * _dot(act, wd_ref[...])

    row = lambda w: pl.BlockSpec((tm, w), lambda i: (i, 0))
    return pl.pallas_call(
        body, name=name, grid=(T // tm,),
        in_specs=[row(D), _resident((1, D)), _resident((D, F)), _resident((D, F)), _resident((F, D))],
        out_specs=[row(D), row(F), row(F)],
        out_shape=[jax.ShapeDtypeStruct((T, D), F32),
                   jax.ShapeDtypeStruct((T, F), BF16),
                   jax.ShapeDtypeStruct((T, F), BF16)],
        compiler_params=_cp(("parallel",)),
    )(x, g, wg, wu, wd)


def _ffn_bwd_dx(dout, x, g, gate, up, wg, wu, wd, name):
    T, D = x.shape
    F = wg.shape[1]
    tm, th = _tile(T, 256), _tile(F, 256)
    nj = F // th

    def body(dout_ref, x_ref, g_ref, gate_ref, up_ref, wg_ref, wu_ref, wd_ref,
             dx_ref, dgate_ref, dup_ref, act_ref, xnt_ref, dacct_ref, dg_ref):
        @pl.when(pl.program_id(0) == 0)
        def _():
            dg_ref[...] = jnp.zeros_like(dg_ref)

        gv = g_ref[...]
        dout_v = dout_ref[...]
        xn, xhat, rstd = _rms_fwd(x_ref[...], gv)
        xnt_ref[...] = xn.T.astype(BF16)
        dacc = 0.5 * dout_v
        dacct_ref[...] = dacc.T.astype(BF16)
        dact = _dot_nt(dacc.astype(BF16), wd_ref[...])
        gt = gate_ref[...].astype(F32)
        u = up_ref[...].astype(F32)
        sig = _sigmoid(gt)
        sl = gt * sig
        dgate = (dact * u * (sig * (1.0 + gt * (1.0 - sig)))).astype(BF16)
        dup = (dact * sl).astype(BF16)
        act = (sl * u).astype(BF16)
        for j in range(nj):
            cs = slice(j * th, (j + 1) * th)
            dgate_ref[j] = dgate[:, cs]
            dup_ref[j] = dup[:, cs]
            act_ref[j] = act[:, cs]
        dxn = _dot_nt(dgate, wg_ref[...]) + _dot_nt(dup, wu_ref[...])
        dx, dgrow = _rms_bwd(dxn, xhat, rstd, gv)
        dx_ref[...] = dout_v + dx
        dg_ref[...] += jnp.sum(dgrow, axis=0, keepdims=True)

    row = lambda w: pl.BlockSpec((tm, w), lambda i: (i, 0))
    tiled = pl.BlockSpec((nj, tm, th), lambda i: (0, i, 0))
    tr = pl.BlockSpec((D, tm), lambda i: (0, i))
    return pl.pallas_call(
        body, name=name, grid=(T // tm,),
        in_specs=[row(D), row(D), _resident((1, D)), row(F), row(F),
                  _resident((D, F)), _resident((D, F)), _resident((F, D))],
        out_specs=[row(D), tiled, tiled, tiled, tr, tr, pl.BlockSpec((1, D), lambda i: (0, 0))],
        out_shape=[jax.ShapeDtypeStruct((T, D), F32)] + [jax.ShapeDtypeStruct((nj, T, th), BF16)] * 3
        + [jax.ShapeDtypeStruct((D, T), BF16)] * 2 + [jax.ShapeDtypeStruct((1, D), F32)],
        compiler_params=_cp(("arbitrary",)),
    )(dout, x, g, gate, up, wg, wu, wd)


def _ffn_dw(xnt, dacct, dgate, dup, act, name):
    D, T = xnt.shape
    nj, _, th = dgate.shape

    def body(xnt_ref, dacct_ref, dgate_ref, dup_ref, act_ref, dwg_ref, dwu_ref, dwdt_ref):
        xv = xnt_ref[...]
        dwg_ref[...] = _dot(xv, dgate_ref[...])
        dwu_ref[...] = _dot(xv, dup_ref[...])
        dwdt_ref[...] = _dot(dacct_ref[...], act_ref[...])

    tile = pl.BlockSpec((None, T, th), lambda j: (j, 0, 0))
    out = pl.BlockSpec((D, th), lambda j: (0, j))
    return pl.pallas_call(
        body, name=name, grid=(nj,),
        in_specs=[_resident((D, T)), _resident((D, T)), tile, tile, tile],
        out_specs=[out, out, out], out_shape=[jax.ShapeDtypeStruct((D, nj * th), F32)] * 3,
        compiler_params=_cp(("parallel",)),
    )(xnt, dacct, dgate, dup, act)


def _norm_mm(x, g, w, name):
    T, D = x.shape
    N = w.shape[1]
    tm = _tile(T, 512)

    def body(x_ref, g_ref, w_ref, o_ref, ht_ref):
        xn = _rms_fwd(x_ref[...], g_ref[...])[0]
        ht_ref[...] = xn.T.astype(BF16)
        o_ref[...] = _dot(xn.astype(BF16), w_ref[...])

    return pl.pallas_call(
        body, name=name, grid=(T // tm,),
        in_specs=[pl.BlockSpec((tm, D), lambda i: (i, 0)), _resident((1, D)), _resident((D, N))],
        out_specs=[pl.BlockSpec((tm, N), lambda i: (i, 0)), pl.BlockSpec((D, tm), lambda i: (0, i))],
        out_shape=[jax.ShapeDtypeStruct((T, N), F32), jax.ShapeDtypeStruct((D, T), BF16)],
        compiler_params=_cp(("parallel",)),
    )(x, g, w)


def _norm_mm_bwd(dproj, x, g, w, dres, name):
    T, D = x.shape
    N = w.shape[1]
    tm = _tile(T, 512)

    def body(dp_ref, x_ref, g_ref, w_ref, dres_ref, dx_ref, dg_ref):
        @pl.when(pl.program_id(0) == 0)
        def _():
            dg_ref[...] = jnp.zeros_like(dg_ref)

        gv = g_ref[...]
        dh = _dot_nt(dp_ref[...], w_ref[...])
        _, xhat, rstd = _rms_fwd(x_ref[...], gv)
        dx, dgrow = _rms_bwd(dh, xhat, rstd, gv)
        dx_ref[...] = dres_ref[...] + dx
        dg_ref[...] += jnp.sum(dgrow, axis=0, keepdims=True)

    row = pl.BlockSpec((tm, D), lambda i: (i, 0))
    one = pl.BlockSpec((1, D), lambda i: (0, 0))
    return pl.pallas_call(
        body, name=name, grid=(T // tm,),
        in_specs=[pl.BlockSpec((tm, N), lambda i: (i, 0)), row, _resident((1, D)), _resident((D, N)), row],
        out_specs=[row, one],
        out_shape=[jax.ShapeDtypeStruct((T, D), F32), jax.ShapeDtypeStruct((1, D), F32)],
        compiler_params=_cp(("arbitrary",)),
    )(dproj, x, g, w, dres)


def _mm_resident_lhs(at, b, name, tn_cap=512):
    M, K = at.shape
    N = b.shape[1]
    tn = _tile(N, tn_cap)

    def body(a_ref, b_ref, o_ref):
        o_ref[...] = _dot(a_ref[...], b_ref[...])

    return pl.pallas_call(
        body, name=name, grid=(N // tn,),
        in_specs=[_resident((M, K)), pl.BlockSpec((K, tn), lambda j: (0, j))],
        out_specs=pl.BlockSpec((M, tn), lambda j: (0, j)),
        out_shape=jax.ShapeDtypeStruct((M, N), F32),
        compiler_params=_cp(("parallel",)),
    )(at, b)


def _final_loss(x, g, target, name):
    T, D = x.shape
    tm = _tile(T, 512)

    def body(x_ref, g_ref, t_ref, loss_ref, dx_ref, dg_ref):
        @pl.when(pl.program_id(0) == 0)
        def _():
            dg_ref[...] = jnp.zeros_like(dg_ref)
            loss_ref[...] = jnp.zeros_like(loss_ref)

        gv = g_ref[...]
        y, xhat, rstd = _rms_fwd(x_ref[...], gv)
        err = y - t_ref[...]
        part = 0.5 * jnp.sum(jnp.mean(err * err, axis=-1, keepdims=True), axis=0, keepdims=True)
        loss_ref[...] += jnp.broadcast_to(part, loss_ref.shape)
        dy = err * (1.0 / D)
        dx, dgrow = _rms_bwd(dy, xhat, rstd, gv)
        dx_ref[...] = dx
        dg_ref[...] += jnp.sum(dgrow, axis=0, keepdims=True)

    row = pl.BlockSpec((tm, D), lambda i: (i, 0))
    one = pl.BlockSpec((1, D), lambda i: (0, 0))
    return pl.pallas_call(
        body, name=name, grid=(T // tm,),
        in_specs=[row, one, row],
        out_specs=[pl.BlockSpec((1, LANE), lambda i: (0, 0)), row, one],
        out_shape=[jax.ShapeDtypeStruct((1, LANE), F32), jax.ShapeDtypeStruct((T, D), F32),
                   jax.ShapeDtypeStruct((1, D), F32)],
        compiler_params=_cp(("arbitrary",)),
    )(x, g, target)


def _attn_bias(S, bq):
    nb = S // bq
    d = (jnp.arange(nb)[:, None, None] * bq + jnp.arange(bq)[None, :, None]
         - jnp.arange(bq)[None, None, :])
    ok = d >= 0
    mult = ((ok & (d <= 128)).astype(F32) + (ok & (d % 4 == 0) & (d <= 512)).astype(F32)
            + (ok & (d % 16 == 0) & (d <= 2048)).astype(F32))
    return jnp.where(mult > 0, jnp.log(jnp.maximum(mult, 1.0)), NEG).astype(F32)


def _attn_fwd(proj, bias, B, S, name):
    T = B * S
    nb, bq, _ = bias.shape
    qcol, kcol, vcol = Q0 // LANE, K0 // LANE, V0 // LANE

    def body(q_ref, k_ref, v_ref, bias_ref, o_ref, lse_ref, qs, ks, vs):
        for hh in range(2):
            sl = slice(HEAD_DIM * hh, HEAD_DIM * (hh + 1))
            qs[hh] = (q_ref[:, sl] * 0.125).astype(BF16)
            ks[hh] = k_ref[:, sl].astype(BF16)
            vs[hh] = v_ref[:, sl].astype(BF16)
        def qloop(qb, carry):
            q0 = pl.multiple_of(qb * bq, bq)
            qv = [qs[hh, pl.ds(q0, bq), :] for hh in range(2)]

            def kloop(kb, c):
                k0 = pl.multiple_of(kb * bq, bq)
                bias_t = bias_ref[qb - kb]
                out = []
                for hh in range(2):
                    m, l, acc = c[hh]
                    s = _dot_nt(qv[hh], ks[hh, pl.ds(k0, bq), :]) + bias_t
                    mn = jnp.maximum(m, jnp.max(s, axis=-1, keepdims=True))
                    p = jnp.exp(s - mn)
                    a = jnp.exp(m - mn)
                    l = a * l + jnp.sum(p, axis=-1, keepdims=True)
                    acc = a * acc + _dot(p.astype(BF16), vs[hh, pl.ds(k0, bq), :])
                    out.append((mn, l, acc))
                return tuple(out)

            one = (jnp.full((bq, 1), NEG, F32), jnp.zeros((bq, 1), F32), jnp.zeros((bq, HEAD_DIM), F32))
            res = lax.fori_loop(0, qb + 1, kloop, (one, one))
            for hh in range(2):
                m, l, acc = res[hh]
                o_ref[pl.ds(q0, bq), HEAD_DIM * hh:HEAD_DIM * (hh + 1)] = acc / l
                lse_ref[pl.ds(q0, bq), hh:hh + 1] = m + jnp.log(l)
            return carry

        lax.fori_loop(0, nb, qloop, 0)

    blk = lambda c0: pl.BlockSpec((S, LANE), lambda b, p: (b, c0 + p))
    return pl.pallas_call(
        body, name=name, grid=(B, ATT_W // LANE),
        in_specs=[blk(qcol), blk(kcol), blk(vcol),
                  pl.BlockSpec((nb, bq, bq), lambda b, p: (0, 0, 0))],
        out_specs=[pl.BlockSpec((S, LANE), lambda b, p: (b, p)),
                   pl.BlockSpec((None, None, S, 2), lambda b, p: (b, p, 0, 0))],
        out_shape=[jax.ShapeDtypeStruct((T, ATT_W), F32),
                   jax.ShapeDtypeStruct((B, ATT_W // LANE, S, 2), F32)],
        scratch_shapes=[pltpu.VMEM((2, S, HEAD_DIM), BF16)] * 3,
        compiler_params=_cp(("parallel", "parallel")),
    )(proj, proj, proj, bias)


def _attn_bwd(proj, o, lse, dy, bias, B, S, name):
    T = B * S
    nb, bq, _ = bias.shape
    qcol, kcol, vcol = Q0 // LANE, K0 // LANE, V0 // LANE

    def body(q_ref, k_ref, v_ref, o_ref, lse_ref, do_ref, bias_ref, dq_ref, dk_ref, dv_ref,
             qs, ks, vs, dos, dl, dqs):
        for hh in range(2):
            sl = slice(HEAD_DIM * hh, HEAD_DIM * (hh + 1))
            qs[hh] = (q_ref[:, sl] * 0.125).astype(BF16)
            ks[hh] = k_ref[:, sl].astype(BF16)
            vs[hh] = v_ref[:, sl].astype(BF16)
            do = do_ref[:, sl]
            dos[hh] = do.astype(BF16)
            dl[:, hh:hh + 1] = jnp.sum(do * o_ref[:, sl], axis=-1, keepdims=True)
        dqs[...] = jnp.zeros_like(dqs)

        def kloop(kb, carry):
            k0 = pl.multiple_of(kb * bq, bq)
            kv = [ks[hh, pl.ds(k0, bq), :] for hh in range(2)]
            vv = [vs[hh, pl.ds(k0, bq), :] for hh in range(2)]

            def qloop(qb, c):
                q0 = pl.multiple_of(qb * bq, bq)
                bias_t = bias_ref[qb - kb]
                out = []
                for hh in range(2):
                    dk, dv = c[hh]
                    q = qs[hh, pl.ds(q0, bq), :]
                    do = dos[hh, pl.ds(q0, bq), :]
                    s = _dot_nt(q, kv[hh]) + bias_t
                    p = jnp.exp(s - lse_ref[pl.ds(q0, bq), hh:hh + 1])
                    dv = dv + _dot_tn(p.astype(BF16), do)
                    dp = _dot_nt(do, vv[hh])
                    ds = (p * (dp - dl[pl.ds(q0, bq), hh:hh + 1])).astype(BF16)
                    dk = dk + _dot_tn(ds, q)
                    dqs[hh, pl.ds(q0, bq), :] += _dot(ds, kv[hh])
                    out.append((dk, dv))
                return tuple(out)

            z = jnp.zeros((bq, HEAD_DIM), F32)
            res = lax.fori_loop(kb, nb, qloop, ((z, z), (z, z)))
            for hh in range(2):
                sl = slice(HEAD_DIM * hh, HEAD_DIM * (hh + 1))
                dk_ref[pl.ds(k0, bq), sl] = res[hh][0].astype(dk_ref.dtype)
                dv_ref[pl.ds(k0, bq), sl] = res[hh][1].astype(dv_ref.dtype)
            return carry

        lax.fori_loop(0, nb, kloop, 0)
        for hh in range(2):
            dq_ref[:, HEAD_DIM * hh:HEAD_DIM * (hh + 1)] = (dqs[hh] * 0.125).astype(dq_ref.dtype)

    blk = lambda c0: pl.BlockSpec((S, LANE), lambda b, p: (b, c0 + p))
    own = pl.BlockSpec((S, LANE), lambda b, p: (b, p))
    return pl.pallas_call(
        body, name=name, grid=(B, ATT_W // LANE),
        in_specs=[blk(qcol), blk(kcol), blk(vcol), own,
                  pl.BlockSpec((None, None, S, 2), lambda b, p: (b, p, 0, 0)), own,
                  pl.BlockSpec((nb, bq, bq), lambda b, p: (0, 0, 0))],
        out_specs=[own, own, own],
        out_shape=[jax.ShapeDtypeStruct((T, ATT_W), BF16)] * 3,
        scratch_shapes=[pltpu.VMEM((2, S, HEAD_DIM), BF16)] * 4
        + [pltpu.VMEM((S, 2), F32), pltpu.VMEM((2, S, HEAD_DIM), F32)],
        compiler_params=_cp(("parallel", "parallel")),
    )(proj, proj, proj, o, lse, dy, bias)


def _conv_fwd(proj, cw, cb, B, S, name):
    T = B * S
    nc = SSD_CONV_DIM // LANE
    c0 = XBC0 // LANE

    def body(x_ref, w_ref, b_ref, o_ref):
        x = x_ref[...]
        t = lax.broadcasted_iota(jnp.int32, (S, 1), 0)
        acc = b_ref[...] + w_ref[SSD_CONV - 1:SSD_CONV, :] * x
        for k in range(SSD_CONV - 1):
            sh = SSD_CONV - 1 - k
            xs = jnp.where(t >= sh, pltpu.roll(x, sh, 0), 0.0)
            acc = acc + w_ref[k:k + 1, :] * xs
        o_ref[...] = acc

    return pl.pallas_call(
        body, name=name, grid=(B, nc),
        in_specs=[pl.BlockSpec((S, LANE), lambda b, j: (b, c0 + j)),
                  pl.BlockSpec((SUBLANE, LANE), lambda b, j: (0, j)),
                  pl.BlockSpec((1, LANE), lambda b, j: (0, j))],
        out_specs=pl.BlockSpec((S, LANE), lambda b, j: (b, j)),
        out_shape=jax.ShapeDtypeStruct((T, SSD_CONV_DIM), F32),
        compiler_params=_cp(("parallel", "parallel")),
    )(proj, cw, cb)


def _conv_bwd(dpre, proj, cw, B, S, name):
    T = B * S
    nc = SSD_CONV_DIM // LANE
    c0 = XBC0 // LANE

    def body(d_ref, x_ref, w_ref, dx_ref, dwb_ref):
        @pl.when(pl.program_id(1) == 0)
        def _():
            dwb_ref[...] = jnp.zeros_like(dwb_ref)

        d = d_ref[...]
        x = x_ref[...]
        t = lax.broadcasted_iota(jnp.int32, (S, 1), 0)
        dx = w_ref[SSD_CONV - 1:SSD_CONV, :] * d
        rows = [None] * SUBLANE
        rows[SSD_CONV - 1] = jnp.sum(d * x, axis=0, keepdims=True)
        for k in range(SSD_CONV - 1):
            sh = SSD_CONV - 1 - k
            dx = dx + w_ref[k:k + 1, :] * jnp.where(t < S - sh, pltpu.roll(d, S - sh, 0), 0.0)
            xs = jnp.where(t >= sh, pltpu.roll(x, sh, 0), 0.0)
            rows[k] = jnp.sum(d * xs, axis=0, keepdims=True)
        rows[SSD_CONV] = jnp.sum(d, axis=0, keepdims=True)
        dx_ref[...] = dx.astype(BF16)
        r = lax.broadcasted_iota(jnp.int32, (SUBLANE, LANE), 0)
        upd = jnp.zeros((SUBLANE, LANE), F32)
        for k in range(SSD_CONV + 1):
            upd = upd + jnp.where(r == k, rows[k], 0.0)
        dwb_ref[...] += upd

    return pl.pallas_call(
        body, name=name, grid=(nc, B),
        in_specs=[pl.BlockSpec((S, LANE), lambda j, b: (b, j)),
                  pl.BlockSpec((S, LANE), lambda j, b: (b, c0 + j)),
                  pl.BlockSpec((SUBLANE, LANE), lambda j, b: (0, j))],
        out_specs=[pl.BlockSpec((S, LANE), lambda j, b: (b, j)),
                   pl.BlockSpec((SUBLANE, LANE), lambda j, b: (0, j))],
        out_shape=[jax.ShapeDtypeStruct((T, SSD_CONV_DIM), BF16),
                   jax.ShapeDtypeStruct((SUBLANE, SSD_CONV_DIM), F32)],
        compiler_params=_cp(("parallel", "arbitrary")),
    )(dpre, proj, cw)


def _ssd_consts():
    e = np.zeros((LANE, SSD_W), np.float32)
    p = np.zeros((SUBLANE, SSD_W), np.float32)
    for h in range(SSD_HEADS):
        e[h, HEAD_DIM * h:HEAD_DIM * (h + 1)] = 1.0
        p[h, HEAD_DIM * h] = 1.0
    return jnp.asarray(e), jnp.asarray(p)


def _ssd_chunk(pre, z, dtr, sprev, par, e_mat, psel):
    L = CHUNK
    xc = _silu(pre)
    xs, bm, cm = xc[:, :SSD_W], xc[:, SSD_W:SSD_W + 2 * SSD_STATE], xc[:, SSD_W + 2 * SSD_STATE:]
    dtb, alog, dskip, ng = par[0:1], par[1:2], par[2:3], par[3:4]
    dt = _softplus(_dot(dtr, e_mat, HI) + dtb)
    a = dt * (-jnp.exp(alog))
    X = xs * dt
    ri = lax.broadcasted_iota(jnp.int32, (L, L), 0)
    ci = lax.broadcasted_iota(jnp.int32, (L, L), 1)
    tril = ri >= ci
    acs = _dot(tril.astype(F32), a, HI)
    acs_t = _dot_nt(psel, acs, HI)
    ecs = jnp.exp(acs)
    alast = acs[L - 1:L, :]
    xd = (X * jnp.exp(alast - acs)).astype(BF16)
    xb = X.astype(BF16)
    col = lax.broadcasted_iota(jnp.int32, (1, SSD_W), 1)
    sb = sprev.astype(BF16)
    y = dskip * xs
    snew = sprev * jnp.exp(alast)
    for g in range(2):
        gmask = (col >= g * (SSD_W // 2)) & (col < (g + 1) * (SSD_W // 2))
        bg = bm[:, SSD_STATE * g:SSD_STATE * (g + 1)].astype(BF16)
        cg = cm[:, SSD_STATE * g:SSD_STATE * (g + 1)].astype(BF16)
        cb = _dot_nt(cg, bg)
        for j in range(3):
            h = 3 * g + j
            seg = acs[:, HEAD_DIM * h:HEAD_DIM * h + 1] - acs_t[h:h + 1, :]
            dec = jnp.exp(jnp.where(tril, seg, NEG))
            yh = _dot((cb * dec).astype(BF16), xb)
            hmask = (col >= HEAD_DIM * h) & (col < HEAD_DIM * (h + 1))
            y = y + jnp.where(hmask, yh, 0.0)
        y = y + jnp.where(gmask, _dot(cg, sb) * ecs, 0.0)
        snew = snew + jnp.where(gmask, _dot_tn(bg, xd), 0.0)
    yg = y * _silu(z)
    sq = yg * yg
    g0 = col < SSD_W // 2
    ms0 = jnp.sum(jnp.where(g0, sq, 0.0), axis=-1, keepdims=True) * (2.0 / SSD_W)
    ms1 = jnp.sum(jnp.where(g0, 0.0, sq), axis=-1, keepdims=True) * (2.0 / SSD_W)
    r = jnp.where(g0, lax.rsqrt(ms0 + RMS_EPS), lax.rsqrt(ms1 + RMS_EPS))
    return yg * r * ng, snew


def _ssd_fwd(pre, proj, par, B, S, name):
    T = B * S
    nc = S // CHUNK
    e_mat, psel = _ssd_consts()

    def body(pre_ref, z_ref, dt_ref, par_ref, e_ref, p_ref, y_ref, sall_ref, st):
        @pl.when(pl.program_id(1) == 0)
        def _():
            st[...] = jnp.zeros_like(st)

        sprev = st[...]
        sall_ref[...] = sprev
        y, snew = _ssd_chunk(pre_ref[...], z_ref[...], dt_ref[...], sprev, par_ref[...], e_ref[...], p_ref[...])
        y_ref[...] = y.astype(BF16)
        st[...] = snew

    row = lambda b, c: b * nc + c
    full = lambda shp: pl.BlockSpec(shp, lambda b, c: (0, 0))
    return pl.pallas_call(
        body, name=name, grid=(B, nc),
        in_specs=[pl.BlockSpec((CHUNK, SSD_CONV_DIM), lambda b, c: (row(b, c), 0)),
                  pl.BlockSpec((CHUNK, SSD_W), lambda b, c: (row(b, c), Z0 // SSD_W)),
                  pl.BlockSpec((CHUNK, LANE), lambda b, c: (row(b, c), DT0 // LANE)),
                  full((SUBLANE, SSD_W)), full((LANE, SSD_W)), full((SUBLANE, SSD_W))],
        out_specs=[pl.BlockSpec((CHUNK, SSD_W), lambda b, c: (row(b, c), 0)),
                   pl.BlockSpec((None, SSD_STATE, SSD_W), lambda b, c: (row(b, c), 0, 0))],
        out_shape=[jax.ShapeDtypeStruct((T, SSD_W), BF16),
                   jax.ShapeDtypeStruct((B * nc, SSD_STATE, SSD_W), F32)],
        scratch_shapes=[pltpu.VMEM((SSD_STATE, SSD_W), F32)],
        compiler_params=_cp(("parallel", "arbitrary")),
    )(pre, proj, proj, par, e_mat, psel)


def _ssd_bwd(pre, proj, sall, dy, par, B, S, name):
    T = B * S
    nc = S // CHUNK
    e_mat, psel = _ssd_consts()

    def body(pre_ref, z_ref, dt_ref, sall_ref, dy_ref, par_ref, e_ref, p_ref,
             dpre_ref, dz_ref, ddt_ref, dpar_ref, ds):
        b, c = pl.program_id(0), pl.program_id(1)

        @pl.when(c == 0)
        def _():
            ds[...] = jnp.zeros_like(ds)

        @pl.when((b == 0) & (c == 0))
        def _():
            dpar_ref[...] = jnp.zeros_like(dpar_ref)

        e_v, p_v = e_ref[...], p_ref[...]
        fn = lambda pre, z, dtr, sprev, par: _ssd_chunk(pre, z, dtr, sprev, par, e_v, p_v)
        _, vjp = jax.vjp(fn, pre_ref[...], z_ref[...], dt_ref[...], sall_ref[...], par_ref[...])
        dpre, dz, ddt, dsp, dpar = vjp((dy_ref[...], ds[...]))
        dpre_ref[...] = dpre
        dz_ref[...] = dz.astype(BF16)
        ddt_ref[...] = ddt.astype(BF16)
        dpar_ref[...] += dpar
        ds[...] = dsp

    row = lambda b, c: b * nc + (nc - 1 - c)
    full = lambda shp: pl.BlockSpec(shp, lambda b, c: (0, 0))
    return pl.pallas_call(
        body, name=name, grid=(B, nc),
        in_specs=[pl.BlockSpec((CHUNK, SSD_CONV_DIM), lambda b, c: (row(b, c), 0)),
                  pl.BlockSpec((CHUNK, SSD_W), lambda b, c: (row(b, c), Z0 // SSD_W)),
                  pl.BlockSpec((CHUNK, LANE), lambda b, c: (row(b, c), DT0 // LANE)),
                  pl.BlockSpec((None, SSD_STATE, SSD_W), lambda b, c: (row(b, c), 0, 0)),
                  pl.BlockSpec((CHUNK, SSD_W), lambda b, c: (row(b, c), ATT_W // SSD_W)),
                  full((SUBLANE, SSD_W)), full((LANE, SSD_W)), full((SUBLANE, SSD_W))],
        out_specs=[pl.BlockSpec((CHUNK, SSD_CONV_DIM), lambda b, c: (row(b, c), 0)),
                   pl.BlockSpec((CHUNK, SSD_W), lambda b, c: (row(b, c), 0)),
                   pl.BlockSpec((CHUNK, LANE), lambda b, c: (row(b, c), 0)),
                   full((SUBLANE, SSD_W))],
        out_shape=[jax.ShapeDtypeStruct((T, SSD_CONV_DIM), F32),
                   jax.ShapeDtypeStruct((T, SSD_W), BF16),
                   jax.ShapeDtypeStruct((T, LANE), BF16),
                   jax.ShapeDtypeStruct((SUBLANE, SSD_W), F32)],
        scratch_shapes=[pltpu.VMEM((SSD_STATE, SSD_W), F32)],
        compiler_params=_cp(("arbitrary", "arbitrary")),
    )(pre, proj, proj, sall, dy, par, e_mat, psel)


def _sgu_consts():
    e = np.zeros((SUBLANE, SGU_W), np.float32)
    for g in range(SGU_GROUPS):
        e[g, HEAD_DIM * g:HEAD_DIM * (g + 1)] = 1.0
    return jnp.asarray(e)


def _sgu_chunk(u_raw, v_raw, ln, w, bst, e4):
    L = CHUNK
    u = _gelu(u_raw)
    v = _gelu(v_raw)
    mu = jnp.mean(v, axis=-1, keepdims=True)
    vc = v - mu
    var = jnp.mean(vc * vc, axis=-1, keepdims=True)
    vn = vc * lax.rsqrt(var + LN_EPS) * ln[0:1] + ln[1:2]
    vb = vn.astype(BF16)
    ri = lax.broadcasted_iota(jnp.int32, (L, L), 0)
    ci = lax.broadcasted_iota(jnp.int32, (L, L), 1)
    tril = ri >= ci
    col = lax.broadcasted_iota(jnp.int32, (1, SGU_W), 1)
    mixed = _dot(bst, e4, HI)
    for g in range(SGU_GROUPS):
        wc = jnp.where(tril, w[g], 0.0).astype(BF16)
        gm = (col >= HEAD_DIM * g) & (col < HEAD_DIM * (g + 1))
        mixed = mixed + jnp.where(gm, _dot(wc, vb), 0.0)
    return u * mixed


def _sgu_fwd(proj, ln, w, bst, B, S, name):
    T = B * S
    nc = S // CHUNK
    e4 = _sgu_consts()

    def body(u_ref, v_ref, ln_ref, w_ref, b_ref, e_ref, y_ref):
        y_ref[...] = _sgu_chunk(u_ref[...], v_ref[...], ln_ref[...], w_ref[...], b_ref[...], e_ref[...]).astype(BF16)

    return pl.pallas_call(
        body, name=name, grid=(T // CHUNK,),
        in_specs=[pl.BlockSpec((CHUNK, SGU_W), lambda i: (i, U0 // SGU_W)),
                  pl.BlockSpec((CHUNK, SGU_W), lambda i: (i, VS0 // SGU_W)),
                  pl.BlockSpec((SUBLANE, SGU_W), lambda i: (0, 0)),
                  pl.BlockSpec((SGU_GROUPS, CHUNK, CHUNK), lambda i: (0, 0, 0)),
                  pl.BlockSpec((CHUNK, SUBLANE), lambda i: (0, 0)),
                  pl.BlockSpec((SUBLANE, SGU_W), lambda i: (0, 0))],
        out_specs=pl.BlockSpec((CHUNK, SGU_W), lambda i: (i, 0)),
        out_shape=jax.ShapeDtypeStruct((T, SGU_W), BF16),
        compiler_params=_cp(("parallel",)),
    )(proj, proj, ln, w, bst, e4)


def _sgu_bwd(proj, dy, ln, w, bst, B, S, name):
    T = B * S
    e4 = _sgu_consts()
    ycol = (ATT_W + SSD_W) // SGU_W

    def body(u_ref, v_ref, dy_ref, ln_ref, w_ref, b_ref, e_ref, du_ref, dv_ref, dln_ref, dw_ref, db_ref):
        @pl.when(pl.program_id(0) == 0)
        def _():
            dln_ref[...] = jnp.zeros_like(dln_ref)
            dw_ref[...] = jnp.zeros_like(dw_ref)
            db_ref[...] = jnp.zeros_like(db_ref)

        e_v = e_ref[...]
        fn = lambda u, v, ln, w, b: _sgu_chunk(u, v, ln, w, b, e_v)
        _, vjp = jax.vjp(fn, u_ref[...], v_ref[...], ln_ref[...], w_ref[...], b_ref[...])
        du, dv, dln, dw, db = vjp(dy_ref[...])
        du_ref[...] = du.astype(BF16)
        dv_ref[...] = dv.astype(BF16)
        dln_ref[...] += dln
        dw_ref[...] += dw
        db_ref[...] += db

    c_ln = pl.BlockSpec((SUBLANE, SGU_W), lambda i: (0, 0))
    c_w = pl.BlockSpec((SGU_GROUPS, CHUNK, CHUNK), lambda i: (0, 0, 0))
    c_b = pl.BlockSpec((CHUNK, SUBLANE), lambda i: (0, 0))
    return pl.pallas_call(
        body, name=name, grid=(T // CHUNK,),
        in_specs=[pl.BlockSpec((CHUNK, SGU_W), lambda i: (i, U0 // SGU_W)),
                  pl.BlockSpec((CHUNK, SGU_W), lambda i: (i, VS0 // SGU_W)),
                  pl.BlockSpec((CHUNK, SGU_W), lambda i: (i, ycol)),
                  c_ln, c_w, c_b, pl.BlockSpec((SUBLANE, SGU_W), lambda i: (0, 0))],
        out_specs=[pl.BlockSpec((CHUNK, SGU_W), lambda i: (i, 0)),
                   pl.BlockSpec((CHUNK, SGU_W), lambda i: (i, 0)), c_ln, c_w, c_b],
        out_shape=[jax.ShapeDtypeStruct((T, SGU_W), BF16), jax.ShapeDtypeStruct((T, SGU_W), BF16),
                   jax.ShapeDtypeStruct((SUBLANE, SGU_W), F32),
                   jax.ShapeDtypeStruct((SGU_GROUPS, CHUNK, CHUNK), F32),
                   jax.ShapeDtypeStruct((CHUNK, SUBLANE), F32)],
        compiler_params=_cp(("arbitrary",)),
    )(proj, proj, dy, ln, w, bst, e4)


def _exchange(xs, a2a, name):
    n = len(xs)

    def body(*refs):
        ins, outs = refs[:n], refs[n:2 * n]
        send_sems, recv_sems, loc_sems = refs[2 * n:]
        x, y, c = lax.axis_index("x"), lax.axis_index("y"), lax.axis_index("c")
        me = 4 * x + 2 * y + c
        local = []
        for t in range(n):
            src = ins[t].at[me] if a2a[t] else ins[t]
            cp = pltpu.make_async_copy(src, outs[t].at[me], loc_sems.at[t])
            cp.start()
            local.append(cp)
        sent = []
        for p in range(1, N_DEV):
            px, py, pc = x ^ ((p >> 2) & 1), y ^ ((p >> 1) & 1), c ^ (p & 1)
            peer = 4 * px + 2 * py + pc
            for t in range(n):
                src = ins[t].at[peer] if a2a[t] else ins[t]
                cp = pltpu.make_async_remote_copy(
                    src_ref=src, dst_ref=outs[t].at[me],
                    send_sem=send_sems.at[p - 1, t], recv_sem=recv_sems.at[p - 1, t],
                    device_id=(px, py, pc), device_id_type=MESH)
                cp.start()
                landed = pltpu.make_async_remote_copy(
                    src_ref=src, dst_ref=outs[t].at[peer],
                    send_sem=send_sems.at[p - 1, t], recv_sem=recv_sems.at[p - 1, t],
                    device_id=(px, py, pc), device_id_type=MESH)
                sent.append((cp, landed))
        for cp, landed in sent:
            landed.wait_recv()
        for cp, landed in sent:
            cp.wait_send()
        for cp in local:
            cp.wait()

    out_shape = [jax.ShapeDtypeStruct(a.shape if f else (N_DEV,) + a.shape, a.dtype) for a, f in zip(xs, a2a)]
    anyspec = pl.BlockSpec(memory_space=pl.ANY)
    return pl.pallas_call(
        body, name=name,
        in_specs=[anyspec] * n, out_specs=[anyspec] * n, out_shape=out_shape,
        scratch_shapes=[pltpu.SemaphoreType.DMA((N_DEV - 1, n)), pltpu.SemaphoreType.DMA((N_DEV - 1, n)),
                        pltpu.SemaphoreType.DMA((n,))],
    )(*xs)


def _adamw(w, m, v, parts, name):
    R, C = w.shape
    P = parts.shape[0]
    tr = R
    t = 16
    while t <= R:
        if R % t == 0 and t * C <= 131072:
            tr = t
        t += 16
    if tr == R and R * C > 131072 and R % 16 == 0:
        tr = 16

    def body(w_ref, m_ref, v_ref, p_ref, g_ref, d_ref, mo_ref, vo_ref):
        g = p_ref[0].astype(F32)
        for p in range(1, P):
            g = g + p_ref[p].astype(F32)
        mn = ADAM_B1 * m_ref[...] + (1.0 - ADAM_B1) * g
        vn = ADAM_B2 * v_ref[...] + (1.0 - ADAM_B2) * (g * g)
        m_hat = mn / (1.0 - ADAM_B1 ** ADAM_STEP)
        v_hat = vn / (1.0 - ADAM_B2 ** ADAM_STEP)
        g_ref[...] = g
        d_ref[...] = -ADAM_LR * (m_hat / (jnp.sqrt(v_hat) + ADAM_EPS) + ADAM_WD * w_ref[...])
        mo_ref[...] = mn
        vo_ref[...] = vn

    blk = pl.BlockSpec((tr, C), lambda i: (i, 0))
    return pl.pallas_call(
        body, name=name, grid=(R // tr,),
        in_specs=[blk, blk, blk, pl.BlockSpec((P, tr, C), lambda i: (0, i, 0))],
        out_specs=[blk] * 4, out_shape=[jax.ShapeDtypeStruct((R, C), F32)] * 4,
        compiler_params=_cp(("parallel",)),
    )(w, m, v, parts)


def _perm_cols(w):
    pad = jnp.zeros((w.shape[0], LANE - SSD_HEADS), w.dtype)
    return jnp.concatenate([w[:, 0:1536], w[:, 2438:2694], w[:, 1536:2432], w[:, 2432:2438], pad,
                            w[:, 2694:2950]], axis=1)


def _unperm_cols(w):
    return jnp.concatenate([w[:, 0:1536], w[:, XBC0:XBC0 + SSD_CONV_DIM], w[:, DT0:DT0 + SSD_HEADS],
                            w[:, U0:U0 + SGU_W], w[:, VS0:VS0 + SGU_W]], axis=1)


_SMALL = ("ffn1_norm", "mix_norm", "conv_w", "conv_b", "dt_bias", "a_log", "d_skip", "ssd_norm",
          "sgu_ln_g", "sgu_ln_b", "sgu_w", "sgu_b", "ffn2_norm", "final_norm", "loss")


def _pack(d):
    v = jnp.concatenate([d[k].astype(F32).reshape(-1) for k in _SMALL])
    n = v.shape[0]
    npad = -(-n // (LANE * 16)) * (LANE * 16)
    return jnp.pad(v, (0, npad - n)).reshape(npad // LANE, LANE)


def _unpack(p, shapes):
    v = p.reshape(-1)
    out, o = {}, 0
    for k in _SMALL:
        n = int(np.prod(shapes[k]))
        out[k] = v[o:o + n].reshape(shapes[k])
        o += n
    return out


def kernel(x, ffn1_norm, ffn1_w_gate, ffn1_w_up, ffn1_w_down, mix_norm, w_in, conv_w, conv_b, dt_bias, a_log, d_skip, ssd_norm, sgu_ln_g, sgu_ln_b, sgu_w, sgu_b, w_out, ffn2_norm, ffn2_w_gate, ffn2_w_up, ffn2_w_down, final_norm, loss_target, m_ffn1_norm, m_ffn1_w_gate, m_ffn1_w_up, m_ffn1_w_down, m_mix_norm, m_w_in, m_conv_w, m_conv_b, m_dt_bias, m_a_log, m_d_skip, m_ssd_norm, m_sgu_ln_g, m_sgu_ln_b, m_sgu_w, m_sgu_b, m_w_out, m_ffn2_norm, m_ffn2_w_gate, m_ffn2_w_up, m_ffn2_w_down, m_final_norm, v_ffn1_norm, v_ffn1_w_gate, v_ffn1_w_up, v_ffn1_w_down, v_mix_norm, v_w_in, v_conv_w, v_conv_b, v_dt_bias, v_a_log, v_d_skip, v_ssd_norm, v_sgu_ln_g, v_sgu_ln_b, v_sgu_w, v_sgu_b, v_w_out, v_ffn2_norm, v_ffn2_w_gate, v_ffn2_w_up, v_ffn2_w_down, v_final_norm):
    B, S, D = x.shape
    T = B * S
    L = ffn1_norm.shape[0]
    me = 4 * lax.axis_index("x") + 2 * lax.axis_index("y") + lax.axis_index("c")
    cs = conv_w.shape[2]
    W = dict(ffn1_norm=ffn1_norm, ffn1_w_gate=ffn1_w_gate, ffn1_w_up=ffn1_w_up, ffn1_w_down=ffn1_w_down,
             mix_norm=mix_norm, w_in=w_in, conv_w=conv_w, conv_b=conv_b, dt_bias=dt_bias, a_log=a_log,
             d_skip=d_skip, ssd_norm=ssd_norm, sgu_ln_g=sgu_ln_g, sgu_ln_b=sgu_ln_b, sgu_w=sgu_w, sgu_b=sgu_b,
             w_out=w_out, ffn2_norm=ffn2_norm, ffn2_w_gate=ffn2_w_gate, ffn2_w_up=ffn2_w_up,
             ffn2_w_down=ffn2_w_down, final_norm=final_norm)
    M = dict(ffn1_norm=m_ffn1_norm, ffn1_w_gate=m_ffn1_w_gate, ffn1_w_up=m_ffn1_w_up, ffn1_w_down=m_ffn1_w_down,
             mix_norm=m_mix_norm, w_in=m_w_in, conv_w=m_conv_w, conv_b=m_conv_b, dt_bias=m_dt_bias, a_log=m_a_log,
             d_skip=m_d_skip, ssd_norm=m_ssd_norm, sgu_ln_g=m_sgu_ln_g, sgu_ln_b=m_sgu_ln_b, sgu_w=m_sgu_w,
             sgu_b=m_sgu_b, w_out=m_w_out, ffn2_norm=m_ffn2_norm, ffn2_w_gate=m_ffn2_w_gate,
             ffn2_w_up=m_ffn2_w_up, ffn2_w_down=m_ffn2_w_down, final_norm=m_final_norm)
    V = dict(ffn1_norm=v_ffn1_norm, ffn1_w_gate=v_ffn1_w_gate, ffn1_w_up=v_ffn1_w_up, ffn1_w_down=v_ffn1_w_down,
             mix_norm=v_mix_norm, w_in=v_w_in, conv_w=v_conv_w, conv_b=v_conv_b, dt_bias=v_dt_bias, a_log=v_a_log,
             d_skip=v_d_skip, ssd_norm=v_ssd_norm, sgu_ln_g=v_sgu_ln_g, sgu_ln_b=v_sgu_ln_b, sgu_w=v_sgu_w,
             sgu_b=v_sgu_b, w_out=v_w_out, ffn2_norm=v_ffn2_norm, ffn2_w_gate=v_ffn2_w_gate,
             ffn2_w_up=v_ffn2_w_up, ffn2_w_down=v_ffn2_w_down, final_norm=v_final_norm)
    big = ("ffn1_w_gate", "ffn1_w_up", "ffn1_w_down", "w_in", "w_out", "ffn2_w_gate", "ffn2_w_up", "ffn2_w_down")

    gathered = _exchange([W[k].astype(BF16) for k in big] + [conv_w], [False] * (len(big) + 1), "gather_weights")
    G = dict(zip(big + ("conv_w",), gathered))

    def cols(k, l):
        a = G[k][:, l]
        return jnp.transpose(a, (1, 0, 2)).reshape(a.shape[1], -1)

    def rows(k, l):
        a = G[k][:, l]
        return a.reshape(-1, a.shape[-1])

    bias = _attn_bias(S, min(256, S))
    row1 = lambda a: a.reshape(1, -1)
    lay = []
    for l in range(L):
        cw = jnp.transpose(G["conv_w"][:, l], (1, 0, 2)).reshape(SSD_CONV, -1)
        lay.append(dict(
            g1=row1(ffn1_norm[l]), wg1=cols("ffn1_w_gate", l), wu1=cols("ffn1_w_up", l), wd1=rows("ffn1_w_down", l),
            gm=row1(mix_norm[l]), win=_perm_cols(rows("w_in", l)), wout=rows("w_out", l),
            g2=row1(ffn2_norm[l]), wg2=cols("ffn2_w_gate", l), wu2=cols("ffn2_w_up", l), wd2=rows("ffn2_w_down", l),
            cw=jnp.pad(cw, ((0, SUBLANE - SSD_CONV), (0, 0))), cb=row1(conv_b[l]),
            par=jnp.pad(jnp.stack([jnp.repeat(dt_bias[l], HEAD_DIM), jnp.repeat(a_log[l], HEAD_DIM),
                                   jnp.repeat(d_skip[l], HEAD_DIM), ssd_norm[l]]), ((0, SUBLANE - 4), (0, 0))),
            ln=jnp.pad(jnp.stack([sgu_ln_g[l], sgu_ln_b[l]]), ((0, SUBLANE - 2), (0, 0))),
            sw=sgu_w[l], bst=jnp.pad(sgu_b[l].T, ((0, 0), (0, SUBLANE - SGU_GROUPS)))))

    xc = x.reshape(T, D)
    saved = []
    for l in range(L):
        p = lay[l]
        x1, gate1, up1 = _ffn_fwd(xc, p["g1"], p["wg1"], p["wu1"], p["wd1"], f"ffn1_fwd_{l}")
        proj, ht = _norm_mm(x1, p["gm"], p["win"], f"in_proj_{l}")
        o_att, lse = _attn_fwd(proj, bias, B, S, f"attn_fwd_{l}")
        pre = _conv_fwd(proj, p["cw"], p["cb"], B, S, f"conv_fwd_{l}")
        y_ssd, sall = _ssd_fwd(pre, proj, p["par"], B, S, f"ssd_fwd_{l}")
        y_sgu = _sgu_fwd(proj, p["ln"], p["sw"], p["bst"], B, S, f"sgu_fwd_{l}")
        ycat = jnp.concatenate([o_att.astype(BF16), y_ssd, y_sgu], axis=1)
        x2 = _mm(ycat, p["wout"], "nn", f"out_proj_{l}", residual=x1)
        x3, gate2, up2 = _ffn_fwd(x2, p["g2"], p["wg2"], p["wu2"], p["wd2"], f"ffn2_fwd_{l}")
        saved.append(dict(x0=xc, gate1=gate1, up1=up1, x1=x1, ht=ht, proj=proj, o_att=o_att, lse=lse, pre=pre,
                          sall=sall, ycat=ycat, x2=x2, gate2=gate2, up2=up2))
        xc = x3
    loss_part, dx, dgf = _final_loss(xc, row1(final_norm), loss_target.reshape(T, D), "final_loss")

    gl = [dict() for _ in range(L)]
    for l in reversed(range(L)):
        p, s, g = lay[l], saved[l], gl[l]
        dx2, dgate, dup, act, xnt, dacct, g["ffn2_norm"] = _ffn_bwd_dx(
            dx, s["x2"], p["g2"], s["gate2"], s["up2"], p["wg2"], p["wu2"], p["wd2"], f"ffn2_bwd_{l}")
        g["ffn2_w_gate"], g["ffn2_w_up"], g["ffn2_w_down"] = _ffn_dw(xnt, dacct, dgate, dup, act, f"ffn2_dw_{l}")
        dycat = _mm(dx2, p["wout"], "nt", f"out_proj_dx_{l}")
        g["w_out"] = _mm(s["ycat"], dx2, "tn", f"out_proj_dw_{l}", tm_cap=1024, tk_cap=512)
        dq, dk, dv = _attn_bwd(s["proj"], s["o_att"], s["lse"], dycat, bias, B, S, f"attn_bwd_{l}")
        dpre, dz, ddt, dpar = _ssd_bwd(s["pre"], s["proj"], s["sall"], dycat, p["par"], B, S, f"ssd_bwd_{l}")
        dxbc, dwb = _conv_bwd(dpre, s["proj"], p["cw"], B, S, f"conv_bwd_{l}")
        du, dvs, dln, dsw, dbst = _sgu_bwd(s["proj"], dycat, p["ln"], p["sw"], p["bst"], B, S, f"sgu_bwd_{l}")
        dproj = jnp.concatenate([dq, dk, dv, dz, du, dxbc, ddt, dvs], axis=1)
        g["w_in"] = _unperm_cols(_mm_resident_lhs(s["ht"], dproj, f"in_proj_dw_{l}"))
        dx1, g["mix_norm"] = _norm_mm_bwd(dproj, s["x1"], p["gm"], p["win"], dx2, f"in_proj_bwd_{l}")
        dx, dgate, dup, act, xnt, dacct, g["ffn1_norm"] = _ffn_bwd_dx(
            dx1, s["x0"], p["g1"], s["gate1"], s["up1"], p["wg1"], p["wu1"], p["wd1"], f"ffn1_bwd_{l}")
        g["ffn1_w_gate"], g["ffn1_w_up"], g["ffn1_w_down"] = _ffn_dw(xnt, dacct, dgate, dup, act, f"ffn1_dw_{l}")
        hsum = lambda r: r.reshape(SSD_HEADS, HEAD_DIM).sum(-1)
        g["conv_w"], g["conv_b"] = dwb[:SSD_CONV], dwb[SSD_CONV]
        g["dt_bias"], g["a_log"], g["d_skip"], g["ssd_norm"] = hsum(dpar[0]), hsum(dpar[1]), hsum(dpar[2]), dpar[3]
        g["sgu_ln_g"], g["sgu_ln_b"], g["sgu_w"], g["sgu_b"] = dln[0], dln[1], dsw, dbst[:, :SGU_GROUPS].T
    grad_x = dx.reshape(B, S, D)

    stack = lambda k: jnp.stack([gl[l][k] for l in range(L)])

    def to_blocks(k):
        a = stack(k)
        if k.endswith("w_gate") or k.endswith("w_up"):
            a = a.reshape(L, a.shape[1], N_DEV, -1).transpose(2, 0, 1, 3)
        elif k.endswith("w_down"):
            a = a.reshape(L, a.shape[1], N_DEV, -1).transpose(2, 0, 3, 1)
        else:
            a = a.reshape(L, N_DEV, -1, a.shape[-1]).transpose(1, 0, 2, 3)
        return a.astype(BF16)

    zero1 = jnp.zeros((1,), F32)
    W["loss"], M["loss"], V["loss"] = zero1, zero1, zero1
    per_layer = lambda k: k not in ("final_norm", "conv_w", "loss")
    small = {k: stack(k) for k in _SMALL if per_layer(k) or k == "conv_w"}
    small["final_norm"], small["loss"] = dgf.reshape(-1), loss_part[0, :1]
    small = {k: small[k].reshape((L,) + W[k].shape[1:]) if per_layer(k) else small[k] for k in _SMALL}
    full_shapes = {k: (W[k].shape if k != "conv_w" else (L, SSD_CONV, SSD_CONV_DIM)) for k in _SMALL}
    parts = _exchange([to_blocks(k) for k in big] + [_pack(small)], [True] * len(big) + [False], "exchange_grads")

    grads, deltas, new_m, new_v = {}, {}, {}, {}
    for k, pk in zip(big, parts[:-1]):
        shp = W[k].shape
        two = lambda a: a.reshape(-1, shp[-1])
        outs = _adamw(two(W[k]), two(M[k]), two(V[k]), pk.reshape(N_DEV, -1, shp[-1]), f"adamw_{k}")
        grads[k], deltas[k], new_m[k], new_v[k] = [o.reshape(shp) for o in outs]

    def embed(a, k):
        if k != "conv_w":
            return a
        return lax.dynamic_update_slice(jnp.zeros(full_shapes[k], F32), a, (0, 0, me * cs))

    outs = _adamw(_pack({k: embed(W[k], k) for k in _SMALL}), _pack({k: embed(M[k], k) for k in _SMALL}),
                  _pack({k: embed(V[k], k) for k in _SMALL}), parts[-1], "adamw_small")
    for d, o in zip((grads, deltas, new_m, new_v), outs):
        u = _unpack(o, full_shapes)
        u["conv_w"] = lax.dynamic_slice(u["conv_w"], (0, 0, me * cs), (L, SSD_CONV, cs))
        d.update(u)

    names = ("ffn1_norm", "ffn1_w_gate", "ffn1_w_up", "ffn1_w_down", "mix_norm", "w_in", "conv_w", "conv_b",
             "dt_bias", "a_log", "d_skip", "ssd_norm", "sgu_ln_g", "sgu_ln_b", "sgu_w", "sgu_b", "w_out",
             "ffn2_norm", "ffn2_w_gate", "ffn2_w_up", "ffn2_w_down", "final_norm")
    loss = grads["loss"][0]
    return (loss, grad_x, *[grads[n] for n in names], *[deltas[n] for n in names],
            *[new_m[n] for n in names], *[new_v[n] for n in names])
```

```python
import functools

import numpy as np
import jax
import jax.numpy as jnp
from jax import lax
from jax.experimental import pallas as pl
from jax.experimental.pallas import tpu as pltpu

F32, BF16 = jnp.float32, jnp.bfloat16
HI = lax.Precision.HIGHEST
MESH = pl.DeviceIdType.MESH
N_DEV = 8
VMEM_LIMIT_BYTES = 56 * 1024 * 1024
LANE, SUBLANE = 128, 8

HEAD_DIM = 64
ATT_W = 384
SSD_W = 384
SSD_HEADS = 6
SSD_STATE = 128
SSD_CONV = 4
CHUNK = 128
SSD_CONV_DIM = 896
SGU_W = 256
SGU_GROUPS = 4
D_IN = 2950
RMS_EPS = 1e-6
LN_EPS = 1e-5
NEG = -1e30

PW = 3072
Q0, K0, V0, Z0, U0, XBC0, DT0, VS0 = 0, 384, 768, 1152, 1536, 1792, 2688, 2816

ADAM_LR, ADAM_B1, ADAM_B2, ADAM_EPS, ADAM_WD, ADAM_STEP = 0.001, 0.9, 0.999, 1e-08, 0.01, 10


def _cp(sem=None):
    return pltpu.CompilerParams(dimension_semantics=sem, vmem_limit_bytes=VMEM_LIMIT_BYTES)


def _tile(n, cap, mult=LANE):
    best = None
    t = mult
    while t <= min(n, cap):
        if n % t == 0:
            best = t
        t += mult
    return best if best is not None else n


def _dot(a, b, prec=None):
    return jnp.dot(a, b, preferred_element_type=F32, precision=prec)


def _dot_nt(a, b, prec=None):
    return lax.dot_general(a, b, (((1,), (1,)), ((), ())), preferred_element_type=F32, precision=prec)


def _dot_tn(a, b, prec=None):
    return lax.dot_general(a, b, (((0,), (0,)), ((), ())), preferred_element_type=F32, precision=prec)


def _sigmoid(x):
    return 1.0 / (1.0 + jnp.exp(-x))


def _silu(x):
    return x * _sigmoid(x)


def _gelu(x):
    return 0.5 * x * (1.0 + lax.erf(x * 0.7071067811865476))


def _softplus(x):
    return jnp.maximum(x, 0.0) + jnp.log(1.0 + jnp.exp(-jnp.abs(x)))


def _rms_fwd(x, g):
    rstd = lax.rsqrt(jnp.mean(x * x, axis=-1, keepdims=True) + RMS_EPS)
    xhat = x * rstd
    return xhat * g, xhat, rstd


def _rms_bwd(dy, xhat, rstd, g):
    dxhat = dy * g
    dx = rstd * (dxhat - xhat * jnp.mean(dxhat * xhat, axis=-1, keepdims=True))
    return dx, dy * xhat


def _resident(shape):
    return pl.BlockSpec(shape, lambda *_: (0,) * len(shape), pipeline_mode=pl.Buffered(1))


def _mm(a, b, mode, name, out_dtype=F32, residual=None, tm_cap=512, tn_cap=1024, tk_cap=1024):
    if mode == "nn":
        (M, K), (_, N) = a.shape, b.shape
    elif mode == "nt":
        (M, K), (N, _) = a.shape, b.shape
    else:
        (K, M), (_, N) = a.shape, b.shape
    tm, tn, tk = _tile(M, tm_cap), _tile(N, tn_cap), _tile(K, tk_cap)
    nk = K // tk
    if mode == "tn":
        a_spec = pl.BlockSpec((tk, tm), lambda i, j, k: (k, i))
    else:
        a_spec = pl.BlockSpec((tm, tk), lambda i, j, k: (i, k))
    if mode == "nt":
        b_spec = pl.BlockSpec((tn, tk), lambda i, j, k: (j, k))
    else:
        b_spec = pl.BlockSpec((tk, tn), lambda i, j, k: (k, j))
    o_spec = pl.BlockSpec((tm, tn), lambda i, j, k: (i, j))
    has_res = residual is not None

    def prod(a_ref, b_ref):
        av = a_ref[...].astype(BF16)
        bv = b_ref[...].astype(BF16)
        if mode == "nn":
            return _dot(av, bv)
        if mode == "nt":
            return _dot_nt(av, bv)
        return _dot_tn(av, bv)

    def body(*refs):
        a_ref, b_ref = refs[:2]
        r_ref = refs[2] if has_res else None
        o_ref = refs[2 + has_res]
        if nk == 1:
            o = prod(a_ref, b_ref)
            if has_res:
                o = r_ref[...] + o
            o_ref[...] = o.astype(out_dtype)
            return
        acc = refs[3 + has_res]
        k = pl.program_id(2)

        @pl.when(k == 0)
        def _():
            acc[...] = jnp.zeros_like(acc)

        acc[...] += prod(a_ref, b_ref)

        @pl.when(k == nk - 1)
        def _():
            o = acc[...]
            if has_res:
                o = r_ref[...] + o
            o_ref[...] = o.astype(out_dtype)

    ins = [a, b] + ([residual] if has_res else [])
    in_specs = [a_spec, b_spec] + ([o_spec] if has_res else [])
    return pl.pallas_call(
        body, name=name, grid=(M // tm, N // tn, nk),
        in_specs=in_specs, out_specs=o_spec,
        out_shape=jax.ShapeDtypeStruct((M, N), out_dtype),
        scratch_shapes=[pltpu.VMEM((tm, tn), F32)] if nk > 1 else [],
        compiler_params=_cp(("parallel", "parallel", "arbitrary")),
    )(*ins)


def _ffn_fwd(x, g, wg, wu, wd, name):
    T, D = x.shape
    F = wg.shape[1]
    tm = _tile(T, 512)

    def body(x_ref, g_ref, wg_ref, wu_ref, wd_ref, out_ref, gate_ref, up_ref):
        xv = x_ref[...]
        xn = _rms_fwd(xv, g_ref[...])[0].astype(BF16)
        gate = _dot(xn, wg_ref[...])
        up = _dot(xn, wu_ref[...])
        gate_ref[...] = gate.astype(BF16)
        up_ref[...] = up.astype(BF16)
        act = (_silu(gate) * up).astype(BF16)
        out_ref[...] = xv + 0.5 * _dot(act, wd_ref[...])

    row = lambda w: pl.BlockSpec((tm, w), lambda i: (i, 0))
    return pl.pallas_call(
        body, name=name, grid=(T // tm,),
        in_specs=[row(D), _resident((1, D)), _resident((D, F)), _resident((D, F)), _resident((F, D))],
        out_specs=[row(D), row(F), row(F)],
        out_shape=[jax.ShapeDtypeStruct((T, D), F32),
                   jax.ShapeDtypeStruct((T, F), BF16),
                   jax.ShapeDtypeStruct((T, F), BF16)],
        compiler_params=_cp(("parallel",)),
    )(x, g, wg, wu, wd)


def _ffn_bwd_dx(dout, x, g, gate, up, wg, wu, wd, name):
    T, D = x.shape
    F = wg.shape[1]
    tm, th = _tile(T, 256), _tile(F, 256)
    nj = F // th

    def body(dout_ref, x_ref, g_ref, gate_ref, up_ref, wg_ref, wu_ref, wd_ref,
             dx_ref, dgate_ref, dup_ref, act_ref, xnt_ref, dacct_ref, dg_ref):
        @pl.when(pl.program_id(0) == 0)
        def _():
            dg_ref[...] = jnp.zeros_like(dg_ref)

        gv = g_ref[...]
        dout_v = dout_ref[...]
        xn, xhat, rstd = _rms_fwd(x_ref[...], gv)
        xnt_ref[...] = xn.T.astype(BF16)
        dacc = 0.5 * dout_v
        dacct_ref[...] = dacc.T.astype(BF16)
        dact = _dot_nt(dacc.astype(BF16), wd_ref[...])
        gt = gate_ref[...].astype(F32)
        u = up_ref[...].astype(F32)
        sig = _sigmoid(gt)
        sl = gt * sig
        dgate = (dact * u * (sig * (1.0 + gt * (1.0 - sig)))).astype(BF16)
        dup = (dact * sl).astype(BF16)
        act = (sl * u).astype(BF16)
        for j in range(nj):
            cs = slice(j * th, (j + 1) * th)
            dgate_ref[j] = dgate[:, cs]
            dup_ref[j] = dup[:, cs]
            act_ref[j] = act[:, cs]
        dxn = _dot_nt(dgate, wg_ref[...]) + _dot_nt(dup, wu_ref[...])
        dx, dgrow = _rms_bwd(dxn, xhat, rstd, gv)
        dx_ref[...] = dout_v + dx
        dg_ref[...] += jnp.sum(dgrow, axis=0, keepdims=True)

    row = lambda w: pl.BlockSpec((tm, w), lambda i: (i, 0))
    tiled = pl.BlockSpec((nj, tm, th), lambda i: (0, i, 0))
    tr = pl.BlockSpec((D, tm), lambda i: (0, i))
    return pl.pallas_call(
        body, name=name, grid=(T // tm,),
        in_specs=[row(D), row(D), _resident((1, D)), row(F), row(F),
                  _resident((D, F)), _resident((D, F)), _resident((F, D))],
        out_specs=[row(D), tiled, tiled, tiled, tr, tr, pl.BlockSpec((1, D), lambda i: (0, 0))],
        out_shape=[jax.ShapeDtypeStruct((T, D), F32)] + [jax.ShapeDtypeStruct((nj, T, th), BF16)] * 3
        + [jax.ShapeDtypeStruct((D, T), BF16)] * 2 + [jax.ShapeDtypeStruct((1, D), F32)],
        compiler_params=_cp(("arbitrary",)),
    )(dout, x, g, gate, up, wg, wu, wd)


def _ffn_dw(xnt, dacct, dgate, dup, act, name):
    D, T = xnt.shape
    nj, _, th = dgate.shape

    def body(xnt_ref, dacct_ref, dgate_ref, dup_ref, act_ref, dwg_ref, dwu_ref, dwdt_ref):
        xv = xnt_ref[...]
        dwg_ref[...] = _dot(xv, dgate_ref[...])
        dwu_ref[...] = _dot(xv, dup_ref[...])
        dwdt_ref[...] = _dot(dacct_ref[...], act_ref[...])

    tile = pl.BlockSpec((None, T, th), lambda j: (j, 0, 0))
    out = pl.BlockSpec((D, th), lambda j: (0, j))
    return pl.pallas_call(
        body, name=name, grid=(nj,),
        in_specs=[_resident((D, T)), _resident((D, T)), tile, tile, tile],
        out_specs=[out, out, out], out_shape=[jax.ShapeDtypeStruct((D, nj * th), F32)] * 3,
        compiler_params=_cp(("parallel",)),
    )(xnt, dacct, dgate, dup, act)


def _norm_mm(x, g, w, name):
    T, D = x.shape
    N = w.shape[1]
    tm = _tile(T, 512)

    def body(x_ref, g_ref, w_ref, o_ref, ht_ref):
        xn = _rms_fwd(x_ref[...], g_ref[...])[0]
        ht_ref[...] = xn.T.astype(BF16)
        o_ref[...] = _dot(xn.astype(BF16), w_ref[...])

    return pl.pallas_call(
        body, name=name, grid=(T // tm,),
        in_specs=[pl.BlockSpec((tm, D), lambda i: (i, 0)), _resident((1, D)), _resident((D, N))],
        out_specs=[pl.BlockSpec((tm, N), lambda i: (i, 0)), pl.BlockSpec((D, tm), lambda i: (0, i))],
        out_shape=[jax.ShapeDtypeStruct((T, N), F32), jax.ShapeDtypeStruct((D, T), BF16)],
        compiler_params=_cp(("parallel",)),
    )(x, g, w)


def _norm_mm_bwd(dproj, x, g, w, dres, name):
    T, D = x.shape
    N = w.shape[1]
    tm = _tile(T, 512)

    def body(dp_ref, x_ref, g_ref, w_ref, dres_ref, dx_ref, dg_ref):
        @pl.when(pl.program_id(0) == 0)
        def _():
            dg_ref[...] = jnp.zeros_like(dg_ref)

        gv = g_ref[...]
        dh = _dot_nt(dp_ref[...], w_ref[...])
        _, xhat, rstd = _rms_fwd(x_ref[...], gv)
        dx, dgrow = _rms_bwd(dh, xhat, rstd, gv)
        dx_ref[...] = dres_ref[...] + dx
        dg_ref[...] += jnp.sum(dgrow, axis=0, keepdims=True)

    row = pl.BlockSpec((tm, D), lambda i: (i, 0))
    one = pl.BlockSpec((1, D), lambda i: (0, 0))
    return pl.pallas_call(
        body, name=name, grid=(T // tm,),
        in_specs=[pl.BlockSpec((tm, N), lambda i: (i, 0)), row, _resident((1, D)), _resident((D, N)), row],
        out_specs=[row, one],
        out_shape=[jax.ShapeDtypeStruct((T, D), F32), jax.ShapeDtypeStruct((1, D), F32)],
        compiler_params=_cp(("arbitrary",)),
    )(dproj, x, g, w, dres)


def _mm_resident_lhs(at, b, name, tn_cap=512):
    M, K = at.shape
    N = b.shape[1]
    tn = _tile(N, tn_cap)

    def body(a_ref, b_ref, o_ref):
        o_ref[...] = _dot(a_ref[...], b_ref[...])

    return pl.pallas_call(
        body, name=name, grid=(N // tn,),
        in_specs=[_resident((M, K)), pl.BlockSpec((K, tn), lambda j: (0, j))],
        out_specs=pl.BlockSpec((M, tn), lambda j: (0, j)),
        out_shape=jax.ShapeDtypeStruct((M, N), F32),
        compiler_params=_cp(("parallel",)),
    )(at, b)


def _final_loss(x, g, target, name):
    T, D = x.shape
    tm = _tile(T, 512)

    def body(x_ref, g_ref, t_ref, loss_ref, dx_ref, dg_ref):
        @pl.when(pl.program_id(0) == 0)
        def _():
            dg_ref[...] = jnp.zeros_like(dg_ref)
            loss_ref[...] = jnp.zeros_like(loss_ref)

        gv = g_ref[...]
        y, xhat, rstd = _rms_fwd(x_ref[...], gv)
        err = y - t_ref[...]
        part = 0.5 * jnp.sum(jnp.mean(err * err, axis=-1, keepdims=True), axis=0, keepdims=True)
        loss_ref[...] += jnp.broadcast_to(part, loss_ref.shape)
        dy = err * (1.0 / D)
        dx, dgrow = _rms_bwd(dy, xhat, rstd, gv)
        dx_ref[...] = dx
        dg_ref[...] += jnp.sum(dgrow, axis=0, keepdims=True)

    row = pl.BlockSpec((tm, D), lambda i: (i, 0))
    one = pl.BlockSpec((1, D), lambda i: (0, 0))
    return pl.pallas_call(
        body, name=name, grid=(T // tm,),
        in_specs=[row, one, row],
        out_specs=[pl.BlockSpec((1, LANE), lambda i: (0, 0)), row, one],
        out_shape=[jax.ShapeDtypeStruct((1, LANE), F32), jax.ShapeDtypeStruct((T, D), F32),
                   jax.ShapeDtypeStruct((1, D), F32)],
        compiler_params=_cp(("arbitrary",)),
    )(x, g, target)


def _attn_bias(S, bq):
    nb = S // bq
    d = (jnp.arange(nb)[:, None, None] * bq + jnp.arange(bq)[None, :, None]
         - jnp.arange(bq)[None, None, :])
    ok = d >= 0
    mult = ((ok & (d <= 128)).astype(F32) + (ok & (d % 4 == 0) & (d <= 512)).astype(F32)
            + (ok & (d % 16 == 0) & (d <= 2048)).astype(F32))
    return jnp.where(mult > 0, jnp.log(jnp.maximum(mult, 1.0)), NEG).astype(F32)


def _attn_fwd(proj, bias, B, S, name):
    T = B * S
    nb, bq, _ = bias.shape
    qcol, kcol, vcol = Q0 // LANE, K0 // LANE, V0 // LANE

    def body(q_ref, k_ref, v_ref, bias_ref, o_ref, lse_ref, qs, ks, vs):
        for hh in range(2):
            sl = slice(HEAD_DIM * hh, HEAD_DIM * (hh + 1))
            qs[hh] = (q_ref[:, sl] * 0.125).astype(BF16)
            ks[hh] = k_ref[:, sl].astype(BF16)
            vs[hh] = v_ref[:, sl].astype(BF16)
        def qloop(qb, carry):
            q0 = pl.multiple_of(qb * bq, bq)
            qv = [qs[hh, pl.ds(q0, bq), :] for hh in range(2)]

            def kloop(kb, c):
                k0 = pl.multiple_of(kb * bq, bq)
                bias_t = bias_ref[qb - kb]
                out = []
                for hh in range(2):
                    m, l, acc = c[hh]
                    s = _dot_nt(qv[hh], ks[hh, pl.ds(k0, bq), :]) + bias_t
                    mn = jnp.maximum(m, jnp.max(s, axis=-1, keepdims=True))
                    p = jnp.exp(s - mn)
                    a = jnp.exp(m - mn)
                    l = a * l + jnp.sum(p, axis=-1, keepdims=True)
                    acc = a * acc + _dot(p.astype(BF16), vs[hh, pl.ds(k0, bq), :])
                    out.append((mn, l, acc))
                return tuple(out)

            one = (jnp.full((bq, 1), NEG, F32), jnp.zeros((bq, 1), F32), jnp.zeros((bq, HEAD_DIM), F32))
            res = lax.fori_loop(0, qb + 1, kloop, (one, one))
            for hh in range(2):
                m, l, acc = res[hh]
                o_ref[pl.ds(q0, bq), HEAD_DIM * hh:HEAD_DIM * (hh + 1)] = acc / l
                lse_ref[pl.ds(q0, bq), hh:hh + 1] = m + jnp.log(l)
            return carry

        lax.fori_loop(0, nb, qloop, 0)

    blk = lambda c0: pl.BlockSpec((S, LANE), lambda b, p: (b, c0 + p))
    return pl.pallas_call(
        body, name=name, grid=(B, ATT_W // LANE),
        in_specs=[blk(qcol), blk(kcol), blk(vcol),
                  pl.BlockSpec((nb, bq, bq), lambda b, p: (0, 0, 0))],
        out_specs=[pl.BlockSpec((S, LANE), lambda b, p: (b, p)),
                   pl.BlockSpec((None, None, S, 2), lambda b, p: (b, p, 0, 0))],
        out_shape=[jax.ShapeDtypeStruct((T, ATT_W), F32),
                   jax.ShapeDtypeStruct((B, ATT_W // LANE, S, 2), F32)],
        scratch_shapes=[pltpu.VMEM((2, S, HEAD_DIM), BF16)] * 3,
        compiler_params=_cp(("parallel", "parallel")),
    )(proj, proj, proj, bias)


def _attn_bwd(proj, o, lse, dy, bias, B, S, name):
    T = B * S
    nb, bq, _ = bias.shape
    qcol, kcol, vcol = Q0 // LANE, K0 // LANE, V0 // LANE

    def body(q_ref, k_ref, v_ref, o_ref, lse_ref, do_ref, bias_ref, dq_ref, dk_ref, dv_ref,
             qs, ks, vs, dos, dl, dqs):
        for hh in range(2):
            sl = slice(HEAD_DIM * hh, HEAD_DIM * (hh + 1))
            qs[hh] = (q_ref[:, sl] * 0.125).astype(BF16)
            ks[hh] = k_ref[:, sl].astype(BF16)
            vs[hh] = v_ref[:, sl].astype(BF16)
            do = do_ref[:, sl]
            dos[hh] = do.astype(BF16)
            dl[:, hh:hh + 1] = jnp.sum(do * o_ref[:, sl], axis=-1, keepdims=True)
        dqs[...] = jnp.zeros_like(dqs)

        def kloop(kb, carry):
            k0 = pl.multiple_of(kb * bq, bq)
            kv = [ks[hh, pl.ds(k0, bq), :] for hh in range(2)]
            vv = [vs[hh, pl.ds(k0, bq), :] for hh in range(2)]

            def qloop(qb, c):
                q0 = pl.multiple_of(qb * bq, bq)
                bias_t = bias_ref[qb - kb]
                out = []
                for hh in range(2):
                    dk, dv = c[hh]
                    q = qs[hh, pl.ds(q0, bq), :]
                    do = dos[hh, pl.ds(q0, bq), :]
                    s = _dot_nt(q, kv[hh]) + bias_t
                    p = jnp.exp(s - lse_ref[pl.ds(q0, bq), hh:hh + 1])
                    dv = dv + _dot_tn(p.astype(BF16), do)
                    dp = _dot_nt(do, vv[hh])
                    ds = (p * (dp - dl[pl.ds(q0, bq), hh:hh + 1])).astype(BF16)
                    dk = dk + _dot_tn(ds, q)
                    dqs[hh, pl.ds(q0, bq), :] += _dot(ds, kv[hh])
                    out.append((dk, dv))
                return tuple(out)

            z = jnp.zeros((bq, HEAD_DIM), F32)
            res = lax.fori_loop(kb, nb, qloop, ((z, z), (z, z)))
            for hh in range(2):
                sl = slice(HEAD_DIM * hh, HEAD_DIM * (hh + 1))
                dk_ref[pl.ds(k0, bq), sl] = res[hh][0].astype(dk_ref.dtype)
                dv_ref[pl.ds(k0, bq), sl] = res[hh][1].astype(dv_ref.dtype)
            return carry

        lax.fori_loop(0, nb, kloop, 0)
        for hh in range(2):
            dq_ref[:, HEAD_DIM * hh:HEAD_DIM * (hh + 1)] = (dqs[hh] * 0.125).astype(dq_ref.dtype)

    blk = lambda c0: pl.BlockSpec((S, LANE), lambda b, p: (b, c0 + p))
    own = pl.BlockSpec((S, LANE), lambda b, p: (b, p))
    return pl.pallas_call(
        body, name=name, grid=(B, ATT_W // LANE),
        in_specs=[blk(qcol), blk(kcol), blk(vcol), own,
                  pl.BlockSpec((None, None, S, 2), lambda b, p: (b, p, 0, 0)), own,
                  pl.BlockSpec((nb, bq, bq), lambda b, p: (0, 0, 0))],
        out_specs=[own, own, own],
        out_shape=[jax.ShapeDtypeStruct((T, ATT_W), BF16)] * 3,
        scratch_shapes=[pltpu.VMEM((2, S, HEAD_DIM), BF16)] * 4
        + [pltpu.VMEM((S, 2), F32), pltpu.VMEM((2, S, HEAD_DIM), F32)],
        compiler_params=_cp(("parallel", "parallel")),
    )(proj, proj, proj, o, lse, dy, bias)


def _conv_fwd(proj, cw, cb, B, S, name):
    T = B * S
    nc = SSD_CONV_DIM // LANE
    c0 = XBC0 // LANE

    def body(x_ref, w_ref, b_ref, o_ref):
        x = x_ref[...]
        t = lax.broadcasted_iota(jnp.int32, (S, 1), 0)
        acc = b_ref[...] + w_ref[SSD_CONV - 1:SSD_CONV, :] * x
        for k in range(SSD_CONV - 1):
            sh = SSD_CONV - 1 - k
            xs = jnp.where(t >= sh, pltpu.roll(x, sh, 0), 0.0)
            acc = acc + w_ref[k:k + 1, :] * xs
        o_ref[...] = acc

    return pl.pallas_call(
        body, name=name, grid=(B, nc),
        in_specs=[pl.BlockSpec((S, LANE), lambda b, j: (b, c0 + j)),
                  pl.BlockSpec((SUBLANE, LANE), lambda b, j: (0, j)),
                  pl.BlockSpec((1, LANE), lambda b, j: (0, j))],
        out_specs=pl.BlockSpec((S, LANE), lambda b, j: (b, j)),
        out_shape=jax.ShapeDtypeStruct((T, SSD_CONV_DIM), F32),
        compiler_params=_cp(("parallel", "parallel")),
    )(proj, cw, cb)


def _conv_bwd(dpre, proj, cw, B, S, name):
    T = B * S
    nc = SSD_CONV_DIM // LANE
    c0 = XBC0 // LANE

    def body(d_ref, x_ref, w_ref, dx_ref, dwb_ref):
        @pl.when(pl.program_id(1) == 0)
        def _():
            dwb_ref[...] = jnp.zeros_like(dwb_ref)

        d = d_ref[...]
        x = x_ref[...]
        t = lax.broadcasted_iota(jnp.int32, (S, 1), 0)
        dx = w_ref[SSD_CONV - 1:SSD_CONV, :] * d
        rows = [None] * SUBLANE
        rows[SSD_CONV - 1] = jnp.sum(d * x, axis=0, keepdims=True)
        for k in range(SSD_CONV - 1):
            sh = SSD_CONV - 1 - k
            dx = dx + w_ref[k:k + 1, :] * jnp.where(t < S - sh, pltpu.roll(d, S - sh, 0), 0.0)
            xs = jnp.where(t >= sh, pltpu.roll(x, sh, 0), 0.0)
            rows[k] = jnp.sum(d * xs, axis=0, keepdims=True)
        rows[SSD_CONV] = jnp.sum(d, axis=0, keepdims=True)
        dx_ref[...] = dx.astype(BF16)
        r = lax.broadcasted_iota(jnp.int32, (SUBLANE, LANE), 0)
        upd = jnp.zeros((SUBLANE, LANE), F32)
        for k in range(SSD_CONV + 1):
            upd = upd + jnp.where(r == k, rows[k], 0.0)
        dwb_ref[...] += upd

    return pl.pallas_call(
        body, name=name, grid=(nc, B),
        in_specs=[pl.BlockSpec((S, LANE), lambda j, b: (b, j)),
                  pl.BlockSpec((S, LANE), lambda j, b: (b, c0 + j)),
                  pl.BlockSpec((SUBLANE, LANE), lambda j, b: (0, j))],
        out_specs=[pl.BlockSpec((S, LANE), lambda j, b: (b, j)),
                   pl.BlockSpec((SUBLANE, LANE), lambda j, b: (0, j))],
        out_shape=[jax.ShapeDtypeStruct((T, SSD_CONV_DIM), BF16),
                   jax.ShapeDtypeStruct((SUBLANE, SSD_CONV_DIM), F32)],
        compiler_params=_cp(("parallel", "arbitrary")),
    )(dpre, proj, cw)


def _ssd_consts():
    e = np.zeros((LANE, SSD_W), np.float32)
    p = np.zeros((SUBLANE, SSD_W), np.float32)
    for h in range(SSD_HEADS):
        e[h, HEAD_DIM * h:HEAD_DIM * (h + 1)] = 1.0
        p[h, HEAD_DIM * h] = 1.0
    return jnp.asarray(e), jnp.asarray(p)


def _ssd_chunk(pre, z, dtr, sprev, par, e_mat, psel):
    L = CHUNK
    xc = _silu(pre)
    xs, bm, cm = xc[:, :SSD_W], xc[:, SSD_W:SSD_W + 2 * SSD_STATE], xc[:, SSD_W + 2 * SSD_STATE:]
    dtb, alog, dskip, ng = par[0:1], par[1:2], par[2:3], par[3:4]
    dt = _softplus(_dot(dtr, e_mat, HI) + dtb)
    a = dt * (-jnp.exp(alog))
    X = xs * dt
    ri = lax.broadcasted_iota(jnp.int32, (L, L), 0)
    ci = lax.broadcasted_iota(jnp.int32, (L, L), 1)
    tril = ri >= ci
    acs = _dot(tril.astype(F32), a, HI)
    acs_t = _dot_nt(psel, acs, HI)
    ecs = jnp.exp(acs)
    alast = acs[L - 1:L, :]
    xd = (X * jnp.exp(alast - acs)).astype(BF16)
    xb = X.astype(BF16)
    col = lax.broadcasted_iota(jnp.int32, (1, SSD_W), 1)
    sb = sprev.astype(BF16)
    y = dskip * xs
    snew = sprev * jnp.exp(alast)
    for g in range(2):
        gmask = (col >= g * (SSD_W // 2)) & (col < (g + 1) * (SSD_W // 2))
        bg = bm[:, SSD_STATE * g:SSD_STATE * (g + 1)].astype(BF16)
        cg = cm[:, SSD_STATE * g:SSD_STATE * (g + 1)].astype(BF16)
        cb = _dot_nt(cg, bg)
        for j in range(3):
            h = 3 * g + j
            seg = acs[:, HEAD_DIM * h:HEAD_DIM * h + 1] - acs_t[h:h + 1, :]
            dec = jnp.exp(jnp.where(tril, seg, NEG))
            yh = _dot((cb * dec).astype(BF16), xb)
            hmask = (col >= HEAD_DIM * h) & (col < HEAD_DIM * (h + 1))
            y = y + jnp.where(hmask, yh, 0.0)
        y = y + jnp.where(gmask, _dot(cg, sb) * ecs, 0.0)
        snew = snew + jnp.where(gmask, _dot_tn(bg, xd), 0.0)
    yg = y * _silu(z)
    sq = yg * yg
    g0 = col < SSD_W // 2
    ms0 = jnp.sum(jnp.where(g0, sq, 0.0), axis=-1, keepdims=True) * (2.0 / SSD_W)
    ms1 = jnp.sum(jnp.where(g0, 0.0, sq), axis=-1, keepdims=True) * (2.0 / SSD_W)
    r = jnp.where(g0, lax.rsqrt(ms0 + RMS_EPS), lax.rsqrt(ms1 + RMS_EPS))
    return yg * r * ng, snew


def _ssd_fwd(pre, proj, par, B, S, name):
    T = B * S
    nc = S // CHUNK
    e_mat, psel = _ssd_consts()

    def body(pre_ref, z_ref, dt_ref, par_ref, e_ref, p_ref, y_ref, sall_ref, st):
        @pl.when(pl.program_id(1) == 0)
        def _():
            st[...] = jnp.zeros_like(st)

        sprev = st[...]
        sall_ref[...] = sprev
        y, snew = _ssd_chunk(pre_ref[...], z_ref[...], dt_ref[...], sprev, par_ref[...], e_ref[...], p_ref[...])
        y_ref[...] = y.astype(BF16)
        st[...] = snew

    row = lambda b, c: b * nc + c
    full = lambda shp: pl.BlockSpec(shp, lambda b, c: (0, 0))
    return pl.pallas_call(
        body, name=name, grid=(B, nc),
        in_specs=[pl.BlockSpec((CHUNK, SSD_CONV_DIM), lambda b, c: (row(b, c), 0)),
                  pl.BlockSpec((CHUNK, SSD_W), lambda b, c: (row(b, c), Z0 // SSD_W)),
                  pl.BlockSpec((CHUNK, LANE), lambda b, c: (row(b, c), DT0 // LANE)),
                  full((SUBLANE, SSD_W)), full((LANE, SSD_W)), full((SUBLANE, SSD_W))],
        out_specs=[pl.BlockSpec((CHUNK, SSD_W), lambda b, c: (row(b, c), 0)),
                   pl.BlockSpec((None, SSD_STATE, SSD_W), lambda b, c: (row(b, c), 0, 0))],
        out_shape=[jax.ShapeDtypeStruct((T, SSD_W), BF16),
                   jax.ShapeDtypeStruct((B * nc, SSD_STATE, SSD_W), F32)],
        scratch_shapes=[pltpu.VMEM((SSD_STATE, SSD_W), F32)],
        compiler_params=_cp(("parallel", "arbitrary")),
    )(pre, proj, proj, par, e_mat, psel)


def _ssd_bwd(pre, proj, sall, dy, par, B, S, name):
    T = B * S
    nc = S // CHUNK
    e_mat, psel = _ssd_consts()

    def body(pre_ref, z_ref, dt_ref, sall_ref, dy_ref, par_ref, e_ref, p_ref,
             dpre_ref, dz_ref, ddt_ref, dpar_ref, ds):
        b, c = pl.program_id(0), pl.program_id(1)

        @pl.when(c == 0)
        def _():
            ds[...] = jnp.zeros_like(ds)

        @pl.when((b == 0) & (c == 0))
        def _():
            dpar_ref[...] = jnp.zeros_like(dpar_ref)

        e_v, p_v = e_ref[...], p_ref[...]
        fn = lambda pre, z, dtr, sprev, par: _ssd_chunk(pre, z, dtr, sprev, par, e_v, p_v)
        _, vjp = jax.vjp(fn, pre_ref[...], z_ref[...], dt_ref[...], sall_ref[...], par_ref[...])
        dpre, dz, ddt, dsp, dpar = vjp((dy_ref[...], ds[...]))
        dpre_ref[...] = dpre
        dz_ref[...] = dz.astype(BF16)
        ddt_ref[...] = ddt.astype(BF16)
        dpar_ref[...] += dpar
        ds[...] = dsp

    row = lambda b, c: b * nc + (nc - 1 - c)
    full = lambda shp: pl.BlockSpec(shp, lambda b, c: (0, 0))
    return pl.pallas_call(
        body, name=name, grid=(B, nc),
        in_specs=[pl.BlockSpec((CHUNK, SSD_CONV_DIM), lambda b, c: (row(b, c), 0)),
                  pl.BlockSpec((CHUNK, SSD_W), lambda b, c: (row(b, c), Z0 // SSD_W)),
                  pl.BlockSpec((CHUNK, LANE), lambda b, c: (row(b, c), DT0 // LANE)),
                  pl.BlockSpec((None, SSD_STATE, SSD_W), lambda b, c: (row(b, c), 0, 0)),
                  pl.BlockSpec((CHUNK, SSD_W), lambda b, c: (row(b, c), ATT_W // SSD_W)),
                  full((SUBLANE, SSD_W)), full((LANE, SSD_W)), full((SUBLANE, SSD_W))],
        out_specs=[pl.BlockSpec((CHUNK, SSD_CONV_DIM), lambda b, c: (row(b, c), 0)),
                   pl.BlockSpec((CHUNK, SSD_W), lambda b, c: (row(b, c), 0)),
                   pl.BlockSpec((CHUNK, LANE), lambda b, c: (row(b, c), 0)),
                   full((SUBLANE, SSD_W))],
        out_shape=[jax.ShapeDtypeStruct((T, SSD_CONV_DIM), F32),
                   jax.ShapeDtypeStruct((T, SSD_W), BF16),
                   jax.ShapeDtypeStruct((T, LANE), BF16),
                   jax.ShapeDtypeStruct((SUBLANE, SSD_W), F32)],
        scratch_shapes=[pltpu.VMEM((SSD_STATE, SSD_W), F32)],
        compiler_params=_cp(("arbitrary", "arbitrary")),
    )(pre, proj, proj, sall, dy, par, e_mat, psel)


def _sgu_consts():
    e = np.zeros((SUBLANE, SGU_W), np.float32)
    for g in range(SGU_GROUPS):
        e[g, HEAD_DIM * g:HEAD_DIM * (g + 1)] = 1.0
    return jnp.asarray(e)


def _sgu_chunk(u_raw, v_raw, ln, w, bst, e4):
    L = CHUNK
    u = _gelu(u_raw)
    v = _gelu(v_raw)
    mu = jnp.mean(v, axis=-1, keepdims=True)
    vc = v - mu
    var = jnp.mean(vc * vc, axis=-1, keepdims=True)
    vn = vc * lax.rsqrt(var + LN_EPS) * ln[0:1] + ln[1:2]
    vb = vn.astype(BF16)
    ri = lax.broadcasted_iota(jnp.int32, (L, L), 0)
    ci = lax.broadcasted_iota(jnp.int32, (L, L), 1)
    tril = ri >= ci
    col = lax.broadcasted_iota(jnp.int32, (1, SGU_W), 1)
    mixed = _dot(bst, e4, HI)
    for g in range(SGU_GROUPS):
        wc = jnp.where(tril, w[g], 0.0).astype(BF16)
        gm = (col >= HEAD_DIM * g) & (col < HEAD_DIM * (g + 1))
        mixed = mixed + jnp.where(gm, _dot(wc, vb), 0.0)
    return u * mixed


def _sgu_fwd(proj, ln, w, bst, B, S, name):
    T = B * S
    nc = S // CHUNK
    e4 = _sgu_consts()

    def body(u_ref, v_ref, ln_ref, w_ref, b_ref, e_ref, y_ref):
        y_ref[...] = _sgu_chunk(u_ref[...], v_ref[...], ln_ref[...], w_ref[...], b_ref[...], e_ref[...]).astype(BF16)

    return pl.pallas_call(
        body, name=name, grid=(T // CHUNK,),
        in_specs=[pl.BlockSpec((CHUNK, SGU_W), lambda i: (i, U0 // SGU_W)),
                  pl.BlockSpec((CHUNK, SGU_W), lambda i: (i, VS0 // SGU_W)),
                  pl.BlockSpec((SUBLANE, SGU_W), lambda i: (0, 0)),
                  pl.BlockSpec((SGU_GROUPS, CHUNK, CHUNK), lambda i: (0, 0, 0)),
                  pl.BlockSpec((CHUNK, SUBLANE), lambda i: (0, 0)),
                  pl.BlockSpec((SUBLANE, SGU_W), lambda i: (0, 0))],
        out_specs=pl.BlockSpec((CHUNK, SGU_W), lambda i: (i, 0)),
        out_shape=jax.ShapeDtypeStruct((T, SGU_W), BF16),
        compiler_params=_cp(("parallel",)),
    )(proj, proj, ln, w, bst, e4)


def _sgu_bwd(proj, dy, ln, w, bst, B, S, name):
    T = B * S
    e4 = _sgu_consts()
    ycol = (ATT_W + SSD_W) // SGU_W

    def body(u_ref, v_ref, dy_ref, ln_ref, w_ref, b_ref, e_ref, du_ref, dv_ref, dln_ref, dw_ref, db_ref):
        @pl.when(pl.program_id(0) == 0)
        def _():
            dln_ref[...] = jnp.zeros_like(dln_ref)
            dw_ref[...] = jnp.zeros_like(dw_ref)
            db_ref[...] = jnp.zeros_like(db_ref)

        e_v = e_ref[...]
        fn = lambda u, v, ln, w, b: _sgu_chunk(u, v, ln, w, b, e_v)
        _, vjp = jax.vjp(fn, u_ref[...], v_ref[...], ln_ref[...], w_ref[...], b_ref[...])
        du, dv, dln, dw, db = vjp(dy_ref[...])
        du_ref[...] = du.astype(BF16)
        dv_ref[...] = dv.astype(BF16)
        dln_ref[...] += dln
        dw_ref[...] += dw
        db_ref[...] += db

    c_ln = pl.BlockSpec((SUBLANE, SGU_W), lambda i: (0, 0))
    c_w = pl.BlockSpec((SGU_GROUPS, CHUNK, CHUNK), lambda i: (0, 0, 0))
    c_b = pl.BlockSpec((CHUNK, SUBLANE), lambda i: (0, 0))
    return pl.pallas_call(
        body, name=name, grid=(T // CHUNK,),
        in_specs=[pl.BlockSpec((CHUNK, SGU_W), lambda i: (i, U0 // SGU_W)),
                  pl.BlockSpec((CHUNK, SGU_W), lambda i: (i, VS0 // SGU_W)),
                  pl.BlockSpec((CHUNK, SGU_W), lambda i: (i, ycol)),
                  c_ln, c_w, c_b, pl.BlockSpec((SUBLANE, SGU_W), lambda i: (0, 0))],
        out_specs=[pl.BlockSpec((CHUNK, SGU_W), lambda i: (i, 0)),
                   pl.BlockSpec((CHUNK, SGU_W), lambda i: (i, 0)), c_ln, c_w, c_b],
        out_shape=[jax.ShapeDtypeStruct((T, SGU_W), BF16), jax.ShapeDtypeStruct((T, SGU_W), BF16),
                   jax.ShapeDtypeStruct((SUBLANE, SGU_W), F32),
                   jax.ShapeDtypeStruct((SGU_GROUPS, CHUNK, CHUNK), F32),
                   jax.ShapeDtypeStruct((CHUNK, SUBLANE), F32)],
        compiler_params=_cp(("arbitrary",)),
    )(proj, proj, dy, ln, w, bst, e4)


_HBM = pl.BlockSpec(memory_space=pltpu.HBM)
_SEM = pl.BlockSpec(memory_space=pltpu.SEMAPHORE)
_ANY = pl.BlockSpec(memory_space=pl.ANY)
_EFFECT = pltpu.SideEffectType.DATAFLOW_SIDE_EFFECTING


def _peers():
    x, y, c = lax.axis_index("x"), lax.axis_index("y"), lax.axis_index("c")
    out = []
    for p in range(1, N_DEV):
        px, py, pc = x ^ ((p >> 2) & 1), y ^ ((p >> 1) & 1), c ^ (p & 1)
        out.append(((px, py, pc), 4 * px + 2 * py + pc))
    return 4 * x + 2 * y + c, out


def _place_local(xs, a2a, name):
    n = len(xs)

    def body(*refs):
        ins, outs, sems = refs[:n], refs[n:2 * n], refs[2 * n]
        me, _ = _peers()
        cps = [pltpu.make_async_copy(ins[t].at[me] if a2a[t] else ins[t], outs[t].at[me], sems.at[t])
               for t in range(n)]
        for cp in cps:
            cp.start()
        for cp in cps:
            cp.wait()

    out_shape = [jax.ShapeDtypeStruct(a.shape if f else (N_DEV,) + a.shape, a.dtype) for a, f in zip(xs, a2a)]
    return pl.pallas_call(
        body, name=name, in_specs=[_ANY] * n, out_specs=[_ANY] * n, out_shape=out_shape,
        scratch_shapes=[pltpu.SemaphoreType.DMA((n,))],
    )(*xs)


def _xchg_start(xs, lands, a2a, order, name):
    n = len(xs)

    def body(*refs):
        ins, zones = refs[:n], refs[n:2 * n]
        send_sems, recv_sems = refs[2 * n + 1], refs[2 * n + 2]
        token = refs[-1]
        me, peers = _peers()
        for p, (dev, peer) in enumerate(peers):
            for t in range(n):
                pltpu.make_async_remote_copy(
                    src_ref=ins[t].at[peer] if a2a[t] else ins[t], dst_ref=zones[t].at[me],
                    send_sem=send_sems.at[p * n + t], recv_sem=recv_sems.at[p * n + t],
                    device_id=dev, device_id_type=MESH).start()
        token[...] = jnp.zeros_like(token)

    hbm = lambda a: pltpu.HBM(a.shape, a.dtype)
    sems = pltpu.SemaphoreType.DMA(((N_DEV - 1) * n,))
    out = pl.pallas_call(
        body, name=name,
        in_specs=[_HBM] * (2 * n) + [_ANY],
        out_specs=[_SEM, _SEM] + [_HBM] * (2 * n) + [pl.BlockSpec(memory_space=pltpu.VMEM)],
        out_shape=[sems, sems] + [hbm(a) for a in xs] + [hbm(a) for a in lands]
        + [jax.ShapeDtypeStruct((SUBLANE, LANE), F32)],
        input_output_aliases={t: 2 + t for t in range(2 * n)},
        compiler_params=pltpu.CompilerParams(has_side_effects=_EFFECT),
    )(*[pltpu.with_memory_space_constraint(a, pltpu.HBM) for a in list(xs) + list(lands)], order)
    return out[0], out[1], out[2:2 + n], out[2 + n:2 + 2 * n], out[-1]


def _xchg_wait(started, a2a, after, name):
    send_sems, recv_sems, xs, lands, _ = started
    n = len(xs)

    def body(*refs):
        ins, zones = refs[:n], refs[n:2 * n]
        send_s, recv_s = refs[2 * n], refs[2 * n + 1]
        me, peers = _peers()
        cps = []
        for p, (dev, peer) in enumerate(peers):
            for t in range(n):
                cps.append(pltpu.make_async_remote_copy(
                    src_ref=ins[t].at[peer] if a2a[t] else ins[t], dst_ref=zones[t].at[peer],
                    send_sem=send_s.at[p * n + t], recv_sem=recv_s.at[p * n + t],
                    device_id=dev, device_id_type=MESH))
        for cp in cps:
            cp.wait_recv()
        for cp in cps:
            cp.wait_send()

    hbm = lambda a: pltpu.HBM(a.shape, a.dtype)
    out = pl.pallas_call(
        body, name=name,
        in_specs=[_HBM] * (2 * n) + [_SEM, _SEM, _ANY],
        out_specs=[_HBM] * (2 * n),
        out_shape=[hbm(a) for a in xs] + [hbm(a) for a in lands],
        input_output_aliases={t: t for t in range(2 * n)},
        compiler_params=pltpu.CompilerParams(has_side_effects=_EFFECT),
    )(*xs, *lands, send_sems, recv_sems, after)
    return out[n:]


def _adamw(w, m, v, parts, name, layer=0, into=None):
    L, R, C = w.shape
    P = parts.shape[0]
    tr = R
    t = 16
    while t <= R:
        if R % t == 0 and t * C <= 131072:
            tr = t
        t += 16
    if tr == R and R * C > 131072 and R % 16 == 0:
        tr = 16

    def body(w_ref, m_ref, v_ref, p_ref, *rest):
        g_ref, d_ref, mo_ref, vo_ref = rest[-4:]
        g = p_ref[0].astype(F32)
        for p in range(1, P):
            g = g + p_ref[p].astype(F32)
        mn = ADAM_B1 * m_ref[...] + (1.0 - ADAM_B1) * g
        vn = ADAM_B2 * v_ref[...] + (1.0 - ADAM_B2) * (g * g)
        m_hat = mn / (1.0 - ADAM_B1 ** ADAM_STEP)
        v_hat = vn / (1.0 - ADAM_B2 ** ADAM_STEP)
        g_ref[...] = g
        d_ref[...] = -ADAM_LR * (m_hat / (jnp.sqrt(v_hat) + ADAM_EPS) + ADAM_WD * w_ref[...])
        mo_ref[...] = mn
        vo_ref[...] = vn

    blk = pl.BlockSpec((None, tr, C), lambda i: (layer, i, 0))
    prev = list(into) if into is not None else []
    return pl.pallas_call(
        body, name=name, grid=(R // tr,),
        in_specs=[blk, blk, blk, pl.BlockSpec((P, tr, C), lambda i: (0, i, 0))] + [_ANY] * len(prev),
        out_specs=[blk] * 4, out_shape=[jax.ShapeDtypeStruct((L, R, C), F32)] * 4,
        input_output_aliases={4 + i: i for i in range(len(prev))},
        compiler_params=_cp(("parallel",)),
    )(w, m, v, parts, *prev)


def _perm_cols(w):
    pad = jnp.zeros((w.shape[0], LANE - SSD_HEADS), w.dtype)
    return jnp.concatenate([w[:, 0:1536], w[:, 2438:2694], w[:, 1536:2432], w[:, 2432:2438], pad,
                            w[:, 2694:2950]], axis=1)


def _unperm_cols(w):
    return jnp.concatenate([w[:, 0:1536], w[:, XBC0:XBC0 + SSD_CONV_DIM], w[:, DT0:DT0 + SSD_HEADS],
                            w[:, U0:U0 + SGU_W], w[:, VS0:VS0 + SGU_W]], axis=1)


_SMALL = ("ffn1_norm", "mix_norm", "conv_w", "conv_b", "dt_bias", "a_log", "d_skip", "ssd_norm",
          "sgu_ln_g", "sgu_ln_b", "sgu_w", "sgu_b", "ffn2_norm", "final_norm", "loss")


def _pack(d):
    v = jnp.concatenate([d[k].astype(F32).reshape(-1) for k in _SMALL])
    n = v.shape[0]
    npad = -(-n // (LANE * 16)) * (LANE * 16)
    return jnp.pad(v, (0, npad - n)).reshape(npad // LANE, LANE)


def _unpack(p, shapes):
    v = p.reshape(-1)
    out, o = {}, 0
    for k in _SMALL:
        n = int(np.prod(shapes[k]))
        out[k] = v[o:o + n].reshape(shapes[k])
        o += n
    return out


def kernel(x, ffn1_norm, ffn1_w_gate, ffn1_w_up, ffn1_w_down, mix_norm, w_in, conv_w, conv_b, dt_bias, a_log, d_skip, ssd_norm, sgu_ln_g, sgu_ln_b, sgu_w, sgu_b, w_out, ffn2_norm, ffn2_w_gate, ffn2_w_up, ffn2_w_down, final_norm, loss_target, m_ffn1_norm, m_ffn1_w_gate, m_ffn1_w_up, m_ffn1_w_down, m_mix_norm, m_w_in, m_conv_w, m_conv_b, m_dt_bias, m_a_log, m_d_skip, m_ssd_norm, m_sgu_ln_g, m_sgu_ln_b, m_sgu_w, m_sgu_b, m_w_out, m_ffn2_norm, m_ffn2_w_gate, m_ffn2_w_up, m_ffn2_w_down, m_final_norm, v_ffn1_norm, v_ffn1_w_gate, v_ffn1_w_up, v_ffn1_w_down, v_mix_norm, v_w_in, v_conv_w, v_conv_b, v_dt_bias, v_a_log, v_d_skip, v_ssd_norm, v_sgu_ln_g, v_sgu_ln_b, v_sgu_w, v_sgu_b, v_w_out, v_ffn2_norm, v_ffn2_w_gate, v_ffn2_w_up, v_ffn2_w_down, v_final_norm):
    B, S, D = x.shape
    T = B * S
    L = ffn1_norm.shape[0]
    me = 4 * lax.axis_index("x") + 2 * lax.axis_index("y") + lax.axis_index("c")
    cs = conv_w.shape[2]
    W = dict(ffn1_norm=ffn1_norm, ffn1_w_gate=ffn1_w_gate, ffn1_w_up=ffn1_w_up, ffn1_w_down=ffn1_w_down,
             mix_norm=mix_norm, w_in=w_in, conv_w=conv_w, conv_b=conv_b, dt_bias=dt_bias, a_log=a_log,
             d_skip=d_skip, ssd_norm=ssd_norm, sgu_ln_g=sgu_ln_g, sgu_ln_b=sgu_ln_b, sgu_w=sgu_w, sgu_b=sgu_b,
             w_out=w_out, ffn2_norm=ffn2_norm, ffn2_w_gate=ffn2_w_gate, ffn2_w_up=ffn2_w_up,
             ffn2_w_down=ffn2_w_down, final_norm=final_norm)
    M = dict(ffn1_norm=m_ffn1_norm, ffn1_w_gate=m_ffn1_w_gate, ffn1_w_up=m_ffn1_w_up, ffn1_w_down=m_ffn1_w_down,
             mix_norm=m_mix_norm, w_in=m_w_in, conv_w=m_conv_w, conv_b=m_conv_b, dt_bias=m_dt_bias, a_log=m_a_log,
             d_skip=m_d_skip, ssd_norm=m_ssd_norm, sgu_ln_g=m_sgu_ln_g, sgu_ln_b=m_sgu_ln_b, sgu_w=m_sgu_w,
             sgu_b=m_sgu_b, w_out=m_w_out, ffn2_norm=m_ffn2_norm, ffn2_w_gate=m_ffn2_w_gate,
             ffn2_w_up=m_ffn2_w_up, ffn2_w_down=m_ffn2_w_down, final_norm=m_final_norm)
    V = dict(ffn1_norm=v_ffn1_norm, ffn1_w_gate=v_ffn1_w_gate, ffn1_w_up=v_ffn1_w_up, ffn1_w_down=v_ffn1_w_down,
             mix_norm=v_mix_norm, w_in=v_w_in, conv_w=v_conv_w, conv_b=v_conv_b, dt_bias=v_dt_bias, a_log=v_a_log,
             d_skip=v_d_skip, ssd_norm=v_ssd_norm, sgu_ln_g=v_sgu_ln_g, sgu_ln_b=v_sgu_ln_b, sgu_w=v_sgu_w,
             sgu_b=v_sgu_b, w_out=v_w_out, ffn2_norm=v_ffn2_norm, ffn2_w_gate=v_ffn2_w_gate,
             ffn2_w_up=v_ffn2_w_up, ffn2_w_down=v_ffn2_w_down, final_norm=v_final_norm)
    FFN1 = ("ffn1_w_gate", "ffn1_w_up", "ffn1_w_down")
    FFN2 = ("ffn2_w_gate", "ffn2_w_up", "ffn2_w_down")
    MIX = ("w_in", "w_out")
    big = FFN1 + MIX + FFN2

    wgroups = [[(k, 0) for k in FFN1], [(k, 0) for k in MIX + FFN2] + [("conv_w", None)]]
    wgroups += [[(k, l) for k in big] for l in range(1, L)]
    wstarted, order = [], x
    for gi, grp in enumerate(wgroups):
        xs = [conv_w if k == "conv_w" else W[k][l].astype(BF16) for k, l in grp]
        flags = [False] * len(xs)
        lands = _place_local(xs, flags, f"gather_place_{gi}")
        st = _xchg_start(xs, lands, flags, order, f"gather_start_{gi}")
        order = st[-1]
        wstarted.append(st)
    G = {}

    def gathered(gi, after):
        lands = _xchg_wait(wstarted[gi], [False] * len(wgroups[gi]), after, f"gather_wait_{gi}")
        G.update(zip(wgroups[gi], lands))

    def cols(k, l):
        a = G[(k, l)]
        return jnp.transpose(a, (1, 0, 2)).reshape(a.shape[1], -1)

    def rows(k, l):
        a = G[(k, l)]
        return a.reshape(-1, a.shape[-1])

    bias = _attn_bias(S, min(256, S))
    row1 = lambda a: a.reshape(1, -1)

    def ffn1_params(l):
        return dict(g1=row1(ffn1_norm[l]), wg1=cols("ffn1_w_gate", l), wu1=cols("ffn1_w_up", l),
                    wd1=rows("ffn1_w_down", l))

    def rest_params(l):
        cw = jnp.transpose(G[("conv_w", None)][:, l], (1, 0, 2)).reshape(SSD_CONV, -1)
        return dict(
            gm=row1(mix_norm[l]), win=_perm_cols(rows("w_in", l)), wout=rows("w_out", l),
            g2=row1(ffn2_norm[l]), wg2=cols("ffn2_w_gate", l), wu2=cols("ffn2_w_up", l), wd2=rows("ffn2_w_down", l),
            cw=jnp.pad(cw, ((0, SUBLANE - SSD_CONV), (0, 0))), cb=row1(conv_b[l]),
            par=jnp.pad(jnp.stack([jnp.repeat(dt_bias[l], HEAD_DIM), jnp.repeat(a_log[l], HEAD_DIM),
                                   jnp.repeat(d_skip[l], HEAD_DIM), ssd_norm[l]]), ((0, SUBLANE - 4), (0, 0))),
            ln=jnp.pad(jnp.stack([sgu_ln_g[l], sgu_ln_b[l]]), ((0, SUBLANE - 2), (0, 0))),
            sw=sgu_w[l], bst=jnp.pad(sgu_b[l].T, ((0, 0), (0, SUBLANE - SGU_GROUPS))))

    xc = x.reshape(T, D)
    saved, lay = [], []
    for l in range(L):
        gathered(0 if l == 0 else l + 1, order if l == 0 else xc)
        p = ffn1_params(l)
        x1, gate1, up1 = _ffn_fwd(xc, p["g1"], p["wg1"], p["wu1"], p["wd1"], f"ffn1_fwd_{l}")
        if l == 0:
            gathered(1, x1)
        p.update(rest_params(l))
        lay.append(p)
        proj, ht = _norm_mm(x1, p["gm"], p["win"], f"in_proj_{l}")
        o_att, lse = _attn_fwd(proj, bias, B, S, f"attn_fwd_{l}")
        pre = _conv_fwd(proj, p["cw"], p["cb"], B, S, f"conv_fwd_{l}")
        y_ssd, sall = _ssd_fwd(pre, proj, p["par"], B, S, f"ssd_fwd_{l}")
        y_sgu = _sgu_fwd(proj, p["ln"], p["sw"], p["bst"], B, S, f"sgu_fwd_{l}")
        ycat = jnp.concatenate([o_att.astype(BF16), y_ssd, y_sgu], axis=1)
        x2 = _mm(ycat, p["wout"], "nn", f"out_proj_{l}", residual=x1)
        x3, gate2, up2 = _ffn_fwd(x2, p["g2"], p["wg2"], p["wu2"], p["wd2"], f"ffn2_fwd_{l}")
        saved.append(dict(x0=xc, gate1=gate1, up1=up1, x1=x1, ht=ht, proj=proj, o_att=o_att, lse=lse, pre=pre,
                          sall=sall, ycat=ycat, x2=x2, gate2=gate2, up2=up2))
        xc = x3
    loss_part, dx, dgf = _final_loss(xc, row1(final_norm), loss_target.reshape(T, D), "final_loss")

    gl = [dict() for _ in range(L)]
    gstarted, gorder = [], [order]

    def to_blocks(k, a):
        if k.endswith("w_gate") or k.endswith("w_up"):
            a = a.reshape(a.shape[0], N_DEV, -1).transpose(1, 0, 2)
        elif k.endswith("w_down"):
            a = a.reshape(a.shape[0], N_DEV, -1).transpose(1, 2, 0)
        else:
            a = a.reshape(N_DEV, -1, a.shape[-1])
        return a.astype(BF16)

    def send_grads(keys, l, extra, tag):
        xs = [to_blocks(k, gl[l][k]) for k in keys] + extra
        flags = [True] * len(keys) + [False] * len(extra)
        lands = _place_local(xs, flags, f"grads_place_{tag}")
        st = _xchg_start(xs, lands, flags, gorder[0], f"grads_start_{tag}")
        gorder[0] = st[-1]
        gstarted.append((keys, l, st, flags, tag))
    for l in reversed(range(L)):
        p, s, g = lay[l], saved[l], gl[l]
        dx2, dgate, dup, act, xnt, dacct, g["ffn2_norm"] = _ffn_bwd_dx(
            dx, s["x2"], p["g2"], s["gate2"], s["up2"], p["wg2"], p["wu2"], p["wd2"], f"ffn2_bwd_{l}")
        g["ffn2_w_gate"], g["ffn2_w_up"], g["ffn2_w_down"] = _ffn_dw(xnt, dacct, dgate, dup, act, f"ffn2_dw_{l}")
        dycat = _mm(dx2, p["wout"], "nt", f"out_proj_dx_{l}")
        g["w_out"] = _mm(s["ycat"], dx2, "tn", f"out_proj_dw_{l}", tm_cap=1024, tk_cap=512)
        dq, dk, dv = _attn_bwd(s["proj"], s["o_att"], s["lse"], dycat, bias, B, S, f"attn_bwd_{l}")
        dpre, dz, ddt, dpar = _ssd_bwd(s["pre"], s["proj"], s["sall"], dycat, p["par"], B, S, f"ssd_bwd_{l}")
        dxbc, dwb = _conv_bwd(dpre, s["proj"], p["cw"], B, S, f"conv_bwd_{l}")
        du, dvs, dln, dsw, dbst = _sgu_bwd(s["proj"], dycat, p["ln"], p["sw"], p["bst"], B, S, f"sgu_bwd_{l}")
        dproj = jnp.concatenate([dq, dk, dv, dz, du, dxbc, ddt, dvs], axis=1)
        g["w_in"] = _unperm_cols(_mm_resident_lhs(s["ht"], dproj, f"in_proj_dw_{l}"))
        dx1, g["mix_norm"] = _norm_mm_bwd(dproj, s["x1"], p["gm"], p["win"], dx2, f"in_proj_bwd_{l}")
        if l == 0:
            send_grads(MIX + FFN2, 0, [], "l0a")
        dx, dgate, dup, act, xnt, dacct, g["ffn1_norm"] = _ffn_bwd_dx(
            dx1, s["x0"], p["g1"], s["gate1"], s["up1"], p["wg1"], p["wu1"], p["wd1"], f"ffn1_bwd_{l}")
        g["ffn1_w_gate"], g["ffn1_w_up"], g["ffn1_w_down"] = _ffn_dw(xnt, dacct, dgate, dup, act, f"ffn1_dw_{l}")
        if l > 0:
            send_grads(big, l, [], f"l{l}")
        hsum = lambda r: r.reshape(SSD_HEADS, HEAD_DIM).sum(-1)
        g["conv_w"], g["conv_b"] = dwb[:SSD_CONV], dwb[SSD_CONV]
        g["dt_bias"], g["a_log"], g["d_skip"], g["ssd_norm"] = hsum(dpar[0]), hsum(dpar[1]), hsum(dpar[2]), dpar[3]
        g["sgu_ln_g"], g["sgu_ln_b"], g["sgu_w"], g["sgu_b"] = dln[0], dln[1], dsw, dbst[:, :SGU_GROUPS].T
    grad_x = dx.reshape(B, S, D)

    stack = lambda k: jnp.stack([gl[l][k] for l in range(L)])
    zero1 = jnp.zeros((1,), F32)
    W["loss"], M["loss"], V["loss"] = zero1, zero1, zero1
    per_layer = lambda k: k not in ("final_norm", "conv_w", "loss")
    small = {k: stack(k) for k in _SMALL if per_layer(k) or k == "conv_w"}
    small["final_norm"], small["loss"] = dgf.reshape(-1), loss_part[0, :1]
    small = {k: small[k].reshape((L,) + W[k].shape[1:]) if per_layer(k) else small[k] for k in _SMALL}
    full_shapes = {k: (W[k].shape if k != "conv_w" else (L, SSD_CONV, SSD_CONV_DIM)) for k in _SMALL}
    send_grads(FFN1, 0, [_pack(small)], "l0b")

    res, after, small_parts = {}, gorder[0], None
    for keys, l, st, flags, tag in gstarted:
        lands = _xchg_wait(st, flags, after, f"grads_wait_{tag}")
        for k, pk in zip(keys, lands):
            res[k] = _adamw(W[k], M[k], V[k], pk, f"adamw_{k}_{l}", layer=l, into=res.get(k))
            after = res[k][0]
        if len(lands) > len(keys):
            small_parts = lands[-1]
    grads, deltas, new_m, new_v = [{k: res[k][i] for k in big} for i in range(4)]

    def embed(a, k):
        if k != "conv_w":
            return a
        return lax.dynamic_update_slice(jnp.zeros(full_shapes[k], F32), a, (0, 0, me * cs))

    outs = _adamw(_pack({k: embed(W[k], k) for k in _SMALL})[None], _pack({k: embed(M[k], k) for k in _SMALL})[None],
                  _pack({k: embed(V[k], k) for k in _SMALL})[None], small_parts, "adamw_small")
    for d, o in zip((grads, deltas, new_m, new_v), outs):
        u = _unpack(o, full_shapes)
        u["conv_w"] = lax.dynamic_slice(u["conv_w"], (0, 0, me * cs), (L, SSD_CONV, cs))
        d.update(u)

    names = ("ffn1_norm", "ffn1_w_gate", "ffn1_w_up", "ffn1_w_down", "mix_norm", "w_in", "conv_w", "conv_b",
             "dt_bias", "a_log", "d_skip", "ssd_norm", "sgu_ln_g", "sgu_ln_b", "sgu_w", "sgu_b", "w_out",
             "ffn2_norm", "ffn2_w_gate", "ffn2_w_up", "ffn2_w_down", "final_norm")
    loss = grads["loss"][0]
    return (loss, grad_x, *[grads[n] for n in names], *[deltas[n] for n in names],
            *[new_m[n] for n in names], *[new_v[n] for n in names])
```

```python
import functools

import numpy as np
import jax
import jax.numpy as jnp
from jax import lax
from jax.experimental import pallas as pl
from jax.experimental.pallas import tpu as pltpu

F32, BF16 = jnp.float32, jnp.bfloat16
HI = lax.Precision.HIGHEST
MESH = pl.DeviceIdType.MESH
N_DEV = 8
VMEM_LIMIT_BYTES = 56 * 1024 * 1024
LANE, SUBLANE = 128, 8

HEAD_DIM = 64
ATT_W = 384
SSD_W = 384
SSD_HEADS = 6
SSD_STATE = 128
SSD_CONV = 4
CHUNK = 128
SSD_CONV_DIM = 896
SGU_W = 256
SGU_GROUPS = 4
D_IN = 2950
RMS_EPS = 1e-6
LN_EPS = 1e-5
NEG = -1e30

PW = 3072
Q0, K0, V0, Z0, U0, XBC0, DT0, VS0 = 0, 384, 768, 1152, 1536, 1792, 2688, 2816

ADAM_LR, ADAM_B1, ADAM_B2, ADAM_EPS, ADAM_WD, ADAM_STEP = 0.001, 0.9, 0.999, 1e-08, 0.01, 10


def _cp(sem=None):
    return pltpu.CompilerParams(dimension_semantics=sem, vmem_limit_bytes=VMEM_LIMIT_BYTES)


def _tile(n, cap, mult=LANE):
    best = None
    t = mult
    while t <= min(n, cap):
        if n % t == 0:
            best = t
        t += mult
    return best if best is not None else n


def _dot(a, b, prec=None):
    return jnp.dot(a, b, preferred_element_type=F32, precision=prec)


def _dot_nt(a, b, prec=None):
    return lax.dot_general(a, b, (((1,), (1,)), ((), ())), preferred_element_type=F32, precision=prec)


def _dot_tn(a, b, prec=None):
    return lax.dot_general(a, b, (((0,), (0,)), ((), ())), preferred_element_type=F32, precision=prec)


def _sigmoid(x):
    return 1.0 / (1.0 + jnp.exp(-x))


def _silu(x):
    return x * _sigmoid(x)


def _gelu(x):
    return 0.5 * x * (1.0 + lax.erf(x * 0.7071067811865476))


def _softplus(x):
    return jnp.maximum(x, 0.0) + jnp.log(1.0 + jnp.exp(-jnp.abs(x)))


def _rms_fwd(x, g):
    rstd = lax.rsqrt(jnp.mean(x * x, axis=-1, keepdims=True) + RMS_EPS)
    xhat = x * rstd
    return xhat * g, xhat, rstd


def _rms_bwd(dy, xhat, rstd, g):
    dxhat = dy * g
    dx = rstd * (dxhat - xhat * jnp.mean(dxhat * xhat, axis=-1, keepdims=True))
    return dx, dy * xhat


def _resident(shape):
    return pl.BlockSpec(shape, lambda *_: (0,) * len(shape), pipeline_mode=pl.Buffered(1))


def _mm(a, b, mode, name, out_dtype=F32, residual=None, tm_cap=512, tn_cap=1024, tk_cap=1024):
    if mode == "nn":
        (M, K), (_, N) = a.shape, b.shape
    elif mode == "nt":
        (M, K), (N, _) = a.shape, b.shape
    else:
        (K, M), (_, N) = a.shape, b.shape
    tm, tn, tk = _tile(M, tm_cap), _tile(N, tn_cap), _tile(K, tk_cap)
    nk = K // tk
    if mode == "tn":
        a_spec = pl.BlockSpec((tk, tm), lambda i, j, k: (k, i))
    else:
        a_spec = pl.BlockSpec((tm, tk), lambda i, j, k: (i, k))
    if mode == "nt":
        b_spec = pl.BlockSpec((tn, tk), lambda i, j, k: (j, k))
    else:
        b_spec = pl.BlockSpec((tk, tn), lambda i, j, k: (k, j))
    o_spec = pl.BlockSpec((tm, tn), lambda i, j, k: (i, j))
    has_res = residual is not None

    def prod(a_ref, b_ref):
        av = a_ref[...].astype(BF16)
        bv = b_ref[...].astype(BF16)
        if mode == "nn":
            return _dot(av, bv)
        if mode == "nt":
            return _dot_nt(av, bv)
        return _dot_tn(av, bv)

    def body(*refs):
        a_ref, b_ref = refs[:2]
        r_ref = refs[2] if has_res else None
        o_ref = refs[2 + has_res]
        if nk == 1:
            o = prod(a_ref, b_ref)
            if has_res:
                o = r_ref[...] + o
            o_ref[...] = o.astype(out_dtype)
            return
        acc = refs[3 + has_res]
        k = pl.program_id(2)

        @pl.when(k == 0)
        def _():
            acc[...] = jnp.zeros_like(acc)

        acc[...] += prod(a_ref, b_ref)

        @pl.when(k == nk - 1)
        def _():
            o = acc[...]
            if has_res:
                o = r_ref[...] + o
            o_ref[...] = o.astype(out_dtype)

    ins = [a, b] + ([residual] if has_res else [])
    in_specs = [a_spec, b_spec] + ([o_spec] if has_res else [])
    return pl.pallas_call(
        body, name=name, grid=(M // tm, N // tn, nk),
        in_specs=in_specs, out_specs=o_spec,
        out_shape=jax.ShapeDtypeStruct((M, N), out_dtype),
        scratch_shapes=[pltpu.VMEM((tm, tn), F32)] if nk > 1 else [],
        compiler_params=_cp(("parallel", "parallel", "arbitrary")),
    )(*ins)


def _ffn_fwd(x, g, wg, wu, wd, name):
    T, D = x.shape
    F = wg.shape[1]
    tm = _tile(T, 512)

    def body(x_ref, g_ref, wg_ref, wu_ref, wd_ref, out_ref, gate_ref, up_ref):
        xv = x_ref[...]
        xn = _rms_fwd(xv, g_ref[...])[0].astype(BF16)
        gate = _dot(xn, wg_ref[...])
        up = _dot(xn, wu_ref[...])
        gate_ref[...] = gate.astype(BF16)
        up_ref[...] = up.astype(BF16)
        act = (_silu(gate) * up).astype(BF16)
        out_ref[...] = xv + 0.5 * _dot(act, wd_ref[...])

    row = lambda w: pl.BlockSpec((tm, w), lambda i: (i, 0))
    return pl.pallas_call(
        body, name=name, grid=(T // tm,),
        in_specs=[row(D), _resident((1, D)), _resident((D, F)), _resident((D, F)), _resident((F, D))],
        out_specs=[row(D), row(F), row(F)],
        out_shape=[jax.ShapeDtypeStruct((T, D), F32),
                   jax.ShapeDtypeStruct((T, F), BF16),
                   jax.ShapeDtypeStruct((T, F), BF16)],
        compiler_params=_cp(("parallel",)),
    )(x, g, wg, wu, wd)


def _ffn_bwd_dx(dout, x, g, gate, up, wg, wu, wd, name):
    T, D = x.shape
    F = wg.shape[1]
    tm, th = _tile(T, 256), _tile(F, 256)
    nj = F // th

    def body(dout_ref, x_ref, g_ref, gate_ref, up_ref, wg_ref, wu_ref, wd_ref,
             dx_ref, dgate_ref, dup_ref, act_ref, xnt_ref, dacct_ref, dg_ref):
        @pl.when(pl.program_id(0) == 0)
        def _():
            dg_ref[...] = jnp.zeros_like(dg_ref)

        gv = g_ref[...]
        dout_v = dout_ref[...]
        xn, xhat, rstd = _rms_fwd(x_ref[...], gv)
        xnt_ref[...] = xn.T.astype(BF16)
        dacc = 0.5 * dout_v
        dacct_ref[...] = dacc.T.astype(BF16)
        dact = _dot_nt(dacc.astype(BF16), wd_ref[...])
        gt = gate_ref[...].astype(F32)
        u = up_ref[...].astype(F32)
        sig = _sigmoid(gt)
        sl = gt * sig
        dgate = (dact * u * (sig * (1.0 + gt * (1.0 - sig)))).astype(BF16)
        dup = (dact * sl).astype(BF16)
        act = (sl * u).astype(BF16)
        for j in range(nj):
            cs = slice(j * th, (j + 1) * th)
            dgate_ref[j] = dgate[:, cs]
            dup_ref[j] = dup[:, cs]
            act_ref[j] = act[:, cs]
        dxn = _dot_nt(dgate, wg_ref[...]) + _dot_nt(dup, wu_ref[...])
        dx, dgrow = _rms_bwd(dxn, xhat, rstd, gv)
        dx_ref[...] = dout_v + dx
        dg_ref[...] += jnp.sum(dgrow, axis=0, keepdims=True)

    row = lambda w: pl.BlockSpec((tm, w), lambda i: (i, 0))
    tiled = pl.BlockSpec((nj, tm, th), lambda i: (0, i, 0))
    tr = pl.BlockSpec((D, tm), lambda i: (0, i))
    return pl.pallas_call(
        body, name=name, grid=(T // tm,),
        in_specs=[row(D), row(D), _resident((1, D)), row(F), row(F),
                  _resident((D, F)), _resident((D, F)), _resident((F, D))],
        out_specs=[row(D), tiled, tiled, tiled, tr, tr, pl.BlockSpec((1, D), lambda i: (0, 0))],
        out_shape=[jax.ShapeDtypeStruct((T, D), F32)] + [jax.ShapeDtypeStruct((nj, T, th), BF16)] * 3
        + [jax.ShapeDtypeStruct((D, T), BF16)] * 2 + [jax.ShapeDtypeStruct((1, D), F32)],
        compiler_params=_cp(("arbitrary",)),
    )(dout, x, g, gate, up, wg, wu, wd)


def _ffn_dw(xnt, dacct, dgate, dup, act, name):
    D, T = xnt.shape
    nj, _, th = dgate.shape

    def body(xnt_ref, dacct_ref, dgate_ref, dup_ref, act_ref, dwg_ref, dwu_ref, dwdt_ref):
        xv = xnt_ref[...]
        dwg_ref[...] = _dot(xv, dgate_ref[...])
        dwu_ref[...] = _dot(xv, dup_ref[...])
        dwdt_ref[...] = _dot(dacct_ref[...], act_ref[...])

    tile = pl.BlockSpec((None, T, th), lambda j: (j, 0, 0))
    out = pl.BlockSpec((D, th), lambda j: (0, j))
    return pl.pallas_call(
        body, name=name, grid=(nj,),
        in_specs=[_resident((D, T)), _resident((D, T)), tile, tile, tile],
        out_specs=[out, out, out], out_shape=[jax.ShapeDtypeStruct((D, nj * th), F32)] * 3,
        compiler_params=_cp(("parallel",)),
    )(xnt, dacct, dgate, dup, act)


def _norm_mm(x, g, w, name):
    T, D = x.shape
    N = w.shape[1]
    tm = _tile(T, 512)

    def body(x_ref, g_ref, w_ref, o_ref, ht_ref):
        xn = _rms_fwd(x_ref[...], g_ref[...])[0]
        ht_ref[...] = xn.T.astype(BF16)
        o_ref[...] = _dot(xn.astype(BF16), w_ref[...])

    return pl.pallas_call(
        body, name=name, grid=(T // tm,),
        in_specs=[pl.BlockSpec((tm, D), lambda i: (i, 0)), _resident((1, D)), _resident((D, N))],
        out_specs=[pl.BlockSpec((tm, N), lambda i: (i, 0)), pl.BlockSpec((D, tm), lambda i: (0, i))],
        out_shape=[jax.ShapeDtypeStruct((T, N), F32), jax.ShapeDtypeStruct((D, T), BF16)],
        compiler_params=_cp(("parallel",)),
    )(x, g, w)


def _norm_mm_bwd(dproj, x, g, w, dres, name):
    T, D = x.shape
    N = w.shape[1]
    tm = _tile(T, 512)

    def body(dp_ref, x_ref, g_ref, w_ref, dres_ref, dx_ref, dg_ref):
        @pl.when(pl.program_id(0) == 0)
        def _():
            dg_ref[...] = jnp.zeros_like(dg_ref)

        gv = g_ref[...]
        dh = _dot_nt(dp_ref[...], w_ref[...])
        _, xhat, rstd = _rms_fwd(x_ref[...], gv)
        dx, dgrow = _rms_bwd(dh, xhat, rstd, gv)
        dx_ref[...] = dres_ref[...] + dx
        dg_ref[...] += jnp.sum(dgrow, axis=0, keepdims=True)

    row = pl.BlockSpec((tm, D), lambda i: (i, 0))
    one = pl.BlockSpec((1, D), lambda i: (0, 0))
    return pl.pallas_call(
        body, name=name, grid=(T // tm,),
        in_specs=[pl.BlockSpec((tm, N), lambda i: (i, 0)), row, _resident((1, D)), _resident((D, N)), row],
        out_specs=[row, one],
        out_shape=[jax.ShapeDtypeStruct((T, D), F32), jax.ShapeDtypeStruct((1, D), F32)],
        compiler_params=_cp(("arbitrary",)),
    )(dproj, x, g, w, dres)


def _mm_resident_lhs(at, b, name, tn_cap=512):
    M, K = at.shape
    N = b.shape[1]
    tn = _tile(N, tn_cap)

    def body(a_ref, b_ref, o_ref):
        o_ref[...] = _dot(a_ref[...], b_ref[...])

    return pl.pallas_call(
        body, name=name, grid=(N // tn,),
        in_specs=[_resident((M, K)), pl.BlockSpec((K, tn), lambda j: (0, j))],
        out_specs=pl.BlockSpec((M, tn), lambda j: (0, j)),
        out_shape=jax.ShapeDtypeStruct((M, N), F32),
        compiler_params=_cp(("parallel",)),
    )(at, b)


def _final_loss(x, g, target, name):
    T, D = x.shape
    tm = _tile(T, 512)

    def body(x_ref, g_ref, t_ref, loss_ref, dx_ref, dg_ref):
        @pl.when(pl.program_id(0) == 0)
        def _():
            dg_ref[...] = jnp.zeros_like(dg_ref)
            loss_ref[...] = jnp.zeros_like(loss_ref)

        gv = g_ref[...]
        y, xhat, rstd = _rms_fwd(x_ref[...], gv)
        err = y - t_ref[...]
        part = 0.5 * jnp.sum(jnp.mean(err * err, axis=-1, keepdims=True), axis=0, keepdims=True)
        loss_ref[...] += jnp.broadcast_to(part, loss_ref.shape)
        dy = err * (1.0 / D)
        dx, dgrow = _rms_bwd(dy, xhat, rstd, gv)
        dx_ref[...] = dx
        dg_ref[...] += jnp.sum(dgrow, axis=0, keepdims=True)

    row = pl.BlockSpec((tm, D), lambda i: (i, 0))
    one = pl.BlockSpec((1, D), lambda i: (0, 0))
    return pl.pallas_call(
        body, name=name, grid=(T // tm,),
        in_specs=[row, one, row],
        out_specs=[pl.BlockSpec((1, LANE), lambda i: (0, 0)), row, one],
        out_shape=[jax.ShapeDtypeStruct((1, LANE), F32), jax.ShapeDtypeStruct((T, D), F32),
                   jax.ShapeDtypeStruct((1, D), F32)],
        compiler_params=_cp(("arbitrary",)),
    )(x, g, target)


def _attn_bias(S, bq):
    nb = S // bq
    d = (jnp.arange(nb)[:, None, None] * bq + jnp.arange(bq)[None, :, None]
         - jnp.arange(bq)[None, None, :])
    ok = d >= 0
    mult = ((ok & (d <= 128)).astype(F32) + (ok & (d % 4 == 0) & (d <= 512)).astype(F32)
            + (ok & (d % 16 == 0) & (d <= 2048)).astype(F32))
    return jnp.where(mult > 0, jnp.log(jnp.maximum(mult, 1.0)), NEG).astype(F32)


def _attn_fwd(proj, bias, B, S, name):
    T = B * S
    nb, bq, _ = bias.shape
    qcol, kcol, vcol = Q0 // LANE, K0 // LANE, V0 // LANE

    def body(q_ref, k_ref, v_ref, bias_ref, o_ref, lse_ref, qs, ks, vs):
        for hh in range(2):
            sl = slice(HEAD_DIM * hh, HEAD_DIM * (hh + 1))
            qs[hh] = (q_ref[:, sl] * 0.125).astype(BF16)
            ks[hh] = k_ref[:, sl].astype(BF16)
            vs[hh] = v_ref[:, sl].astype(BF16)
        def qloop(qb, carry):
            q0 = pl.multiple_of(qb * bq, bq)
            qv = [qs[hh, pl.ds(q0, bq), :] for hh in range(2)]

            def kloop(kb, c):
                k0 = pl.multiple_of(kb * bq, bq)
                bias_t = bias_ref[qb - kb]
                out = []
                for hh in range(2):
                    m, l, acc = c[hh]
                    s = _dot_nt(qv[hh], ks[hh, pl.ds(k0, bq), :]) + bias_t
                    mn = jnp.maximum(m, jnp.max(s, axis=-1, keepdims=True))
                    p = jnp.exp(s - mn)
                    a = jnp.exp(m - mn)
                    l = a * l + jnp.sum(p, axis=-1, keepdims=True)
                    acc = a * acc + _dot(p.astype(BF16), vs[hh, pl.ds(k0, bq), :])
                    out.append((mn, l, acc))
                return tuple(out)

            one = (jnp.full((bq, 1), NEG, F32), jnp.zeros((bq, 1), F32), jnp.zeros((bq, HEAD_DIM), F32))
            res = lax.fori_loop(0, qb + 1, kloop, (one, one))
            for hh in range(2):
                m, l, acc = res[hh]
                o_ref[pl.ds(q0, bq), HEAD_DIM * hh:HEAD_DIM * (hh + 1)] = acc / l
                lse_ref[pl.ds(q0, bq), hh:hh + 1] = m + jnp.log(l)
            return carry

        lax.fori_loop(0, nb, qloop, 0)

    blk = lambda c0: pl.BlockSpec((S, LANE), lambda b, p: (b, c0 + p))
    return pl.pallas_call(
        body, name=name, grid=(B, ATT_W // LANE),
        in_specs=[blk(qcol), blk(kcol), blk(vcol),
                  pl.BlockSpec((nb, bq, bq), lambda b, p: (0, 0, 0))],
        out_specs=[pl.BlockSpec((S, LANE), lambda b, p: (b, p)),
                   pl.BlockSpec((None, None, S, 2), lambda b, p: (b, p, 0, 0))],
        out_shape=[jax.ShapeDtypeStruct((T, ATT_W), F32),
                   jax.ShapeDtypeStruct((B, ATT_W // LANE, S, 2), F32)],
        scratch_shapes=[pltpu.VMEM((2, S, HEAD_DIM), BF16)] * 3,
        compiler_params=_cp(("parallel", "parallel")),
    )(proj, proj, proj, bias)


def _attn_bwd(proj, o, lse, dy, bias, B, S, name):
    T = B * S
    nb, bq, _ = bias.shape
    qcol, kcol, vcol = Q0 // LANE, K0 // LANE, V0 // LANE

    def body(q_ref, k_ref, v_ref, o_ref, lse_ref, do_ref, bias_ref, dq_ref, dk_ref, dv_ref,
             qs, ks, vs, dos, dl, dqs):
        for hh in range(2):
            sl = slice(HEAD_DIM * hh, HEAD_DIM * (hh + 1))
            qs[hh] = (q_ref[:, sl] * 0.125).astype(BF16)
            ks[hh] = k_ref[:, sl].astype(BF16)
            vs[hh] = v_ref[:, sl].astype(BF16)
            do = do_ref[:, sl]
            dos[hh] = do.astype(BF16)
            dl[:, hh:hh + 1] = jnp.sum(do * o_ref[:, sl], axis=-1, keepdims=True)
        dqs[...] = jnp.zeros_like(dqs)

        def kloop(kb, carry):
            k0 = pl.multiple_of(kb * bq, bq)
            kv = [ks[hh, pl.ds(k0, bq), :] for hh in range(2)]
            vv = [vs[hh, pl.ds(k0, bq), :] for hh in range(2)]

            def qloop(qb, c):
                q0 = pl.multiple_of(qb * bq, bq)
                bias_t = bias_ref[qb - kb]
                out = []
                for hh in range(2):
                    dk, dv = c[hh]
                    q = qs[hh, pl.ds(q0, bq), :]
                    do = dos[hh, pl.ds(q0, bq), :]
                    s = _dot_nt(q, kv[hh]) + bias_t
                    p = jnp.exp(s - lse_ref[pl.ds(q0, bq), hh:hh + 1])
                    dv = dv + _dot_tn(p.astype(BF16), do)
                    dp = _dot_nt(do, vv[hh])
                    ds = (p * (dp - dl[pl.ds(q0, bq), hh:hh + 1])).astype(BF16)
                    dk = dk + _dot_tn(ds, q)
                    dqs[hh, pl.ds(q0, bq), :] += _dot(ds, kv[hh])
                    out.append((dk, dv))
                return tuple(out)

            z = jnp.zeros((bq, HEAD_DIM), F32)
            res = lax.fori_loop(kb, nb, qloop, ((z, z), (z, z)))
            for hh in range(2):
                sl = slice(HEAD_DIM * hh, HEAD_DIM * (hh + 1))
                dk_ref[pl.ds(k0, bq), sl] = res[hh][0].astype(dk_ref.dtype)
                dv_ref[pl.ds(k0, bq), sl] = res[hh][1].astype(dv_ref.dtype)
            return carry

        lax.fori_loop(0, nb, kloop, 0)
        for hh in range(2):
            dq_ref[:, HEAD_DIM * hh:HEAD_DIM * (hh + 1)] = (dqs[hh] * 0.125).astype(dq_ref.dtype)

    blk = lambda c0: pl.BlockSpec((S, LANE), lambda b, p: (b, c0 + p))
    own = pl.BlockSpec((S, LANE), lambda b, p: (b, p))
    return pl.pallas_call(
        body, name=name, grid=(B, ATT_W // LANE),
        in_specs=[blk(qcol), blk(kcol), blk(vcol), own,
                  pl.BlockSpec((None, None, S, 2), lambda b, p: (b, p, 0, 0)), own,
                  pl.BlockSpec((nb, bq, bq), lambda b, p: (0, 0, 0))],
        out_specs=[own, own, own],
        out_shape=[jax.ShapeDtypeStruct((T, ATT_W), BF16)] * 3,
        scratch_shapes=[pltpu.VMEM((2, S, HEAD_DIM), BF16)] * 4
        + [pltpu.VMEM((S, 2), F32), pltpu.VMEM((2, S, HEAD_DIM), F32)],
        compiler_params=_cp(("parallel", "parallel")),
    )(proj, proj, proj, o, lse, dy, bias)


def _conv_fwd(proj, cw, cb, B, S, name):
    T = B * S
    nc = SSD_CONV_DIM // LANE
    c0 = XBC0 // LANE

    def body(x_ref, w_ref, b_ref, o_ref):
        x = x_ref[...]
        t = lax.broadcasted_iota(jnp.int32, (S, 1), 0)
        acc = b_ref[...] + w_ref[SSD_CONV - 1:SSD_CONV, :] * x
        for k in range(SSD_CONV - 1):
            sh = SSD_CONV - 1 - k
            xs = jnp.where(t >= sh, pltpu.roll(x, sh, 0), 0.0)
            acc = acc + w_ref[k:k + 1, :] * xs
        o_ref[...] = acc

    return pl.pallas_call(
        body, name=name, grid=(B, nc),
        in_specs=[pl.BlockSpec((S, LANE), lambda b, j: (b, c0 + j)),
                  pl.BlockSpec((SUBLANE, LANE), lambda b, j: (0, j)),
                  pl.BlockSpec((1, LANE), lambda b, j: (0, j))],
        out_specs=pl.BlockSpec((S, LANE), lambda b, j: (b, j)),
        out_shape=jax.ShapeDtypeStruct((T, SSD_CONV_DIM), F32),
        compiler_params=_cp(("parallel", "parallel")),
    )(proj, cw, cb)


def _conv_bwd(dpre, proj, cw, B, S, name):
    T = B * S
    nc = SSD_CONV_DIM // LANE
    c0 = XBC0 // LANE

    def body(d_ref, x_ref, w_ref, dx_ref, dwb_ref):
        @pl.when(pl.program_id(1) == 0)
        def _():
            dwb_ref[...] = jnp.zeros_like(dwb_ref)

        d = d_ref[...]
        x = x_ref[...]
        t = lax.broadcasted_iota(jnp.int32, (S, 1), 0)
        dx = w_ref[SSD_CONV - 1:SSD_CONV, :] * d
        rows = [None] * SUBLANE
        rows[SSD_CONV - 1] = jnp.sum(d * x, axis=0, keepdims=True)
        for k in range(SSD_CONV - 1):
            sh = SSD_CONV - 1 - k
            dx = dx + w_ref[k:k + 1, :] * jnp.where(t < S - sh, pltpu.roll(d, S - sh, 0), 0.0)
            xs = jnp.where(t >= sh, pltpu.roll(x, sh, 0), 0.0)
            rows[k] = jnp.sum(d * xs, axis=0, keepdims=True)
        rows[SSD_CONV] = jnp.sum(d, axis=0, keepdims=True)
        dx_ref[...] = dx.astype(BF16)
        r = lax.broadcasted_iota(jnp.int32, (SUBLANE, LANE), 0)
        upd = jnp.zeros((SUBLANE, LANE), F32)
        for k in range(SSD_CONV + 1):
            upd = upd + jnp.where(r == k, rows[k], 0.0)
        dwb_ref[...] += upd

    return pl.pallas_call(
        body, name=name, grid=(nc, B),
        in_specs=[pl.BlockSpec((S, LANE), lambda j, b: (b, j)),
                  pl.BlockSpec((S, LANE), lambda j, b: (b, c0 + j)),
                  pl.BlockSpec((SUBLANE, LANE), lambda j, b: (0, j))],
        out_specs=[pl.BlockSpec((S, LANE), lambda j, b: (b, j)),
                   pl.BlockSpec((SUBLANE, LANE), lambda j, b: (0, j))],
        out_shape=[jax.ShapeDtypeStruct((T, SSD_CONV_DIM), BF16),
                   jax.ShapeDtypeStruct((SUBLANE, SSD_CONV_DIM), F32)],
        compiler_params=_cp(("parallel", "arbitrary")),
    )(dpre, proj, cw)


def _ssd_consts():
    e = np.zeros((LANE, SSD_W), np.float32)
    p = np.zeros((SUBLANE, SSD_W), np.float32)
    for h in range(SSD_HEADS):
        e[h, HEAD_DIM * h:HEAD_DIM * (h + 1)] = 1.0
        p[h, HEAD_DIM * h] = 1.0
    return jnp.asarray(e), jnp.asarray(p)


def _ssd_chunk(pre, z, dtr, sprev, par, e_mat, psel):
    L = CHUNK
    xc = _silu(pre)
    xs, bm, cm = xc[:, :SSD_W], xc[:, SSD_W:SSD_W + 2 * SSD_STATE], xc[:, SSD_W + 2 * SSD_STATE:]
    dtb, alog, dskip, ng = par[0:1], par[1:2], par[2:3], par[3:4]
    dt = _softplus(_dot(dtr, e_mat, HI) + dtb)
    a = dt * (-jnp.exp(alog))
    X = xs * dt
    ri = lax.broadcasted_iota(jnp.int32, (L, L), 0)
    ci = lax.broadcasted_iota(jnp.int32, (L, L), 1)
    tril = ri >= ci
    acs = _dot(tril.astype(F32), a, HI)
    acs_t = _dot_nt(psel, acs, HI)
    ecs = jnp.exp(acs)
    alast = acs[L - 1:L, :]
    xd = (X * jnp.exp(alast - acs)).astype(BF16)
    xb = X.astype(BF16)
    col = lax.broadcasted_iota(jnp.int32, (1, SSD_W), 1)
    sb = sprev.astype(BF16)
    y = dskip * xs
    snew = sprev * jnp.exp(alast)
    for g in range(2):
        gmask = (col >= g * (SSD_W // 2)) & (col < (g + 1) * (SSD_W // 2))
        bg = bm[:, SSD_STATE * g:SSD_STATE * (g + 1)].astype(BF16)
        cg = cm[:, SSD_STATE * g:SSD_STATE * (g + 1)].astype(BF16)
        cb = _dot_nt(cg, bg)
        for j in range(3):
            h = 3 * g + j
            seg = acs[:, HEAD_DIM * h:HEAD_DIM * h + 1] - acs_t[h:h + 1, :]
            dec = jnp.exp(jnp.where(tril, seg, NEG))
            yh = _dot((cb * dec).astype(BF16), xb)
            hmask = (col >= HEAD_DIM * h) & (col < HEAD_DIM * (h + 1))
            y = y + jnp.where(hmask, yh, 0.0)
        y = y + jnp.where(gmask, _dot(cg, sb) * ecs, 0.0)
        snew = snew + jnp.where(gmask, _dot_tn(bg, xd), 0.0)
    yg = y * _silu(z)
    sq = yg * yg
    g0 = col < SSD_W // 2
    ms0 = jnp.sum(jnp.where(g0, sq, 0.0), axis=-1, keepdims=True) * (2.0 / SSD_W)
    ms1 = jnp.sum(jnp.where(g0, 0.0, sq), axis=-1, keepdims=True) * (2.0 / SSD_W)
    r = jnp.where(g0, lax.rsqrt(ms0 + RMS_EPS), lax.rsqrt(ms1 + RMS_EPS))
    return yg * r * ng, snew


def _ssd_fwd(pre, proj, par, B, S, name):
    T = B * S
    nc = S // CHUNK
    e_mat, psel = _ssd_consts()

    def body(pre_ref, z_ref, dt_ref, par_ref, e_ref, p_ref, y_ref, sall_ref, st):
        @pl.when(pl.program_id(1) == 0)
        def _():
            st[...] = jnp.zeros_like(st)

        sprev = st[...]
        sall_ref[...] = sprev
        y, snew = _ssd_chunk(pre_ref[...], z_ref[...], dt_ref[...], sprev, par_ref[...], e_ref[...], p_ref[...])
        y_ref[...] = y.astype(BF16)
        st[...] = snew

    row = lambda b, c: b * nc + c
    full = lambda shp: pl.BlockSpec(shp, lambda b, c: (0, 0))
    return pl.pallas_call(
        body, name=name, grid=(B, nc),
        in_specs=[pl.BlockSpec((CHUNK, SSD_CONV_DIM), lambda b, c: (row(b, c), 0)),
                  pl.BlockSpec((CHUNK, SSD_W), lambda b, c: (row(b, c), Z0 // SSD_W)),
                  pl.BlockSpec((CHUNK, LANE), lambda b, c: (row(b, c), DT0 // LANE)),
                  full((SUBLANE, SSD_W)), full((LANE, SSD_W)), full((SUBLANE, SSD_W))],
        out_specs=[pl.BlockSpec((CHUNK, SSD_W), lambda b, c: (row(b, c), 0)),
                   pl.BlockSpec((None, SSD_STATE, SSD_W), lambda b, c: (row(b, c), 0, 0))],
        out_shape=[jax.ShapeDtypeStruct((T, SSD_W), BF16),
                   jax.ShapeDtypeStruct((B * nc, SSD_STATE, SSD_W), F32)],
        scratch_shapes=[pltpu.VMEM((SSD_STATE, SSD_W), F32)],
        compiler_params=_cp(("parallel", "arbitrary")),
    )(pre, proj, proj, par, e_mat, psel)


def _ssd_bwd(pre, proj, sall, dy, par, B, S, name):
    T = B * S
    nc = S // CHUNK
    e_mat, psel = _ssd_consts()

    def body(pre_ref, z_ref, dt_ref, sall_ref, dy_ref, par_ref, e_ref, p_ref,
             dpre_ref, dz_ref, ddt_ref, dpar_ref, ds):
        b, c = pl.program_id(0), pl.program_id(1)

        @pl.when(c == 0)
        def _():
            ds[...] = jnp.zeros_like(ds)

        @pl.when((b == 0) & (c == 0))
        def _():
            dpar_ref[...] = jnp.zeros_like(dpar_ref)

        e_v, p_v = e_ref[...], p_ref[...]
        fn = lambda pre, z, dtr, sprev, par: _ssd_chunk(pre, z, dtr, sprev, par, e_v, p_v)
        _, vjp = jax.vjp(fn, pre_ref[...], z_ref[...], dt_ref[...], sall_ref[...], par_ref[...])
        dpre, dz, ddt, dsp, dpar = vjp((dy_ref[...], ds[...]))
        dpre_ref[...] = dpre
        dz_ref[...] = dz.astype(BF16)
        ddt_ref[...] = ddt.astype(BF16)
        dpar_ref[...] += dpar
        ds[...] = dsp

    row = lambda b, c: b * nc + (nc - 1 - c)
    full = lambda shp: pl.BlockSpec(shp, lambda b, c: (0, 0))
    return pl.pallas_call(
        body, name=name, grid=(B, nc),
        in_specs=[pl.BlockSpec((CHUNK, SSD_CONV_DIM), lambda b, c: (row(b, c), 0)),
                  pl.BlockSpec((CHUNK, SSD_W), lambda b, c: (row(b, c), Z0 // SSD_W)),
                  pl.BlockSpec((CHUNK, LANE), lambda b, c: (row(b, c), DT0 // LANE)),
                  pl.BlockSpec((None, SSD_STATE, SSD_W), lambda b, c: (row(b, c), 0, 0)),
                  pl.BlockSpec((CHUNK, SSD_W), lambda b, c: (row(b, c), ATT_W // SSD_W)),
                  full((SUBLANE, SSD_W)), full((LANE, SSD_W)), full((SUBLANE, SSD_W))],
        out_specs=[pl.BlockSpec((CHUNK, SSD_CONV_DIM), lambda b, c: (row(b, c), 0)),
                   pl.BlockSpec((CHUNK, SSD_W), lambda b, c: (row(b, c), 0)),
                   pl.BlockSpec((CHUNK, LANE), lambda b, c: (row(b, c), 0)),
                   full((SUBLANE, SSD_W))],
        out_shape=[jax.ShapeDtypeStruct((T, SSD_CONV_DIM), F32),
                   jax.ShapeDtypeStruct((T, SSD_W), BF16),
                   jax.ShapeDtypeStruct((T, LANE), BF16),
                   jax.ShapeDtypeStruct((SUBLANE, SSD_W), F32)],
        scratch_shapes=[pltpu.VMEM((SSD_STATE, SSD_W), F32)],
        compiler_params=_cp(("arbitrary", "arbitrary")),
    )(pre, proj, proj, sall, dy, par, e_mat, psel)


def _sgu_consts():
    e = np.zeros((SUBLANE, SGU_W), np.float32)
    for g in range(SGU_GROUPS):
        e[g, HEAD_DIM * g:HEAD_DIM * (g + 1)] = 1.0
    return jnp.asarray(e)


def _sgu_chunk(u_raw, v_raw, ln, w, bst, e4):
    L = CHUNK
    u = _gelu(u_raw)
    v = _gelu(v_raw)
    mu = jnp.mean(v, axis=-1, keepdims=True)
    vc = v - mu
    var = jnp.mean(vc * vc, axis=-1, keepdims=True)
    vn = vc * lax.rsqrt(var + LN_EPS) * ln[0:1] + ln[1:2]
    vb = vn.astype(BF16)
    ri = lax.broadcasted_iota(jnp.int32, (L, L), 0)
    ci = lax.broadcasted_iota(jnp.int32, (L, L), 1)
    tril = ri >= ci
    col = lax.broadcasted_iota(jnp.int32, (1, SGU_W), 1)
    mixed = _dot(bst, e4, HI)
    for g in range(SGU_GROUPS):
        wc = jnp.where(tril, w[g], 0.0).astype(BF16)
        gm = (col >= HEAD_DIM * g) & (col < HEAD_DIM * (g + 1))
        mixed = mixed + jnp.where(gm, _dot(wc, vb), 0.0)
    return u * mixed


def _sgu_fwd(proj, ln, w, bst, B, S, name):
    T = B * S
    nc = S // CHUNK
    e4 = _sgu_consts()

    def body(u_ref, v_ref, ln_ref, w_ref, b_ref, e_ref, y_ref):
        y_ref[...] = _sgu_chunk(u_ref[...], v_ref[...], ln_ref[...], w_ref[...], b_ref[...], e_ref[...]).astype(BF16)

    return pl.pallas_call(
        body, name=name, grid=(T // CHUNK,),
        in_specs=[pl.BlockSpec((CHUNK, SGU_W), lambda i: (i, U0 // SGU_W)),
                  pl.BlockSpec((CHUNK, SGU_W), lambda i: (i, VS0 // SGU_W)),
                  pl.BlockSpec((SUBLANE, SGU_W), lambda i: (0, 0)),
                  pl.BlockSpec((SGU_GROUPS, CHUNK, CHUNK), lambda i: (0, 0, 0)),
                  pl.BlockSpec((CHUNK, SUBLANE), lambda i: (0, 0)),
                  pl.BlockSpec((SUBLANE, SGU_W), lambda i: (0, 0))],
        out_specs=pl.BlockSpec((CHUNK, SGU_W), lambda i: (i, 0)),
        out_shape=jax.ShapeDtypeStruct((T, SGU_W), BF16),
        compiler_params=_cp(("parallel",)),
    )(proj, proj, ln, w, bst, e4)


def _sgu_bwd(proj, dy, ln, w, bst, B, S, name):
    T = B * S
    e4 = _sgu_consts()
    ycol = (ATT_W + SSD_W) // SGU_W

    def body(u_ref, v_ref, dy_ref, ln_ref, w_ref, b_ref, e_ref, du_ref, dv_ref, dln_ref, dw_ref, db_ref):
        @pl.when(pl.program_id(0) == 0)
        def _():
            dln_ref[...] = jnp.zeros_like(dln_ref)
            dw_ref[...] = jnp.zeros_like(dw_ref)
            db_ref[...] = jnp.zeros_like(db_ref)

        e_v = e_ref[...]
        fn = lambda u, v, ln, w, b: _sgu_chunk(u, v, ln, w, b, e_v)
        _, vjp = jax.vjp(fn, u_ref[...], v_ref[...], ln_ref[...], w_ref[...], b_ref[...])
        du, dv, dln, dw, db = vjp(dy_ref[...])
        du_ref[...] = du.astype(BF16)
        dv_ref[...] = dv.astype(BF16)
        dln_ref[...] += dln
        dw_ref[...] += dw
        db_ref[...] += db

    c_ln = pl.BlockSpec((SUBLANE, SGU_W), lambda i: (0, 0))
    c_w = pl.BlockSpec((SGU_GROUPS, CHUNK, CHUNK), lambda i: (0, 0, 0))
    c_b = pl.BlockSpec((CHUNK, SUBLANE), lambda i: (0, 0))
    return pl.pallas_call(
        body, name=name, grid=(T // CHUNK,),
        in_specs=[pl.BlockSpec((CHUNK, SGU_W), lambda i: (i, U0 // SGU_W)),
                  pl.BlockSpec((CHUNK, SGU_W), lambda i: (i, VS0 // SGU_W)),
                  pl.BlockSpec((CHUNK, SGU_W), lambda i: (i, ycol)),
                  c_ln, c_w, c_b, pl.BlockSpec((SUBLANE, SGU_W), lambda i: (0, 0))],
        out_specs=[pl.BlockSpec((CHUNK, SGU_W), lambda i: (i, 0)),
                   pl.BlockSpec((CHUNK, SGU_W), lambda i: (i, 0)), c_ln, c_w, c_b],
        out_shape=[jax.ShapeDtypeStruct((T, SGU_W), BF16), jax.ShapeDtypeStruct((T, SGU_W), BF16),
                   jax.ShapeDtypeStruct((SUBLANE, SGU_W), F32),
                   jax.ShapeDtypeStruct((SGU_GROUPS, CHUNK, CHUNK), F32),
                   jax.ShapeDtypeStruct((CHUNK, SUBLANE), F32)],
        compiler_params=_cp(("arbitrary",)),
    )(proj, proj, dy, ln, w, bst, e4)


_HBM = pl.BlockSpec(memory_space=pltpu.HBM)
_SEM = pl.BlockSpec(memory_space=pltpu.SEMAPHORE)
_ANY = pl.BlockSpec(memory_space=pl.ANY)
_EFFECT = pltpu.SideEffectType.DATAFLOW_SIDE_EFFECTING


def _peers():
    x, y, c = lax.axis_index("x"), lax.axis_index("y"), lax.axis_index("c")
    out = []
    for p in range(1, N_DEV):
        px, py, pc = x ^ ((p >> 2) & 1), y ^ ((p >> 1) & 1), c ^ (p & 1)
        out.append(((px, py, pc), 4 * px + 2 * py + pc))
    return 4 * x + 2 * y + c, out


def _xchg_start(xs, a2a, order, name):
    n = len(xs)
    lands = [lax.empty(a.shape if f else (N_DEV,) + a.shape, a.dtype) for a, f in zip(xs, a2a)]

    def body(*refs):
        ins, zones = refs[:n], refs[n:2 * n]
        send_sems, recv_sems = refs[2 * n + 1], refs[2 * n + 2]
        token = refs[-1]
        me, peers = _peers()
        for p, (dev, peer) in enumerate(peers):
            for t in range(n):
                pltpu.make_async_remote_copy(
                    src_ref=ins[t].at[peer] if a2a[t] else ins[t], dst_ref=zones[t].at[me],
                    send_sem=send_sems.at[p * n + t], recv_sem=recv_sems.at[p * n + t],
                    device_id=dev, device_id_type=MESH).start()
        token[...] = jnp.zeros_like(token)

    hbm = lambda a: pltpu.HBM(a.shape, a.dtype)
    sems = pltpu.SemaphoreType.DMA(((N_DEV - 1) * n,))
    out = pl.pallas_call(
        body, name=name,
        in_specs=[_HBM] * (2 * n) + [_ANY],
        out_specs=[_SEM, _SEM] + [_HBM] * (2 * n) + [pl.BlockSpec(memory_space=pltpu.VMEM)],
        out_shape=[sems, sems] + [hbm(a) for a in xs] + [hbm(a) for a in lands]
        + [jax.ShapeDtypeStruct((SUBLANE, LANE), F32)],
        input_output_aliases={t: 2 + t for t in range(2 * n)},
        compiler_params=pltpu.CompilerParams(has_side_effects=_EFFECT),
    )(*[pltpu.with_memory_space_constraint(a, pltpu.HBM) for a in list(xs) + list(lands)], order)
    return out[0], out[1], out[2:2 + n], out[2 + n:2 + 2 * n], out[-1]


def _xchg_wait(started, a2a, after, name):
    send_sems, recv_sems, xs, lands, _ = started
    n = len(xs)

    def body(*refs):
        ins, zones = refs[:n], refs[n:2 * n]
        send_s, recv_s = refs[2 * n], refs[2 * n + 1]
        me, peers = _peers()
        cps = []
        for p, (dev, peer) in enumerate(peers):
            for t in range(n):
                cps.append(pltpu.make_async_remote_copy(
                    src_ref=ins[t].at[peer] if a2a[t] else ins[t], dst_ref=zones[t].at[peer],
                    send_sem=send_s.at[p * n + t], recv_sem=recv_s.at[p * n + t],
                    device_id=dev, device_id_type=MESH))
        for cp in cps:
            cp.wait_recv()
        for cp in cps:
            cp.wait_send()

    hbm = lambda a: pltpu.HBM(a.shape, a.dtype)
    out = pl.pallas_call(
        body, name=name,
        in_specs=[_HBM] * (2 * n) + [_SEM, _SEM, _ANY],
        out_specs=[_HBM] * (2 * n),
        out_shape=[hbm(a) for a in xs] + [hbm(a) for a in lands],
        input_output_aliases={t: t for t in range(2 * n)},
        compiler_params=pltpu.CompilerParams(has_side_effects=_EFFECT),
    )(*xs, *lands, send_sems, recv_sems, after)
    return out[:n], out[n:]


def _adamw(me, w, m, v, parts, own, name, layer=0, into=None):
    L, R, C = w.shape
    P = parts.shape[0]
    tr = R
    t = 16
    while t <= R:
        if R % t == 0 and t * C <= 131072:
            tr = t
        t += 16
    if tr == R and R * C > 131072 and R % 16 == 0:
        tr = 16
    own_all = own.shape[0] == P

    def body(me_ref, w_ref, m_ref, v_ref, p_ref, own_ref, *rest):
        g_ref, d_ref, mo_ref, vo_ref = rest[-4:]
        mine = own_ref[...].astype(F32)
        g = None
        for p in range(P):
            term = jnp.where(me_ref[0] == p, mine, p_ref[p].astype(F32))
            g = term if g is None else g + term
        mn = ADAM_B1 * m_ref[...] + (1.0 - ADAM_B1) * g
        vn = ADAM_B2 * v_ref[...] + (1.0 - ADAM_B2) * (g * g)
        m_hat = mn / (1.0 - ADAM_B1 ** ADAM_STEP)
        v_hat = vn / (1.0 - ADAM_B2 ** ADAM_STEP)
        g_ref[...] = g
        d_ref[...] = -ADAM_LR * (m_hat / (jnp.sqrt(v_hat) + ADAM_EPS) + ADAM_WD * w_ref[...])
        mo_ref[...] = mn
        vo_ref[...] = vn

    blk = pl.BlockSpec((None, tr, C), lambda i, me_ref: (layer, i, 0))
    own_blk = pl.BlockSpec((None, tr, C), lambda i, me_ref: (me_ref[0] if own_all else 0, i, 0))
    prev = list(into) if into is not None else []
    return pl.pallas_call(
        body, name=name,
        grid_spec=pltpu.PrefetchScalarGridSpec(
            num_scalar_prefetch=1, grid=(R // tr,),
            in_specs=[blk, blk, blk, pl.BlockSpec((P, tr, C), lambda i, me_ref: (0, i, 0)), own_blk]
            + [_ANY] * len(prev),
            out_specs=[blk] * 4),
        out_shape=[jax.ShapeDtypeStruct((L, R, C), F32)] * 4,
        input_output_aliases={6 + i: i for i in range(len(prev))},
        compiler_params=_cp(("parallel",)),
    )(me, w, m, v, parts, own, *prev)


def _perm_cols(w):
    pad = jnp.zeros((w.shape[0], LANE - SSD_HEADS), w.dtype)
    return jnp.concatenate([w[:, 0:1536], w[:, 2438:2694], w[:, 1536:2432], w[:, 2432:2438], pad,
                            w[:, 2694:2950]], axis=1)


def _unperm_cols(w):
    return jnp.concatenate([w[:, 0:1536], w[:, XBC0:XBC0 + SSD_CONV_DIM], w[:, DT0:DT0 + SSD_HEADS],
                            w[:, U0:U0 + SGU_W], w[:, VS0:VS0 + SGU_W]], axis=1)


_SMALL = ("ffn1_norm", "mix_norm", "conv_w", "conv_b", "dt_bias", "a_log", "d_skip", "ssd_norm",
          "sgu_ln_g", "sgu_ln_b", "sgu_w", "sgu_b", "ffn2_norm", "final_norm", "loss")


def _pack(d):
    v = jnp.concatenate([d[k].astype(F32).reshape(-1) for k in _SMALL])
    n = v.shape[0]
    npad = -(-n // (LANE * 16)) * (LANE * 16)
    return jnp.pad(v, (0, npad - n)).reshape(npad // LANE, LANE)


def _unpack(p, shapes):
    v = p.reshape(-1)
    out, o = {}, 0
    for k in _SMALL:
        n = int(np.prod(shapes[k]))
        out[k] = v[o:o + n].reshape(shapes[k])
        o += n
    return out


def kernel(x, ffn1_norm, ffn1_w_gate, ffn1_w_up, ffn1_w_down, mix_norm, w_in, conv_w, conv_b, dt_bias, a_log, d_skip, ssd_norm, sgu_ln_g, sgu_ln_b, sgu_w, sgu_b, w_out, ffn2_norm, ffn2_w_gate, ffn2_w_up, ffn2_w_down, final_norm, loss_target, m_ffn1_norm, m_ffn1_w_gate, m_ffn1_w_up, m_ffn1_w_down, m_mix_norm, m_w_in, m_conv_w, m_conv_b, m_dt_bias, m_a_log, m_d_skip, m_ssd_norm, m_sgu_ln_g, m_sgu_ln_b, m_sgu_w, m_sgu_b, m_w_out, m_ffn2_norm, m_ffn2_w_gate, m_ffn2_w_up, m_ffn2_w_down, m_final_norm, v_ffn1_norm, v_ffn1_w_gate, v_ffn1_w_up, v_ffn1_w_down, v_mix_norm, v_w_in, v_conv_w, v_conv_b, v_dt_bias, v_a_log, v_d_skip, v_ssd_norm, v_sgu_ln_g, v_sgu_ln_b, v_sgu_w, v_sgu_b, v_w_out, v_ffn2_norm, v_ffn2_w_gate, v_ffn2_w_up, v_ffn2_w_down, v_final_norm):
    B, S, D = x.shape
    T = B * S
    L = ffn1_norm.shape[0]
    me = 4 * lax.axis_index("x") + 2 * lax.axis_index("y") + lax.axis_index("c")
    cs = conv_w.shape[2]
    W = dict(ffn1_norm=ffn1_norm, ffn1_w_gate=ffn1_w_gate, ffn1_w_up=ffn1_w_up, ffn1_w_down=ffn1_w_down,
             mix_norm=mix_norm, w_in=w_in, conv_w=conv_w, conv_b=conv_b, dt_bias=dt_bias, a_log=a_log,
             d_skip=d_skip, ssd_norm=ssd_norm, sgu_ln_g=sgu_ln_g, sgu_ln_b=sgu_ln_b, sgu_w=sgu_w, sgu_b=sgu_b,
             w_out=w_out, ffn2_norm=ffn2_norm, ffn2_w_gate=ffn2_w_gate, ffn2_w_up=ffn2_w_up,
             ffn2_w_down=ffn2_w_down, final_norm=final_norm)
    M = dict(ffn1_norm=m_ffn1_norm, ffn1_w_gate=m_ffn1_w_gate, ffn1_w_up=m_ffn1_w_up, ffn1_w_down=m_ffn1_w_down,
             mix_norm=m_mix_norm, w_in=m_w_in, conv_w=m_conv_w, conv_b=m_conv_b, dt_bias=m_dt_bias, a_log=m_a_log,
             d_skip=m_d_skip, ssd_norm=m_ssd_norm, sgu_ln_g=m_sgu_ln_g, sgu_ln_b=m_sgu_ln_b, sgu_w=m_sgu_w,
             sgu_b=m_sgu_b, w_out=m_w_out, ffn2_norm=m_ffn2_norm, ffn2_w_gate=m_ffn2_w_gate,
             ffn2_w_up=m_ffn2_w_up, ffn2_w_down=m_ffn2_w_down, final_norm=m_final_norm)
    V = dict(ffn1_norm=v_ffn1_norm, ffn1_w_gate=v_ffn1_w_gate, ffn1_w_up=v_ffn1_w_up, ffn1_w_down=v_ffn1_w_down,
             mix_norm=v_mix_norm, w_in=v_w_in, conv_w=v_conv_w, conv_b=v_conv_b, dt_bias=v_dt_bias, a_log=v_a_log,
             d_skip=v_d_skip, ssd_norm=v_ssd_norm, sgu_ln_g=v_sgu_ln_g, sgu_ln_b=v_sgu_ln_b, sgu_w=v_sgu_w,
             sgu_b=v_sgu_b, w_out=v_w_out, ffn2_norm=v_ffn2_norm, ffn2_w_gate=v_ffn2_w_gate,
             ffn2_w_up=v_ffn2_w_up, ffn2_w_down=v_ffn2_w_down, final_norm=v_final_norm)
    FFN1 = ("ffn1_w_gate", "ffn1_w_up", "ffn1_w_down")
    FFN2 = ("ffn2_w_gate", "ffn2_w_up", "ffn2_w_down")
    MIX = ("w_in", "w_out")
    big = FFN1 + MIX + FFN2

    wgroups = [[(k, 0) for k in FFN1], [("w_in", 0), ("conv_w", None)], [("w_out", 0)] + [(k, 0) for k in FFN2]]
    wgroups += [[(k, l) for k in big] for l in range(1, L)]
    wstarted, order = [], x
    for gi, grp in enumerate(wgroups):
        xs = [conv_w if k == "conv_w" else W[k][l].astype(BF16) for k, l in grp]
        st = _xchg_start(xs, [False] * len(xs), order, f"gather_start_{gi}")
        order = st[-1]
        wstarted.append(st)
    G = {}
    is_me = (jnp.arange(N_DEV) == me)

    def gathered(gi, after):
        own, lands = _xchg_wait(wstarted[gi], [False] * len(wgroups[gi]), after, f"gather_wait_{gi}")
        for key, o, z in zip(wgroups[gi], own, lands):
            G[key] = jnp.where(is_me.reshape((N_DEV,) + (1,) * o.ndim), o[None], z)

    def cols(k, l):
        a = G[(k, l)]
        return jnp.transpose(a, (1, 0, 2)).reshape(a.shape[1], -1)

    def rows(k, l):
        a = G[(k, l)]
        return a.reshape(-1, a.shape[-1])

    bias = _attn_bias(S, min(256, S))
    row1 = lambda a: a.reshape(1, -1)

    def ffn1_params(l):
        return dict(g1=row1(ffn1_norm[l]), wg1=cols("ffn1_w_gate", l), wu1=cols("ffn1_w_up", l),
                    wd1=rows("ffn1_w_down", l))

    def out_params(l):
        return dict(wout=rows("w_out", l), g2=row1(ffn2_norm[l]), wg2=cols("ffn2_w_gate", l),
                    wu2=cols("ffn2_w_up", l), wd2=rows("ffn2_w_down", l))

    def mix_params(l):
        cw = jnp.transpose(G[("conv_w", None)][:, l], (1, 0, 2)).reshape(SSD_CONV, -1)
        return dict(
            gm=row1(mix_norm[l]), win=_perm_cols(rows("w_in", l)),
            cw=jnp.pad(cw, ((0, SUBLANE - SSD_CONV), (0, 0))), cb=row1(conv_b[l]),
            par=jnp.pad(jnp.stack([jnp.repeat(dt_bias[l], HEAD_DIM), jnp.repeat(a_log[l], HEAD_DIM),
                                   jnp.repeat(d_skip[l], HEAD_DIM), ssd_norm[l]]), ((0, SUBLANE - 4), (0, 0))),
            ln=jnp.pad(jnp.stack([sgu_ln_g[l], sgu_ln_b[l]]), ((0, SUBLANE - 2), (0, 0))),
            sw=sgu_w[l], bst=jnp.pad(sgu_b[l].T, ((0, 0), (0, SUBLANE - SGU_GROUPS))))

    xc = x.reshape(T, D)
    saved, lay = [], []
    for l in range(L):
        gathered(0 if l == 0 else l + 2, order if l == 0 else xc)
        p = ffn1_params(l)
        x1, gate1, up1 = _ffn_fwd(xc, p["g1"], p["wg1"], p["wu1"], p["wd1"], f"ffn1_fwd_{l}")
        if l == 0:
            gathered(1, x1)
        p.update(mix_params(l))
        lay.append(p)
        proj, ht = _norm_mm(x1, p["gm"], p["win"], f"in_proj_{l}")
        o_att, lse = _attn_fwd(proj, bias, B, S, f"attn_fwd_{l}")
        pre = _conv_fwd(proj, p["cw"], p["cb"], B, S, f"conv_fwd_{l}")
        y_ssd, sall = _ssd_fwd(pre, proj, p["par"], B, S, f"ssd_fwd_{l}")
        y_sgu = _sgu_fwd(proj, p["ln"], p["sw"], p["bst"], B, S, f"sgu_fwd_{l}")
        ycat = jnp.concatenate([o_att.astype(BF16), y_ssd, y_sgu], axis=1)
        if l == 0:
            gathered(2, ycat)
        p.update(out_params(l))
        x2 = _mm(ycat, p["wout"], "nn", f"out_proj_{l}", residual=x1)
        x3, gate2, up2 = _ffn_fwd(x2, p["g2"], p["wg2"], p["wu2"], p["wd2"], f"ffn2_fwd_{l}")
        saved.append(dict(x0=xc, gate1=gate1, up1=up1, x1=x1, ht=ht, proj=proj, o_att=o_att, lse=lse, pre=pre,
                          sall=sall, ycat=ycat, x2=x2, gate2=gate2, up2=up2))
        xc = x3
    loss_part, dx, dgf = _final_loss(xc, row1(final_norm), loss_target.reshape(T, D), "final_loss")

    gl = [dict() for _ in range(L)]
    gstarted, gorder = [], [order]

    def to_blocks(k, a):
        if k.endswith("w_gate") or k.endswith("w_up"):
            a = a.reshape(a.shape[0], N_DEV, -1).transpose(1, 0, 2)
        elif k.endswith("w_down"):
            a = a.reshape(a.shape[0], N_DEV, -1).transpose(1, 2, 0)
        else:
            a = a.reshape(N_DEV, -1, a.shape[-1])
        return a.astype(BF16)

    def send_grads(keys, l, extra, tag):
        xs = [to_blocks(k, gl[l][k]) for k in keys] + extra
        flags = [True] * len(keys) + [False] * len(extra)
        st = _xchg_start(xs, flags, gorder[0], f"grads_start_{tag}")
        gorder[0] = st[-1]
        gstarted.append((keys, l, st, flags, tag))

    def behind(a):
        return lax.optimization_barrier((a, gorder[0]))[0]

    for l in reversed(range(L)):
        p, s, g = lay[l], saved[l], gl[l]
        dx2, dgate, dup, act, xnt, dacct, g["ffn2_norm"] = _ffn_bwd_dx(
            dx, s["x2"], p["g2"], s["gate2"], s["up2"], p["wg2"], p["wu2"], p["wd2"], f"ffn2_bwd_{l}")
        g["ffn2_w_gate"], g["ffn2_w_up"], g["ffn2_w_down"] = _ffn_dw(xnt, dacct, dgate, dup, act, f"ffn2_dw_{l}")
        dycat = _mm(dx2, p["wout"], "nt", f"out_proj_dx_{l}")
        g["w_out"] = _mm(s["ycat"], dx2, "tn", f"out_proj_dw_{l}", tm_cap=1024, tk_cap=512)
        dq, dk, dv = _attn_bwd(s["proj"], s["o_att"], s["lse"], dycat, bias, B, S, f"attn_bwd_{l}")
        dpre, dz, ddt, dpar = _ssd_bwd(s["pre"], s["proj"], s["sall"], dycat, p["par"], B, S, f"ssd_bwd_{l}")
        dxbc, dwb = _conv_bwd(dpre, s["proj"], p["cw"], B, S, f"conv_bwd_{l}")
        du, dvs, dln, dsw, dbst = _sgu_bwd(s["proj"], dycat, p["ln"], p["sw"], p["bst"], B, S, f"sgu_bwd_{l}")
        dproj = jnp.concatenate([dq, dk, dv, dz, du, dxbc, ddt, dvs], axis=1)
        g["w_in"] = _unperm_cols(_mm_resident_lhs(s["ht"], dproj, f"in_proj_dw_{l}"))
        dx1, g["mix_norm"] = _norm_mm_bwd(dproj, s["x1"], p["gm"], p["win"], dx2, f"in_proj_bwd_{l}")
        if l == 0:
            send_grads(MIX + FFN2, 0, [], "l0a")
            dx1 = behind(dx1)
        dx, dgate, dup, act, xnt, dacct, g["ffn1_norm"] = _ffn_bwd_dx(
            dx1, s["x0"], p["g1"], s["gate1"], s["up1"], p["wg1"], p["wu1"], p["wd1"], f"ffn1_bwd_{l}")
        g["ffn1_w_gate"], g["ffn1_w_up"], g["ffn1_w_down"] = _ffn_dw(xnt, dacct, dgate, dup, act, f"ffn1_dw_{l}")
        if l > 0:
            send_grads(big, l, [], f"l{l}")
            dx = behind(dx)
        hsum = lambda r: r.reshape(SSD_HEADS, HEAD_DIM).sum(-1)
        g["conv_w"], g["conv_b"] = dwb[:SSD_CONV], dwb[SSD_CONV]
        g["dt_bias"], g["a_log"], g["d_skip"], g["ssd_norm"] = hsum(dpar[0]), hsum(dpar[1]), hsum(dpar[2]), dpar[3]
        g["sgu_ln_g"], g["sgu_ln_b"], g["sgu_w"], g["sgu_b"] = dln[0], dln[1], dsw, dbst[:, :SGU_GROUPS].T
    grad_x = dx.reshape(B, S, D)

    stack = lambda k: jnp.stack([gl[l][k] for l in range(L)])
    zero1 = jnp.zeros((1,), F32)
    W["loss"], M["loss"], V["loss"] = zero1, zero1, zero1
    per_layer = lambda k: k not in ("final_norm", "conv_w", "loss")
    small = {k: stack(k) for k in _SMALL if per_layer(k) or k == "conv_w"}
    small["final_norm"], small["loss"] = dgf.reshape(-1), loss_part[0, :1]
    small = {k: small[k].reshape((L,) + W[k].shape[1:]) if per_layer(k) else small[k] for k in _SMALL}
    full_shapes = {k: (W[k].shape if k != "conv_w" else (L, SSD_CONV, SSD_CONV_DIM)) for k in _SMALL}
    send_grads(FFN1, 0, [_pack(small)], "l0b")

    res, after, small_parts = {}, gorder[0], None
    me1 = me.reshape(1).astype(jnp.int32)
    for keys, l, st, flags, tag in gstarted:
        own, lands = _xchg_wait(st, flags, after, f"grads_wait_{tag}")
        for k, mine, pk in zip(keys, own, lands):
            res[k] = _adamw(me1, W[k], M[k], V[k], pk, mine, f"adamw_{k}_{l}", layer=l, into=res.get(k))
            after = res[k][0]
        if len(lands) > len(keys):
            small_parts = (lands[-1], own[-1][None])
    grads, deltas, new_m, new_v = [{k: res[k][i] for k in big} for i in range(4)]

    def embed(a, k):
        if k != "conv_w":
            return a
        return lax.dynamic_update_slice(jnp.zeros(full_shapes[k], F32), a, (0, 0, me * cs))

    outs = _adamw(me1, _pack({k: embed(W[k], k) for k in _SMALL})[None],
                  _pack({k: embed(M[k], k) for k in _SMALL})[None],
                  _pack({k: embed(V[k], k) for k in _SMALL})[None], small_parts[0], small_parts[1], "adamw_small")
    for d, o in zip((grads, deltas, new_m, new_v), outs):
        u = _unpack(o, full_shapes)
        u["conv_w"] = lax.dynamic_slice(u["conv_w"], (0, 0, me * cs), (L, SSD_CONV, cs))
        d.update(u)

    names = ("ffn1_norm", "ffn1_w_gate", "ffn1_w_up", "ffn1_w_down", "mix_norm", "w_in", "conv_w", "conv_b",
             "dt_bias", "a_log", "d_skip", "ssd_norm", "sgu_ln_g", "sgu_ln_b", "sgu_w", "sgu_b", "w_out",
             "ffn2_norm", "ffn2_w_gate", "ffn2_w_up", "ffn2_w_down", "final_norm")
    loss = grads["loss"][0]
    return (loss, grad_x, *[grads[n] for n in names], *[deltas[n] for n in names],
            *[new_m[n] for n in names], *[new_v[n] for n in names])
```

```python
import functools

import numpy as np
import jax
import jax.numpy as jnp
from jax import lax
from jax.experimental import pallas as pl
from jax.experimental.pallas import tpu as pltpu

F32, BF16 = jnp.float32, jnp.bfloat16
HI = lax.Precision.HIGHEST
MESH = pl.DeviceIdType.MESH
N_DEV = 8
VMEM_LIMIT_BYTES = 56 * 1024 * 1024
LANE, SUBLANE = 128, 8

HEAD_DIM = 64
ATT_W = 384
SSD_W = 384
SSD_HEADS = 6
SSD_STATE = 128
SSD_CONV = 4
CHUNK = 128
SSD_CONV_DIM = 896
SGU_W = 256
SGU_GROUPS = 4
D_IN = 2950
RMS_EPS = 1e-6
LN_EPS = 1e-5
NEG = -1e30

PW = 3072
Q0, K0, V0, Z0, U0, XBC0, DT0, VS0 = 0, 384, 768, 1152, 1536, 1792, 2688, 2816

ADAM_LR, ADAM_B1, ADAM_B2, ADAM_EPS, ADAM_WD, ADAM_STEP = 0.001, 0.9, 0.999, 1e-08, 0.01, 10


def _cp(sem=None):
    return pltpu.CompilerParams(dimension_semantics=sem, vmem_limit_bytes=VMEM_LIMIT_BYTES)


def _tile(n, cap, mult=LANE):
    best = None
    t = mult
    while t <= min(n, cap):
        if n % t == 0:
            best = t
        t += mult
    return best if best is not None else n


def _dot(a, b, prec=None):
    return jnp.dot(a, b, preferred_element_type=F32, precision=prec)


def _dot_nt(a, b, prec=None):
    return lax.dot_general(a, b, (((1,), (1,)), ((), ())), preferred_element_type=F32, precision=prec)


def _dot_tn(a, b, prec=None):
    return lax.dot_general(a, b, (((0,), (0,)), ((), ())), preferred_element_type=F32, precision=prec)


def _sigmoid(x):
    return 1.0 / (1.0 + jnp.exp(-x))


def _silu(x):
    return x * _sigmoid(x)


def _gelu(x):
    return 0.5 * x * (1.0 + lax.erf(x * 0.7071067811865476))


def _softplus(x):
    return jnp.maximum(x, 0.0) + jnp.log(1.0 + jnp.exp(-jnp.abs(x)))


def _rms_fwd(x, g):
    rstd = lax.rsqrt(jnp.mean(x * x, axis=-1, keepdims=True) + RMS_EPS)
    xhat = x * rstd
    return xhat * g, xhat, rstd


def _rms_bwd(dy, xhat, rstd, g):
    dxhat = dy * g
    dx = rstd * (dxhat - xhat * jnp.mean(dxhat * xhat, axis=-1, keepdims=True))
    return dx, dy * xhat


def _resident(shape):
    return pl.BlockSpec(shape, lambda *_: (0,) * len(shape), pipeline_mode=pl.Buffered(1))


def _mm(a, b, mode, name, out_dtype=F32, residual=None, tm_cap=512, tn_cap=1024, tk_cap=1024):
    if mode == "nn":
        (M, K), (_, N) = a.shape, b.shape
    elif mode == "nt":
        (M, K), (N, _) = a.shape, b.shape
    else:
        (K, M), (_, N) = a.shape, b.shape
    tm, tn, tk = _tile(M, tm_cap), _tile(N, tn_cap), _tile(K, tk_cap)
    nk = K // tk
    if mode == "tn":
        a_spec = pl.BlockSpec((tk, tm), lambda i, j, k: (k, i))
    else:
        a_spec = pl.BlockSpec((tm, tk), lambda i, j, k: (i, k))
    if mode == "nt":
        b_spec = pl.BlockSpec((tn, tk), lambda i, j, k: (j, k))
    else:
        b_spec = pl.BlockSpec((tk, tn), lambda i, j, k: (k, j))
    o_spec = pl.BlockSpec((tm, tn), lambda i, j, k: (i, j))
    has_res = residual is not None

    def prod(a_ref, b_ref):
        av = a_ref[...].astype(BF16)
        bv = b_ref[...].astype(BF16)
        if mode == "nn":
            return _dot(av, bv)
        if mode == "nt":
            return _dot_nt(av, bv)
        return _dot_tn(av, bv)

    def body(*refs):
        a_ref, b_ref = refs[:2]
        r_ref = refs[2] if has_res else None
        o_ref = refs[2 + has_res]
        if nk == 1:
            o = prod(a_ref, b_ref)
            if has_res:
                o = r_ref[...] + o
            o_ref[...] = o.astype(out_dtype)
            return
        acc = refs[3 + has_res]
        k = pl.program_id(2)

        @pl.when(k == 0)
        def _():
            acc[...] = jnp.zeros_like(acc)

        acc[...] += prod(a_ref, b_ref)

        @pl.when(k == nk - 1)
        def _():
            o = acc[...]
            if has_res:
                o = r_ref[...] + o
            o_ref[...] = o.astype(out_dtype)

    ins = [a, b] + ([residual] if has_res else [])
    in_specs = [a_spec, b_spec] + ([o_spec] if has_res else [])
    return pl.pallas_call(
        body, name=name, grid=(M // tm, N // tn, nk),
        in_specs=in_specs, out_specs=o_spec,
        out_shape=jax.ShapeDtypeStruct((M, N), out_dtype),
        scratch_shapes=[pltpu.VMEM((tm, tn), F32)] if nk > 1 else [],
        compiler_params=_cp(("parallel", "parallel", "arbitrary")),
    )(*ins)


def _ffn_fwd(x, g, wg, wu, wd, name):
    T, D = x.shape
    F = wg.shape[1]
    tm = _tile(T, 512)

    def body(x_ref, g_ref, wg_ref, wu_ref, wd_ref, out_ref, gate_ref, up_ref):
        xv = x_ref[...]
        xn = _rms_fwd(xv, g_ref[...])[0].astype(BF16)
        gate = _dot(xn, wg_ref[...])
        up = _dot(xn, wu_ref[...])
        gate_ref[...] = gate.astype(BF16)
        up_ref[...] = up.astype(BF16)
        act = (_silu(gate) * up).astype(BF16)
        out_ref[...] = xv + 0.5 * _dot(act, wd_ref[...])

    row = lambda w: pl.BlockSpec((tm, w), lambda i: (i, 0))
    return pl.pallas_call(
        body, name=name, grid=(T // tm,),
        in_specs=[row(D), _resident((1, D)), _resident((D, F)), _resident((D, F)), _resident((F, D))],
        out_specs=[row(D), row(F), row(F)],
        out_shape=[jax.ShapeDtypeStruct((T, D), F32),
                   jax.ShapeDtypeStruct((T, F), BF16),
                   jax.ShapeDtypeStruct((T, F), BF16)],
        compiler_params=_cp(("parallel",)),
    )(x, g, wg, wu, wd)


def _ffn_bwd_dx(dout, x, g, gate, up, wg, wu, wd, name):
    T, D = x.shape
    F = wg.shape[1]
    tm, th = _tile(T, 256), _tile(F, 256)
    nj = F // th

    def body(dout_ref, x_ref, g_ref, gate_ref, up_ref, wg_ref, wu_ref, wd_ref,
             dx_ref, dgate_ref, dup_ref, act_ref, xnt_ref, dacct_ref, dg_ref):
        @pl.when(pl.program_id(0) == 0)
        def _():
            dg_ref[...] = jnp.zeros_like(dg_ref)

        gv = g_ref[...]
        dout_v = dout_ref[...]
        xn, xhat, rstd = _rms_fwd(x_ref[...], gv)
        xnt_ref[...] = xn.T.astype(BF16)
        dacc = 0.5 * dout_v
        dacct_ref[...] = dacc.T.astype(BF16)
        dact = _dot_nt(dacc.astype(BF16), wd_ref[...])
        gt = gate_ref[...].astype(F32)
        u = up_ref[...].astype(F32)
        sig = _sigmoid(gt)
        sl = gt * sig
        dgate = (dact * u * (sig * (1.0 + gt * (1.0 - sig)))).astype(BF16)
        dup = (dact * sl).astype(BF16)
        act = (sl * u).astype(BF16)
        for j in range(nj):
            cs = slice(j * th, (j + 1) * th)
            dgate_ref[j] = dgate[:, cs]
            dup_ref[j] = dup[:, cs]
            act_ref[j] = act[:, cs]
        dxn = _dot_nt(dgate, wg_ref[...]) + _dot_nt(dup, wu_ref[...])
        dx, dgrow = _rms_bwd(dxn, xhat, rstd, gv)
        dx_ref[...] = dout_v + dx
        dg_ref[...] += jnp.sum(dgrow, axis=0, keepdims=True)

    row = lambda w: pl.BlockSpec((tm, w), lambda i: (i, 0))
    tiled = pl.BlockSpec((nj, tm, th), lambda i: (0, i, 0))
    tr = pl.BlockSpec((D, tm), lambda i: (0, i))
    return pl.pallas_call(
        body, name=name, grid=(T // tm,),
        in_specs=[row(D), row(D), _resident((1, D)), row(F), row(F),
                  _resident((D, F)), _resident((D, F)), _resident((F, D))],
        out_specs=[row(D), tiled, tiled, tiled, tr, tr, pl.BlockSpec((1, D), lambda i: (0, 0))],
        out_shape=[jax.ShapeDtypeStruct((T, D), F32)] + [jax.ShapeDtypeStruct((nj, T, th), BF16)] * 3
        + [jax.ShapeDtypeStruct((D, T), BF16)] * 2 + [jax.ShapeDtypeStruct((1, D), F32)],
        compiler_params=_cp(("arbitrary",)),
    )(dout, x, g, gate, up, wg, wu, wd)


def _ffn_dw(xnt, dacct, dgate, dup, act, name):
    D, T = xnt.shape
    nj, _, th = dgate.shape

    def body(xnt_ref, dacct_ref, dgate_ref, dup_ref, act_ref, dwg_ref, dwu_ref, dwdt_ref):
        xv = xnt_ref[...]
        dwg_ref[...] = _dot(xv, dgate_ref[...])
        dwu_ref[...] = _dot(xv, dup_ref[...])
        dwdt_ref[...] = _dot(dacct_ref[...], act_ref[...])

    tile = pl.BlockSpec((None, T, th), lambda j: (j, 0, 0))
    out = pl.BlockSpec((D, th), lambda j: (0, j))
    return pl.pallas_call(
        body, name=name, grid=(nj,),
        in_specs=[_resident((D, T)), _resident((D, T)), tile, tile, tile],
        out_specs=[out, out, out], out_shape=[jax.ShapeDtypeStruct((D, nj * th), F32)] * 3,
        compiler_params=_cp(("parallel",)),
    )(xnt, dacct, dgate, dup, act)


def _norm_mm(x, g, w, name):
    T, D = x.shape
    N = w.shape[1]
    tm = _tile(T, 512)

    def body(x_ref, g_ref, w_ref, o_ref, ht_ref):
        xn = _rms_fwd(x_ref[...], g_ref[...])[0]
        ht_ref[...] = xn.T.astype(BF16)
        o_ref[...] = _dot(xn.astype(BF16), w_ref[...])

    return pl.pallas_call(
        body, name=name, grid=(T // tm,),
        in_specs=[pl.BlockSpec((tm, D), lambda i: (i, 0)), _resident((1, D)), _resident((D, N))],
        out_specs=[pl.BlockSpec((tm, N), lambda i: (i, 0)), pl.BlockSpec((D, tm), lambda i: (0, i))],
        out_shape=[jax.ShapeDtypeStruct((T, N), F32), jax.ShapeDtypeStruct((D, T), BF16)],
        compiler_params=_cp(("parallel",)),
    )(x, g, w)


def _norm_mm_bwd(dproj, x, g, w, dres, name):
    T, D = x.shape
    N = w.shape[1]
    tm = _tile(T, 512)

    def body(dp_ref, x_ref, g_ref, w_ref, dres_ref, dx_ref, dg_ref):
        @pl.when(pl.program_id(0) == 0)
        def _():
            dg_ref[...] = jnp.zeros_like(dg_ref)

        gv = g_ref[...]
        dh = _dot_nt(dp_ref[...], w_ref[...])
        _, xhat, rstd = _rms_fwd(x_ref[...], gv)
        dx, dgrow = _rms_bwd(dh, xhat, rstd, gv)
        dx_ref[...] = dres_ref[...] + dx
        dg_ref[...] += jnp.sum(dgrow, axis=0, keepdims=True)

    row = pl.BlockSpec((tm, D), lambda i: (i, 0))
    one = pl.BlockSpec((1, D), lambda i: (0, 0))
    return pl.pallas_call(
        body, name=name, grid=(T // tm,),
        in_specs=[pl.BlockSpec((tm, N), lambda i: (i, 0)), row, _resident((1, D)), _resident((D, N)), row],
        out_specs=[row, one],
        out_shape=[jax.ShapeDtypeStruct((T, D), F32), jax.ShapeDtypeStruct((1, D), F32)],
        compiler_params=_cp(("arbitrary",)),
    )(dproj, x, g, w, dres)


def _mm_resident_lhs(at, b, name, tn_cap=512):
    M, K = at.shape
    N = b.shape[1]
    tn = _tile(N, tn_cap)

    def body(a_ref, b_ref, o_ref):
        o_ref[...] = _dot(a_ref[...], b_ref[...])

    return pl.pallas_call(
        body, name=name, grid=(N // tn,),
        in_specs=[_resident((M, K)), pl.BlockSpec((K, tn), lambda j: (0, j))],
        out_specs=pl.BlockSpec((M, tn), lambda j: (0, j)),
        out_shape=jax.ShapeDtypeStruct((M, N), F32),
        compiler_params=_cp(("parallel",)),
    )(at, b)


def _final_loss(x, g, target, name):
    T, D = x.shape
    tm = _tile(T, 512)

    def body(x_ref, g_ref, t_ref, loss_ref, dx_ref, dg_ref):
        @pl.when(pl.program_id(0) == 0)
        def _():
            dg_ref[...] = jnp.zeros_like(dg_ref)
            loss_ref[...] = jnp.zeros_like(loss_ref)

        gv = g_ref[...]
        y, xhat, rstd = _rms_fwd(x_ref[...], gv)
        err = y - t_ref[...]
        part = 0.5 * jnp.sum(jnp.mean(err * err, axis=-1, keepdims=True), axis=0, keepdims=True)
        loss_ref[...] += jnp.broadcast_to(part, loss_ref.shape)
        dy = err * (1.0 / D)
        dx, dgrow = _rms_bwd(dy, xhat, rstd, gv)
        dx_ref[...] = dx
        dg_ref[...] += jnp.sum(dgrow, axis=0, keepdims=True)

    row = pl.BlockSpec((tm, D), lambda i: (i, 0))
    one = pl.BlockSpec((1, D), lambda i: (0, 0))
    return pl.pallas_call(
        body, name=name, grid=(T // tm,),
        in_specs=[row, one, row],
        out_specs=[pl.BlockSpec((1, LANE), lambda i: (0, 0)), row, one],
        out_shape=[jax.ShapeDtypeStruct((1, LANE), F32), jax.ShapeDtypeStruct((T, D), F32),
                   jax.ShapeDtypeStruct((1, D), F32)],
        compiler_params=_cp(("arbitrary",)),
    )(x, g, target)


def _attn_bias(S, bq):
    d = jnp.arange(bq)[:, None] - jnp.arange(S)[None, :] + (S // bq - 1) * bq
    ok = d >= 0
    mult = ((ok & (d <= 128)).astype(F32) + (ok & (d % 4 == 0) & (d <= 512)).astype(F32)
            + (ok & (d % 16 == 0) & (d <= 2048)).astype(F32))
    return jnp.where(mult > 0, jnp.log(jnp.maximum(mult, 1.0)), NEG).astype(F32)


def _attn_fwd(proj, bias, B, S, name):
    T = B * S
    bq = bias.shape[0]
    nb = S // bq
    qcol, kcol, vcol = Q0 // LANE, K0 // LANE, V0 // LANE

    def body(q_ref, k_ref, v_ref, t_ref, o_ref, lse_ref, ks, vs):
        for hh in range(2):
            sl = slice(HEAD_DIM * hh, HEAD_DIM * (hh + 1))
            ks[hh] = k_ref[:, sl].astype(BF16)
            vs[hh] = v_ref[:, sl].astype(BF16)
        for hh in range(2):
            sl = slice(HEAD_DIM * hh, HEAD_DIM * (hh + 1))
            for qb in range(nb):
                w, off, rows = bq * (qb + 1), (nb - 1 - qb) * bq, slice(qb * bq, (qb + 1) * bq)
                q = (q_ref[rows, sl] * 0.125).astype(BF16)
                s = _dot_nt(q, ks[hh, 0:w, :]) + t_ref[:, off:off + w]
                m = jnp.max(s, axis=-1, keepdims=True)
                p = jnp.exp(s - m)
                l = jnp.sum(p, axis=-1, keepdims=True)
                o_ref[rows, sl] = _dot(p.astype(BF16), vs[hh, 0:w, :]) / l
                lse_ref[rows, hh:hh + 1] = m + jnp.log(l)

    blk = lambda c0: pl.BlockSpec((S, LANE), lambda b, p: (b, c0 + p))
    return pl.pallas_call(
        body, name=name, grid=(B, ATT_W // LANE),
        in_specs=[blk(qcol), blk(kcol), blk(vcol), _resident((bq, S))],
        out_specs=[pl.BlockSpec((S, LANE), lambda b, p: (b, p)),
                   pl.BlockSpec((None, None, S, 2), lambda b, p: (b, p, 0, 0))],
        out_shape=[jax.ShapeDtypeStruct((T, ATT_W), F32),
                   jax.ShapeDtypeStruct((B, ATT_W // LANE, S, 2), F32)],
        scratch_shapes=[pltpu.VMEM((2, S, HEAD_DIM), BF16)] * 2,
        compiler_params=_cp(("parallel", "parallel")),
    )(proj, proj, proj, bias)


def _attn_bwd(proj, o, lse, dy, bias, B, S, name):
    T = B * S
    bq = bias.shape[0]
    nb = S // bq
    qcol, kcol, vcol = Q0 // LANE, K0 // LANE, V0 // LANE

    def body(q_ref, k_ref, v_ref, o_ref, lse_ref, do_ref, t_ref, dq_ref, dk_ref, dv_ref, ks, vs, dks, dvs):
        for hh in range(2):
            sl = slice(HEAD_DIM * hh, HEAD_DIM * (hh + 1))
            ks[hh] = k_ref[:, sl].astype(BF16)
            vs[hh] = v_ref[:, sl].astype(BF16)
        dks[...] = jnp.zeros_like(dks)
        dvs[...] = jnp.zeros_like(dvs)
        for hh in range(2):
            sl = slice(HEAD_DIM * hh, HEAD_DIM * (hh + 1))
            for qb in range(nb):
                w, off, rows = bq * (qb + 1), (nb - 1 - qb) * bq, slice(qb * bq, (qb + 1) * bq)
                q = (q_ref[rows, sl] * 0.125).astype(BF16)
                do = do_ref[rows, sl]
                dob = do.astype(BF16)
                delta = jnp.sum(do * o_ref[rows, sl], axis=-1, keepdims=True)
                k, v = ks[hh, 0:w, :], vs[hh, 0:w, :]
                s = _dot_nt(q, k) + t_ref[:, off:off + w]
                p = jnp.exp(s - lse_ref[rows, hh:hh + 1])
                ds = (p * (_dot_nt(dob, v) - delta)).astype(BF16)
                dq_ref[rows, sl] = (_dot(ds, k) * 0.125).astype(dq_ref.dtype)
                dks[hh, 0:w, :] += _dot_tn(ds, q)
                dvs[hh, 0:w, :] += _dot_tn(p.astype(BF16), dob)
            dk_ref[:, sl] = dks[hh].astype(dk_ref.dtype)
            dv_ref[:, sl] = dvs[hh].astype(dv_ref.dtype)

    blk = lambda c0: pl.BlockSpec((S, LANE), lambda b, p: (b, c0 + p))
    own = pl.BlockSpec((S, LANE), lambda b, p: (b, p))
    return pl.pallas_call(
        body, name=name, grid=(B, ATT_W // LANE),
        in_specs=[blk(qcol), blk(kcol), blk(vcol), own,
                  pl.BlockSpec((None, None, S, 2), lambda b, p: (b, p, 0, 0)), own, _resident((bq, S))],
        out_specs=[own, own, own],
        out_shape=[jax.ShapeDtypeStruct((T, ATT_W), BF16)] * 3,
        scratch_shapes=[pltpu.VMEM((2, S, HEAD_DIM), BF16)] * 2 + [pltpu.VMEM((2, S, HEAD_DIM), F32)] * 2,
        compiler_params=_cp(("parallel", "parallel")),
    )(proj, proj, proj, o, lse, dy, bias)


def _conv_fwd(proj, cw, cb, B, S, name):
    T = B * S
    nc = SSD_CONV_DIM // LANE
    c0 = XBC0 // LANE

    def body(x_ref, w_ref, b_ref, o_ref):
        x = x_ref[...]
        t = lax.broadcasted_iota(jnp.int32, (S, 1), 0)
        acc = b_ref[...] + w_ref[SSD_CONV - 1:SSD_CONV, :] * x
        for k in range(SSD_CONV - 1):
            sh = SSD_CONV - 1 - k
            xs = jnp.where(t >= sh, pltpu.roll(x, sh, 0), 0.0)
            acc = acc + w_ref[k:k + 1, :] * xs
        o_ref[...] = acc

    return pl.pallas_call(
        body, name=name, grid=(B, nc),
        in_specs=[pl.BlockSpec((S, LANE), lambda b, j: (b, c0 + j)),
                  pl.BlockSpec((SUBLANE, LANE), lambda b, j: (0, j)),
                  pl.BlockSpec((1, LANE), lambda b, j: (0, j))],
        out_specs=pl.BlockSpec((S, LANE), lambda b, j: (b, j)),
        out_shape=jax.ShapeDtypeStruct((T, SSD_CONV_DIM), F32),
        compiler_params=_cp(("parallel", "parallel")),
    )(proj, cw, cb)


def _conv_bwd(dpre, proj, cw, B, S, name):
    T = B * S
    nc = SSD_CONV_DIM // LANE
    c0 = XBC0 // LANE

    def body(d_ref, x_ref, w_ref, dx_ref, dwb_ref):
        @pl.when(pl.program_id(1) == 0)
        def _():
            dwb_ref[...] = jnp.zeros_like(dwb_ref)

        d = d_ref[...]
        x = x_ref[...]
        t = lax.broadcasted_iota(jnp.int32, (S, 1), 0)
        dx = w_ref[SSD_CONV - 1:SSD_CONV, :] * d
        rows = [None] * SUBLANE
        rows[SSD_CONV - 1] = jnp.sum(d * x, axis=0, keepdims=True)
        for k in range(SSD_CONV - 1):
            sh = SSD_CONV - 1 - k
            dx = dx + w_ref[k:k + 1, :] * jnp.where(t < S - sh, pltpu.roll(d, S - sh, 0), 0.0)
            xs = jnp.where(t >= sh, pltpu.roll(x, sh, 0), 0.0)
            rows[k] = jnp.sum(d * xs, axis=0, keepdims=True)
        rows[SSD_CONV] = jnp.sum(d, axis=0, keepdims=True)
        dx_ref[...] = dx.astype(BF16)
        r = lax.broadcasted_iota(jnp.int32, (SUBLANE, LANE), 0)
        upd = jnp.zeros((SUBLANE, LANE), F32)
        for k in range(SSD_CONV + 1):
            upd = upd + jnp.where(r == k, rows[k], 0.0)
        dwb_ref[...] += upd

    return pl.pallas_call(
        body, name=name, grid=(nc, B),
        in_specs=[pl.BlockSpec((S, LANE), lambda j, b: (b, j)),
                  pl.BlockSpec((S, LANE), lambda j, b: (b, c0 + j)),
                  pl.BlockSpec((SUBLANE, LANE), lambda j, b: (0, j))],
        out_specs=[pl.BlockSpec((S, LANE), lambda j, b: (b, j)),
                   pl.BlockSpec((SUBLANE, LANE), lambda j, b: (0, j))],
        out_shape=[jax.ShapeDtypeStruct((T, SSD_CONV_DIM), BF16),
                   jax.ShapeDtypeStruct((SUBLANE, SSD_CONV_DIM), F32)],
        compiler_params=_cp(("parallel", "arbitrary")),
    )(dpre, proj, cw)


def _ssd_consts():
    e = np.zeros((LANE, SSD_W), np.float32)
    p = np.zeros((SUBLANE, SSD_W), np.float32)
    for h in range(SSD_HEADS):
        e[h, HEAD_DIM * h:HEAD_DIM * (h + 1)] = 1.0
        p[h, HEAD_DIM * h] = 1.0
    return jnp.asarray(e), jnp.asarray(p)


def _ssd_chunk(pre, z, dtr, sprev, par, e_mat, psel):
    L = CHUNK
    xc = _silu(pre)
    xs, bm, cm = xc[:, :SSD_W], xc[:, SSD_W:SSD_W + 2 * SSD_STATE], xc[:, SSD_W + 2 * SSD_STATE:]
    dtb, alog, dskip, ng = par[0:1], par[1:2], par[2:3], par[3:4]
    dt = _softplus(_dot(dtr, e_mat, HI) + dtb)
    a = dt * (-jnp.exp(alog))
    X = xs * dt
    ri = lax.broadcasted_iota(jnp.int32, (L, L), 0)
    ci = lax.broadcasted_iota(jnp.int32, (L, L), 1)
    tril = ri >= ci
    acs = _dot(tril.astype(F32), a, HI)
    acs_t = _dot_nt(psel, acs, HI)
    ecs = jnp.exp(acs)
    alast = acs[L - 1:L, :]
    xd = (X * jnp.exp(alast - acs)).astype(BF16)
    xb = X.astype(BF16)
    col = lax.broadcasted_iota(jnp.int32, (1, SSD_W), 1)
    sb = sprev.astype(BF16)
    y = dskip * xs
    snew = sprev * jnp.exp(alast)
    for g in range(2):
        gmask = (col >= g * (SSD_W // 2)) & (col < (g + 1) * (SSD_W // 2))
        bg = bm[:, SSD_STATE * g:SSD_STATE * (g + 1)].astype(BF16)
        cg = cm[:, SSD_STATE * g:SSD_STATE * (g + 1)].astype(BF16)
        cb = _dot_nt(cg, bg)
        for j in range(3):
            h = 3 * g + j
            seg = acs[:, HEAD_DIM * h:HEAD_DIM * h + 1] - acs_t[h:h + 1, :]
            dec = jnp.exp(jnp.where(tril, seg, NEG))
            yh = _dot((cb * dec).astype(BF16), xb)
            hmask = (col >= HEAD_DIM * h) & (col < HEAD_DIM * (h + 1))
            y = y + jnp.where(hmask, yh, 0.0)
        y = y + jnp.where(gmask, _dot(cg, sb) * ecs, 0.0)
        snew = snew + jnp.where(gmask, _dot_tn(bg, xd), 0.0)
    yg = y * _silu(z)
    sq = yg * yg
    g0 = col < SSD_W // 2
    ms0 = jnp.sum(jnp.where(g0, sq, 0.0), axis=-1, keepdims=True) * (2.0 / SSD_W)
    ms1 = jnp.sum(jnp.where(g0, 0.0, sq), axis=-1, keepdims=True) * (2.0 / SSD_W)
    r = jnp.where(g0, lax.rsqrt(ms0 + RMS_EPS), lax.rsqrt(ms1 + RMS_EPS))
    return yg * r * ng, snew


def _ssd_fwd(pre, proj, par, B, S, name):
    T = B * S
    nc = S // CHUNK
    e_mat, psel = _ssd_consts()

    def body(pre_ref, z_ref, dt_ref, par_ref, e_ref, p_ref, y_ref, sall_ref, st):
        @pl.when(pl.program_id(1) == 0)
        def _():
            st[...] = jnp.zeros_like(st)

        sprev = st[...]
        sall_ref[...] = sprev
        y, snew = _ssd_chunk(pre_ref[...], z_ref[...], dt_ref[...], sprev, par_ref[...], e_ref[...], p_ref[...])
        y_ref[...] = y.astype(BF16)
        st[...] = snew

    row = lambda b, c: b * nc + c
    full = lambda shp: pl.BlockSpec(shp, lambda b, c: (0, 0))
    return pl.pallas_call(
        body, name=name, grid=(B, nc),
        in_specs=[pl.BlockSpec((CHUNK, SSD_CONV_DIM), lambda b, c: (row(b, c), 0)),
                  pl.BlockSpec((CHUNK, SSD_W), lambda b, c: (row(b, c), Z0 // SSD_W)),
                  pl.BlockSpec((CHUNK, LANE), lambda b, c: (row(b, c), DT0 // LANE)),
                  full((SUBLANE, SSD_W)), full((LANE, SSD_W)), full((SUBLANE, SSD_W))],
        out_specs=[pl.BlockSpec((CHUNK, SSD_W), lambda b, c: (row(b, c), 0)),
                   pl.BlockSpec((None, SSD_STATE, SSD_W), lambda b, c: (row(b, c), 0, 0))],
        out_shape=[jax.ShapeDtypeStruct((T, SSD_W), BF16),
                   jax.ShapeDtypeStruct((B * nc, SSD_STATE, SSD_W), F32)],
        scratch_shapes=[pltpu.VMEM((SSD_STATE, SSD_W), F32)],
        compiler_params=_cp(("parallel", "arbitrary")),
    )(pre, proj, proj, par, e_mat, psel)


def _ssd_bwd(pre, proj, sall, dy, par, B, S, name):
    T = B * S
    nc = S // CHUNK
    e_mat, psel = _ssd_consts()

    def body(pre_ref, z_ref, dt_ref, sall_ref, dy_ref, par_ref, e_ref, p_ref,
             dpre_ref, dz_ref, ddt_ref, dpar_ref, ds):
        b, c = pl.program_id(0), pl.program_id(1)

        @pl.when(c == 0)
        def _():
            ds[...] = jnp.zeros_like(ds)

        @pl.when((b == 0) & (c == 0))
        def _():
            dpar_ref[...] = jnp.zeros_like(dpar_ref)

        e_v, p_v = e_ref[...], p_ref[...]
        fn = lambda pre, z, dtr, sprev, par: _ssd_chunk(pre, z, dtr, sprev, par, e_v, p_v)
        _, vjp = jax.vjp(fn, pre_ref[...], z_ref[...], dt_ref[...], sall_ref[...], par_ref[...])
        dpre, dz, ddt, dsp, dpar = vjp((dy_ref[...], ds[...]))
        dpre_ref[...] = dpre
        dz_ref[...] = dz.astype(BF16)
        ddt_ref[...] = ddt.astype(BF16)
        dpar_ref[...] += dpar
        ds[...] = dsp

    row = lambda b, c: b * nc + (nc - 1 - c)
    full = lambda shp: pl.BlockSpec(shp, lambda b, c: (0, 0))
    return pl.pallas_call(
        body, name=name, grid=(B, nc),
        in_specs=[pl.BlockSpec((CHUNK, SSD_CONV_DIM), lambda b, c: (row(b, c), 0)),
                  pl.BlockSpec((CHUNK, SSD_W), lambda b, c: (row(b, c), Z0 // SSD_W)),
                  pl.BlockSpec((CHUNK, LANE), lambda b, c: (row(b, c), DT0 // LANE)),
                  pl.BlockSpec((None, SSD_STATE, SSD_W), lambda b, c: (row(b, c), 0, 0)),
                  pl.BlockSpec((CHUNK, SSD_W), lambda b, c: (row(b, c), ATT_W // SSD_W)),
                  full((SUBLANE, SSD_W)), full((LANE, SSD_W)), full((SUBLANE, SSD_W))],
        out_specs=[pl.BlockSpec((CHUNK, SSD_CONV_DIM), lambda b, c: (row(b, c), 0)),
                   pl.BlockSpec((CHUNK, SSD_W), lambda b, c: (row(b, c), 0)),
                   pl.BlockSpec((CHUNK, LANE), lambda b, c: (row(b, c), 0)),
                   full((SUBLANE, SSD_W))],
        out_shape=[jax.ShapeDtypeStruct((T, SSD_CONV_DIM), F32),
                   jax.ShapeDtypeStruct((T, SSD_W), BF16),
                   jax.ShapeDtypeStruct((T, LANE), BF16),
                   jax.ShapeDtypeStruct((SUBLANE, SSD_W), F32)],
        scratch_shapes=[pltpu.VMEM((SSD_STATE, SSD_W), F32)],
        compiler_params=_cp(("arbitrary", "arbitrary")),
    )(pre, proj, proj, sall, dy, par, e_mat, psel)


def _sgu_consts():
    e = np.zeros((SUBLANE, SGU_W), np.float32)
    for g in range(SGU_GROUPS):
        e[g, HEAD_DIM * g:HEAD_DIM * (g + 1)] = 1.0
    return jnp.asarray(e)


def _sgu_chunk(u_raw, v_raw, ln, w, bst, e4):
    L = CHUNK
    u = _gelu(u_raw)
    v = _gelu(v_raw)
    mu = jnp.mean(v, axis=-1, keepdims=True)
    vc = v - mu
    var = jnp.mean(vc * vc, axis=-1, keepdims=True)
    vn = vc * lax.rsqrt(var + LN_EPS) * ln[0:1] + ln[1:2]
    vb = vn.astype(BF16)
    ri = lax.broadcasted_iota(jnp.int32, (L, L), 0)
    ci = lax.broadcasted_iota(jnp.int32, (L, L), 1)
    tril = ri >= ci
    col = lax.broadcasted_iota(jnp.int32, (1, SGU_W), 1)
    mixed = _dot(bst, e4, HI)
    for g in range(SGU_GROUPS):
        wc = jnp.where(tril, w[g], 0.0).astype(BF16)
        gm = (col >= HEAD_DIM * g) & (col < HEAD_DIM * (g + 1))
        mixed = mixed + jnp.where(gm, _dot(wc, vb), 0.0)
    return u * mixed


def _sgu_fwd(proj, ln, w, bst, B, S, name):
    T = B * S
    nc = S // CHUNK
    e4 = _sgu_consts()

    def body(u_ref, v_ref, ln_ref, w_ref, b_ref, e_ref, y_ref):
        y_ref[...] = _sgu_chunk(u_ref[...], v_ref[...], ln_ref[...], w_ref[...], b_ref[...], e_ref[...]).astype(BF16)

    return pl.pallas_call(
        body, name=name, grid=(T // CHUNK,),
        in_specs=[pl.BlockSpec((CHUNK, SGU_W), lambda i: (i, U0 // SGU_W)),
                  pl.BlockSpec((CHUNK, SGU_W), lambda i: (i, VS0 // SGU_W)),
                  pl.BlockSpec((SUBLANE, SGU_W), lambda i: (0, 0)),
                  pl.BlockSpec((SGU_GROUPS, CHUNK, CHUNK), lambda i: (0, 0, 0)),
                  pl.BlockSpec((CHUNK, SUBLANE), lambda i: (0, 0)),
                  pl.BlockSpec((SUBLANE, SGU_W), lambda i: (0, 0))],
        out_specs=pl.BlockSpec((CHUNK, SGU_W), lambda i: (i, 0)),
        out_shape=jax.ShapeDtypeStruct((T, SGU_W), BF16),
        compiler_params=_cp(("parallel",)),
    )(proj, proj, ln, w, bst, e4)


def _sgu_bwd(proj, dy, ln, w, bst, B, S, name):
    T = B * S
    e4 = _sgu_consts()
    ycol = (ATT_W + SSD_W) // SGU_W

    def body(u_ref, v_ref, dy_ref, ln_ref, w_ref, b_ref, e_ref, du_ref, dv_ref, dln_ref, dw_ref, db_ref):
        @pl.when(pl.program_id(0) == 0)
        def _():
            dln_ref[...] = jnp.zeros_like(dln_ref)
            dw_ref[...] = jnp.zeros_like(dw_ref)
            db_ref[...] = jnp.zeros_like(db_ref)

        e_v = e_ref[...]
        fn = lambda u, v, ln, w, b: _sgu_chunk(u, v, ln, w, b, e_v)
        _, vjp = jax.vjp(fn, u_ref[...], v_ref[...], ln_ref[...], w_ref[...], b_ref[...])
        du, dv, dln, dw, db = vjp(dy_ref[...])
        du_ref[...] = du.astype(BF16)
        dv_ref[...] = dv.astype(BF16)
        dln_ref[...] += dln
        dw_ref[...] += dw
        db_ref[...] += db

    c_ln = pl.BlockSpec((SUBLANE, SGU_W), lambda i: (0, 0))
    c_w = pl.BlockSpec((SGU_GROUPS, CHUNK, CHUNK), lambda i: (0, 0, 0))
    c_b = pl.BlockSpec((CHUNK, SUBLANE), lambda i: (0, 0))
    return pl.pallas_call(
        body, name=name, grid=(T // CHUNK,),
        in_specs=[pl.BlockSpec((CHUNK, SGU_W), lambda i: (i, U0 // SGU_W)),
                  pl.BlockSpec((CHUNK, SGU_W), lambda i: (i, VS0 // SGU_W)),
                  pl.BlockSpec((CHUNK, SGU_W), lambda i: (i, ycol)),
                  c_ln, c_w, c_b, pl.BlockSpec((SUBLANE, SGU_W), lambda i: (0, 0))],
        out_specs=[pl.BlockSpec((CHUNK, SGU_W), lambda i: (i, 0)),
                   pl.BlockSpec((CHUNK, SGU_W), lambda i: (i, 0)), c_ln, c_w, c_b],
        out_shape=[jax.ShapeDtypeStruct((T, SGU_W), BF16), jax.ShapeDtypeStruct((T, SGU_W), BF16),
                   jax.ShapeDtypeStruct((SUBLANE, SGU_W), F32),
                   jax.ShapeDtypeStruct((SGU_GROUPS, CHUNK, CHUNK), F32),
                   jax.ShapeDtypeStruct((CHUNK, SUBLANE), F32)],
        compiler_params=_cp(("arbitrary",)),
    )(proj, proj, dy, ln, w, bst, e4)


_HBM = pl.BlockSpec(memory_space=pltpu.HBM)
_SEM = pl.BlockSpec(memory_space=pltpu.SEMAPHORE)
_ANY = pl.BlockSpec(memory_space=pl.ANY)
_EFFECT = pltpu.SideEffectType.DATAFLOW_SIDE_EFFECTING


def _peers():
    x, y, c = lax.axis_index("x"), lax.axis_index("y"), lax.axis_index("c")
    out = []
    for p in range(1, N_DEV):
        px, py, pc = x ^ ((p >> 2) & 1), y ^ ((p >> 1) & 1), c ^ (p & 1)
        out.append(((px, py, pc), 4 * px + 2 * py + pc))
    return 4 * x + 2 * y + c, out


def _xchg_start(xs, a2a, order, name):
    n = len(xs)
    lands = [lax.empty(a.shape if f else (N_DEV,) + a.shape, a.dtype) for a, f in zip(xs, a2a)]

    def body(*refs):
        ins, zones = refs[:n], refs[n:2 * n]
        send_sems, recv_sems = refs[2 * n + 1], refs[2 * n + 2]
        token = refs[-1]
        me, peers = _peers()
        for p, (dev, peer) in enumerate(peers):
            for t in range(n):
                pltpu.make_async_remote_copy(
                    src_ref=ins[t].at[peer] if a2a[t] else ins[t], dst_ref=zones[t].at[me],
                    send_sem=send_sems.at[p * n + t], recv_sem=recv_sems.at[p * n + t],
                    device_id=dev, device_id_type=MESH).start()
        token[...] = jnp.zeros_like(token)

    hbm = lambda a: pltpu.HBM(a.shape, a.dtype)
    sems = pltpu.SemaphoreType.DMA(((N_DEV - 1) * n,))
    out = pl.pallas_call(
        body, name=name,
        in_specs=[_HBM] * (2 * n) + [_ANY],
        out_specs=[_SEM, _SEM] + [_HBM] * (2 * n) + [pl.BlockSpec(memory_space=pltpu.VMEM)],
        out_shape=[sems, sems] + [hbm(a) for a in xs] + [hbm(a) for a in lands]
        + [jax.ShapeDtypeStruct((SUBLANE, LANE), F32)],
        input_output_aliases={t: 2 + t for t in range(2 * n)},
        compiler_params=pltpu.CompilerParams(has_side_effects=_EFFECT),
    )(*[pltpu.with_memory_space_constraint(a, pltpu.HBM) for a in list(xs) + list(lands)], order)
    return out[0], out[1], out[2:2 + n], out[2 + n:2 + 2 * n], out[-1]


def _xchg_wait(started, a2a, after, name):
    send_sems, recv_sems, xs, lands, _ = started
    n = len(xs)

    def body(*refs):
        ins, zones = refs[:n], refs[n:2 * n]
        send_s, recv_s = refs[2 * n], refs[2 * n + 1]
        me, peers = _peers()
        cps = []
        for p, (dev, peer) in enumerate(peers):
            for t in range(n):
                cps.append(pltpu.make_async_remote_copy(
                    src_ref=ins[t].at[peer] if a2a[t] else ins[t], dst_ref=zones[t].at[peer],
                    send_sem=send_s.at[p * n + t], recv_sem=recv_s.at[p * n + t],
                    device_id=dev, device_id_type=MESH))
        for cp in cps:
            cp.wait_recv()
        for cp in cps:
            cp.wait_send()

    hbm = lambda a: pltpu.HBM(a.shape, a.dtype)
    out = pl.pallas_call(
        body, name=name,
        in_specs=[_HBM] * (2 * n) + [_SEM, _SEM, _ANY],
        out_specs=[_HBM] * (2 * n),
        out_shape=[hbm(a) for a in xs] + [hbm(a) for a in lands],
        input_output_aliases={t: t for t in range(2 * n)},
        compiler_params=pltpu.CompilerParams(has_side_effects=_EFFECT),
    )(*xs, *lands, send_sems, recv_sems, after)
    return out[:n], out[n:]


def _adamw(me, w, m, v, parts, own, name, layer=0, into=None):
    L, R, C = w.shape
    P = parts.shape[0]
    tr = R
    t = 16
    while t <= R:
        if R % t == 0 and t * C <= 131072:
            tr = t
        t += 16
    if tr == R and R * C > 131072 and R % 16 == 0:
        tr = 16
    own_all = own.shape[0] == P

    def body(me_ref, w_ref, m_ref, v_ref, p_ref, own_ref, *rest):
        g_ref, d_ref, mo_ref, vo_ref = rest[-4:]
        mine = own_ref[...].astype(F32)
        g = None
        for p in range(P):
            term = jnp.where(me_ref[0] == p, mine, p_ref[p].astype(F32))
            g = term if g is None else g + term
        mn = ADAM_B1 * m_ref[...] + (1.0 - ADAM_B1) * g
        vn = ADAM_B2 * v_ref[...] + (1.0 - ADAM_B2) * (g * g)
        m_hat = mn / (1.0 - ADAM_B1 ** ADAM_STEP)
        v_hat = vn / (1.0 - ADAM_B2 ** ADAM_STEP)
        g_ref[...] = g
        d_ref[...] = -ADAM_LR * (m_hat / (jnp.sqrt(v_hat) + ADAM_EPS) + ADAM_WD * w_ref[...])
        mo_ref[...] = mn
        vo_ref[...] = vn

    blk = pl.BlockSpec((None, tr, C), lambda i, me_ref: (layer, i, 0))
    own_blk = pl.BlockSpec((None, tr, C), lambda i, me_ref: (me_ref[0] if own_all else 0, i, 0))
    prev = list(into) if into is not None else []
    return pl.pallas_call(
        body, name=name,
        grid_spec=pltpu.PrefetchScalarGridSpec(
            num_scalar_prefetch=1, grid=(R // tr,),
            in_specs=[blk, blk, blk, pl.BlockSpec((P, tr, C), lambda i, me_ref: (0, i, 0)), own_blk]
            + [_ANY] * len(prev),
            out_specs=[blk] * 4),
        out_shape=[jax.ShapeDtypeStruct((L, R, C), F32)] * 4,
        input_output_aliases={6 + i: i for i in range(len(prev))},
        compiler_params=_cp(("parallel",)),
    )(me, w, m, v, parts, own, *prev)


def _perm_cols(w):
    pad = jnp.zeros((w.shape[0], LANE - SSD_HEADS), w.dtype)
    return jnp.concatenate([w[:, 0:1536], w[:, 2438:2694], w[:, 1536:2432], w[:, 2432:2438], pad,
                            w[:, 2694:2950]], axis=1)


def _unperm_cols(w):
    return jnp.concatenate([w[:, 0:1536], w[:, XBC0:XBC0 + SSD_CONV_DIM], w[:, DT0:DT0 + SSD_HEADS],
                            w[:, U0:U0 + SGU_W], w[:, VS0:VS0 + SGU_W]], axis=1)


_SMALL = ("ffn1_norm", "mix_norm", "conv_w", "conv_b", "dt_bias", "a_log", "d_skip", "ssd_norm",
          "sgu_ln_g", "sgu_ln_b", "sgu_w", "sgu_b", "ffn2_norm", "final_norm", "loss")


def _pack(d):
    v = jnp.concatenate([d[k].astype(F32).reshape(-1) for k in _SMALL])
    n = v.shape[0]
    npad = -(-n // (LANE * 16)) * (LANE * 16)
    return jnp.pad(v, (0, npad - n)).reshape(npad // LANE, LANE)


def _unpack(p, shapes):
    v = p.reshape(-1)
    out, o = {}, 0
    for k in _SMALL:
        n = int(np.prod(shapes[k]))
        out[k] = v[o:o + n].reshape(shapes[k])
        o += n
    return out


def kernel(x, ffn1_norm, ffn1_w_gate, ffn1_w_up, ffn1_w_down, mix_norm, w_in, conv_w, conv_b, dt_bias, a_log, d_skip, ssd_norm, sgu_ln_g, sgu_ln_b, sgu_w, sgu_b, w_out, ffn2_norm, ffn2_w_gate, ffn2_w_up, ffn2_w_down, final_norm, loss_target, m_ffn1_norm, m_ffn1_w_gate, m_ffn1_w_up, m_ffn1_w_down, m_mix_norm, m_w_in, m_conv_w, m_conv_b, m_dt_bias, m_a_log, m_d_skip, m_ssd_norm, m_sgu_ln_g, m_sgu_ln_b, m_sgu_w, m_sgu_b, m_w_out, m_ffn2_norm, m_ffn2_w_gate, m_ffn2_w_up, m_ffn2_w_down, m_final_norm, v_ffn1_norm, v_ffn1_w_gate, v_ffn1_w_up, v_ffn1_w_down, v_mix_norm, v_w_in, v_conv_w, v_conv_b, v_dt_bias, v_a_log, v_d_skip, v_ssd_norm, v_sgu_ln_g, v_sgu_ln_b, v_sgu_w, v_sgu_b, v_w_out, v_ffn2_norm, v_ffn2_w_gate, v_ffn2_w_up, v_ffn2_w_down, v_final_norm):
    B, S, D = x.shape
    T = B * S
    L = ffn1_norm.shape[0]
    me = 4 * lax.axis_index("x") + 2 * lax.axis_index("y") + lax.axis_index("c")
    cs = conv_w.shape[2]
    W = dict(ffn1_norm=ffn1_norm, ffn1_w_gate=ffn1_w_gate, ffn1_w_up=ffn1_w_up, ffn1_w_down=ffn1_w_down,
             mix_norm=mix_norm, w_in=w_in, conv_w=conv_w, conv_b=conv_b, dt_bias=dt_bias, a_log=a_log,
             d_skip=d_skip, ssd_norm=ssd_norm, sgu_ln_g=sgu_ln_g, sgu_ln_b=sgu_ln_b, sgu_w=sgu_w, sgu_b=sgu_b,
             w_out=w_out, ffn2_norm=ffn2_norm, ffn2_w_gate=ffn2_w_gate, ffn2_w_up=ffn2_w_up,
             ffn2_w_down=ffn2_w_down, final_norm=final_norm)
    M = dict(ffn1_norm=m_ffn1_norm, ffn1_w_gate=m_ffn1_w_gate, ffn1_w_up=m_ffn1_w_up, ffn1_w_down=m_ffn1_w_down,
             mix_norm=m_mix_norm, w_in=m_w_in, conv_w=m_conv_w, conv_b=m_conv_b, dt_bias=m_dt_bias, a_log=m_a_log,
             d_skip=m_d_skip, ssd_norm=m_ssd_norm, sgu_ln_g=m_sgu_ln_g, sgu_ln_b=m_sgu_ln_b, sgu_w=m_sgu_w,
             sgu_b=m_sgu_b, w_out=m_w_out, ffn2_norm=m_ffn2_norm, ffn2_w_gate=m_ffn2_w_gate,
             ffn2_w_up=m_ffn2_w_up, ffn2_w_down=m_ffn2_w_down, final_norm=m_final_norm)
    V = dict(ffn1_norm=v_ffn1_norm, ffn1_w_gate=v_ffn1_w_gate, ffn1_w_up=v_ffn1_w_up, ffn1_w_down=v_ffn1_w_down,
             mix_norm=v_mix_norm, w_in=v_w_in, conv_w=v_conv_w, conv_b=v_conv_b, dt_bias=v_dt_bias, a_log=v_a_log,
             d_skip=v_d_skip, ssd_norm=v_ssd_norm, sgu_ln_g=v_sgu_ln_g, sgu_ln_b=v_sgu_ln_b, sgu_w=v_sgu_w,
             sgu_b=v_sgu_b, w_out=v_w_out, ffn2_norm=v_ffn2_norm, ffn2_w_gate=v_ffn2_w_gate,
             ffn2_w_up=v_ffn2_w_up, ffn2_w_down=v_ffn2_w_down, final_norm=v_final_norm)
    FFN1 = ("ffn1_w_gate", "ffn1_w_up", "ffn1_w_down")
    FFN2 = ("ffn2_w_gate", "ffn2_w_up", "ffn2_w_down")
    MIX = ("w_in", "w_out")
    big = FFN1 + MIX + FFN2

    wgroups = [[(k, 0) for k in FFN1], [("w_in", 0), ("conv_w", None)], [("w_out", 0)] + [(k, 0) for k in FFN2]]
    wgroups += [[(k, l) for k in big] for l in range(1, L)]
    wstarted, order = [], x
    for gi, grp in enumerate(wgroups):
        xs = [conv_w if k == "conv_w" else W[k][l].astype(BF16) for k, l in grp]
        st = _xchg_start(xs, [False] * len(xs), order, f"gather_start_{gi}")
        order = st[-1]
        wstarted.append(st)
    G = {}
    is_me = (jnp.arange(N_DEV) == me)

    def gathered(gi, after):
        own, lands = _xchg_wait(wstarted[gi], [False] * len(wgroups[gi]), after, f"gather_wait_{gi}")
        for key, o, z in zip(wgroups[gi], own, lands):
            G[key] = jnp.where(is_me.reshape((N_DEV,) + (1,) * o.ndim), o[None], z)

    def cols(k, l):
        a = G[(k, l)]
        return jnp.transpose(a, (1, 0, 2)).reshape(a.shape[1], -1)

    def rows(k, l):
        a = G[(k, l)]
        return a.reshape(-1, a.shape[-1])

    bias = _attn_bias(S, min(256, S))
    row1 = lambda a: a.reshape(1, -1)

    def ffn1_params(l):
        return dict(g1=row1(ffn1_norm[l]), wg1=cols("ffn1_w_gate", l), wu1=cols("ffn1_w_up", l),
                    wd1=rows("ffn1_w_down", l))

    def out_params(l):
        return dict(wout=rows("w_out", l), g2=row1(ffn2_norm[l]), wg2=cols("ffn2_w_gate", l),
                    wu2=cols("ffn2_w_up", l), wd2=rows("ffn2_w_down", l))

    def mix_params(l):
        cw = jnp.transpose(G[("conv_w", None)][:, l], (1, 0, 2)).reshape(SSD_CONV, -1)
        return dict(
            gm=row1(mix_norm[l]), win=_perm_cols(rows("w_in", l)),
            cw=jnp.pad(cw, ((0, SUBLANE - SSD_CONV), (0, 0))), cb=row1(conv_b[l]),
            par=jnp.pad(jnp.stack([jnp.repeat(dt_bias[l], HEAD_DIM), jnp.repeat(a_log[l], HEAD_DIM),
                                   jnp.repeat(d_skip[l], HEAD_DIM), ssd_norm[l]]), ((0, SUBLANE - 4), (0, 0))),
            ln=jnp.pad(jnp.stack([sgu_ln_g[l], sgu_ln_b[l]]), ((0, SUBLANE - 2), (0, 0))),
            sw=sgu_w[l], bst=jnp.pad(sgu_b[l].T, ((0, 0), (0, SUBLANE - SGU_GROUPS))))

    xc = x.reshape(T, D)
    saved, lay = [], []
    for l in range(L):
        gathered(0 if l == 0 else l + 2, order if l == 0 else xc)
        p = ffn1_params(l)
        x1, gate1, up1 = _ffn_fwd(xc, p["g1"], p["wg1"], p["wu1"], p["wd1"], f"ffn1_fwd_{l}")
        if l == 0:
            gathered(1, x1)
        p.update(mix_params(l))
        lay.append(p)
        proj, ht = _norm_mm(x1, p["gm"], p["win"], f"in_proj_{l}")
        o_att, lse = _attn_fwd(proj, bias, B, S, f"attn_fwd_{l}")
        pre = _conv_fwd(proj, p["cw"], p["cb"], B, S, f"conv_fwd_{l}")
        y_ssd, sall = _ssd_fwd(pre, proj, p["par"], B, S, f"ssd_fwd_{l}")
        y_sgu = _sgu_fwd(proj, p["ln"], p["sw"], p["bst"], B, S, f"sgu_fwd_{l}")
        ycat = jnp.concatenate([o_att.astype(BF16), y_ssd, y_sgu], axis=1)
        if l == 0:
            gathered(2, ycat)
        p.update(out_params(l))
        x2 = _mm(ycat, p["wout"], "nn", f"out_proj_{l}", residual=x1)
        x3, gate2, up2 = _ffn_fwd(x2, p["g2"], p["wg2"], p["wu2"], p["wd2"], f"ffn2_fwd_{l}")
        saved.append(dict(x0=xc, gate1=gate1, up1=up1, x1=x1, ht=ht, proj=proj, o_att=o_att, lse=lse, pre=pre,
                          sall=sall, ycat=ycat, x2=x2, gate2=gate2, up2=up2))
        xc = x3
    loss_part, dx, dgf = _final_loss(xc, row1(final_norm), loss_target.reshape(T, D), "final_loss")

    gl = [dict() for _ in range(L)]
    gstarted, gorder = [], [order]

    def to_blocks(k, a):
        if k.endswith("w_gate") or k.endswith("w_up"):
            a = a.reshape(a.shape[0], N_DEV, -1).transpose(1, 0, 2)
        elif k.endswith("w_down"):
            a = a.reshape(a.shape[0], N_DEV, -1).transpose(1, 2, 0)
        else:
            a = a.reshape(N_DEV, -1, a.shape[-1])
        return a.astype(BF16)

    def send_grads(keys, l, extra, tag):
        xs = [to_blocks(k, gl[l][k]) for k in keys] + extra
        flags = [True] * len(keys) + [False] * len(extra)
        st = _xchg_start(xs, flags, gorder[0], f"grads_start_{tag}")
        gorder[0] = st[-1]
        gstarted.append((keys, l, st, flags, tag))

    def behind(a):
        return lax.optimization_barrier((a, gorder[0]))[0]

    for l in reversed(range(L)):
        p, s, g = lay[l], saved[l], gl[l]
        dx2, dgate, dup, act, xnt, dacct, g["ffn2_norm"] = _ffn_bwd_dx(
            dx, s["x2"], p["g2"], s["gate2"], s["up2"], p["wg2"], p["wu2"], p["wd2"], f"ffn2_bwd_{l}")
        g["ffn2_w_gate"], g["ffn2_w_up"], g["ffn2_w_down"] = _ffn_dw(xnt, dacct, dgate, dup, act, f"ffn2_dw_{l}")
        dycat = _mm(dx2, p["wout"], "nt", f"out_proj_dx_{l}")
        g["w_out"] = _mm(s["ycat"], dx2, "tn", f"out_proj_dw_{l}", tm_cap=1024, tk_cap=512)
        dq, dk, dv = _attn_bwd(s["proj"], s["o_att"], s["lse"], dycat, bias, B, S, f"attn_bwd_{l}")
        dpre, dz, ddt, dpar = _ssd_bwd(s["pre"], s["proj"], s["sall"], dycat, p["par"], B, S, f"ssd_bwd_{l}")
        dxbc, dwb = _conv_bwd(dpre, s["proj"], p["cw"], B, S, f"conv_bwd_{l}")
        du, dvs, dln, dsw, dbst = _sgu_bwd(s["proj"], dycat, p["ln"], p["sw"], p["bst"], B, S, f"sgu_bwd_{l}")
        dproj = jnp.concatenate([dq, dk, dv, dz, du, dxbc, ddt, dvs], axis=1)
        g["w_in"] = _unperm_cols(_mm_resident_lhs(s["ht"], dproj, f"in_proj_dw_{l}"))
        dx1, g["mix_norm"] = _norm_mm_bwd(dproj, s["x1"], p["gm"], p["win"], dx2, f"in_proj_bwd_{l}")
        if l == 0:
            send_grads(MIX + FFN2, 0, [], "l0a")
            dx1 = behind(dx1)
        dx, dgate, dup, act, xnt, dacct, g["ffn1_norm"] = _ffn_bwd_dx(
            dx1, s["x0"], p["g1"], s["gate1"], s["up1"], p["wg1"], p["wu1"], p["wd1"], f"ffn1_bwd_{l}")
        g["ffn1_w_gate"], g["ffn1_w_up"], g["ffn1_w_down"] = _ffn_dw(xnt, dacct, dgate, dup, act, f"ffn1_dw_{l}")
        if l > 0:
            send_grads(big, l, [], f"l{l}")
            dx = behind(dx)
        hsum = lambda r: r.reshape(SSD_HEADS, HEAD_DIM).sum(-1)
        g["conv_w"], g["conv_b"] = dwb[:SSD_CONV], dwb[SSD_CONV]
        g["dt_bias"], g["a_log"], g["d_skip"], g["ssd_norm"] = hsum(dpar[0]), hsum(dpar[1]), hsum(dpar[2]), dpar[3]
        g["sgu_ln_g"], g["sgu_ln_b"], g["sgu_w"], g["sgu_b"] = dln[0], dln[1], dsw, dbst[:, :SGU_GROUPS].T
    grad_x = dx.reshape(B, S, D)

    stack = lambda k: jnp.stack([gl[l][k] for l in range(L)])
    zero1 = jnp.zeros((1,), F32)
    W["loss"], M["loss"], V["loss"] = zero1, zero1, zero1
    per_layer = lambda k: k not in ("final_norm", "conv_w", "loss")
    small = {k: stack(k) for k in _SMALL if per_layer(k) or k == "conv_w"}
    small["final_norm"], small["loss"] = dgf.reshape(-1), loss_part[0, :1]
    small = {k: small[k].reshape((L,) + W[k].shape[1:]) if per_layer(k) else small[k] for k in _SMALL}
    full_shapes = {k: (W[k].shape if k != "conv_w" else (L, SSD_CONV, SSD_CONV_DIM)) for k in _SMALL}
    send_grads(FFN1, 0, [_pack(small)], "l0b")

    res, after, small_parts = {}, gorder[0], None
    me1 = me.reshape(1).astype(jnp.int32)
    for keys, l, st, flags, tag in gstarted:
        own, lands = _xchg_wait(st, flags, after, f"grads_wait_{tag}")
        for k, mine, pk in zip(keys, own, lands):
            res[k] = _adamw(me1, W[k], M[k], V[k], pk, mine, f"adamw_{k}_{l}", layer=l, into=res.get(k))
        after = lax.optimization_barrier(tuple(res[k][0] for k in keys))[0]
        if len(lands) > len(keys):
            small_parts = (lands[-1], own[-1][None])
    grads, deltas, new_m, new_v = [{k: res[k][i] for k in big} for i in range(4)]

    def embed(a, k):
        if k != "conv_w":
            return a
        return lax.dynamic_update_slice(jnp.zeros(full_shapes[k], F32), a, (0, 0, me * cs))

    outs = _adamw(me1, _pack({k: embed(W[k], k) for k in _SMALL})[None],
                  _pack({k: embed(M[k], k) for k in _SMALL})[None],
                  _pack({k: embed(V[k], k) for k in _SMALL})[None], small_parts[0], small_parts[1], "adamw_small")
    for d, o in zip((grads, deltas, new_m, new_v), outs):
        u = _unpack(o, full_shapes)
        u["conv_w"] = lax.dynamic_slice(u["conv_w"], (0, 0, me * cs), (L, SSD_CONV, cs))
        d.update(u)

    names = ("ffn1_norm", "ffn1_w_gate", "ffn1_w_up", "ffn1_w_down", "mix_norm", "w_in", "conv_w", "conv_b",
             "dt_bias", "a_log", "d_skip", "ssd_norm", "sgu_ln_g", "sgu_ln_b", "sgu_w", "sgu_b", "w_out",
             "ffn2_norm", "ffn2_w_gate", "ffn2_w_up", "ffn2_w_down", "final_norm")
    loss = grads["loss"][0]
    return (loss, grad_x, *[grads[n] for n in names], *[deltas[n] for n in names],
            *[new_m[n] for n in names], *[new_v[n] for n in names])
```

```python
import functools

import numpy as np
import jax
import jax.numpy as jnp
from jax import lax
from jax.experimental import pallas as pl
from jax.experimental.pallas import tpu as pltpu

F32, BF16 = jnp.float32, jnp.bfloat16
HI = lax.Precision.HIGHEST
MESH = pl.DeviceIdType.MESH
N_DEV = 8
VMEM_LIMIT_BYTES = 56 * 1024 * 1024
LANE, SUBLANE = 128, 8

HEAD_DIM = 64
ATT_W = 384
SSD_W = 384
SSD_HEADS = 6
SSD_STATE = 128
SSD_CONV = 4
CHUNK = 128
SSD_CONV_DIM = 896
SGU_W = 256
SGU_GROUPS = 4
D_IN = 2950
RMS_EPS = 1e-6
LN_EPS = 1e-5
NEG = -1e30

PW = 3072
Q0, K0, V0, Z0, U0, XBC0, DT0, VS0 = 0, 384, 768, 1152, 1536, 1792, 2688, 2816

ADAM_LR, ADAM_B1, ADAM_B2, ADAM_EPS, ADAM_WD, ADAM_STEP = 0.001, 0.9, 0.999, 1e-08, 0.01, 10


def _cp(sem=None):
    return pltpu.CompilerParams(dimension_semantics=sem, vmem_limit_bytes=VMEM_LIMIT_BYTES)


def _tile(n, cap, mult=LANE):
    best = None
    t = mult
    while t <= min(n, cap):
        if n % t == 0:
            best = t
        t += mult
    return best if best is not None else n


def _dot(a, b, prec=None):
    return jnp.dot(a, b, preferred_element_type=F32, precision=prec)


def _dot_nt(a, b, prec=None):
    return lax.dot_general(a, b, (((1,), (1,)), ((), ())), preferred_element_type=F32, precision=prec)


def _dot_tn(a, b, prec=None):
    return lax.dot_general(a, b, (((0,), (0,)), ((), ())), preferred_element_type=F32, precision=prec)


def _sigmoid(x):
    return 1.0 / (1.0 + jnp.exp(-x))


def _silu(x):
    return x * _sigmoid(x)


def _gelu(x):
    return 0.5 * x * (1.0 + lax.erf(x * 0.7071067811865476))


def _softplus(x):
    return jnp.maximum(x, 0.0) + jnp.log(1.0 + jnp.exp(-jnp.abs(x)))


def _rms_fwd(x, g):
    rstd = lax.rsqrt(jnp.mean(x * x, axis=-1, keepdims=True) + RMS_EPS)
    xhat = x * rstd
    return xhat * g, xhat, rstd


def _rms_bwd(dy, xhat, rstd, g):
    dxhat = dy * g
    dx = rstd * (dxhat - xhat * jnp.mean(dxhat * xhat, axis=-1, keepdims=True))
    return dx, dy * xhat


def _resident(shape):
    return pl.BlockSpec(shape, lambda *_: (0,) * len(shape), pipeline_mode=pl.Buffered(1))


def _mm(a, b, mode, name, out_dtype=F32, residual=None, tm_cap=512, tn_cap=1024, tk_cap=1024):
    if mode == "nn":
        (M, K), (_, N) = a.shape, b.shape
    elif mode == "nt":
        (M, K), (N, _) = a.shape, b.shape
    else:
        (K, M), (_, N) = a.shape, b.shape
    tm, tn, tk = _tile(M, tm_cap), _tile(N, tn_cap), _tile(K, tk_cap)
    nk = K // tk
    if mode == "tn":
        a_spec = pl.BlockSpec((tk, tm), lambda i, j, k: (k, i))
    else:
        a_spec = pl.BlockSpec((tm, tk), lambda i, j, k: (i, k))
    if mode == "nt":
        b_spec = pl.BlockSpec((tn, tk), lambda i, j, k: (j, k))
    else:
        b_spec = pl.BlockSpec((tk, tn), lambda i, j, k: (k, j))
    o_spec = pl.BlockSpec((tm, tn), lambda i, j, k: (i, j))
    has_res = residual is not None

    def prod(a_ref, b_ref):
        av = a_ref[...].astype(BF16)
        bv = b_ref[...].astype(BF16)
        if mode == "nn":
            return _dot(av, bv)
        if mode == "nt":
            return _dot_nt(av, bv)
        return _dot_tn(av, bv)

    def body(*refs):
        a_ref, b_ref = refs[:2]
        r_ref = refs[2] if has_res else None
        o_ref = refs[2 + has_res]
        if nk == 1:
            o = prod(a_ref, b_ref)
            if has_res:
                o = r_ref[...] + o
            o_ref[...] = o.astype(out_dtype)
            return
        acc = refs[3 + has_res]
        k = pl.program_id(2)

        @pl.when(k == 0)
        def _():
            acc[...] = jnp.zeros_like(acc)

        acc[...] += prod(a_ref, b_ref)

        @pl.when(k == nk - 1)
        def _():
            o = acc[...]
            if has_res:
                o = r_ref[...] + o
            o_ref[...] = o.astype(out_dtype)

    ins = [a, b] + ([residual] if has_res else [])
    in_specs = [a_spec, b_spec] + ([o_spec] if has_res else [])
    return pl.pallas_call(
        body, name=name, grid=(M // tm, N // tn, nk),
        in_specs=in_specs, out_specs=o_spec,
        out_shape=jax.ShapeDtypeStruct((M, N), out_dtype),
        scratch_shapes=[pltpu.VMEM((tm, tn), F32)] if nk > 1 else [],
        compiler_params=_cp(("parallel", "parallel", "arbitrary")),
    )(*ins)


def _ffn_fwd(x, g, wg, wu, wd, name):
    T, D = x.shape
    F = wg.shape[1]
    tm = _tile(T, 512)

    def body(x_ref, g_ref, wg_ref, wu_ref, wd_ref, out_ref, gate_ref, up_ref):
        xv = x_ref[...]
        xn = _rms_fwd(xv, g_ref[...])[0].astype(BF16)
        gate = _dot(xn, wg_ref[...])
        up = _dot(xn, wu_ref[...])
        gate_ref[...] = gate.astype(BF16)
        up_ref[...] = up.astype(BF16)
        act = (_silu(gate) * up).astype(BF16)
        out_ref[...] = xv + 0.5 * _dot(act, wd_ref[...])

    row = lambda w: pl.BlockSpec((tm, w), lambda i: (i, 0))
    return pl.pallas_call(
        body, name=name, grid=(T // tm,),
        in_specs=[row(D), _resident((1, D)), _resident((D, F)), _resident((D, F)), _resident((F, D))],
        out_specs=[row(D), row(F), row(F)],
        out_shape=[jax.ShapeDtypeStruct((T, D), F32),
                   jax.ShapeDtypeStruct((T, F), BF16),
                   jax.ShapeDtypeStruct((T, F), BF16)],
        compiler_params=_cp(("parallel",)),
    )(x, g, wg, wu, wd)


def _ffn_bwd_dx(dout, x, g, gate, up, wg, wu, wd, name):
    T, D = x.shape
    F = wg.shape[1]
    tm, th = _tile(T, 256), _tile(F, 256)
    nj = F // th

    def body(dout_ref, x_ref, g_ref, gate_ref, up_ref, wg_ref, wu_ref, wd_ref,
             dx_ref, dgate_ref, dup_ref, act_ref, xnt_ref, dacct_ref, dg_ref):
        @pl.when(pl.program_id(0) == 0)
        def _():
            dg_ref[...] = jnp.zeros_like(dg_ref)

        gv = g_ref[...]
        dout_v = dout_ref[...]
        xn, xhat, rstd = _rms_fwd(x_ref[...], gv)
        xnt_ref[...] = xn.T.astype(BF16)
        dacc = 0.5 * dout_v
        dacct_ref[...] = dacc.T.astype(BF16)
        dact = _dot_nt(dacc.astype(BF16), wd_ref[...])
        gt = gate_ref[...].astype(F32)
        u = up_ref[...].astype(F32)
        sig = _sigmoid(gt)
        sl = gt * sig
        dgate = (dact * u * (sig * (1.0 + gt * (1.0 - sig)))).astype(BF16)
        dup = (dact * sl).astype(BF16)
        act = (sl * u).astype(BF16)
        for j in range(nj):
            cs = slice(j * th, (j + 1) * th)
            dgate_ref[j] = dgate[:, cs]
            dup_ref[j] = dup[:, cs]
            act_ref[j] = act[:, cs]
        dxn = _dot_nt(dgate, wg_ref[...]) + _dot_nt(dup, wu_ref[...])
        dx, dgrow = _rms_bwd(dxn, xhat, rstd, gv)
        dx_ref[...] = dout_v + dx
        dg_ref[...] += jnp.sum(dgrow, axis=0, keepdims=True)

    row = lambda w: pl.BlockSpec((tm, w), lambda i: (i, 0))
    tiled = pl.BlockSpec((nj, tm, th), lambda i: (0, i, 0))
    tr = pl.BlockSpec((D, tm), lambda i: (0, i))
    return pl.pallas_call(
        body, name=name, grid=(T // tm,),
        in_specs=[row(D), row(D), _resident((1, D)), row(F), row(F),
                  _resident((D, F)), _resident((D, F)), _resident((F, D))],
        out_specs=[row(D), tiled, tiled, tiled, tr, tr, pl.BlockSpec((1, D), lambda i: (0, 0))],
        out_shape=[jax.ShapeDtypeStruct((T, D), F32)] + [jax.ShapeDtypeStruct((nj, T, th), BF16)] * 3
        + [jax.ShapeDtypeStruct((D, T), BF16)] * 2 + [jax.ShapeDtypeStruct((1, D), F32)],
        compiler_params=_cp(("arbitrary",)),
    )(dout, x, g, gate, up, wg, wu, wd)


def _ffn_dw(xnt, dacct, dgate, dup, act, name):
    D, T = xnt.shape
    nj, _, th = dgate.shape

    def body(xnt_ref, dacct_ref, dgate_ref, dup_ref, act_ref, dwg_ref, dwu_ref, dwdt_ref):
        xv = xnt_ref[...]
        dwg_ref[...] = _dot(xv, dgate_ref[...]).astype(BF16)
        dwu_ref[...] = _dot(xv, dup_ref[...]).astype(BF16)
        dwdt_ref[...] = _dot(dacct_ref[...], act_ref[...]).astype(BF16)

    tile = pl.BlockSpec((None, T, th), lambda j: (j, 0, 0))
    out = pl.BlockSpec((D, th), lambda j: (0, j))
    return pl.pallas_call(
        body, name=name, grid=(nj,),
        in_specs=[_resident((D, T)), _resident((D, T)), tile, tile, tile],
        out_specs=[out, out, out], out_shape=[jax.ShapeDtypeStruct((D, nj * th), BF16)] * 3,
        compiler_params=_cp(("parallel",)),
    )(xnt, dacct, dgate, dup, act)


def _norm_mm(x, g, w, name):
    T, D = x.shape
    N = w.shape[1]
    tm = _tile(T, 512)

    def body(x_ref, g_ref, w_ref, o_ref, ht_ref):
        xn = _rms_fwd(x_ref[...], g_ref[...])[0]
        ht_ref[...] = xn.T.astype(BF16)
        o_ref[...] = _dot(xn.astype(BF16), w_ref[...])

    return pl.pallas_call(
        body, name=name, grid=(T // tm,),
        in_specs=[pl.BlockSpec((tm, D), lambda i: (i, 0)), _resident((1, D)), _resident((D, N))],
        out_specs=[pl.BlockSpec((tm, N), lambda i: (i, 0)), pl.BlockSpec((D, tm), lambda i: (0, i))],
        out_shape=[jax.ShapeDtypeStruct((T, N), F32), jax.ShapeDtypeStruct((D, T), BF16)],
        compiler_params=_cp(("parallel",)),
    )(x, g, w)


def _norm_mm_bwd(dproj, x, g, w, dres, name):
    T, D = x.shape
    N = w.shape[1]
    tm = _tile(T, 512)

    def body(dp_ref, x_ref, g_ref, w_ref, dres_ref, dx_ref, dg_ref):
        @pl.when(pl.program_id(0) == 0)
        def _():
            dg_ref[...] = jnp.zeros_like(dg_ref)

        gv = g_ref[...]
        dh = _dot_nt(dp_ref[...], w_ref[...])
        _, xhat, rstd = _rms_fwd(x_ref[...], gv)
        dx, dgrow = _rms_bwd(dh, xhat, rstd, gv)
        dx_ref[...] = dres_ref[...] + dx
        dg_ref[...] += jnp.sum(dgrow, axis=0, keepdims=True)

    row = pl.BlockSpec((tm, D), lambda i: (i, 0))
    one = pl.BlockSpec((1, D), lambda i: (0, 0))
    return pl.pallas_call(
        body, name=name, grid=(T // tm,),
        in_specs=[pl.BlockSpec((tm, N), lambda i: (i, 0)), row, _resident((1, D)), _resident((D, N)), row],
        out_specs=[row, one],
        out_shape=[jax.ShapeDtypeStruct((T, D), F32), jax.ShapeDtypeStruct((1, D), F32)],
        compiler_params=_cp(("arbitrary",)),
    )(dproj, x, g, w, dres)


def _mm_resident_lhs(at, b, name, tn_cap=512):
    M, K = at.shape
    N = b.shape[1]
    tn = _tile(N, tn_cap)

    def body(a_ref, b_ref, o_ref):
        o_ref[...] = _dot(a_ref[...], b_ref[...]).astype(BF16)

    return pl.pallas_call(
        body, name=name, grid=(N // tn,),
        in_specs=[_resident((M, K)), pl.BlockSpec((K, tn), lambda j: (0, j))],
        out_specs=pl.BlockSpec((M, tn), lambda j: (0, j)),
        out_shape=jax.ShapeDtypeStruct((M, N), BF16),
        compiler_params=_cp(("parallel",)),
    )(at, b)


def _final_loss(x, g, target, name):
    T, D = x.shape
    tm = _tile(T, 512)

    def body(x_ref, g_ref, t_ref, loss_ref, dx_ref, dg_ref):
        @pl.when(pl.program_id(0) == 0)
        def _():
            dg_ref[...] = jnp.zeros_like(dg_ref)
            loss_ref[...] = jnp.zeros_like(loss_ref)

        gv = g_ref[...]
        y, xhat, rstd = _rms_fwd(x_ref[...], gv)
        err = y - t_ref[...]
        part = 0.5 * jnp.sum(jnp.mean(err * err, axis=-1, keepdims=True), axis=0, keepdims=True)
        loss_ref[...] += jnp.broadcast_to(part, loss_ref.shape)
        dy = err * (1.0 / D)
        dx, dgrow = _rms_bwd(dy, xhat, rstd, gv)
        dx_ref[...] = dx
        dg_ref[...] += jnp.sum(dgrow, axis=0, keepdims=True)

    row = pl.BlockSpec((tm, D), lambda i: (i, 0))
    one = pl.BlockSpec((1, D), lambda i: (0, 0))
    return pl.pallas_call(
        body, name=name, grid=(T // tm,),
        in_specs=[row, one, row],
        out_specs=[pl.BlockSpec((1, LANE), lambda i: (0, 0)), row, one],
        out_shape=[jax.ShapeDtypeStruct((1, LANE), F32), jax.ShapeDtypeStruct((T, D), F32),
                   jax.ShapeDtypeStruct((1, D), F32)],
        compiler_params=_cp(("arbitrary",)),
    )(x, g, target)


def _attn_bias(S, bq):
    d = jnp.arange(bq)[:, None] - jnp.arange(S)[None, :] + (S // bq - 1) * bq
    ok = d >= 0
    mult = ((ok & (d <= 128)).astype(F32) + (ok & (d % 4 == 0) & (d <= 512)).astype(F32)
            + (ok & (d % 16 == 0) & (d <= 2048)).astype(F32))
    return jnp.where(mult > 0, jnp.log(jnp.maximum(mult, 1.0)), NEG).astype(F32)


def _attn_fwd(proj, bias, B, S, name):
    T = B * S
    bq = bias.shape[0]
    nb = S // bq
    qcol, kcol, vcol = Q0 // LANE, K0 // LANE, V0 // LANE

    def body(q_ref, k_ref, v_ref, t_ref, o_ref, lse_ref, ks, vs):
        for hh in range(2):
            sl = slice(HEAD_DIM * hh, HEAD_DIM * (hh + 1))
            ks[hh] = k_ref[:, sl].astype(BF16)
            vs[hh] = v_ref[:, sl].astype(BF16)
        for hh in range(2):
            sl = slice(HEAD_DIM * hh, HEAD_DIM * (hh + 1))
            for qb in range(nb):
                w, off, rows = bq * (qb + 1), (nb - 1 - qb) * bq, slice(qb * bq, (qb + 1) * bq)
                q = (q_ref[rows, sl] * 0.125).astype(BF16)
                s = _dot_nt(q, ks[hh, 0:w, :]) + t_ref[:, off:off + w]
                m = jnp.max(s, axis=-1, keepdims=True)
                p = jnp.exp(s - m)
                l = jnp.sum(p, axis=-1, keepdims=True)
                o_ref[rows, sl] = _dot(p.astype(BF16), vs[hh, 0:w, :]) / l
                lse_ref[rows, hh:hh + 1] = m + jnp.log(l)

    blk = lambda c0: pl.BlockSpec((S, LANE), lambda b, p: (b, c0 + p))
    return pl.pallas_call(
        body, name=name, grid=(B, ATT_W // LANE),
        in_specs=[blk(qcol), blk(kcol), blk(vcol), _resident((bq, S))],
        out_specs=[pl.BlockSpec((S, LANE), lambda b, p: (b, p)),
                   pl.BlockSpec((None, None, S, 2), lambda b, p: (b, p, 0, 0))],
        out_shape=[jax.ShapeDtypeStruct((T, ATT_W), F32),
                   jax.ShapeDtypeStruct((B, ATT_W // LANE, S, 2), F32)],
        scratch_shapes=[pltpu.VMEM((2, S, HEAD_DIM), BF16)] * 2,
        compiler_params=_cp(("parallel", "parallel")),
    )(proj, proj, proj, bias)


def _attn_bwd(proj, o, lse, dy, bias, B, S, name):
    T = B * S
    bq = bias.shape[0]
    nb = S // bq
    qcol, kcol, vcol = Q0 // LANE, K0 // LANE, V0 // LANE

    def body(q_ref, k_ref, v_ref, o_ref, lse_ref, do_ref, t_ref, dq_ref, dk_ref, dv_ref, ks, vs, dks, dvs):
        for hh in range(2):
            sl = slice(HEAD_DIM * hh, HEAD_DIM * (hh + 1))
            ks[hh] = k_ref[:, sl].astype(BF16)
            vs[hh] = v_ref[:, sl].astype(BF16)
        dks[...] = jnp.zeros_like(dks)
        dvs[...] = jnp.zeros_like(dvs)
        for hh in range(2):
            sl = slice(HEAD_DIM * hh, HEAD_DIM * (hh + 1))
            for qb in range(nb):
                w, off, rows = bq * (qb + 1), (nb - 1 - qb) * bq, slice(qb * bq, (qb + 1) * bq)
                q = (q_ref[rows, sl] * 0.125).astype(BF16)
                do = do_ref[rows, sl]
                dob = do.astype(BF16)
                delta = jnp.sum(do * o_ref[rows, sl], axis=-1, keepdims=True)
                k, v = ks[hh, 0:w, :], vs[hh, 0:w, :]
                s = _dot_nt(q, k) + t_ref[:, off:off + w]
                p = jnp.exp(s - lse_ref[rows, hh:hh + 1])
                ds = (p * (_dot_nt(dob, v) - delta)).astype(BF16)
                dq_ref[rows, sl] = (_dot(ds, k) * 0.125).astype(dq_ref.dtype)
                dks[hh, 0:w, :] += _dot_tn(ds, q)
                dvs[hh, 0:w, :] += _dot_tn(p.astype(BF16), dob)
            dk_ref[:, sl] = dks[hh].astype(dk_ref.dtype)
            dv_ref[:, sl] = dvs[hh].astype(dv_ref.dtype)

    blk = lambda c0: pl.BlockSpec((S, LANE), lambda b, p: (b, c0 + p))
    own = pl.BlockSpec((S, LANE), lambda b, p: (b, p))
    return pl.pallas_call(
        body, name=name, grid=(B, ATT_W // LANE),
        in_specs=[blk(qcol), blk(kcol), blk(vcol), own,
                  pl.BlockSpec((None, None, S, 2), lambda b, p: (b, p, 0, 0)), own, _resident((bq, S))],
        out_specs=[own, own, own],
        out_shape=[jax.ShapeDtypeStruct((T, ATT_W), BF16)] * 3,
        scratch_shapes=[pltpu.VMEM((2, S, HEAD_DIM), BF16)] * 2 + [pltpu.VMEM((2, S, HEAD_DIM), F32)] * 2,
        compiler_params=_cp(("parallel", "parallel")),
    )(proj, proj, proj, o, lse, dy, bias)


def _conv_fwd(proj, cw, cb, B, S, name):
    T = B * S
    nc = SSD_CONV_DIM // LANE
    c0 = XBC0 // LANE

    def body(x_ref, w_ref, b_ref, o_ref):
        x = x_ref[...]
        t = lax.broadcasted_iota(jnp.int32, (S, 1), 0)
        acc = b_ref[...] + w_ref[SSD_CONV - 1:SSD_CONV, :] * x
        for k in range(SSD_CONV - 1):
            sh = SSD_CONV - 1 - k
            xs = jnp.where(t >= sh, pltpu.roll(x, sh, 0), 0.0)
            acc = acc + w_ref[k:k + 1, :] * xs
        o_ref[...] = acc

    return pl.pallas_call(
        body, name=name, grid=(B, nc),
        in_specs=[pl.BlockSpec((S, LANE), lambda b, j: (b, c0 + j)),
                  pl.BlockSpec((SUBLANE, LANE), lambda b, j: (0, j)),
                  pl.BlockSpec((1, LANE), lambda b, j: (0, j))],
        out_specs=pl.BlockSpec((S, LANE), lambda b, j: (b, j)),
        out_shape=jax.ShapeDtypeStruct((T, SSD_CONV_DIM), F32),
        compiler_params=_cp(("parallel", "parallel")),
    )(proj, cw, cb)


def _conv_bwd(dpre, proj, cw, B, S, name):
    T = B * S
    nc = SSD_CONV_DIM // LANE
    c0 = XBC0 // LANE

    def body(d_ref, x_ref, w_ref, dx_ref, dwb_ref):
        @pl.when(pl.program_id(1) == 0)
        def _():
            dwb_ref[...] = jnp.zeros_like(dwb_ref)

        d = d_ref[...]
        x = x_ref[...]
        t = lax.broadcasted_iota(jnp.int32, (S, 1), 0)
        dx = w_ref[SSD_CONV - 1:SSD_CONV, :] * d
        rows = [None] * SUBLANE
        rows[SSD_CONV - 1] = jnp.sum(d * x, axis=0, keepdims=True)
        for k in range(SSD_CONV - 1):
            sh = SSD_CONV - 1 - k
            dx = dx + w_ref[k:k + 1, :] * jnp.where(t < S - sh, pltpu.roll(d, S - sh, 0), 0.0)
            xs = jnp.where(t >= sh, pltpu.roll(x, sh, 0), 0.0)
            rows[k] = jnp.sum(d * xs, axis=0, keepdims=True)
        rows[SSD_CONV] = jnp.sum(d, axis=0, keepdims=True)
        dx_ref[...] = dx.astype(BF16)
        r = lax.broadcasted_iota(jnp.int32, (SUBLANE, LANE), 0)
        upd = jnp.zeros((SUBLANE, LANE), F32)
        for k in range(SSD_CONV + 1):
            upd = upd + jnp.where(r == k, rows[k], 0.0)
        dwb_ref[...] += upd

    return pl.pallas_call(
        body, name=name, grid=(nc, B),
        in_specs=[pl.BlockSpec((S, LANE), lambda j, b: (b, j)),
                  pl.BlockSpec((S, LANE), lambda j, b: (b, c0 + j)),
                  pl.BlockSpec((SUBLANE, LANE), lambda j, b: (0, j))],
        out_specs=[pl.BlockSpec((S, LANE), lambda j, b: (b, j)),
                   pl.BlockSpec((SUBLANE, LANE), lambda j, b: (0, j))],
        out_shape=[jax.ShapeDtypeStruct((T, SSD_CONV_DIM), BF16),
                   jax.ShapeDtypeStruct((SUBLANE, SSD_CONV_DIM), F32)],
        compiler_params=_cp(("parallel", "arbitrary")),
    )(dpre, proj, cw)


def _ssd_consts():
    e = np.zeros((LANE, SSD_W), np.float32)
    p = np.zeros((SUBLANE, SSD_W), np.float32)
    for h in range(SSD_HEADS):
        e[h, HEAD_DIM * h:HEAD_DIM * (h + 1)] = 1.0
        p[h, HEAD_DIM * h] = 1.0
    return jnp.asarray(e), jnp.asarray(p)


def _ssd_chunk(pre, z, dtr, sprev, par, e_mat, psel):
    L = CHUNK
    xc = _silu(pre)
    xs, bm, cm = xc[:, :SSD_W], xc[:, SSD_W:SSD_W + 2 * SSD_STATE], xc[:, SSD_W + 2 * SSD_STATE:]
    dtb, alog, dskip, ng = par[0:1], par[1:2], par[2:3], par[3:4]
    dt = _softplus(_dot(dtr, e_mat, HI) + dtb)
    a = dt * (-jnp.exp(alog))
    X = xs * dt
    ri = lax.broadcasted_iota(jnp.int32, (L, L), 0)
    ci = lax.broadcasted_iota(jnp.int32, (L, L), 1)
    tril = ri >= ci
    acs = _dot(tril.astype(F32), a, HI)
    acs_t = _dot_nt(psel, acs, HI)
    ecs = jnp.exp(acs)
    alast = acs[L - 1:L, :]
    xd = (X * jnp.exp(alast - acs)).astype(BF16)
    xb = X.astype(BF16)
    col = lax.broadcasted_iota(jnp.int32, (1, SSD_W), 1)
    sb = sprev.astype(BF16)
    bgs = [bm[:, SSD_STATE * g:SSD_STATE * (g + 1)].astype(BF16) for g in range(2)]
    cgs = [cm[:, SSD_STATE * g:SSD_STATE * (g + 1)].astype(BF16) for g in range(2)]
    cbs = [_dot_nt(cgs[g], bgs[g]) for g in range(2)]
    first = lax.broadcasted_iota(jnp.int32, (1, LANE), 1) < HEAD_DIM
    y_tiles, s_tiles = [], []
    for t in range(SSD_W // LANE):
        cl = slice(LANE * t, LANE * (t + 1))
        xb_t, xd_t, sb_t = xb[:, cl], xd[:, cl], sb[:, cl]
        per_head = []
        for h in (2 * t, 2 * t + 1):
            seg = acs[:, HEAD_DIM * h:HEAD_DIM * h + 1] - acs_t[h:h + 1, :]
            dec = jnp.exp(jnp.where(tril, seg, NEG))
            per_head.append(_dot((cbs[h // 3] * dec).astype(BF16), xb_t))
        y_t = jnp.where(first, per_head[0], per_head[1])
        ga, gb = (2 * t) // 3, (2 * t + 1) // 3
        if ga == gb:
            y_off, s_add = _dot(cgs[ga], sb_t), _dot_tn(bgs[ga], xd_t)
        else:
            y_off = jnp.where(first, _dot(cgs[ga], sb_t), _dot(cgs[gb], sb_t))
            s_add = jnp.where(first, _dot_tn(bgs[ga], xd_t), _dot_tn(bgs[gb], xd_t))
        y_tiles.append(y_t + y_off * ecs[:, cl])
        s_tiles.append(s_add)
    y = dskip * xs + jnp.concatenate(y_tiles, axis=1)
    snew = sprev * jnp.exp(alast) + jnp.concatenate(s_tiles, axis=1)
    yg = y * _silu(z)
    sq = yg * yg
    g0 = col < SSD_W // 2
    ms0 = jnp.sum(jnp.where(g0, sq, 0.0), axis=-1, keepdims=True) * (2.0 / SSD_W)
    ms1 = jnp.sum(jnp.where(g0, 0.0, sq), axis=-1, keepdims=True) * (2.0 / SSD_W)
    r = jnp.where(g0, lax.rsqrt(ms0 + RMS_EPS), lax.rsqrt(ms1 + RMS_EPS))
    return yg * r * ng, snew


SSD_CHUNKS_PER_STEP = 2


def _ssd_chunks_per_step(S):
    k = SSD_CHUNKS_PER_STEP
    while (S // CHUNK) % k:
        k //= 2
    return k


def _ssd_fwd(pre, proj, par, B, S, name):
    T = B * S
    k = _ssd_chunks_per_step(S)
    nc, rows = S // (CHUNK * k), CHUNK * k
    e_mat, psel = _ssd_consts()

    def body(pre_ref, z_ref, dt_ref, par_ref, e_ref, p_ref, y_ref, sall_ref, st):
        @pl.when(pl.program_id(1) == 0)
        def _():
            st[...] = jnp.zeros_like(st)

        sprev = st[...]
        for i in range(k):
            r = slice(CHUNK * i, CHUNK * (i + 1))
            sall_ref[i] = sprev
            y, sprev = _ssd_chunk(pre_ref[r, :], z_ref[r, :], dt_ref[r, :], sprev, par_ref[...], e_ref[...],
                                  p_ref[...])
            y_ref[r, :] = y.astype(BF16)
        st[...] = sprev

    row = lambda b, c: b * nc + c
    full = lambda shp: pl.BlockSpec(shp, lambda b, c: (0, 0))
    return pl.pallas_call(
        body, name=name, grid=(B, nc),
        in_specs=[pl.BlockSpec((rows, SSD_CONV_DIM), lambda b, c: (row(b, c), 0)),
                  pl.BlockSpec((rows, SSD_W), lambda b, c: (row(b, c), Z0 // SSD_W)),
                  pl.BlockSpec((rows, LANE), lambda b, c: (row(b, c), DT0 // LANE)),
                  full((SUBLANE, SSD_W)), full((LANE, SSD_W)), full((SUBLANE, SSD_W))],
        out_specs=[pl.BlockSpec((rows, SSD_W), lambda b, c: (row(b, c), 0)),
                   pl.BlockSpec((k, SSD_STATE, SSD_W), lambda b, c: (row(b, c), 0, 0))],
        out_shape=[jax.ShapeDtypeStruct((T, SSD_W), BF16),
                   jax.ShapeDtypeStruct((B * nc * k, SSD_STATE, SSD_W), F32)],
        scratch_shapes=[pltpu.VMEM((SSD_STATE, SSD_W), F32)],
        compiler_params=_cp(("parallel", "arbitrary")),
    )(pre, proj, proj, par, e_mat, psel)


def _ssd_bwd(pre, proj, sall, dy, par, B, S, name):
    T = B * S
    k = _ssd_chunks_per_step(S)
    nc, rows = S // (CHUNK * k), CHUNK * k
    e_mat, psel = _ssd_consts()

    def body(pre_ref, z_ref, dt_ref, sall_ref, dy_ref, par_ref, e_ref, p_ref,
             dpre_ref, dz_ref, ddt_ref, dpar_ref, ds):
        b, c = pl.program_id(0), pl.program_id(1)

        @pl.when(c == 0)
        def _():
            ds[...] = jnp.zeros_like(ds)

        @pl.when((b == 0) & (c == 0))
        def _():
            dpar_ref[...] = jnp.zeros_like(dpar_ref)

        e_v, p_v = e_ref[...], p_ref[...]
        fn = lambda pre, z, dtr, sprev, par: _ssd_chunk(pre, z, dtr, sprev, par, e_v, p_v)
        dstate, dpar_sum = ds[...], None
        for i in reversed(range(k)):
            r = slice(CHUNK * i, CHUNK * (i + 1))
            _, vjp = jax.vjp(fn, pre_ref[r, :], z_ref[r, :], dt_ref[r, :], sall_ref[i], par_ref[...])
            dpre, dz, ddt, dstate, dpar = vjp((dy_ref[r, :], dstate))
            dpre_ref[r, :] = dpre
            dz_ref[r, :] = dz.astype(BF16)
            ddt_ref[r, :] = ddt.astype(BF16)
            dpar_sum = dpar if dpar_sum is None else dpar_sum + dpar
        dpar_ref[...] += dpar_sum
        ds[...] = dstate

    row = lambda b, c: b * nc + (nc - 1 - c)
    full = lambda shp: pl.BlockSpec(shp, lambda b, c: (0, 0))
    return pl.pallas_call(
        body, name=name, grid=(B, nc),
        in_specs=[pl.BlockSpec((rows, SSD_CONV_DIM), lambda b, c: (row(b, c), 0)),
                  pl.BlockSpec((rows, SSD_W), lambda b, c: (row(b, c), Z0 // SSD_W)),
                  pl.BlockSpec((rows, LANE), lambda b, c: (row(b, c), DT0 // LANE)),
                  pl.BlockSpec((k, SSD_STATE, SSD_W), lambda b, c: (row(b, c), 0, 0)),
                  pl.BlockSpec((rows, SSD_W), lambda b, c: (row(b, c), ATT_W // SSD_W)),
                  full((SUBLANE, SSD_W)), full((LANE, SSD_W)), full((SUBLANE, SSD_W))],
        out_specs=[pl.BlockSpec((rows, SSD_CONV_DIM), lambda b, c: (row(b, c), 0)),
                   pl.BlockSpec((rows, SSD_W), lambda b, c: (row(b, c), 0)),
                   pl.BlockSpec((rows, LANE), lambda b, c: (row(b, c), 0)),
                   full((SUBLANE, SSD_W))],
        out_shape=[jax.ShapeDtypeStruct((T, SSD_CONV_DIM), F32),
                   jax.ShapeDtypeStruct((T, SSD_W), BF16),
                   jax.ShapeDtypeStruct((T, LANE), BF16),
                   jax.ShapeDtypeStruct((SUBLANE, SSD_W), F32)],
        scratch_shapes=[pltpu.VMEM((SSD_STATE, SSD_W), F32)],
        compiler_params=_cp(("arbitrary", "arbitrary")),
    )(pre, proj, proj, sall, dy, par, e_mat, psel)


def _sgu_consts():
    e = np.zeros((SUBLANE, SGU_W), np.float32)
    for g in range(SGU_GROUPS):
        e[g, HEAD_DIM * g:HEAD_DIM * (g + 1)] = 1.0
    return jnp.asarray(e)


def _sgu_chunk(u_raw, v_raw, ln, w, bst, e4):
    L = CHUNK
    u = _gelu(u_raw)
    v = _gelu(v_raw)
    mu = jnp.mean(v, axis=-1, keepdims=True)
    vc = v - mu
    var = jnp.mean(vc * vc, axis=-1, keepdims=True)
    vn = vc * lax.rsqrt(var + LN_EPS) * ln[0:1] + ln[1:2]
    vb = vn.astype(BF16)
    ri = lax.broadcasted_iota(jnp.int32, (L, L), 0)
    ci = lax.broadcasted_iota(jnp.int32, (L, L), 1)
    tril = ri >= ci
    col = lax.broadcasted_iota(jnp.int32, (1, SGU_W), 1)
    mixed = _dot(bst, e4, HI)
    for g in range(SGU_GROUPS):
        wc = jnp.where(tril, w[g], 0.0).astype(BF16)
        gm = (col >= HEAD_DIM * g) & (col < HEAD_DIM * (g + 1))
        mixed = mixed + jnp.where(gm, _dot(wc, vb), 0.0)
    return u * mixed


def _sgu_fwd(proj, ln, w, bst, B, S, name):
    T = B * S
    nc = S // CHUNK
    e4 = _sgu_consts()

    def body(u_ref, v_ref, ln_ref, w_ref, b_ref, e_ref, y_ref):
        y_ref[...] = _sgu_chunk(u_ref[...], v_ref[...], ln_ref[...], w_ref[...], b_ref[...], e_ref[...]).astype(BF16)

    return pl.pallas_call(
        body, name=name, grid=(T // CHUNK,),
        in_specs=[pl.BlockSpec((CHUNK, SGU_W), lambda i: (i, U0 // SGU_W)),
                  pl.BlockSpec((CHUNK, SGU_W), lambda i: (i, VS0 // SGU_W)),
                  pl.BlockSpec((SUBLANE, SGU_W), lambda i: (0, 0)),
                  pl.BlockSpec((SGU_GROUPS, CHUNK, CHUNK), lambda i: (0, 0, 0)),
                  pl.BlockSpec((CHUNK, SUBLANE), lambda i: (0, 0)),
                  pl.BlockSpec((SUBLANE, SGU_W), lambda i: (0, 0))],
        out_specs=pl.BlockSpec((CHUNK, SGU_W), lambda i: (i, 0)),
        out_shape=jax.ShapeDtypeStruct((T, SGU_W), BF16),
        compiler_params=_cp(("parallel",)),
    )(proj, proj, ln, w, bst, e4)


def _sgu_bwd(proj, dy, ln, w, bst, B, S, name):
    T = B * S
    e4 = _sgu_consts()
    ycol = (ATT_W + SSD_W) // SGU_W

    def body(u_ref, v_ref, dy_ref, ln_ref, w_ref, b_ref, e_ref, du_ref, dv_ref, dln_ref, dw_ref, db_ref):
        @pl.when(pl.program_id(0) == 0)
        def _():
            dln_ref[...] = jnp.zeros_like(dln_ref)
            dw_ref[...] = jnp.zeros_like(dw_ref)
            db_ref[...] = jnp.zeros_like(db_ref)

        e_v = e_ref[...]
        fn = lambda u, v, ln, w, b: _sgu_chunk(u, v, ln, w, b, e_v)
        _, vjp = jax.vjp(fn, u_ref[...], v_ref[...], ln_ref[...], w_ref[...], b_ref[...])
        du, dv, dln, dw, db = vjp(dy_ref[...])
        du_ref[...] = du.astype(BF16)
        dv_ref[...] = dv.astype(BF16)
        dln_ref[...] += dln
        dw_ref[...] += dw
        db_ref[...] += db

    c_ln = pl.BlockSpec((SUBLANE, SGU_W), lambda i: (0, 0))
    c_w = pl.BlockSpec((SGU_GROUPS, CHUNK, CHUNK), lambda i: (0, 0, 0))
    c_b = pl.BlockSpec((CHUNK, SUBLANE), lambda i: (0, 0))
    return pl.pallas_call(
        body, name=name, grid=(T // CHUNK,),
        in_specs=[pl.BlockSpec((CHUNK, SGU_W), lambda i: (i, U0 // SGU_W)),
                  pl.BlockSpec((CHUNK, SGU_W), lambda i: (i, VS0 // SGU_W)),
                  pl.BlockSpec((CHUNK, SGU_W), lambda i: (i, ycol)),
                  c_ln, c_w, c_b, pl.BlockSpec((SUBLANE, SGU_W), lambda i: (0, 0))],
        out_specs=[pl.BlockSpec((CHUNK, SGU_W), lambda i: (i, 0)),
                   pl.BlockSpec((CHUNK, SGU_W), lambda i: (i, 0)), c_ln, c_w, c_b],
        out_shape=[jax.ShapeDtypeStruct((T, SGU_W), BF16), jax.ShapeDtypeStruct((T, SGU_W), BF16),
                   jax.ShapeDtypeStruct((SUBLANE, SGU_W), F32),
                   jax.ShapeDtypeStruct((SGU_GROUPS, CHUNK, CHUNK), F32),
                   jax.ShapeDtypeStruct((CHUNK, SUBLANE), F32)],
        compiler_params=_cp(("arbitrary",)),
    )(proj, proj, dy, ln, w, bst, e4)


_HBM = pl.BlockSpec(memory_space=pltpu.HBM)
_SEM = pl.BlockSpec(memory_space=pltpu.SEMAPHORE)
_ANY = pl.BlockSpec(memory_space=pl.ANY)
_EFFECT = pltpu.SideEffectType.DATAFLOW_SIDE_EFFECTING


def _peers():
    x, y, c = lax.axis_index("x"), lax.axis_index("y"), lax.axis_index("c")
    out = []
    for p in range(1, N_DEV):
        px, py, pc = x ^ ((p >> 2) & 1), y ^ ((p >> 1) & 1), c ^ (p & 1)
        out.append(((px, py, pc), 4 * px + 2 * py + pc))
    return 4 * x + 2 * y + c, out


def _xchg_start(xs, a2a, order, name):
    n = len(xs)
    lands = [lax.empty(a.shape if f else (N_DEV,) + a.shape, a.dtype) for a, f in zip(xs, a2a)]

    def body(*refs):
        ins, zones = refs[:n], refs[n:2 * n]
        send_sems, recv_sems = refs[2 * n + 1], refs[2 * n + 2]
        token = refs[-1]
        me, peers = _peers()
        for p, (dev, peer) in enumerate(peers):
            for t in range(n):
                pltpu.make_async_remote_copy(
                    src_ref=ins[t].at[peer] if a2a[t] else ins[t], dst_ref=zones[t].at[me],
                    send_sem=send_sems.at[p * n + t], recv_sem=recv_sems.at[p * n + t],
                    device_id=dev, device_id_type=MESH).start()
        token[...] = jnp.zeros_like(token)

    hbm = lambda a: pltpu.HBM(a.shape, a.dtype)
    sems = pltpu.SemaphoreType.DMA(((N_DEV - 1) * n,))
    out = pl.pallas_call(
        body, name=name,
        in_specs=[_HBM] * (2 * n) + [_ANY],
        out_specs=[_SEM, _SEM] + [_HBM] * (2 * n) + [pl.BlockSpec(memory_space=pltpu.VMEM)],
        out_shape=[sems, sems] + [hbm(a) for a in xs] + [hbm(a) for a in lands]
        + [jax.ShapeDtypeStruct((SUBLANE, LANE), F32)],
        input_output_aliases={t: 2 + t for t in range(2 * n)},
        compiler_params=pltpu.CompilerParams(has_side_effects=_EFFECT),
    )(*[pltpu.with_memory_space_constraint(a, pltpu.HBM) for a in list(xs) + list(lands)], order)
    return out[0], out[1], out[2:2 + n], out[2 + n:2 + 2 * n], out[-1]


def _xchg_wait(started, a2a, after, name):
    send_sems, recv_sems, xs, lands, _ = started
    n = len(xs)

    def body(*refs):
        ins, zones = refs[:n], refs[n:2 * n]
        send_s, recv_s = refs[2 * n], refs[2 * n + 1]
        me, peers = _peers()
        cps = []
        for p, (dev, peer) in enumerate(peers):
            for t in range(n):
                cps.append(pltpu.make_async_remote_copy(
                    src_ref=ins[t].at[peer] if a2a[t] else ins[t], dst_ref=zones[t].at[peer],
                    send_sem=send_s.at[p * n + t], recv_sem=recv_s.at[p * n + t],
                    device_id=dev, device_id_type=MESH))
        for cp in cps:
            cp.wait_recv()
        for cp in cps:
            cp.wait_send()

    hbm = lambda a: pltpu.HBM(a.shape, a.dtype)
    out = pl.pallas_call(
        body, name=name,
        in_specs=[_HBM] * (2 * n) + [_SEM, _SEM, _ANY],
        out_specs=[_HBM] * (2 * n),
        out_shape=[hbm(a) for a in xs] + [hbm(a) for a in lands],
        input_output_aliases={t: t for t in range(2 * n)},
        compiler_params=pltpu.CompilerParams(has_side_effects=_EFFECT),
    )(*xs, *lands, send_sems, recv_sems, after)
    return out[:n], out[n:]


def _cast_layers(pairs, name):
    def body(*refs):
        n = len(refs) // 2
        for i in range(n):
            refs[n + i][...] = refs[i][...].astype(BF16)

    in_specs = [pl.BlockSpec((None,) + w.shape[1:], functools.partial(lambda l, i: (l, 0, 0), l),
                             pipeline_mode=pl.Buffered(1)) for w, l in pairs]
    return pl.pallas_call(
        body, name=name, grid=(1,), in_specs=in_specs,
        out_specs=[pl.BlockSpec(w.shape[1:], lambda i: (0, 0)) for w, _ in pairs],
        out_shape=[jax.ShapeDtypeStruct(w.shape[1:], BF16) for w, _ in pairs],
        compiler_params=_cp(("arbitrary",)),
    )(*[w for w, _ in pairs])


def _adamw(me, w, m, v, parts, own, name, layer=0, into=None):
    L, R, C = w.shape
    P = parts.shape[0]
    tr = R
    t = 16
    while t <= R:
        if R % t == 0 and t * C <= 131072:
            tr = t
        t += 16
    if tr == R and R * C > 131072 and R % 16 == 0:
        tr = 16
    own_all = own.shape[0] == P

    def body(me_ref, w_ref, m_ref, v_ref, p_ref, own_ref, *rest):
        g_ref, d_ref, mo_ref, vo_ref = rest[-4:]
        mine = own_ref[...].astype(F32)
        g = None
        for p in range(P):
            term = jnp.where(me_ref[0] == p, mine, p_ref[p].astype(F32))
            g = term if g is None else g + term
        mn = ADAM_B1 * m_ref[...] + (1.0 - ADAM_B1) * g
        vn = ADAM_B2 * v_ref[...] + (1.0 - ADAM_B2) * (g * g)
        m_hat = mn / (1.0 - ADAM_B1 ** ADAM_STEP)
        v_hat = vn / (1.0 - ADAM_B2 ** ADAM_STEP)
        g_ref[...] = g
        d_ref[...] = -ADAM_LR * (m_hat / (jnp.sqrt(v_hat) + ADAM_EPS) + ADAM_WD * w_ref[...])
        mo_ref[...] = mn
        vo_ref[...] = vn

    blk = pl.BlockSpec((None, tr, C), lambda i, me_ref: (layer, i, 0))
    own_blk = pl.BlockSpec((None, tr, C), lambda i, me_ref: (me_ref[0] if own_all else 0, i, 0))
    prev = list(into) if into is not None else []
    return pl.pallas_call(
        body, name=name,
        grid_spec=pltpu.PrefetchScalarGridSpec(
            num_scalar_prefetch=1, grid=(R // tr,),
            in_specs=[blk, blk, blk, pl.BlockSpec((P, tr, C), lambda i, me_ref: (0, i, 0)), own_blk]
            + [_ANY] * len(prev),
            out_specs=[blk] * 4),
        out_shape=[jax.ShapeDtypeStruct((L, R, C), F32)] * 4,
        input_output_aliases={6 + i: i for i in range(len(prev))},
        compiler_params=_cp(("parallel",)),
    )(me, w, m, v, parts, own, *prev)


def _perm_cols(w):
    pad = jnp.zeros((w.shape[0], LANE - SSD_HEADS), w.dtype)
    return jnp.concatenate([w[:, 0:1536], w[:, 2438:2694], w[:, 1536:2432], w[:, 2432:2438], pad,
                            w[:, 2694:2950]], axis=1)


def _unperm_cols(w):
    return jnp.concatenate([w[:, 0:1536], w[:, XBC0:XBC0 + SSD_CONV_DIM], w[:, DT0:DT0 + SSD_HEADS],
                            w[:, U0:U0 + SGU_W], w[:, VS0:VS0 + SGU_W]], axis=1)


_SMALL = ("ffn1_norm", "mix_norm", "conv_w", "conv_b", "dt_bias", "a_log", "d_skip", "ssd_norm",
          "sgu_ln_g", "sgu_ln_b", "sgu_w", "sgu_b", "ffn2_norm", "final_norm", "loss")


def _pack(d):
    v = jnp.concatenate([d[k].astype(F32).reshape(-1) for k in _SMALL])
    n = v.shape[0]
    npad = -(-n // (LANE * 16)) * (LANE * 16)
    return jnp.pad(v, (0, npad - n)).reshape(npad // LANE, LANE)


def _unpack(p, shapes):
    v = p.reshape(-1)
    out, o = {}, 0
    for k in _SMALL:
        n = int(np.prod(shapes[k]))
        out[k] = v[o:o + n].reshape(shapes[k])
        o += n
    return out


def kernel(x, ffn1_norm, ffn1_w_gate, ffn1_w_up, ffn1_w_down, mix_norm, w_in, conv_w, conv_b, dt_bias, a_log, d_skip, ssd_norm, sgu_ln_g, sgu_ln_b, sgu_w, sgu_b, w_out, ffn2_norm, ffn2_w_gate, ffn2_w_up, ffn2_w_down, final_norm, loss_target, m_ffn1_norm, m_ffn1_w_gate, m_ffn1_w_up, m_ffn1_w_down, m_mix_norm, m_w_in, m_conv_w, m_conv_b, m_dt_bias, m_a_log, m_d_skip, m_ssd_norm, m_sgu_ln_g, m_sgu_ln_b, m_sgu_w, m_sgu_b, m_w_out, m_ffn2_norm, m_ffn2_w_gate, m_ffn2_w_up, m_ffn2_w_down, m_final_norm, v_ffn1_norm, v_ffn1_w_gate, v_ffn1_w_up, v_ffn1_w_down, v_mix_norm, v_w_in, v_conv_w, v_conv_b, v_dt_bias, v_a_log, v_d_skip, v_ssd_norm, v_sgu_ln_g, v_sgu_ln_b, v_sgu_w, v_sgu_b, v_w_out, v_ffn2_norm, v_ffn2_w_gate, v_ffn2_w_up, v_ffn2_w_down, v_final_norm):
    B, S, D = x.shape
    T = B * S
    L = ffn1_norm.shape[0]
    me = 4 * lax.axis_index("x") + 2 * lax.axis_index("y") + lax.axis_index("c")
    cs = conv_w.shape[2]
    W = dict(ffn1_norm=ffn1_norm, ffn1_w_gate=ffn1_w_gate, ffn1_w_up=ffn1_w_up, ffn1_w_down=ffn1_w_down,
             mix_norm=mix_norm, w_in=w_in, conv_w=conv_w, conv_b=conv_b, dt_bias=dt_bias, a_log=a_log,
             d_skip=d_skip, ssd_norm=ssd_norm, sgu_ln_g=sgu_ln_g, sgu_ln_b=sgu_ln_b, sgu_w=sgu_w, sgu_b=sgu_b,
             w_out=w_out, ffn2_norm=ffn2_norm, ffn2_w_gate=ffn2_w_gate, ffn2_w_up=ffn2_w_up,
             ffn2_w_down=ffn2_w_down, final_norm=final_norm)
    M = dict(ffn1_norm=m_ffn1_norm, ffn1_w_gate=m_ffn1_w_gate, ffn1_w_up=m_ffn1_w_up, ffn1_w_down=m_ffn1_w_down,
             mix_norm=m_mix_norm, w_in=m_w_in, conv_w=m_conv_w, conv_b=m_conv_b, dt_bias=m_dt_bias, a_log=m_a_log,
             d_skip=m_d_skip, ssd_norm=m_ssd_norm, sgu_ln_g=m_sgu_ln_g, sgu_ln_b=m_sgu_ln_b, sgu_w=m_sgu_w,
             sgu_b=m_sgu_b, w_out=m_w_out, ffn2_norm=m_ffn2_norm, ffn2_w_gate=m_ffn2_w_gate,
             ffn2_w_up=m_ffn2_w_up, ffn2_w_down=m_ffn2_w_down, final_norm=m_final_norm)
    V = dict(ffn1_norm=v_ffn1_norm, ffn1_w_gate=v_ffn1_w_gate, ffn1_w_up=v_ffn1_w_up, ffn1_w_down=v_ffn1_w_down,
             mix_norm=v_mix_norm, w_in=v_w_in, conv_w=v_conv_w, conv_b=v_conv_b, dt_bias=v_dt_bias, a_log=v_a_log,
             d_skip=v_d_skip, ssd_norm=v_ssd_norm, sgu_ln_g=v_sgu_ln_g, sgu_ln_b=v_sgu_ln_b, sgu_w=v_sgu_w,
             sgu_b=v_sgu_b, w_out=v_w_out, ffn2_norm=v_ffn2_norm, ffn2_w_gate=v_ffn2_w_gate,
             ffn2_w_up=v_ffn2_w_up, ffn2_w_down=v_ffn2_w_down, final_norm=v_final_norm)
    FFN1 = ("ffn1_w_gate", "ffn1_w_up", "ffn1_w_down")
    FFN2 = ("ffn2_w_gate", "ffn2_w_up", "ffn2_w_down")
    MIX = ("w_in", "w_out")
    big = FFN1 + MIX + FFN2

    wgroups = [[(k, 0) for k in FFN1], [("w_in", 0), ("conv_w", None)], [("w_out", 0)] + [(k, 0) for k in FFN2]]
    wgroups += [[(k, l) for k in big] for l in range(1, L)]
    wstarted, order = [], x
    later = [kl for grp in wgroups[1:] for kl in grp if kl[0] != "conv_w"]
    cast = dict(zip(wgroups[0], _cast_layers([(W[k], l) for k, l in wgroups[0]], "cast_first")))
    for gi, grp in enumerate(wgroups):
        if gi == 1:
            first = lax.optimization_barrier((W[later[0][0]], order))[0]
            srcs = [(first if i == 0 else W[k], l) for i, (k, l) in enumerate(later)]
            cast.update(zip(later, _cast_layers(srcs, "cast_rest")))
        xs = [conv_w if k == "conv_w" else cast[(k, l)] for k, l in grp]
        st = _xchg_start(xs, [False] * len(xs), order, f"gather_start_{gi}")
        order = st[-1]
        wstarted.append(st)
    G = {}
    is_me = (jnp.arange(N_DEV) == me)

    def gathered(gi, after):
        own, lands = _xchg_wait(wstarted[gi], [False] * len(wgroups[gi]), after, f"gather_wait_{gi}")
        for key, o, z in zip(wgroups[gi], own, lands):
            G[key] = jnp.where(is_me.reshape((N_DEV,) + (1,) * o.ndim), o[None], z)

    def cols(k, l):
        a = G[(k, l)]
        return jnp.transpose(a, (1, 0, 2)).reshape(a.shape[1], -1)

    def rows(k, l):
        a = G[(k, l)]
        return a.reshape(-1, a.shape[-1])

    bias = _attn_bias(S, min(256, S))
    row1 = lambda a: a.reshape(1, -1)

    def ffn1_params(l):
        return dict(g1=row1(ffn1_norm[l]), wg1=cols("ffn1_w_gate", l), wu1=cols("ffn1_w_up", l),
                    wd1=rows("ffn1_w_down", l))

    def out_params(l):
        return dict(wout=rows("w_out", l), g2=row1(ffn2_norm[l]), wg2=cols("ffn2_w_gate", l),
                    wu2=cols("ffn2_w_up", l), wd2=rows("ffn2_w_down", l))

    def mix_params(l):
        cw = jnp.transpose(G[("conv_w", None)][:, l], (1, 0, 2)).reshape(SSD_CONV, -1)
        return dict(
            gm=row1(mix_norm[l]), win=_perm_cols(rows("w_in", l)),
            cw=jnp.pad(cw, ((0, SUBLANE - SSD_CONV), (0, 0))), cb=row1(conv_b[l]),
            par=jnp.pad(jnp.stack([jnp.repeat(dt_bias[l], HEAD_DIM), jnp.repeat(a_log[l], HEAD_DIM),
                                   jnp.repeat(d_skip[l], HEAD_DIM), ssd_norm[l]]), ((0, SUBLANE - 4), (0, 0))),
            ln=jnp.pad(jnp.stack([sgu_ln_g[l], sgu_ln_b[l]]), ((0, SUBLANE - 2), (0, 0))),
            sw=sgu_w[l], bst=jnp.pad(sgu_b[l].T, ((0, 0), (0, SUBLANE - SGU_GROUPS))))

    xc = x.reshape(T, D)
    saved, lay = [], []
    for l in range(L):
        gathered(0 if l == 0 else l + 2, order if l == 0 else xc)
        p = ffn1_params(l)
        x1, gate1, up1 = _ffn_fwd(xc, p["g1"], p["wg1"], p["wu1"], p["wd1"], f"ffn1_fwd_{l}")
        if l == 0:
            gathered(1, x1)
        p.update(mix_params(l))
        lay.append(p)
        proj, ht = _norm_mm(x1, p["gm"], p["win"], f"in_proj_{l}")
        o_att, lse = _attn_fwd(proj, bias, B, S, f"attn_fwd_{l}")
        pre = _conv_fwd(proj, p["cw"], p["cb"], B, S, f"conv_fwd_{l}")
        y_ssd, sall = _ssd_fwd(pre, proj, p["par"], B, S, f"ssd_fwd_{l}")
        y_sgu = _sgu_fwd(proj, p["ln"], p["sw"], p["bst"], B, S, f"sgu_fwd_{l}")
        ycat = jnp.concatenate([o_att.astype(BF16), y_ssd, y_sgu], axis=1)
        if l == 0:
            gathered(2, ycat)
        p.update(out_params(l))
        x2 = _mm(ycat, p["wout"], "nn", f"out_proj_{l}", residual=x1)
        x3, gate2, up2 = _ffn_fwd(x2, p["g2"], p["wg2"], p["wu2"], p["wd2"], f"ffn2_fwd_{l}")
        saved.append(dict(x0=xc, gate1=gate1, up1=up1, x1=x1, ht=ht, proj=proj, o_att=o_att, lse=lse, pre=pre,
                          sall=sall, ycat=ycat, x2=x2, gate2=gate2, up2=up2))
        xc = x3
    loss_part, dx, dgf = _final_loss(xc, row1(final_norm), loss_target.reshape(T, D), "final_loss")

    gl = [dict() for _ in range(L)]
    gstarted, gorder = [], [order]

    def to_blocks(k, a):
        if k.endswith("w_gate") or k.endswith("w_up"):
            a = a.reshape(a.shape[0], N_DEV, -1).transpose(1, 0, 2)
        elif k.endswith("w_down"):
            a = a.reshape(a.shape[0], N_DEV, -1).transpose(1, 2, 0)
        else:
            a = a.reshape(N_DEV, -1, a.shape[-1])
        return a.astype(BF16)

    def send_grads(keys, l, extra, tag):
        xs = [to_blocks(k, gl[l][k]) for k in keys] + extra
        flags = [True] * len(keys) + [False] * len(extra)
        st = _xchg_start(xs, flags, gorder[0], f"grads_start_{tag}")
        gorder[0] = st[-1]
        gstarted.append((keys, l, st, flags, tag))

    def behind(a):
        return lax.optimization_barrier((a, gorder[0]))[0]

    for l in reversed(range(L)):
        p, s, g = lay[l], saved[l], gl[l]
        dx2, dgate, dup, act, xnt, dacct, g["ffn2_norm"] = _ffn_bwd_dx(
            dx, s["x2"], p["g2"], s["gate2"], s["up2"], p["wg2"], p["wu2"], p["wd2"], f"ffn2_bwd_{l}")
        g["ffn2_w_gate"], g["ffn2_w_up"], g["ffn2_w_down"] = _ffn_dw(xnt, dacct, dgate, dup, act, f"ffn2_dw_{l}")
        dycat = _mm(dx2, p["wout"], "nt", f"out_proj_dx_{l}")
        g["w_out"] = _mm(s["ycat"], dx2, "tn", f"out_proj_dw_{l}", out_dtype=BF16, tm_cap=1024, tk_cap=512)
        dq, dk, dv = _attn_bwd(s["proj"], s["o_att"], s["lse"], dycat, bias, B, S, f"attn_bwd_{l}")
        dpre, dz, ddt, dpar = _ssd_bwd(s["pre"], s["proj"], s["sall"], dycat, p["par"], B, S, f"ssd_bwd_{l}")
        dxbc, dwb = _conv_bwd(dpre, s["proj"], p["cw"], B, S, f"conv_bwd_{l}")
        du, dvs, dln, dsw, dbst = _sgu_bwd(s["proj"], dycat, p["ln"], p["sw"], p["bst"], B, S, f"sgu_bwd_{l}")
        dproj = jnp.concatenate([dq, dk, dv, dz, du, dxbc, ddt, dvs], axis=1)
        g["w_in"] = _unperm_cols(_mm_resident_lhs(s["ht"], dproj, f"in_proj_dw_{l}"))
        dx1, g["mix_norm"] = _norm_mm_bwd(dproj, s["x1"], p["gm"], p["win"], dx2, f"in_proj_bwd_{l}")
        if l == 0:
            send_grads(MIX + FFN2, 0, [], "l0a")
            dx1 = behind(dx1)
        dx, dgate, dup, act, xnt, dacct, g["ffn1_norm"] = _ffn_bwd_dx(
            dx1, s["x0"], p["g1"], s["gate1"], s["up1"], p["wg1"], p["wu1"], p["wd1"], f"ffn1_bwd_{l}")
        g["ffn1_w_gate"], g["ffn1_w_up"], g["ffn1_w_down"] = _ffn_dw(xnt, dacct, dgate, dup, act, f"ffn1_dw_{l}")
        if l > 0:
            send_grads(big, l, [], f"l{l}")
            dx = behind(dx)
        hsum = lambda r: r.reshape(SSD_HEADS, HEAD_DIM).sum(-1)
        g["conv_w"], g["conv_b"] = dwb[:SSD_CONV], dwb[SSD_CONV]
        g["dt_bias"], g["a_log"], g["d_skip"], g["ssd_norm"] = hsum(dpar[0]), hsum(dpar[1]), hsum(dpar[2]), dpar[3]
        g["sgu_ln_g"], g["sgu_ln_b"], g["sgu_w"], g["sgu_b"] = dln[0], dln[1], dsw, dbst[:, :SGU_GROUPS].T
    grad_x = dx.reshape(B, S, D)

    stack = lambda k: jnp.stack([gl[l][k] for l in range(L)])
    zero1 = jnp.zeros((1,), F32)
    W["loss"], M["loss"], V["loss"] = zero1, zero1, zero1
    per_layer = lambda k: k not in ("final_norm", "conv_w", "loss")
    small = {k: stack(k) for k in _SMALL if per_layer(k) or k == "conv_w"}
    small["final_norm"], small["loss"] = dgf.reshape(-1), loss_part[0, :1]
    small = {k: small[k].reshape((L,) + W[k].shape[1:]) if per_layer(k) else small[k] for k in _SMALL}
    full_shapes = {k: (W[k].shape if k != "conv_w" else (L, SSD_CONV, SSD_CONV_DIM)) for k in _SMALL}
    send_grads(FFN1, 0, [_pack(small)], "l0b")

    res, after, small_parts = {}, gorder[0], None
    me1 = me.reshape(1).astype(jnp.int32)
    for keys, l, st, flags, tag in gstarted:
        own, lands = _xchg_wait(st, flags, after, f"grads_wait_{tag}")
        for k, mine, pk in zip(keys, own, lands):
            res[k] = _adamw(me1, W[k], M[k], V[k], pk, mine, f"adamw_{k}_{l}", layer=l, into=res.get(k))
        after = lax.optimization_barrier(tuple(res[k][0] for k in keys))[0]
        if len(lands) > len(keys):
            small_parts = (lands[-1], own[-1][None])
    grads, deltas, new_m, new_v = [{k: res[k][i] for k in big} for i in range(4)]

    def embed(a, k):
        if k != "conv_w":
            return a
        return lax.dynamic_update_slice(jnp.zeros(full_shapes[k], F32), a, (0, 0, me * cs))

    outs = _adamw(me1, _pack({k: embed(W[k], k) for k in _SMALL})[None],
                  _pack({k: embed(M[k], k) for k in _SMALL})[None],
                  _pack({k: embed(V[k], k) for k in _SMALL})[None], small_parts[0], small_parts[1], "adamw_small")
    for d, o in zip((grads, deltas, new_m, new_v), outs):
        u = _unpack(o, full_shapes)
        u["conv_w"] = lax.dynamic_slice(u["conv_w"], (0, 0, me * cs), (L, SSD_CONV, cs))
        d.update(u)

    names = ("ffn1_norm", "ffn1_w_gate", "ffn1_w_up", "ffn1_w_down", "mix_norm", "w_in", "conv_w", "conv_b",
             "dt_bias", "a_log", "d_skip", "ssd_norm", "sgu_ln_g", "sgu_ln_b", "sgu_w", "sgu_b", "w_out",
             "ffn2_norm", "ffn2_w_gate", "ffn2_w_up", "ffn2_w_down", "final_norm")
    loss = grads["loss"][0]
    return (loss, grad_x, *[grads[n] for n in names], *[deltas[n] for n in names],
            *[new_m[n] for n in names], *[new_v[n] for n in names])
```

```python
import functools

import numpy as np
import jax
import jax.numpy as jnp
from jax import lax
from jax.experimental import pallas as pl
from jax.experimental.pallas import tpu as pltpu

F32, BF16 = jnp.float32, jnp.bfloat16
HI = lax.Precision.HIGHEST
MESH = pl.DeviceIdType.MESH
N_DEV = 8
VMEM_LIMIT_BYTES = 56 * 1024 * 1024
LANE, SUBLANE = 128, 8

HEAD_DIM = 64
ATT_W = 384
SSD_W = 384
SSD_HEADS = 6
SSD_STATE = 128
SSD_CONV = 4
CHUNK = 128
SSD_CONV_DIM = 896
SGU_W = 256
SGU_GROUPS = 4
D_IN = 2950
RMS_EPS = 1e-6
LN_EPS = 1e-5
NEG = -1e30

PW = 3072
Q0, K0, V0, Z0, U0, XBC0, DT0, VS0 = 0, 384, 768, 1152, 1536, 1792, 2688, 2816

ADAM_LR, ADAM_B1, ADAM_B2, ADAM_EPS, ADAM_WD, ADAM_STEP = 0.001, 0.9, 0.999, 1e-08, 0.01, 10


def _cp(sem=None):
    return pltpu.CompilerParams(dimension_semantics=sem, vmem_limit_bytes=VMEM_LIMIT_BYTES)


def _tile(n, cap, mult=LANE):
    best = None
    t = mult
    while t <= min(n, cap):
        if n % t == 0:
            best = t
        t += mult
    return best if best is not None else n


def _dot(a, b, prec=None):
    return jnp.dot(a, b, preferred_element_type=F32, precision=prec)


def _dot_nt(a, b, prec=None):
    return lax.dot_general(a, b, (((1,), (1,)), ((), ())), preferred_element_type=F32, precision=prec)


def _dot_tn(a, b, prec=None):
    return lax.dot_general(a, b, (((0,), (0,)), ((), ())), preferred_element_type=F32, precision=prec)


def _sigmoid(x):
    return 1.0 / (1.0 + jnp.exp(-x))


def _silu(x):
    return x * _sigmoid(x)


def _gelu(x):
    return 0.5 * x * (1.0 + lax.erf(x * 0.7071067811865476))


def _softplus(x):
    return jnp.maximum(x, 0.0) + jnp.log(1.0 + jnp.exp(-jnp.abs(x)))


def _rms_fwd(x, g):
    rstd = lax.rsqrt(jnp.mean(x * x, axis=-1, keepdims=True) + RMS_EPS)
    xhat = x * rstd
    return xhat * g, xhat, rstd


def _rms_bwd(dy, xhat, rstd, g):
    dxhat = dy * g
    dx = rstd * (dxhat - xhat * jnp.mean(dxhat * xhat, axis=-1, keepdims=True))
    return dx, dy * xhat


def _resident(shape):
    return pl.BlockSpec(shape, lambda *_: (0,) * len(shape), pipeline_mode=pl.Buffered(1))


def _mm(a, b, mode, name, out_dtype=F32, residual=None, tm_cap=512, tn_cap=1024, tk_cap=1024):
    if mode == "nn":
        (M, K), (_, N) = a.shape, b.shape
    elif mode == "nt":
        (M, K), (N, _) = a.shape, b.shape
    else:
        (K, M), (_, N) = a.shape, b.shape
    tm, tn, tk = _tile(M, tm_cap), _tile(N, tn_cap), _tile(K, tk_cap)
    nk = K // tk
    if mode == "tn":
        a_spec = pl.BlockSpec((tk, tm), lambda i, j, k: (k, i))
    else:
        a_spec = pl.BlockSpec((tm, tk), lambda i, j, k: (i, k))
    if mode == "nt":
        b_spec = pl.BlockSpec((tn, tk), lambda i, j, k: (j, k))
    else:
        b_spec = pl.BlockSpec((tk, tn), lambda i, j, k: (k, j))
    o_spec = pl.BlockSpec((tm, tn), lambda i, j, k: (i, j))
    has_res = residual is not None

    def prod(a_ref, b_ref):
        av = a_ref[...].astype(BF16)
        bv = b_ref[...].astype(BF16)
        if mode == "nn":
            return _dot(av, bv)
        if mode == "nt":
            return _dot_nt(av, bv)
        return _dot_tn(av, bv)

    def body(*refs):
        a_ref, b_ref = refs[:2]
        r_ref = refs[2] if has_res else None
        o_ref = refs[2 + has_res]
        if nk == 1:
            o = prod(a_ref, b_ref)
            if has_res:
                o = r_ref[...] + o
            o_ref[...] = o.astype(out_dtype)
            return
        acc = refs[3 + has_res]
        k = pl.program_id(2)

        @pl.when(k == 0)
        def _():
            acc[...] = jnp.zeros_like(acc)

        acc[...] += prod(a_ref, b_ref)

        @pl.when(k == nk - 1)
        def _():
            o = acc[...]
            if has_res:
                o = r_ref[...] + o
            o_ref[...] = o.astype(out_dtype)

    ins = [a, b] + ([residual] if has_res else [])
    in_specs = [a_spec, b_spec] + ([o_spec] if has_res else [])
    return pl.pallas_call(
        body, name=name, grid=(M // tm, N // tn, nk),
        in_specs=in_specs, out_specs=o_spec,
        out_shape=jax.ShapeDtypeStruct((M, N), out_dtype),
        scratch_shapes=[pltpu.VMEM((tm, tn), F32)] if nk > 1 else [],
        compiler_params=_cp(("parallel", "parallel", "arbitrary")),
    )(*ins)


def _ffn_fwd(x, g, wg, wu, wd, name):
    T, D = x.shape
    F = wg.shape[1]
    tm = _tile(T, 512)

    def body(x_ref, g_ref, wg_ref, wu_ref, wd_ref, out_ref, gate_ref, up_ref):
        xv = x_ref[...]
        xn = _rms_fwd(xv, g_ref[...])[0].astype(BF16)
        gate = _dot(xn, wg_ref[...])
        up = _dot(xn, wu_ref[...])
        gate_ref[...] = gate.astype(BF16)
        up_ref[...] = up.astype(BF16)
        act = (_silu(gate) * up).astype(BF16)
        out_ref[...] = xv + 0.5 * _dot(act, wd_ref[...])

    row = lambda w: pl.BlockSpec((tm, w), lambda i: (i, 0))
    return pl.pallas_call(
        body, name=name, grid=(T // tm,),
        in_specs=[row(D), _resident((1, D)), _resident((D, F)), _resident((D, F)), _resident((F, D))],
        out_specs=[row(D), row(F), row(F)],
        out_shape=[jax.ShapeDtypeStruct((T, D), F32),
                   jax.ShapeDtypeStruct((T, F), BF16),
                   jax.ShapeDtypeStruct((T, F), BF16)],
        compiler_params=_cp(("parallel",)),
    )(x, g, wg, wu, wd)


def _ffn_bwd_dx(dout, x, g, gate, up, wg, wu, wd, name):
    T, D = x.shape
    F = wg.shape[1]
    tm, th = _tile(T, 256), _tile(F, 256)
    nj = F // th

    def body(dout_ref, x_ref, g_ref, gate_ref, up_ref, wg_ref, wu_ref, wd_ref,
             dx_ref, dgate_ref, dup_ref, act_ref, xnt_ref, dacct_ref, dg_ref):
        @pl.when(pl.program_id(0) == 0)
        def _():
            dg_ref[...] = jnp.zeros_like(dg_ref)

        gv = g_ref[...]
        dout_v = dout_ref[...]
        xn, xhat, rstd = _rms_fwd(x_ref[...], gv)
        xnt_ref[...] = xn.T.astype(BF16)
        dacc = 0.5 * dout_v
        dacct_ref[...] = dacc.T.astype(BF16)
        dact = _dot_nt(dacc.astype(BF16), wd_ref[...])
        gt = gate_ref[...].astype(F32)
        u = up_ref[...].astype(F32)
        sig = _sigmoid(gt)
        sl = gt * sig
        dgate = (dact * u * (sig * (1.0 + gt * (1.0 - sig)))).astype(BF16)
        dup = (dact * sl).astype(BF16)
        act = (sl * u).astype(BF16)
        for j in range(nj):
            cs = slice(j * th, (j + 1) * th)
            dgate_ref[j] = dgate[:, cs]
            dup_ref[j] = dup[:, cs]
            act_ref[j] = act[:, cs]
        dxn = _dot_nt(dgate, wg_ref[...]) + _dot_nt(dup, wu_ref[...])
        dx, dgrow = _rms_bwd(dxn, xhat, rstd, gv)
        dx_ref[...] = dout_v + dx
        dg_ref[...] += jnp.sum(dgrow, axis=0, keepdims=True)

    row = lambda w: pl.BlockSpec((tm, w), lambda i: (i, 0))
    tiled = pl.BlockSpec((nj, tm, th), lambda i: (0, i, 0))
    tr = pl.BlockSpec((D, tm), lambda i: (0, i))
    return pl.pallas_call(
        body, name=name, grid=(T // tm,),
        in_specs=[row(D), row(D), _resident((1, D)), row(F), row(F),
                  _resident((D, F)), _resident((D, F)), _resident((F, D))],
        out_specs=[row(D), tiled, tiled, tiled, tr, tr, pl.BlockSpec((1, D), lambda i: (0, 0))],
        out_shape=[jax.ShapeDtypeStruct((T, D), F32)] + [jax.ShapeDtypeStruct((nj, T, th), BF16)] * 3
        + [jax.ShapeDtypeStruct((D, T), BF16)] * 2 + [jax.ShapeDtypeStruct((1, D), F32)],
        compiler_params=_cp(("arbitrary",)),
    )(dout, x, g, gate, up, wg, wu, wd)


def _ffn_dw(xnt, dacct, dgate, dup, act, name):
    D, T = xnt.shape
    nj, _, th = dgate.shape

    def body(xnt_ref, dacct_ref, dgate_ref, dup_ref, act_ref, dwg_ref, dwu_ref, dwdt_ref):
        xv = xnt_ref[...]
        dwg_ref[...] = _dot(xv, dgate_ref[...]).astype(BF16)
        dwu_ref[...] = _dot(xv, dup_ref[...]).astype(BF16)
        dwdt_ref[...] = _dot(dacct_ref[...], act_ref[...]).astype(BF16)

    tile = pl.BlockSpec((None, T, th), lambda j: (j, 0, 0))
    out = pl.BlockSpec((D, th), lambda j: (0, j))
    return pl.pallas_call(
        body, name=name, grid=(nj,),
        in_specs=[_resident((D, T)), _resident((D, T)), tile, tile, tile],
        out_specs=[out, out, out], out_shape=[jax.ShapeDtypeStruct((D, nj * th), BF16)] * 3,
        compiler_params=_cp(("parallel",)),
    )(xnt, dacct, dgate, dup, act)


def _norm_mm(x, g, w, name):
    T, D = x.shape
    N = w.shape[1]
    tm = _tile(T, 512)

    def body(x_ref, g_ref, w_ref, o_ref, ht_ref):
        xn = _rms_fwd(x_ref[...], g_ref[...])[0]
        ht_ref[...] = xn.T.astype(BF16)
        o_ref[...] = _dot(xn.astype(BF16), w_ref[...])

    return pl.pallas_call(
        body, name=name, grid=(T // tm,),
        in_specs=[pl.BlockSpec((tm, D), lambda i: (i, 0)), _resident((1, D)), _resident((D, N))],
        out_specs=[pl.BlockSpec((tm, N), lambda i: (i, 0)), pl.BlockSpec((D, tm), lambda i: (0, i))],
        out_shape=[jax.ShapeDtypeStruct((T, N), F32), jax.ShapeDtypeStruct((D, T), BF16)],
        compiler_params=_cp(("parallel",)),
    )(x, g, w)


def _norm_mm_bwd(dproj, x, g, w, dres, name):
    T, D = x.shape
    N = w.shape[1]
    tm = _tile(T, 512)

    def body(dp_ref, x_ref, g_ref, w_ref, dres_ref, dx_ref, dg_ref):
        @pl.when(pl.program_id(0) == 0)
        def _():
            dg_ref[...] = jnp.zeros_like(dg_ref)

        gv = g_ref[...]
        dh = _dot_nt(dp_ref[...], w_ref[...])
        _, xhat, rstd = _rms_fwd(x_ref[...], gv)
        dx, dgrow = _rms_bwd(dh, xhat, rstd, gv)
        dx_ref[...] = dres_ref[...] + dx
        dg_ref[...] += jnp.sum(dgrow, axis=0, keepdims=True)

    row = pl.BlockSpec((tm, D), lambda i: (i, 0))
    one = pl.BlockSpec((1, D), lambda i: (0, 0))
    return pl.pallas_call(
        body, name=name, grid=(T // tm,),
        in_specs=[pl.BlockSpec((tm, N), lambda i: (i, 0)), row, _resident((1, D)), _resident((D, N)), row],
        out_specs=[row, one],
        out_shape=[jax.ShapeDtypeStruct((T, D), F32), jax.ShapeDtypeStruct((1, D), F32)],
        compiler_params=_cp(("arbitrary",)),
    )(dproj, x, g, w, dres)


def _mm_resident_lhs(at, b, name, tn_cap=512):
    M, K = at.shape
    N = b.shape[1]
    tn = _tile(N, tn_cap)

    def body(a_ref, b_ref, o_ref):
        o_ref[...] = _dot(a_ref[...], b_ref[...]).astype(BF16)

    return pl.pallas_call(
        body, name=name, grid=(N // tn,),
        in_specs=[_resident((M, K)), pl.BlockSpec((K, tn), lambda j: (0, j))],
        out_specs=pl.BlockSpec((M, tn), lambda j: (0, j)),
        out_shape=jax.ShapeDtypeStruct((M, N), BF16),
        compiler_params=_cp(("parallel",)),
    )(at, b)


def _final_loss(x, g, target, name):
    T, D = x.shape
    tm = _tile(T, 512)

    def body(x_ref, g_ref, t_ref, loss_ref, dx_ref, dg_ref):
        @pl.when(pl.program_id(0) == 0)
        def _():
            dg_ref[...] = jnp.zeros_like(dg_ref)
            loss_ref[...] = jnp.zeros_like(loss_ref)

        gv = g_ref[...]
        y, xhat, rstd = _rms_fwd(x_ref[...], gv)
        err = y - t_ref[...]
        part = 0.5 * jnp.sum(jnp.mean(err * err, axis=-1, keepdims=True), axis=0, keepdims=True)
        loss_ref[...] += jnp.broadcast_to(part, loss_ref.shape)
        dy = err * (1.0 / D)
        dx, dgrow = _rms_bwd(dy, xhat, rstd, gv)
        dx_ref[...] = dx
        dg_ref[...] += jnp.sum(dgrow, axis=0, keepdims=True)

    row = pl.BlockSpec((tm, D), lambda i: (i, 0))
    one = pl.BlockSpec((1, D), lambda i: (0, 0))
    return pl.pallas_call(
        body, name=name, grid=(T // tm,),
        in_specs=[row, one, row],
        out_specs=[pl.BlockSpec((1, LANE), lambda i: (0, 0)), row, one],
        out_shape=[jax.ShapeDtypeStruct((1, LANE), F32), jax.ShapeDtypeStruct((T, D), F32),
                   jax.ShapeDtypeStruct((1, D), F32)],
        compiler_params=_cp(("arbitrary",)),
    )(x, g, target)


def _attn_bias(S, bq):
    d = jnp.arange(bq)[:, None] - jnp.arange(S)[None, :] + (S // bq - 1) * bq
    ok = d >= 0
    mult = ((ok & (d <= 128)).astype(F32) + (ok & (d % 4 == 0) & (d <= 512)).astype(F32)
            + (ok & (d % 16 == 0) & (d <= 2048)).astype(F32))
    return jnp.where(mult > 0, jnp.log(jnp.maximum(mult, 1.0)), NEG).astype(F32)


def _attn_fwd(proj, bias, B, S, name):
    T = B * S
    bq = bias.shape[0]
    nb = S // bq
    qcol, kcol, vcol = Q0 // LANE, K0 // LANE, V0 // LANE

    def body(q_ref, k_ref, v_ref, t_ref, o_ref, lse_ref, ks, vs):
        for hh in range(2):
            sl = slice(HEAD_DIM * hh, HEAD_DIM * (hh + 1))
            ks[hh] = k_ref[:, sl].astype(BF16)
            vs[hh] = v_ref[:, sl].astype(BF16)
        for hh in range(2):
            sl = slice(HEAD_DIM * hh, HEAD_DIM * (hh + 1))
            for qb in range(nb):
                w, off, rows = bq * (qb + 1), (nb - 1 - qb) * bq, slice(qb * bq, (qb + 1) * bq)
                q = (q_ref[rows, sl] * 0.125).astype(BF16)
                s = _dot_nt(q, ks[hh, 0:w, :]) + t_ref[:, off:off + w]
                m = jnp.max(s, axis=-1, keepdims=True)
                p = jnp.exp(s - m)
                l = jnp.sum(p, axis=-1, keepdims=True)
                o_ref[rows, sl] = _dot(p.astype(BF16), vs[hh, 0:w, :]) / l
                lse_ref[rows, hh:hh + 1] = m + jnp.log(l)

    blk = lambda c0: pl.BlockSpec((S, LANE), lambda b, p: (b, c0 + p))
    return pl.pallas_call(
        body, name=name, grid=(B, ATT_W // LANE),
        in_specs=[blk(qcol), blk(kcol), blk(vcol), _resident((bq, S))],
        out_specs=[pl.BlockSpec((S, LANE), lambda b, p: (b, p)),
                   pl.BlockSpec((None, None, S, 2), lambda b, p: (b, p, 0, 0))],
        out_shape=[jax.ShapeDtypeStruct((T, ATT_W), F32),
                   jax.ShapeDtypeStruct((B, ATT_W // LANE, S, 2), F32)],
        scratch_shapes=[pltpu.VMEM((2, S, HEAD_DIM), BF16)] * 2,
        compiler_params=_cp(("parallel", "parallel")),
    )(proj, proj, proj, bias)


def _attn_bwd(proj, o, lse, dy, bias, B, S, name):
    T = B * S
    bq = bias.shape[0]
    nb = S // bq
    qcol, kcol, vcol = Q0 // LANE, K0 // LANE, V0 // LANE

    def body(q_ref, k_ref, v_ref, o_ref, lse_ref, do_ref, t_ref, dq_ref, dk_ref, dv_ref, ks, vs, dks, dvs):
        for hh in range(2):
            sl = slice(HEAD_DIM * hh, HEAD_DIM * (hh + 1))
            ks[hh] = k_ref[:, sl].astype(BF16)
            vs[hh] = v_ref[:, sl].astype(BF16)
        dks[...] = jnp.zeros_like(dks)
        dvs[...] = jnp.zeros_like(dvs)
        for hh in range(2):
            sl = slice(HEAD_DIM * hh, HEAD_DIM * (hh + 1))
            for qb in range(nb):
                w, off, rows = bq * (qb + 1), (nb - 1 - qb) * bq, slice(qb * bq, (qb + 1) * bq)
                q = (q_ref[rows, sl] * 0.125).astype(BF16)
                do = do_ref[rows, sl]
                dob = do.astype(BF16)
                delta = jnp.sum(do * o_ref[rows, sl], axis=-1, keepdims=True)
                k, v = ks[hh, 0:w, :], vs[hh, 0:w, :]
                s = _dot_nt(q, k) + t_ref[:, off:off + w]
                p = jnp.exp(s - lse_ref[rows, hh:hh + 1])
                ds = (p * (_dot_nt(dob, v) - delta)).astype(BF16)
                dq_ref[rows, sl] = (_dot(ds, k) * 0.125).astype(dq_ref.dtype)
                dks[hh, 0:w, :] += _dot_tn(ds, q)
                dvs[hh, 0:w, :] += _dot_tn(p.astype(BF16), dob)
            dk_ref[:, sl] = dks[hh].astype(dk_ref.dtype)
            dv_ref[:, sl] = dvs[hh].astype(dv_ref.dtype)

    blk = lambda c0: pl.BlockSpec((S, LANE), lambda b, p: (b, c0 + p))
    own = pl.BlockSpec((S, LANE), lambda b, p: (b, p))
    return pl.pallas_call(
        body, name=name, grid=(B, ATT_W // LANE),
        in_specs=[blk(qcol), blk(kcol), blk(vcol), own,
                  pl.BlockSpec((None, None, S, 2), lambda b, p: (b, p, 0, 0)), own, _resident((bq, S))],
        out_specs=[own, own, own],
        out_shape=[jax.ShapeDtypeStruct((T, ATT_W), BF16)] * 3,
        scratch_shapes=[pltpu.VMEM((2, S, HEAD_DIM), BF16)] * 2 + [pltpu.VMEM((2, S, HEAD_DIM), F32)] * 2,
        compiler_params=_cp(("parallel", "parallel")),
    )(proj, proj, proj, o, lse, dy, bias)


def _conv_fwd(proj, cw, cb, B, S, name):
    T = B * S
    nc = SSD_CONV_DIM // LANE
    c0 = XBC0 // LANE

    def body(x_ref, w_ref, b_ref, o_ref):
        x = x_ref[...]
        t = lax.broadcasted_iota(jnp.int32, (S, 1), 0)
        acc = b_ref[...] + w_ref[SSD_CONV - 1:SSD_CONV, :] * x
        for k in range(SSD_CONV - 1):
            sh = SSD_CONV - 1 - k
            xs = jnp.where(t >= sh, pltpu.roll(x, sh, 0), 0.0)
            acc = acc + w_ref[k:k + 1, :] * xs
        o_ref[...] = acc

    return pl.pallas_call(
        body, name=name, grid=(B, nc),
        in_specs=[pl.BlockSpec((S, LANE), lambda b, j: (b, c0 + j)),
                  pl.BlockSpec((SUBLANE, LANE), lambda b, j: (0, j)),
                  pl.BlockSpec((1, LANE), lambda b, j: (0, j))],
        out_specs=pl.BlockSpec((S, LANE), lambda b, j: (b, j)),
        out_shape=jax.ShapeDtypeStruct((T, SSD_CONV_DIM), F32),
        compiler_params=_cp(("parallel", "parallel")),
    )(proj, cw, cb)


def _conv_bwd(dpre, proj, cw, B, S, name):
    T = B * S
    nc = SSD_CONV_DIM // LANE
    c0 = XBC0 // LANE

    def body(d_ref, x_ref, w_ref, dx_ref, dwb_ref):
        @pl.when(pl.program_id(1) == 0)
        def _():
            dwb_ref[...] = jnp.zeros_like(dwb_ref)

        d = d_ref[...]
        x = x_ref[...]
        t = lax.broadcasted_iota(jnp.int32, (S, 1), 0)
        dx = w_ref[SSD_CONV - 1:SSD_CONV, :] * d
        rows = [None] * SUBLANE
        rows[SSD_CONV - 1] = jnp.sum(d * x, axis=0, keepdims=True)
        for k in range(SSD_CONV - 1):
            sh = SSD_CONV - 1 - k
            dx = dx + w_ref[k:k + 1, :] * jnp.where(t < S - sh, pltpu.roll(d, S - sh, 0), 0.0)
            xs = jnp.where(t >= sh, pltpu.roll(x, sh, 0), 0.0)
            rows[k] = jnp.sum(d * xs, axis=0, keepdims=True)
        rows[SSD_CONV] = jnp.sum(d, axis=0, keepdims=True)
        dx_ref[...] = dx.astype(BF16)
        r = lax.broadcasted_iota(jnp.int32, (SUBLANE, LANE), 0)
        upd = jnp.zeros((SUBLANE, LANE), F32)
        for k in range(SSD_CONV + 1):
            upd = upd + jnp.where(r == k, rows[k], 0.0)
        dwb_ref[...] += upd

    return pl.pallas_call(
        body, name=name, grid=(nc, B),
        in_specs=[pl.BlockSpec((S, LANE), lambda j, b: (b, j)),
                  pl.BlockSpec((S, LANE), lambda j, b: (b, c0 + j)),
                  pl.BlockSpec((SUBLANE, LANE), lambda j, b: (0, j))],
        out_specs=[pl.BlockSpec((S, LANE), lambda j, b: (b, j)),
                   pl.BlockSpec((SUBLANE, LANE), lambda j, b: (0, j))],
        out_shape=[jax.ShapeDtypeStruct((T, SSD_CONV_DIM), BF16),
                   jax.ShapeDtypeStruct((SUBLANE, SSD_CONV_DIM), F32)],
        compiler_params=_cp(("parallel", "arbitrary")),
    )(dpre, proj, cw)


def _ssd_consts():
    e = np.zeros((LANE, SSD_W), np.float32)
    p = np.zeros((SUBLANE, SSD_W), np.float32)
    for h in range(SSD_HEADS):
        e[h, HEAD_DIM * h:HEAD_DIM * (h + 1)] = 1.0
        p[h, HEAD_DIM * h] = 1.0
    return jnp.asarray(e), jnp.asarray(p)


def _ssd_chunk(pre, z, dtr, sprev, par, e_mat, psel):
    L = CHUNK
    xc = _silu(pre)
    xs, bm, cm = xc[:, :SSD_W], xc[:, SSD_W:SSD_W + 2 * SSD_STATE], xc[:, SSD_W + 2 * SSD_STATE:]
    dtb, alog, dskip, ng = par[0:1], par[1:2], par[2:3], par[3:4]
    dt = _softplus(_dot(dtr, e_mat, HI) + dtb)
    a = dt * (-jnp.exp(alog))
    X = xs * dt
    ri = lax.broadcasted_iota(jnp.int32, (L, L), 0)
    ci = lax.broadcasted_iota(jnp.int32, (L, L), 1)
    tril = ri >= ci
    acs = _dot(tril.astype(F32), a, HI)
    acs_t = _dot_nt(psel, acs, HI)
    ecs = jnp.exp(acs)
    alast = acs[L - 1:L, :]
    xd = (X * jnp.exp(alast - acs)).astype(BF16)
    xb = X.astype(BF16)
    col = lax.broadcasted_iota(jnp.int32, (1, SSD_W), 1)
    sb = sprev.astype(BF16)
    bgs = [bm[:, SSD_STATE * g:SSD_STATE * (g + 1)].astype(BF16) for g in range(2)]
    cgs = [cm[:, SSD_STATE * g:SSD_STATE * (g + 1)].astype(BF16) for g in range(2)]
    cbs = [_dot_nt(cgs[g], bgs[g]) for g in range(2)]
    first = lax.broadcasted_iota(jnp.int32, (1, LANE), 1) < HEAD_DIM
    y_tiles, s_tiles = [], []
    for t in range(SSD_W // LANE):
        cl = slice(LANE * t, LANE * (t + 1))
        xb_t, xd_t, sb_t = xb[:, cl], xd[:, cl], sb[:, cl]
        per_head = []
        for h in (2 * t, 2 * t + 1):
            seg = acs[:, HEAD_DIM * h:HEAD_DIM * h + 1] - acs_t[h:h + 1, :]
            dec = jnp.exp(jnp.where(tril, seg, NEG))
            per_head.append(_dot((cbs[h // 3] * dec).astype(BF16), xb_t))
        y_t = jnp.where(first, per_head[0], per_head[1])
        ga, gb = (2 * t) // 3, (2 * t + 1) // 3
        if ga == gb:
            y_off, s_add = _dot(cgs[ga], sb_t), _dot_tn(bgs[ga], xd_t)
        else:
            y_off = jnp.where(first, _dot(cgs[ga], sb_t), _dot(cgs[gb], sb_t))
            s_add = jnp.where(first, _dot_tn(bgs[ga], xd_t), _dot_tn(bgs[gb], xd_t))
        y_tiles.append(y_t + y_off * ecs[:, cl])
        s_tiles.append(s_add)
    y = dskip * xs + jnp.concatenate(y_tiles, axis=1)
    snew = sprev * jnp.exp(alast) + jnp.concatenate(s_tiles, axis=1)
    yg = y * _silu(z)
    sq = yg * yg
    g0 = col < SSD_W // 2
    ms0 = jnp.sum(jnp.where(g0, sq, 0.0), axis=-1, keepdims=True) * (2.0 / SSD_W)
    ms1 = jnp.sum(jnp.where(g0, 0.0, sq), axis=-1, keepdims=True) * (2.0 / SSD_W)
    r = jnp.where(g0, lax.rsqrt(ms0 + RMS_EPS), lax.rsqrt(ms1 + RMS_EPS))
    return yg * r * ng, snew


SSD_CHUNKS_PER_STEP = 2


def _ssd_chunks_per_step(S):
    k = SSD_CHUNKS_PER_STEP
    while (S // CHUNK) % k:
        k //= 2
    return k


def _ssd_fwd(pre, proj, par, B, S, name):
    T = B * S
    k = _ssd_chunks_per_step(S)
    nc, rows = S // (CHUNK * k), CHUNK * k
    e_mat, psel = _ssd_consts()

    def body(pre_ref, z_ref, dt_ref, par_ref, e_ref, p_ref, y_ref, sall_ref, st):
        @pl.when(pl.program_id(1) == 0)
        def _():
            st[...] = jnp.zeros_like(st)

        sprev = st[...]
        for i in range(k):
            r = slice(CHUNK * i, CHUNK * (i + 1))
            sall_ref[i] = sprev
            y, sprev = _ssd_chunk(pre_ref[r, :], z_ref[r, :], dt_ref[r, :], sprev, par_ref[...], e_ref[...],
                                  p_ref[...])
            y_ref[r, :] = y.astype(BF16)
        st[...] = sprev

    row = lambda b, c: b * nc + c
    full = lambda shp: pl.BlockSpec(shp, lambda b, c: (0, 0))
    return pl.pallas_call(
        body, name=name, grid=(B, nc),
        in_specs=[pl.BlockSpec((rows, SSD_CONV_DIM), lambda b, c: (row(b, c), 0)),
                  pl.BlockSpec((rows, SSD_W), lambda b, c: (row(b, c), Z0 // SSD_W)),
                  pl.BlockSpec((rows, LANE), lambda b, c: (row(b, c), DT0 // LANE)),
                  full((SUBLANE, SSD_W)), full((LANE, SSD_W)), full((SUBLANE, SSD_W))],
        out_specs=[pl.BlockSpec((rows, SSD_W), lambda b, c: (row(b, c), 0)),
                   pl.BlockSpec((k, SSD_STATE, SSD_W), lambda b, c: (row(b, c), 0, 0))],
        out_shape=[jax.ShapeDtypeStruct((T, SSD_W), BF16),
                   jax.ShapeDtypeStruct((B * nc * k, SSD_STATE, SSD_W), F32)],
        scratch_shapes=[pltpu.VMEM((SSD_STATE, SSD_W), F32)],
        compiler_params=_cp(("parallel", "arbitrary")),
    )(pre, proj, proj, par, e_mat, psel)


def _ssd_bwd(pre, proj, sall, dy, par, B, S, name):
    T = B * S
    k = _ssd_chunks_per_step(S)
    nc, rows = S // (CHUNK * k), CHUNK * k
    e_mat, psel = _ssd_consts()

    def body(pre_ref, z_ref, dt_ref, sall_ref, dy_ref, par_ref, e_ref, p_ref,
             dpre_ref, dz_ref, ddt_ref, dpar_ref, ds):
        b, c = pl.program_id(0), pl.program_id(1)

        @pl.when(c == 0)
        def _():
            ds[...] = jnp.zeros_like(ds)

        @pl.when((b == 0) & (c == 0))
        def _():
            dpar_ref[...] = jnp.zeros_like(dpar_ref)

        e_v, p_v = e_ref[...], p_ref[...]
        fn = lambda pre, z, dtr, sprev, par: _ssd_chunk(pre, z, dtr, sprev, par, e_v, p_v)
        dstate, dpar_sum = ds[...], None
        for i in reversed(range(k)):
            r = slice(CHUNK * i, CHUNK * (i + 1))
            _, vjp = jax.vjp(fn, pre_ref[r, :], z_ref[r, :], dt_ref[r, :], sall_ref[i], par_ref[...])
            dpre, dz, ddt, dstate, dpar = vjp((dy_ref[r, :], dstate))
            dpre_ref[r, :] = dpre
            dz_ref[r, :] = dz.astype(BF16)
            ddt_ref[r, :] = ddt.astype(BF16)
            dpar_sum = dpar if dpar_sum is None else dpar_sum + dpar
        dpar_ref[...] += dpar_sum
        ds[...] = dstate

    row = lambda b, c: b * nc + (nc - 1 - c)
    full = lambda shp: pl.BlockSpec(shp, lambda b, c: (0, 0))
    return pl.pallas_call(
        body, name=name, grid=(B, nc),
        in_specs=[pl.BlockSpec((rows, SSD_CONV_DIM), lambda b, c: (row(b, c), 0)),
                  pl.BlockSpec((rows, SSD_W), lambda b, c: (row(b, c), Z0 // SSD_W)),
                  pl.BlockSpec((rows, LANE), lambda b, c: (row(b, c), DT0 // LANE)),
                  pl.BlockSpec((k, SSD_STATE, SSD_W), lambda b, c: (row(b, c), 0, 0)),
                  pl.BlockSpec((rows, SSD_W), lambda b, c: (row(b, c), ATT_W // SSD_W)),
                  full((SUBLANE, SSD_W)), full((LANE, SSD_W)), full((SUBLANE, SSD_W))],
        out_specs=[pl.BlockSpec((rows, SSD_CONV_DIM), lambda b, c: (row(b, c), 0)),
                   pl.BlockSpec((rows, SSD_W), lambda b, c: (row(b, c), 0)),
                   pl.BlockSpec((rows, LANE), lambda b, c: (row(b, c), 0)),
                   full((SUBLANE, SSD_W))],
        out_shape=[jax.ShapeDtypeStruct((T, SSD_CONV_DIM), F32),
                   jax.ShapeDtypeStruct((T, SSD_W), BF16),
                   jax.ShapeDtypeStruct((T, LANE), BF16),
                   jax.ShapeDtypeStruct((SUBLANE, SSD_W), F32)],
        scratch_shapes=[pltpu.VMEM((SSD_STATE, SSD_W), F32)],
        compiler_params=_cp(("arbitrary", "arbitrary")),
    )(pre, proj, proj, sall, dy, par, e_mat, psel)


def _sgu_consts():
    e = np.zeros((SUBLANE, SGU_W), np.float32)
    for g in range(SGU_GROUPS):
        e[g, HEAD_DIM * g:HEAD_DIM * (g + 1)] = 1.0
    return jnp.asarray(e)


def _sgu_chunk(u_raw, v_raw, ln, w, bst, e4):
    L = CHUNK
    u = _gelu(u_raw)
    v = _gelu(v_raw)
    mu = jnp.mean(v, axis=-1, keepdims=True)
    vc = v - mu
    var = jnp.mean(vc * vc, axis=-1, keepdims=True)
    vn = vc * lax.rsqrt(var + LN_EPS) * ln[0:1] + ln[1:2]
    vb = vn.astype(BF16)
    ri = lax.broadcasted_iota(jnp.int32, (L, L), 0)
    ci = lax.broadcasted_iota(jnp.int32, (L, L), 1)
    tril = ri >= ci
    col = lax.broadcasted_iota(jnp.int32, (1, SGU_W), 1)
    mixed = _dot(bst, e4, HI)
    for g in range(SGU_GROUPS):
        wc = jnp.where(tril, w[g], 0.0).astype(BF16)
        gm = (col >= HEAD_DIM * g) & (col < HEAD_DIM * (g + 1))
        mixed = mixed + jnp.where(gm, _dot(wc, vb), 0.0)
    return u * mixed


def _sgu_fwd(proj, ln, w, bst, B, S, name):
    T = B * S
    nc = S // CHUNK
    e4 = _sgu_consts()

    def body(u_ref, v_ref, ln_ref, w_ref, b_ref, e_ref, y_ref):
        y_ref[...] = _sgu_chunk(u_ref[...], v_ref[...], ln_ref[...], w_ref[...], b_ref[...], e_ref[...]).astype(BF16)

    return pl.pallas_call(
        body, name=name, grid=(T // CHUNK,),
        in_specs=[pl.BlockSpec((CHUNK, SGU_W), lambda i: (i, U0 // SGU_W)),
                  pl.BlockSpec((CHUNK, SGU_W), lambda i: (i, VS0 // SGU_W)),
                  pl.BlockSpec((SUBLANE, SGU_W), lambda i: (0, 0)),
                  pl.BlockSpec((SGU_GROUPS, CHUNK, CHUNK), lambda i: (0, 0, 0)),
                  pl.BlockSpec((CHUNK, SUBLANE), lambda i: (0, 0)),
                  pl.BlockSpec((SUBLANE, SGU_W), lambda i: (0, 0))],
        out_specs=pl.BlockSpec((CHUNK, SGU_W), lambda i: (i, 0)),
        out_shape=jax.ShapeDtypeStruct((T, SGU_W), BF16),
        compiler_params=_cp(("parallel",)),
    )(proj, proj, ln, w, bst, e4)


def _sgu_bwd(proj, dy, ln, w, bst, B, S, name):
    T = B * S
    e4 = _sgu_consts()
    ycol = (ATT_W + SSD_W) // SGU_W

    def body(u_ref, v_ref, dy_ref, ln_ref, w_ref, b_ref, e_ref, du_ref, dv_ref, dln_ref, dw_ref, db_ref):
        @pl.when(pl.program_id(0) == 0)
        def _():
            dln_ref[...] = jnp.zeros_like(dln_ref)
            dw_ref[...] = jnp.zeros_like(dw_ref)
            db_ref[...] = jnp.zeros_like(db_ref)

        e_v = e_ref[...]
        fn = lambda u, v, ln, w, b: _sgu_chunk(u, v, ln, w, b, e_v)
        _, vjp = jax.vjp(fn, u_ref[...], v_ref[...], ln_ref[...], w_ref[...], b_ref[...])
        du, dv, dln, dw, db = vjp(dy_ref[...])
        du_ref[...] = du.astype(BF16)
        dv_ref[...] = dv.astype(BF16)
        dln_ref[...] += dln
        dw_ref[...] += dw
        db_ref[...] += db

    c_ln = pl.BlockSpec((SUBLANE, SGU_W), lambda i: (0, 0))
    c_w = pl.BlockSpec((SGU_GROUPS, CHUNK, CHUNK), lambda i: (0, 0, 0))
    c_b = pl.BlockSpec((CHUNK, SUBLANE), lambda i: (0, 0))
    return pl.pallas_call(
        body, name=name, grid=(T // CHUNK,),
        in_specs=[pl.BlockSpec((CHUNK, SGU_W), lambda i: (i, U0 // SGU_W)),
                  pl.BlockSpec((CHUNK, SGU_W), lambda i: (i, VS0 // SGU_W)),
                  pl.BlockSpec((CHUNK, SGU_W), lambda i: (i, ycol)),
                  c_ln, c_w, c_b, pl.BlockSpec((SUBLANE, SGU_W), lambda i: (0, 0))],
        out_specs=[pl.BlockSpec((CHUNK, SGU_W), lambda i: (i, 0)),
                   pl.BlockSpec((CHUNK, SGU_W), lambda i: (i, 0)), c_ln, c_w, c_b],
        out_shape=[jax.ShapeDtypeStruct((T, SGU_W), BF16), jax.ShapeDtypeStruct((T, SGU_W), BF16),
                   jax.ShapeDtypeStruct((SUBLANE, SGU_W), F32),
                   jax.ShapeDtypeStruct((SGU_GROUPS, CHUNK, CHUNK), F32),
                   jax.ShapeDtypeStruct((CHUNK, SUBLANE), F32)],
        compiler_params=_cp(("arbitrary",)),
    )(proj, proj, dy, ln, w, bst, e4)


_HBM = pl.BlockSpec(memory_space=pltpu.HBM)
_SEM = pl.BlockSpec(memory_space=pltpu.SEMAPHORE)
_ANY = pl.BlockSpec(memory_space=pl.ANY)
_EFFECT = pltpu.SideEffectType.DATAFLOW_SIDE_EFFECTING


def _peers():
    x, y, c = lax.axis_index("x"), lax.axis_index("y"), lax.axis_index("c")
    out = []
    for p in range(1, N_DEV):
        px, py, pc = x ^ ((p >> 2) & 1), y ^ ((p >> 1) & 1), c ^ (p & 1)
        out.append(((px, py, pc), 4 * px + 2 * py + pc))
    return 4 * x + 2 * y + c, out


def _xchg_start(xs, a2a, order, name):
    n = len(xs)
    lands = [lax.empty(a.shape if f else (N_DEV,) + a.shape, a.dtype) for a, f in zip(xs, a2a)]

    def body(*refs):
        ins, zones = refs[:n], refs[n:2 * n]
        send_sems, recv_sems = refs[2 * n + 1], refs[2 * n + 2]
        token = refs[-1]
        me, peers = _peers()
        for p, (dev, peer) in enumerate(peers):
            for t in range(n):
                pltpu.make_async_remote_copy(
                    src_ref=ins[t].at[peer] if a2a[t] else ins[t], dst_ref=zones[t].at[me],
                    send_sem=send_sems.at[p * n + t], recv_sem=recv_sems.at[p * n + t],
                    device_id=dev, device_id_type=MESH).start()
        token[...] = jnp.zeros_like(token)

    hbm = lambda a: pltpu.HBM(a.shape, a.dtype)
    sems = pltpu.SemaphoreType.DMA(((N_DEV - 1) * n,))
    out = pl.pallas_call(
        body, name=name,
        in_specs=[_HBM] * (2 * n) + [_ANY],
        out_specs=[_SEM, _SEM] + [_HBM] * (2 * n) + [pl.BlockSpec(memory_space=pltpu.VMEM)],
        out_shape=[sems, sems] + [hbm(a) for a in xs] + [hbm(a) for a in lands]
        + [jax.ShapeDtypeStruct((SUBLANE, LANE), F32)],
        input_output_aliases={t: 2 + t for t in range(2 * n)},
        compiler_params=pltpu.CompilerParams(has_side_effects=_EFFECT),
    )(*[pltpu.with_memory_space_constraint(a, pltpu.HBM) for a in list(xs) + list(lands)], order)
    return out[0], out[1], out[2:2 + n], out[2 + n:2 + 2 * n], out[-1]


def _xchg_wait(started, a2a, after, name):
    send_sems, recv_sems, xs, lands, _ = started
    n = len(xs)

    def body(*refs):
        ins, zones = refs[:n], refs[n:2 * n]
        send_s, recv_s = refs[2 * n], refs[2 * n + 1]
        me, peers = _peers()
        cps = []
        for p, (dev, peer) in enumerate(peers):
            for t in range(n):
                cps.append(pltpu.make_async_remote_copy(
                    src_ref=ins[t].at[peer] if a2a[t] else ins[t], dst_ref=zones[t].at[peer],
                    send_sem=send_s.at[p * n + t], recv_sem=recv_s.at[p * n + t],
                    device_id=dev, device_id_type=MESH))
        for cp in cps:
            cp.wait_recv()
        for cp in cps:
            cp.wait_send()

    hbm = lambda a: pltpu.HBM(a.shape, a.dtype)
    out = pl.pallas_call(
        body, name=name,
        in_specs=[_HBM] * (2 * n) + [_SEM, _SEM, _ANY],
        out_specs=[_HBM] * (2 * n),
        out_shape=[hbm(a) for a in xs] + [hbm(a) for a in lands],
        input_output_aliases={t: t for t in range(2 * n)},
        compiler_params=pltpu.CompilerParams(has_side_effects=_EFFECT),
    )(*xs, *lands, send_sems, recv_sems, after)
    return out[:n], out[n:]


def _chip_peers():
    x, y, c = lax.axis_index("x"), lax.axis_index("y"), lax.axis_index("c")
    chips = [(1 - x, y), (x, 1 - y), (1 - x, 1 - y)]
    slot = lambda px, py, pc: 4 * px + 2 * py + pc
    return (x, y, c), chips, slot


def _gather_start(xs, order, name):
    n = len(xs)
    lands = [lax.empty((N_DEV,) + a.shape, a.dtype) for a in xs]

    def body(*refs):
        ins, zones = refs[:n], refs[n:2 * n]
        send_sems, d2d_sems, ici_sems = refs[2 * n + 1:2 * n + 4]
        token = refs[-1]
        (x, y, c), chips, slot = _chip_peers()
        me = slot(x, y, c)
        for t in range(n):
            for j, (px, py) in enumerate(chips):
                pltpu.make_async_remote_copy(
                    src_ref=ins[t], dst_ref=zones[t].at[me], send_sem=send_sems.at[(1 + j) * n + t],
                    recv_sem=ici_sems.at[j * n + t], device_id=(px, py, c), device_id_type=MESH).start()
            pltpu.make_async_remote_copy(
                src_ref=ins[t], dst_ref=zones[t].at[me], send_sem=send_sems.at[t],
                recv_sem=d2d_sems.at[t], device_id=(x, y, 1 - c), device_id_type=MESH).start()
        token[...] = jnp.zeros_like(token)

    hbm = lambda a: pltpu.HBM(a.shape, a.dtype)
    dma = lambda k: pltpu.SemaphoreType.DMA((k,))
    out = pl.pallas_call(
        body, name=name,
        in_specs=[_HBM] * (2 * n) + [_ANY],
        out_specs=[_SEM, _SEM, _SEM] + [_HBM] * (2 * n) + [pl.BlockSpec(memory_space=pltpu.VMEM)],
        out_shape=[dma(4 * n), dma(n), dma(3 * n)] + [hbm(a) for a in xs] + [hbm(a) for a in lands]
        + [jax.ShapeDtypeStruct((SUBLANE, LANE), F32)],
        input_output_aliases={t: 3 + t for t in range(2 * n)},
        compiler_params=pltpu.CompilerParams(has_side_effects=_EFFECT),
    )(*[pltpu.with_memory_space_constraint(a, pltpu.HBM) for a in list(xs) + list(lands)], order)
    return dict(send=out[0], d2d=out[1], ici=out[2], xs=out[3:3 + n], lands=out[3 + n:3 + 2 * n], token=out[-1])


def _gather_relay(st, after, name):
    n = len(st["xs"])

    def body(*refs):
        zones, ici_sems = refs[:n], refs[n]
        fsend, frecv = refs[n + 2], refs[n + 3]
        token = refs[-1]
        (x, y, c), chips, slot = _chip_peers()
        for t in range(n):
            for j, (px, py) in enumerate(chips):
                blk = zones[t].at[slot(px, py, c)]
                fwd = pltpu.make_async_remote_copy(
                    src_ref=blk, dst_ref=blk, send_sem=fsend.at[j * n + t], recv_sem=ici_sems.at[j * n + t],
                    device_id=(x, y, 1 - c), device_id_type=MESH)
                fwd.wait_recv()
                pltpu.make_async_remote_copy(
                    src_ref=blk, dst_ref=blk, send_sem=fsend.at[j * n + t], recv_sem=frecv.at[j * n + t],
                    device_id=(x, y, 1 - c), device_id_type=MESH).start()
        token[...] = jnp.zeros_like(token)

    hbm = lambda a: pltpu.HBM(a.shape, a.dtype)
    dma = lambda k: pltpu.SemaphoreType.DMA((k,))
    out = pl.pallas_call(
        body, name=name,
        in_specs=[_HBM] * n + [_SEM, _ANY],
        out_specs=[_SEM, _SEM] + [_HBM] * n + [pl.BlockSpec(memory_space=pltpu.VMEM)],
        out_shape=[dma(3 * n), dma(3 * n)] + [hbm(a) for a in st["lands"]]
        + [jax.ShapeDtypeStruct((SUBLANE, LANE), F32)],
        input_output_aliases={t: 2 + t for t in range(n)},
        compiler_params=pltpu.CompilerParams(has_side_effects=_EFFECT),
    )(*st["lands"], st["ici"], after)
    return dict(st, fsend=out[0], frecv=out[1], lands=out[2:2 + n], token=out[-1])


def _gather_wait(st, after, name):
    n = len(st["xs"])

    def body(*refs):
        ins, zones = refs[:n], refs[n:2 * n]
        send_sems, d2d_sems, fsend, frecv = refs[2 * n:2 * n + 4]
        (x, y, c), chips, slot = _chip_peers()
        sib = (x, y, 1 - c)
        for t in range(n):
            mine = lambda s, r, dst: pltpu.make_async_remote_copy(
                src_ref=ins[t], dst_ref=dst, send_sem=s, recv_sem=r, device_id=sib, device_id_type=MESH)
            direct = mine(send_sems.at[t], d2d_sems.at[t], zones[t].at[slot(x, y, 1 - c)])
            direct.wait_recv()
            direct.wait_send()
            for j, (px, py) in enumerate(chips):
                mine(send_sems.at[(1 + j) * n + t], d2d_sems.at[t], zones[t].at[slot(px, py, c)]).wait_send()
                relayed = mine(fsend.at[j * n + t], frecv.at[j * n + t], zones[t].at[slot(px, py, 1 - c)])
                relayed.wait_recv()
                relayed.wait_send()

    hbm = lambda a: pltpu.HBM(a.shape, a.dtype)
    out = pl.pallas_call(
        body, name=name,
        in_specs=[_HBM] * (2 * n) + [_SEM] * 4 + [_ANY],
        out_specs=[_HBM] * (2 * n),
        out_shape=[hbm(a) for a in st["xs"]] + [hbm(a) for a in st["lands"]],
        input_output_aliases={t: t for t in range(2 * n)},
        compiler_params=pltpu.CompilerParams(has_side_effects=_EFFECT),
    )(*st["xs"], *st["lands"], st["send"], st["d2d"], st["fsend"], st["frecv"], after)
    return out[:n], out[n:]


def _cast_layers(pairs, name):
    def body(*refs):
        n = len(refs) // 2
        for i in range(n):
            refs[n + i][...] = refs[i][...].astype(BF16)

    in_specs = [pl.BlockSpec((None,) + w.shape[1:], functools.partial(lambda l, i: (l, 0, 0), l),
                             pipeline_mode=pl.Buffered(1)) for w, l in pairs]
    return pl.pallas_call(
        body, name=name, grid=(1,), in_specs=in_specs,
        out_specs=[pl.BlockSpec(w.shape[1:], lambda i: (0, 0)) for w, _ in pairs],
        out_shape=[jax.ShapeDtypeStruct(w.shape[1:], BF16) for w, _ in pairs],
        compiler_params=_cp(("arbitrary",)),
    )(*[w for w, _ in pairs])


def _adamw(me, w, m, v, parts, own, name, layer=0, into=None):
    L, R, C = w.shape
    P = parts.shape[0]
    tr = R
    t = 16
    while t <= R:
        if R % t == 0 and t * C <= 131072:
            tr = t
        t += 16
    if tr == R and R * C > 131072 and R % 16 == 0:
        tr = 16
    own_all = own.shape[0] == P

    def body(me_ref, w_ref, m_ref, v_ref, p_ref, own_ref, *rest):
        g_ref, d_ref, mo_ref, vo_ref = rest[-4:]
        mine = own_ref[...].astype(F32)
        g = None
        for p in range(P):
            term = jnp.where(me_ref[0] == p, mine, p_ref[p].astype(F32))
            g = term if g is None else g + term
        mn = ADAM_B1 * m_ref[...] + (1.0 - ADAM_B1) * g
        vn = ADAM_B2 * v_ref[...] + (1.0 - ADAM_B2) * (g * g)
        m_hat = mn / (1.0 - ADAM_B1 ** ADAM_STEP)
        v_hat = vn / (1.0 - ADAM_B2 ** ADAM_STEP)
        g_ref[...] = g
        d_ref[...] = -ADAM_LR * (m_hat / (jnp.sqrt(v_hat) + ADAM_EPS) + ADAM_WD * w_ref[...])
        mo_ref[...] = mn
        vo_ref[...] = vn

    blk = pl.BlockSpec((None, tr, C), lambda i, me_ref: (layer, i, 0))
    own_blk = pl.BlockSpec((None, tr, C), lambda i, me_ref: (me_ref[0] if own_all else 0, i, 0))
    prev = list(into) if into is not None else []
    return pl.pallas_call(
        body, name=name,
        grid_spec=pltpu.PrefetchScalarGridSpec(
            num_scalar_prefetch=1, grid=(R // tr,),
            in_specs=[blk, blk, blk, pl.BlockSpec((P, tr, C), lambda i, me_ref: (0, i, 0)), own_blk]
            + [_ANY] * len(prev),
            out_specs=[blk] * 4),
        out_shape=[jax.ShapeDtypeStruct((L, R, C), F32)] * 4,
        input_output_aliases={6 + i: i for i in range(len(prev))},
        compiler_params=_cp(("parallel",)),
    )(me, w, m, v, parts, own, *prev)


def _perm_cols(w):
    pad = jnp.zeros((w.shape[0], LANE - SSD_HEADS), w.dtype)
    return jnp.concatenate([w[:, 0:1536], w[:, 2438:2694], w[:, 1536:2432], w[:, 2432:2438], pad,
                            w[:, 2694:2950]], axis=1)


def _unperm_cols(w):
    return jnp.concatenate([w[:, 0:1536], w[:, XBC0:XBC0 + SSD_CONV_DIM], w[:, DT0:DT0 + SSD_HEADS],
                            w[:, U0:U0 + SGU_W], w[:, VS0:VS0 + SGU_W]], axis=1)


_SMALL = ("ffn1_norm", "mix_norm", "conv_w", "conv_b", "dt_bias", "a_log", "d_skip", "ssd_norm",
          "sgu_ln_g", "sgu_ln_b", "sgu_w", "sgu_b", "ffn2_norm", "final_norm", "loss")


def _pack(d):
    v = jnp.concatenate([d[k].astype(F32).reshape(-1) for k in _SMALL])
    n = v.shape[0]
    npad = -(-n // (LANE * 16)) * (LANE * 16)
    return jnp.pad(v, (0, npad - n)).reshape(npad // LANE, LANE)


def _unpack(p, shapes):
    v = p.reshape(-1)
    out, o = {}, 0
    for k in _SMALL:
        n = int(np.prod(shapes[k]))
        out[k] = v[o:o + n].reshape(shapes[k])
        o += n
    return out


def kernel(x, ffn1_norm, ffn1_w_gate, ffn1_w_up, ffn1_w_down, mix_norm, w_in, conv_w, conv_b, dt_bias, a_log, d_skip, ssd_norm, sgu_ln_g, sgu_ln_b, sgu_w, sgu_b, w_out, ffn2_norm, ffn2_w_gate, ffn2_w_up, ffn2_w_down, final_norm, loss_target, m_ffn1_norm, m_ffn1_w_gate, m_ffn1_w_up, m_ffn1_w_down, m_mix_norm, m_w_in, m_conv_w, m_conv_b, m_dt_bias, m_a_log, m_d_skip, m_ssd_norm, m_sgu_ln_g, m_sgu_ln_b, m_sgu_w, m_sgu_b, m_w_out, m_ffn2_norm, m_ffn2_w_gate, m_ffn2_w_up, m_ffn2_w_down, m_final_norm, v_ffn1_norm, v_ffn1_w_gate, v_ffn1_w_up, v_ffn1_w_down, v_mix_norm, v_w_in, v_conv_w, v_conv_b, v_dt_bias, v_a_log, v_d_skip, v_ssd_norm, v_sgu_ln_g, v_sgu_ln_b, v_sgu_w, v_sgu_b, v_w_out, v_ffn2_norm, v_ffn2_w_gate, v_ffn2_w_up, v_ffn2_w_down, v_final_norm):
    B, S, D = x.shape
    T = B * S
    L = ffn1_norm.shape[0]
    me = 4 * lax.axis_index("x") + 2 * lax.axis_index("y") + lax.axis_index("c")
    cs = conv_w.shape[2]
    W = dict(ffn1_norm=ffn1_norm, ffn1_w_gate=ffn1_w_gate, ffn1_w_up=ffn1_w_up, ffn1_w_down=ffn1_w_down,
             mix_norm=mix_norm, w_in=w_in, conv_w=conv_w, conv_b=conv_b, dt_bias=dt_bias, a_log=a_log,
             d_skip=d_skip, ssd_norm=ssd_norm, sgu_ln_g=sgu_ln_g, sgu_ln_b=sgu_ln_b, sgu_w=sgu_w, sgu_b=sgu_b,
             w_out=w_out, ffn2_norm=ffn2_norm, ffn2_w_gate=ffn2_w_gate, ffn2_w_up=ffn2_w_up,
             ffn2_w_down=ffn2_w_down, final_norm=final_norm)
    M = dict(ffn1_norm=m_ffn1_norm, ffn1_w_gate=m_ffn1_w_gate, ffn1_w_up=m_ffn1_w_up, ffn1_w_down=m_ffn1_w_down,
             mix_norm=m_mix_norm, w_in=m_w_in, conv_w=m_conv_w, conv_b=m_conv_b, dt_bias=m_dt_bias, a_log=m_a_log,
             d_skip=m_d_skip, ssd_norm=m_ssd_norm, sgu_ln_g=m_sgu_ln_g, sgu_ln_b=m_sgu_ln_b, sgu_w=m_sgu_w,
             sgu_b=m_sgu_b, w_out=m_w_out, ffn2_norm=m_ffn2_norm, ffn2_w_gate=m_ffn2_w_gate,
             ffn2_w_up=m_ffn2_w_up, ffn2_w_down=m_ffn2_w_down, final_norm=m_final_norm)
    V = dict(ffn1_norm=v_ffn1_norm, ffn1_w_gate=v_ffn1_w_gate, ffn1_w_up=v_ffn1_w_up, ffn1_w_down=v_ffn1_w_down,
             mix_norm=v_mix_norm, w_in=v_w_in, conv_w=v_conv_w, conv_b=v_conv_b, dt_bias=v_dt_bias, a_log=v_a_log,
             d_skip=v_d_skip, ssd_norm=v_ssd_norm, sgu_ln_g=v_sgu_ln_g, sgu_ln_b=v_sgu_ln_b, sgu_w=v_sgu_w,
             sgu_b=v_sgu_b, w_out=v_w_out, ffn2_norm=v_ffn2_norm, ffn2_w_gate=v_ffn2_w_gate,
             ffn2_w_up=v_ffn2_w_up, ffn2_w_down=v_ffn2_w_down, final_norm=v_final_norm)
    FFN1 = ("ffn1_w_gate", "ffn1_w_up", "ffn1_w_down")
    FFN2 = ("ffn2_w_gate", "ffn2_w_up", "ffn2_w_down")
    MIX = ("w_in", "w_out")
    big = FFN1 + MIX + FFN2

    wgroups = [[(k, 0) for k in FFN1], [("w_in", 0), ("conv_w", None)], [("w_out", 0)] + [(k, 0) for k in FFN2]]
    wgroups += [[(k, l) for k in big] for l in range(1, L)]
    wstarted, order = [], x
    later = [kl for grp in wgroups[1:] for kl in grp if kl[0] != "conv_w"]
    cast = dict(zip(wgroups[0], _cast_layers([(W[k], l) for k, l in wgroups[0]], "cast_first")))
    for gi, grp in enumerate(wgroups):
        if gi == 1:
            first = lax.optimization_barrier((W[later[0][0]], order))[0]
            srcs = [(first if i == 0 else W[k], l) for i, (k, l) in enumerate(later)]
            cast.update(zip(later, _cast_layers(srcs, "cast_rest")))
        xs = [conv_w if k == "conv_w" else cast[(k, l)] for k, l in grp]
        st = _gather_start(xs, order, f"gather_start_{gi}")
        order = st["token"]
        wstarted.append(st)
    G = {}
    is_me = (jnp.arange(N_DEV) == me)

    def relay(gi, after):
        wstarted[gi] = _gather_relay(wstarted[gi], after, f"gather_relay_{gi}")
        return wstarted[gi]["token"]

    def gathered(gi, after):
        own, lands = _gather_wait(wstarted[gi], after, f"gather_wait_{gi}")
        for key, o, z in zip(wgroups[gi], own, lands):
            G[key] = jnp.where(is_me.reshape((N_DEV,) + (1,) * o.ndim), o[None], z)

    def cols(k, l):
        a = G[(k, l)]
        return jnp.transpose(a, (1, 0, 2)).reshape(a.shape[1], -1)

    def rows(k, l):
        a = G[(k, l)]
        return a.reshape(-1, a.shape[-1])

    bias = _attn_bias(S, min(256, S))
    row1 = lambda a: a.reshape(1, -1)

    def ffn1_params(l):
        return dict(g1=row1(ffn1_norm[l]), wg1=cols("ffn1_w_gate", l), wu1=cols("ffn1_w_up", l),
                    wd1=rows("ffn1_w_down", l))

    def out_params(l):
        return dict(wout=rows("w_out", l), g2=row1(ffn2_norm[l]), wg2=cols("ffn2_w_gate", l),
                    wu2=cols("ffn2_w_up", l), wd2=rows("ffn2_w_down", l))

    def mix_params(l):
        cw = jnp.transpose(G[("conv_w", None)][:, l], (1, 0, 2)).reshape(SSD_CONV, -1)
        return dict(
            gm=row1(mix_norm[l]), win=_perm_cols(rows("w_in", l)),
            cw=jnp.pad(cw, ((0, SUBLANE - SSD_CONV), (0, 0))), cb=row1(conv_b[l]),
            par=jnp.pad(jnp.stack([jnp.repeat(dt_bias[l], HEAD_DIM), jnp.repeat(a_log[l], HEAD_DIM),
                                   jnp.repeat(d_skip[l], HEAD_DIM), ssd_norm[l]]), ((0, SUBLANE - 4), (0, 0))),
            ln=jnp.pad(jnp.stack([sgu_ln_g[l], sgu_ln_b[l]]), ((0, SUBLANE - 2), (0, 0))),
            sw=sgu_w[l], bst=jnp.pad(sgu_b[l].T, ((0, 0), (0, SUBLANE - SGU_GROUPS))))

    xc = x.reshape(T, D)
    saved, lay = [], []
    tie = lambda a, tok: lax.optimization_barrier((a, tok))[0]
    for l in range(L):
        gathered(0 if l == 0 else l + 2, relay(0, order) if l == 0 else xc)
        p = ffn1_params(l)
        x1, gate1, up1 = _ffn_fwd(xc, p["g1"], p["wg1"], p["wu1"], p["wd1"], f"ffn1_fwd_{l}")
        if l == 0:
            gathered(1, relay(1, x1))
        p.update(mix_params(l))
        lay.append(p)
        proj, ht = _norm_mm(x1, p["gm"], p["win"], f"in_proj_{l}")
        if l == 0:
            proj = tie(proj, relay(2, proj))
        o_att, lse = _attn_fwd(proj, bias, B, S, f"attn_fwd_{l}")
        pre = _conv_fwd(proj, p["cw"], p["cb"], B, S, f"conv_fwd_{l}")
        y_ssd, sall = _ssd_fwd(pre, proj, p["par"], B, S, f"ssd_fwd_{l}")
        y_sgu = _sgu_fwd(proj, p["ln"], p["sw"], p["bst"], B, S, f"sgu_fwd_{l}")
        ycat = jnp.concatenate([o_att.astype(BF16), y_ssd, y_sgu], axis=1)
        if l + 1 < L:
            ycat = tie(ycat, relay(l + 3, ycat))
        if l == 0:
            gathered(2, ycat)
        p.update(out_params(l))
        x2 = _mm(ycat, p["wout"], "nn", f"out_proj_{l}", residual=x1)
        x3, gate2, up2 = _ffn_fwd(x2, p["g2"], p["wg2"], p["wu2"], p["wd2"], f"ffn2_fwd_{l}")
        saved.append(dict(x0=xc, gate1=gate1, up1=up1, x1=x1, ht=ht, proj=proj, o_att=o_att, lse=lse, pre=pre,
                          sall=sall, ycat=ycat, x2=x2, gate2=gate2, up2=up2))
        xc = x3
    loss_part, dx, dgf = _final_loss(xc, row1(final_norm), loss_target.reshape(T, D), "final_loss")

    gl = [dict() for _ in range(L)]
    gstarted, gorder = [], [order]

    def to_blocks(k, a):
        if k.endswith("w_gate") or k.endswith("w_up"):
            a = a.reshape(a.shape[0], N_DEV, -1).transpose(1, 0, 2)
        elif k.endswith("w_down"):
            a = a.reshape(a.shape[0], N_DEV, -1).transpose(1, 2, 0)
        else:
            a = a.reshape(N_DEV, -1, a.shape[-1])
        return a.astype(BF16)

    def send_grads(keys, l, extra, tag):
        xs = [to_blocks(k, gl[l][k]) for k in keys] + extra
        flags = [True] * len(keys) + [False] * len(extra)
        st = _xchg_start(xs, flags, gorder[0], f"grads_start_{tag}")
        gorder[0] = st[-1]
        gstarted.append((keys, l, st, flags, tag))

    def behind(a):
        return lax.optimization_barrier((a, gorder[0]))[0]

    for l in reversed(range(L)):
        p, s, g = lay[l], saved[l], gl[l]
        dx2, dgate, dup, act, xnt, dacct, g["ffn2_norm"] = _ffn_bwd_dx(
            dx, s["x2"], p["g2"], s["gate2"], s["up2"], p["wg2"], p["wu2"], p["wd2"], f"ffn2_bwd_{l}")
        g["ffn2_w_gate"], g["ffn2_w_up"], g["ffn2_w_down"] = _ffn_dw(xnt, dacct, dgate, dup, act, f"ffn2_dw_{l}")
        if l == 0:
            send_grads(FFN2, 0, [], "l0f")
            dx2 = behind(dx2)
        dycat = _mm(dx2, p["wout"], "nt", f"out_proj_dx_{l}")
        g["w_out"] = _mm(s["ycat"], dx2, "tn", f"out_proj_dw_{l}", out_dtype=BF16, tm_cap=1024, tk_cap=512)
        dq, dk, dv = _attn_bwd(s["proj"], s["o_att"], s["lse"], dycat, bias, B, S, f"attn_bwd_{l}")
        dpre, dz, ddt, dpar = _ssd_bwd(s["pre"], s["proj"], s["sall"], dycat, p["par"], B, S, f"ssd_bwd_{l}")
        dxbc, dwb = _conv_bwd(dpre, s["proj"], p["cw"], B, S, f"conv_bwd_{l}")
        du, dvs, dln, dsw, dbst = _sgu_bwd(s["proj"], dycat, p["ln"], p["sw"], p["bst"], B, S, f"sgu_bwd_{l}")
        dproj = jnp.concatenate([dq, dk, dv, dz, du, dxbc, ddt, dvs], axis=1)
        g["w_in"] = _unperm_cols(_mm_resident_lhs(s["ht"], dproj, f"in_proj_dw_{l}"))
        dx1, g["mix_norm"] = _norm_mm_bwd(dproj, s["x1"], p["gm"], p["win"], dx2, f"in_proj_bwd_{l}")
        if l == 0:
            send_grads(MIX, 0, [], "l0a")
            dx1 = behind(dx1)
        dx, dgate, dup, act, xnt, dacct, g["ffn1_norm"] = _ffn_bwd_dx(
            dx1, s["x0"], p["g1"], s["gate1"], s["up1"], p["wg1"], p["wu1"], p["wd1"], f"ffn1_bwd_{l}")
        g["ffn1_w_gate"], g["ffn1_w_up"], g["ffn1_w_down"] = _ffn_dw(xnt, dacct, dgate, dup, act, f"ffn1_dw_{l}")
        if l > 0:
            send_grads(big, l, [], f"l{l}")
            dx = behind(dx)
        hsum = lambda r: r.reshape(SSD_HEADS, HEAD_DIM).sum(-1)
        g["conv_w"], g["conv_b"] = dwb[:SSD_CONV], dwb[SSD_CONV]
        g["dt_bias"], g["a_log"], g["d_skip"], g["ssd_norm"] = hsum(dpar[0]), hsum(dpar[1]), hsum(dpar[2]), dpar[3]
        g["sgu_ln_g"], g["sgu_ln_b"], g["sgu_w"], g["sgu_b"] = dln[0], dln[1], dsw, dbst[:, :SGU_GROUPS].T
    grad_x = dx.reshape(B, S, D)

    stack = lambda k: jnp.stack([gl[l][k] for l in range(L)])
    zero1 = jnp.zeros((1,), F32)
    W["loss"], M["loss"], V["loss"] = zero1, zero1, zero1
    per_layer = lambda k: k not in ("final_norm", "conv_w", "loss")
    small = {k: stack(k) for k in _SMALL if per_layer(k) or k == "conv_w"}
    small["final_norm"], small["loss"] = dgf.reshape(-1), loss_part[0, :1]
    small = {k: small[k].reshape((L,) + W[k].shape[1:]) if per_layer(k) else small[k] for k in _SMALL}
    full_shapes = {k: (W[k].shape if k != "conv_w" else (L, SSD_CONV, SSD_CONV_DIM)) for k in _SMALL}
    send_grads(FFN1, 0, [_pack(small)], "l0b")

    res, after, small_parts = {}, gorder[0], None
    me1 = me.reshape(1).astype(jnp.int32)
    for keys, l, st, flags, tag in gstarted:
        own, lands = _xchg_wait(st, flags, after, f"grads_wait_{tag}")
        for k, mine, pk in zip(keys, own, lands):
            res[k] = _adamw(me1, W[k], M[k], V[k], pk, mine, f"adamw_{k}_{l}", layer=l, into=res.get(k))
        after = lax.optimization_barrier(tuple(res[k][0] for k in keys))[0]
        if len(lands) > len(keys):
            small_parts = (lands[-1], own[-1][None])
    grads, deltas, new_m, new_v = [{k: res[k][i] for k in big} for i in range(4)]

    def embed(a, k):
        if k != "conv_w":
            return a
        return lax.dynamic_update_slice(jnp.zeros(full_shapes[k], F32), a, (0, 0, me * cs))

    outs = _adamw(me1, _pack({k: embed(W[k], k) for k in _SMALL})[None],
                  _pack({k: embed(M[k], k) for k in _SMALL})[None],
                  _pack({k: embed(V[k], k) for k in _SMALL})[None], small_parts[0], small_parts[1], "adamw_small")
    for d, o in zip((grads, deltas, new_m, new_v), outs):
        u = _unpack(o, full_shapes)
        u["conv_w"] = lax.dynamic_slice(u["conv_w"], (0, 0, me * cs), (L, SSD_CONV, cs))
        d.update(u)

    names = ("ffn1_norm", "ffn1_w_gate", "ffn1_w_up", "ffn1_w_down", "mix_norm", "w_in", "conv_w", "conv_b",
             "dt_bias", "a_log", "d_skip", "ssd_norm", "sgu_ln_g", "sgu_ln_b", "sgu_w", "sgu_b", "w_out",
             "ffn2_norm", "ffn2_w_gate", "ffn2_w_up", "ffn2_w_down", "final_norm")
    loss = grads["loss"][0]
    return (loss, grad_x, *[grads[n] for n in names], *[deltas[n] for n in names],
            *[new_m[n] for n in names], *[new_v[n] for n in names])
```

```python
import functools

import numpy as np
import jax
import jax.numpy as jnp
from jax import lax
from jax.experimental import pallas as pl
from jax.experimental.pallas import tpu as pltpu

F32, BF16 = jnp.float32, jnp.bfloat16
HI = lax.Precision.HIGHEST
MESH = pl.DeviceIdType.MESH
N_DEV = 8
VMEM_LIMIT_BYTES = 56 * 1024 * 1024
LANE, SUBLANE = 128, 8

HEAD_DIM = 64
ATT_W = 384
SSD_W = 384
SSD_HEADS = 6
SSD_STATE = 128
SSD_CONV = 4
CHUNK = 128
SSD_CONV_DIM = 896
SGU_W = 256
SGU_GROUPS = 4
D_IN = 2950
RMS_EPS = 1e-6
LN_EPS = 1e-5
NEG = -1e30

PW = 3072
Q0, K0, V0, Z0, U0, XBC0, DT0, VS0 = 0, 384, 768, 1152, 1536, 1792, 2688, 2816

ADAM_LR, ADAM_B1, ADAM_B2, ADAM_EPS, ADAM_WD, ADAM_STEP = 0.001, 0.9, 0.999, 1e-08, 0.01, 10


def _cp(sem=None):
    return pltpu.CompilerParams(dimension_semantics=sem, vmem_limit_bytes=VMEM_LIMIT_BYTES)


def _tile(n, cap, mult=LANE):
    best = None
    t = mult
    while t <= min(n, cap):
        if n % t == 0:
            best = t
        t += mult
    return best if best is not None else n


def _dot(a, b, prec=None):
    return jnp.dot(a, b, preferred_element_type=F32, precision=prec)


def _dot_nt(a, b, prec=None):
    return lax.dot_general(a, b, (((1,), (1,)), ((), ())), preferred_element_type=F32, precision=prec)


def _dot_tn(a, b, prec=None):
    return lax.dot_general(a, b, (((0,), (0,)), ((), ())), preferred_element_type=F32, precision=prec)


def _sigmoid(x):
    return 1.0 / (1.0 + jnp.exp(-x))


def _silu(x):
    return x * _sigmoid(x)


def _gelu(x):
    return 0.5 * x * (1.0 + lax.erf(x * 0.7071067811865476))


def _softplus(x):
    return jnp.maximum(x, 0.0) + jnp.log(1.0 + jnp.exp(-jnp.abs(x)))


def _rms_fwd(x, g):
    rstd = lax.rsqrt(jnp.mean(x * x, axis=-1, keepdims=True) + RMS_EPS)
    xhat = x * rstd
    return xhat * g, xhat, rstd


def _rms_bwd(dy, xhat, rstd, g):
    dxhat = dy * g
    dx = rstd * (dxhat - xhat * jnp.mean(dxhat * xhat, axis=-1, keepdims=True))
    return dx, dy * xhat


def _resident(shape):
    return pl.BlockSpec(shape, lambda *_: (0,) * len(shape), pipeline_mode=pl.Buffered(1))


def _mm(a, b, mode, name, out_dtype=F32, residual=None, tm_cap=512, tn_cap=1024, tk_cap=1024):
    if mode == "nn":
        (M, K), (_, N) = a.shape, b.shape
    elif mode == "nt":
        (M, K), (N, _) = a.shape, b.shape
    else:
        (K, M), (_, N) = a.shape, b.shape
    tm, tn, tk = _tile(M, tm_cap), _tile(N, tn_cap), _tile(K, tk_cap)
    nk = K // tk
    if mode == "tn":
        a_spec = pl.BlockSpec((tk, tm), lambda i, j, k: (k, i))
    else:
        a_spec = pl.BlockSpec((tm, tk), lambda i, j, k: (i, k))
    if mode == "nt":
        b_spec = pl.BlockSpec((tn, tk), lambda i, j, k: (j, k))
    else:
        b_spec = pl.BlockSpec((tk, tn), lambda i, j, k: (k, j))
    o_spec = pl.BlockSpec((tm, tn), lambda i, j, k: (i, j))
    has_res = residual is not None

    def prod(a_ref, b_ref):
        av = a_ref[...].astype(BF16)
        bv = b_ref[...].astype(BF16)
        if mode == "nn":
            return _dot(av, bv)
        if mode == "nt":
            return _dot_nt(av, bv)
        return _dot_tn(av, bv)

    def body(*refs):
        a_ref, b_ref = refs[:2]
        r_ref = refs[2] if has_res else None
        o_ref = refs[2 + has_res]
        if nk == 1:
            o = prod(a_ref, b_ref)
            if has_res:
                o = r_ref[...] + o
            o_ref[...] = o.astype(out_dtype)
            return
        acc = refs[3 + has_res]
        k = pl.program_id(2)

        @pl.when(k == 0)
        def _():
            acc[...] = jnp.zeros_like(acc)

        acc[...] += prod(a_ref, b_ref)

        @pl.when(k == nk - 1)
        def _():
            o = acc[...]
            if has_res:
                o = r_ref[...] + o
            o_ref[...] = o.astype(out_dtype)

    ins = [a, b] + ([residual] if has_res else [])
    in_specs = [a_spec, b_spec] + ([o_spec] if has_res else [])
    return pl.pallas_call(
        body, name=name, grid=(M // tm, N // tn, nk),
        in_specs=in_specs, out_specs=o_spec,
        out_shape=jax.ShapeDtypeStruct((M, N), out_dtype),
        scratch_shapes=[pltpu.VMEM((tm, tn), F32)] if nk > 1 else [],
        compiler_params=_cp(("parallel", "parallel", "arbitrary")),
    )(*ins)


def _ffn_fwd(x, g, wgt, wut, wd, name):
    T, D = x.shape
    F = wgt.shape[0]
    tm = _tile(T, 512)

    def body(x_ref, g_ref, wg_ref, wu_ref, wd_ref, out_ref, gate_ref, up_ref):
        xv = x_ref[...]
        xn = _rms_fwd(xv, g_ref[...])[0].astype(BF16)
        gate = _dot_nt(xn, wg_ref[...])
        up = _dot_nt(xn, wu_ref[...])
        gate_ref[...] = gate.astype(BF16)
        up_ref[...] = up.astype(BF16)
        act = (_silu(gate) * up).astype(BF16)
        out_ref[...] = xv + 0.5 * _dot(act, wd_ref[...])

    row = lambda w: pl.BlockSpec((tm, w), lambda i: (i, 0))
    return pl.pallas_call(
        body, name=name, grid=(T // tm,),
        in_specs=[row(D), _resident((1, D)), _resident((F, D)), _resident((F, D)), _resident((F, D))],
        out_specs=[row(D), row(F), row(F)],
        out_shape=[jax.ShapeDtypeStruct((T, D), F32),
                   jax.ShapeDtypeStruct((T, F), BF16),
                   jax.ShapeDtypeStruct((T, F), BF16)],
        compiler_params=_cp(("parallel",)),
    )(x, g, wgt, wut, wd)


def _ffn_bwd_dx(dout, x, g, gate, up, wg, wu, wd, name):
    T, D = x.shape
    F = wg.shape[0]
    tm, th = _tile(T, 256), _tile(F, 256)
    nj = F // th

    def body(dout_ref, x_ref, g_ref, gate_ref, up_ref, wg_ref, wu_ref, wd_ref,
             dx_ref, dgate_ref, dup_ref, act_ref, xnt_ref, dacct_ref, dg_ref):
        @pl.when(pl.program_id(0) == 0)
        def _():
            dg_ref[...] = jnp.zeros_like(dg_ref)

        gv = g_ref[...]
        dout_v = dout_ref[...]
        xn, xhat, rstd = _rms_fwd(x_ref[...], gv)
        xnt_ref[...] = xn.T.astype(BF16)
        dacc = 0.5 * dout_v
        dacct_ref[...] = dacc.T.astype(BF16)
        dact = _dot_nt(dacc.astype(BF16), wd_ref[...])
        gt = gate_ref[...].astype(F32)
        u = up_ref[...].astype(F32)
        sig = _sigmoid(gt)
        sl = gt * sig
        dgate = (dact * u * (sig * (1.0 + gt * (1.0 - sig)))).astype(BF16)
        dup = (dact * sl).astype(BF16)
        act = (sl * u).astype(BF16)
        for j in range(nj):
            cs = slice(j * th, (j + 1) * th)
            dgate_ref[j] = dgate[:, cs]
            dup_ref[j] = dup[:, cs]
            act_ref[j] = act[:, cs]
        dxn = _dot(dgate, wg_ref[...]) + _dot(dup, wu_ref[...])
        dx, dgrow = _rms_bwd(dxn, xhat, rstd, gv)
        dx_ref[...] = dout_v + dx
        dg_ref[...] += jnp.sum(dgrow, axis=0, keepdims=True)

    row = lambda w: pl.BlockSpec((tm, w), lambda i: (i, 0))
    tiled = pl.BlockSpec((nj, tm, th), lambda i: (0, i, 0))
    tr = pl.BlockSpec((D, tm), lambda i: (0, i))
    return pl.pallas_call(
        body, name=name, grid=(T // tm,),
        in_specs=[row(D), row(D), _resident((1, D)), row(F), row(F),
                  _resident((F, D)), _resident((F, D)), _resident((F, D))],
        out_specs=[row(D), tiled, tiled, tiled, tr, tr, pl.BlockSpec((1, D), lambda i: (0, 0))],
        out_shape=[jax.ShapeDtypeStruct((T, D), F32)] + [jax.ShapeDtypeStruct((nj, T, th), BF16)] * 3
        + [jax.ShapeDtypeStruct((D, T), BF16)] * 2 + [jax.ShapeDtypeStruct((1, D), F32)],
        compiler_params=_cp(("arbitrary",)),
    )(dout, x, g, gate, up, wg, wu, wd)


def _ffn_dw(xnt, dacct, dgate, dup, act, name):
    D, T = xnt.shape
    nj, _, th = dgate.shape

    def body(xnt_ref, dacct_ref, dgate_ref, dup_ref, act_ref, dwg_ref, dwu_ref, dwdt_ref):
        xv = xnt_ref[...]
        dwg_ref[...] = _dot(xv, dgate_ref[...]).astype(BF16)
        dwu_ref[...] = _dot(xv, dup_ref[...]).astype(BF16)
        dwdt_ref[...] = _dot(dacct_ref[...], act_ref[...]).astype(BF16)

    tile = pl.BlockSpec((None, T, th), lambda j: (j, 0, 0))
    out = pl.BlockSpec((D, th), lambda j: (0, j))
    return pl.pallas_call(
        body, name=name, grid=(nj,),
        in_specs=[_resident((D, T)), _resident((D, T)), tile, tile, tile],
        out_specs=[out, out, out], out_shape=[jax.ShapeDtypeStruct((D, nj * th), BF16)] * 3,
        compiler_params=_cp(("parallel",)),
    )(xnt, dacct, dgate, dup, act)


def _norm_mm(x, g, w, name):
    T, D = x.shape
    N = w.shape[1]
    tm = _tile(T, 512)

    def body(x_ref, g_ref, w_ref, o_ref, ht_ref):
        xn = _rms_fwd(x_ref[...], g_ref[...])[0]
        ht_ref[...] = xn.T.astype(BF16)
        o_ref[...] = _dot(xn.astype(BF16), w_ref[...])

    return pl.pallas_call(
        body, name=name, grid=(T // tm,),
        in_specs=[pl.BlockSpec((tm, D), lambda i: (i, 0)), _resident((1, D)), _resident((D, N))],
        out_specs=[pl.BlockSpec((tm, N), lambda i: (i, 0)), pl.BlockSpec((D, tm), lambda i: (0, i))],
        out_shape=[jax.ShapeDtypeStruct((T, N), F32), jax.ShapeDtypeStruct((D, T), BF16)],
        compiler_params=_cp(("parallel",)),
    )(x, g, w)


def _norm_mm_bwd(dproj, x, g, w, dres, name):
    T, D = x.shape
    N = w.shape[1]
    tm = _tile(T, 512)

    def body(dp_ref, x_ref, g_ref, w_ref, dres_ref, dx_ref, dg_ref):
        @pl.when(pl.program_id(0) == 0)
        def _():
            dg_ref[...] = jnp.zeros_like(dg_ref)

        gv = g_ref[...]
        dh = _dot_nt(dp_ref[...], w_ref[...])
        _, xhat, rstd = _rms_fwd(x_ref[...], gv)
        dx, dgrow = _rms_bwd(dh, xhat, rstd, gv)
        dx_ref[...] = dres_ref[...] + dx
        dg_ref[...] += jnp.sum(dgrow, axis=0, keepdims=True)

    row = pl.BlockSpec((tm, D), lambda i: (i, 0))
    one = pl.BlockSpec((1, D), lambda i: (0, 0))
    return pl.pallas_call(
        body, name=name, grid=(T // tm,),
        in_specs=[pl.BlockSpec((tm, N), lambda i: (i, 0)), row, _resident((1, D)), _resident((D, N)), row],
        out_specs=[row, one],
        out_shape=[jax.ShapeDtypeStruct((T, D), F32), jax.ShapeDtypeStruct((1, D), F32)],
        compiler_params=_cp(("arbitrary",)),
    )(dproj, x, g, w, dres)


def _mm_resident_lhs(at, b, name, tn_cap=512):
    M, K = at.shape
    N = b.shape[1]
    tn = _tile(N, tn_cap)

    def body(a_ref, b_ref, o_ref):
        o_ref[...] = _dot(a_ref[...], b_ref[...]).astype(BF16)

    return pl.pallas_call(
        body, name=name, grid=(N // tn,),
        in_specs=[_resident((M, K)), pl.BlockSpec((K, tn), lambda j: (0, j))],
        out_specs=pl.BlockSpec((M, tn), lambda j: (0, j)),
        out_shape=jax.ShapeDtypeStruct((M, N), BF16),
        compiler_params=_cp(("parallel",)),
    )(at, b)


def _final_loss(x, g, target, name):
    T, D = x.shape
    tm = _tile(T, 512)

    def body(x_ref, g_ref, t_ref, loss_ref, dx_ref, dg_ref):
        @pl.when(pl.program_id(0) == 0)
        def _():
            dg_ref[...] = jnp.zeros_like(dg_ref)
            loss_ref[...] = jnp.zeros_like(loss_ref)

        gv = g_ref[...]
        y, xhat, rstd = _rms_fwd(x_ref[...], gv)
        err = y - t_ref[...]
        part = 0.5 * jnp.sum(jnp.mean(err * err, axis=-1, keepdims=True), axis=0, keepdims=True)
        loss_ref[...] += jnp.broadcast_to(part, loss_ref.shape)
        dy = err * (1.0 / D)
        dx, dgrow = _rms_bwd(dy, xhat, rstd, gv)
        dx_ref[...] = dx
        dg_ref[...] += jnp.sum(dgrow, axis=0, keepdims=True)

    row = pl.BlockSpec((tm, D), lambda i: (i, 0))
    one = pl.BlockSpec((1, D), lambda i: (0, 0))
    return pl.pallas_call(
        body, name=name, grid=(T // tm,),
        in_specs=[row, one, row],
        out_specs=[pl.BlockSpec((1, LANE), lambda i: (0, 0)), row, one],
        out_shape=[jax.ShapeDtypeStruct((1, LANE), F32), jax.ShapeDtypeStruct((T, D), F32),
                   jax.ShapeDtypeStruct((1, D), F32)],
        compiler_params=_cp(("arbitrary",)),
    )(x, g, target)


def _attn_bias(S, bq):
    d = jnp.arange(bq)[:, None] - jnp.arange(S)[None, :] + (S // bq - 1) * bq
    ok = d >= 0
    mult = ((ok & (d <= 128)).astype(F32) + (ok & (d % 4 == 0) & (d <= 512)).astype(F32)
            + (ok & (d % 16 == 0) & (d <= 2048)).astype(F32))
    return jnp.where(mult > 0, jnp.log(jnp.maximum(mult, 1.0)), NEG).astype(F32)


def _attn_fwd(proj, bias, B, S, name):
    T = B * S
    bq = bias.shape[0]
    nb = S // bq
    qcol, kcol, vcol = Q0 // LANE, K0 // LANE, V0 // LANE

    def body(q_ref, k_ref, v_ref, t_ref, o_ref, lse_ref, ks, vs):
        for hh in range(2):
            sl = slice(HEAD_DIM * hh, HEAD_DIM * (hh + 1))
            ks[hh] = k_ref[:, sl].astype(BF16)
            vs[hh] = v_ref[:, sl].astype(BF16)
        for hh in range(2):
            sl = slice(HEAD_DIM * hh, HEAD_DIM * (hh + 1))
            for qb in range(nb):
                w, off, rows = bq * (qb + 1), (nb - 1 - qb) * bq, slice(qb * bq, (qb + 1) * bq)
                q = (q_ref[rows, sl] * 0.125).astype(BF16)
                s = _dot_nt(q, ks[hh, 0:w, :]) + t_ref[:, off:off + w]
                m = jnp.max(s, axis=-1, keepdims=True)
                p = jnp.exp(s - m)
                l = jnp.sum(p, axis=-1, keepdims=True)
                o_ref[rows, sl] = _dot(p.astype(BF16), vs[hh, 0:w, :]) / l
                lse_ref[rows, hh:hh + 1] = m + jnp.log(l)

    blk = lambda c0: pl.BlockSpec((S, LANE), lambda b, p: (b, c0 + p))
    return pl.pallas_call(
        body, name=name, grid=(B, ATT_W // LANE),
        in_specs=[blk(qcol), blk(kcol), blk(vcol), _resident((bq, S))],
        out_specs=[pl.BlockSpec((S, LANE), lambda b, p: (b, p)),
                   pl.BlockSpec((None, None, S, 2), lambda b, p: (b, p, 0, 0))],
        out_shape=[jax.ShapeDtypeStruct((T, ATT_W), F32),
                   jax.ShapeDtypeStruct((B, ATT_W // LANE, S, 2), F32)],
        scratch_shapes=[pltpu.VMEM((2, S, HEAD_DIM), BF16)] * 2,
        compiler_params=_cp(("parallel", "parallel")),
    )(proj, proj, proj, bias)


def _attn_bwd(proj, o, lse, dy, bias, B, S, name):
    T = B * S
    bq = bias.shape[0]
    nb = S // bq
    qcol, kcol, vcol = Q0 // LANE, K0 // LANE, V0 // LANE

    def body(q_ref, k_ref, v_ref, o_ref, lse_ref, do_ref, t_ref, dq_ref, dk_ref, dv_ref, ks, vs, dks, dvs):
        for hh in range(2):
            sl = slice(HEAD_DIM * hh, HEAD_DIM * (hh + 1))
            ks[hh] = k_ref[:, sl].astype(BF16)
            vs[hh] = v_ref[:, sl].astype(BF16)
        dks[...] = jnp.zeros_like(dks)
        dvs[...] = jnp.zeros_like(dvs)
        for hh in range(2):
            sl = slice(HEAD_DIM * hh, HEAD_DIM * (hh + 1))
            for qb in range(nb):
                w, off, rows = bq * (qb + 1), (nb - 1 - qb) * bq, slice(qb * bq, (qb + 1) * bq)
                q = (q_ref[rows, sl] * 0.125).astype(BF16)
                do = do_ref[rows, sl]
                dob = do.astype(BF16)
                delta = jnp.sum(do * o_ref[rows, sl], axis=-1, keepdims=True)
                k, v = ks[hh, 0:w, :], vs[hh, 0:w, :]
                s = _dot_nt(q, k) + t_ref[:, off:off + w]
                p = jnp.exp(s - lse_ref[rows, hh:hh + 1])
                ds = (p * (_dot_nt(dob, v) - delta)).astype(BF16)
                dq_ref[rows, sl] = (_dot(ds, k) * 0.125).astype(dq_ref.dtype)
                dks[hh, 0:w, :] += _dot_tn(ds, q)
                dvs[hh, 0:w, :] += _dot_tn(p.astype(BF16), dob)
            dk_ref[:, sl] = dks[hh].astype(dk_ref.dtype)
            dv_ref[:, sl] = dvs[hh].astype(dv_ref.dtype)

    blk = lambda c0: pl.BlockSpec((S, LANE), lambda b, p: (b, c0 + p))
    own = pl.BlockSpec((S, LANE), lambda b, p: (b, p))
    return pl.pallas_call(
        body, name=name, grid=(B, ATT_W // LANE),
        in_specs=[blk(qcol), blk(kcol), blk(vcol), own,
                  pl.BlockSpec((None, None, S, 2), lambda b, p: (b, p, 0, 0)), own, _resident((bq, S))],
        out_specs=[own, own, own],
        out_shape=[jax.ShapeDtypeStruct((T, ATT_W), BF16)] * 3,
        scratch_shapes=[pltpu.VMEM((2, S, HEAD_DIM), BF16)] * 2 + [pltpu.VMEM((2, S, HEAD_DIM), F32)] * 2,
        compiler_params=_cp(("parallel", "parallel")),
    )(proj, proj, proj, o, lse, dy, bias)


def _conv_fwd(proj, cw, cb, B, S, name):
    T = B * S
    nc = SSD_CONV_DIM // LANE
    c0 = XBC0 // LANE

    def body(x_ref, w_ref, b_ref, o_ref):
        x = x_ref[...]
        t = lax.broadcasted_iota(jnp.int32, (S, 1), 0)
        acc = b_ref[...] + w_ref[SSD_CONV - 1:SSD_CONV, :] * x
        for k in range(SSD_CONV - 1):
            sh = SSD_CONV - 1 - k
            xs = jnp.where(t >= sh, pltpu.roll(x, sh, 0), 0.0)
            acc = acc + w_ref[k:k + 1, :] * xs
        o_ref[...] = acc

    return pl.pallas_call(
        body, name=name, grid=(B, nc),
        in_specs=[pl.BlockSpec((S, LANE), lambda b, j: (b, c0 + j)),
                  pl.BlockSpec((SUBLANE, LANE), lambda b, j: (0, j)),
                  pl.BlockSpec((1, LANE), lambda b, j: (0, j))],
        out_specs=pl.BlockSpec((S, LANE), lambda b, j: (b, j)),
        out_shape=jax.ShapeDtypeStruct((T, SSD_CONV_DIM), F32),
        compiler_params=_cp(("parallel", "parallel")),
    )(proj, cw, cb)


def _conv_bwd(dpre, proj, cw, B, S, name):
    T = B * S
    nc = SSD_CONV_DIM // LANE
    c0 = XBC0 // LANE

    def body(d_ref, x_ref, w_ref, dx_ref, dwb_ref):
        @pl.when(pl.program_id(1) == 0)
        def _():
            dwb_ref[...] = jnp.zeros_like(dwb_ref)

        d = d_ref[...]
        x = x_ref[...]
        t = lax.broadcasted_iota(jnp.int32, (S, 1), 0)
        dx = w_ref[SSD_CONV - 1:SSD_CONV, :] * d
        rows = [None] * SUBLANE
        rows[SSD_CONV - 1] = jnp.sum(d * x, axis=0, keepdims=True)
        for k in range(SSD_CONV - 1):
            sh = SSD_CONV - 1 - k
            dx = dx + w_ref[k:k + 1, :] * jnp.where(t < S - sh, pltpu.roll(d, S - sh, 0), 0.0)
            xs = jnp.where(t >= sh, pltpu.roll(x, sh, 0), 0.0)
            rows[k] = jnp.sum(d * xs, axis=0, keepdims=True)
        rows[SSD_CONV] = jnp.sum(d, axis=0, keepdims=True)
        dx_ref[...] = dx.astype(BF16)
        r = lax.broadcasted_iota(jnp.int32, (SUBLANE, LANE), 0)
        upd = jnp.zeros((SUBLANE, LANE), F32)
        for k in range(SSD_CONV + 1):
            upd = upd + jnp.where(r == k, rows[k], 0.0)
        dwb_ref[...] += upd

    return pl.pallas_call(
        body, name=name, grid=(nc, B),
        in_specs=[pl.BlockSpec((S, LANE), lambda j, b: (b, j)),
                  pl.BlockSpec((S, LANE), lambda j, b: (b, c0 + j)),
                  pl.BlockSpec((SUBLANE, LANE), lambda j, b: (0, j))],
        out_specs=[pl.BlockSpec((S, LANE), lambda j, b: (b, j)),
                   pl.BlockSpec((SUBLANE, LANE), lambda j, b: (0, j))],
        out_shape=[jax.ShapeDtypeStruct((T, SSD_CONV_DIM), BF16),
                   jax.ShapeDtypeStruct((SUBLANE, SSD_CONV_DIM), F32)],
        compiler_params=_cp(("parallel", "arbitrary")),
    )(dpre, proj, cw)


def _ssd_consts():
    e = np.zeros((LANE, SSD_W), np.float32)
    p = np.zeros((SUBLANE, SSD_W), np.float32)
    for h in range(SSD_HEADS):
        e[h, HEAD_DIM * h:HEAD_DIM * (h + 1)] = 1.0
        p[h, HEAD_DIM * h] = 1.0
    return jnp.asarray(e), jnp.asarray(p)


def _ssd_chunk(pre, z, dtr, sprev, par, e_mat, psel):
    L = CHUNK
    xc = _silu(pre)
    xs, bm, cm = xc[:, :SSD_W], xc[:, SSD_W:SSD_W + 2 * SSD_STATE], xc[:, SSD_W + 2 * SSD_STATE:]
    dtb, alog, dskip, ng = par[0:1], par[1:2], par[2:3], par[3:4]
    dt = _softplus(_dot(dtr, e_mat, HI) + dtb)
    a = dt * (-jnp.exp(alog))
    X = xs * dt
    ri = lax.broadcasted_iota(jnp.int32, (L, L), 0)
    ci = lax.broadcasted_iota(jnp.int32, (L, L), 1)
    tril = ri >= ci
    acs = _dot(tril.astype(F32), a, HI)
    acs_t = _dot_nt(psel, acs, HI)
    ecs = jnp.exp(acs)
    alast = acs[L - 1:L, :]
    xd = (X * jnp.exp(alast - acs)).astype(BF16)
    xb = X.astype(BF16)
    col = lax.broadcasted_iota(jnp.int32, (1, SSD_W), 1)
    sb = sprev.astype(BF16)
    bgs = [bm[:, SSD_STATE * g:SSD_STATE * (g + 1)].astype(BF16) for g in range(2)]
    cgs = [cm[:, SSD_STATE * g:SSD_STATE * (g + 1)].astype(BF16) for g in range(2)]
    cbs = [_dot_nt(cgs[g], bgs[g]) for g in range(2)]
    first = lax.broadcasted_iota(jnp.int32, (1, LANE), 1) < HEAD_DIM
    y_tiles, s_tiles = [], []
    for t in range(SSD_W // LANE):
        cl = slice(LANE * t, LANE * (t + 1))
        xb_t, xd_t, sb_t = xb[:, cl], xd[:, cl], sb[:, cl]
        per_head = []
        for h in (2 * t, 2 * t + 1):
            seg = acs[:, HEAD_DIM * h:HEAD_DIM * h + 1] - acs_t[h:h + 1, :]
            dec = jnp.exp(jnp.where(tril, seg, NEG))
            per_head.append(_dot((cbs[h // 3] * dec).astype(BF16), xb_t))
        y_t = jnp.where(first, per_head[0], per_head[1])
        ga, gb = (2 * t) // 3, (2 * t + 1) // 3
        if ga == gb:
            y_off, s_add = _dot(cgs[ga], sb_t), _dot_tn(bgs[ga], xd_t)
        else:
            y_off = jnp.where(first, _dot(cgs[ga], sb_t), _dot(cgs[gb], sb_t))
            s_add = jnp.where(first, _dot_tn(bgs[ga], xd_t), _dot_tn(bgs[gb], xd_t))
        y_tiles.append(y_t + y_off * ecs[:, cl])
        s_tiles.append(s_add)
    y = dskip * xs + jnp.concatenate(y_tiles, axis=1)
    snew = sprev * jnp.exp(alast) + jnp.concatenate(s_tiles, axis=1)
    yg = y * _silu(z)
    sq = yg * yg
    g0 = col < SSD_W // 2
    ms0 = jnp.sum(jnp.where(g0, sq, 0.0), axis=-1, keepdims=True) * (2.0 / SSD_W)
    ms1 = jnp.sum(jnp.where(g0, 0.0, sq), axis=-1, keepdims=True) * (2.0 / SSD_W)
    r = jnp.where(g0, lax.rsqrt(ms0 + RMS_EPS), lax.rsqrt(ms1 + RMS_EPS))
    return yg * r * ng, snew


SSD_CHUNKS_PER_STEP = 2


def _ssd_chunks_per_step(S):
    k = SSD_CHUNKS_PER_STEP
    while (S // CHUNK) % k:
        k //= 2
    return k


def _ssd_fwd(pre, proj, par, B, S, name):
    T = B * S
    k = _ssd_chunks_per_step(S)
    nc, rows = S // (CHUNK * k), CHUNK * k
    e_mat, psel = _ssd_consts()

    def body(pre_ref, z_ref, dt_ref, par_ref, e_ref, p_ref, y_ref, sall_ref, st):
        @pl.when(pl.program_id(1) == 0)
        def _():
            st[...] = jnp.zeros_like(st)

        sprev = st[...]
        for i in range(k):
            r = slice(CHUNK * i, CHUNK * (i + 1))
            sall_ref[i] = sprev
            y, sprev = _ssd_chunk(pre_ref[r, :], z_ref[r, :], dt_ref[r, :], sprev, par_ref[...], e_ref[...],
                                  p_ref[...])
            y_ref[r, :] = y.astype(BF16)
        st[...] = sprev

    row = lambda b, c: b * nc + c
    full = lambda shp: pl.BlockSpec(shp, lambda b, c: (0, 0))
    return pl.pallas_call(
        body, name=name, grid=(B, nc),
        in_specs=[pl.BlockSpec((rows, SSD_CONV_DIM), lambda b, c: (row(b, c), 0)),
                  pl.BlockSpec((rows, SSD_W), lambda b, c: (row(b, c), Z0 // SSD_W)),
                  pl.BlockSpec((rows, LANE), lambda b, c: (row(b, c), DT0 // LANE)),
                  full((SUBLANE, SSD_W)), full((LANE, SSD_W)), full((SUBLANE, SSD_W))],
        out_specs=[pl.BlockSpec((rows, SSD_W), lambda b, c: (row(b, c), 0)),
                   pl.BlockSpec((k, SSD_STATE, SSD_W), lambda b, c: (row(b, c), 0, 0))],
        out_shape=[jax.ShapeDtypeStruct((T, SSD_W), BF16),
                   jax.ShapeDtypeStruct((B * nc * k, SSD_STATE, SSD_W), F32)],
        scratch_shapes=[pltpu.VMEM((SSD_STATE, SSD_W), F32)],
        compiler_params=_cp(("parallel", "arbitrary")),
    )(pre, proj, proj, par, e_mat, psel)


def _ssd_bwd(pre, proj, sall, dy, par, B, S, name):
    T = B * S
    k = _ssd_chunks_per_step(S)
    nc, rows = S // (CHUNK * k), CHUNK * k
    e_mat, psel = _ssd_consts()

    def body(pre_ref, z_ref, dt_ref, sall_ref, dy_ref, par_ref, e_ref, p_ref,
             dpre_ref, dz_ref, ddt_ref, dpar_ref, ds):
        b, c = pl.program_id(0), pl.program_id(1)

        @pl.when(c == 0)
        def _():
            ds[...] = jnp.zeros_like(ds)

        @pl.when((b == 0) & (c == 0))
        def _():
            dpar_ref[...] = jnp.zeros_like(dpar_ref)

        e_v, p_v = e_ref[...], p_ref[...]
        fn = lambda pre, z, dtr, sprev, par: _ssd_chunk(pre, z, dtr, sprev, par, e_v, p_v)
        dstate, dpar_sum = ds[...], None
        for i in reversed(range(k)):
            r = slice(CHUNK * i, CHUNK * (i + 1))
            _, vjp = jax.vjp(fn, pre_ref[r, :], z_ref[r, :], dt_ref[r, :], sall_ref[i], par_ref[...])
            dpre, dz, ddt, dstate, dpar = vjp((dy_ref[r, :], dstate))
            dpre_ref[r, :] = dpre
            dz_ref[r, :] = dz.astype(BF16)
            ddt_ref[r, :] = ddt.astype(BF16)
            dpar_sum = dpar if dpar_sum is None else dpar_sum + dpar
        dpar_ref[...] += dpar_sum
        ds[...] = dstate

    row = lambda b, c: b * nc + (nc - 1 - c)
    full = lambda shp: pl.BlockSpec(shp, lambda b, c: (0, 0))
    return pl.pallas_call(
        body, name=name, grid=(B, nc),
        in_specs=[pl.BlockSpec((rows, SSD_CONV_DIM), lambda b, c: (row(b, c), 0)),
                  pl.BlockSpec((rows, SSD_W), lambda b, c: (row(b, c), Z0 // SSD_W)),
                  pl.BlockSpec((rows, LANE), lambda b, c: (row(b, c), DT0 // LANE)),
                  pl.BlockSpec((k, SSD_STATE, SSD_W), lambda b, c: (row(b, c), 0, 0)),
                  pl.BlockSpec((rows, SSD_W), lambda b, c: (row(b, c), ATT_W // SSD_W)),
                  full((SUBLANE, SSD_W)), full((LANE, SSD_W)), full((SUBLANE, SSD_W))],
        out_specs=[pl.BlockSpec((rows, SSD_CONV_DIM), lambda b, c: (row(b, c), 0)),
                   pl.BlockSpec((rows, SSD_W), lambda b, c: (row(b, c), 0)),
                   pl.BlockSpec((rows, LANE), lambda b, c: (row(b, c), 0)),
                   full((SUBLANE, SSD_W))],
        out_shape=[jax.ShapeDtypeStruct((T, SSD_CONV_DIM), F32),
                   jax.ShapeDtypeStruct((T, SSD_W), BF16),
                   jax.ShapeDtypeStruct((T, LANE), BF16),
                   jax.ShapeDtypeStruct((SUBLANE, SSD_W), F32)],
        scratch_shapes=[pltpu.VMEM((SSD_STATE, SSD_W), F32)],
        compiler_params=_cp(("arbitrary", "arbitrary")),
    )(pre, proj, proj, sall, dy, par, e_mat, psel)


def _sgu_consts():
    e = np.zeros((SUBLANE, SGU_W), np.float32)
    for g in range(SGU_GROUPS):
        e[g, HEAD_DIM * g:HEAD_DIM * (g + 1)] = 1.0
    return jnp.asarray(e)


def _sgu_chunk(u_raw, v_raw, ln, w, bst, e4):
    L = CHUNK
    u = _gelu(u_raw)
    v = _gelu(v_raw)
    mu = jnp.mean(v, axis=-1, keepdims=True)
    vc = v - mu
    var = jnp.mean(vc * vc, axis=-1, keepdims=True)
    vn = vc * lax.rsqrt(var + LN_EPS) * ln[0:1] + ln[1:2]
    vb = vn.astype(BF16)
    ri = lax.broadcasted_iota(jnp.int32, (L, L), 0)
    ci = lax.broadcasted_iota(jnp.int32, (L, L), 1)
    tril = ri >= ci
    col = lax.broadcasted_iota(jnp.int32, (1, SGU_W), 1)
    mixed = _dot(bst, e4, HI)
    for g in range(SGU_GROUPS):
        wc = jnp.where(tril, w[g], 0.0).astype(BF16)
        gm = (col >= HEAD_DIM * g) & (col < HEAD_DIM * (g + 1))
        mixed = mixed + jnp.where(gm, _dot(wc, vb), 0.0)
    return u * mixed


def _sgu_fwd(proj, ln, w, bst, B, S, name):
    T = B * S
    nc = S // CHUNK
    e4 = _sgu_consts()

    def body(u_ref, v_ref, ln_ref, w_ref, b_ref, e_ref, y_ref):
        y_ref[...] = _sgu_chunk(u_ref[...], v_ref[...], ln_ref[...], w_ref[...], b_ref[...], e_ref[...]).astype(BF16)

    return pl.pallas_call(
        body, name=name, grid=(T // CHUNK,),
        in_specs=[pl.BlockSpec((CHUNK, SGU_W), lambda i: (i, U0 // SGU_W)),
                  pl.BlockSpec((CHUNK, SGU_W), lambda i: (i, VS0 // SGU_W)),
                  pl.BlockSpec((SUBLANE, SGU_W), lambda i: (0, 0)),
                  pl.BlockSpec((SGU_GROUPS, CHUNK, CHUNK), lambda i: (0, 0, 0)),
                  pl.BlockSpec((CHUNK, SUBLANE), lambda i: (0, 0)),
                  pl.BlockSpec((SUBLANE, SGU_W), lambda i: (0, 0))],
        out_specs=pl.BlockSpec((CHUNK, SGU_W), lambda i: (i, 0)),
        out_shape=jax.ShapeDtypeStruct((T, SGU_W), BF16),
        compiler_params=_cp(("parallel",)),
    )(proj, proj, ln, w, bst, e4)


def _sgu_bwd(proj, dy, ln, w, bst, B, S, name):
    T = B * S
    e4 = _sgu_consts()
    ycol = (ATT_W + SSD_W) // SGU_W

    def body(u_ref, v_ref, dy_ref, ln_ref, w_ref, b_ref, e_ref, du_ref, dv_ref, dln_ref, dw_ref, db_ref):
        @pl.when(pl.program_id(0) == 0)
        def _():
            dln_ref[...] = jnp.zeros_like(dln_ref)
            dw_ref[...] = jnp.zeros_like(dw_ref)
            db_ref[...] = jnp.zeros_like(db_ref)

        e_v = e_ref[...]
        fn = lambda u, v, ln, w, b: _sgu_chunk(u, v, ln, w, b, e_v)
        _, vjp = jax.vjp(fn, u_ref[...], v_ref[...], ln_ref[...], w_ref[...], b_ref[...])
        du, dv, dln, dw, db = vjp(dy_ref[...])
        du_ref[...] = du.astype(BF16)
        dv_ref[...] = dv.astype(BF16)
        dln_ref[...] += dln
        dw_ref[...] += dw
        db_ref[...] += db

    c_ln = pl.BlockSpec((SUBLANE, SGU_W), lambda i: (0, 0))
    c_w = pl.BlockSpec((SGU_GROUPS, CHUNK, CHUNK), lambda i: (0, 0, 0))
    c_b = pl.BlockSpec((CHUNK, SUBLANE), lambda i: (0, 0))
    return pl.pallas_call(
        body, name=name, grid=(T // CHUNK,),
        in_specs=[pl.BlockSpec((CHUNK, SGU_W), lambda i: (i, U0 // SGU_W)),
                  pl.BlockSpec((CHUNK, SGU_W), lambda i: (i, VS0 // SGU_W)),
                  pl.BlockSpec((CHUNK, SGU_W), lambda i: (i, ycol)),
                  c_ln, c_w, c_b, pl.BlockSpec((SUBLANE, SGU_W), lambda i: (0, 0))],
        out_specs=[pl.BlockSpec((CHUNK, SGU_W), lambda i: (i, 0)),
                   pl.BlockSpec((CHUNK, SGU_W), lambda i: (i, 0)), c_ln, c_w, c_b],
        out_shape=[jax.ShapeDtypeStruct((T, SGU_W), BF16), jax.ShapeDtypeStruct((T, SGU_W), BF16),
                   jax.ShapeDtypeStruct((SUBLANE, SGU_W), F32),
                   jax.ShapeDtypeStruct((SGU_GROUPS, CHUNK, CHUNK), F32),
                   jax.ShapeDtypeStruct((CHUNK, SUBLANE), F32)],
        compiler_params=_cp(("arbitrary",)),
    )(proj, proj, dy, ln, w, bst, e4)


_HBM = pl.BlockSpec(memory_space=pltpu.HBM)
_SEM = pl.BlockSpec(memory_space=pltpu.SEMAPHORE)
_ANY = pl.BlockSpec(memory_space=pl.ANY)
_EFFECT = pltpu.SideEffectType.DATAFLOW_SIDE_EFFECTING


def _peers():
    x, y, c = lax.axis_index("x"), lax.axis_index("y"), lax.axis_index("c")
    out = []
    for p in range(1, N_DEV):
        px, py, pc = x ^ ((p >> 2) & 1), y ^ ((p >> 1) & 1), c ^ (p & 1)
        out.append(((px, py, pc), 4 * px + 2 * py + pc))
    return 4 * x + 2 * y + c, out


def _xchg_start(xs, a2a, order, name):
    n = len(xs)
    lands = [lax.empty(a.shape if f else (N_DEV,) + a.shape, a.dtype) for a, f in zip(xs, a2a)]

    def body(*refs):
        ins, zones = refs[:n], refs[n:2 * n]
        send_sems, recv_sems = refs[2 * n + 1], refs[2 * n + 2]
        token = refs[-1]
        me, peers = _peers()
        for p, (dev, peer) in enumerate(peers):
            for t in range(n):
                pltpu.make_async_remote_copy(
                    src_ref=ins[t].at[peer] if a2a[t] else ins[t], dst_ref=zones[t].at[me],
                    send_sem=send_sems.at[p * n + t], recv_sem=recv_sems.at[p * n + t],
                    device_id=dev, device_id_type=MESH).start()
        token[...] = jnp.zeros_like(token)

    hbm = lambda a: pltpu.HBM(a.shape, a.dtype)
    sems = pltpu.SemaphoreType.DMA(((N_DEV - 1) * n,))
    out = pl.pallas_call(
        body, name=name,
        in_specs=[_HBM] * (2 * n) + [_ANY],
        out_specs=[_SEM, _SEM] + [_HBM] * (2 * n) + [pl.BlockSpec(memory_space=pltpu.VMEM)],
        out_shape=[sems, sems] + [hbm(a) for a in xs] + [hbm(a) for a in lands]
        + [jax.ShapeDtypeStruct((SUBLANE, LANE), F32)],
        input_output_aliases={t: 2 + t for t in range(2 * n)},
        compiler_params=pltpu.CompilerParams(has_side_effects=_EFFECT),
    )(*[pltpu.with_memory_space_constraint(a, pltpu.HBM) for a in list(xs) + list(lands)], order)
    return out[0], out[1], out[2:2 + n], out[2 + n:2 + 2 * n], out[-1]


def _xchg_wait(started, a2a, after, name):
    send_sems, recv_sems, xs, lands, _ = started
    n = len(xs)

    def body(*refs):
        ins, zones = refs[:n], refs[n:2 * n]
        send_s, recv_s = refs[2 * n], refs[2 * n + 1]
        me, peers = _peers()
        cps = []
        for p, (dev, peer) in enumerate(peers):
            for t in range(n):
                cps.append(pltpu.make_async_remote_copy(
                    src_ref=ins[t].at[peer] if a2a[t] else ins[t], dst_ref=zones[t].at[peer],
                    send_sem=send_s.at[p * n + t], recv_sem=recv_s.at[p * n + t],
                    device_id=dev, device_id_type=MESH))
        for cp in cps:
            cp.wait_recv()
        for cp in cps:
            cp.wait_send()

    hbm = lambda a: pltpu.HBM(a.shape, a.dtype)
    out = pl.pallas_call(
        body, name=name,
        in_specs=[_HBM] * (2 * n) + [_SEM, _SEM, _ANY],
        out_specs=[_HBM] * (2 * n),
        out_shape=[hbm(a) for a in xs] + [hbm(a) for a in lands],
        input_output_aliases={t: t for t in range(2 * n)},
        compiler_params=pltpu.CompilerParams(has_side_effects=_EFFECT),
    )(*xs, *lands, send_sems, recv_sems, after)
    return out[:n], out[n:]


def _chip_peers():
    x, y, c = lax.axis_index("x"), lax.axis_index("y"), lax.axis_index("c")
    chips = [(1 - x, y), (x, 1 - y), (1 - x, 1 - y)]
    slot = lambda px, py, pc: 4 * px + 2 * py + pc
    return (x, y, c), chips, slot


def _gather_start(xs, order, name):
    n = len(xs)
    lands = [lax.empty((N_DEV,) + a.shape, a.dtype) for a in xs]

    def body(*refs):
        ins, zones = refs[:n], refs[n:2 * n]
        send_sems, d2d_sems, ici_sems = refs[2 * n + 1:2 * n + 4]
        token = refs[-1]
        (x, y, c), chips, slot = _chip_peers()
        me = slot(x, y, c)
        for t in range(n):
            for j, (px, py) in enumerate(chips):
                pltpu.make_async_remote_copy(
                    src_ref=ins[t], dst_ref=zones[t].at[me], send_sem=send_sems.at[(1 + j) * n + t],
                    recv_sem=ici_sems.at[j * n + t], device_id=(px, py, c), device_id_type=MESH).start()
            pltpu.make_async_remote_copy(
                src_ref=ins[t], dst_ref=zones[t].at[me], send_sem=send_sems.at[t],
                recv_sem=d2d_sems.at[t], device_id=(x, y, 1 - c), device_id_type=MESH).start()
        token[...] = jnp.zeros_like(token)

    hbm = lambda a: pltpu.HBM(a.shape, a.dtype)
    dma = lambda k: pltpu.SemaphoreType.DMA((k,))
    out = pl.pallas_call(
        body, name=name,
        in_specs=[_HBM] * (2 * n) + [_ANY],
        out_specs=[_SEM, _SEM, _SEM] + [_HBM] * (2 * n) + [pl.BlockSpec(memory_space=pltpu.VMEM)],
        out_shape=[dma(4 * n), dma(n), dma(3 * n)] + [hbm(a) for a in xs] + [hbm(a) for a in lands]
        + [jax.ShapeDtypeStruct((SUBLANE, LANE), F32)],
        input_output_aliases={t: 3 + t for t in range(2 * n)},
        compiler_params=pltpu.CompilerParams(has_side_effects=_EFFECT),
    )(*[pltpu.with_memory_space_constraint(a, pltpu.HBM) for a in list(xs) + list(lands)], order)
    return dict(send=out[0], d2d=out[1], ici=out[2], xs=out[3:3 + n], lands=out[3 + n:3 + 2 * n], token=out[-1])


def _gather_relay(st, after, name):
    n = len(st["xs"])

    def body(*refs):
        zones, ici_sems = refs[:n], refs[n]
        fsend, frecv = refs[n + 2], refs[n + 3]
        token = refs[-1]
        (x, y, c), chips, slot = _chip_peers()
        for t in range(n):
            for j, (px, py) in enumerate(chips):
                blk = zones[t].at[slot(px, py, c)]
                fwd = pltpu.make_async_remote_copy(
                    src_ref=blk, dst_ref=blk, send_sem=fsend.at[j * n + t], recv_sem=ici_sems.at[j * n + t],
                    device_id=(x, y, 1 - c), device_id_type=MESH)
                fwd.wait_recv()
                pltpu.make_async_remote_copy(
                    src_ref=blk, dst_ref=blk, send_sem=fsend.at[j * n + t], recv_sem=frecv.at[j * n + t],
                    device_id=(x, y, 1 - c), device_id_type=MESH).start()
        token[...] = jnp.zeros_like(token)

    hbm = lambda a: pltpu.HBM(a.shape, a.dtype)
    dma = lambda k: pltpu.SemaphoreType.DMA((k,))
    out = pl.pallas_call(
        body, name=name,
        in_specs=[_HBM] * n + [_SEM, _ANY],
        out_specs=[_SEM, _SEM] + [_HBM] * n + [pl.BlockSpec(memory_space=pltpu.VMEM)],
        out_shape=[dma(3 * n), dma(3 * n)] + [hbm(a) for a in st["lands"]]
        + [jax.ShapeDtypeStruct((SUBLANE, LANE), F32)],
        input_output_aliases={t: 2 + t for t in range(n)},
        compiler_params=pltpu.CompilerParams(has_side_effects=_EFFECT),
    )(*st["lands"], st["ici"], after)
    return dict(st, fsend=out[0], frecv=out[1], lands=out[2:2 + n], token=out[-1])


def _gather_wait(st, after, name):
    n = len(st["xs"])

    def body(*refs):
        ins, zones = refs[:n], refs[n:2 * n]
        send_sems, d2d_sems, fsend, frecv = refs[2 * n:2 * n + 4]
        (x, y, c), chips, slot = _chip_peers()
        sib = (x, y, 1 - c)
        for t in range(n):
            mine = lambda s, r, dst: pltpu.make_async_remote_copy(
                src_ref=ins[t], dst_ref=dst, send_sem=s, recv_sem=r, device_id=sib, device_id_type=MESH)
            direct = mine(send_sems.at[t], d2d_sems.at[t], zones[t].at[slot(x, y, 1 - c)])
            direct.wait_recv()
            direct.wait_send()
            for j, (px, py) in enumerate(chips):
                mine(send_sems.at[(1 + j) * n + t], d2d_sems.at[t], zones[t].at[slot(px, py, c)]).wait_send()
                relayed = mine(fsend.at[j * n + t], frecv.at[j * n + t], zones[t].at[slot(px, py, 1 - c)])
                relayed.wait_recv()
                relayed.wait_send()

    hbm = lambda a: pltpu.HBM(a.shape, a.dtype)
    out = pl.pallas_call(
        body, name=name,
        in_specs=[_HBM] * (2 * n) + [_SEM] * 4 + [_ANY],
        out_specs=[_HBM] * (2 * n),
        out_shape=[hbm(a) for a in st["xs"]] + [hbm(a) for a in st["lands"]],
        input_output_aliases={t: t for t in range(2 * n)},
        compiler_params=pltpu.CompilerParams(has_side_effects=_EFFECT),
    )(*st["xs"], *st["lands"], st["send"], st["d2d"], st["fsend"], st["frecv"], after)
    return out[:n], out[n:]


def _cast_layers(pairs, name):
    def body(*refs):
        n = len(refs) // 2
        for i in range(n):
            refs[n + i][...] = refs[i][...].astype(BF16)

    in_specs = [pl.BlockSpec((None,) + w.shape[1:], functools.partial(lambda l, i: (l, 0, 0), l),
                             pipeline_mode=pl.Buffered(1)) for w, l in pairs]
    return pl.pallas_call(
        body, name=name, grid=(1,), in_specs=in_specs,
        out_specs=[pl.BlockSpec(w.shape[1:], lambda i: (0, 0)) for w, _ in pairs],
        out_shape=[jax.ShapeDtypeStruct(w.shape[1:], BF16) for w, _ in pairs],
        compiler_params=_cp(("arbitrary",)),
    )(*[w for w, _ in pairs])


def _adamw(me, w, m, v, parts, own, name, layer=0, into=None):
    L, R, C = w.shape
    P = parts.shape[0]
    tr = R
    t = 16
    while t <= R:
        if R % t == 0 and t * C <= 131072:
            tr = t
        t += 16
    if tr == R and R * C > 131072 and R % 16 == 0:
        tr = 16
    own_all = own.shape[0] == P

    def body(me_ref, w_ref, m_ref, v_ref, p_ref, own_ref, *rest):
        g_ref, d_ref, mo_ref, vo_ref = rest[-4:]
        mine = own_ref[...].astype(F32)
        g = None
        for p in range(P):
            term = jnp.where(me_ref[0] == p, mine, p_ref[p].astype(F32))
            g = term if g is None else g + term
        mn = ADAM_B1 * m_ref[...] + (1.0 - ADAM_B1) * g
        vn = ADAM_B2 * v_ref[...] + (1.0 - ADAM_B2) * (g * g)
        m_hat = mn / (1.0 - ADAM_B1 ** ADAM_STEP)
        v_hat = vn / (1.0 - ADAM_B2 ** ADAM_STEP)
        g_ref[...] = g
        d_ref[...] = -ADAM_LR * (m_hat / (jnp.sqrt(v_hat) + ADAM_EPS) + ADAM_WD * w_ref[...])
        mo_ref[...] = mn
        vo_ref[...] = vn

    blk = pl.BlockSpec((None, tr, C), lambda i, me_ref: (layer, i, 0))
    own_blk = pl.BlockSpec((None, tr, C), lambda i, me_ref: (me_ref[0] if own_all else 0, i, 0))
    prev = list(into) if into is not None else []
    return pl.pallas_call(
        body, name=name,
        grid_spec=pltpu.PrefetchScalarGridSpec(
            num_scalar_prefetch=1, grid=(R // tr,),
            in_specs=[blk, blk, blk, pl.BlockSpec((P, tr, C), lambda i, me_ref: (0, i, 0)), own_blk]
            + [_ANY] * len(prev),
            out_specs=[blk] * 4),
        out_shape=[jax.ShapeDtypeStruct((L, R, C), F32)] * 4,
        input_output_aliases={6 + i: i for i in range(len(prev))},
        compiler_params=_cp(("parallel",)),
    )(me, w, m, v, parts, own, *prev)


def _perm_cols(w):
    pad = jnp.zeros((w.shape[0], LANE - SSD_HEADS), w.dtype)
    return jnp.concatenate([w[:, 0:1536], w[:, 2438:2694], w[:, 1536:2432], w[:, 2432:2438], pad,
                            w[:, 2694:2950]], axis=1)


def _unperm_cols(w):
    return jnp.concatenate([w[:, 0:1536], w[:, XBC0:XBC0 + SSD_CONV_DIM], w[:, DT0:DT0 + SSD_HEADS],
                            w[:, U0:U0 + SGU_W], w[:, VS0:VS0 + SGU_W]], axis=1)


_SMALL = ("ffn1_norm", "mix_norm", "conv_w", "conv_b", "dt_bias", "a_log", "d_skip", "ssd_norm",
          "sgu_ln_g", "sgu_ln_b", "sgu_w", "sgu_b", "ffn2_norm", "final_norm", "loss")


def _pack(d):
    v = jnp.concatenate([d[k].astype(F32).reshape(-1) for k in _SMALL])
    n = v.shape[0]
    npad = -(-n // (LANE * 16)) * (LANE * 16)
    return jnp.pad(v, (0, npad - n)).reshape(npad // LANE, LANE)


def _unpack(p, shapes):
    v = p.reshape(-1)
    out, o = {}, 0
    for k in _SMALL:
        n = int(np.prod(shapes[k]))
        out[k] = v[o:o + n].reshape(shapes[k])
        o += n
    return out


def kernel(x, ffn1_norm, ffn1_w_gate, ffn1_w_up, ffn1_w_down, mix_norm, w_in, conv_w, conv_b, dt_bias, a_log, d_skip, ssd_norm, sgu_ln_g, sgu_ln_b, sgu_w, sgu_b, w_out, ffn2_norm, ffn2_w_gate, ffn2_w_up, ffn2_w_down, final_norm, loss_target, m_ffn1_norm, m_ffn1_w_gate, m_ffn1_w_up, m_ffn1_w_down, m_mix_norm, m_w_in, m_conv_w, m_conv_b, m_dt_bias, m_a_log, m_d_skip, m_ssd_norm, m_sgu_ln_g, m_sgu_ln_b, m_sgu_w, m_sgu_b, m_w_out, m_ffn2_norm, m_ffn2_w_gate, m_ffn2_w_up, m_ffn2_w_down, m_final_norm, v_ffn1_norm, v_ffn1_w_gate, v_ffn1_w_up, v_ffn1_w_down, v_mix_norm, v_w_in, v_conv_w, v_conv_b, v_dt_bias, v_a_log, v_d_skip, v_ssd_norm, v_sgu_ln_g, v_sgu_ln_b, v_sgu_w, v_sgu_b, v_w_out, v_ffn2_norm, v_ffn2_w_gate, v_ffn2_w_up, v_ffn2_w_down, v_final_norm):
    B, S, D = x.shape
    T = B * S
    L = ffn1_norm.shape[0]
    me = 4 * lax.axis_index("x") + 2 * lax.axis_index("y") + lax.axis_index("c")
    cs = conv_w.shape[2]
    W = dict(ffn1_norm=ffn1_norm, ffn1_w_gate=ffn1_w_gate, ffn1_w_up=ffn1_w_up, ffn1_w_down=ffn1_w_down,
             mix_norm=mix_norm, w_in=w_in, conv_w=conv_w, conv_b=conv_b, dt_bias=dt_bias, a_log=a_log,
             d_skip=d_skip, ssd_norm=ssd_norm, sgu_ln_g=sgu_ln_g, sgu_ln_b=sgu_ln_b, sgu_w=sgu_w, sgu_b=sgu_b,
             w_out=w_out, ffn2_norm=ffn2_norm, ffn2_w_gate=ffn2_w_gate, ffn2_w_up=ffn2_w_up,
             ffn2_w_down=ffn2_w_down, final_norm=final_norm)
    M = dict(ffn1_norm=m_ffn1_norm, ffn1_w_gate=m_ffn1_w_gate, ffn1_w_up=m_ffn1_w_up, ffn1_w_down=m_ffn1_w_down,
             mix_norm=m_mix_norm, w_in=m_w_in, conv_w=m_conv_w, conv_b=m_conv_b, dt_bias=m_dt_bias, a_log=m_a_log,
             d_skip=m_d_skip, ssd_norm=m_ssd_norm, sgu_ln_g=m_sgu_ln_g, sgu_ln_b=m_sgu_ln_b, sgu_w=m_sgu_w,
             sgu_b=m_sgu_b, w_out=m_w_out, ffn2_norm=m_ffn2_norm, ffn2_w_gate=m_ffn2_w_gate,
             ffn2_w_up=m_ffn2_w_up, ffn2_w_down=m_ffn2_w_down, final_norm=m_final_norm)
    V = dict(ffn1_norm=v_ffn1_norm, ffn1_w_gate=v_ffn1_w_gate, ffn1_w_up=v_ffn1_w_up, ffn1_w_down=v_ffn1_w_down,
             mix_norm=v_mix_norm, w_in=v_w_in, conv_w=v_conv_w, conv_b=v_conv_b, dt_bias=v_dt_bias, a_log=v_a_log,
             d_skip=v_d_skip, ssd_norm=v_ssd_norm, sgu_ln_g=v_sgu_ln_g, sgu_ln_b=v_sgu_ln_b, sgu_w=v_sgu_w,
             sgu_b=v_sgu_b, w_out=v_w_out, ffn2_norm=v_ffn2_norm, ffn2_w_gate=v_ffn2_w_gate,
             ffn2_w_up=v_ffn2_w_up, ffn2_w_down=v_ffn2_w_down, final_norm=v_final_norm)
    FFN1 = ("ffn1_w_gate", "ffn1_w_up", "ffn1_w_down")
    FFN2 = ("ffn2_w_gate", "ffn2_w_up", "ffn2_w_down")
    MIX = ("w_in", "w_out")
    big = FFN1 + MIX + FFN2
    col_sharded = lambda k: k.endswith("w_gate") or k.endswith("w_up")
    for dct in (W, M, V):
        for k in big:
            if col_sharded(k):
                dct[k] = jnp.swapaxes(dct[k], 1, 2)

    wgroups = [[(k, 0) for k in FFN1], [("w_in", 0), ("conv_w", None)], [("w_out", 0)] + [(k, 0) for k in FFN2]]
    wgroups += [[(k, l) for k in big] for l in range(1, L)]
    wstarted, order = [], x
    later = [kl for grp in wgroups[1:] for kl in grp if kl[0] != "conv_w"]
    cast = dict(zip(wgroups[0], _cast_layers([(W[k], l) for k, l in wgroups[0]], "cast_first")))
    for gi, grp in enumerate(wgroups):
        if gi == 1:
            first = lax.optimization_barrier((W[later[0][0]], order))[0]
            srcs = [(first if i == 0 else W[k], l) for i, (k, l) in enumerate(later)]
            cast.update(zip(later, _cast_layers(srcs, "cast_rest")))
        xs = [conv_w if k == "conv_w" else cast[(k, l)] for k, l in grp]
        st = _gather_start(xs, order, f"gather_start_{gi}")
        order = st["token"]
        wstarted.append(st)
    G = {}
    is_me = (jnp.arange(N_DEV) == me)

    def relay(gi, after):
        wstarted[gi] = _gather_relay(wstarted[gi], after, f"gather_relay_{gi}")
        return wstarted[gi]["token"]

    def gathered(gi, after):
        own, lands = _gather_wait(wstarted[gi], after, f"gather_wait_{gi}")
        for key, o, z in zip(wgroups[gi], own, lands):
            G[key] = jnp.where(is_me.reshape((N_DEV,) + (1,) * o.ndim), o[None], z)

    def rows(k, l):
        a = G[(k, l)]
        return a.reshape(-1, a.shape[-1])

    bias = _attn_bias(S, min(256, S))
    row1 = lambda a: a.reshape(1, -1)

    def ffn1_params(l):
        return dict(g1=row1(ffn1_norm[l]), wg1=rows("ffn1_w_gate", l), wu1=rows("ffn1_w_up", l),
                    wd1=rows("ffn1_w_down", l))

    def out_params(l):
        return dict(wout=rows("w_out", l), g2=row1(ffn2_norm[l]), wg2=rows("ffn2_w_gate", l),
                    wu2=rows("ffn2_w_up", l), wd2=rows("ffn2_w_down", l))

    def mix_params(l):
        cw = jnp.transpose(G[("conv_w", None)][:, l], (1, 0, 2)).reshape(SSD_CONV, -1)
        return dict(
            gm=row1(mix_norm[l]), win=_perm_cols(rows("w_in", l)),
            cw=jnp.pad(cw, ((0, SUBLANE - SSD_CONV), (0, 0))), cb=row1(conv_b[l]),
            par=jnp.pad(jnp.stack([jnp.repeat(dt_bias[l], HEAD_DIM), jnp.repeat(a_log[l], HEAD_DIM),
                                   jnp.repeat(d_skip[l], HEAD_DIM), ssd_norm[l]]), ((0, SUBLANE - 4), (0, 0))),
            ln=jnp.pad(jnp.stack([sgu_ln_g[l], sgu_ln_b[l]]), ((0, SUBLANE - 2), (0, 0))),
            sw=sgu_w[l], bst=jnp.pad(sgu_b[l].T, ((0, 0), (0, SUBLANE - SGU_GROUPS))))

    xc = x.reshape(T, D)
    saved, lay = [], []
    tie = lambda a, tok: lax.optimization_barrier((a, tok))[0]
    for l in range(L):
        gathered(0 if l == 0 else l + 2, relay(0, order) if l == 0 else xc)
        p = ffn1_params(l)
        x1, gate1, up1 = _ffn_fwd(xc, p["g1"], p["wg1"], p["wu1"], p["wd1"], f"ffn1_fwd_{l}")
        if l == 0:
            gathered(1, relay(1, x1))
        p.update(mix_params(l))
        lay.append(p)
        proj, ht = _norm_mm(x1, p["gm"], p["win"], f"in_proj_{l}")
        if l == 0:
            proj = tie(proj, relay(2, proj))
        o_att, lse = _attn_fwd(proj, bias, B, S, f"attn_fwd_{l}")
        pre = _conv_fwd(proj, p["cw"], p["cb"], B, S, f"conv_fwd_{l}")
        y_ssd, sall = _ssd_fwd(pre, proj, p["par"], B, S, f"ssd_fwd_{l}")
        y_sgu = _sgu_fwd(proj, p["ln"], p["sw"], p["bst"], B, S, f"sgu_fwd_{l}")
        ycat = jnp.concatenate([o_att.astype(BF16), y_ssd, y_sgu], axis=1)
        if l + 1 < L:
            ycat = tie(ycat, relay(l + 3, ycat))
        if l == 0:
            gathered(2, ycat)
        p.update(out_params(l))
        x2 = _mm(ycat, p["wout"], "nn", f"out_proj_{l}", residual=x1)
        x3, gate2, up2 = _ffn_fwd(x2, p["g2"], p["wg2"], p["wu2"], p["wd2"], f"ffn2_fwd_{l}")
        saved.append(dict(x0=xc, gate1=gate1, up1=up1, x1=x1, ht=ht, proj=proj, o_att=o_att, lse=lse, pre=pre,
                          sall=sall, ycat=ycat, x2=x2, gate2=gate2, up2=up2))
        xc = x3
    loss_part, dx, dgf = _final_loss(xc, row1(final_norm), loss_target.reshape(T, D), "final_loss")

    gl = [dict() for _ in range(L)]
    gstarted, gorder = [], [order]

    def to_blocks(k, a):
        if col_sharded(k) or k.endswith("w_down"):
            a = a.reshape(a.shape[0], N_DEV, -1).transpose(1, 2, 0)
        else:
            a = a.reshape(N_DEV, -1, a.shape[-1])
        return a.astype(BF16)

    def send_grads(keys, l, extra, tag):
        xs = [to_blocks(k, gl[l][k]) for k in keys] + extra
        flags = [True] * len(keys) + [False] * len(extra)
        st = _xchg_start(xs, flags, gorder[0], f"grads_start_{tag}")
        gorder[0] = st[-1]
        gstarted.append((keys, l, st, flags, tag))

    def behind(a):
        return lax.optimization_barrier((a, gorder[0]))[0]

    for l in reversed(range(L)):
        p, s, g = lay[l], saved[l], gl[l]
        dx2, dgate, dup, act, xnt, dacct, g["ffn2_norm"] = _ffn_bwd_dx(
            dx, s["x2"], p["g2"], s["gate2"], s["up2"], p["wg2"], p["wu2"], p["wd2"], f"ffn2_bwd_{l}")
        g["ffn2_w_gate"], g["ffn2_w_up"], g["ffn2_w_down"] = _ffn_dw(xnt, dacct, dgate, dup, act, f"ffn2_dw_{l}")
        if l == 0:
            send_grads(FFN2, 0, [], "l0f")
            dx2 = behind(dx2)
        dycat = _mm(dx2, p["wout"], "nt", f"out_proj_dx_{l}")
        g["w_out"] = _mm(s["ycat"], dx2, "tn", f"out_proj_dw_{l}", out_dtype=BF16, tm_cap=1024, tk_cap=512)
        dq, dk, dv = _attn_bwd(s["proj"], s["o_att"], s["lse"], dycat, bias, B, S, f"attn_bwd_{l}")
        dpre, dz, ddt, dpar = _ssd_bwd(s["pre"], s["proj"], s["sall"], dycat, p["par"], B, S, f"ssd_bwd_{l}")
        dxbc, dwb = _conv_bwd(dpre, s["proj"], p["cw"], B, S, f"conv_bwd_{l}")
        du, dvs, dln, dsw, dbst = _sgu_bwd(s["proj"], dycat, p["ln"], p["sw"], p["bst"], B, S, f"sgu_bwd_{l}")
        dproj = jnp.concatenate([dq, dk, dv, dz, du, dxbc, ddt, dvs], axis=1)
        g["w_in"] = _unperm_cols(_mm_resident_lhs(s["ht"], dproj, f"in_proj_dw_{l}"))
        dx1, g["mix_norm"] = _norm_mm_bwd(dproj, s["x1"], p["gm"], p["win"], dx2, f"in_proj_bwd_{l}")
        if l == 0:
            send_grads(MIX, 0, [], "l0a")
            dx1 = behind(dx1)
        dx, dgate, dup, act, xnt, dacct, g["ffn1_norm"] = _ffn_bwd_dx(
            dx1, s["x0"], p["g1"], s["gate1"], s["up1"], p["wg1"], p["wu1"], p["wd1"], f"ffn1_bwd_{l}")
        g["ffn1_w_gate"], g["ffn1_w_up"], g["ffn1_w_down"] = _ffn_dw(xnt, dacct, dgate, dup, act, f"ffn1_dw_{l}")
        if l > 0:
            send_grads(big, l, [], f"l{l}")
            dx = behind(dx)
        hsum = lambda r: r.reshape(SSD_HEADS, HEAD_DIM).sum(-1)
        g["conv_w"], g["conv_b"] = dwb[:SSD_CONV], dwb[SSD_CONV]
        g["dt_bias"], g["a_log"], g["d_skip"], g["ssd_norm"] = hsum(dpar[0]), hsum(dpar[1]), hsum(dpar[2]), dpar[3]
        g["sgu_ln_g"], g["sgu_ln_b"], g["sgu_w"], g["sgu_b"] = dln[0], dln[1], dsw, dbst[:, :SGU_GROUPS].T
    grad_x = dx.reshape(B, S, D)

    stack = lambda k: jnp.stack([gl[l][k] for l in range(L)])
    zero1 = jnp.zeros((1,), F32)
    W["loss"], M["loss"], V["loss"] = zero1, zero1, zero1
    per_layer = lambda k: k not in ("final_norm", "conv_w", "loss")
    small = {k: stack(k) for k in _SMALL if per_layer(k) or k == "conv_w"}
    small["final_norm"], small["loss"] = dgf.reshape(-1), loss_part[0, :1]
    small = {k: small[k].reshape((L,) + W[k].shape[1:]) if per_layer(k) else small[k] for k in _SMALL}
    full_shapes = {k: (W[k].shape if k != "conv_w" else (L, SSD_CONV, SSD_CONV_DIM)) for k in _SMALL}
    send_grads(FFN1, 0, [_pack(small)], "l0b")

    res, after, small_parts = {}, gorder[0], None
    me1 = me.reshape(1).astype(jnp.int32)
    for keys, l, st, flags, tag in gstarted:
        own, lands = _xchg_wait(st, flags, after, f"grads_wait_{tag}")
        for k, mine, pk in zip(keys, own, lands):
            res[k] = _adamw(me1, W[k], M[k], V[k], pk, mine, f"adamw_{k}_{l}", layer=l, into=res.get(k))
        after = lax.optimization_barrier(tuple(res[k][0] for k in keys))[0]
        if len(lands) > len(keys):
            small_parts = (lands[-1], own[-1][None])
    back = lambda k, a: jnp.swapaxes(a, 1, 2) if col_sharded(k) else a
    grads, deltas, new_m, new_v = [{k: back(k, res[k][i]) for k in big} for i in range(4)]

    def embed(a, k):
        if k != "conv_w":
            return a
        return lax.dynamic_update_slice(jnp.zeros(full_shapes[k], F32), a, (0, 0, me * cs))

    outs = _adamw(me1, _pack({k: embed(W[k], k) for k in _SMALL})[None],
                  _pack({k: embed(M[k], k) for k in _SMALL})[None],
                  _pack({k: embed(V[k], k) for k in _SMALL})[None], small_parts[0], small_parts[1], "adamw_small")
    for d, o in zip((grads, deltas, new_m, new_v), outs):
        u = _unpack(o, full_shapes)
        u["conv_w"] = lax.dynamic_slice(u["conv_w"], (0, 0, me * cs), (L, SSD_CONV, cs))
        d.update(u)

    names = ("ffn1_norm", "ffn1_w_gate", "ffn1_w_up", "ffn1_w_down", "mix_norm", "w_in", "conv_w", "conv_b",
             "dt_bias", "a_log", "d_skip", "ssd_norm", "sgu_ln_g", "sgu_ln_b", "sgu_w", "sgu_b", "w_out",
             "ffn2_norm", "ffn2_w_gate", "ffn2_w_up", "ffn2_w_down", "final_norm")
    loss = grads["loss"][0]
    return (loss, grad_x, *[grads[n] for n in names], *[deltas[n] for n in names],
            *[new_m[n] for n in names], *[new_v[n] for n in names])
```

```python
import functools

import numpy as np
import jax
import jax.numpy as jnp
from jax import lax
from jax.experimental import pallas as pl
from jax.experimental.pallas import tpu as pltpu

F32, BF16 = jnp.float32, jnp.bfloat16
HI = lax.Precision.HIGHEST
MESH = pl.DeviceIdType.MESH
N_DEV = 8
VMEM_LIMIT_BYTES = 56 * 1024 * 1024
LANE, SUBLANE = 128, 8

HEAD_DIM = 64
ATT_W = 384
SSD_W = 384
SSD_HEADS = 6
SSD_STATE = 128
SSD_CONV = 4
CHUNK = 128
SSD_CONV_DIM = 896
SGU_W = 256
SGU_GROUPS = 4
D_IN = 2950
RMS_EPS = 1e-6
LN_EPS = 1e-5
NEG = -1e30

PW = 3072
Q0, K0, V0, Z0, U0, XBC0, DT0, VS0 = 0, 384, 768, 1152, 1536, 1792, 2688, 2816

ADAM_LR, ADAM_B1, ADAM_B2, ADAM_EPS, ADAM_WD, ADAM_STEP = 0.001, 0.9, 0.999, 1e-08, 0.01, 10


def _cp(sem=None):
    return pltpu.CompilerParams(dimension_semantics=sem, vmem_limit_bytes=VMEM_LIMIT_BYTES)


def _tile(n, cap, mult=LANE):
    best = None
    t = mult
    while t <= min(n, cap):
        if n % t == 0:
            best = t
        t += mult
    return best if best is not None else n


def _dot(a, b, prec=None):
    return jnp.dot(a, b, preferred_element_type=F32, precision=prec)


def _dot_nt(a, b, prec=None):
    return lax.dot_general(a, b, (((1,), (1,)), ((), ())), preferred_element_type=F32, precision=prec)


def _dot_tn(a, b, prec=None):
    return lax.dot_general(a, b, (((0,), (0,)), ((), ())), preferred_element_type=F32, precision=prec)


def _sigmoid(x):
    return 1.0 / (1.0 + jnp.exp(-x))


def _silu(x):
    return x * _sigmoid(x)


def _gelu(x):
    return 0.5 * x * (1.0 + lax.erf(x * 0.7071067811865476))


def _softplus(x):
    return jnp.maximum(x, 0.0) + jnp.log(1.0 + jnp.exp(-jnp.abs(x)))


def _rms_fwd(x, g):
    rstd = lax.rsqrt(jnp.mean(x * x, axis=-1, keepdims=True) + RMS_EPS)
    xhat = x * rstd
    return xhat * g, xhat, rstd


def _rms_bwd(dy, xhat, rstd, g):
    dxhat = dy * g
    dx = rstd * (dxhat - xhat * jnp.mean(dxhat * xhat, axis=-1, keepdims=True))
    return dx, dy * xhat


def _resident(shape):
    return pl.BlockSpec(shape, lambda *_: (0,) * len(shape), pipeline_mode=pl.Buffered(1))


def _mm(a, b, mode, name, out_dtype=F32, residual=None, tm_cap=512, tn_cap=1024, tk_cap=1024):
    if mode == "nn":
        (M, K), (_, N) = a.shape, b.shape
    elif mode == "nt":
        (M, K), (N, _) = a.shape, b.shape
    else:
        (K, M), (_, N) = a.shape, b.shape
    tm, tn, tk = _tile(M, tm_cap), _tile(N, tn_cap), _tile(K, tk_cap)
    nk = K // tk
    if mode == "tn":
        a_spec = pl.BlockSpec((tk, tm), lambda i, j, k: (k, i))
    else:
        a_spec = pl.BlockSpec((tm, tk), lambda i, j, k: (i, k))
    if mode == "nt":
        b_spec = pl.BlockSpec((tn, tk), lambda i, j, k: (j, k))
    else:
        b_spec = pl.BlockSpec((tk, tn), lambda i, j, k: (k, j))
    o_spec = pl.BlockSpec((tm, tn), lambda i, j, k: (i, j))
    has_res = residual is not None

    def prod(a_ref, b_ref):
        av = a_ref[...].astype(BF16)
        bv = b_ref[...].astype(BF16)
        if mode == "nn":
            return _dot(av, bv)
        if mode == "nt":
            return _dot_nt(av, bv)
        return _dot_tn(av, bv)

    def body(*refs):
        a_ref, b_ref = refs[:2]
        r_ref = refs[2] if has_res else None
        o_ref = refs[2 + has_res]
        if nk == 1:
            o = prod(a_ref, b_ref)
            if has_res:
                o = r_ref[...] + o
            o_ref[...] = o.astype(out_dtype)
            return
        acc = refs[3 + has_res]
        k = pl.program_id(2)

        @pl.when(k == 0)
        def _():
            acc[...] = jnp.zeros_like(acc)

        acc[...] += prod(a_ref, b_ref)

        @pl.when(k == nk - 1)
        def _():
            o = acc[...]
            if has_res:
                o = r_ref[...] + o
            o_ref[...] = o.astype(out_dtype)

    ins = [a, b] + ([residual] if has_res else [])
    in_specs = [a_spec, b_spec] + ([o_spec] if has_res else [])
    return pl.pallas_call(
        body, name=name, grid=(M // tm, N // tn, nk),
        in_specs=in_specs, out_specs=o_spec,
        out_shape=jax.ShapeDtypeStruct((M, N), out_dtype),
        scratch_shapes=[pltpu.VMEM((tm, tn), F32)] if nk > 1 else [],
        compiler_params=_cp(("parallel", "parallel", "arbitrary")),
    )(*ins)


def _ffn_fwd(x, g, wgt, wut, wd, name):
    T, D = x.shape
    F = wgt.shape[0]
    tm = _tile(T, 512)

    def body(x_ref, g_ref, wg_ref, wu_ref, wd_ref, out_ref, gate_ref, up_ref):
        xv = x_ref[...]
        xn = _rms_fwd(xv, g_ref[...])[0].astype(BF16)
        gate = _dot_nt(xn, wg_ref[...])
        up = _dot_nt(xn, wu_ref[...])
        gate_ref[...] = gate.astype(BF16)
        up_ref[...] = up.astype(BF16)
        act = (_silu(gate) * up).astype(BF16)
        out_ref[...] = xv + 0.5 * _dot(act, wd_ref[...])

    row = lambda w: pl.BlockSpec((tm, w), lambda i: (i, 0))
    return pl.pallas_call(
        body, name=name, grid=(T // tm,),
        in_specs=[row(D), _resident((1, D)), _resident((F, D)), _resident((F, D)), _resident((F, D))],
        out_specs=[row(D), row(F), row(F)],
        out_shape=[jax.ShapeDtypeStruct((T, D), F32),
                   jax.ShapeDtypeStruct((T, F), BF16),
                   jax.ShapeDtypeStruct((T, F), BF16)],
        compiler_params=_cp(("parallel",)),
    )(x, g, wgt, wut, wd)


def _ffn_bwd_dx(dout, x, g, gate, up, wg, wu, wd, name):
    T, D = x.shape
    F = wg.shape[0]
    tm = _tile(T, 256)

    def body(dout_ref, x_ref, g_ref, gate_ref, up_ref, wg_ref, wu_ref, wd_ref,
             dx_ref, dgt_ref, dut_ref, actt_ref, xn_ref, dacc_ref, dg_ref):
        @pl.when(pl.program_id(0) == 0)
        def _():
            dg_ref[...] = jnp.zeros_like(dg_ref)

        gv = g_ref[...]
        dout_v = dout_ref[...]
        xn, xhat, rstd = _rms_fwd(x_ref[...], gv)
        xn_ref[...] = xn.astype(BF16)
        dacc = (0.5 * dout_v).astype(BF16)
        dacc_ref[...] = dacc
        dact = _dot_nt(dacc, wd_ref[...])
        gt = gate_ref[...].astype(F32)
        u = up_ref[...].astype(F32)
        sig = _sigmoid(gt)
        sl = gt * sig
        dgate = (dact * u * (sig * (1.0 + gt * (1.0 - sig)))).astype(BF16)
        dup = (dact * sl).astype(BF16)
        dgt_ref[...] = dgate.T
        dut_ref[...] = dup.T
        actt_ref[...] = (sl * u).astype(BF16).T
        dxn = _dot(dgate, wg_ref[...]) + _dot(dup, wu_ref[...])
        dx, dgrow = _rms_bwd(dxn, xhat, rstd, gv)
        dx_ref[...] = dout_v + dx
        dg_ref[...] += jnp.sum(dgrow, axis=0, keepdims=True)

    row = lambda w: pl.BlockSpec((tm, w), lambda i: (i, 0))
    tr = pl.BlockSpec((F, tm), lambda i: (0, i))
    return pl.pallas_call(
        body, name=name, grid=(T // tm,),
        in_specs=[row(D), row(D), _resident((1, D)), row(F), row(F),
                  _resident((F, D)), _resident((F, D)), _resident((F, D))],
        out_specs=[row(D), tr, tr, tr, row(D), row(D), pl.BlockSpec((1, D), lambda i: (0, 0))],
        out_shape=[jax.ShapeDtypeStruct((T, D), F32)] + [jax.ShapeDtypeStruct((F, T), BF16)] * 3
        + [jax.ShapeDtypeStruct((T, D), BF16)] * 2 + [jax.ShapeDtypeStruct((1, D), F32)],
        compiler_params=_cp(("arbitrary",)),
    )(dout, x, g, gate, up, wg, wu, wd)


def _ffn_dw(dgt, dut, actt, xn, dacc, name):
    F, T = dgt.shape
    D = xn.shape[1]
    th = _tile(F, 256)

    def body(dg_ref, du_ref, a_ref, xn_ref, dacc_ref, dwg_ref, dwu_ref, dwd_ref):
        xv = xn_ref[...]
        dwg_ref[...] = _dot(dg_ref[...], xv).astype(BF16)
        dwu_ref[...] = _dot(du_ref[...], xv).astype(BF16)
        dwd_ref[...] = _dot(a_ref[...], dacc_ref[...]).astype(BF16)

    tile = pl.BlockSpec((th, T), lambda j: (j, 0))
    out = pl.BlockSpec((th, D), lambda j: (j, 0))
    return pl.pallas_call(
        body, name=name, grid=(F // th,),
        in_specs=[tile, tile, tile, _resident((T, D)), _resident((T, D))],
        out_specs=[out, out, out], out_shape=[jax.ShapeDtypeStruct((F, D), BF16)] * 3,
        compiler_params=_cp(("parallel",)),
    )(dgt, dut, actt, xn, dacc)


def _norm_mm(x, g, w, name):
    T, D = x.shape
    N = w.shape[1]
    tm = _tile(T, 512)

    def body(x_ref, g_ref, w_ref, o_ref, ht_ref):
        xn = _rms_fwd(x_ref[...], g_ref[...])[0]
        ht_ref[...] = xn.T.astype(BF16)
        o_ref[...] = _dot(xn.astype(BF16), w_ref[...])

    return pl.pallas_call(
        body, name=name, grid=(T // tm,),
        in_specs=[pl.BlockSpec((tm, D), lambda i: (i, 0)), _resident((1, D)), _resident((D, N))],
        out_specs=[pl.BlockSpec((tm, N), lambda i: (i, 0)), pl.BlockSpec((D, tm), lambda i: (0, i))],
        out_shape=[jax.ShapeDtypeStruct((T, N), F32), jax.ShapeDtypeStruct((D, T), BF16)],
        compiler_params=_cp(("parallel",)),
    )(x, g, w)


def _norm_mm_bwd(dproj, x, g, w, dres, name):
    T, D = x.shape
    N = w.shape[1]
    tm = _tile(T, 512)

    def body(dp_ref, x_ref, g_ref, w_ref, dres_ref, dx_ref, dg_ref):
        @pl.when(pl.program_id(0) == 0)
        def _():
            dg_ref[...] = jnp.zeros_like(dg_ref)

        gv = g_ref[...]
        dh = _dot_nt(dp_ref[...], w_ref[...])
        _, xhat, rstd = _rms_fwd(x_ref[...], gv)
        dx, dgrow = _rms_bwd(dh, xhat, rstd, gv)
        dx_ref[...] = dres_ref[...] + dx
        dg_ref[...] += jnp.sum(dgrow, axis=0, keepdims=True)

    row = pl.BlockSpec((tm, D), lambda i: (i, 0))
    one = pl.BlockSpec((1, D), lambda i: (0, 0))
    return pl.pallas_call(
        body, name=name, grid=(T // tm,),
        in_specs=[pl.BlockSpec((tm, N), lambda i: (i, 0)), row, _resident((1, D)), _resident((D, N)), row],
        out_specs=[row, one],
        out_shape=[jax.ShapeDtypeStruct((T, D), F32), jax.ShapeDtypeStruct((1, D), F32)],
        compiler_params=_cp(("arbitrary",)),
    )(dproj, x, g, w, dres)


def _mm_resident_lhs(at, b, name, tn_cap=512):
    M, K = at.shape
    N = b.shape[1]
    tn = _tile(N, tn_cap)

    def body(a_ref, b_ref, o_ref):
        o_ref[...] = _dot(a_ref[...], b_ref[...]).astype(BF16)

    return pl.pallas_call(
        body, name=name, grid=(N // tn,),
        in_specs=[_resident((M, K)), pl.BlockSpec((K, tn), lambda j: (0, j))],
        out_specs=pl.BlockSpec((M, tn), lambda j: (0, j)),
        out_shape=jax.ShapeDtypeStruct((M, N), BF16),
        compiler_params=_cp(("parallel",)),
    )(at, b)


def _final_loss(x, g, target, name):
    T, D = x.shape
    tm = _tile(T, 512)

    def body(x_ref, g_ref, t_ref, loss_ref, dx_ref, dg_ref):
        @pl.when(pl.program_id(0) == 0)
        def _():
            dg_ref[...] = jnp.zeros_like(dg_ref)
            loss_ref[...] = jnp.zeros_like(loss_ref)

        gv = g_ref[...]
        y, xhat, rstd = _rms_fwd(x_ref[...], gv)
        err = y - t_ref[...]
        part = 0.5 * jnp.sum(jnp.mean(err * err, axis=-1, keepdims=True), axis=0, keepdims=True)
        loss_ref[...] += jnp.broadcast_to(part, loss_ref.shape)
        dy = err * (1.0 / D)
        dx, dgrow = _rms_bwd(dy, xhat, rstd, gv)
        dx_ref[...] = dx
        dg_ref[...] += jnp.sum(dgrow, axis=0, keepdims=True)

    row = pl.BlockSpec((tm, D), lambda i: (i, 0))
    one = pl.BlockSpec((1, D), lambda i: (0, 0))
    return pl.pallas_call(
        body, name=name, grid=(T // tm,),
        in_specs=[row, one, row],
        out_specs=[pl.BlockSpec((1, LANE), lambda i: (0, 0)), row, one],
        out_shape=[jax.ShapeDtypeStruct((1, LANE), F32), jax.ShapeDtypeStruct((T, D), F32),
                   jax.ShapeDtypeStruct((1, D), F32)],
        compiler_params=_cp(("arbitrary",)),
    )(x, g, target)


def _attn_bias(S, bq):
    d = jnp.arange(bq)[:, None] - jnp.arange(S)[None, :] + (S // bq - 1) * bq
    ok = d >= 0
    mult = ((ok & (d <= 128)).astype(F32) + (ok & (d % 4 == 0) & (d <= 512)).astype(F32)
            + (ok & (d % 16 == 0) & (d <= 2048)).astype(F32))
    return jnp.where(mult > 0, jnp.log(jnp.maximum(mult, 1.0)), NEG).astype(F32)


def _attn_fwd(proj, bias, B, S, name):
    T = B * S
    bq = bias.shape[0]
    nb = S // bq
    qcol, kcol, vcol = Q0 // LANE, K0 // LANE, V0 // LANE

    def body(q_ref, k_ref, v_ref, t_ref, o_ref, lse_ref, ks, vs):
        for hh in range(2):
            sl = slice(HEAD_DIM * hh, HEAD_DIM * (hh + 1))
            ks[hh] = k_ref[:, sl].astype(BF16)
            vs[hh] = v_ref[:, sl].astype(BF16)
        for hh in range(2):
            sl = slice(HEAD_DIM * hh, HEAD_DIM * (hh + 1))
            for qb in range(nb):
                w, off, rows = bq * (qb + 1), (nb - 1 - qb) * bq, slice(qb * bq, (qb + 1) * bq)
                q = (q_ref[rows, sl] * 0.125).astype(BF16)
                s = _dot_nt(q, ks[hh, 0:w, :]) + t_ref[:, off:off + w]
                m = jnp.max(s, axis=-1, keepdims=True)
                p = jnp.exp(s - m)
                l = jnp.sum(p, axis=-1, keepdims=True)
                o_ref[rows, sl] = _dot(p.astype(BF16), vs[hh, 0:w, :]) / l
                lse_ref[rows, hh:hh + 1] = m + jnp.log(l)

    blk = lambda c0: pl.BlockSpec((S, LANE), lambda b, p: (b, c0 + p))
    return pl.pallas_call(
        body, name=name, grid=(B, ATT_W // LANE),
        in_specs=[blk(qcol), blk(kcol), blk(vcol), _resident((bq, S))],
        out_specs=[pl.BlockSpec((S, LANE), lambda b, p: (b, p)),
                   pl.BlockSpec((None, None, S, 2), lambda b, p: (b, p, 0, 0))],
        out_shape=[jax.ShapeDtypeStruct((T, ATT_W), F32),
                   jax.ShapeDtypeStruct((B, ATT_W // LANE, S, 2), F32)],
        scratch_shapes=[pltpu.VMEM((2, S, HEAD_DIM), BF16)] * 2,
        compiler_params=_cp(("parallel", "parallel")),
    )(proj, proj, proj, bias)


def _attn_bwd(proj, o, lse, dy, bias, B, S, name):
    T = B * S
    bq = bias.shape[0]
    nb = S // bq
    qcol, kcol, vcol = Q0 // LANE, K0 // LANE, V0 // LANE

    def body(q_ref, k_ref, v_ref, o_ref, lse_ref, do_ref, t_ref, dq_ref, dk_ref, dv_ref, ks, vs, dks, dvs):
        for hh in range(2):
            sl = slice(HEAD_DIM * hh, HEAD_DIM * (hh + 1))
            ks[hh] = k_ref[:, sl].astype(BF16)
            vs[hh] = v_ref[:, sl].astype(BF16)
        dks[...] = jnp.zeros_like(dks)
        dvs[...] = jnp.zeros_like(dvs)
        for hh in range(2):
            sl = slice(HEAD_DIM * hh, HEAD_DIM * (hh + 1))
            for qb in range(nb):
                w, off, rows = bq * (qb + 1), (nb - 1 - qb) * bq, slice(qb * bq, (qb + 1) * bq)
                q = (q_ref[rows, sl] * 0.125).astype(BF16)
                do = do_ref[rows, sl]
                dob = do.astype(BF16)
                delta = jnp.sum(do * o_ref[rows, sl], axis=-1, keepdims=True)
                k, v = ks[hh, 0:w, :], vs[hh, 0:w, :]
                s = _dot_nt(q, k) + t_ref[:, off:off + w]
                p = jnp.exp(s - lse_ref[rows, hh:hh + 1])
                ds = (p * (_dot_nt(dob, v) - delta)).astype(BF16)
                dq_ref[rows, sl] = (_dot(ds, k) * 0.125).astype(dq_ref.dtype)
                dks[hh, 0:w, :] += _dot_tn(ds, q)
                dvs[hh, 0:w, :] += _dot_tn(p.astype(BF16), dob)
            dk_ref[:, sl] = dks[hh].astype(dk_ref.dtype)
            dv_ref[:, sl] = dvs[hh].astype(dv_ref.dtype)

    blk = lambda c0: pl.BlockSpec((S, LANE), lambda b, p: (b, c0 + p))
    own = pl.BlockSpec((S, LANE), lambda b, p: (b, p))
    return pl.pallas_call(
        body, name=name, grid=(B, ATT_W // LANE),
        in_specs=[blk(qcol), blk(kcol), blk(vcol), own,
                  pl.BlockSpec((None, None, S, 2), lambda b, p: (b, p, 0, 0)), own, _resident((bq, S))],
        out_specs=[own, own, own],
        out_shape=[jax.ShapeDtypeStruct((T, ATT_W), BF16)] * 3,
        scratch_shapes=[pltpu.VMEM((2, S, HEAD_DIM), BF16)] * 2 + [pltpu.VMEM((2, S, HEAD_DIM), F32)] * 2,
        compiler_params=_cp(("parallel", "parallel")),
    )(proj, proj, proj, o, lse, dy, bias)


def _conv_fwd(proj, cw, cb, B, S, name):
    T = B * S
    nc = SSD_CONV_DIM // LANE
    c0 = XBC0 // LANE

    def body(x_ref, w_ref, b_ref, o_ref):
        x = x_ref[...]
        t = lax.broadcasted_iota(jnp.int32, (S, 1), 0)
        acc = b_ref[...] + w_ref[SSD_CONV - 1:SSD_CONV, :] * x
        for k in range(SSD_CONV - 1):
            sh = SSD_CONV - 1 - k
            xs = jnp.where(t >= sh, pltpu.roll(x, sh, 0), 0.0)
            acc = acc + w_ref[k:k + 1, :] * xs
        o_ref[...] = acc

    return pl.pallas_call(
        body, name=name, grid=(B, nc),
        in_specs=[pl.BlockSpec((S, LANE), lambda b, j: (b, c0 + j)),
                  pl.BlockSpec((SUBLANE, LANE), lambda b, j: (0, j)),
                  pl.BlockSpec((1, LANE), lambda b, j: (0, j))],
        out_specs=pl.BlockSpec((S, LANE), lambda b, j: (b, j)),
        out_shape=jax.ShapeDtypeStruct((T, SSD_CONV_DIM), F32),
        compiler_params=_cp(("parallel", "parallel")),
    )(proj, cw, cb)


def _conv_bwd(dpre, proj, cw, B, S, name):
    T = B * S
    nc = SSD_CONV_DIM // LANE
    c0 = XBC0 // LANE

    def body(d_ref, x_ref, w_ref, dx_ref, dwb_ref):
        @pl.when(pl.program_id(1) == 0)
        def _():
            dwb_ref[...] = jnp.zeros_like(dwb_ref)

        d = d_ref[...]
        x = x_ref[...]
        t = lax.broadcasted_iota(jnp.int32, (S, 1), 0)
        dx = w_ref[SSD_CONV - 1:SSD_CONV, :] * d
        rows = [None] * SUBLANE
        rows[SSD_CONV - 1] = jnp.sum(d * x, axis=0, keepdims=True)
        for k in range(SSD_CONV - 1):
            sh = SSD_CONV - 1 - k
            dx = dx + w_ref[k:k + 1, :] * jnp.where(t < S - sh, pltpu.roll(d, S - sh, 0), 0.0)
            xs = jnp.where(t >= sh, pltpu.roll(x, sh, 0), 0.0)
            rows[k] = jnp.sum(d * xs, axis=0, keepdims=True)
        rows[SSD_CONV] = jnp.sum(d, axis=0, keepdims=True)
        dx_ref[...] = dx.astype(BF16)
        r = lax.broadcasted_iota(jnp.int32, (SUBLANE, LANE), 0)
        upd = jnp.zeros((SUBLANE, LANE), F32)
        for k in range(SSD_CONV + 1):
            upd = upd + jnp.where(r == k, rows[k], 0.0)
        dwb_ref[...] += upd

    return pl.pallas_call(
        body, name=name, grid=(nc, B),
        in_specs=[pl.BlockSpec((S, LANE), lambda j, b: (b, j)),
                  pl.BlockSpec((S, LANE), lambda j, b: (b, c0 + j)),
                  pl.BlockSpec((SUBLANE, LANE), lambda j, b: (0, j))],
        out_specs=[pl.BlockSpec((S, LANE), lambda j, b: (b, j)),
                   pl.BlockSpec((SUBLANE, LANE), lambda j, b: (0, j))],
        out_shape=[jax.ShapeDtypeStruct((T, SSD_CONV_DIM), BF16),
                   jax.ShapeDtypeStruct((SUBLANE, SSD_CONV_DIM), F32)],
        compiler_params=_cp(("parallel", "arbitrary")),
    )(dpre, proj, cw)


def _ssd_consts():
    e = np.zeros((LANE, SSD_W), np.float32)
    p = np.zeros((SUBLANE, SSD_W), np.float32)
    for h in range(SSD_HEADS):
        e[h, HEAD_DIM * h:HEAD_DIM * (h + 1)] = 1.0
        p[h, HEAD_DIM * h] = 1.0
    return jnp.asarray(e), jnp.asarray(p)


def _ssd_chunk(pre, z, dtr, sprev, par, e_mat, psel):
    L = CHUNK
    xc = _silu(pre)
    xs, bm, cm = xc[:, :SSD_W], xc[:, SSD_W:SSD_W + 2 * SSD_STATE], xc[:, SSD_W + 2 * SSD_STATE:]
    dtb, alog, dskip, ng = par[0:1], par[1:2], par[2:3], par[3:4]
    dt = _softplus(_dot(dtr, e_mat, HI) + dtb)
    a = dt * (-jnp.exp(alog))
    X = xs * dt
    ri = lax.broadcasted_iota(jnp.int32, (L, L), 0)
    ci = lax.broadcasted_iota(jnp.int32, (L, L), 1)
    tril = ri >= ci
    acs = _dot(tril.astype(F32), a, HI)
    acs_t = _dot_nt(psel, acs, HI)
    ecs = jnp.exp(acs)
    alast = acs[L - 1:L, :]
    xd = (X * jnp.exp(alast - acs)).astype(BF16)
    xb = X.astype(BF16)
    col = lax.broadcasted_iota(jnp.int32, (1, SSD_W), 1)
    sb = sprev.astype(BF16)
    bgs = [bm[:, SSD_STATE * g:SSD_STATE * (g + 1)].astype(BF16) for g in range(2)]
    cgs = [cm[:, SSD_STATE * g:SSD_STATE * (g + 1)].astype(BF16) for g in range(2)]
    cbs = [_dot_nt(cgs[g], bgs[g]) for g in range(2)]
    first = lax.broadcasted_iota(jnp.int32, (1, LANE), 1) < HEAD_DIM
    y_tiles, s_tiles = [], []
    for t in range(SSD_W // LANE):
        cl = slice(LANE * t, LANE * (t + 1))
        xb_t, xd_t, sb_t = xb[:, cl], xd[:, cl], sb[:, cl]
        per_head = []
        for h in (2 * t, 2 * t + 1):
            seg = acs[:, HEAD_DIM * h:HEAD_DIM * h + 1] - acs_t[h:h + 1, :]
            dec = jnp.exp(jnp.where(tril, seg, NEG))
            per_head.append(_dot((cbs[h // 3] * dec).astype(BF16), xb_t))
        y_t = jnp.where(first, per_head[0], per_head[1])
        ga, gb = (2 * t) // 3, (2 * t + 1) // 3
        if ga == gb:
            y_off, s_add = _dot(cgs[ga], sb_t), _dot_tn(bgs[ga], xd_t)
        else:
            y_off = jnp.where(first, _dot(cgs[ga], sb_t), _dot(cgs[gb], sb_t))
            s_add = jnp.where(first, _dot_tn(bgs[ga], xd_t), _dot_tn(bgs[gb], xd_t))
        y_tiles.append(y_t + y_off * ecs[:, cl])
        s_tiles.append(s_add)
    y = dskip * xs + jnp.concatenate(y_tiles, axis=1)
    snew = sprev * jnp.exp(alast) + jnp.concatenate(s_tiles, axis=1)
    yg = y * _silu(z)
    sq = yg * yg
    g0 = col < SSD_W // 2
    ms0 = jnp.sum(jnp.where(g0, sq, 0.0), axis=-1, keepdims=True) * (2.0 / SSD_W)
    ms1 = jnp.sum(jnp.where(g0, 0.0, sq), axis=-1, keepdims=True) * (2.0 / SSD_W)
    r = jnp.where(g0, lax.rsqrt(ms0 + RMS_EPS), lax.rsqrt(ms1 + RMS_EPS))
    return yg * r * ng, snew


SSD_CHUNKS_PER_STEP = 2


def _ssd_chunks_per_step(S):
    k = SSD_CHUNKS_PER_STEP
    while (S // CHUNK) % k:
        k //= 2
    return k


def _ssd_fwd(pre, proj, par, B, S, name):
    T = B * S
    k = _ssd_chunks_per_step(S)
    nc, rows = S // (CHUNK * k), CHUNK * k
    e_mat, psel = _ssd_consts()

    def body(pre_ref, z_ref, dt_ref, par_ref, e_ref, p_ref, y_ref, sall_ref, st):
        @pl.when(pl.program_id(1) == 0)
        def _():
            st[...] = jnp.zeros_like(st)

        sprev = st[...]
        for i in range(k):
            r = slice(CHUNK * i, CHUNK * (i + 1))
            sall_ref[i] = sprev
            y, sprev = _ssd_chunk(pre_ref[r, :], z_ref[r, :], dt_ref[r, :], sprev, par_ref[...], e_ref[...],
                                  p_ref[...])
            y_ref[r, :] = y.astype(BF16)
        st[...] = sprev

    row = lambda b, c: b * nc + c
    full = lambda shp: pl.BlockSpec(shp, lambda b, c: (0, 0))
    return pl.pallas_call(
        body, name=name, grid=(B, nc),
        in_specs=[pl.BlockSpec((rows, SSD_CONV_DIM), lambda b, c: (row(b, c), 0)),
                  pl.BlockSpec((rows, SSD_W), lambda b, c: (row(b, c), Z0 // SSD_W)),
                  pl.BlockSpec((rows, LANE), lambda b, c: (row(b, c), DT0 // LANE)),
                  full((SUBLANE, SSD_W)), full((LANE, SSD_W)), full((SUBLANE, SSD_W))],
        out_specs=[pl.BlockSpec((rows, SSD_W), lambda b, c: (row(b, c), 0)),
                   pl.BlockSpec((k, SSD_STATE, SSD_W), lambda b, c: (row(b, c), 0, 0))],
        out_shape=[jax.ShapeDtypeStruct((T, SSD_W), BF16),
                   jax.ShapeDtypeStruct((B * nc * k, SSD_STATE, SSD_W), F32)],
        scratch_shapes=[pltpu.VMEM((SSD_STATE, SSD_W), F32)],
        compiler_params=_cp(("parallel", "arbitrary")),
    )(pre, proj, proj, par, e_mat, psel)


def _ssd_bwd(pre, proj, sall, dy, par, B, S, name):
    T = B * S
    k = _ssd_chunks_per_step(S)
    nc, rows = S // (CHUNK * k), CHUNK * k
    e_mat, psel = _ssd_consts()

    def body(pre_ref, z_ref, dt_ref, sall_ref, dy_ref, par_ref, e_ref, p_ref,
             dpre_ref, dz_ref, ddt_ref, dpar_ref, ds):
        b, c = pl.program_id(0), pl.program_id(1)

        @pl.when(c == 0)
        def _():
            ds[...] = jnp.zeros_like(ds)

        @pl.when((b == 0) & (c == 0))
        def _():
            dpar_ref[...] = jnp.zeros_like(dpar_ref)

        e_v, p_v = e_ref[...], p_ref[...]
        fn = lambda pre, z, dtr, sprev, par: _ssd_chunk(pre, z, dtr, sprev, par, e_v, p_v)
        dstate, dpar_sum = ds[...], None
        for i in reversed(range(k)):
            r = slice(CHUNK * i, CHUNK * (i + 1))
            _, vjp = jax.vjp(fn, pre_ref[r, :], z_ref[r, :], dt_ref[r, :], sall_ref[i], par_ref[...])
            dpre, dz, ddt, dstate, dpar = vjp((dy_ref[r, :], dstate))
            dpre_ref[r, :] = dpre
            dz_ref[r, :] = dz.astype(BF16)
            ddt_ref[r, :] = ddt.astype(BF16)
            dpar_sum = dpar if dpar_sum is None else dpar_sum + dpar
        dpar_ref[...] += dpar_sum
        ds[...] = dstate

    row = lambda b, c: b * nc + (nc - 1 - c)
    full = lambda shp: pl.BlockSpec(shp, lambda b, c: (0, 0))
    return pl.pallas_call(
        body, name=name, grid=(B, nc),
        in_specs=[pl.BlockSpec((rows, SSD_CONV_DIM), lambda b, c: (row(b, c), 0)),
                  pl.BlockSpec((rows, SSD_W), lambda b, c: (row(b, c), Z0 // SSD_W)),
                  pl.BlockSpec((rows, LANE), lambda b, c: (row(b, c), DT0 // LANE)),
                  pl.BlockSpec((k, SSD_STATE, SSD_W), lambda b, c: (row(b, c), 0, 0)),
                  pl.BlockSpec((rows, SSD_W), lambda b, c: (row(b, c), ATT_W // SSD_W)),
                  full((SUBLANE, SSD_W)), full((LANE, SSD_W)), full((SUBLANE, SSD_W))],
        out_specs=[pl.BlockSpec((rows, SSD_CONV_DIM), lambda b, c: (row(b, c), 0)),
                   pl.BlockSpec((rows, SSD_W), lambda b, c: (row(b, c), 0)),
                   pl.BlockSpec((rows, LANE), lambda b, c: (row(b, c), 0)),
                   full((SUBLANE, SSD_W))],
        out_shape=[jax.ShapeDtypeStruct((T, SSD_CONV_DIM), F32),
                   jax.ShapeDtypeStruct((T, SSD_W), BF16),
                   jax.ShapeDtypeStruct((T, LANE), BF16),
                   jax.ShapeDtypeStruct((SUBLANE, SSD_W), F32)],
        scratch_shapes=[pltpu.VMEM((SSD_STATE, SSD_W), F32)],
        compiler_params=_cp(("arbitrary", "arbitrary")),
    )(pre, proj, proj, sall, dy, par, e_mat, psel)


def _sgu_consts():
    e = np.zeros((SUBLANE, SGU_W), np.float32)
    for g in range(SGU_GROUPS):
        e[g, HEAD_DIM * g:HEAD_DIM * (g + 1)] = 1.0
    return jnp.asarray(e)


def _sgu_chunk(u_raw, v_raw, ln, w, bst, e4):
    L = CHUNK
    u = _gelu(u_raw)
    v = _gelu(v_raw)
    mu = jnp.mean(v, axis=-1, keepdims=True)
    vc = v - mu
    var = jnp.mean(vc * vc, axis=-1, keepdims=True)
    vn = vc * lax.rsqrt(var + LN_EPS) * ln[0:1] + ln[1:2]
    vb = vn.astype(BF16)
    ri = lax.broadcasted_iota(jnp.int32, (L, L), 0)
    ci = lax.broadcasted_iota(jnp.int32, (L, L), 1)
    tril = ri >= ci
    col = lax.broadcasted_iota(jnp.int32, (1, SGU_W), 1)
    mixed = _dot(bst, e4, HI)
    for g in range(SGU_GROUPS):
        wc = jnp.where(tril, w[g], 0.0).astype(BF16)
        gm = (col >= HEAD_DIM * g) & (col < HEAD_DIM * (g + 1))
        mixed = mixed + jnp.where(gm, _dot(wc, vb), 0.0)
    return u * mixed


def _sgu_fwd(proj, ln, w, bst, B, S, name):
    T = B * S
    nc = S // CHUNK
    e4 = _sgu_consts()

    def body(u_ref, v_ref, ln_ref, w_ref, b_ref, e_ref, y_ref):
        y_ref[...] = _sgu_chunk(u_ref[...], v_ref[...], ln_ref[...], w_ref[...], b_ref[...], e_ref[...]).astype(BF16)

    return pl.pallas_call(
        body, name=name, grid=(T // CHUNK,),
        in_specs=[pl.BlockSpec((CHUNK, SGU_W), lambda i: (i, U0 // SGU_W)),
                  pl.BlockSpec((CHUNK, SGU_W), lambda i: (i, VS0 // SGU_W)),
                  pl.BlockSpec((SUBLANE, SGU_W), lambda i: (0, 0)),
                  pl.BlockSpec((SGU_GROUPS, CHUNK, CHUNK), lambda i: (0, 0, 0)),
                  pl.BlockSpec((CHUNK, SUBLANE), lambda i: (0, 0)),
                  pl.BlockSpec((SUBLANE, SGU_W), lambda i: (0, 0))],
        out_specs=pl.BlockSpec((CHUNK, SGU_W), lambda i: (i, 0)),
        out_shape=jax.ShapeDtypeStruct((T, SGU_W), BF16),
        compiler_params=_cp(("parallel",)),
    )(proj, proj, ln, w, bst, e4)


def _sgu_bwd(proj, dy, ln, w, bst, B, S, name):
    T = B * S
    e4 = _sgu_consts()
    ycol = (ATT_W + SSD_W) // SGU_W

    def body(u_ref, v_ref, dy_ref, ln_ref, w_ref, b_ref, e_ref, du_ref, dv_ref, dln_ref, dw_ref, db_ref):
        @pl.when(pl.program_id(0) == 0)
        def _():
            dln_ref[...] = jnp.zeros_like(dln_ref)
            dw_ref[...] = jnp.zeros_like(dw_ref)
            db_ref[...] = jnp.zeros_like(db_ref)

        e_v = e_ref[...]
        fn = lambda u, v, ln, w, b: _sgu_chunk(u, v, ln, w, b, e_v)
        _, vjp = jax.vjp(fn, u_ref[...], v_ref[...], ln_ref[...], w_ref[...], b_ref[...])
        du, dv, dln, dw, db = vjp(dy_ref[...])
        du_ref[...] = du.astype(BF16)
        dv_ref[...] = dv.astype(BF16)
        dln_ref[...] += dln
        dw_ref[...] += dw
        db_ref[...] += db

    c_ln = pl.BlockSpec((SUBLANE, SGU_W), lambda i: (0, 0))
    c_w = pl.BlockSpec((SGU_GROUPS, CHUNK, CHUNK), lambda i: (0, 0, 0))
    c_b = pl.BlockSpec((CHUNK, SUBLANE), lambda i: (0, 0))
    return pl.pallas_call(
        body, name=name, grid=(T // CHUNK,),
        in_specs=[pl.BlockSpec((CHUNK, SGU_W), lambda i: (i, U0 // SGU_W)),
                  pl.BlockSpec((CHUNK, SGU_W), lambda i: (i, VS0 // SGU_W)),
                  pl.BlockSpec((CHUNK, SGU_W), lambda i: (i, ycol)),
                  c_ln, c_w, c_b, pl.BlockSpec((SUBLANE, SGU_W), lambda i: (0, 0))],
        out_specs=[pl.BlockSpec((CHUNK, SGU_W), lambda i: (i, 0)),
                   pl.BlockSpec((CHUNK, SGU_W), lambda i: (i, 0)), c_ln, c_w, c_b],
        out_shape=[jax.ShapeDtypeStruct((T, SGU_W), BF16), jax.ShapeDtypeStruct((T, SGU_W), BF16),
                   jax.ShapeDtypeStruct((SUBLANE, SGU_W), F32),
                   jax.ShapeDtypeStruct((SGU_GROUPS, CHUNK, CHUNK), F32),
                   jax.ShapeDtypeStruct((CHUNK, SUBLANE), F32)],
        compiler_params=_cp(("arbitrary",)),
    )(proj, proj, dy, ln, w, bst, e4)


_HBM = pl.BlockSpec(memory_space=pltpu.HBM)
_SEM = pl.BlockSpec(memory_space=pltpu.SEMAPHORE)
_ANY = pl.BlockSpec(memory_space=pl.ANY)
_EFFECT = pltpu.SideEffectType.DATAFLOW_SIDE_EFFECTING


def _peers():
    x, y, c = lax.axis_index("x"), lax.axis_index("y"), lax.axis_index("c")
    out = []
    for p in range(1, N_DEV):
        px, py, pc = x ^ ((p >> 2) & 1), y ^ ((p >> 1) & 1), c ^ (p & 1)
        out.append(((px, py, pc), 4 * px + 2 * py + pc))
    return 4 * x + 2 * y + c, out


def _xchg_start(xs, a2a, order, name):
    n = len(xs)
    lands = [lax.empty(a.shape if f else (N_DEV,) + a.shape, a.dtype) for a, f in zip(xs, a2a)]

    def body(*refs):
        ins, zones = refs[:n], refs[n:2 * n]
        send_sems, recv_sems = refs[2 * n + 1], refs[2 * n + 2]
        token = refs[-1]
        me, peers = _peers()
        for p, (dev, peer) in enumerate(peers):
            for t in range(n):
                pltpu.make_async_remote_copy(
                    src_ref=ins[t].at[peer] if a2a[t] else ins[t], dst_ref=zones[t].at[me],
                    send_sem=send_sems.at[p * n + t], recv_sem=recv_sems.at[p * n + t],
                    device_id=dev, device_id_type=MESH).start()
        token[...] = jnp.zeros_like(token)

    hbm = lambda a: pltpu.HBM(a.shape, a.dtype)
    sems = pltpu.SemaphoreType.DMA(((N_DEV - 1) * n,))
    out = pl.pallas_call(
        body, name=name,
        in_specs=[_HBM] * (2 * n) + [_ANY],
        out_specs=[_SEM, _SEM] + [_HBM] * (2 * n) + [pl.BlockSpec(memory_space=pltpu.VMEM)],
        out_shape=[sems, sems] + [hbm(a) for a in xs] + [hbm(a) for a in lands]
        + [jax.ShapeDtypeStruct((SUBLANE, LANE), F32)],
        input_output_aliases={t: 2 + t for t in range(2 * n)},
        compiler_params=pltpu.CompilerParams(has_side_effects=_EFFECT),
    )(*[pltpu.with_memory_space_constraint(a, pltpu.HBM) for a in list(xs) + list(lands)], order)
    return out[0], out[1], out[2:2 + n], out[2 + n:2 + 2 * n], out[-1]


def _xchg_wait(started, a2a, after, name):
    send_sems, recv_sems, xs, lands, _ = started
    n = len(xs)

    def body(*refs):
        ins, zones = refs[:n], refs[n:2 * n]
        send_s, recv_s = refs[2 * n], refs[2 * n + 1]
        me, peers = _peers()
        cps = []
        for p, (dev, peer) in enumerate(peers):
            for t in range(n):
                cps.append(pltpu.make_async_remote_copy(
                    src_ref=ins[t].at[peer] if a2a[t] else ins[t], dst_ref=zones[t].at[peer],
                    send_sem=send_s.at[p * n + t], recv_sem=recv_s.at[p * n + t],
                    device_id=dev, device_id_type=MESH))
        for cp in cps:
            cp.wait_recv()
        for cp in cps:
            cp.wait_send()

    hbm = lambda a: pltpu.HBM(a.shape, a.dtype)
    out = pl.pallas_call(
        body, name=name,
        in_specs=[_HBM] * (2 * n) + [_SEM, _SEM, _ANY],
        out_specs=[_HBM] * (2 * n),
        out_shape=[hbm(a) for a in xs] + [hbm(a) for a in lands],
        input_output_aliases={t: t for t in range(2 * n)},
        compiler_params=pltpu.CompilerParams(has_side_effects=_EFFECT),
    )(*xs, *lands, send_sems, recv_sems, after)
    return out[:n], out[n:]


def _chip_peers():
    x, y, c = lax.axis_index("x"), lax.axis_index("y"), lax.axis_index("c")
    chips = [(1 - x, y), (x, 1 - y), (1 - x, 1 - y)]
    slot = lambda px, py, pc: 4 * px + 2 * py + pc
    return (x, y, c), chips, slot


def _gather_start(xs, order, name):
    n = len(xs)
    lands = [lax.empty((N_DEV,) + a.shape, a.dtype) for a in xs]

    def body(*refs):
        ins, zones = refs[:n], refs[n:2 * n]
        send_sems, d2d_sems, ici_sems = refs[2 * n + 1:2 * n + 4]
        token = refs[-1]
        (x, y, c), chips, slot = _chip_peers()
        me = slot(x, y, c)
        for t in range(n):
            for j, (px, py) in enumerate(chips):
                pltpu.make_async_remote_copy(
                    src_ref=ins[t], dst_ref=zones[t].at[me], send_sem=send_sems.at[(1 + j) * n + t],
                    recv_sem=ici_sems.at[j * n + t], device_id=(px, py, c), device_id_type=MESH).start()
            pltpu.make_async_remote_copy(
                src_ref=ins[t], dst_ref=zones[t].at[me], send_sem=send_sems.at[t],
                recv_sem=d2d_sems.at[t], device_id=(x, y, 1 - c), device_id_type=MESH).start()
        token[...] = jnp.zeros_like(token)

    hbm = lambda a: pltpu.HBM(a.shape, a.dtype)
    dma = lambda k: pltpu.SemaphoreType.DMA((k,))
    out = pl.pallas_call(
        body, name=name,
        in_specs=[_HBM] * (2 * n) + [_ANY],
        out_specs=[_SEM, _SEM, _SEM] + [_HBM] * (2 * n) + [pl.BlockSpec(memory_space=pltpu.VMEM)],
        out_shape=[dma(4 * n), dma(n), dma(3 * n)] + [hbm(a) for a in xs] + [hbm(a) for a in lands]
        + [jax.ShapeDtypeStruct((SUBLANE, LANE), F32)],
        input_output_aliases={t: 3 + t for t in range(2 * n)},
        compiler_params=pltpu.CompilerParams(has_side_effects=_EFFECT),
    )(*[pltpu.with_memory_space_constraint(a, pltpu.HBM) for a in list(xs) + list(lands)], order)
    return dict(send=out[0], d2d=out[1], ici=out[2], xs=out[3:3 + n], lands=out[3 + n:3 + 2 * n], token=out[-1])


def _gather_relay(st, after, name):
    n = len(st["xs"])

    def body(*refs):
        zones, ici_sems = refs[:n], refs[n]
        fsend, frecv = refs[n + 2], refs[n + 3]
        token = refs[-1]
        (x, y, c), chips, slot = _chip_peers()
        for t in range(n):
            for j, (px, py) in enumerate(chips):
                blk = zones[t].at[slot(px, py, c)]
                fwd = pltpu.make_async_remote_copy(
                    src_ref=blk, dst_ref=blk, send_sem=fsend.at[j * n + t], recv_sem=ici_sems.at[j * n + t],
                    device_id=(x, y, 1 - c), device_id_type=MESH)
                fwd.wait_recv()
                pltpu.make_async_remote_copy(
                    src_ref=blk, dst_ref=blk, send_sem=fsend.at[j * n + t], recv_sem=frecv.at[j * n + t],
                    device_id=(x, y, 1 - c), device_id_type=MESH).start()
        token[...] = jnp.zeros_like(token)

    hbm = lambda a: pltpu.HBM(a.shape, a.dtype)
    dma = lambda k: pltpu.SemaphoreType.DMA((k,))
    out = pl.pallas_call(
        body, name=name,
        in_specs=[_HBM] * n + [_SEM, _ANY],
        out_specs=[_SEM, _SEM] + [_HBM] * n + [pl.BlockSpec(memory_space=pltpu.VMEM)],
        out_shape=[dma(3 * n), dma(3 * n)] + [hbm(a) for a in st["lands"]]
        + [jax.ShapeDtypeStruct((SUBLANE, LANE), F32)],
        input_output_aliases={t: 2 + t for t in range(n)},
        compiler_params=pltpu.CompilerParams(has_side_effects=_EFFECT),
    )(*st["lands"], st["ici"], after)
    return dict(st, fsend=out[0], frecv=out[1], lands=out[2:2 + n], token=out[-1])


def _gather_wait(st, after, name):
    n = len(st["xs"])

    def body(*refs):
        ins, zones = refs[:n], refs[n:2 * n]
        send_sems, d2d_sems, fsend, frecv = refs[2 * n:2 * n + 4]
        (x, y, c), chips, slot = _chip_peers()
        sib = (x, y, 1 - c)
        for t in range(n):
            mine = lambda s, r, dst: pltpu.make_async_remote_copy(
                src_ref=ins[t], dst_ref=dst, send_sem=s, recv_sem=r, device_id=sib, device_id_type=MESH)
            direct = mine(send_sems.at[t], d2d_sems.at[t], zones[t].at[slot(x, y, 1 - c)])
            direct.wait_recv()
            direct.wait_send()
            for j, (px, py) in enumerate(chips):
                mine(send_sems.at[(1 + j) * n + t], d2d_sems.at[t], zones[t].at[slot(px, py, c)]).wait_send()
                relayed = mine(fsend.at[j * n + t], frecv.at[j * n + t], zones[t].at[slot(px, py, 1 - c)])
                relayed.wait_recv()
                relayed.wait_send()

    hbm = lambda a: pltpu.HBM(a.shape, a.dtype)
    out = pl.pallas_call(
        body, name=name,
        in_specs=[_HBM] * (2 * n) + [_SEM] * 4 + [_ANY],
        out_specs=[_HBM] * (2 * n),
        out_shape=[hbm(a) for a in st["xs"]] + [hbm(a) for a in st["lands"]],
        input_output_aliases={t: t for t in range(2 * n)},
        compiler_params=pltpu.CompilerParams(has_side_effects=_EFFECT),
    )(*st["xs"], *st["lands"], st["send"], st["d2d"], st["fsend"], st["frecv"], after)
    return out[:n], out[n:]


def _cast_layers(pairs, name):
    def body(*refs):
        n = len(refs) // 2
        for i in range(n):
            refs[n + i][...] = refs[i][...].astype(BF16)

    in_specs = [pl.BlockSpec((None,) + w.shape[1:], functools.partial(lambda l, i: (l, 0, 0), l),
                             pipeline_mode=pl.Buffered(1)) for w, l in pairs]
    return pl.pallas_call(
        body, name=name, grid=(1,), in_specs=in_specs,
        out_specs=[pl.BlockSpec(w.shape[1:], lambda i: (0, 0)) for w, _ in pairs],
        out_shape=[jax.ShapeDtypeStruct(w.shape[1:], BF16) for w, _ in pairs],
        compiler_params=_cp(("arbitrary",)),
    )(*[w for w, _ in pairs])


def _adamw(me, w, m, v, parts, own, name, layer=0, into=None):
    L, R, C = w.shape
    P = parts.shape[0]
    tr = R
    t = 16
    while t <= R:
        if R % t == 0 and t * C <= 131072:
            tr = t
        t += 16
    if tr == R and R * C > 131072 and R % 16 == 0:
        tr = 16
    own_all = own.shape[0] == P

    def body(me_ref, w_ref, m_ref, v_ref, p_ref, own_ref, *rest):
        g_ref, d_ref, mo_ref, vo_ref = rest[-4:]
        mine = own_ref[...].astype(F32)
        g = None
        for p in range(P):
            term = jnp.where(me_ref[0] == p, mine, p_ref[p].astype(F32))
            g = term if g is None else g + term
        mn = ADAM_B1 * m_ref[...] + (1.0 - ADAM_B1) * g
        vn = ADAM_B2 * v_ref[...] + (1.0 - ADAM_B2) * (g * g)
        m_hat = mn / (1.0 - ADAM_B1 ** ADAM_STEP)
        v_hat = vn / (1.0 - ADAM_B2 ** ADAM_STEP)
        g_ref[...] = g
        d_ref[...] = -ADAM_LR * (m_hat / (jnp.sqrt(v_hat) + ADAM_EPS) + ADAM_WD * w_ref[...])
        mo_ref[...] = mn
        vo_ref[...] = vn

    blk = pl.BlockSpec((None, tr, C), lambda i, me_ref: (layer, i, 0))
    own_blk = pl.BlockSpec((None, tr, C), lambda i, me_ref: (me_ref[0] if own_all else 0, i, 0))
    prev = list(into) if into is not None else []
    return pl.pallas_call(
        body, name=name,
        grid_spec=pltpu.PrefetchScalarGridSpec(
            num_scalar_prefetch=1, grid=(R // tr,),
            in_specs=[blk, blk, blk, pl.BlockSpec((P, tr, C), lambda i, me_ref: (0, i, 0)), own_blk]
            + [_ANY] * len(prev),
            out_specs=[blk] * 4),
        out_shape=[jax.ShapeDtypeStruct((L, R, C), F32)] * 4,
        input_output_aliases={6 + i: i for i in range(len(prev))},
        compiler_params=_cp(("parallel",)),
    )(me, w, m, v, parts, own, *prev)


def _perm_cols(w):
    pad = jnp.zeros((w.shape[0], LANE - SSD_HEADS), w.dtype)
    return jnp.concatenate([w[:, 0:1536], w[:, 2438:2694], w[:, 1536:2432], w[:, 2432:2438], pad,
                            w[:, 2694:2950]], axis=1)


def _unperm_cols(w):
    return jnp.concatenate([w[:, 0:1536], w[:, XBC0:XBC0 + SSD_CONV_DIM], w[:, DT0:DT0 + SSD_HEADS],
                            w[:, U0:U0 + SGU_W], w[:, VS0:VS0 + SGU_W]], axis=1)


_SMALL = ("ffn1_norm", "mix_norm", "conv_w", "conv_b", "dt_bias", "a_log", "d_skip", "ssd_norm",
          "sgu_ln_g", "sgu_ln_b", "sgu_w", "sgu_b", "ffn2_norm", "final_norm", "loss")


_SMALL_LAST = ("ffn1_norm",)
_SMALL_EARLY = tuple(k for k in _SMALL if k not in _SMALL_LAST)


def _pack(d, names):
    v = jnp.concatenate([d[k].astype(F32).reshape(-1) for k in names])
    n = v.shape[0]
    npad = -(-n // (LANE * 16)) * (LANE * 16)
    return jnp.pad(v, (0, npad - n)).reshape(npad // LANE, LANE)


def _unpack(p, shapes, names):
    v = p.reshape(-1)
    out, o = {}, 0
    for k in names:
        n = int(np.prod(shapes[k]))
        out[k] = v[o:o + n].reshape(shapes[k])
        o += n
    return out


def kernel(x, ffn1_norm, ffn1_w_gate, ffn1_w_up, ffn1_w_down, mix_norm, w_in, conv_w, conv_b, dt_bias, a_log, d_skip, ssd_norm, sgu_ln_g, sgu_ln_b, sgu_w, sgu_b, w_out, ffn2_norm, ffn2_w_gate, ffn2_w_up, ffn2_w_down, final_norm, loss_target, m_ffn1_norm, m_ffn1_w_gate, m_ffn1_w_up, m_ffn1_w_down, m_mix_norm, m_w_in, m_conv_w, m_conv_b, m_dt_bias, m_a_log, m_d_skip, m_ssd_norm, m_sgu_ln_g, m_sgu_ln_b, m_sgu_w, m_sgu_b, m_w_out, m_ffn2_norm, m_ffn2_w_gate, m_ffn2_w_up, m_ffn2_w_down, m_final_norm, v_ffn1_norm, v_ffn1_w_gate, v_ffn1_w_up, v_ffn1_w_down, v_mix_norm, v_w_in, v_conv_w, v_conv_b, v_dt_bias, v_a_log, v_d_skip, v_ssd_norm, v_sgu_ln_g, v_sgu_ln_b, v_sgu_w, v_sgu_b, v_w_out, v_ffn2_norm, v_ffn2_w_gate, v_ffn2_w_up, v_ffn2_w_down, v_final_norm):
    B, S, D = x.shape
    T = B * S
    L = ffn1_norm.shape[0]
    me = 4 * lax.axis_index("x") + 2 * lax.axis_index("y") + lax.axis_index("c")
    cs = conv_w.shape[2]
    W = dict(ffn1_norm=ffn1_norm, ffn1_w_gate=ffn1_w_gate, ffn1_w_up=ffn1_w_up, ffn1_w_down=ffn1_w_down,
             mix_norm=mix_norm, w_in=w_in, conv_w=conv_w, conv_b=conv_b, dt_bias=dt_bias, a_log=a_log,
             d_skip=d_skip, ssd_norm=ssd_norm, sgu_ln_g=sgu_ln_g, sgu_ln_b=sgu_ln_b, sgu_w=sgu_w, sgu_b=sgu_b,
             w_out=w_out, ffn2_norm=ffn2_norm, ffn2_w_gate=ffn2_w_gate, ffn2_w_up=ffn2_w_up,
             ffn2_w_down=ffn2_w_down, final_norm=final_norm)
    M = dict(ffn1_norm=m_ffn1_norm, ffn1_w_gate=m_ffn1_w_gate, ffn1_w_up=m_ffn1_w_up, ffn1_w_down=m_ffn1_w_down,
             mix_norm=m_mix_norm, w_in=m_w_in, conv_w=m_conv_w, conv_b=m_conv_b, dt_bias=m_dt_bias, a_log=m_a_log,
             d_skip=m_d_skip, ssd_norm=m_ssd_norm, sgu_ln_g=m_sgu_ln_g, sgu_ln_b=m_sgu_ln_b, sgu_w=m_sgu_w,
             sgu_b=m_sgu_b, w_out=m_w_out, ffn2_norm=m_ffn2_norm, ffn2_w_gate=m_ffn2_w_gate,
             ffn2_w_up=m_ffn2_w_up, ffn2_w_down=m_ffn2_w_down, final_norm=m_final_norm)
    V = dict(ffn1_norm=v_ffn1_norm, ffn1_w_gate=v_ffn1_w_gate, ffn1_w_up=v_ffn1_w_up, ffn1_w_down=v_ffn1_w_down,
             mix_norm=v_mix_norm, w_in=v_w_in, conv_w=v_conv_w, conv_b=v_conv_b, dt_bias=v_dt_bias, a_log=v_a_log,
             d_skip=v_d_skip, ssd_norm=v_ssd_norm, sgu_ln_g=v_sgu_ln_g, sgu_ln_b=v_sgu_ln_b, sgu_w=v_sgu_w,
             sgu_b=v_sgu_b, w_out=v_w_out, ffn2_norm=v_ffn2_norm, ffn2_w_gate=v_ffn2_w_gate,
             ffn2_w_up=v_ffn2_w_up, ffn2_w_down=v_ffn2_w_down, final_norm=v_final_norm)
    FFN1 = ("ffn1_w_gate", "ffn1_w_up", "ffn1_w_down")
    FFN2 = ("ffn2_w_gate", "ffn2_w_up", "ffn2_w_down")
    MIX = ("w_in", "w_out")
    big = FFN1 + MIX + FFN2
    col_sharded = lambda k: k.endswith("w_gate") or k.endswith("w_up")
    for dct in (W, M, V):
        for k in big:
            if col_sharded(k):
                dct[k] = jnp.swapaxes(dct[k], 1, 2)

    wgroups = [[(k, 0) for k in FFN1], [("w_in", 0), ("conv_w", None)], [("w_out", 0)] + [(k, 0) for k in FFN2]]
    wgroups += [[(k, l) for k in big] for l in range(1, L)]
    wstarted, order = [], x
    later = [kl for grp in wgroups[1:] for kl in grp if kl[0] != "conv_w"]
    cast = dict(zip(wgroups[0], _cast_layers([(W[k], l) for k, l in wgroups[0]], "cast_first")))
    for gi, grp in enumerate(wgroups):
        if gi == 1:
            first = lax.optimization_barrier((W[later[0][0]], order))[0]
            srcs = [(first if i == 0 else W[k], l) for i, (k, l) in enumerate(later)]
            cast.update(zip(later, _cast_layers(srcs, "cast_rest")))
        xs = [conv_w if k == "conv_w" else cast[(k, l)] for k, l in grp]
        st = _gather_start(xs, order, f"gather_start_{gi}")
        order = st["token"]
        wstarted.append(st)
    G = {}
    is_me = (jnp.arange(N_DEV) == me)

    zero1 = jnp.zeros((1,), F32)
    W["loss"], M["loss"], V["loss"] = zero1, zero1, zero1
    full_shapes = {k: (W[k].shape if k != "conv_w" else (L, SSD_CONV, SSD_CONV_DIM)) for k in _SMALL}
    embed = lambda a, k: a if k != "conv_w" else lax.dynamic_update_slice(
        jnp.zeros(full_shapes[k], F32), a, (0, 0, me * cs))
    small_packs = {names: [_pack({k: embed(d[k], k) for k in names}, names)[None] for d in (W, M, V)]
                   for names in (_SMALL_EARLY, _SMALL_LAST)}
    tied = lax.optimization_barrier((order, small_packs))
    order, small_packs = tied

    def relay(gi, after):
        wstarted[gi] = _gather_relay(wstarted[gi], after, f"gather_relay_{gi}")
        return wstarted[gi]["token"]

    def gathered(gi, after):
        own, lands = _gather_wait(wstarted[gi], after, f"gather_wait_{gi}")
        for key, o, z in zip(wgroups[gi], own, lands):
            G[key] = jnp.where(is_me.reshape((N_DEV,) + (1,) * o.ndim), o[None], z)

    def rows(k, l):
        a = G[(k, l)]
        return a.reshape(-1, a.shape[-1])

    bias = _attn_bias(S, min(256, S))
    row1 = lambda a: a.reshape(1, -1)

    def ffn1_params(l):
        return dict(g1=row1(ffn1_norm[l]), wg1=rows("ffn1_w_gate", l), wu1=rows("ffn1_w_up", l),
                    wd1=rows("ffn1_w_down", l))

    def out_params(l):
        return dict(wout=rows("w_out", l), g2=row1(ffn2_norm[l]), wg2=rows("ffn2_w_gate", l),
                    wu2=rows("ffn2_w_up", l), wd2=rows("ffn2_w_down", l))

    def mix_params(l):
        cw = jnp.transpose(G[("conv_w", None)][:, l], (1, 0, 2)).reshape(SSD_CONV, -1)
        return dict(
            gm=row1(mix_norm[l]), win=_perm_cols(rows("w_in", l)),
            cw=jnp.pad(cw, ((0, SUBLANE - SSD_CONV), (0, 0))), cb=row1(conv_b[l]),
            par=jnp.pad(jnp.stack([jnp.repeat(dt_bias[l], HEAD_DIM), jnp.repeat(a_log[l], HEAD_DIM),
                                   jnp.repeat(d_skip[l], HEAD_DIM), ssd_norm[l]]), ((0, SUBLANE - 4), (0, 0))),
            ln=jnp.pad(jnp.stack([sgu_ln_g[l], sgu_ln_b[l]]), ((0, SUBLANE - 2), (0, 0))),
            sw=sgu_w[l], bst=jnp.pad(sgu_b[l].T, ((0, 0), (0, SUBLANE - SGU_GROUPS))))

    xc = x.reshape(T, D)
    saved, lay = [], []
    tie = lambda a, tok: lax.optimization_barrier((a, tok))[0]
    for l in range(L):
        gathered(0 if l == 0 else l + 2, relay(0, order) if l == 0 else xc)
        p = ffn1_params(l)
        x1, gate1, up1 = _ffn_fwd(xc, p["g1"], p["wg1"], p["wu1"], p["wd1"], f"ffn1_fwd_{l}")
        if l == 0:
            gathered(1, relay(1, x1))
        p.update(mix_params(l))
        lay.append(p)
        proj, ht = _norm_mm(x1, p["gm"], p["win"], f"in_proj_{l}")
        if l == 0:
            proj = tie(proj, relay(2, proj))
        o_att, lse = _attn_fwd(proj, bias, B, S, f"attn_fwd_{l}")
        pre = _conv_fwd(proj, p["cw"], p["cb"], B, S, f"conv_fwd_{l}")
        y_ssd, sall = _ssd_fwd(pre, proj, p["par"], B, S, f"ssd_fwd_{l}")
        y_sgu = _sgu_fwd(proj, p["ln"], p["sw"], p["bst"], B, S, f"sgu_fwd_{l}")
        ycat = jnp.concatenate([o_att.astype(BF16), y_ssd, y_sgu], axis=1)
        if l + 1 < L:
            ycat = tie(ycat, relay(l + 3, ycat))
        if l == 0:
            gathered(2, ycat)
        p.update(out_params(l))
        x2 = _mm(ycat, p["wout"], "nn", f"out_proj_{l}", residual=x1)
        x3, gate2, up2 = _ffn_fwd(x2, p["g2"], p["wg2"], p["wu2"], p["wd2"], f"ffn2_fwd_{l}")
        saved.append(dict(x0=xc, gate1=gate1, up1=up1, x1=x1, ht=ht, proj=proj, o_att=o_att, lse=lse, pre=pre,
                          sall=sall, ycat=ycat, x2=x2, gate2=gate2, up2=up2))
        xc = x3
    loss_part, dx, dgf = _final_loss(xc, row1(final_norm), loss_target.reshape(T, D), "final_loss")

    gl = [dict() for _ in range(L)]
    gstarted, gorder = [], [order]

    def to_blocks(k, a):
        return a.reshape(N_DEV, -1, a.shape[-1]).astype(BF16)

    def send_grads(keys, l, extra, tag, small_names=None):
        xs = [to_blocks(k, gl[l][k]) for k in keys] + extra
        flags = [True] * len(keys) + [False] * len(extra)
        st = _xchg_start(xs, flags, gorder[0], f"grads_start_{tag}")
        gorder[0] = st[-1]
        gstarted.append((keys, l, st, flags, tag, small_names))

    def small_grads(names):
        sm = {}
        for k in names:
            if k == "final_norm":
                sm[k] = dgf.reshape(-1)
            elif k == "loss":
                sm[k] = loss_part[0, :1]
            else:
                sm[k] = jnp.stack([gl[l][k] for l in range(L)])
        return _pack(sm, names)

    def behind(a):
        return lax.optimization_barrier((a, gorder[0]))[0]

    for l in reversed(range(L)):
        p, s, g = lay[l], saved[l], gl[l]
        dx2, dgt, dut, actt, xn, dacc, g["ffn2_norm"] = _ffn_bwd_dx(
            dx, s["x2"], p["g2"], s["gate2"], s["up2"], p["wg2"], p["wu2"], p["wd2"], f"ffn2_bwd_{l}")
        g["ffn2_w_gate"], g["ffn2_w_up"], g["ffn2_w_down"] = _ffn_dw(dgt, dut, actt, xn, dacc, f"ffn2_dw_{l}")
        if l == 0:
            send_grads(FFN2, 0, [], "l0f")
            dx2 = behind(dx2)
        dycat = _mm(dx2, p["wout"], "nt", f"out_proj_dx_{l}")
        g["w_out"] = _mm(s["ycat"], dx2, "tn", f"out_proj_dw_{l}", out_dtype=BF16, tm_cap=1024, tk_cap=512)
        dq, dk, dv = _attn_bwd(s["proj"], s["o_att"], s["lse"], dycat, bias, B, S, f"attn_bwd_{l}")
        dpre, dz, ddt, dpar = _ssd_bwd(s["pre"], s["proj"], s["sall"], dycat, p["par"], B, S, f"ssd_bwd_{l}")
        dxbc, dwb = _conv_bwd(dpre, s["proj"], p["cw"], B, S, f"conv_bwd_{l}")
        du, dvs, dln, dsw, dbst = _sgu_bwd(s["proj"], dycat, p["ln"], p["sw"], p["bst"], B, S, f"sgu_bwd_{l}")
        hsum = lambda r: r.reshape(SSD_HEADS, HEAD_DIM).sum(-1)
        g["conv_w"], g["conv_b"] = dwb[:SSD_CONV], dwb[SSD_CONV]
        g["dt_bias"], g["a_log"], g["d_skip"], g["ssd_norm"] = hsum(dpar[0]), hsum(dpar[1]), hsum(dpar[2]), dpar[3]
        g["sgu_ln_g"], g["sgu_ln_b"], g["sgu_w"], g["sgu_b"] = dln[0], dln[1], dsw, dbst[:, :SGU_GROUPS].T
        dproj = jnp.concatenate([dq, dk, dv, dz, du, dxbc, ddt, dvs], axis=1)
        g["w_in"] = _unperm_cols(_mm_resident_lhs(s["ht"], dproj, f"in_proj_dw_{l}"))
        dx1, g["mix_norm"] = _norm_mm_bwd(dproj, s["x1"], p["gm"], p["win"], dx2, f"in_proj_bwd_{l}")
        if l == 0:
            send_grads(MIX, 0, [small_grads(_SMALL_EARLY)], "l0a", _SMALL_EARLY)
            dx1 = behind(dx1)
        dx, dgt, dut, actt, xn, dacc, g["ffn1_norm"] = _ffn_bwd_dx(
            dx1, s["x0"], p["g1"], s["gate1"], s["up1"], p["wg1"], p["wu1"], p["wd1"], f"ffn1_bwd_{l}")
        g["ffn1_w_gate"], g["ffn1_w_up"], g["ffn1_w_down"] = _ffn_dw(dgt, dut, actt, xn, dacc, f"ffn1_dw_{l}")
        if l > 0:
            send_grads(big, l, [], f"l{l}")
            dx = behind(dx)
    grad_x = dx.reshape(B, S, D)
    send_grads(FFN1, 0, [small_grads(_SMALL_LAST)], "l0b", _SMALL_LAST)

    res, after = {}, gorder[0]
    small_out = [dict() for _ in range(4)]
    me1 = me.reshape(1).astype(jnp.int32)
    for keys, l, st, flags, tag, names in gstarted:
        own, lands = _xchg_wait(st, flags, after, f"grads_wait_{tag}")
        for k, mine, pk in zip(keys, own, lands):
            res[k] = _adamw(me1, W[k], M[k], V[k], pk, mine, f"adamw_{k}_{l}", layer=l, into=res.get(k))
        done = [res[k][0] for k in keys]
        if names:
            outs = _adamw(me1, *small_packs[names], lands[-1], own[-1][None], f"adamw_small_{tag}")
            for d, o in zip(small_out, outs):
                u = _unpack(o, full_shapes, names)
                if "conv_w" in u:
                    u["conv_w"] = lax.dynamic_slice(u["conv_w"], (0, 0, me * cs), (L, SSD_CONV, cs))
                d.update(u)
                done.extend(u.values())
        after = lax.optimization_barrier(tuple(done))[0]
    back = lambda k, a: jnp.swapaxes(a, 1, 2) if col_sharded(k) else a
    grads, deltas, new_m, new_v = [dict({k: back(k, res[k][i]) for k in big}, **small_out[i]) for i in range(4)]

    names = ("ffn1_norm", "ffn1_w_gate", "ffn1_w_up", "ffn1_w_down", "mix_norm", "w_in", "conv_w", "conv_b",
             "dt_bias", "a_log", "d_skip", "ssd_norm", "sgu_ln_g", "sgu_ln_b", "sgu_w", "sgu_b", "w_out",
             "ffn2_norm", "ffn2_w_gate", "ffn2_w_up", "ffn2_w_down", "final_norm")
    loss = grads["loss"][0]
    return (loss, grad_x, *[grads[n] for n in names], *[deltas[n] for n in names],
            *[new_m[n] for n in names], *[new_v[n] for n in names])
```

```python
import functools

import numpy as np
import jax
import jax.numpy as jnp
from jax import lax
from jax.experimental import pallas as pl
from jax.experimental.pallas import tpu as pltpu

F32, BF16 = jnp.float32, jnp.bfloat16
HI = lax.Precision.HIGHEST
MESH = pl.DeviceIdType.MESH
N_DEV = 8
VMEM_LIMIT_BYTES = 56 * 1024 * 1024
LANE, SUBLANE = 128, 8

HEAD_DIM = 64
ATT_W = 384
SSD_W = 384
SSD_HEADS = 6
SSD_STATE = 128
SSD_CONV = 4
CHUNK = 128
SSD_CONV_DIM = 896
SGU_W = 256
SGU_GROUPS = 4
D_IN = 2950
RMS_EPS = 1e-6
LN_EPS = 1e-5
NEG = -1e30

PW = 3072
Q0, K0, V0, Z0, U0, XBC0, DT0, VS0 = 0, 384, 768, 1152, 1536, 1792, 2688, 2816

ADAM_LR, ADAM_B1, ADAM_B2, ADAM_EPS, ADAM_WD, ADAM_STEP = 0.001, 0.9, 0.999, 1e-08, 0.01, 10


def _cp(sem=None):
    return pltpu.CompilerParams(dimension_semantics=sem, vmem_limit_bytes=VMEM_LIMIT_BYTES)


def _tile(n, cap, mult=LANE):
    best = None
    t = mult
    while t <= min(n, cap):
        if n % t == 0:
            best = t
        t += mult
    return best if best is not None else n


def _dot(a, b, prec=None):
    return jnp.dot(a, b, preferred_element_type=F32, precision=prec)


def _dot_nt(a, b, prec=None):
    return lax.dot_general(a, b, (((1,), (1,)), ((), ())), preferred_element_type=F32, precision=prec)


def _dot_tn(a, b, prec=None):
    return lax.dot_general(a, b, (((0,), (0,)), ((), ())), preferred_element_type=F32, precision=prec)


def _sigmoid(x):
    return 1.0 / (1.0 + jnp.exp(-x))


def _silu(x):
    return x * _sigmoid(x)


def _gelu(x):
    return 0.5 * x * (1.0 + lax.erf(x * 0.7071067811865476))


def _softplus(x):
    return jnp.maximum(x, 0.0) + jnp.log(1.0 + jnp.exp(-jnp.abs(x)))


def _rms_fwd(x, g):
    rstd = lax.rsqrt(jnp.mean(x * x, axis=-1, keepdims=True) + RMS_EPS)
    xhat = x * rstd
    return xhat * g, xhat, rstd


def _rms_bwd(dy, xhat, rstd, g):
    dxhat = dy * g
    dx = rstd * (dxhat - xhat * jnp.mean(dxhat * xhat, axis=-1, keepdims=True))
    return dx, dy * xhat


def _resident(shape):
    return pl.BlockSpec(shape, lambda *_: (0,) * len(shape), pipeline_mode=pl.Buffered(1))


def _mm(a, b, mode, name, out_dtype=F32, residual=None, tm_cap=512, tn_cap=1024, tk_cap=1024):
    if mode == "nn":
        (M, K), (_, N) = a.shape, b.shape
    elif mode == "nt":
        (M, K), (N, _) = a.shape, b.shape
    else:
        (K, M), (_, N) = a.shape, b.shape
    tm, tn, tk = _tile(M, tm_cap), _tile(N, tn_cap), _tile(K, tk_cap)
    nk = K // tk
    if mode == "tn":
        a_spec = pl.BlockSpec((tk, tm), lambda i, j, k: (k, i))
    else:
        a_spec = pl.BlockSpec((tm, tk), lambda i, j, k: (i, k))
    if mode == "nt":
        b_spec = pl.BlockSpec((tn, tk), lambda i, j, k: (j, k))
    else:
        b_spec = pl.BlockSpec((tk, tn), lambda i, j, k: (k, j))
    o_spec = pl.BlockSpec((tm, tn), lambda i, j, k: (i, j))
    has_res = residual is not None

    def prod(a_ref, b_ref):
        av = a_ref[...].astype(BF16)
        bv = b_ref[...].astype(BF16)
        if mode == "nn":
            return _dot(av, bv)
        if mode == "nt":
            return _dot_nt(av, bv)
        return _dot_tn(av, bv)

    def body(*refs):
        a_ref, b_ref = refs[:2]
        r_ref = refs[2] if has_res else None
        o_ref = refs[2 + has_res]
        if nk == 1:
            o = prod(a_ref, b_ref)
            if has_res:
                o = r_ref[...] + o
            o_ref[...] = o.astype(out_dtype)
            return
        acc = refs[3 + has_res]
        k = pl.program_id(2)

        @pl.when(k == 0)
        def _():
            acc[...] = jnp.zeros_like(acc)

        acc[...] += prod(a_ref, b_ref)

        @pl.when(k == nk - 1)
        def _():
            o = acc[...]
            if has_res:
                o = r_ref[...] + o
            o_ref[...] = o.astype(out_dtype)

    ins = [a, b] + ([residual] if has_res else [])
    in_specs = [a_spec, b_spec] + ([o_spec] if has_res else [])
    return pl.pallas_call(
        body, name=name, grid=(M // tm, N // tn, nk),
        in_specs=in_specs, out_specs=o_spec,
        out_shape=jax.ShapeDtypeStruct((M, N), out_dtype),
        scratch_shapes=[pltpu.VMEM((tm, tn), F32)] if nk > 1 else [],
        compiler_params=_cp(("parallel", "parallel", "arbitrary")),
    )(*ins)


def _ffn_fwd(x, g, wgt, wut, wd, name):
    T, D = x.shape
    F = wgt.shape[0]
    tm = _tile(T, 512)

    def body(x_ref, g_ref, wg_ref, wu_ref, wd_ref, out_ref, gate_ref, up_ref):
        xv = x_ref[...]
        xn = _rms_fwd(xv, g_ref[...])[0].astype(BF16)
        gate = _dot_nt(xn, wg_ref[...])
        up = _dot_nt(xn, wu_ref[...])
        gate_ref[...] = gate.astype(BF16)
        up_ref[...] = up.astype(BF16)
        act = (_silu(gate) * up).astype(BF16)
        out_ref[...] = xv + 0.5 * _dot(act, wd_ref[...])

    row = lambda w: pl.BlockSpec((tm, w), lambda i: (i, 0))
    return pl.pallas_call(
        body, name=name, grid=(T // tm,),
        in_specs=[row(D), _resident((1, D)), _resident((F, D)), _resident((F, D)), _resident((F, D))],
        out_specs=[row(D), row(F), row(F)],
        out_shape=[jax.ShapeDtypeStruct((T, D), F32),
                   jax.ShapeDtypeStruct((T, F), BF16),
                   jax.ShapeDtypeStruct((T, F), BF16)],
        compiler_params=_cp(("parallel",)),
    )(x, g, wgt, wut, wd)


def _ffn_bwd_dx(dout, x, g, gate, up, wg, wu, wd, name):
    T, D = x.shape
    F = wg.shape[0]
    tm = _tile(T, 256)

    def body(dout_ref, x_ref, g_ref, gate_ref, up_ref, wg_ref, wu_ref, wd_ref,
             dx_ref, dgt_ref, dut_ref, actt_ref, xn_ref, dacc_ref, dg_ref):
        @pl.when(pl.program_id(0) == 0)
        def _():
            dg_ref[...] = jnp.zeros_like(dg_ref)

        gv = g_ref[...]
        dout_v = dout_ref[...]
        xn, xhat, rstd = _rms_fwd(x_ref[...], gv)
        xn_ref[...] = xn.astype(BF16)
        dacc = (0.5 * dout_v).astype(BF16)
        dacc_ref[...] = dacc
        dact = _dot_nt(dacc, wd_ref[...])
        gt = gate_ref[...].astype(F32)
        u = up_ref[...].astype(F32)
        sig = _sigmoid(gt)
        sl = gt * sig
        dgate = (dact * u * (sig * (1.0 + gt * (1.0 - sig)))).astype(BF16)
        dup = (dact * sl).astype(BF16)
        dgt_ref[...] = dgate.T
        dut_ref[...] = dup.T
        actt_ref[...] = (sl * u).astype(BF16).T
        dxn = _dot(dgate, wg_ref[...]) + _dot(dup, wu_ref[...])
        dx, dgrow = _rms_bwd(dxn, xhat, rstd, gv)
        dx_ref[...] = dout_v + dx
        dg_ref[...] += jnp.sum(dgrow, axis=0, keepdims=True)

    row = lambda w: pl.BlockSpec((tm, w), lambda i: (i, 0))
    tr = pl.BlockSpec((F, tm), lambda i: (0, i))
    return pl.pallas_call(
        body, name=name, grid=(T // tm,),
        in_specs=[row(D), row(D), _resident((1, D)), row(F), row(F),
                  _resident((F, D)), _resident((F, D)), _resident((F, D))],
        out_specs=[row(D), tr, tr, tr, row(D), row(D), pl.BlockSpec((1, D), lambda i: (0, 0))],
        out_shape=[jax.ShapeDtypeStruct((T, D), F32)] + [jax.ShapeDtypeStruct((F, T), BF16)] * 3
        + [jax.ShapeDtypeStruct((T, D), BF16)] * 2 + [jax.ShapeDtypeStruct((1, D), F32)],
        compiler_params=_cp(("arbitrary",)),
    )(dout, x, g, gate, up, wg, wu, wd)


def _ffn_dw(dgt, dut, actt, xn, dacc, name):
    F, T = dgt.shape
    D = xn.shape[1]
    th = _tile(F, 256)

    def body(dg_ref, du_ref, a_ref, xn_ref, dacc_ref, dwg_ref, dwu_ref, dwd_ref):
        xv = xn_ref[...]
        dwg_ref[...] = _dot(dg_ref[...], xv).astype(BF16)
        dwu_ref[...] = _dot(du_ref[...], xv).astype(BF16)
        dwd_ref[...] = _dot(a_ref[...], dacc_ref[...]).astype(BF16)

    tile = pl.BlockSpec((th, T), lambda j: (j, 0))
    out = pl.BlockSpec((th, D), lambda j: (j, 0))
    return pl.pallas_call(
        body, name=name, grid=(F // th,),
        in_specs=[tile, tile, tile, _resident((T, D)), _resident((T, D))],
        out_specs=[out, out, out], out_shape=[jax.ShapeDtypeStruct((F, D), BF16)] * 3,
        compiler_params=_cp(("parallel",)),
    )(dgt, dut, actt, xn, dacc)


def _norm_mm(x, g, w, name):
    T, D = x.shape
    N = w.shape[1]
    tm = _tile(T, 512)

    def body(x_ref, g_ref, w_ref, o_ref, ht_ref):
        xn = _rms_fwd(x_ref[...], g_ref[...])[0]
        ht_ref[...] = xn.T.astype(BF16)
        o_ref[...] = _dot(xn.astype(BF16), w_ref[...])

    return pl.pallas_call(
        body, name=name, grid=(T // tm,),
        in_specs=[pl.BlockSpec((tm, D), lambda i: (i, 0)), _resident((1, D)), _resident((D, N))],
        out_specs=[pl.BlockSpec((tm, N), lambda i: (i, 0)), pl.BlockSpec((D, tm), lambda i: (0, i))],
        out_shape=[jax.ShapeDtypeStruct((T, N), F32), jax.ShapeDtypeStruct((D, T), BF16)],
        compiler_params=_cp(("parallel",)),
    )(x, g, w)


def _norm_mm_bwd(dproj, x, g, w, dres, name):
    T, D = x.shape
    N = w.shape[1]
    tm = _tile(T, 512)

    def body(dp_ref, x_ref, g_ref, w_ref, dres_ref, dx_ref, dg_ref):
        @pl.when(pl.program_id(0) == 0)
        def _():
            dg_ref[...] = jnp.zeros_like(dg_ref)

        gv = g_ref[...]
        dh = _dot_nt(dp_ref[...], w_ref[...])
        _, xhat, rstd = _rms_fwd(x_ref[...], gv)
        dx, dgrow = _rms_bwd(dh, xhat, rstd, gv)
        dx_ref[...] = dres_ref[...] + dx
        dg_ref[...] += jnp.sum(dgrow, axis=0, keepdims=True)

    row = pl.BlockSpec((tm, D), lambda i: (i, 0))
    one = pl.BlockSpec((1, D), lambda i: (0, 0))
    return pl.pallas_call(
        body, name=name, grid=(T // tm,),
        in_specs=[pl.BlockSpec((tm, N), lambda i: (i, 0)), row, _resident((1, D)), _resident((D, N)), row],
        out_specs=[row, one],
        out_shape=[jax.ShapeDtypeStruct((T, D), F32), jax.ShapeDtypeStruct((1, D), F32)],
        compiler_params=_cp(("arbitrary",)),
    )(dproj, x, g, w, dres)


def _mm_resident_lhs(at, b, name, tn_cap=512):
    M, K = at.shape
    N = b.shape[1]
    tn = _tile(N, tn_cap)

    def body(a_ref, b_ref, o_ref):
        o_ref[...] = _dot(a_ref[...], b_ref[...]).astype(BF16)

    return pl.pallas_call(
        body, name=name, grid=(N // tn,),
        in_specs=[_resident((M, K)), pl.BlockSpec((K, tn), lambda j: (0, j))],
        out_specs=pl.BlockSpec((M, tn), lambda j: (0, j)),
        out_shape=jax.ShapeDtypeStruct((M, N), BF16),
        compiler_params=_cp(("parallel",)),
    )(at, b)


def _final_loss(x, g, target, name):
    T, D = x.shape
    tm = _tile(T, 512)

    def body(x_ref, g_ref, t_ref, loss_ref, dx_ref, dg_ref):
        @pl.when(pl.program_id(0) == 0)
        def _():
            dg_ref[...] = jnp.zeros_like(dg_ref)
            loss_ref[...] = jnp.zeros_like(loss_ref)

        gv = g_ref[...]
        y, xhat, rstd = _rms_fwd(x_ref[...], gv)
        err = y - t_ref[...]
        part = 0.5 * jnp.sum(jnp.mean(err * err, axis=-1, keepdims=True), axis=0, keepdims=True)
        loss_ref[...] += jnp.broadcast_to(part, loss_ref.shape)
        dy = err * (1.0 / D)
        dx, dgrow = _rms_bwd(dy, xhat, rstd, gv)
        dx_ref[...] = dx
        dg_ref[...] += jnp.sum(dgrow, axis=0, keepdims=True)

    row = pl.BlockSpec((tm, D), lambda i: (i, 0))
    one = pl.BlockSpec((1, D), lambda i: (0, 0))
    return pl.pallas_call(
        body, name=name, grid=(T // tm,),
        in_specs=[row, one, row],
        out_specs=[pl.BlockSpec((1, LANE), lambda i: (0, 0)), row, one],
        out_shape=[jax.ShapeDtypeStruct((1, LANE), F32), jax.ShapeDtypeStruct((T, D), F32),
                   jax.ShapeDtypeStruct((1, D), F32)],
        compiler_params=_cp(("arbitrary",)),
    )(x, g, target)


def _attn_bias(S, bq):
    d = jnp.arange(bq)[:, None] - jnp.arange(S)[None, :] + (S // bq - 1) * bq
    ok = d >= 0
    mult = ((ok & (d <= 128)).astype(F32) + (ok & (d % 4 == 0) & (d <= 512)).astype(F32)
            + (ok & (d % 16 == 0) & (d <= 2048)).astype(F32))
    return jnp.where(mult > 0, jnp.log(jnp.maximum(mult, 1.0)), NEG).astype(F32)


def _attn_fwd(proj, bias, B, S, name):
    T = B * S
    bq = bias.shape[0]
    nb = S // bq
    qcol, kcol, vcol = Q0 // LANE, K0 // LANE, V0 // LANE

    def body(q_ref, k_ref, v_ref, t_ref, o_ref, lse_ref, ks, vs):
        for hh in range(2):
            sl = slice(HEAD_DIM * hh, HEAD_DIM * (hh + 1))
            ks[hh] = k_ref[:, sl].astype(BF16)
            vs[hh] = v_ref[:, sl].astype(BF16)
        for hh in range(2):
            sl = slice(HEAD_DIM * hh, HEAD_DIM * (hh + 1))
            for qb in range(nb):
                w, off, rows = bq * (qb + 1), (nb - 1 - qb) * bq, slice(qb * bq, (qb + 1) * bq)
                q = (q_ref[rows, sl] * 0.125).astype(BF16)
                s = _dot_nt(q, ks[hh, 0:w, :]) + t_ref[:, off:off + w]
                m = jnp.max(s, axis=-1, keepdims=True)
                p = jnp.exp(s - m)
                l = jnp.sum(p, axis=-1, keepdims=True)
                o_ref[rows, sl] = _dot(p.astype(BF16), vs[hh, 0:w, :]) / l
                lse_ref[rows, hh:hh + 1] = m + jnp.log(l)

    blk = lambda c0: pl.BlockSpec((S, LANE), lambda b, p: (b, c0 + p))
    return pl.pallas_call(
        body, name=name, grid=(B, ATT_W // LANE),
        in_specs=[blk(qcol), blk(kcol), blk(vcol), _resident((bq, S))],
        out_specs=[pl.BlockSpec((S, LANE), lambda b, p: (b, p)),
                   pl.BlockSpec((None, None, S, 2), lambda b, p: (b, p, 0, 0))],
        out_shape=[jax.ShapeDtypeStruct((T, ATT_W), F32),
                   jax.ShapeDtypeStruct((B, ATT_W // LANE, S, 2), F32)],
        scratch_shapes=[pltpu.VMEM((2, S, HEAD_DIM), BF16)] * 2,
        compiler_params=_cp(("parallel", "parallel")),
    )(proj, proj, proj, bias)


def _attn_bwd(proj, o, lse, dy, bias, B, S, name):
    T = B * S
    bq = bias.shape[0]
    nb = S // bq
    qcol, kcol, vcol = Q0 // LANE, K0 // LANE, V0 // LANE

    def body(q_ref, k_ref, v_ref, o_ref, lse_ref, do_ref, t_ref, dq_ref, dk_ref, dv_ref, ks, vs, dks, dvs):
        for hh in range(2):
            sl = slice(HEAD_DIM * hh, HEAD_DIM * (hh + 1))
            ks[hh] = k_ref[:, sl].astype(BF16)
            vs[hh] = v_ref[:, sl].astype(BF16)
        dks[...] = jnp.zeros_like(dks)
        dvs[...] = jnp.zeros_like(dvs)
        for hh in range(2):
            sl = slice(HEAD_DIM * hh, HEAD_DIM * (hh + 1))
            for qb in range(nb):
                w, off, rows = bq * (qb + 1), (nb - 1 - qb) * bq, slice(qb * bq, (qb + 1) * bq)
                q = (q_ref[rows, sl] * 0.125).astype(BF16)
                do = do_ref[rows, sl]
                dob = do.astype(BF16)
                delta = jnp.sum(do * o_ref[rows, sl], axis=-1, keepdims=True)
                k, v = ks[hh, 0:w, :], vs[hh, 0:w, :]
                s = _dot_nt(q, k) + t_ref[:, off:off + w]
                p = jnp.exp(s - lse_ref[rows, hh:hh + 1])
                ds = (p * (_dot_nt(dob, v) - delta)).astype(BF16)
                dq_ref[rows, sl] = (_dot(ds, k) * 0.125).astype(dq_ref.dtype)
                dks[hh, 0:w, :] += _dot_tn(ds, q)
                dvs[hh, 0:w, :] += _dot_tn(p.astype(BF16), dob)
            dk_ref[:, sl] = dks[hh].astype(dk_ref.dtype)
            dv_ref[:, sl] = dvs[hh].astype(dv_ref.dtype)

    blk = lambda c0: pl.BlockSpec((S, LANE), lambda b, p: (b, c0 + p))
    own = pl.BlockSpec((S, LANE), lambda b, p: (b, p))
    return pl.pallas_call(
        body, name=name, grid=(B, ATT_W // LANE),
        in_specs=[blk(qcol), blk(kcol), blk(vcol), own,
                  pl.BlockSpec((None, None, S, 2), lambda b, p: (b, p, 0, 0)), own, _resident((bq, S))],
        out_specs=[own, own, own],
        out_shape=[jax.ShapeDtypeStruct((T, ATT_W), BF16)] * 3,
        scratch_shapes=[pltpu.VMEM((2, S, HEAD_DIM), BF16)] * 2 + [pltpu.VMEM((2, S, HEAD_DIM), F32)] * 2,
        compiler_params=_cp(("parallel", "parallel")),
    )(proj, proj, proj, o, lse, dy, bias)


def _conv_fwd(proj, cw, cb, B, S, name):
    T = B * S
    nc = SSD_CONV_DIM // LANE
    c0 = XBC0 // LANE

    def body(x_ref, w_ref, b_ref, o_ref):
        x = x_ref[...]
        t = lax.broadcasted_iota(jnp.int32, (S, 1), 0)
        acc = b_ref[...] + w_ref[SSD_CONV - 1:SSD_CONV, :] * x
        for k in range(SSD_CONV - 1):
            sh = SSD_CONV - 1 - k
            xs = jnp.where(t >= sh, pltpu.roll(x, sh, 0), 0.0)
            acc = acc + w_ref[k:k + 1, :] * xs
        o_ref[...] = acc

    return pl.pallas_call(
        body, name=name, grid=(B, nc),
        in_specs=[pl.BlockSpec((S, LANE), lambda b, j: (b, c0 + j)),
                  pl.BlockSpec((SUBLANE, LANE), lambda b, j: (0, j)),
                  pl.BlockSpec((1, LANE), lambda b, j: (0, j))],
        out_specs=pl.BlockSpec((S, LANE), lambda b, j: (b, j)),
        out_shape=jax.ShapeDtypeStruct((T, SSD_CONV_DIM), F32),
        compiler_params=_cp(("parallel", "parallel")),
    )(proj, cw, cb)


def _conv_bwd(dpre, proj, cw, B, S, name):
    T = B * S
    nc = SSD_CONV_DIM // LANE
    c0 = XBC0 // LANE

    def body(d_ref, x_ref, w_ref, dx_ref, dwb_ref):
        @pl.when(pl.program_id(1) == 0)
        def _():
            dwb_ref[...] = jnp.zeros_like(dwb_ref)

        d = d_ref[...]
        x = x_ref[...]
        t = lax.broadcasted_iota(jnp.int32, (S, 1), 0)
        dx = w_ref[SSD_CONV - 1:SSD_CONV, :] * d
        rows = [None] * SUBLANE
        rows[SSD_CONV - 1] = jnp.sum(d * x, axis=0, keepdims=True)
        for k in range(SSD_CONV - 1):
            sh = SSD_CONV - 1 - k
            dx = dx + w_ref[k:k + 1, :] * jnp.where(t < S - sh, pltpu.roll(d, S - sh, 0), 0.0)
            xs = jnp.where(t >= sh, pltpu.roll(x, sh, 0), 0.0)
            rows[k] = jnp.sum(d * xs, axis=0, keepdims=True)
        rows[SSD_CONV] = jnp.sum(d, axis=0, keepdims=True)
        dx_ref[...] = dx.astype(BF16)
        r = lax.broadcasted_iota(jnp.int32, (SUBLANE, LANE), 0)
        upd = jnp.zeros((SUBLANE, LANE), F32)
        for k in range(SSD_CONV + 1):
            upd = upd + jnp.where(r == k, rows[k], 0.0)
        dwb_ref[...] += upd

    return pl.pallas_call(
        body, name=name, grid=(nc, B),
        in_specs=[pl.BlockSpec((S, LANE), lambda j, b: (b, j)),
                  pl.BlockSpec((S, LANE), lambda j, b: (b, c0 + j)),
                  pl.BlockSpec((SUBLANE, LANE), lambda j, b: (0, j))],
        out_specs=[pl.BlockSpec((S, LANE), lambda j, b: (b, j)),
                   pl.BlockSpec((SUBLANE, LANE), lambda j, b: (0, j))],
        out_shape=[jax.ShapeDtypeStruct((T, SSD_CONV_DIM), BF16),
                   jax.ShapeDtypeStruct((SUBLANE, SSD_CONV_DIM), F32)],
        compiler_params=_cp(("parallel", "arbitrary")),
    )(dpre, proj, cw)


def _ssd_consts():
    e = np.zeros((LANE, SSD_W), np.float32)
    p = np.zeros((SUBLANE, SSD_W), np.float32)
    for h in range(SSD_HEADS):
        e[h, HEAD_DIM * h:HEAD_DIM * (h + 1)] = 1.0
        p[h, HEAD_DIM * h] = 1.0
    return jnp.asarray(e), jnp.asarray(p)


def _ssd_chunk(pre, z, dtr, sprev, par, e_mat, psel):
    L = CHUNK
    xc = _silu(pre)
    xs, bm, cm = xc[:, :SSD_W], xc[:, SSD_W:SSD_W + 2 * SSD_STATE], xc[:, SSD_W + 2 * SSD_STATE:]
    dtb, alog, dskip, ng = par[0:1], par[1:2], par[2:3], par[3:4]
    dt = _softplus(_dot(dtr, e_mat, HI) + dtb)
    a = dt * (-jnp.exp(alog))
    X = xs * dt
    ri = lax.broadcasted_iota(jnp.int32, (L, L), 0)
    ci = lax.broadcasted_iota(jnp.int32, (L, L), 1)
    tril = ri >= ci
    acs = _dot(tril.astype(F32), a, HI)
    acs_t = _dot_nt(psel, acs, HI)
    ecs = jnp.exp(acs)
    alast = acs[L - 1:L, :]
    xd = (X * jnp.exp(alast - acs)).astype(BF16)
    xb = X.astype(BF16)
    col = lax.broadcasted_iota(jnp.int32, (1, SSD_W), 1)
    sb = sprev.astype(BF16)
    bgs = [bm[:, SSD_STATE * g:SSD_STATE * (g + 1)].astype(BF16) for g in range(2)]
    cgs = [cm[:, SSD_STATE * g:SSD_STATE * (g + 1)].astype(BF16) for g in range(2)]
    cbs = [_dot_nt(cgs[g], bgs[g]) for g in range(2)]
    first = lax.broadcasted_iota(jnp.int32, (1, LANE), 1) < HEAD_DIM
    y_tiles, s_tiles = [], []
    for t in range(SSD_W // LANE):
        cl = slice(LANE * t, LANE * (t + 1))
        xb_t, xd_t, sb_t = xb[:, cl], xd[:, cl], sb[:, cl]
        per_head = []
        for h in (2 * t, 2 * t + 1):
            seg = acs[:, HEAD_DIM * h:HEAD_DIM * h + 1] - acs_t[h:h + 1, :]
            dec = jnp.exp(jnp.where(tril, seg, NEG))
            per_head.append(_dot((cbs[h // 3] * dec).astype(BF16), xb_t))
        y_t = jnp.where(first, per_head[0], per_head[1])
        ga, gb = (2 * t) // 3, (2 * t + 1) // 3
        if ga == gb:
            y_off, s_add = _dot(cgs[ga], sb_t), _dot_tn(bgs[ga], xd_t)
        else:
            y_off = jnp.where(first, _dot(cgs[ga], sb_t), _dot(cgs[gb], sb_t))
            s_add = jnp.where(first, _dot_tn(bgs[ga], xd_t), _dot_tn(bgs[gb], xd_t))
        y_tiles.append(y_t + y_off * ecs[:, cl])
        s_tiles.append(s_add)
    y = dskip * xs + jnp.concatenate(y_tiles, axis=1)
    snew = sprev * jnp.exp(alast) + jnp.concatenate(s_tiles, axis=1)
    yg = y * _silu(z)
    sq = yg * yg
    g0 = col < SSD_W // 2
    ms0 = jnp.sum(jnp.where(g0, sq, 0.0), axis=-1, keepdims=True) * (2.0 / SSD_W)
    ms1 = jnp.sum(jnp.where(g0, 0.0, sq), axis=-1, keepdims=True) * (2.0 / SSD_W)
    r = jnp.where(g0, lax.rsqrt(ms0 + RMS_EPS), lax.rsqrt(ms1 + RMS_EPS))
    return yg * r * ng, snew


SSD_CHUNKS_PER_STEP = 2


def _ssd_chunks_per_step(S):
    k = SSD_CHUNKS_PER_STEP
    while (S // CHUNK) % k:
        k //= 2
    return k


def _ssd_fwd(pre, proj, par, B, S, name):
    T = B * S
    k = _ssd_chunks_per_step(S)
    nc, rows = S // (CHUNK * k), CHUNK * k
    e_mat, psel = _ssd_consts()

    def body(pre_ref, z_ref, dt_ref, par_ref, e_ref, p_ref, y_ref, sall_ref, st):
        @pl.when(pl.program_id(1) == 0)
        def _():
            st[...] = jnp.zeros_like(st)

        sprev = st[...]
        for i in range(k):
            r = slice(CHUNK * i, CHUNK * (i + 1))
            sall_ref[i] = sprev
            y, sprev = _ssd_chunk(pre_ref[r, :], z_ref[r, :], dt_ref[r, :], sprev, par_ref[...], e_ref[...],
                                  p_ref[...])
            y_ref[r, :] = y.astype(BF16)
        st[...] = sprev

    row = lambda b, c: b * nc + c
    full = lambda shp: pl.BlockSpec(shp, lambda b, c: (0, 0))
    return pl.pallas_call(
        body, name=name, grid=(B, nc),
        in_specs=[pl.BlockSpec((rows, SSD_CONV_DIM), lambda b, c: (row(b, c), 0)),
                  pl.BlockSpec((rows, SSD_W), lambda b, c: (row(b, c), Z0 // SSD_W)),
                  pl.BlockSpec((rows, LANE), lambda b, c: (row(b, c), DT0 // LANE)),
                  full((SUBLANE, SSD_W)), full((LANE, SSD_W)), full((SUBLANE, SSD_W))],
        out_specs=[pl.BlockSpec((rows, SSD_W), lambda b, c: (row(b, c), 0)),
                   pl.BlockSpec((k, SSD_STATE, SSD_W), lambda b, c: (row(b, c), 0, 0))],
        out_shape=[jax.ShapeDtypeStruct((T, SSD_W), BF16),
                   jax.ShapeDtypeStruct((B * nc * k, SSD_STATE, SSD_W), F32)],
        scratch_shapes=[pltpu.VMEM((SSD_STATE, SSD_W), F32)],
        compiler_params=_cp(("parallel", "arbitrary")),
    )(pre, proj, proj, par, e_mat, psel)


def _ssd_bwd(pre, proj, sall, dy, par, B, S, name):
    T = B * S
    k = _ssd_chunks_per_step(S)
    nc, rows = S // (CHUNK * k), CHUNK * k
    e_mat, psel = _ssd_consts()

    def body(pre_ref, z_ref, dt_ref, sall_ref, dy_ref, par_ref, e_ref, p_ref,
             dpre_ref, dz_ref, ddt_ref, dpar_ref, ds):
        b, c = pl.program_id(0), pl.program_id(1)

        @pl.when(c == 0)
        def _():
            ds[...] = jnp.zeros_like(ds)

        @pl.when((b == 0) & (c == 0))
        def _():
            dpar_ref[...] = jnp.zeros_like(dpar_ref)

        e_v, p_v = e_ref[...], p_ref[...]
        fn = lambda pre, z, dtr, sprev, par: _ssd_chunk(pre, z, dtr, sprev, par, e_v, p_v)
        dstate, dpar_sum = ds[...], None
        for i in reversed(range(k)):
            r = slice(CHUNK * i, CHUNK * (i + 1))
            _, vjp = jax.vjp(fn, pre_ref[r, :], z_ref[r, :], dt_ref[r, :], sall_ref[i], par_ref[...])
            dpre, dz, ddt, dstate, dpar = vjp((dy_ref[r, :], dstate))
            dpre_ref[r, :] = dpre
            dz_ref[r, :] = dz.astype(BF16)
            ddt_ref[r, :] = ddt.astype(BF16)
            dpar_sum = dpar if dpar_sum is None else dpar_sum + dpar
        dpar_ref[...] += dpar_sum
        ds[...] = dstate

    row = lambda b, c: b * nc + (nc - 1 - c)
    full = lambda shp: pl.BlockSpec(shp, lambda b, c: (0, 0))
    return pl.pallas_call(
        body, name=name, grid=(B, nc),
        in_specs=[pl.BlockSpec((rows, SSD_CONV_DIM), lambda b, c: (row(b, c), 0)),
                  pl.BlockSpec((rows, SSD_W), lambda b, c: (row(b, c), Z0 // SSD_W)),
                  pl.BlockSpec((rows, LANE), lambda b, c: (row(b, c), DT0 // LANE)),
                  pl.BlockSpec((k, SSD_STATE, SSD_W), lambda b, c: (row(b, c), 0, 0)),
                  pl.BlockSpec((rows, SSD_W), lambda b, c: (row(b, c), ATT_W // SSD_W)),
                  full((SUBLANE, SSD_W)), full((LANE, SSD_W)), full((SUBLANE, SSD_W))],
        out_specs=[pl.BlockSpec((rows, SSD_CONV_DIM), lambda b, c: (row(b, c), 0)),
                   pl.BlockSpec((rows, SSD_W), lambda b, c: (row(b, c), 0)),
                   pl.BlockSpec((rows, LANE), lambda b, c: (row(b, c), 0)),
                   full((SUBLANE, SSD_W))],
        out_shape=[jax.ShapeDtypeStruct((T, SSD_CONV_DIM), F32),
                   jax.ShapeDtypeStruct((T, SSD_W), BF16),
                   jax.ShapeDtypeStruct((T, LANE), BF16),
                   jax.ShapeDtypeStruct((SUBLANE, SSD_W), F32)],
        scratch_shapes=[pltpu.VMEM((SSD_STATE, SSD_W), F32)],
        compiler_params=_cp(("arbitrary", "arbitrary")),
    )(pre, proj, proj, sall, dy, par, e_mat, psel)


def _sgu_consts():
    e = np.zeros((SUBLANE, SGU_W), np.float32)
    for g in range(SGU_GROUPS):
        e[g, HEAD_DIM * g:HEAD_DIM * (g + 1)] = 1.0
    return jnp.asarray(e)


def _sgu_chunk(u_raw, v_raw, ln, w, bst, e4):
    L = CHUNK
    u = _gelu(u_raw)
    v = _gelu(v_raw)
    mu = jnp.mean(v, axis=-1, keepdims=True)
    vc = v - mu
    var = jnp.mean(vc * vc, axis=-1, keepdims=True)
    vn = vc * lax.rsqrt(var + LN_EPS) * ln[0:1] + ln[1:2]
    vb = vn.astype(BF16)
    ri = lax.broadcasted_iota(jnp.int32, (L, L), 0)
    ci = lax.broadcasted_iota(jnp.int32, (L, L), 1)
    tril = ri >= ci
    col = lax.broadcasted_iota(jnp.int32, (1, SGU_W), 1)
    mixed = _dot(bst, e4, HI)
    for g in range(SGU_GROUPS):
        wc = jnp.where(tril, w[g], 0.0).astype(BF16)
        gm = (col >= HEAD_DIM * g) & (col < HEAD_DIM * (g + 1))
        mixed = mixed + jnp.where(gm, _dot(wc, vb), 0.0)
    return u * mixed


def _sgu_fwd(proj, ln, w, bst, B, S, name):
    T = B * S
    nc = S // CHUNK
    e4 = _sgu_consts()

    def body(u_ref, v_ref, ln_ref, w_ref, b_ref, e_ref, y_ref):
        y_ref[...] = _sgu_chunk(u_ref[...], v_ref[...], ln_ref[...], w_ref[...], b_ref[...], e_ref[...]).astype(BF16)

    return pl.pallas_call(
        body, name=name, grid=(T // CHUNK,),
        in_specs=[pl.BlockSpec((CHUNK, SGU_W), lambda i: (i, U0 // SGU_W)),
                  pl.BlockSpec((CHUNK, SGU_W), lambda i: (i, VS0 // SGU_W)),
                  pl.BlockSpec((SUBLANE, SGU_W), lambda i: (0, 0)),
                  pl.BlockSpec((SGU_GROUPS, CHUNK, CHUNK), lambda i: (0, 0, 0)),
                  pl.BlockSpec((CHUNK, SUBLANE), lambda i: (0, 0)),
                  pl.BlockSpec((SUBLANE, SGU_W), lambda i: (0, 0))],
        out_specs=pl.BlockSpec((CHUNK, SGU_W), lambda i: (i, 0)),
        out_shape=jax.ShapeDtypeStruct((T, SGU_W), BF16),
        compiler_params=_cp(("parallel",)),
    )(proj, proj, ln, w, bst, e4)


def _sgu_bwd(proj, dy, ln, w, bst, B, S, name):
    T = B * S
    e4 = _sgu_consts()
    ycol = (ATT_W + SSD_W) // SGU_W

    def body(u_ref, v_ref, dy_ref, ln_ref, w_ref, b_ref, e_ref, du_ref, dv_ref, dln_ref, dw_ref, db_ref):
        @pl.when(pl.program_id(0) == 0)
        def _():
            dln_ref[...] = jnp.zeros_like(dln_ref)
            dw_ref[...] = jnp.zeros_like(dw_ref)
            db_ref[...] = jnp.zeros_like(db_ref)

        e_v = e_ref[...]
        fn = lambda u, v, ln, w, b: _sgu_chunk(u, v, ln, w, b, e_v)
        _, vjp = jax.vjp(fn, u_ref[...], v_ref[...], ln_ref[...], w_ref[...], b_ref[...])
        du, dv, dln, dw, db = vjp(dy_ref[...])
        du_ref[...] = du.astype(BF16)
        dv_ref[...] = dv.astype(BF16)
        dln_ref[...] += dln
        dw_ref[...] += dw
        db_ref[...] += db

    c_ln = pl.BlockSpec((SUBLANE, SGU_W), lambda i: (0, 0))
    c_w = pl.BlockSpec((SGU_GROUPS, CHUNK, CHUNK), lambda i: (0, 0, 0))
    c_b = pl.BlockSpec((CHUNK, SUBLANE), lambda i: (0, 0))
    return pl.pallas_call(
        body, name=name, grid=(T // CHUNK,),
        in_specs=[pl.BlockSpec((CHUNK, SGU_W), lambda i: (i, U0 // SGU_W)),
                  pl.BlockSpec((CHUNK, SGU_W), lambda i: (i, VS0 // SGU_W)),
                  pl.BlockSpec((CHUNK, SGU_W), lambda i: (i, ycol)),
                  c_ln, c_w, c_b, pl.BlockSpec((SUBLANE, SGU_W), lambda i: (0, 0))],
        out_specs=[pl.BlockSpec((CHUNK, SGU_W), lambda i: (i, 0)),
                   pl.BlockSpec((CHUNK, SGU_W), lambda i: (i, 0)), c_ln, c_w, c_b],
        out_shape=[jax.ShapeDtypeStruct((T, SGU_W), BF16), jax.ShapeDtypeStruct((T, SGU_W), BF16),
                   jax.ShapeDtypeStruct((SUBLANE, SGU_W), F32),
                   jax.ShapeDtypeStruct((SGU_GROUPS, CHUNK, CHUNK), F32),
                   jax.ShapeDtypeStruct((CHUNK, SUBLANE), F32)],
        compiler_params=_cp(("arbitrary",)),
    )(proj, proj, dy, ln, w, bst, e4)


_HBM = pl.BlockSpec(memory_space=pltpu.HBM)
_SEM = pl.BlockSpec(memory_space=pltpu.SEMAPHORE)
_ANY = pl.BlockSpec(memory_space=pl.ANY)
_EFFECT = pltpu.SideEffectType.DATAFLOW_SIDE_EFFECTING


def _peers():
    x, y, c = lax.axis_index("x"), lax.axis_index("y"), lax.axis_index("c")
    out = []
    for p in range(1, N_DEV):
        px, py, pc = x ^ ((p >> 2) & 1), y ^ ((p >> 1) & 1), c ^ (p & 1)
        out.append(((px, py, pc), 4 * px + 2 * py + pc))
    return 4 * x + 2 * y + c, out


def _xchg_start(xs, a2a, order, name):
    n = len(xs)
    lands = [lax.empty(a.shape if f else (N_DEV,) + a.shape, a.dtype) for a, f in zip(xs, a2a)]

    def body(*refs):
        ins, zones = refs[:n], refs[n:2 * n]
        send_sems, recv_sems = refs[2 * n + 1], refs[2 * n + 2]
        token = refs[-1]
        me, peers = _peers()
        for p, (dev, peer) in enumerate(peers):
            for t in range(n):
                pltpu.make_async_remote_copy(
                    src_ref=ins[t].at[peer] if a2a[t] else ins[t], dst_ref=zones[t].at[me],
                    send_sem=send_sems.at[p * n + t], recv_sem=recv_sems.at[p * n + t],
                    device_id=dev, device_id_type=MESH).start()
        token[...] = jnp.zeros_like(token)

    hbm = lambda a: pltpu.HBM(a.shape, a.dtype)
    sems = pltpu.SemaphoreType.DMA(((N_DEV - 1) * n,))
    out = pl.pallas_call(
        body, name=name,
        in_specs=[_HBM] * (2 * n) + [_ANY],
        out_specs=[_SEM, _SEM] + [_HBM] * (2 * n) + [pl.BlockSpec(memory_space=pltpu.VMEM)],
        out_shape=[sems, sems] + [hbm(a) for a in xs] + [hbm(a) for a in lands]
        + [jax.ShapeDtypeStruct((SUBLANE, LANE), F32)],
        input_output_aliases={t: 2 + t for t in range(2 * n)},
        compiler_params=pltpu.CompilerParams(has_side_effects=_EFFECT),
    )(*[pltpu.with_memory_space_constraint(a, pltpu.HBM) for a in list(xs) + list(lands)], order)
    return out[0], out[1], out[2:2 + n], out[2 + n:2 + 2 * n], out[-1]


def _xchg_wait(started, a2a, after, name):
    send_sems, recv_sems, xs, lands, _ = started
    n = len(xs)

    def body(*refs):
        ins, zones = refs[:n], refs[n:2 * n]
        send_s, recv_s = refs[2 * n], refs[2 * n + 1]
        me, peers = _peers()
        cps = []
        for p, (dev, peer) in enumerate(peers):
            for t in range(n):
                cps.append(pltpu.make_async_remote_copy(
                    src_ref=ins[t].at[peer] if a2a[t] else ins[t], dst_ref=zones[t].at[peer],
                    send_sem=send_s.at[p * n + t], recv_sem=recv_s.at[p * n + t],
                    device_id=dev, device_id_type=MESH))
        for cp in cps:
            cp.wait_recv()
        for cp in cps:
            cp.wait_send()

    hbm = lambda a: pltpu.HBM(a.shape, a.dtype)
    out = pl.pallas_call(
        body, name=name,
        in_specs=[_HBM] * (2 * n) + [_SEM, _SEM, _ANY],
        out_specs=[_HBM] * (2 * n),
        out_shape=[hbm(a) for a in xs] + [hbm(a) for a in lands],
        input_output_aliases={t: t for t in range(2 * n)},
        compiler_params=pltpu.CompilerParams(has_side_effects=_EFFECT),
    )(*xs, *lands, send_sems, recv_sems, after)
    return out[:n], out[n:]


def _chip_peers():
    x, y, c = lax.axis_index("x"), lax.axis_index("y"), lax.axis_index("c")
    chips = [(1 - x, y), (x, 1 - y), (1 - x, 1 - y)]
    slot = lambda px, py, pc: 4 * px + 2 * py + pc
    return (x, y, c), chips, slot


def _gather_start(xs, order, name, own_slot):
    n = len(xs)
    lands = [lax.empty((N_DEV,) + a.shape, a.dtype) for a in xs]

    def body(*refs):
        ins, zones = refs[:n], refs[n:2 * n]
        send_sems, d2d_sems, ici_sems = refs[2 * n + 1:2 * n + 4]
        token = refs[-1]
        (x, y, c), chips, slot = _chip_peers()
        me = slot(x, y, c)
        for t in range(n):
            if own_slot:
                pltpu.make_async_copy(ins[t], zones[t].at[me], d2d_sems.at[n + t]).start()
            for j, (px, py) in enumerate(chips):
                pltpu.make_async_remote_copy(
                    src_ref=ins[t], dst_ref=zones[t].at[me], send_sem=send_sems.at[(1 + j) * n + t],
                    recv_sem=ici_sems.at[j * n + t], device_id=(px, py, c), device_id_type=MESH).start()
            pltpu.make_async_remote_copy(
                src_ref=ins[t], dst_ref=zones[t].at[me], send_sem=send_sems.at[t],
                recv_sem=d2d_sems.at[t], device_id=(x, y, 1 - c), device_id_type=MESH).start()
        token[...] = jnp.zeros_like(token)

    hbm = lambda a: pltpu.HBM(a.shape, a.dtype)
    dma = lambda k: pltpu.SemaphoreType.DMA((k,))
    out = pl.pallas_call(
        body, name=name,
        in_specs=[_HBM] * (2 * n) + [_ANY],
        out_specs=[_SEM, _SEM, _SEM] + [_HBM] * (2 * n) + [pl.BlockSpec(memory_space=pltpu.VMEM)],
        out_shape=[dma(4 * n), dma(2 * n), dma(3 * n)] + [hbm(a) for a in xs] + [hbm(a) for a in lands]
        + [jax.ShapeDtypeStruct((SUBLANE, LANE), F32)],
        input_output_aliases={t: 3 + t for t in range(2 * n)},
        compiler_params=pltpu.CompilerParams(has_side_effects=_EFFECT),
    )(*[pltpu.with_memory_space_constraint(a, pltpu.HBM) for a in list(xs) + list(lands)], order)
    return dict(send=out[0], d2d=out[1], ici=out[2], xs=out[3:3 + n], lands=out[3 + n:3 + 2 * n], token=out[-1],
                own_slot=own_slot)


def _gather_relay(st, after, name):
    n = len(st["xs"])

    def body(*refs):
        zones, ici_sems = refs[:n], refs[n]
        fsend, frecv = refs[n + 2], refs[n + 3]
        token = refs[-1]
        (x, y, c), chips, slot = _chip_peers()
        for t in range(n):
            for j, (px, py) in enumerate(chips):
                blk = zones[t].at[slot(px, py, c)]
                fwd = pltpu.make_async_remote_copy(
                    src_ref=blk, dst_ref=blk, send_sem=fsend.at[j * n + t], recv_sem=ici_sems.at[j * n + t],
                    device_id=(x, y, 1 - c), device_id_type=MESH)
                fwd.wait_recv()
                pltpu.make_async_remote_copy(
                    src_ref=blk, dst_ref=blk, send_sem=fsend.at[j * n + t], recv_sem=frecv.at[j * n + t],
                    device_id=(x, y, 1 - c), device_id_type=MESH).start()
        token[...] = jnp.zeros_like(token)

    hbm = lambda a: pltpu.HBM(a.shape, a.dtype)
    dma = lambda k: pltpu.SemaphoreType.DMA((k,))
    out = pl.pallas_call(
        body, name=name,
        in_specs=[_HBM] * n + [_SEM, _ANY],
        out_specs=[_SEM, _SEM] + [_HBM] * n + [pl.BlockSpec(memory_space=pltpu.VMEM)],
        out_shape=[dma(3 * n), dma(3 * n)] + [hbm(a) for a in st["lands"]]
        + [jax.ShapeDtypeStruct((SUBLANE, LANE), F32)],
        input_output_aliases={t: 2 + t for t in range(n)},
        compiler_params=pltpu.CompilerParams(has_side_effects=_EFFECT),
    )(*st["lands"], st["ici"], after)
    return dict(st, fsend=out[0], frecv=out[1], lands=out[2:2 + n], token=out[-1])


def _gather_wait(st, after, name):
    n = len(st["xs"])

    def body(*refs):
        ins, zones = refs[:n], refs[n:2 * n]
        send_sems, d2d_sems, fsend, frecv = refs[2 * n:2 * n + 4]
        (x, y, c), chips, slot = _chip_peers()
        sib = (x, y, 1 - c)
        for t in range(n):
            if st["own_slot"]:
                pltpu.make_async_copy(ins[t], zones[t].at[slot(x, y, c)], d2d_sems.at[n + t]).wait()
            mine = lambda s, r, dst: pltpu.make_async_remote_copy(
                src_ref=ins[t], dst_ref=dst, send_sem=s, recv_sem=r, device_id=sib, device_id_type=MESH)
            direct = mine(send_sems.at[t], d2d_sems.at[t], zones[t].at[slot(x, y, 1 - c)])
            direct.wait_recv()
            direct.wait_send()
            for j, (px, py) in enumerate(chips):
                mine(send_sems.at[(1 + j) * n + t], d2d_sems.at[t], zones[t].at[slot(px, py, c)]).wait_send()
                relayed = mine(fsend.at[j * n + t], frecv.at[j * n + t], zones[t].at[slot(px, py, 1 - c)])
                relayed.wait_recv()
                relayed.wait_send()

    hbm = lambda a: pltpu.HBM(a.shape, a.dtype)
    out = pl.pallas_call(
        body, name=name,
        in_specs=[_HBM] * (2 * n) + [_SEM] * 4 + [_ANY],
        out_specs=[_HBM] * (2 * n),
        out_shape=[hbm(a) for a in st["xs"]] + [hbm(a) for a in st["lands"]],
        input_output_aliases={t: t for t in range(2 * n)},
        compiler_params=pltpu.CompilerParams(has_side_effects=_EFFECT),
    )(*st["xs"], *st["lands"], st["send"], st["d2d"], st["fsend"], st["frecv"], after)
    return out[:n], out[n:]


def _cast_layers(pairs, name):
    def body(*refs):
        n = len(refs) // 2
        for i in range(n):
            refs[n + i][...] = refs[i][...].astype(BF16)

    in_specs = [pl.BlockSpec((None,) + w.shape[1:], functools.partial(lambda l, i: (l, 0, 0), l),
                             pipeline_mode=pl.Buffered(1)) for w, l in pairs]
    return pl.pallas_call(
        body, name=name, grid=(1,), in_specs=in_specs,
        out_specs=[pl.BlockSpec(w.shape[1:], lambda i: (0, 0)) for w, _ in pairs],
        out_shape=[jax.ShapeDtypeStruct(w.shape[1:], BF16) for w, _ in pairs],
        compiler_params=_cp(("arbitrary",)),
    )(*[w for w, _ in pairs])


def _adamw(me, w, m, v, parts, own, name, layer=0, into=None):
    L, R, C = w.shape
    P = parts.shape[0]
    tr = R
    t = 16
    while t <= R:
        if R % t == 0 and t * C <= 131072:
            tr = t
        t += 16
    if tr == R and R * C > 131072 and R % 16 == 0:
        tr = 16
    own_all = own.shape[0] == P

    def body(me_ref, w_ref, m_ref, v_ref, p_ref, own_ref, *rest):
        g_ref, d_ref, mo_ref, vo_ref = rest[-4:]
        mine = own_ref[...].astype(F32)
        g = None
        for p in range(P):
            term = jnp.where(me_ref[0] == p, mine, p_ref[p].astype(F32))
            g = term if g is None else g + term
        mn = ADAM_B1 * m_ref[...] + (1.0 - ADAM_B1) * g
        vn = ADAM_B2 * v_ref[...] + (1.0 - ADAM_B2) * (g * g)
        m_hat = mn / (1.0 - ADAM_B1 ** ADAM_STEP)
        v_hat = vn / (1.0 - ADAM_B2 ** ADAM_STEP)
        g_ref[...] = g
        d_ref[...] = -ADAM_LR * (m_hat / (jnp.sqrt(v_hat) + ADAM_EPS) + ADAM_WD * w_ref[...])
        mo_ref[...] = mn
        vo_ref[...] = vn

    blk = pl.BlockSpec((None, tr, C), lambda i, me_ref: (layer, i, 0))
    own_blk = pl.BlockSpec((None, tr, C), lambda i, me_ref: (me_ref[0] if own_all else 0, i, 0))
    prev = list(into) if into is not None else []
    return pl.pallas_call(
        body, name=name,
        grid_spec=pltpu.PrefetchScalarGridSpec(
            num_scalar_prefetch=1, grid=(R // tr,),
            in_specs=[blk, blk, blk, pl.BlockSpec((P, tr, C), lambda i, me_ref: (0, i, 0)), own_blk]
            + [_ANY] * len(prev),
            out_specs=[blk] * 4),
        out_shape=[jax.ShapeDtypeStruct((L, R, C), F32)] * 4,
        input_output_aliases={6 + i: i for i in range(len(prev))},
        compiler_params=_cp(("parallel",)),
    )(me, w, m, v, parts, own, *prev)


def _perm_cols(w):
    pad = jnp.zeros((w.shape[0], LANE - SSD_HEADS), w.dtype)
    return jnp.concatenate([w[:, 0:1536], w[:, 2438:2694], w[:, 1536:2432], w[:, 2432:2438], pad,
                            w[:, 2694:2950]], axis=1)


def _unperm_cols(w):
    return jnp.concatenate([w[:, 0:1536], w[:, XBC0:XBC0 + SSD_CONV_DIM], w[:, DT0:DT0 + SSD_HEADS],
                            w[:, U0:U0 + SGU_W], w[:, VS0:VS0 + SGU_W]], axis=1)


_SMALL = ("ffn1_norm", "mix_norm", "conv_w", "conv_b", "dt_bias", "a_log", "d_skip", "ssd_norm",
          "sgu_ln_g", "sgu_ln_b", "sgu_w", "sgu_b", "ffn2_norm", "final_norm", "loss")


_SMALL_LAST = ("ffn1_norm",)
_SMALL_EARLY = tuple(k for k in _SMALL if k not in _SMALL_LAST)


def _pack(d, names):
    v = jnp.concatenate([d[k].astype(F32).reshape(-1) for k in names])
    n = v.shape[0]
    npad = -(-n // (LANE * 16)) * (LANE * 16)
    return jnp.pad(v, (0, npad - n)).reshape(npad // LANE, LANE)


def _unpack(p, shapes, names):
    v = p.reshape(-1)
    out, o = {}, 0
    for k in names:
        n = int(np.prod(shapes[k]))
        out[k] = v[o:o + n].reshape(shapes[k])
        o += n
    return out


def kernel(x, ffn1_norm, ffn1_w_gate, ffn1_w_up, ffn1_w_down, mix_norm, w_in, conv_w, conv_b, dt_bias, a_log, d_skip, ssd_norm, sgu_ln_g, sgu_ln_b, sgu_w, sgu_b, w_out, ffn2_norm, ffn2_w_gate, ffn2_w_up, ffn2_w_down, final_norm, loss_target, m_ffn1_norm, m_ffn1_w_gate, m_ffn1_w_up, m_ffn1_w_down, m_mix_norm, m_w_in, m_conv_w, m_conv_b, m_dt_bias, m_a_log, m_d_skip, m_ssd_norm, m_sgu_ln_g, m_sgu_ln_b, m_sgu_w, m_sgu_b, m_w_out, m_ffn2_norm, m_ffn2_w_gate, m_ffn2_w_up, m_ffn2_w_down, m_final_norm, v_ffn1_norm, v_ffn1_w_gate, v_ffn1_w_up, v_ffn1_w_down, v_mix_norm, v_w_in, v_conv_w, v_conv_b, v_dt_bias, v_a_log, v_d_skip, v_ssd_norm, v_sgu_ln_g, v_sgu_ln_b, v_sgu_w, v_sgu_b, v_w_out, v_ffn2_norm, v_ffn2_w_gate, v_ffn2_w_up, v_ffn2_w_down, v_final_norm):
    B, S, D = x.shape
    T = B * S
    L = ffn1_norm.shape[0]
    me = 4 * lax.axis_index("x") + 2 * lax.axis_index("y") + lax.axis_index("c")
    cs = conv_w.shape[2]
    W = dict(ffn1_norm=ffn1_norm, ffn1_w_gate=ffn1_w_gate, ffn1_w_up=ffn1_w_up, ffn1_w_down=ffn1_w_down,
             mix_norm=mix_norm, w_in=w_in, conv_w=conv_w, conv_b=conv_b, dt_bias=dt_bias, a_log=a_log,
             d_skip=d_skip, ssd_norm=ssd_norm, sgu_ln_g=sgu_ln_g, sgu_ln_b=sgu_ln_b, sgu_w=sgu_w, sgu_b=sgu_b,
             w_out=w_out, ffn2_norm=ffn2_norm, ffn2_w_gate=ffn2_w_gate, ffn2_w_up=ffn2_w_up,
             ffn2_w_down=ffn2_w_down, final_norm=final_norm)
    M = dict(ffn1_norm=m_ffn1_norm, ffn1_w_gate=m_ffn1_w_gate, ffn1_w_up=m_ffn1_w_up, ffn1_w_down=m_ffn1_w_down,
             mix_norm=m_mix_norm, w_in=m_w_in, conv_w=m_conv_w, conv_b=m_conv_b, dt_bias=m_dt_bias, a_log=m_a_log,
             d_skip=m_d_skip, ssd_norm=m_ssd_norm, sgu_ln_g=m_sgu_ln_g, sgu_ln_b=m_sgu_ln_b, sgu_w=m_sgu_w,
             sgu_b=m_sgu_b, w_out=m_w_out, ffn2_norm=m_ffn2_norm, ffn2_w_gate=m_ffn2_w_gate,
             ffn2_w_up=m_ffn2_w_up, ffn2_w_down=m_ffn2_w_down, final_norm=m_final_norm)
    V = dict(ffn1_norm=v_ffn1_norm, ffn1_w_gate=v_ffn1_w_gate, ffn1_w_up=v_ffn1_w_up, ffn1_w_down=v_ffn1_w_down,
             mix_norm=v_mix_norm, w_in=v_w_in, conv_w=v_conv_w, conv_b=v_conv_b, dt_bias=v_dt_bias, a_log=v_a_log,
             d_skip=v_d_skip, ssd_norm=v_ssd_norm, sgu_ln_g=v_sgu_ln_g, sgu_ln_b=v_sgu_ln_b, sgu_w=v_sgu_w,
             sgu_b=v_sgu_b, w_out=v_w_out, ffn2_norm=v_ffn2_norm, ffn2_w_gate=v_ffn2_w_gate,
             ffn2_w_up=v_ffn2_w_up, ffn2_w_down=v_ffn2_w_down, final_norm=v_final_norm)
    FFN1 = ("ffn1_w_gate", "ffn1_w_up", "ffn1_w_down")
    FFN2 = ("ffn2_w_gate", "ffn2_w_up", "ffn2_w_down")
    MIX = ("w_in", "w_out")
    big = FFN1 + MIX + FFN2
    col_sharded = lambda k: k.endswith("w_gate") or k.endswith("w_up")
    for dct in (W, M, V):
        for k in big:
            if col_sharded(k):
                dct[k] = jnp.swapaxes(dct[k], 1, 2)

    wgroups = [[(k, 0) for k in FFN1], [("w_in", 0), ("conv_w", None)], [("w_out", 0)] + [(k, 0) for k in FFN2]]
    wgroups += [[(k, l) for k in big] for l in range(1, L)]
    wstarted, order = [], x
    later = [kl for grp in wgroups[1:] for kl in grp if kl[0] != "conv_w"]
    cast = dict(zip(wgroups[0], _cast_layers([(W[k], l) for k, l in wgroups[0]], "cast_first")))
    for gi, grp in enumerate(wgroups):
        if gi == 1:
            first = lax.optimization_barrier((W[later[0][0]], order))[0]
            srcs = [(first if i == 0 else W[k], l) for i, (k, l) in enumerate(later)]
            cast.update(zip(later, _cast_layers(srcs, "cast_rest")))
        xs = [conv_w if k == "conv_w" else cast[(k, l)] for k, l in grp]
        st = _gather_start(xs, order, f"gather_start_{gi}", own_slot=gi > 0)
        order = st["token"]
        wstarted.append(st)
    G = {}
    is_me = (jnp.arange(N_DEV) == me)

    zero1 = jnp.zeros((1,), F32)
    W["loss"], M["loss"], V["loss"] = zero1, zero1, zero1
    full_shapes = {k: (W[k].shape if k != "conv_w" else (L, SSD_CONV, SSD_CONV_DIM)) for k in _SMALL}
    embed = lambda a, k: a if k != "conv_w" else lax.dynamic_update_slice(
        jnp.zeros(full_shapes[k], F32), a, (0, 0, me * cs))
    small_packs = {names: [_pack({k: embed(d[k], k) for k in names}, names)[None] for d in (W, M, V)]
                   for names in (_SMALL_EARLY, _SMALL_LAST)}

    def relay(gi, after):
        wstarted[gi] = _gather_relay(wstarted[gi], after, f"gather_relay_{gi}")
        return wstarted[gi]["token"]

    def gathered(gi, after):
        own, lands = _gather_wait(wstarted[gi], after, f"gather_wait_{gi}")
        for key, o, z in zip(wgroups[gi], own, lands):
            G[key] = z if wstarted[gi]["own_slot"] else jnp.where(
                is_me.reshape((N_DEV,) + (1,) * o.ndim), o[None], z)

    def rows(k, l):
        a = G[(k, l)]
        return a.reshape(-1, a.shape[-1])

    bias = _attn_bias(S, min(256, S))
    row1 = lambda a: a.reshape(1, -1)

    def ffn1_params(l):
        return dict(g1=row1(ffn1_norm[l]), wg1=rows("ffn1_w_gate", l), wu1=rows("ffn1_w_up", l),
                    wd1=rows("ffn1_w_down", l))

    def out_params(l):
        return dict(wout=rows("w_out", l), g2=row1(ffn2_norm[l]), wg2=rows("ffn2_w_gate", l),
                    wu2=rows("ffn2_w_up", l), wd2=rows("ffn2_w_down", l))

    def mix_params(l):
        cw = jnp.transpose(G[("conv_w", None)][:, l], (1, 0, 2)).reshape(SSD_CONV, -1)
        return dict(
            gm=row1(mix_norm[l]), win=_perm_cols(rows("w_in", l)),
            cw=jnp.pad(cw, ((0, SUBLANE - SSD_CONV), (0, 0))), cb=row1(conv_b[l]),
            par=jnp.pad(jnp.stack([jnp.repeat(dt_bias[l], HEAD_DIM), jnp.repeat(a_log[l], HEAD_DIM),
                                   jnp.repeat(d_skip[l], HEAD_DIM), ssd_norm[l]]), ((0, SUBLANE - 4), (0, 0))),
            ln=jnp.pad(jnp.stack([sgu_ln_g[l], sgu_ln_b[l]]), ((0, SUBLANE - 2), (0, 0))),
            sw=sgu_w[l], bst=jnp.pad(sgu_b[l].T, ((0, 0), (0, SUBLANE - SGU_GROUPS))))

    xc = x.reshape(T, D)
    saved, lay = [], []
    tie = lambda a, tok: lax.optimization_barrier((a, tok))[0]
    for l in range(L):
        gathered(0 if l == 0 else l + 2, relay(0, order) if l == 0 else xc)
        p = ffn1_params(l)
        x1, gate1, up1 = _ffn_fwd(xc, p["g1"], p["wg1"], p["wu1"], p["wd1"], f"ffn1_fwd_{l}")
        if l == 0:
            gathered(1, relay(1, x1))
        p.update(mix_params(l))
        lay.append(p)
        proj, ht = _norm_mm(x1, p["gm"], p["win"], f"in_proj_{l}")
        if l == 0:
            proj = tie(proj, relay(2, proj))
        o_att, lse = _attn_fwd(proj, bias, B, S, f"attn_fwd_{l}")
        pre = _conv_fwd(proj, p["cw"], p["cb"], B, S, f"conv_fwd_{l}")
        y_ssd, sall = _ssd_fwd(pre, proj, p["par"], B, S, f"ssd_fwd_{l}")
        y_sgu = _sgu_fwd(proj, p["ln"], p["sw"], p["bst"], B, S, f"sgu_fwd_{l}")
        ycat = jnp.concatenate([o_att.astype(BF16), y_ssd, y_sgu], axis=1)
        if l + 1 < L:
            ycat = tie(ycat, relay(l + 3, ycat))
        if l == 0:
            gathered(2, ycat)
        p.update(out_params(l))
        x2 = _mm(ycat, p["wout"], "nn", f"out_proj_{l}", residual=x1)
        x3, gate2, up2 = _ffn_fwd(x2, p["g2"], p["wg2"], p["wu2"], p["wd2"], f"ffn2_fwd_{l}")
        saved.append(dict(x0=xc, gate1=gate1, up1=up1, x1=x1, ht=ht, proj=proj, o_att=o_att, lse=lse, pre=pre,
                          sall=sall, ycat=ycat, x2=x2, gate2=gate2, up2=up2))
        xc = x3
    loss_part, dx, dgf = _final_loss(xc, row1(final_norm), loss_target.reshape(T, D), "final_loss")

    gl = [dict() for _ in range(L)]
    gstarted, gorder = [], [order]

    def to_blocks(k, a):
        return a.reshape(N_DEV, -1, a.shape[-1]).astype(BF16)

    def send_grads(keys, l, extra, tag, small_names=None):
        xs = [to_blocks(k, gl[l][k]) for k in keys] + extra
        flags = [True] * len(keys) + [False] * len(extra)
        st = _xchg_start(xs, flags, gorder[0], f"grads_start_{tag}")
        gorder[0] = st[-1]
        gstarted.append((keys, l, st, flags, tag, small_names))

    def small_grads(names):
        sm = {}
        for k in names:
            if k == "final_norm":
                sm[k] = dgf.reshape(-1)
            elif k == "loss":
                sm[k] = loss_part[0, :1]
            else:
                sm[k] = jnp.stack([gl[l][k] for l in range(L)])
        return _pack(sm, names)

    def behind(a):
        return lax.optimization_barrier((a, gorder[0]))[0]

    for l in reversed(range(L)):
        p, s, g = lay[l], saved[l], gl[l]
        dx2, dgt, dut, actt, xn, dacc, g["ffn2_norm"] = _ffn_bwd_dx(
            dx, s["x2"], p["g2"], s["gate2"], s["up2"], p["wg2"], p["wu2"], p["wd2"], f"ffn2_bwd_{l}")
        g["ffn2_w_gate"], g["ffn2_w_up"], g["ffn2_w_down"] = _ffn_dw(dgt, dut, actt, xn, dacc, f"ffn2_dw_{l}")
        if l == 0:
            send_grads(FFN2, 0, [], "l0f")
            dx2 = behind(dx2)
        dycat = _mm(dx2, p["wout"], "nt", f"out_proj_dx_{l}")
        g["w_out"] = _mm(s["ycat"], dx2, "tn", f"out_proj_dw_{l}", out_dtype=BF16, tm_cap=1024, tk_cap=512)
        dq, dk, dv = _attn_bwd(s["proj"], s["o_att"], s["lse"], dycat, bias, B, S, f"attn_bwd_{l}")
        dpre, dz, ddt, dpar = _ssd_bwd(s["pre"], s["proj"], s["sall"], dycat, p["par"], B, S, f"ssd_bwd_{l}")
        dxbc, dwb = _conv_bwd(dpre, s["proj"], p["cw"], B, S, f"conv_bwd_{l}")
        du, dvs, dln, dsw, dbst = _sgu_bwd(s["proj"], dycat, p["ln"], p["sw"], p["bst"], B, S, f"sgu_bwd_{l}")
        hsum = lambda r: r.reshape(SSD_HEADS, HEAD_DIM).sum(-1)
        g["conv_w"], g["conv_b"] = dwb[:SSD_CONV], dwb[SSD_CONV]
        g["dt_bias"], g["a_log"], g["d_skip"], g["ssd_norm"] = hsum(dpar[0]), hsum(dpar[1]), hsum(dpar[2]), dpar[3]
        g["sgu_ln_g"], g["sgu_ln_b"], g["sgu_w"], g["sgu_b"] = dln[0], dln[1], dsw, dbst[:, :SGU_GROUPS].T
        dproj = jnp.concatenate([dq, dk, dv, dz, du, dxbc, ddt, dvs], axis=1)
        g["w_in"] = _unperm_cols(_mm_resident_lhs(s["ht"], dproj, f"in_proj_dw_{l}"))
        dx1, g["mix_norm"] = _norm_mm_bwd(dproj, s["x1"], p["gm"], p["win"], dx2, f"in_proj_bwd_{l}")
        if l == 0:
            send_grads(MIX, 0, [small_grads(_SMALL_EARLY)], "l0a", _SMALL_EARLY)
            dx1, small_packs = lax.optimization_barrier((behind(dx1), small_packs))
        dx, dgt, dut, actt, xn, dacc, g["ffn1_norm"] = _ffn_bwd_dx(
            dx1, s["x0"], p["g1"], s["gate1"], s["up1"], p["wg1"], p["wu1"], p["wd1"], f"ffn1_bwd_{l}")
        g["ffn1_w_gate"], g["ffn1_w_up"], g["ffn1_w_down"] = _ffn_dw(dgt, dut, actt, xn, dacc, f"ffn1_dw_{l}")
        if l > 0:
            send_grads(big, l, [], f"l{l}")
            dx = behind(dx)
    grad_x = dx.reshape(B, S, D)
    send_grads(FFN1, 0, [small_grads(_SMALL_LAST)], "l0b", _SMALL_LAST)

    res, after = {}, gorder[0]
    small_out = [dict() for _ in range(4)]
    me1 = me.reshape(1).astype(jnp.int32)
    for keys, l, st, flags, tag, names in gstarted:
        own, lands = _xchg_wait(st, flags, after, f"grads_wait_{tag}")
        for k, mine, pk in zip(keys, own, lands):
            res[k] = _adamw(me1, W[k], M[k], V[k], pk, mine, f"adamw_{k}_{l}", layer=l, into=res.get(k))
        done = [res[k][0] for k in keys]
        if names:
            outs = _adamw(me1, *small_packs[names], lands[-1], own[-1][None], f"adamw_small_{tag}")
            for d, o in zip(small_out, outs):
                u = _unpack(o, full_shapes, names)
                if "conv_w" in u:
                    u["conv_w"] = lax.dynamic_slice(u["conv_w"], (0, 0, me * cs), (L, SSD_CONV, cs))
                d.update(u)
                done.extend(u.values())
        after = lax.optimization_barrier(tuple(done))[0]
    back = lambda k, a: jnp.swapaxes(a, 1, 2) if col_sharded(k) else a
    grads, deltas, new_m, new_v = [dict({k: back(k, res[k][i]) for k in big}, **small_out[i]) for i in range(4)]

    names = ("ffn1_norm", "ffn1_w_gate", "ffn1_w_up", "ffn1_w_down", "mix_norm", "w_in", "conv_w", "conv_b",
             "dt_bias", "a_log", "d_skip", "ssd_norm", "sgu_ln_g", "sgu_ln_b", "sgu_w", "sgu_b", "w_out",
             "ffn2_norm", "ffn2_w_gate", "ffn2_w_up", "ffn2_w_down", "final_norm")
    loss = grads["loss"][0]
    return (loss, grad_x, *[grads[n] for n in names], *[deltas[n] for n in names],
            *[new_m[n] for n in names], *[new_v[n] for n in names])
```

```python
import functools

import numpy as np
import jax
import jax.numpy as jnp
from jax import lax
from jax.experimental import pallas as pl
from jax.experimental.pallas import tpu as pltpu

F32, BF16 = jnp.float32, jnp.bfloat16
HI = lax.Precision.HIGH
MESH = pl.DeviceIdType.MESH
N_DEV = 8
VMEM_LIMIT_BYTES = 56 * 1024 * 1024
LANE, SUBLANE = 128, 8

HEAD_DIM = 64
ATT_W = 384
SSD_W = 384
SSD_HEADS = 6
SSD_STATE = 128
SSD_CONV = 4
CHUNK = 128
SSD_CONV_DIM = 896
SGU_W = 256
SGU_GROUPS = 4
D_IN = 2950
RMS_EPS = 1e-6
LN_EPS = 1e-5
NEG = -1e30

PW = 3072
Q0, K0, V0, Z0, U0, XBC0, DT0, VS0 = 0, 384, 768, 1152, 1536, 1792, 2688, 2816

ADAM_LR, ADAM_B1, ADAM_B2, ADAM_EPS, ADAM_WD, ADAM_STEP = 0.001, 0.9, 0.999, 1e-08, 0.01, 10


def _cp(sem=None):
    return pltpu.CompilerParams(dimension_semantics=sem, vmem_limit_bytes=VMEM_LIMIT_BYTES)


def _tile(n, cap, mult=LANE):
    best = None
    t = mult
    while t <= min(n, cap):
        if n % t == 0:
            best = t
        t += mult
    return best if best is not None else n


def _dot(a, b, prec=None):
    return jnp.dot(a, b, preferred_element_type=F32, precision=prec)


def _dot_nt(a, b, prec=None):
    return lax.dot_general(a, b, (((1,), (1,)), ((), ())), preferred_element_type=F32, precision=prec)


def _dot_tn(a, b, prec=None):
    return lax.dot_general(a, b, (((0,), (0,)), ((), ())), preferred_element_type=F32, precision=prec)


def _sigmoid(x):
    return 1.0 / (1.0 + jnp.exp(-x))


def _silu(x):
    return x * _sigmoid(x)


def _gelu(x):
    return 0.5 * x * (1.0 + lax.erf(x * 0.7071067811865476))


def _softplus(x):
    return jnp.maximum(x, 0.0) + jnp.log(1.0 + jnp.exp(-jnp.abs(x)))


def _rms_fwd(x, g):
    rstd = lax.rsqrt(jnp.mean(x * x, axis=-1, keepdims=True) + RMS_EPS)
    xhat = x * rstd
    return xhat * g, xhat, rstd


def _rms_bwd(dy, xhat, rstd, g):
    dxhat = dy * g
    dx = rstd * (dxhat - xhat * jnp.mean(dxhat * xhat, axis=-1, keepdims=True))
    return dx, dy * xhat


def _resident(shape):
    return pl.BlockSpec(shape, lambda *_: (0,) * len(shape), pipeline_mode=pl.Buffered(1))


def _mm(a, b, mode, name, out_dtype=F32, residual=None, tm_cap=512, tn_cap=1024, tk_cap=1024):
    if mode == "nn":
        (M, K), (_, N) = a.shape, b.shape
    elif mode == "nt":
        (M, K), (N, _) = a.shape, b.shape
    else:
        (K, M), (_, N) = a.shape, b.shape
    tm, tn, tk = _tile(M, tm_cap), _tile(N, tn_cap), _tile(K, tk_cap)
    nk = K // tk
    if mode == "tn":
        a_spec = pl.BlockSpec((tk, tm), lambda i, j, k: (k, i))
    else:
        a_spec = pl.BlockSpec((tm, tk), lambda i, j, k: (i, k))
    if mode == "nt":
        b_spec = pl.BlockSpec((tn, tk), lambda i, j, k: (j, k))
    else:
        b_spec = pl.BlockSpec((tk, tn), lambda i, j, k: (k, j))
    o_spec = pl.BlockSpec((tm, tn), lambda i, j, k: (i, j))
    has_res = residual is not None

    def prod(a_ref, b_ref):
        av = a_ref[...].astype(BF16)
        bv = b_ref[...].astype(BF16)
        if mode == "nn":
            return _dot(av, bv)
        if mode == "nt":
            return _dot_nt(av, bv)
        return _dot_tn(av, bv)

    def body(*refs):
        a_ref, b_ref = refs[:2]
        r_ref = refs[2] if has_res else None
        o_ref = refs[2 + has_res]
        if nk == 1:
            o = prod(a_ref, b_ref)
            if has_res:
                o = r_ref[...] + o
            o_ref[...] = o.astype(out_dtype)
            return
        acc = refs[3 + has_res]
        k = pl.program_id(2)

        @pl.when(k == 0)
        def _():
            acc[...] = jnp.zeros_like(acc)

        acc[...] += prod(a_ref, b_ref)

        @pl.when(k == nk - 1)
        def _():
            o = acc[...]
            if has_res:
                o = r_ref[...] + o
            o_ref[...] = o.astype(out_dtype)

    ins = [a, b] + ([residual] if has_res else [])
    in_specs = [a_spec, b_spec] + ([o_spec] if has_res else [])
    return pl.pallas_call(
        body, name=name, grid=(M // tm, N // tn, nk),
        in_specs=in_specs, out_specs=o_spec,
        out_shape=jax.ShapeDtypeStruct((M, N), out_dtype),
        scratch_shapes=[pltpu.VMEM((tm, tn), F32)] if nk > 1 else [],
        compiler_params=_cp(("parallel", "parallel", "arbitrary")),
    )(*ins)


def _ffn_fwd(x, g, wgt, wut, wd, name):
    T, D = x.shape
    F = wgt.shape[0]
    tm = _tile(T, 512)

    def body(x_ref, g_ref, wg_ref, wu_ref, wd_ref, out_ref, gate_ref, up_ref):
        xv = x_ref[...]
        xn = _rms_fwd(xv, g_ref[...])[0].astype(BF16)
        gate = _dot_nt(xn, wg_ref[...])
        up = _dot_nt(xn, wu_ref[...])
        gate_ref[...] = gate.astype(BF16)
        up_ref[...] = up.astype(BF16)
        act = (_silu(gate) * up).astype(BF16)
        out_ref[...] = xv + 0.5 * _dot(act, wd_ref[...])

    row = lambda w: pl.BlockSpec((tm, w), lambda i: (i, 0))
    return pl.pallas_call(
        body, name=name, grid=(T // tm,),
        in_specs=[row(D), _resident((1, D)), _resident((F, D)), _resident((F, D)), _resident((F, D))],
        out_specs=[row(D), row(F), row(F)],
        out_shape=[jax.ShapeDtypeStruct((T, D), F32),
                   jax.ShapeDtypeStruct((T, F), BF16),
                   jax.ShapeDtypeStruct((T, F), BF16)],
        compiler_params=_cp(("parallel",)),
    )(x, g, wgt, wut, wd)


def _ffn_bwd_dx(dout, x, g, gate, up, wg, wu, wd, name):
    T, D = x.shape
    F = wg.shape[0]
    tm = _tile(T, 256)

    def body(dout_ref, x_ref, g_ref, gate_ref, up_ref, wg_ref, wu_ref, wd_ref,
             dx_ref, dgt_ref, dut_ref, actt_ref, xn_ref, dacc_ref, dg_ref):
        @pl.when(pl.program_id(0) == 0)
        def _():
            dg_ref[...] = jnp.zeros_like(dg_ref)

        gv = g_ref[...]
        dout_v = dout_ref[...]
        xn, xhat, rstd = _rms_fwd(x_ref[...], gv)
        xn_ref[...] = xn.astype(BF16)
        dacc = (0.5 * dout_v).astype(BF16)
        dacc_ref[...] = dacc
        dact = _dot_nt(dacc, wd_ref[...])
        gt = gate_ref[...].astype(F32)
        u = up_ref[...].astype(F32)
        sig = _sigmoid(gt)
        sl = gt * sig
        dgate = (dact * u * (sig * (1.0 + gt * (1.0 - sig)))).astype(BF16)
        dup = (dact * sl).astype(BF16)
        dgt_ref[...] = dgate.T
        dut_ref[...] = dup.T
        actt_ref[...] = (sl * u).astype(BF16).T
        dxn = _dot(dgate, wg_ref[...]) + _dot(dup, wu_ref[...])
        dx, dgrow = _rms_bwd(dxn, xhat, rstd, gv)
        dx_ref[...] = dout_v + dx
        dg_ref[...] += jnp.sum(dgrow, axis=0, keepdims=True)

    row = lambda w: pl.BlockSpec((tm, w), lambda i: (i, 0))
    tr = pl.BlockSpec((F, tm), lambda i: (0, i))
    return pl.pallas_call(
        body, name=name, grid=(T // tm,),
        in_specs=[row(D), row(D), _resident((1, D)), row(F), row(F),
                  _resident((F, D)), _resident((F, D)), _resident((F, D))],
        out_specs=[row(D), tr, tr, tr, row(D), row(D), pl.BlockSpec((1, D), lambda i: (0, 0))],
        out_shape=[jax.ShapeDtypeStruct((T, D), F32)] + [jax.ShapeDtypeStruct((F, T), BF16)] * 3
        + [jax.ShapeDtypeStruct((T, D), BF16)] * 2 + [jax.ShapeDtypeStruct((1, D), F32)],
        compiler_params=_cp(("arbitrary",)),
    )(dout, x, g, gate, up, wg, wu, wd)


def _ffn_dw(dgt, dut, actt, xn, dacc, name):
    F, T = dgt.shape
    D = xn.shape[1]
    th = _tile(F, 256)

    def body(dg_ref, du_ref, a_ref, xn_ref, dacc_ref, dwg_ref, dwu_ref, dwd_ref):
        xv = xn_ref[...]
        dwg_ref[...] = _dot(dg_ref[...], xv).astype(BF16)
        dwu_ref[...] = _dot(du_ref[...], xv).astype(BF16)
        dwd_ref[...] = _dot(a_ref[...], dacc_ref[...]).astype(BF16)

    tile = pl.BlockSpec((th, T), lambda j: (j, 0))
    out = pl.BlockSpec((th, D), lambda j: (j, 0))
    return pl.pallas_call(
        body, name=name, grid=(F // th,),
        in_specs=[tile, tile, tile, _resident((T, D)), _resident((T, D))],
        out_specs=[out, out, out], out_shape=[jax.ShapeDtypeStruct((F, D), BF16)] * 3,
        compiler_params=_cp(("parallel",)),
    )(dgt, dut, actt, xn, dacc)


def _norm_mm(x, g, w, name):
    T, D = x.shape
    N = w.shape[1]
    tm = _tile(T, 512)

    def body(x_ref, g_ref, w_ref, o_ref, ht_ref):
        xn = _rms_fwd(x_ref[...], g_ref[...])[0]
        ht_ref[...] = xn.T.astype(BF16)
        o_ref[...] = _dot(xn.astype(BF16), w_ref[...])

    return pl.pallas_call(
        body, name=name, grid=(T // tm,),
        in_specs=[pl.BlockSpec((tm, D), lambda i: (i, 0)), _resident((1, D)), _resident((D, N))],
        out_specs=[pl.BlockSpec((tm, N), lambda i: (i, 0)), pl.BlockSpec((D, tm), lambda i: (0, i))],
        out_shape=[jax.ShapeDtypeStruct((T, N), F32), jax.ShapeDtypeStruct((D, T), BF16)],
        compiler_params=_cp(("parallel",)),
    )(x, g, w)


def _norm_mm_bwd(dproj, x, g, w, dres, name):
    T, D = x.shape
    N = w.shape[1]
    tm = _tile(T, 512)

    def body(dp_ref, x_ref, g_ref, w_ref, dres_ref, dx_ref, dg_ref):
        @pl.when(pl.program_id(0) == 0)
        def _():
            dg_ref[...] = jnp.zeros_like(dg_ref)

        gv = g_ref[...]
        dh = _dot_nt(dp_ref[...], w_ref[...])
        _, xhat, rstd = _rms_fwd(x_ref[...], gv)
        dx, dgrow = _rms_bwd(dh, xhat, rstd, gv)
        dx_ref[...] = dres_ref[...] + dx
        dg_ref[...] += jnp.sum(dgrow, axis=0, keepdims=True)

    row = pl.BlockSpec((tm, D), lambda i: (i, 0))
    one = pl.BlockSpec((1, D), lambda i: (0, 0))
    return pl.pallas_call(
        body, name=name, grid=(T // tm,),
        in_specs=[pl.BlockSpec((tm, N), lambda i: (i, 0)), row, _resident((1, D)), _resident((D, N)), row],
        out_specs=[row, one],
        out_shape=[jax.ShapeDtypeStruct((T, D), F32), jax.ShapeDtypeStruct((1, D), F32)],
        compiler_params=_cp(("arbitrary",)),
    )(dproj, x, g, w, dres)


def _mm_resident_lhs(at, b, name, tn_cap=512):
    M, K = at.shape
    N = b.shape[1]
    tn = _tile(N, tn_cap)

    def body(a_ref, b_ref, o_ref):
        o_ref[...] = _dot(a_ref[...], b_ref[...]).astype(BF16)

    return pl.pallas_call(
        body, name=name, grid=(N // tn,),
        in_specs=[_resident((M, K)), pl.BlockSpec((K, tn), lambda j: (0, j))],
        out_specs=pl.BlockSpec((M, tn), lambda j: (0, j)),
        out_shape=jax.ShapeDtypeStruct((M, N), BF16),
        compiler_params=_cp(("parallel",)),
    )(at, b)


def _final_loss(x, g, target, name):
    T, D = x.shape
    tm = _tile(T, 512)

    def body(x_ref, g_ref, t_ref, loss_ref, dx_ref, dg_ref):
        @pl.when(pl.program_id(0) == 0)
        def _():
            dg_ref[...] = jnp.zeros_like(dg_ref)
            loss_ref[...] = jnp.zeros_like(loss_ref)

        gv = g_ref[...]
        y, xhat, rstd = _rms_fwd(x_ref[...], gv)
        err = y - t_ref[...]
        part = 0.5 * jnp.sum(jnp.mean(err * err, axis=-1, keepdims=True), axis=0, keepdims=True)
        loss_ref[...] += jnp.broadcast_to(part, loss_ref.shape)
        dy = err * (1.0 / D)
        dx, dgrow = _rms_bwd(dy, xhat, rstd, gv)
        dx_ref[...] = dx
        dg_ref[...] += jnp.sum(dgrow, axis=0, keepdims=True)

    row = pl.BlockSpec((tm, D), lambda i: (i, 0))
    one = pl.BlockSpec((1, D), lambda i: (0, 0))
    return pl.pallas_call(
        body, name=name, grid=(T // tm,),
        in_specs=[row, one, row],
        out_specs=[pl.BlockSpec((1, LANE), lambda i: (0, 0)), row, one],
        out_shape=[jax.ShapeDtypeStruct((1, LANE), F32), jax.ShapeDtypeStruct((T, D), F32),
                   jax.ShapeDtypeStruct((1, D), F32)],
        compiler_params=_cp(("arbitrary",)),
    )(x, g, target)


def _attn_bias(S, bq):
    d = jnp.arange(bq)[:, None] - jnp.arange(S)[None, :] + (S // bq - 1) * bq
    ok = d >= 0
    mult = ((ok & (d <= 128)).astype(F32) + (ok & (d % 4 == 0) & (d <= 512)).astype(F32)
            + (ok & (d % 16 == 0) & (d <= 2048)).astype(F32))
    return jnp.where(mult > 0, jnp.log(jnp.maximum(mult, 1.0)), NEG).astype(F32)


def _attn_fwd(proj, bias, B, S, name):
    T = B * S
    bq = bias.shape[0]
    nb = S // bq
    qcol, kcol, vcol = Q0 // LANE, K0 // LANE, V0 // LANE

    def body(q_ref, k_ref, v_ref, t_ref, o_ref, lse_ref, ks, vs):
        for hh in range(2):
            sl = slice(HEAD_DIM * hh, HEAD_DIM * (hh + 1))
            ks[hh] = k_ref[:, sl].astype(BF16)
            vs[hh] = v_ref[:, sl].astype(BF16)
        for hh in range(2):
            sl = slice(HEAD_DIM * hh, HEAD_DIM * (hh + 1))
            for qb in range(nb):
                w, off, rows = bq * (qb + 1), (nb - 1 - qb) * bq, slice(qb * bq, (qb + 1) * bq)
                q = (q_ref[rows, sl] * 0.125).astype(BF16)
                s = _dot_nt(q, ks[hh, 0:w, :]) + t_ref[:, off:off + w]
                m = jnp.max(s, axis=-1, keepdims=True)
                p = jnp.exp(s - m)
                l = jnp.sum(p, axis=-1, keepdims=True)
                o_ref[rows, sl] = _dot(p.astype(BF16), vs[hh, 0:w, :]) / l
                lse_ref[rows, hh:hh + 1] = m + jnp.log(l)

    blk = lambda c0: pl.BlockSpec((S, LANE), lambda b, p: (b, c0 + p))
    return pl.pallas_call(
        body, name=name, grid=(B, ATT_W // LANE),
        in_specs=[blk(qcol), blk(kcol), blk(vcol), _resident((bq, S))],
        out_specs=[pl.BlockSpec((S, LANE), lambda b, p: (b, p)),
                   pl.BlockSpec((None, None, S, 2), lambda b, p: (b, p, 0, 0))],
        out_shape=[jax.ShapeDtypeStruct((T, ATT_W), F32),
                   jax.ShapeDtypeStruct((B, ATT_W // LANE, S, 2), F32)],
        scratch_shapes=[pltpu.VMEM((2, S, HEAD_DIM), BF16)] * 2,
        compiler_params=_cp(("parallel", "parallel")),
    )(proj, proj, proj, bias)


def _attn_bwd(proj, o, lse, dy, bias, B, S, name):
    T = B * S
    bq = bias.shape[0]
    nb = S // bq
    qcol, kcol, vcol = Q0 // LANE, K0 // LANE, V0 // LANE

    def body(q_ref, k_ref, v_ref, o_ref, lse_ref, do_ref, t_ref, dq_ref, dk_ref, dv_ref, ks, vs, dks, dvs):
        for hh in range(2):
            sl = slice(HEAD_DIM * hh, HEAD_DIM * (hh + 1))
            ks[hh] = k_ref[:, sl].astype(BF16)
            vs[hh] = v_ref[:, sl].astype(BF16)
        dks[...] = jnp.zeros_like(dks)
        dvs[...] = jnp.zeros_like(dvs)
        for hh in range(2):
            sl = slice(HEAD_DIM * hh, HEAD_DIM * (hh + 1))
            for qb in range(nb):
                w, off, rows = bq * (qb + 1), (nb - 1 - qb) * bq, slice(qb * bq, (qb + 1) * bq)
                q = (q_ref[rows, sl] * 0.125).astype(BF16)
                do = do_ref[rows, sl]
                dob = do.astype(BF16)
                delta = jnp.sum(do * o_ref[rows, sl], axis=-1, keepdims=True)
                k, v = ks[hh, 0:w, :], vs[hh, 0:w, :]
                s = _dot_nt(q, k) + t_ref[:, off:off + w]
                p = jnp.exp(s - lse_ref[rows, hh:hh + 1])
                ds = (p * (_dot_nt(dob, v) - delta)).astype(BF16)
                dq_ref[rows, sl] = (_dot(ds, k) * 0.125).astype(dq_ref.dtype)
                dks[hh, 0:w, :] += _dot_tn(ds, q)
                dvs[hh, 0:w, :] += _dot_tn(p.astype(BF16), dob)
            dk_ref[:, sl] = dks[hh].astype(dk_ref.dtype)
            dv_ref[:, sl] = dvs[hh].astype(dv_ref.dtype)

    blk = lambda c0: pl.BlockSpec((S, LANE), lambda b, p: (b, c0 + p))
    own = pl.BlockSpec((S, LANE), lambda b, p: (b, p))
    return pl.pallas_call(
        body, name=name, grid=(B, ATT_W // LANE),
        in_specs=[blk(qcol), blk(kcol), blk(vcol), own,
                  pl.BlockSpec((None, None, S, 2), lambda b, p: (b, p, 0, 0)), own, _resident((bq, S))],
        out_specs=[own, own, own],
        out_shape=[jax.ShapeDtypeStruct((T, ATT_W), BF16)] * 3,
        scratch_shapes=[pltpu.VMEM((2, S, HEAD_DIM), BF16)] * 2 + [pltpu.VMEM((2, S, HEAD_DIM), F32)] * 2,
        compiler_params=_cp(("parallel", "parallel")),
    )(proj, proj, proj, o, lse, dy, bias)


def _conv_fwd(proj, cw, cb, B, S, name):
    T = B * S
    nc = SSD_CONV_DIM // LANE
    c0 = XBC0 // LANE

    def body(x_ref, w_ref, b_ref, o_ref):
        x = x_ref[...]
        t = lax.broadcasted_iota(jnp.int32, (S, 1), 0)
        acc = b_ref[...] + w_ref[SSD_CONV - 1:SSD_CONV, :] * x
        for k in range(SSD_CONV - 1):
            sh = SSD_CONV - 1 - k
            xs = jnp.where(t >= sh, pltpu.roll(x, sh, 0), 0.0)
            acc = acc + w_ref[k:k + 1, :] * xs
        o_ref[...] = acc

    return pl.pallas_call(
        body, name=name, grid=(B, nc),
        in_specs=[pl.BlockSpec((S, LANE), lambda b, j: (b, c0 + j)),
                  pl.BlockSpec((SUBLANE, LANE), lambda b, j: (0, j)),
                  pl.BlockSpec((1, LANE), lambda b, j: (0, j))],
        out_specs=pl.BlockSpec((S, LANE), lambda b, j: (b, j)),
        out_shape=jax.ShapeDtypeStruct((T, SSD_CONV_DIM), F32),
        compiler_params=_cp(("parallel", "parallel")),
    )(proj, cw, cb)


def _conv_bwd(dpre, proj, cw, B, S, name):
    T = B * S
    nc = SSD_CONV_DIM // LANE
    c0 = XBC0 // LANE

    def body(d_ref, x_ref, w_ref, dx_ref, dwb_ref):
        @pl.when(pl.program_id(1) == 0)
        def _():
            dwb_ref[...] = jnp.zeros_like(dwb_ref)

        d = d_ref[...]
        x = x_ref[...]
        t = lax.broadcasted_iota(jnp.int32, (S, 1), 0)
        dx = w_ref[SSD_CONV - 1:SSD_CONV, :] * d
        rows = [None] * SUBLANE
        rows[SSD_CONV - 1] = jnp.sum(d * x, axis=0, keepdims=True)
        for k in range(SSD_CONV - 1):
            sh = SSD_CONV - 1 - k
            dx = dx + w_ref[k:k + 1, :] * jnp.where(t < S - sh, pltpu.roll(d, S - sh, 0), 0.0)
            xs = jnp.where(t >= sh, pltpu.roll(x, sh, 0), 0.0)
            rows[k] = jnp.sum(d * xs, axis=0, keepdims=True)
        rows[SSD_CONV] = jnp.sum(d, axis=0, keepdims=True)
        dx_ref[...] = dx.astype(BF16)
        r = lax.broadcasted_iota(jnp.int32, (SUBLANE, LANE), 0)
        upd = jnp.zeros((SUBLANE, LANE), F32)
        for k in range(SSD_CONV + 1):
            upd = upd + jnp.where(r == k, rows[k], 0.0)
        dwb_ref[...] += upd

    return pl.pallas_call(
        body, name=name, grid=(nc, B),
        in_specs=[pl.BlockSpec((S, LANE), lambda j, b: (b, j)),
                  pl.BlockSpec((S, LANE), lambda j, b: (b, c0 + j)),
                  pl.BlockSpec((SUBLANE, LANE), lambda j, b: (0, j))],
        out_specs=[pl.BlockSpec((S, LANE), lambda j, b: (b, j)),
                   pl.BlockSpec((SUBLANE, LANE), lambda j, b: (0, j))],
        out_shape=[jax.ShapeDtypeStruct((T, SSD_CONV_DIM), BF16),
                   jax.ShapeDtypeStruct((SUBLANE, SSD_CONV_DIM), F32)],
        compiler_params=_cp(("parallel", "arbitrary")),
    )(dpre, proj, cw)


def _ssd_consts():
    e = np.zeros((LANE, SSD_W), np.float32)
    p = np.zeros((SUBLANE, SSD_W), np.float32)
    for h in range(SSD_HEADS):
        e[h, HEAD_DIM * h:HEAD_DIM * (h + 1)] = 1.0
        p[h, HEAD_DIM * h] = 1.0
    return jnp.asarray(e), jnp.asarray(p)


def _ssd_chunk(pre, z, dtr, sprev, par, e_mat, psel):
    L = CHUNK
    xc = _silu(pre)
    xs, bm, cm = xc[:, :SSD_W], xc[:, SSD_W:SSD_W + 2 * SSD_STATE], xc[:, SSD_W + 2 * SSD_STATE:]
    dtb, alog, dskip, ng = par[0:1], par[1:2], par[2:3], par[3:4]
    dt = _softplus(_dot(dtr, e_mat, HI) + dtb)
    a = dt * (-jnp.exp(alog))
    X = xs * dt
    ri = lax.broadcasted_iota(jnp.int32, (L, L), 0)
    ci = lax.broadcasted_iota(jnp.int32, (L, L), 1)
    tril = ri >= ci
    acs = _dot(tril.astype(F32), a, HI)
    acs_t = _dot_nt(psel, acs, HI)
    ecs = jnp.exp(acs)
    alast = acs[L - 1:L, :]
    xd = (X * jnp.exp(alast - acs)).astype(BF16)
    xb = X.astype(BF16)
    col = lax.broadcasted_iota(jnp.int32, (1, SSD_W), 1)
    sb = sprev.astype(BF16)
    bgs = [bm[:, SSD_STATE * g:SSD_STATE * (g + 1)].astype(BF16) for g in range(2)]
    cgs = [cm[:, SSD_STATE * g:SSD_STATE * (g + 1)].astype(BF16) for g in range(2)]
    cbs = [_dot_nt(cgs[g], bgs[g]) for g in range(2)]
    first = lax.broadcasted_iota(jnp.int32, (1, LANE), 1) < HEAD_DIM
    y_tiles, s_tiles = [], []
    for t in range(SSD_W // LANE):
        cl = slice(LANE * t, LANE * (t + 1))
        xb_t, xd_t, sb_t = xb[:, cl], xd[:, cl], sb[:, cl]
        per_head = []
        for h in (2 * t, 2 * t + 1):
            seg = acs[:, HEAD_DIM * h:HEAD_DIM * h + 1] - acs_t[h:h + 1, :]
            dec = jnp.exp(jnp.where(tril, seg, NEG))
            per_head.append(_dot((cbs[h // 3] * dec).astype(BF16), xb_t))
        y_t = jnp.where(first, per_head[0], per_head[1])
        ga, gb = (2 * t) // 3, (2 * t + 1) // 3
        if ga == gb:
            y_off, s_add = _dot(cgs[ga], sb_t), _dot_tn(bgs[ga], xd_t)
        else:
            y_off = jnp.where(first, _dot(cgs[ga], sb_t), _dot(cgs[gb], sb_t))
            s_add = jnp.where(first, _dot_tn(bgs[ga], xd_t), _dot_tn(bgs[gb], xd_t))
        y_tiles.append(y_t + y_off * ecs[:, cl])
        s_tiles.append(s_add)
    y = dskip * xs + jnp.concatenate(y_tiles, axis=1)
    snew = sprev * jnp.exp(alast) + jnp.concatenate(s_tiles, axis=1)
    yg = y * _silu(z)
    sq = yg * yg
    g0 = col < SSD_W // 2
    ms0 = jnp.sum(jnp.where(g0, sq, 0.0), axis=-1, keepdims=True) * (2.0 / SSD_W)
    ms1 = jnp.sum(jnp.where(g0, 0.0, sq), axis=-1, keepdims=True) * (2.0 / SSD_W)
    r = jnp.where(g0, lax.rsqrt(ms0 + RMS_EPS), lax.rsqrt(ms1 + RMS_EPS))
    return yg * r * ng, snew


SSD_CHUNKS_PER_STEP = 2


def _ssd_chunks_per_step(S):
    k = SSD_CHUNKS_PER_STEP
    while (S // CHUNK) % k:
        k //= 2
    return k


def _ssd_fwd(pre, proj, par, B, S, name):
    T = B * S
    k = _ssd_chunks_per_step(S)
    nc, rows = S // (CHUNK * k), CHUNK * k
    e_mat, psel = _ssd_consts()

    def body(pre_ref, z_ref, dt_ref, par_ref, e_ref, p_ref, y_ref, sall_ref, st):
        @pl.when(pl.program_id(1) == 0)
        def _():
            st[...] = jnp.zeros_like(st)

        sprev = st[...]
        for i in range(k):
            r = slice(CHUNK * i, CHUNK * (i + 1))
            sall_ref[i] = sprev
            y, sprev = _ssd_chunk(pre_ref[r, :], z_ref[r, :], dt_ref[r, :], sprev, par_ref[...], e_ref[...],
                                  p_ref[...])
            y_ref[r, :] = y.astype(BF16)
        st[...] = sprev

    row = lambda b, c: b * nc + c
    full = lambda shp: pl.BlockSpec(shp, lambda b, c: (0, 0))
    return pl.pallas_call(
        body, name=name, grid=(B, nc),
        in_specs=[pl.BlockSpec((rows, SSD_CONV_DIM), lambda b, c: (row(b, c), 0)),
                  pl.BlockSpec((rows, SSD_W), lambda b, c: (row(b, c), Z0 // SSD_W)),
                  pl.BlockSpec((rows, LANE), lambda b, c: (row(b, c), DT0 // LANE)),
                  full((SUBLANE, SSD_W)), full((LANE, SSD_W)), full((SUBLANE, SSD_W))],
        out_specs=[pl.BlockSpec((rows, SSD_W), lambda b, c: (row(b, c), 0)),
                   pl.BlockSpec((k, SSD_STATE, SSD_W), lambda b, c: (row(b, c), 0, 0))],
        out_shape=[jax.ShapeDtypeStruct((T, SSD_W), BF16),
                   jax.ShapeDtypeStruct((B * nc * k, SSD_STATE, SSD_W), F32)],
        scratch_shapes=[pltpu.VMEM((SSD_STATE, SSD_W), F32)],
        compiler_params=_cp(("parallel", "arbitrary")),
    )(pre, proj, proj, par, e_mat, psel)


def _ssd_bwd(pre, proj, sall, dy, par, B, S, name):
    T = B * S
    k = _ssd_chunks_per_step(S)
    nc, rows = S // (CHUNK * k), CHUNK * k
    e_mat, psel = _ssd_consts()

    def body(pre_ref, z_ref, dt_ref, sall_ref, dy_ref, par_ref, e_ref, p_ref,
             dpre_ref, dz_ref, ddt_ref, dpar_ref, ds):
        b, c = pl.program_id(0), pl.program_id(1)

        @pl.when(c == 0)
        def _():
            ds[...] = jnp.zeros_like(ds)

        @pl.when((b == 0) & (c == 0))
        def _():
            dpar_ref[...] = jnp.zeros_like(dpar_ref)

        e_v, p_v = e_ref[...], p_ref[...]
        fn = lambda pre, z, dtr, sprev, par: _ssd_chunk(pre, z, dtr, sprev, par, e_v, p_v)
        dstate, dpar_sum = ds[...], None
        for i in reversed(range(k)):
            r = slice(CHUNK * i, CHUNK * (i + 1))
            _, vjp = jax.vjp(fn, pre_ref[r, :], z_ref[r, :], dt_ref[r, :], sall_ref[i], par_ref[...])
            dpre, dz, ddt, dstate, dpar = vjp((dy_ref[r, :], dstate))
            dpre_ref[r, :] = dpre
            dz_ref[r, :] = dz.astype(BF16)
            ddt_ref[r, :] = ddt.astype(BF16)
            dpar_sum = dpar if dpar_sum is None else dpar_sum + dpar
        dpar_ref[...] += dpar_sum
        ds[...] = dstate

    row = lambda b, c: b * nc + (nc - 1 - c)
    full = lambda shp: pl.BlockSpec(shp, lambda b, c: (0, 0))
    return pl.pallas_call(
        body, name=name, grid=(B, nc),
        in_specs=[pl.BlockSpec((rows, SSD_CONV_DIM), lambda b, c: (row(b, c), 0)),
                  pl.BlockSpec((rows, SSD_W), lambda b, c: (row(b, c), Z0 // SSD_W)),
                  pl.BlockSpec((rows, LANE), lambda b, c: (row(b, c), DT0 // LANE)),
                  pl.BlockSpec((k, SSD_STATE, SSD_W), lambda b, c: (row(b, c), 0, 0)),
                  pl.BlockSpec((rows, SSD_W), lambda b, c: (row(b, c), ATT_W // SSD_W)),
                  full((SUBLANE, SSD_W)), full((LANE, SSD_W)), full((SUBLANE, SSD_W))],
        out_specs=[pl.BlockSpec((rows, SSD_CONV_DIM), lambda b, c: (row(b, c), 0)),
                   pl.BlockSpec((rows, SSD_W), lambda b, c: (row(b, c), 0)),
                   pl.BlockSpec((rows, LANE), lambda b, c: (row(b, c), 0)),
                   full((SUBLANE, SSD_W))],
        out_shape=[jax.ShapeDtypeStruct((T, SSD_CONV_DIM), F32),
                   jax.ShapeDtypeStruct((T, SSD_W), BF16),
                   jax.ShapeDtypeStruct((T, LANE), BF16),
                   jax.ShapeDtypeStruct((SUBLANE, SSD_W), F32)],
        scratch_shapes=[pltpu.VMEM((SSD_STATE, SSD_W), F32)],
        compiler_params=_cp(("arbitrary", "arbitrary")),
    )(pre, proj, proj, sall, dy, par, e_mat, psel)


def _sgu_consts():
    e = np.zeros((SUBLANE, SGU_W), np.float32)
    for g in range(SGU_GROUPS):
        e[g, HEAD_DIM * g:HEAD_DIM * (g + 1)] = 1.0
    return jnp.asarray(e)


def _sgu_chunk(u_raw, v_raw, ln, w, bst, e4):
    L = CHUNK
    u = _gelu(u_raw)
    v = _gelu(v_raw)
    mu = jnp.mean(v, axis=-1, keepdims=True)
    vc = v - mu
    var = jnp.mean(vc * vc, axis=-1, keepdims=True)
    vn = vc * lax.rsqrt(var + LN_EPS) * ln[0:1] + ln[1:2]
    vb = vn.astype(BF16)
    ri = lax.broadcasted_iota(jnp.int32, (L, L), 0)
    ci = lax.broadcasted_iota(jnp.int32, (L, L), 1)
    tril = ri >= ci
    col = lax.broadcasted_iota(jnp.int32, (1, SGU_W), 1)
    mixed = _dot(bst, e4, HI)
    for g in range(SGU_GROUPS):
        wc = jnp.where(tril, w[g], 0.0).astype(BF16)
        gm = (col >= HEAD_DIM * g) & (col < HEAD_DIM * (g + 1))
        mixed = mixed + jnp.where(gm, _dot(wc, vb), 0.0)
    return u * mixed


def _sgu_fwd(proj, ln, w, bst, B, S, name):
    T = B * S
    nc = S // CHUNK
    e4 = _sgu_consts()

    def body(u_ref, v_ref, ln_ref, w_ref, b_ref, e_ref, y_ref):
        y_ref[...] = _sgu_chunk(u_ref[...], v_ref[...], ln_ref[...], w_ref[...], b_ref[...], e_ref[...]).astype(BF16)

    return pl.pallas_call(
        body, name=name, grid=(T // CHUNK,),
        in_specs=[pl.BlockSpec((CHUNK, SGU_W), lambda i: (i, U0 // SGU_W)),
                  pl.BlockSpec((CHUNK, SGU_W), lambda i: (i, VS0 // SGU_W)),
                  pl.BlockSpec((SUBLANE, SGU_W), lambda i: (0, 0)),
                  pl.BlockSpec((SGU_GROUPS, CHUNK, CHUNK), lambda i: (0, 0, 0)),
                  pl.BlockSpec((CHUNK, SUBLANE), lambda i: (0, 0)),
                  pl.BlockSpec((SUBLANE, SGU_W), lambda i: (0, 0))],
        out_specs=pl.BlockSpec((CHUNK, SGU_W), lambda i: (i, 0)),
        out_shape=jax.ShapeDtypeStruct((T, SGU_W), BF16),
        compiler_params=_cp(("parallel",)),
    )(proj, proj, ln, w, bst, e4)


def _sgu_bwd(proj, dy, ln, w, bst, B, S, name):
    T = B * S
    e4 = _sgu_consts()
    ycol = (ATT_W + SSD_W) // SGU_W

    def body(u_ref, v_ref, dy_ref, ln_ref, w_ref, b_ref, e_ref, du_ref, dv_ref, dln_ref, dw_ref, db_ref):
        @pl.when(pl.program_id(0) == 0)
        def _():
            dln_ref[...] = jnp.zeros_like(dln_ref)
            dw_ref[...] = jnp.zeros_like(dw_ref)
            db_ref[...] = jnp.zeros_like(db_ref)

        e_v = e_ref[...]
        fn = lambda u, v, ln, w, b: _sgu_chunk(u, v, ln, w, b, e_v)
        _, vjp = jax.vjp(fn, u_ref[...], v_ref[...], ln_ref[...], w_ref[...], b_ref[...])
        du, dv, dln, dw, db = vjp(dy_ref[...])
        du_ref[...] = du.astype(BF16)
        dv_ref[...] = dv.astype(BF16)
        dln_ref[...] += dln
        dw_ref[...] += dw
        db_ref[...] += db

    c_ln = pl.BlockSpec((SUBLANE, SGU_W), lambda i: (0, 0))
    c_w = pl.BlockSpec((SGU_GROUPS, CHUNK, CHUNK), lambda i: (0, 0, 0))
    c_b = pl.BlockSpec((CHUNK, SUBLANE), lambda i: (0, 0))
    return pl.pallas_call(
        body, name=name, grid=(T // CHUNK,),
        in_specs=[pl.BlockSpec((CHUNK, SGU_W), lambda i: (i, U0 // SGU_W)),
                  pl.BlockSpec((CHUNK, SGU_W), lambda i: (i, VS0 // SGU_W)),
                  pl.BlockSpec((CHUNK, SGU_W), lambda i: (i, ycol)),
                  c_ln, c_w, c_b, pl.BlockSpec((SUBLANE, SGU_W), lambda i: (0, 0))],
        out_specs=[pl.BlockSpec((CHUNK, SGU_W), lambda i: (i, 0)),
                   pl.BlockSpec((CHUNK, SGU_W), lambda i: (i, 0)), c_ln, c_w, c_b],
        out_shape=[jax.ShapeDtypeStruct((T, SGU_W), BF16), jax.ShapeDtypeStruct((T, SGU_W), BF16),
                   jax.ShapeDtypeStruct((SUBLANE, SGU_W), F32),
                   jax.ShapeDtypeStruct((SGU_GROUPS, CHUNK, CHUNK), F32),
                   jax.ShapeDtypeStruct((CHUNK, SUBLANE), F32)],
        compiler_params=_cp(("arbitrary",)),
    )(proj, proj, dy, ln, w, bst, e4)


_HBM = pl.BlockSpec(memory_space=pltpu.HBM)
_SEM = pl.BlockSpec(memory_space=pltpu.SEMAPHORE)
_ANY = pl.BlockSpec(memory_space=pl.ANY)
_EFFECT = pltpu.SideEffectType.DATAFLOW_SIDE_EFFECTING


def _peers():
    x, y, c = lax.axis_index("x"), lax.axis_index("y"), lax.axis_index("c")
    out = []
    for p in range(1, N_DEV):
        px, py, pc = x ^ ((p >> 2) & 1), y ^ ((p >> 1) & 1), c ^ (p & 1)
        out.append(((px, py, pc), 4 * px + 2 * py + pc))
    return 4 * x + 2 * y + c, out


def _xchg_start(xs, a2a, order, name):
    n = len(xs)
    lands = [lax.empty(a.shape if f else (N_DEV,) + a.shape, a.dtype) for a, f in zip(xs, a2a)]

    def body(*refs):
        ins, zones = refs[:n], refs[n:2 * n]
        send_sems, recv_sems = refs[2 * n + 1], refs[2 * n + 2]
        token = refs[-1]
        me, peers = _peers()
        for p, (dev, peer) in enumerate(peers):
            for t in range(n):
                pltpu.make_async_remote_copy(
                    src_ref=ins[t].at[peer] if a2a[t] else ins[t], dst_ref=zones[t].at[me],
                    send_sem=send_sems.at[p * n + t], recv_sem=recv_sems.at[p * n + t],
                    device_id=dev, device_id_type=MESH).start()
        token[...] = jnp.zeros_like(token)

    hbm = lambda a: pltpu.HBM(a.shape, a.dtype)
    sems = pltpu.SemaphoreType.DMA(((N_DEV - 1) * n,))
    out = pl.pallas_call(
        body, name=name,
        in_specs=[_HBM] * (2 * n) + [_ANY],
        out_specs=[_SEM, _SEM] + [_HBM] * (2 * n) + [pl.BlockSpec(memory_space=pltpu.VMEM)],
        out_shape=[sems, sems] + [hbm(a) for a in xs] + [hbm(a) for a in lands]
        + [jax.ShapeDtypeStruct((SUBLANE, LANE), F32)],
        input_output_aliases={t: 2 + t for t in range(2 * n)},
        compiler_params=pltpu.CompilerParams(has_side_effects=_EFFECT),
    )(*[pltpu.with_memory_space_constraint(a, pltpu.HBM) for a in list(xs) + list(lands)], order)
    return out[0], out[1], out[2:2 + n], out[2 + n:2 + 2 * n], out[-1]


def _xchg_wait(started, a2a, after, name):
    send_sems, recv_sems, xs, lands, _ = started
    n = len(xs)

    def body(*refs):
        ins, zones = refs[:n], refs[n:2 * n]
        send_s, recv_s = refs[2 * n], refs[2 * n + 1]
        me, peers = _peers()
        cps = []
        for p, (dev, peer) in enumerate(peers):
            for t in range(n):
                cps.append(pltpu.make_async_remote_copy(
                    src_ref=ins[t].at[peer] if a2a[t] else ins[t], dst_ref=zones[t].at[peer],
                    send_sem=send_s.at[p * n + t], recv_sem=recv_s.at[p * n + t],
                    device_id=dev, device_id_type=MESH))
        for cp in cps:
            cp.wait_recv()
        for cp in cps:
            cp.wait_send()

    hbm = lambda a: pltpu.HBM(a.shape, a.dtype)
    out = pl.pallas_call(
        body, name=name,
        in_specs=[_HBM] * (2 * n) + [_SEM, _SEM, _ANY],
        out_specs=[_HBM] * (2 * n),
        out_shape=[hbm(a) for a in xs] + [hbm(a) for a in lands],
        input_output_aliases={t: t for t in range(2 * n)},
        compiler_params=pltpu.CompilerParams(has_side_effects=_EFFECT),
    )(*xs, *lands, send_sems, recv_sems, after)
    return out[:n], out[n:]


def _chip_peers():
    x, y, c = lax.axis_index("x"), lax.axis_index("y"), lax.axis_index("c")
    chips = [(1 - x, y), (x, 1 - y), (1 - x, 1 - y)]
    slot = lambda px, py, pc: 4 * px + 2 * py + pc
    return (x, y, c), chips, slot


def _gather_start(xs, order, name, own_slot):
    n = len(xs)
    lands = [lax.empty((N_DEV,) + a.shape, a.dtype) for a in xs]

    def body(*refs):
        ins, zones = refs[:n], refs[n:2 * n]
        send_sems, d2d_sems, ici_sems = refs[2 * n + 1:2 * n + 4]
        token = refs[-1]
        (x, y, c), chips, slot = _chip_peers()
        me = slot(x, y, c)
        for t in range(n):
            if own_slot:
                pltpu.make_async_copy(ins[t], zones[t].at[me], d2d_sems.at[n + t]).start()
            for j, (px, py) in enumerate(chips):
                pltpu.make_async_remote_copy(
                    src_ref=ins[t], dst_ref=zones[t].at[me], send_sem=send_sems.at[(1 + j) * n + t],
                    recv_sem=ici_sems.at[j * n + t], device_id=(px, py, c), device_id_type=MESH).start()
            pltpu.make_async_remote_copy(
                src_ref=ins[t], dst_ref=zones[t].at[me], send_sem=send_sems.at[t],
                recv_sem=d2d_sems.at[t], device_id=(x, y, 1 - c), device_id_type=MESH).start()
        token[...] = jnp.zeros_like(token)

    hbm = lambda a: pltpu.HBM(a.shape, a.dtype)
    dma = lambda k: pltpu.SemaphoreType.DMA((k,))
    out = pl.pallas_call(
        body, name=name,
        in_specs=[_HBM] * (2 * n) + [_ANY],
        out_specs=[_SEM, _SEM, _SEM] + [_HBM] * (2 * n) + [pl.BlockSpec(memory_space=pltpu.VMEM)],
        out_shape=[dma(4 * n), dma(2 * n), dma(3 * n)] + [hbm(a) for a in xs] + [hbm(a) for a in lands]
        + [jax.ShapeDtypeStruct((SUBLANE, LANE), F32)],
        input_output_aliases={t: 3 + t for t in range(2 * n)},
        compiler_params=pltpu.CompilerParams(has_side_effects=_EFFECT),
    )(*[pltpu.with_memory_space_constraint(a, pltpu.HBM) for a in list(xs) + list(lands)], order)
    return dict(send=out[0], d2d=out[1], ici=out[2], xs=out[3:3 + n], lands=out[3 + n:3 + 2 * n], token=out[-1],
                own_slot=own_slot)


def _gather_relay(st, after, name):
    n = len(st["xs"])

    def body(*refs):
        zones, ici_sems = refs[:n], refs[n]
        fsend, frecv = refs[n + 2], refs[n + 3]
        token = refs[-1]
        (x, y, c), chips, slot = _chip_peers()
        for t in range(n):
            for j, (px, py) in enumerate(chips):
                blk = zones[t].at[slot(px, py, c)]
                fwd = pltpu.make_async_remote_copy(
                    src_ref=blk, dst_ref=blk, send_sem=fsend.at[j * n + t], recv_sem=ici_sems.at[j * n + t],
                    device_id=(x, y, 1 - c), device_id_type=MESH)
                fwd.wait_recv()
                pltpu.make_async_remote_copy(
                    src_ref=blk, dst_ref=blk, send_sem=fsend.at[j * n + t], recv_sem=frecv.at[j * n + t],
                    device_id=(x, y, 1 - c), device_id_type=MESH).start()
        token[...] = jnp.zeros_like(token)

    hbm = lambda a: pltpu.HBM(a.shape, a.dtype)
    dma = lambda k: pltpu.SemaphoreType.DMA((k,))
    out = pl.pallas_call(
        body, name=name,
        in_specs=[_HBM] * n + [_SEM, _ANY],
        out_specs=[_SEM, _SEM] + [_HBM] * n + [pl.BlockSpec(memory_space=pltpu.VMEM)],
        out_shape=[dma(3 * n), dma(3 * n)] + [hbm(a) for a in st["lands"]]
        + [jax.ShapeDtypeStruct((SUBLANE, LANE), F32)],
        input_output_aliases={t: 2 + t for t in range(n)},
        compiler_params=pltpu.CompilerParams(has_side_effects=_EFFECT),
    )(*st["lands"], st["ici"], after)
    return dict(st, fsend=out[0], frecv=out[1], lands=out[2:2 + n], token=out[-1])


def _gather_wait(st, after, name):
    n = len(st["xs"])

    def body(*refs):
        ins, zones = refs[:n], refs[n:2 * n]
        send_sems, d2d_sems, fsend, frecv = refs[2 * n:2 * n + 4]
        (x, y, c), chips, slot = _chip_peers()
        sib = (x, y, 1 - c)
        for t in range(n):
            if st["own_slot"]:
                pltpu.make_async_copy(ins[t], zones[t].at[slot(x, y, c)], d2d_sems.at[n + t]).wait()
            mine = lambda s, r, dst: pltpu.make_async_remote_copy(
                src_ref=ins[t], dst_ref=dst, send_sem=s, recv_sem=r, device_id=sib, device_id_type=MESH)
            direct = mine(send_sems.at[t], d2d_sems.at[t], zones[t].at[slot(x, y, 1 - c)])
            direct.wait_recv()
            direct.wait_send()
            for j, (px, py) in enumerate(chips):
                mine(send_sems.at[(1 + j) * n + t], d2d_sems.at[t], zones[t].at[slot(px, py, c)]).wait_send()
                relayed = mine(fsend.at[j * n + t], frecv.at[j * n + t], zones[t].at[slot(px, py, 1 - c)])
                relayed.wait_recv()
                relayed.wait_send()

    hbm = lambda a: pltpu.HBM(a.shape, a.dtype)
    out = pl.pallas_call(
        body, name=name,
        in_specs=[_HBM] * (2 * n) + [_SEM] * 4 + [_ANY],
        out_specs=[_HBM] * (2 * n),
        out_shape=[hbm(a) for a in st["xs"]] + [hbm(a) for a in st["lands"]],
        input_output_aliases={t: t for t in range(2 * n)},
        compiler_params=pltpu.CompilerParams(has_side_effects=_EFFECT),
    )(*st["xs"], *st["lands"], st["send"], st["d2d"], st["fsend"], st["frecv"], after)
    return out[:n], out[n:]


def _cast_layers(pairs, name):
    def body(*refs):
        n = len(refs) // 2
        for i in range(n):
            refs[n + i][...] = refs[i][...].astype(BF16)

    in_specs = [pl.BlockSpec((None,) + w.shape[1:], functools.partial(lambda l, i: (l, 0, 0), l),
                             pipeline_mode=pl.Buffered(1)) for w, l in pairs]
    return pl.pallas_call(
        body, name=name, grid=(1,), in_specs=in_specs,
        out_specs=[pl.BlockSpec(w.shape[1:], lambda i: (0, 0)) for w, _ in pairs],
        out_shape=[jax.ShapeDtypeStruct(w.shape[1:], BF16) for w, _ in pairs],
        compiler_params=_cp(("arbitrary",)),
    )(*[w for w, _ in pairs])


def _adamw(me, w, m, v, parts, own, name, layer=0, into=None):
    L, R, C = w.shape
    P = parts.shape[0]
    tr = R
    t = 16
    while t <= R:
        if R % t == 0 and t * C <= 131072:
            tr = t
        t += 16
    if tr == R and R * C > 131072 and R % 16 == 0:
        tr = 16
    own_all = own.shape[0] == P

    def body(me_ref, w_ref, m_ref, v_ref, p_ref, own_ref, *rest):
        g_ref, d_ref, mo_ref, vo_ref = rest[-4:]
        mine = own_ref[...].astype(F32)
        g = None
        for p in range(P):
            term = jnp.where(me_ref[0] == p, mine, p_ref[p].astype(F32))
            g = term if g is None else g + term
        mn = ADAM_B1 * m_ref[...] + (1.0 - ADAM_B1) * g
        vn = ADAM_B2 * v_ref[...] + (1.0 - ADAM_B2) * (g * g)
        m_hat = mn / (1.0 - ADAM_B1 ** ADAM_STEP)
        v_hat = vn / (1.0 - ADAM_B2 ** ADAM_STEP)
        g_ref[...] = g
        d_ref[...] = -ADAM_LR * (m_hat / (jnp.sqrt(v_hat) + ADAM_EPS) + ADAM_WD * w_ref[...])
        mo_ref[...] = mn
        vo_ref[...] = vn

    blk = pl.BlockSpec((None, tr, C), lambda i, me_ref: (layer, i, 0))
    own_blk = pl.BlockSpec((None, tr, C), lambda i, me_ref: (me_ref[0] if own_all else 0, i, 0))
    prev = list(into) if into is not None else []
    return pl.pallas_call(
        body, name=name,
        grid_spec=pltpu.PrefetchScalarGridSpec(
            num_scalar_prefetch=1, grid=(R // tr,),
            in_specs=[blk, blk, blk, pl.BlockSpec((P, tr, C), lambda i, me_ref: (0, i, 0)), own_blk]
            + [_ANY] * len(prev),
            out_specs=[blk] * 4),
        out_shape=[jax.ShapeDtypeStruct((L, R, C), F32)] * 4,
        input_output_aliases={6 + i: i for i in range(len(prev))},
        compiler_params=_cp(("parallel",)),
    )(me, w, m, v, parts, own, *prev)


def _perm_cols(w):
    pad = jnp.zeros((w.shape[0], LANE - SSD_HEADS), w.dtype)
    return jnp.concatenate([w[:, 0:1536], w[:, 2438:2694], w[:, 1536:2432], w[:, 2432:2438], pad,
                            w[:, 2694:2950]], axis=1)


def _unperm_cols(w):
    return jnp.concatenate([w[:, 0:1536], w[:, XBC0:XBC0 + SSD_CONV_DIM], w[:, DT0:DT0 + SSD_HEADS],
                            w[:, U0:U0 + SGU_W], w[:, VS0:VS0 + SGU_W]], axis=1)


_SMALL = ("ffn1_norm", "mix_norm", "conv_w", "conv_b", "dt_bias", "a_log", "d_skip", "ssd_norm",
          "sgu_ln_g", "sgu_ln_b", "sgu_w", "sgu_b", "ffn2_norm", "final_norm", "loss")


_SMALL_LAST = ("ffn1_norm",)
_SMALL_EARLY = tuple(k for k in _SMALL if k not in _SMALL_LAST)


def _pack(d, names):
    v = jnp.concatenate([d[k].astype(F32).reshape(-1) for k in names])
    n = v.shape[0]
    npad = -(-n // (LANE * 16)) * (LANE * 16)
    return jnp.pad(v, (0, npad - n)).reshape(npad // LANE, LANE)


def _unpack(p, shapes, names):
    v = p.reshape(-1)
    out, o = {}, 0
    for k in names:
        n = int(np.prod(shapes[k]))
        out[k] = v[o:o + n].reshape(shapes[k])
        o += n
    return out


def kernel(x, ffn1_norm, ffn1_w_gate, ffn1_w_up, ffn1_w_down, mix_norm, w_in, conv_w, conv_b, dt_bias, a_log, d_skip, ssd_norm, sgu_ln_g, sgu_ln_b, sgu_w, sgu_b, w_out, ffn2_norm, ffn2_w_gate, ffn2_w_up, ffn2_w_down, final_norm, loss_target, m_ffn1_norm, m_ffn1_w_gate, m_ffn1_w_up, m_ffn1_w_down, m_mix_norm, m_w_in, m_conv_w, m_conv_b, m_dt_bias, m_a_log, m_d_skip, m_ssd_norm, m_sgu_ln_g, m_sgu_ln_b, m_sgu_w, m_sgu_b, m_w_out, m_ffn2_norm, m_ffn2_w_gate, m_ffn2_w_up, m_ffn2_w_down, m_final_norm, v_ffn1_norm, v_ffn1_w_gate, v_ffn1_w_up, v_ffn1_w_down, v_mix_norm, v_w_in, v_conv_w, v_conv_b, v_dt_bias, v_a_log, v_d_skip, v_ssd_norm, v_sgu_ln_g, v_sgu_ln_b, v_sgu_w, v_sgu_b, v_w_out, v_ffn2_norm, v_ffn2_w_gate, v_ffn2_w_up, v_ffn2_w_down, v_final_norm):
    B, S, D = x.shape
    T = B * S
    L = ffn1_norm.shape[0]
    me = 4 * lax.axis_index("x") + 2 * lax.axis_index("y") + lax.axis_index("c")
    cs = conv_w.shape[2]
    W = dict(ffn1_norm=ffn1_norm, ffn1_w_gate=ffn1_w_gate, ffn1_w_up=ffn1_w_up, ffn1_w_down=ffn1_w_down,
             mix_norm=mix_norm, w_in=w_in, conv_w=conv_w, conv_b=conv_b, dt_bias=dt_bias, a_log=a_log,
             d_skip=d_skip, ssd_norm=ssd_norm, sgu_ln_g=sgu_ln_g, sgu_ln_b=sgu_ln_b, sgu_w=sgu_w, sgu_b=sgu_b,
             w_out=w_out, ffn2_norm=ffn2_norm, ffn2_w_gate=ffn2_w_gate, ffn2_w_up=ffn2_w_up,
             ffn2_w_down=ffn2_w_down, final_norm=final_norm)
    M = dict(ffn1_norm=m_ffn1_norm, ffn1_w_gate=m_ffn1_w_gate, ffn1_w_up=m_ffn1_w_up, ffn1_w_down=m_ffn1_w_down,
             mix_norm=m_mix_norm, w_in=m_w_in, conv_w=m_conv_w, conv_b=m_conv_b, dt_bias=m_dt_bias, a_log=m_a_log,
             d_skip=m_d_skip, ssd_norm=m_ssd_norm, sgu_ln_g=m_sgu_ln_g, sgu_ln_b=m_sgu_ln_b, sgu_w=m_sgu_w,
             sgu_b=m_sgu_b, w_out=m_w_out, ffn2_norm=m_ffn2_norm, ffn2_w_gate=m_ffn2_w_gate,
             ffn2_w_up=m_ffn2_w_up, ffn2_w_down=m_ffn2_w_down, final_norm=m_final_norm)
    V = dict(ffn1_norm=v_ffn1_norm, ffn1_w_gate=v_ffn1_w_gate, ffn1_w_up=v_ffn1_w_up, ffn1_w_down=v_ffn1_w_down,
             mix_norm=v_mix_norm, w_in=v_w_in, conv_w=v_conv_w, conv_b=v_conv_b, dt_bias=v_dt_bias, a_log=v_a_log,
             d_skip=v_d_skip, ssd_norm=v_ssd_norm, sgu_ln_g=v_sgu_ln_g, sgu_ln_b=v_sgu_ln_b, sgu_w=v_sgu_w,
             sgu_b=v_sgu_b, w_out=v_w_out, ffn2_norm=v_ffn2_norm, ffn2_w_gate=v_ffn2_w_gate,
             ffn2_w_up=v_ffn2_w_up, ffn2_w_down=v_ffn2_w_down, final_norm=v_final_norm)
    FFN1 = ("ffn1_w_gate", "ffn1_w_up", "ffn1_w_down")
    FFN2 = ("ffn2_w_gate", "ffn2_w_up", "ffn2_w_down")
    MIX = ("w_in", "w_out")
    big = FFN1 + MIX + FFN2
    col_sharded = lambda k: k.endswith("w_gate") or k.endswith("w_up")
    for dct in (W, M, V):
        for k in big:
            if col_sharded(k):
                dct[k] = jnp.swapaxes(dct[k], 1, 2)

    wgroups = [[(k, 0) for k in FFN1], [("w_in", 0), ("conv_w", None)], [("w_out", 0)] + [(k, 0) for k in FFN2]]
    wgroups += [[(k, l) for k in big] for l in range(1, L)]
    wstarted, order = [], x
    later = [kl for grp in wgroups[1:] for kl in grp if kl[0] != "conv_w"]
    cast = dict(zip(wgroups[0], _cast_layers([(W[k], l) for k, l in wgroups[0]], "cast_first")))
    for gi, grp in enumerate(wgroups):
        if gi == 1:
            first = lax.optimization_barrier((W[later[0][0]], order))[0]
            srcs = [(first if i == 0 else W[k], l) for i, (k, l) in enumerate(later)]
            cast.update(zip(later, _cast_layers(srcs, "cast_rest")))
        xs = [conv_w if k == "conv_w" else cast[(k, l)] for k, l in grp]
        st = _gather_start(xs, order, f"gather_start_{gi}", own_slot=True)
        order = st["token"]
        wstarted.append(st)
    G = {}
    is_me = (jnp.arange(N_DEV) == me)

    zero1 = jnp.zeros((1,), F32)
    W["loss"], M["loss"], V["loss"] = zero1, zero1, zero1
    full_shapes = {k: (W[k].shape if k != "conv_w" else (L, SSD_CONV, SSD_CONV_DIM)) for k in _SMALL}
    embed = lambda a, k: a if k != "conv_w" else lax.dynamic_update_slice(
        jnp.zeros(full_shapes[k], F32), a, (0, 0, me * cs))
    small_packs = {names: [_pack({k: embed(d[k], k) for k in names}, names)[None] for d in (W, M, V)]
                   for names in (_SMALL_EARLY, _SMALL_LAST)}

    def relay(gi, after):
        wstarted[gi] = _gather_relay(wstarted[gi], after, f"gather_relay_{gi}")
        return wstarted[gi]["token"]

    def gathered(gi, after):
        own, lands = _gather_wait(wstarted[gi], after, f"gather_wait_{gi}")
        for key, o, z in zip(wgroups[gi], own, lands):
            G[key] = z if wstarted[gi]["own_slot"] else jnp.where(
                is_me.reshape((N_DEV,) + (1,) * o.ndim), o[None], z)

    def rows(k, l):
        a = G[(k, l)]
        return a.reshape(-1, a.shape[-1])

    bias = _attn_bias(S, min(256, S))
    row1 = lambda a: a.reshape(1, -1)

    def ffn1_params(l):
        return dict(g1=row1(ffn1_norm[l]), wg1=rows("ffn1_w_gate", l), wu1=rows("ffn1_w_up", l),
                    wd1=rows("ffn1_w_down", l))

    def out_params(l):
        return dict(wout=rows("w_out", l), g2=row1(ffn2_norm[l]), wg2=rows("ffn2_w_gate", l),
                    wu2=rows("ffn2_w_up", l), wd2=rows("ffn2_w_down", l))

    def mix_params(l):
        cw = jnp.transpose(G[("conv_w", None)][:, l], (1, 0, 2)).reshape(SSD_CONV, -1)
        return dict(
            gm=row1(mix_norm[l]), win=_perm_cols(rows("w_in", l)),
            cw=jnp.pad(cw, ((0, SUBLANE - SSD_CONV), (0, 0))), cb=row1(conv_b[l]),
            par=jnp.pad(jnp.stack([jnp.repeat(dt_bias[l], HEAD_DIM), jnp.repeat(a_log[l], HEAD_DIM),
                                   jnp.repeat(d_skip[l], HEAD_DIM), ssd_norm[l]]), ((0, SUBLANE - 4), (0, 0))),
            ln=jnp.pad(jnp.stack([sgu_ln_g[l], sgu_ln_b[l]]), ((0, SUBLANE - 2), (0, 0))),
            sw=sgu_w[l], bst=jnp.pad(sgu_b[l].T, ((0, 0), (0, SUBLANE - SGU_GROUPS))))

    xc = x.reshape(T, D)
    saved, lay = [], []
    tie = lambda a, tok: lax.optimization_barrier((a, tok))[0]
    for l in range(L):
        gathered(0 if l == 0 else l + 2, relay(0, order) if l == 0 else xc)
        p = ffn1_params(l)
        x1, gate1, up1 = _ffn_fwd(xc, p["g1"], p["wg1"], p["wu1"], p["wd1"], f"ffn1_fwd_{l}")
        if l == 0:
            gathered(1, relay(1, x1))
        p.update(mix_params(l))
        lay.append(p)
        proj, ht = _norm_mm(x1, p["gm"], p["win"], f"in_proj_{l}")
        if l == 0:
            proj = tie(proj, relay(2, proj))
        o_att, lse = _attn_fwd(proj, bias, B, S, f"attn_fwd_{l}")
        pre = _conv_fwd(proj, p["cw"], p["cb"], B, S, f"conv_fwd_{l}")
        y_ssd, sall = _ssd_fwd(pre, proj, p["par"], B, S, f"ssd_fwd_{l}")
        y_sgu = _sgu_fwd(proj, p["ln"], p["sw"], p["bst"], B, S, f"sgu_fwd_{l}")
        ycat = jnp.concatenate([o_att.astype(BF16), y_ssd, y_sgu], axis=1)
        if l + 1 < L:
            ycat = tie(ycat, relay(l + 3, ycat))
        if l == 0:
            gathered(2, ycat)
        p.update(out_params(l))
        x2 = _mm(ycat, p["wout"], "nn", f"out_proj_{l}", residual=x1)
        x3, gate2, up2 = _ffn_fwd(x2, p["g2"], p["wg2"], p["wu2"], p["wd2"], f"ffn2_fwd_{l}")
        saved.append(dict(x0=xc, gate1=gate1, up1=up1, x1=x1, ht=ht, proj=proj, o_att=o_att, lse=lse, pre=pre,
                          sall=sall, ycat=ycat, x2=x2, gate2=gate2, up2=up2))
        xc = x3
    loss_part, dx, dgf = _final_loss(xc, row1(final_norm), loss_target.reshape(T, D), "final_loss")

    gl = [dict() for _ in range(L)]
    gstarted, gorder = [], [order]

    def to_blocks(k, a):
        return a.reshape(N_DEV, -1, a.shape[-1]).astype(BF16)

    def send_grads(keys, l, extra, tag, small_names=None):
        xs = [to_blocks(k, gl[l][k]) for k in keys] + extra
        flags = [True] * len(keys) + [False] * len(extra)
        st = _xchg_start(xs, flags, gorder[0], f"grads_start_{tag}")
        gorder[0] = st[-1]
        gstarted.append((keys, l, st, flags, tag, small_names))

    def small_grads(names):
        sm = {}
        for k in names:
            if k == "final_norm":
                sm[k] = dgf.reshape(-1)
            elif k == "loss":
                sm[k] = loss_part[0, :1]
            else:
                sm[k] = jnp.stack([gl[l][k] for l in range(L)])
        return _pack(sm, names)

    def behind(a):
        return lax.optimization_barrier((a, gorder[0]))[0]

    for l in reversed(range(L)):
        p, s, g = lay[l], saved[l], gl[l]
        dx2, dgt, dut, actt, xn, dacc, g["ffn2_norm"] = _ffn_bwd_dx(
            dx, s["x2"], p["g2"], s["gate2"], s["up2"], p["wg2"], p["wu2"], p["wd2"], f"ffn2_bwd_{l}")
        g["ffn2_w_gate"], g["ffn2_w_up"], g["ffn2_w_down"] = _ffn_dw(dgt, dut, actt, xn, dacc, f"ffn2_dw_{l}")
        if l == 0:
            send_grads(FFN2, 0, [], "l0f")
            dx2 = behind(dx2)
        dycat = _mm(dx2, p["wout"], "nt", f"out_proj_dx_{l}")
        g["w_out"] = _mm(s["ycat"], dx2, "tn", f"out_proj_dw_{l}", out_dtype=BF16, tm_cap=1024, tk_cap=512)
        dq, dk, dv = _attn_bwd(s["proj"], s["o_att"], s["lse"], dycat, bias, B, S, f"attn_bwd_{l}")
        dpre, dz, ddt, dpar = _ssd_bwd(s["pre"], s["proj"], s["sall"], dycat, p["par"], B, S, f"ssd_bwd_{l}")
        dxbc, dwb = _conv_bwd(dpre, s["proj"], p["cw"], B, S, f"conv_bwd_{l}")
        du, dvs, dln, dsw, dbst = _sgu_bwd(s["proj"], dycat, p["ln"], p["sw"], p["bst"], B, S, f"sgu_bwd_{l}")
        hsum = lambda r: r.reshape(SSD_HEADS, HEAD_DIM).sum(-1)
        g["conv_w"], g["conv_b"] = dwb[:SSD_CONV], dwb[SSD_CONV]
        g["dt_bias"], g["a_log"], g["d_skip"], g["ssd_norm"] = hsum(dpar[0]), hsum(dpar[1]), hsum(dpar[2]), dpar[3]
        g["sgu_ln_g"], g["sgu_ln_b"], g["sgu_w"], g["sgu_b"] = dln[0], dln[1], dsw, dbst[:, :SGU_GROUPS].T
        dproj = jnp.concatenate([dq, dk, dv, dz, du, dxbc, ddt, dvs], axis=1)
        g["w_in"] = _unperm_cols(_mm_resident_lhs(s["ht"], dproj, f"in_proj_dw_{l}"))
        dx1, g["mix_norm"] = _norm_mm_bwd(dproj, s["x1"], p["gm"], p["win"], dx2, f"in_proj_bwd_{l}")
        if l == 0:
            send_grads(MIX, 0, [small_grads(_SMALL_EARLY)], "l0a", _SMALL_EARLY)
            dx1, small_packs = lax.optimization_barrier((behind(dx1), small_packs))
        dx, dgt, dut, actt, xn, dacc, g["ffn1_norm"] = _ffn_bwd_dx(
            dx1, s["x0"], p["g1"], s["gate1"], s["up1"], p["wg1"], p["wu1"], p["wd1"], f"ffn1_bwd_{l}")
        g["ffn1_w_gate"], g["ffn1_w_up"], g["ffn1_w_down"] = _ffn_dw(dgt, dut, actt, xn, dacc, f"ffn1_dw_{l}")
        if l > 0:
            send_grads(big, l, [], f"l{l}")
            dx = behind(dx)
    grad_x = dx.reshape(B, S, D)
    send_grads(FFN1, 0, [small_grads(_SMALL_LAST)], "l0b", _SMALL_LAST)

    res, after = {}, gorder[0]
    small_out = [dict() for _ in range(4)]
    me1 = me.reshape(1).astype(jnp.int32)
    for keys, l, st, flags, tag, names in gstarted:
        own, lands = _xchg_wait(st, flags, after, f"grads_wait_{tag}")
        for k, mine, pk in zip(keys, own, lands):
            res[k] = _adamw(me1, W[k], M[k], V[k], pk, mine, f"adamw_{k}_{l}", layer=l, into=res.get(k))
        done = [res[k][0] for k in keys]
        if names:
            outs = _adamw(me1, *small_packs[names], lands[-1], own[-1][None], f"adamw_small_{tag}")
            for d, o in zip(small_out, outs):
                u = _unpack(o, full_shapes, names)
                if "conv_w" in u:
                    u["conv_w"] = lax.dynamic_slice(u["conv_w"], (0, 0, me * cs), (L, SSD_CONV, cs))
                d.update(u)
                done.extend(u.values())
        after = lax.optimization_barrier(tuple(done))[0]
    back = lambda k, a: jnp.swapaxes(a, 1, 2) if col_sharded(k) else a
    grads, deltas, new_m, new_v = [dict({k: back(k, res[k][i]) for k in big}, **small_out[i]) for i in range(4)]

    names = ("ffn1_norm", "ffn1_w_gate", "ffn1_w_up", "ffn1_w_down", "mix_norm", "w_in", "conv_w", "conv_b",
             "dt_bias", "a_log", "d_skip", "ssd_norm", "sgu_ln_g", "sgu_ln_b", "sgu_w", "sgu_b", "w_out",
             "ffn2_norm", "ffn2_w_gate", "ffn2_w_up", "ffn2_w_down", "final_norm")
    loss = grads["loss"][0]
    return (loss, grad_x, *[grads[n] for n in names], *[deltas[n] for n in names],
            *[new_m[n] for n in names], *[new_v[n] for n in names])
```

```python
import functools

import numpy as np
import jax
import jax.numpy as jnp
from jax import lax
from jax.experimental import pallas as pl
from jax.experimental.pallas import tpu as pltpu

F32, BF16 = jnp.float32, jnp.bfloat16
HI = lax.Precision.HIGH
MESH = pl.DeviceIdType.MESH
N_DEV = 8
VMEM_LIMIT_BYTES = 56 * 1024 * 1024
LANE, SUBLANE = 128, 8

HEAD_DIM = 64
ATT_W = 384
SSD_W = 384
SSD_HEADS = 6
SSD_STATE = 128
SSD_CONV = 4
CHUNK = 128
SSD_CONV_DIM = 896
SGU_W = 256
SGU_GROUPS = 4
D_IN = 2950
RMS_EPS = 1e-6
LN_EPS = 1e-5
NEG = -1e30

PW = 3072
Q0, K0, V0, Z0, U0, XBC0, DT0, VS0 = 0, 384, 768, 1152, 1536, 1792, 2688, 2816

ADAM_LR, ADAM_B1, ADAM_B2, ADAM_EPS, ADAM_WD, ADAM_STEP = 0.001, 0.9, 0.999, 1e-08, 0.01, 10


def _cp(sem=None):
    return pltpu.CompilerParams(dimension_semantics=sem, vmem_limit_bytes=VMEM_LIMIT_BYTES)


def _tile(n, cap, mult=LANE):
    best = None
    t = mult
    while t <= min(n, cap):
        if n % t == 0:
            best = t
        t += mult
    return best if best is not None else n


def _dot(a, b, prec=None):
    return jnp.dot(a, b, preferred_element_type=F32, precision=prec)


def _dot_nt(a, b, prec=None):
    return lax.dot_general(a, b, (((1,), (1,)), ((), ())), preferred_element_type=F32, precision=prec)


def _dot_tn(a, b, prec=None):
    return lax.dot_general(a, b, (((0,), (0,)), ((), ())), preferred_element_type=F32, precision=prec)


def _sigmoid(x):
    return 1.0 / (1.0 + jnp.exp(-x))


def _silu(x):
    return x * _sigmoid(x)


def _gelu(x):
    return 0.5 * x * (1.0 + lax.erf(x * 0.7071067811865476))


def _softplus(x):
    return jnp.maximum(x, 0.0) + jnp.log(1.0 + jnp.exp(-jnp.abs(x)))


def _rms_fwd(x, g):
    rstd = lax.rsqrt(jnp.mean(x * x, axis=-1, keepdims=True) + RMS_EPS)
    xhat = x * rstd
    return xhat * g, xhat, rstd


def _rms_bwd(dy, xhat, rstd, g):
    dxhat = dy * g
    dx = rstd * (dxhat - xhat * jnp.mean(dxhat * xhat, axis=-1, keepdims=True))
    return dx, dy * xhat


def _resident(shape):
    return pl.BlockSpec(shape, lambda *_: (0,) * len(shape), pipeline_mode=pl.Buffered(1))


def _mm(a, b, mode, name, out_dtype=F32, residual=None, tm_cap=512, tn_cap=1024, tk_cap=1024):
    if mode == "nn":
        (M, K), (_, N) = a.shape, b.shape
    elif mode == "nt":
        (M, K), (N, _) = a.shape, b.shape
    else:
        (K, M), (_, N) = a.shape, b.shape
    tm, tn, tk = _tile(M, tm_cap), _tile(N, tn_cap), _tile(K, tk_cap)
    nk = K // tk
    if mode == "tn":
        a_spec = pl.BlockSpec((tk, tm), lambda i, j, k: (k, i))
    else:
        a_spec = pl.BlockSpec((tm, tk), lambda i, j, k: (i, k))
    if mode == "nt":
        b_spec = pl.BlockSpec((tn, tk), lambda i, j, k: (j, k))
    else:
        b_spec = pl.BlockSpec((tk, tn), lambda i, j, k: (k, j))
    o_spec = pl.BlockSpec((tm, tn), lambda i, j, k: (i, j))
    has_res = residual is not None

    def prod(a_ref, b_ref):
        av = a_ref[...].astype(BF16)
        bv = b_ref[...].astype(BF16)
        if mode == "nn":
            return _dot(av, bv)
        if mode == "nt":
            return _dot_nt(av, bv)
        return _dot_tn(av, bv)

    def body(*refs):
        a_ref, b_ref = refs[:2]
        r_ref = refs[2] if has_res else None
        o_ref = refs[2 + has_res]
        if nk == 1:
            o = prod(a_ref, b_ref)
            if has_res:
                o = r_ref[...] + o
            o_ref[...] = o.astype(out_dtype)
            return
        acc = refs[3 + has_res]
        k = pl.program_id(2)

        @pl.when(k == 0)
        def _():
            acc[...] = jnp.zeros_like(acc)

        acc[...] += prod(a_ref, b_ref)

        @pl.when(k == nk - 1)
        def _():
            o = acc[...]
            if has_res:
                o = r_ref[...] + o
            o_ref[...] = o.astype(out_dtype)

    ins = [a, b] + ([residual] if has_res else [])
    in_specs = [a_spec, b_spec] + ([o_spec] if has_res else [])
    return pl.pallas_call(
        body, name=name, grid=(M // tm, N // tn, nk),
        in_specs=in_specs, out_specs=o_spec,
        out_shape=jax.ShapeDtypeStruct((M, N), out_dtype),
        scratch_shapes=[pltpu.VMEM((tm, tn), F32)] if nk > 1 else [],
        compiler_params=_cp(("parallel", "parallel", "arbitrary")),
    )(*ins)


def _ffn_fwd(x, g, wgt, wut, wd, name):
    T, D = x.shape
    F = wgt.shape[0]
    tm = _tile(T, 256)

    def body(x_ref, g_ref, wg_ref, wu_ref, wd_ref, out_ref, dgf_ref, sl_ref, actt_ref):
        xv = x_ref[...]
        xn = _rms_fwd(xv, g_ref[...])[0].astype(BF16)
        gate = _dot_nt(xn, wg_ref[...])
        up = _dot_nt(xn, wu_ref[...])
        sig = _sigmoid(gate)
        sl = gate * sig
        dgf_ref[...] = (up * (sig * (1.0 + gate * (1.0 - sig)))).astype(BF16)
        sl_ref[...] = sl.astype(BF16)
        act = (sl * up).astype(BF16)
        actt_ref[...] = act.T
        out_ref[...] = xv + 0.5 * _dot(act, wd_ref[...])

    row = lambda w: pl.BlockSpec((tm, w), lambda i: (i, 0))
    return pl.pallas_call(
        body, name=name, grid=(T // tm,),
        in_specs=[row(D), _resident((1, D)), _resident((F, D)), _resident((F, D)), _resident((F, D))],
        out_specs=[row(D), row(F), row(F), pl.BlockSpec((F, tm), lambda i: (0, i))],
        out_shape=[jax.ShapeDtypeStruct((T, D), F32),
                   jax.ShapeDtypeStruct((T, F), BF16),
                   jax.ShapeDtypeStruct((T, F), BF16),
                   jax.ShapeDtypeStruct((F, T), BF16)],
        compiler_params=_cp(("parallel",)),
    )(x, g, wgt, wut, wd)


def _ffn_bwd_dx(dout, x, g, dgf, sl, wg, wu, wd, name):
    T, D = x.shape
    F = wg.shape[0]
    tm = _tile(T, 256)

    def body(dout_ref, x_ref, g_ref, dgf_ref, sl_ref, wg_ref, wu_ref, wd_ref,
             dx_ref, dgt_ref, dut_ref, xn_ref, dacc_ref, dg_ref):
        @pl.when(pl.program_id(0) == 0)
        def _():
            dg_ref[...] = jnp.zeros_like(dg_ref)

        gv = g_ref[...]
        dout_v = dout_ref[...]
        xn, xhat, rstd = _rms_fwd(x_ref[...], gv)
        xn_ref[...] = xn.astype(BF16)
        dacc = (0.5 * dout_v).astype(BF16)
        dacc_ref[...] = dacc
        dact = _dot_nt(dacc, wd_ref[...])
        dgate = (dact * dgf_ref[...].astype(F32)).astype(BF16)
        dup = (dact * sl_ref[...].astype(F32)).astype(BF16)
        dgt_ref[...] = dgate.T
        dut_ref[...] = dup.T
        dxn = _dot(dgate, wg_ref[...]) + _dot(dup, wu_ref[...])
        dx, dgrow = _rms_bwd(dxn, xhat, rstd, gv)
        dx_ref[...] = dout_v + dx
        dg_ref[...] += jnp.sum(dgrow, axis=0, keepdims=True)

    row = lambda w: pl.BlockSpec((tm, w), lambda i: (i, 0))
    tr = pl.BlockSpec((F, tm), lambda i: (0, i))
    return pl.pallas_call(
        body, name=name, grid=(T // tm,),
        in_specs=[row(D), row(D), _resident((1, D)), row(F), row(F),
                  _resident((F, D)), _resident((F, D)), _resident((F, D))],
        out_specs=[row(D), tr, tr, row(D), row(D), pl.BlockSpec((1, D), lambda i: (0, 0))],
        out_shape=[jax.ShapeDtypeStruct((T, D), F32)] + [jax.ShapeDtypeStruct((F, T), BF16)] * 2
        + [jax.ShapeDtypeStruct((T, D), BF16)] * 2 + [jax.ShapeDtypeStruct((1, D), F32)],
        compiler_params=_cp(("arbitrary",)),
    )(dout, x, g, dgf, sl, wg, wu, wd)


def _ffn_dw(dgt, dut, actt, xn, dacc, name):
    F, T = dgt.shape
    D = xn.shape[1]
    th = _tile(F, 256)

    def body(dg_ref, du_ref, a_ref, xn_ref, dacc_ref, dwg_ref, dwu_ref, dwd_ref):
        xv = xn_ref[...]
        dwg_ref[...] = _dot(dg_ref[...], xv).astype(BF16)
        dwu_ref[...] = _dot(du_ref[...], xv).astype(BF16)
        dwd_ref[...] = _dot(a_ref[...], dacc_ref[...]).astype(BF16)

    tile = pl.BlockSpec((th, T), lambda j: (j, 0))
    out = pl.BlockSpec((th, D), lambda j: (j, 0))
    return pl.pallas_call(
        body, name=name, grid=(F // th,),
        in_specs=[tile, tile, tile, _resident((T, D)), _resident((T, D))],
        out_specs=[out, out, out], out_shape=[jax.ShapeDtypeStruct((F, D), BF16)] * 3,
        compiler_params=_cp(("parallel",)),
    )(dgt, dut, actt, xn, dacc)


def _norm_mm(x, g, w, name):
    T, D = x.shape
    N = w.shape[1]
    tm = _tile(T, 512)

    def body(x_ref, g_ref, w_ref, o_ref, ht_ref):
        xn = _rms_fwd(x_ref[...], g_ref[...])[0]
        ht_ref[...] = xn.T.astype(BF16)
        o_ref[...] = _dot(xn.astype(BF16), w_ref[...])

    return pl.pallas_call(
        body, name=name, grid=(T // tm,),
        in_specs=[pl.BlockSpec((tm, D), lambda i: (i, 0)), _resident((1, D)), _resident((D, N))],
        out_specs=[pl.BlockSpec((tm, N), lambda i: (i, 0)), pl.BlockSpec((D, tm), lambda i: (0, i))],
        out_shape=[jax.ShapeDtypeStruct((T, N), F32), jax.ShapeDtypeStruct((D, T), BF16)],
        compiler_params=_cp(("parallel",)),
    )(x, g, w)


def _norm_mm_bwd(dproj, x, g, w, dres, name):
    T, D = x.shape
    N = w.shape[1]
    tm = _tile(T, 512)

    def body(dp_ref, x_ref, g_ref, w_ref, dres_ref, dx_ref, dg_ref):
        @pl.when(pl.program_id(0) == 0)
        def _():
            dg_ref[...] = jnp.zeros_like(dg_ref)

        gv = g_ref[...]
        dh = _dot_nt(dp_ref[...], w_ref[...])
        _, xhat, rstd = _rms_fwd(x_ref[...], gv)
        dx, dgrow = _rms_bwd(dh, xhat, rstd, gv)
        dx_ref[...] = dres_ref[...] + dx
        dg_ref[...] += jnp.sum(dgrow, axis=0, keepdims=True)

    row = pl.BlockSpec((tm, D), lambda i: (i, 0))
    one = pl.BlockSpec((1, D), lambda i: (0, 0))
    return pl.pallas_call(
        body, name=name, grid=(T // tm,),
        in_specs=[pl.BlockSpec((tm, N), lambda i: (i, 0)), row, _resident((1, D)), _resident((D, N)), row],
        out_specs=[row, one],
        out_shape=[jax.ShapeDtypeStruct((T, D), F32), jax.ShapeDtypeStruct((1, D), F32)],
        compiler_params=_cp(("arbitrary",)),
    )(dproj, x, g, w, dres)


def _mm_resident_lhs(at, b, name, tn_cap=512):
    M, K = at.shape
    N = b.shape[1]
    tn = _tile(N, tn_cap)

    def body(a_ref, b_ref, o_ref):
        o_ref[...] = _dot(a_ref[...], b_ref[...]).astype(BF16)

    return pl.pallas_call(
        body, name=name, grid=(N // tn,),
        in_specs=[_resident((M, K)), pl.BlockSpec((K, tn), lambda j: (0, j))],
        out_specs=pl.BlockSpec((M, tn), lambda j: (0, j)),
        out_shape=jax.ShapeDtypeStruct((M, N), BF16),
        compiler_params=_cp(("parallel",)),
    )(at, b)


def _final_loss(x, g, target, name):
    T, D = x.shape
    tm = _tile(T, 512)

    def body(x_ref, g_ref, t_ref, loss_ref, dx_ref, dg_ref):
        @pl.when(pl.program_id(0) == 0)
        def _():
            dg_ref[...] = jnp.zeros_like(dg_ref)
            loss_ref[...] = jnp.zeros_like(loss_ref)

        gv = g_ref[...]
        y, xhat, rstd = _rms_fwd(x_ref[...], gv)
        err = y - t_ref[...]
        part = 0.5 * jnp.sum(jnp.mean(err * err, axis=-1, keepdims=True), axis=0, keepdims=True)
        loss_ref[...] += jnp.broadcast_to(part, loss_ref.shape)
        dy = err * (1.0 / D)
        dx, dgrow = _rms_bwd(dy, xhat, rstd, gv)
        dx_ref[...] = dx
        dg_ref[...] += jnp.sum(dgrow, axis=0, keepdims=True)

    row = pl.BlockSpec((tm, D), lambda i: (i, 0))
    one = pl.BlockSpec((1, D), lambda i: (0, 0))
    return pl.pallas_call(
        body, name=name, grid=(T // tm,),
        in_specs=[row, one, row],
        out_specs=[pl.BlockSpec((1, LANE), lambda i: (0, 0)), row, one],
        out_shape=[jax.ShapeDtypeStruct((1, LANE), F32), jax.ShapeDtypeStruct((T, D), F32),
                   jax.ShapeDtypeStruct((1, D), F32)],
        compiler_params=_cp(("arbitrary",)),
    )(x, g, target)


def _attn_bias(S, bq):
    d = jnp.arange(bq)[:, None] - jnp.arange(S)[None, :] + (S // bq - 1) * bq
    ok = d >= 0
    mult = ((ok & (d <= 128)).astype(F32) + (ok & (d % 4 == 0) & (d <= 512)).astype(F32)
            + (ok & (d % 16 == 0) & (d <= 2048)).astype(F32))
    return jnp.where(mult > 0, jnp.log(jnp.maximum(mult, 1.0)), NEG).astype(F32)


def _attn_fwd(proj, bias, B, S, name):
    T = B * S
    bq = bias.shape[0]
    nb = S // bq
    qcol, kcol, vcol = Q0 // LANE, K0 // LANE, V0 // LANE

    def body(q_ref, k_ref, v_ref, t_ref, o_ref, lse_ref, ks, vs):
        for hh in range(2):
            sl = slice(HEAD_DIM * hh, HEAD_DIM * (hh + 1))
            ks[hh] = k_ref[:, sl].astype(BF16)
            vs[hh] = v_ref[:, sl].astype(BF16)
        for hh in range(2):
            sl = slice(HEAD_DIM * hh, HEAD_DIM * (hh + 1))
            for qb in range(nb):
                w, off, rows = bq * (qb + 1), (nb - 1 - qb) * bq, slice(qb * bq, (qb + 1) * bq)
                q = (q_ref[rows, sl] * 0.125).astype(BF16)
                s = _dot_nt(q, ks[hh, 0:w, :]) + t_ref[:, off:off + w]
                m = jnp.max(s, axis=-1, keepdims=True)
                p = jnp.exp(s - m)
                l = jnp.sum(p, axis=-1, keepdims=True)
                o_ref[rows, sl] = _dot(p.astype(BF16), vs[hh, 0:w, :]) / l
                lse_ref[rows, hh:hh + 1] = m + jnp.log(l)

    blk = lambda c0: pl.BlockSpec((S, LANE), lambda b, p: (b, c0 + p))
    return pl.pallas_call(
        body, name=name, grid=(B, ATT_W // LANE),
        in_specs=[blk(qcol), blk(kcol), blk(vcol), _resident((bq, S))],
        out_specs=[pl.BlockSpec((S, LANE), lambda b, p: (b, p)),
                   pl.BlockSpec((None, None, S, 2), lambda b, p: (b, p, 0, 0))],
        out_shape=[jax.ShapeDtypeStruct((T, ATT_W), F32),
                   jax.ShapeDtypeStruct((B, ATT_W // LANE, S, 2), F32)],
        scratch_shapes=[pltpu.VMEM((2, S, HEAD_DIM), BF16)] * 2,
        compiler_params=_cp(("parallel", "parallel")),
    )(proj, proj, proj, bias)


def _attn_bwd(proj, o, lse, dy, bias, B, S, name):
    T = B * S
    bq = bias.shape[0]
    nb = S // bq
    qcol, kcol, vcol = Q0 // LANE, K0 // LANE, V0 // LANE

    def body(q_ref, k_ref, v_ref, o_ref, lse_ref, do_ref, t_ref, dq_ref, dk_ref, dv_ref, ks, vs, dks, dvs):
        for hh in range(2):
            sl = slice(HEAD_DIM * hh, HEAD_DIM * (hh + 1))
            ks[hh] = k_ref[:, sl].astype(BF16)
            vs[hh] = v_ref[:, sl].astype(BF16)
        dks[...] = jnp.zeros_like(dks)
        dvs[...] = jnp.zeros_like(dvs)
        for hh in range(2):
            sl = slice(HEAD_DIM * hh, HEAD_DIM * (hh + 1))
            for qb in range(nb):
                w, off, rows = bq * (qb + 1), (nb - 1 - qb) * bq, slice(qb * bq, (qb + 1) * bq)
                q = (q_ref[rows, sl] * 0.125).astype(BF16)
                do = do_ref[rows, sl]
                dob = do.astype(BF16)
                delta = jnp.sum(do * o_ref[rows, sl], axis=-1, keepdims=True)
                k, v = ks[hh, 0:w, :], vs[hh, 0:w, :]
                s = _dot_nt(q, k) + t_ref[:, off:off + w]
                p = jnp.exp(s - lse_ref[rows, hh:hh + 1])
                ds = (p * (_dot_nt(dob, v) - delta)).astype(BF16)
                dq_ref[rows, sl] = (_dot(ds, k) * 0.125).astype(dq_ref.dtype)
                dks[hh, 0:w, :] += _dot_tn(ds, q)
                dvs[hh, 0:w, :] += _dot_tn(p.astype(BF16), dob)
            dk_ref[:, sl] = dks[hh].astype(dk_ref.dtype)
            dv_ref[:, sl] = dvs[hh].astype(dv_ref.dtype)

    blk = lambda c0: pl.BlockSpec((S, LANE), lambda b, p: (b, c0 + p))
    own = pl.BlockSpec((S, LANE), lambda b, p: (b, p))
    return pl.pallas_call(
        body, name=name, grid=(B, ATT_W // LANE),
        in_specs=[blk(qcol), blk(kcol), blk(vcol), own,
                  pl.BlockSpec((None, None, S, 2), lambda b, p: (b, p, 0, 0)), own, _resident((bq, S))],
        out_specs=[own, own, own],
        out_shape=[jax.ShapeDtypeStruct((T, ATT_W), BF16)] * 3,
        scratch_shapes=[pltpu.VMEM((2, S, HEAD_DIM), BF16)] * 2 + [pltpu.VMEM((2, S, HEAD_DIM), F32)] * 2,
        compiler_params=_cp(("parallel", "parallel")),
    )(proj, proj, proj, o, lse, dy, bias)


def _conv_fwd(proj, cw, cb, B, S, name):
    T = B * S
    nc = SSD_CONV_DIM // LANE
    c0 = XBC0 // LANE

    def body(x_ref, w_ref, b_ref, o_ref):
        x = x_ref[...]
        t = lax.broadcasted_iota(jnp.int32, (S, 1), 0)
        acc = b_ref[...] + w_ref[SSD_CONV - 1:SSD_CONV, :] * x
        for k in range(SSD_CONV - 1):
            sh = SSD_CONV - 1 - k
            xs = jnp.where(t >= sh, pltpu.roll(x, sh, 0), 0.0)
            acc = acc + w_ref[k:k + 1, :] * xs
        o_ref[...] = acc

    return pl.pallas_call(
        body, name=name, grid=(B, nc),
        in_specs=[pl.BlockSpec((S, LANE), lambda b, j: (b, c0 + j)),
                  pl.BlockSpec((SUBLANE, LANE), lambda b, j: (0, j)),
                  pl.BlockSpec((1, LANE), lambda b, j: (0, j))],
        out_specs=pl.BlockSpec((S, LANE), lambda b, j: (b, j)),
        out_shape=jax.ShapeDtypeStruct((T, SSD_CONV_DIM), F32),
        compiler_params=_cp(("parallel", "parallel")),
    )(proj, cw, cb)


def _conv_bwd(dpre, proj, cw, B, S, name):
    T = B * S
    nc = SSD_CONV_DIM // LANE
    c0 = XBC0 // LANE

    def body(d_ref, x_ref, w_ref, dx_ref, dwb_ref):
        @pl.when(pl.program_id(1) == 0)
        def _():
            dwb_ref[...] = jnp.zeros_like(dwb_ref)

        d = d_ref[...]
        x = x_ref[...]
        t = lax.broadcasted_iota(jnp.int32, (S, 1), 0)
        dx = w_ref[SSD_CONV - 1:SSD_CONV, :] * d
        rows = [None] * SUBLANE
        rows[SSD_CONV - 1] = jnp.sum(d * x, axis=0, keepdims=True)
        for k in range(SSD_CONV - 1):
            sh = SSD_CONV - 1 - k
            dx = dx + w_ref[k:k + 1, :] * jnp.where(t < S - sh, pltpu.roll(d, S - sh, 0), 0.0)
            xs = jnp.where(t >= sh, pltpu.roll(x, sh, 0), 0.0)
            rows[k] = jnp.sum(d * xs, axis=0, keepdims=True)
        rows[SSD_CONV] = jnp.sum(d, axis=0, keepdims=True)
        dx_ref[...] = dx.astype(BF16)
        r = lax.broadcasted_iota(jnp.int32, (SUBLANE, LANE), 0)
        upd = jnp.zeros((SUBLANE, LANE), F32)
        for k in range(SSD_CONV + 1):
            upd = upd + jnp.where(r == k, rows[k], 0.0)
        dwb_ref[...] += upd

    return pl.pallas_call(
        body, name=name, grid=(nc, B),
        in_specs=[pl.BlockSpec((S, LANE), lambda j, b: (b, j)),
                  pl.BlockSpec((S, LANE), lambda j, b: (b, c0 + j)),
                  pl.BlockSpec((SUBLANE, LANE), lambda j, b: (0, j))],
        out_specs=[pl.BlockSpec((S, LANE), lambda j, b: (b, j)),
                   pl.BlockSpec((SUBLANE, LANE), lambda j, b: (0, j))],
        out_shape=[jax.ShapeDtypeStruct((T, SSD_CONV_DIM), BF16),
                   jax.ShapeDtypeStruct((SUBLANE, SSD_CONV_DIM), F32)],
        compiler_params=_cp(("parallel", "arbitrary")),
    )(dpre, proj, cw)


def _ssd_consts():
    e = np.zeros((LANE, SSD_W), np.float32)
    p = np.zeros((SUBLANE, SSD_W), np.float32)
    for h in range(SSD_HEADS):
        e[h, HEAD_DIM * h:HEAD_DIM * (h + 1)] = 1.0
        p[h, HEAD_DIM * h] = 1.0
    return jnp.asarray(e), jnp.asarray(p)


def _ssd_chunk(pre, z, dtr, sprev, par, e_mat, psel):
    L = CHUNK
    xc = _silu(pre)
    xs, bm, cm = xc[:, :SSD_W], xc[:, SSD_W:SSD_W + 2 * SSD_STATE], xc[:, SSD_W + 2 * SSD_STATE:]
    dtb, alog, dskip, ng = par[0:1], par[1:2], par[2:3], par[3:4]
    dt = _softplus(_dot(dtr, e_mat, HI) + dtb)
    a = dt * (-jnp.exp(alog))
    X = xs * dt
    ri = lax.broadcasted_iota(jnp.int32, (L, L), 0)
    ci = lax.broadcasted_iota(jnp.int32, (L, L), 1)
    tril = ri >= ci
    acs = _dot(tril.astype(F32), a, HI)
    acs_t = _dot_nt(psel, acs, HI)
    ecs = jnp.exp(acs)
    alast = acs[L - 1:L, :]
    xd = (X * jnp.exp(alast - acs)).astype(BF16)
    xb = X.astype(BF16)
    col = lax.broadcasted_iota(jnp.int32, (1, SSD_W), 1)
    sb = sprev.astype(BF16)
    bgs = [bm[:, SSD_STATE * g:SSD_STATE * (g + 1)].astype(BF16) for g in range(2)]
    cgs = [cm[:, SSD_STATE * g:SSD_STATE * (g + 1)].astype(BF16) for g in range(2)]
    cbs = [_dot_nt(cgs[g], bgs[g]) for g in range(2)]
    first = lax.broadcasted_iota(jnp.int32, (1, LANE), 1) < HEAD_DIM
    y_tiles, s_tiles = [], []
    for t in range(SSD_W // LANE):
        cl = slice(LANE * t, LANE * (t + 1))
        xb_t, xd_t, sb_t = xb[:, cl], xd[:, cl], sb[:, cl]
        per_head = []
        for h in (2 * t, 2 * t + 1):
            seg = acs[:, HEAD_DIM * h:HEAD_DIM * h + 1] - acs_t[h:h + 1, :]
            dec = jnp.exp(jnp.where(tril, seg, NEG))
            per_head.append(_dot((cbs[h // 3] * dec).astype(BF16), xb_t))
        y_t = jnp.where(first, per_head[0], per_head[1])
        ga, gb = (2 * t) // 3, (2 * t + 1) // 3
        if ga == gb:
            y_off, s_add = _dot(cgs[ga], sb_t), _dot_tn(bgs[ga], xd_t)
        else:
            y_off = jnp.where(first, _dot(cgs[ga], sb_t), _dot(cgs[gb], sb_t))
            s_add = jnp.where(first, _dot_tn(bgs[ga], xd_t), _dot_tn(bgs[gb], xd_t))
        y_tiles.append(y_t + y_off * ecs[:, cl])
        s_tiles.append(s_add)
    y = dskip * xs + jnp.concatenate(y_tiles, axis=1)
    snew = sprev * jnp.exp(alast) + jnp.concatenate(s_tiles, axis=1)
    yg = y * _silu(z)
    sq = yg * yg
    g0 = col < SSD_W // 2
    ms0 = jnp.sum(jnp.where(g0, sq, 0.0), axis=-1, keepdims=True) * (2.0 / SSD_W)
    ms1 = jnp.sum(jnp.where(g0, 0.0, sq), axis=-1, keepdims=True) * (2.0 / SSD_W)
    r = jnp.where(g0, lax.rsqrt(ms0 + RMS_EPS), lax.rsqrt(ms1 + RMS_EPS))
    return yg * r * ng, snew


SSD_CHUNKS_PER_STEP = 2


def _ssd_chunks_per_step(S):
    k = SSD_CHUNKS_PER_STEP
    while (S // CHUNK) % k:
        k //= 2
    return k


def _ssd_fwd(pre, proj, par, B, S, name):
    T = B * S
    k = _ssd_chunks_per_step(S)
    nc, rows = S // (CHUNK * k), CHUNK * k
    e_mat, psel = _ssd_consts()

    def body(pre_ref, z_ref, dt_ref, par_ref, e_ref, p_ref, y_ref, sall_ref, st):
        @pl.when(pl.program_id(1) == 0)
        def _():
            st[...] = jnp.zeros_like(st)

        sprev = st[...]
        for i in range(k):
            r = slice(CHUNK * i, CHUNK * (i + 1))
            sall_ref[i] = sprev
            y, sprev = _ssd_chunk(pre_ref[r, :], z_ref[r, :], dt_ref[r, :], sprev, par_ref[...], e_ref[...],
                                  p_ref[...])
            y_ref[r, :] = y.astype(BF16)
        st[...] = sprev

    row = lambda b, c: b * nc + c
    full = lambda shp: pl.BlockSpec(shp, lambda b, c: (0, 0))
    return pl.pallas_call(
        body, name=name, grid=(B, nc),
        in_specs=[pl.BlockSpec((rows, SSD_CONV_DIM), lambda b, c: (row(b, c), 0)),
                  pl.BlockSpec((rows, SSD_W), lambda b, c: (row(b, c), Z0 // SSD_W)),
                  pl.BlockSpec((rows, LANE), lambda b, c: (row(b, c), DT0 // LANE)),
                  full((SUBLANE, SSD_W)), full((LANE, SSD_W)), full((SUBLANE, SSD_W))],
        out_specs=[pl.BlockSpec((rows, SSD_W), lambda b, c: (row(b, c), 0)),
                   pl.BlockSpec((k, SSD_STATE, SSD_W), lambda b, c: (row(b, c), 0, 0))],
        out_shape=[jax.ShapeDtypeStruct((T, SSD_W), BF16),
                   jax.ShapeDtypeStruct((B * nc * k, SSD_STATE, SSD_W), F32)],
        scratch_shapes=[pltpu.VMEM((SSD_STATE, SSD_W), F32)],
        compiler_params=_cp(("parallel", "arbitrary")),
    )(pre, proj, proj, par, e_mat, psel)


def _ssd_bwd(pre, proj, sall, dy, par, B, S, name):
    T = B * S
    k = _ssd_chunks_per_step(S)
    nc, rows = S // (CHUNK * k), CHUNK * k
    e_mat, psel = _ssd_consts()

    def body(pre_ref, z_ref, dt_ref, sall_ref, dy_ref, par_ref, e_ref, p_ref,
             dpre_ref, dz_ref, ddt_ref, dpar_ref, ds):
        b, c = pl.program_id(0), pl.program_id(1)

        @pl.when(c == 0)
        def _():
            ds[...] = jnp.zeros_like(ds)

        @pl.when((b == 0) & (c == 0))
        def _():
            dpar_ref[...] = jnp.zeros_like(dpar_ref)

        e_v, p_v = e_ref[...], p_ref[...]
        fn = lambda pre, z, dtr, sprev, par: _ssd_chunk(pre, z, dtr, sprev, par, e_v, p_v)
        dstate, dpar_sum = ds[...], None
        for i in reversed(range(k)):
            r = slice(CHUNK * i, CHUNK * (i + 1))
            _, vjp = jax.vjp(fn, pre_ref[r, :], z_ref[r, :], dt_ref[r, :], sall_ref[i], par_ref[...])
            dpre, dz, ddt, dstate, dpar = vjp((dy_ref[r, :], dstate))
            dpre_ref[r, :] = dpre
            dz_ref[r, :] = dz.astype(BF16)
            ddt_ref[r, :] = ddt.astype(BF16)
            dpar_sum = dpar if dpar_sum is None else dpar_sum + dpar
        dpar_ref[...] += dpar_sum
        ds[...] = dstate

    row = lambda b, c: b * nc + (nc - 1 - c)
    full = lambda shp: pl.BlockSpec(shp, lambda b, c: (0, 0))
    return pl.pallas_call(
        body, name=name, grid=(B, nc),
        in_specs=[pl.BlockSpec((rows, SSD_CONV_DIM), lambda b, c: (row(b, c), 0)),
                  pl.BlockSpec((rows, SSD_W), lambda b, c: (row(b, c), Z0 // SSD_W)),
                  pl.BlockSpec((rows, LANE), lambda b, c: (row(b, c), DT0 // LANE)),
                  pl.BlockSpec((k, SSD_STATE, SSD_W), lambda b, c: (row(b, c), 0, 0)),
                  pl.BlockSpec((rows, SSD_W), lambda b, c: (row(b, c), ATT_W // SSD_W)),
                  full((SUBLANE, SSD_W)), full((LANE, SSD_W)), full((SUBLANE, SSD_W))],
        out_specs=[pl.BlockSpec((rows, SSD_CONV_DIM), lambda b, c: (row(b, c), 0)),
                   pl.BlockSpec((rows, SSD_W), lambda b, c: (row(b, c), 0)),
                   pl.BlockSpec((rows, LANE), lambda b, c: (row(b, c), 0)),
                   full((SUBLANE, SSD_W))],
        out_shape=[jax.ShapeDtypeStruct((T, SSD_CONV_DIM), F32),
                   jax.ShapeDtypeStruct((T, SSD_W), BF16),
                   jax.ShapeDtypeStruct((T, LANE), BF16),
                   jax.ShapeDtypeStruct((SUBLANE, SSD_W), F32)],
        scratch_shapes=[pltpu.VMEM((SSD_STATE, SSD_W), F32)],
        compiler_params=_cp(("arbitrary", "arbitrary")),
    )(pre, proj, proj, sall, dy, par, e_mat, psel)


def _sgu_consts():
    e = np.zeros((SUBLANE, SGU_W), np.float32)
    for g in range(SGU_GROUPS):
        e[g, HEAD_DIM * g:HEAD_DIM * (g + 1)] = 1.0
    return jnp.asarray(e)


def _sgu_chunk(u_raw, v_raw, ln, w, bst, e4):
    L = CHUNK
    u = _gelu(u_raw)
    v = _gelu(v_raw)
    mu = jnp.mean(v, axis=-1, keepdims=True)
    vc = v - mu
    var = jnp.mean(vc * vc, axis=-1, keepdims=True)
    vn = vc * lax.rsqrt(var + LN_EPS) * ln[0:1] + ln[1:2]
    vb = vn.astype(BF16)
    ri = lax.broadcasted_iota(jnp.int32, (L, L), 0)
    ci = lax.broadcasted_iota(jnp.int32, (L, L), 1)
    tril = ri >= ci
    col = lax.broadcasted_iota(jnp.int32, (1, SGU_W), 1)
    mixed = _dot(bst, e4, HI)
    for g in range(SGU_GROUPS):
        wc = jnp.where(tril, w[g], 0.0).astype(BF16)
        gm = (col >= HEAD_DIM * g) & (col < HEAD_DIM * (g + 1))
        mixed = mixed + jnp.where(gm, _dot(wc, vb), 0.0)
    return u * mixed


def _sgu_fwd(proj, ln, w, bst, B, S, name):
    T = B * S
    nc = S // CHUNK
    e4 = _sgu_consts()

    def body(u_ref, v_ref, ln_ref, w_ref, b_ref, e_ref, y_ref):
        y_ref[...] = _sgu_chunk(u_ref[...], v_ref[...], ln_ref[...], w_ref[...], b_ref[...], e_ref[...]).astype(BF16)

    return pl.pallas_call(
        body, name=name, grid=(T // CHUNK,),
        in_specs=[pl.BlockSpec((CHUNK, SGU_W), lambda i: (i, U0 // SGU_W)),
                  pl.BlockSpec((CHUNK, SGU_W), lambda i: (i, VS0 // SGU_W)),
                  pl.BlockSpec((SUBLANE, SGU_W), lambda i: (0, 0)),
                  pl.BlockSpec((SGU_GROUPS, CHUNK, CHUNK), lambda i: (0, 0, 0)),
                  pl.BlockSpec((CHUNK, SUBLANE), lambda i: (0, 0)),
                  pl.BlockSpec((SUBLANE, SGU_W), lambda i: (0, 0))],
        out_specs=pl.BlockSpec((CHUNK, SGU_W), lambda i: (i, 0)),
        out_shape=jax.ShapeDtypeStruct((T, SGU_W), BF16),
        compiler_params=_cp(("parallel",)),
    )(proj, proj, ln, w, bst, e4)


def _sgu_bwd(proj, dy, ln, w, bst, B, S, name):
    T = B * S
    e4 = _sgu_consts()
    ycol = (ATT_W + SSD_W) // SGU_W

    def body(u_ref, v_ref, dy_ref, ln_ref, w_ref, b_ref, e_ref, du_ref, dv_ref, dln_ref, dw_ref, db_ref):
        @pl.when(pl.program_id(0) == 0)
        def _():
            dln_ref[...] = jnp.zeros_like(dln_ref)
            dw_ref[...] = jnp.zeros_like(dw_ref)
            db_ref[...] = jnp.zeros_like(db_ref)

        e_v = e_ref[...]
        fn = lambda u, v, ln, w, b: _sgu_chunk(u, v, ln, w, b, e_v)
        _, vjp = jax.vjp(fn, u_ref[...], v_ref[...], ln_ref[...], w_ref[...], b_ref[...])
        du, dv, dln, dw, db = vjp(dy_ref[...])
        du_ref[...] = du.astype(BF16)
        dv_ref[...] = dv.astype(BF16)
        dln_ref[...] += dln
        dw_ref[...] += dw
        db_ref[...] += db

    c_ln = pl.BlockSpec((SUBLANE, SGU_W), lambda i: (0, 0))
    c_w = pl.BlockSpec((SGU_GROUPS, CHUNK, CHUNK), lambda i: (0, 0, 0))
    c_b = pl.BlockSpec((CHUNK, SUBLANE), lambda i: (0, 0))
    return pl.pallas_call(
        body, name=name, grid=(T // CHUNK,),
        in_specs=[pl.BlockSpec((CHUNK, SGU_W), lambda i: (i, U0 // SGU_W)),
                  pl.BlockSpec((CHUNK, SGU_W), lambda i: (i, VS0 // SGU_W)),
                  pl.BlockSpec((CHUNK, SGU_W), lambda i: (i, ycol)),
                  c_ln, c_w, c_b, pl.BlockSpec((SUBLANE, SGU_W), lambda i: (0, 0))],
        out_specs=[pl.BlockSpec((CHUNK, SGU_W), lambda i: (i, 0)),
                   pl.BlockSpec((CHUNK, SGU_W), lambda i: (i, 0)), c_ln, c_w, c_b],
        out_shape=[jax.ShapeDtypeStruct((T, SGU_W), BF16), jax.ShapeDtypeStruct((T, SGU_W), BF16),
                   jax.ShapeDtypeStruct((SUBLANE, SGU_W), F32),
                   jax.ShapeDtypeStruct((SGU_GROUPS, CHUNK, CHUNK), F32),
                   jax.ShapeDtypeStruct((CHUNK, SUBLANE), F32)],
        compiler_params=_cp(("arbitrary",)),
    )(proj, proj, dy, ln, w, bst, e4)


_HBM = pl.BlockSpec(memory_space=pltpu.HBM)
_SEM = pl.BlockSpec(memory_space=pltpu.SEMAPHORE)
_ANY = pl.BlockSpec(memory_space=pl.ANY)
_EFFECT = pltpu.SideEffectType.DATAFLOW_SIDE_EFFECTING


def _peers():
    x, y, c = lax.axis_index("x"), lax.axis_index("y"), lax.axis_index("c")
    out = []
    for p in range(1, N_DEV):
        px, py, pc = x ^ ((p >> 2) & 1), y ^ ((p >> 1) & 1), c ^ (p & 1)
        out.append(((px, py, pc), 4 * px + 2 * py + pc))
    return 4 * x + 2 * y + c, out


def _xchg_start(xs, a2a, order, name):
    n = len(xs)
    lands = [lax.empty(a.shape if f else (N_DEV,) + a.shape, a.dtype) for a, f in zip(xs, a2a)]

    def body(*refs):
        ins, zones = refs[:n], refs[n:2 * n]
        send_sems, recv_sems = refs[2 * n + 1], refs[2 * n + 2]
        token = refs[-1]
        me, peers = _peers()
        for p, (dev, peer) in enumerate(peers):
            for t in range(n):
                pltpu.make_async_remote_copy(
                    src_ref=ins[t].at[peer] if a2a[t] else ins[t], dst_ref=zones[t].at[me],
                    send_sem=send_sems.at[p * n + t], recv_sem=recv_sems.at[p * n + t],
                    device_id=dev, device_id_type=MESH).start()
        token[...] = jnp.zeros_like(token)

    hbm = lambda a: pltpu.HBM(a.shape, a.dtype)
    sems = pltpu.SemaphoreType.DMA(((N_DEV - 1) * n,))
    out = pl.pallas_call(
        body, name=name,
        in_specs=[_HBM] * (2 * n) + [_ANY],
        out_specs=[_SEM, _SEM] + [_HBM] * (2 * n) + [pl.BlockSpec(memory_space=pltpu.VMEM)],
        out_shape=[sems, sems] + [hbm(a) for a in xs] + [hbm(a) for a in lands]
        + [jax.ShapeDtypeStruct((SUBLANE, LANE), F32)],
        input_output_aliases={t: 2 + t for t in range(2 * n)},
        compiler_params=pltpu.CompilerParams(has_side_effects=_EFFECT),
    )(*[pltpu.with_memory_space_constraint(a, pltpu.HBM) for a in list(xs) + list(lands)], order)
    return out[0], out[1], out[2:2 + n], out[2 + n:2 + 2 * n], out[-1]


def _xchg_wait(started, a2a, after, name):
    send_sems, recv_sems, xs, lands, _ = started
    n = len(xs)

    def body(*refs):
        ins, zones = refs[:n], refs[n:2 * n]
        send_s, recv_s = refs[2 * n], refs[2 * n + 1]
        me, peers = _peers()
        cps = []
        for p, (dev, peer) in enumerate(peers):
            for t in range(n):
                cps.append(pltpu.make_async_remote_copy(
                    src_ref=ins[t].at[peer] if a2a[t] else ins[t], dst_ref=zones[t].at[peer],
                    send_sem=send_s.at[p * n + t], recv_sem=recv_s.at[p * n + t],
                    device_id=dev, device_id_type=MESH))
        for cp in cps:
            cp.wait_recv()
        for cp in cps:
            cp.wait_send()

    hbm = lambda a: pltpu.HBM(a.shape, a.dtype)
    out = pl.pallas_call(
        body, name=name,
        in_specs=[_HBM] * (2 * n) + [_SEM, _SEM, _ANY],
        out_specs=[_HBM] * (2 * n),
        out_shape=[hbm(a) for a in xs] + [hbm(a) for a in lands],
        input_output_aliases={t: t for t in range(2 * n)},
        compiler_params=pltpu.CompilerParams(has_side_effects=_EFFECT),
    )(*xs, *lands, send_sems, recv_sems, after)
    return out[:n], out[n:]


def _chip_peers():
    x, y, c = lax.axis_index("x"), lax.axis_index("y"), lax.axis_index("c")
    chips = [(1 - x, y), (x, 1 - y), (1 - x, 1 - y)]
    slot = lambda px, py, pc: 4 * px + 2 * py + pc
    return (x, y, c), chips, slot


def _gather_start(xs, order, name, own_slot):
    n = len(xs)
    lands = [lax.empty((N_DEV,) + a.shape, a.dtype) for a in xs]

    def body(*refs):
        ins, zones = refs[:n], refs[n:2 * n]
        send_sems, d2d_sems, ici_sems = refs[2 * n + 1:2 * n + 4]
        token = refs[-1]
        (x, y, c), chips, slot = _chip_peers()
        me = slot(x, y, c)
        for t in range(n):
            if own_slot:
                pltpu.make_async_copy(ins[t], zones[t].at[me], d2d_sems.at[n + t]).start()
            for j, (px, py) in enumerate(chips):
                pltpu.make_async_remote_copy(
                    src_ref=ins[t], dst_ref=zones[t].at[me], send_sem=send_sems.at[(1 + j) * n + t],
                    recv_sem=ici_sems.at[j * n + t], device_id=(px, py, c), device_id_type=MESH).start()
            pltpu.make_async_remote_copy(
                src_ref=ins[t], dst_ref=zones[t].at[me], send_sem=send_sems.at[t],
                recv_sem=d2d_sems.at[t], device_id=(x, y, 1 - c), device_id_type=MESH).start()
        token[...] = jnp.zeros_like(token)

    hbm = lambda a: pltpu.HBM(a.shape, a.dtype)
    dma = lambda k: pltpu.SemaphoreType.DMA((k,))
    out = pl.pallas_call(
        body, name=name,
        in_specs=[_HBM] * (2 * n) + [_ANY],
        out_specs=[_SEM, _SEM, _SEM] + [_HBM] * (2 * n) + [pl.BlockSpec(memory_space=pltpu.VMEM)],
        out_shape=[dma(4 * n), dma(2 * n), dma(3 * n)] + [hbm(a) for a in xs] + [hbm(a) for a in lands]
        + [jax.ShapeDtypeStruct((SUBLANE, LANE), F32)],
        input_output_aliases={t: 3 + t for t in range(2 * n)},
        compiler_params=pltpu.CompilerParams(has_side_effects=_EFFECT),
    )(*[pltpu.with_memory_space_constraint(a, pltpu.HBM) for a in list(xs) + list(lands)], order)
    return dict(send=out[0], d2d=out[1], ici=out[2], xs=out[3:3 + n], lands=out[3 + n:3 + 2 * n], token=out[-1],
                own_slot=own_slot)


def _gather_relay(st, after, name):
    n = len(st["xs"])

    def body(*refs):
        zones, ici_sems = refs[:n], refs[n]
        fsend, frecv = refs[n + 2], refs[n + 3]
        token = refs[-1]
        (x, y, c), chips, slot = _chip_peers()
        for t in range(n):
            for j, (px, py) in enumerate(chips):
                blk = zones[t].at[slot(px, py, c)]
                fwd = pltpu.make_async_remote_copy(
                    src_ref=blk, dst_ref=blk, send_sem=fsend.at[j * n + t], recv_sem=ici_sems.at[j * n + t],
                    device_id=(x, y, 1 - c), device_id_type=MESH)
                fwd.wait_recv()
                pltpu.make_async_remote_copy(
                    src_ref=blk, dst_ref=blk, send_sem=fsend.at[j * n + t], recv_sem=frecv.at[j * n + t],
                    device_id=(x, y, 1 - c), device_id_type=MESH).start()
        token[...] = jnp.zeros_like(token)

    hbm = lambda a: pltpu.HBM(a.shape, a.dtype)
    dma = lambda k: pltpu.SemaphoreType.DMA((k,))
    out = pl.pallas_call(
        body, name=name,
        in_specs=[_HBM] * n + [_SEM, _ANY],
        out_specs=[_SEM, _SEM] + [_HBM] * n + [pl.BlockSpec(memory_space=pltpu.VMEM)],
        out_shape=[dma(3 * n), dma(3 * n)] + [hbm(a) for a in st["lands"]]
        + [jax.ShapeDtypeStruct((SUBLANE, LANE), F32)],
        input_output_aliases={t: 2 + t for t in range(n)},
        compiler_params=pltpu.CompilerParams(has_side_effects=_EFFECT),
    )(*st["lands"], st["ici"], after)
    return dict(st, fsend=out[0], frecv=out[1], lands=out[2:2 + n], token=out[-1])


def _gather_wait(st, after, name):
    n = len(st["xs"])

    def body(*refs):
        ins, zones = refs[:n], refs[n:2 * n]
        send_sems, d2d_sems, fsend, frecv = refs[2 * n:2 * n + 4]
        (x, y, c), chips, slot = _chip_peers()
        sib = (x, y, 1 - c)
        for t in range(n):
            if st["own_slot"]:
                pltpu.make_async_copy(ins[t], zones[t].at[slot(x, y, c)], d2d_sems.at[n + t]).wait()
            mine = lambda s, r, dst: pltpu.make_async_remote_copy(
                src_ref=ins[t], dst_ref=dst, send_sem=s, recv_sem=r, device_id=sib, device_id_type=MESH)
            direct = mine(send_sems.at[t], d2d_sems.at[t], zones[t].at[slot(x, y, 1 - c)])
            direct.wait_recv()
            direct.wait_send()
            for j, (px, py) in enumerate(chips):
                mine(send_sems.at[(1 + j) * n + t], d2d_sems.at[t], zones[t].at[slot(px, py, c)]).wait_send()
                relayed = mine(fsend.at[j * n + t], frecv.at[j * n + t], zones[t].at[slot(px, py, 1 - c)])
                relayed.wait_recv()
                relayed.wait_send()

    hbm = lambda a: pltpu.HBM(a.shape, a.dtype)
    out = pl.pallas_call(
        body, name=name,
        in_specs=[_HBM] * (2 * n) + [_SEM] * 4 + [_ANY],
        out_specs=[_HBM] * (2 * n),
        out_shape=[hbm(a) for a in st["xs"]] + [hbm(a) for a in st["lands"]],
        input_output_aliases={t: t for t in range(2 * n)},
        compiler_params=pltpu.CompilerParams(has_side_effects=_EFFECT),
    )(*st["xs"], *st["lands"], st["send"], st["d2d"], st["fsend"], st["frecv"], after)
    return out[:n], out[n:]


def _cast_layers(pairs, name):
    def body(*refs):
        n = len(refs) // 2
        for i in range(n):
            refs[n + i][...] = refs[i][...].astype(BF16)

    in_specs = [pl.BlockSpec((None,) + w.shape[1:], functools.partial(lambda l, i: (l, 0, 0), l),
                             pipeline_mode=pl.Buffered(1)) for w, l in pairs]
    return pl.pallas_call(
        body, name=name, grid=(1,), in_specs=in_specs,
        out_specs=[pl.BlockSpec(w.shape[1:], lambda i: (0, 0)) for w, _ in pairs],
        out_shape=[jax.ShapeDtypeStruct(w.shape[1:], BF16) for w, _ in pairs],
        compiler_params=_cp(("arbitrary",)),
    )(*[w for w, _ in pairs])


def _adam_step(me, w, m, v, parts_ref, mine, out_refs):
    g = None
    for p in range(N_DEV):
        term = jnp.where(me == p, mine.astype(F32), parts_ref[p].astype(F32))
        g = term if g is None else g + term
    mn = ADAM_B1 * m + (1.0 - ADAM_B1) * g
    vn = ADAM_B2 * v + (1.0 - ADAM_B2) * (g * g)
    m_hat = mn / (1.0 - ADAM_B1 ** ADAM_STEP)
    v_hat = vn / (1.0 - ADAM_B2 ** ADAM_STEP)
    g_ref, d_ref, mo_ref, vo_ref = out_refs
    g_ref[...] = g
    d_ref[...] = -ADAM_LR * (m_hat / (jnp.sqrt(v_hat) + ADAM_EPS) + ADAM_WD * w)
    mo_ref[...] = mn
    vo_ref[...] = vn


def _adamw(me, w, m, v, parts, own, name, layer=0, into=None):
    L, R, C = w.shape
    P = parts.shape[0]
    tr = R
    t = 16
    while t <= R:
        if R % t == 0 and t * C <= 131072:
            tr = t
        t += 16
    if tr == R and R * C > 131072 and R % 16 == 0:
        tr = 16
    own_all = own.shape[0] == P

    def body(me_ref, w_ref, m_ref, v_ref, p_ref, own_ref, *rest):
        _adam_step(me_ref[0], w_ref[...], m_ref[...], v_ref[...], p_ref, own_ref[...], rest[-4:])

    blk = pl.BlockSpec((None, tr, C), lambda i, me_ref: (layer, i, 0))
    own_blk = pl.BlockSpec((None, tr, C), lambda i, me_ref: (me_ref[0] if own_all else 0, i, 0))
    prev = list(into) if into is not None else []
    return pl.pallas_call(
        body, name=name,
        grid_spec=pltpu.PrefetchScalarGridSpec(
            num_scalar_prefetch=1, grid=(R // tr,),
            in_specs=[blk, blk, blk, pl.BlockSpec((P, tr, C), lambda i, me_ref: (0, i, 0)), own_blk]
            + [_ANY] * len(prev),
            out_specs=[blk] * 4),
        out_shape=[jax.ShapeDtypeStruct((L, R, C), F32)] * 4,
        input_output_aliases={6 + i: i for i in range(len(prev))},
        compiler_params=_cp(("parallel",)),
    )(me, w, m, v, parts, own, *prev)


def _perm_cols(w):
    pad = jnp.zeros((w.shape[0], LANE - SSD_HEADS), w.dtype)
    return jnp.concatenate([w[:, 0:1536], w[:, 2438:2694], w[:, 1536:2432], w[:, 2432:2438], pad,
                            w[:, 2694:2950]], axis=1)


def _unperm_cols(w):
    return jnp.concatenate([w[:, 0:1536], w[:, XBC0:XBC0 + SSD_CONV_DIM], w[:, DT0:DT0 + SSD_HEADS],
                            w[:, U0:U0 + SGU_W], w[:, VS0:VS0 + SGU_W]], axis=1)


_SMALL = ("ffn1_norm", "mix_norm", "conv_w", "conv_b", "dt_bias", "a_log", "d_skip", "ssd_norm",
          "sgu_ln_g", "sgu_ln_b", "sgu_w", "sgu_b", "ffn2_norm", "final_norm", "loss")


_SMALL_LAST = ("ffn1_norm",)
_SMALL_EARLY = tuple(k for k in _SMALL if k not in _SMALL_LAST)


def _pack(d, names):
    v = jnp.concatenate([d[k].astype(F32).reshape(-1) for k in names])
    n = v.shape[0]
    npad = -(-n // (LANE * 16)) * (LANE * 16)
    return jnp.pad(v, (0, npad - n)).reshape(npad // LANE, LANE)


def _unpack(p, shapes, names):
    v = p.reshape(-1)
    out, o = {}, 0
    for k in names:
        n = int(np.prod(shapes[k]))
        out[k] = v[o:o + n].reshape(shapes[k])
        o += n
    return out


def kernel(x, ffn1_norm, ffn1_w_gate, ffn1_w_up, ffn1_w_down, mix_norm, w_in, conv_w, conv_b, dt_bias, a_log, d_skip, ssd_norm, sgu_ln_g, sgu_ln_b, sgu_w, sgu_b, w_out, ffn2_norm, ffn2_w_gate, ffn2_w_up, ffn2_w_down, final_norm, loss_target, m_ffn1_norm, m_ffn1_w_gate, m_ffn1_w_up, m_ffn1_w_down, m_mix_norm, m_w_in, m_conv_w, m_conv_b, m_dt_bias, m_a_log, m_d_skip, m_ssd_norm, m_sgu_ln_g, m_sgu_ln_b, m_sgu_w, m_sgu_b, m_w_out, m_ffn2_norm, m_ffn2_w_gate, m_ffn2_w_up, m_ffn2_w_down, m_final_norm, v_ffn1_norm, v_ffn1_w_gate, v_ffn1_w_up, v_ffn1_w_down, v_mix_norm, v_w_in, v_conv_w, v_conv_b, v_dt_bias, v_a_log, v_d_skip, v_ssd_norm, v_sgu_ln_g, v_sgu_ln_b, v_sgu_w, v_sgu_b, v_w_out, v_ffn2_norm, v_ffn2_w_gate, v_ffn2_w_up, v_ffn2_w_down, v_final_norm):
    B, S, D = x.shape
    T = B * S
    L = ffn1_norm.shape[0]
    me = 4 * lax.axis_index("x") + 2 * lax.axis_index("y") + lax.axis_index("c")
    cs = conv_w.shape[2]
    W = dict(ffn1_norm=ffn1_norm, ffn1_w_gate=ffn1_w_gate, ffn1_w_up=ffn1_w_up, ffn1_w_down=ffn1_w_down,
             mix_norm=mix_norm, w_in=w_in, conv_w=conv_w, conv_b=conv_b, dt_bias=dt_bias, a_log=a_log,
             d_skip=d_skip, ssd_norm=ssd_norm, sgu_ln_g=sgu_ln_g, sgu_ln_b=sgu_ln_b, sgu_w=sgu_w, sgu_b=sgu_b,
             w_out=w_out, ffn2_norm=ffn2_norm, ffn2_w_gate=ffn2_w_gate, ffn2_w_up=ffn2_w_up,
             ffn2_w_down=ffn2_w_down, final_norm=final_norm)
    M = dict(ffn1_norm=m_ffn1_norm, ffn1_w_gate=m_ffn1_w_gate, ffn1_w_up=m_ffn1_w_up, ffn1_w_down=m_ffn1_w_down,
             mix_norm=m_mix_norm, w_in=m_w_in, conv_w=m_conv_w, conv_b=m_conv_b, dt_bias=m_dt_bias, a_log=m_a_log,
             d_skip=m_d_skip, ssd_norm=m_ssd_norm, sgu_ln_g=m_sgu_ln_g, sgu_ln_b=m_sgu_ln_b, sgu_w=m_sgu_w,
             sgu_b=m_sgu_b, w_out=m_w_out, ffn2_norm=m_ffn2_norm, ffn2_w_gate=m_ffn2_w_gate,
             ffn2_w_up=m_ffn2_w_up, ffn2_w_down=m_ffn2_w_down, final_norm=m_final_norm)
    V = dict(ffn1_norm=v_ffn1_norm, ffn1_w_gate=v_ffn1_w_gate, ffn1_w_up=v_ffn1_w_up, ffn1_w_down=v_ffn1_w_down,
             mix_norm=v_mix_norm, w_in=v_w_in, conv_w=v_conv_w, conv_b=v_conv_b, dt_bias=v_dt_bias, a_log=v_a_log,
             d_skip=v_d_skip, ssd_norm=v_ssd_norm, sgu_ln_g=v_sgu_ln_g, sgu_ln_b=v_sgu_ln_b, sgu_w=v_sgu_w,
             sgu_b=v_sgu_b, w_out=v_w_out, ffn2_norm=v_ffn2_norm, ffn2_w_gate=v_ffn2_w_gate,
             ffn2_w_up=v_ffn2_w_up, ffn2_w_down=v_ffn2_w_down, final_norm=v_final_norm)
    FFN1 = ("ffn1_w_gate", "ffn1_w_up", "ffn1_w_down")
    FFN2 = ("ffn2_w_gate", "ffn2_w_up", "ffn2_w_down")
    MIX = ("w_in", "w_out")
    big = FFN1 + MIX + FFN2
    col_sharded = lambda k: k.endswith("w_gate") or k.endswith("w_up")
    for dct in (W, M, V):
        for k in big:
            if col_sharded(k):
                dct[k] = jnp.swapaxes(dct[k], 1, 2)

    wgroups = [[(k, 0) for k in FFN1], [("w_in", 0), ("conv_w", None)], [("w_out", 0)] + [(k, 0) for k in FFN2]]
    wgroups += [[(k, l) for k in big] for l in range(1, L)]
    wstarted, order = [], x
    later = [kl for grp in wgroups[1:] for kl in grp if kl[0] != "conv_w"]
    cast = dict(zip(wgroups[0], _cast_layers([(W[k], l) for k, l in wgroups[0]], "cast_first")))
    for gi, grp in enumerate(wgroups):
        if gi == 1:
            first = lax.optimization_barrier((W[later[0][0]], order))[0]
            srcs = [(first if i == 0 else W[k], l) for i, (k, l) in enumerate(later)]
            cast.update(zip(later, _cast_layers(srcs, "cast_rest")))
        xs = [conv_w if k == "conv_w" else cast[(k, l)] for k, l in grp]
        st = _gather_start(xs, order, f"gather_start_{gi}", own_slot=True)
        order = st["token"]
        wstarted.append(st)
    G = {}
    is_me = (jnp.arange(N_DEV) == me)

    zero1 = jnp.zeros((1,), F32)
    W["loss"], M["loss"], V["loss"] = zero1, zero1, zero1
    full_shapes = {k: (W[k].shape if k != "conv_w" else (L, SSD_CONV, SSD_CONV_DIM)) for k in _SMALL}
    embed = lambda a, k: a if k != "conv_w" else lax.dynamic_update_slice(
        jnp.zeros(full_shapes[k], F32), a, (0, 0, me * cs))
    small_packs = {names: [_pack({k: embed(d[k], k) for k in names}, names)[None] for d in (W, M, V)]
                   for names in (_SMALL_EARLY, _SMALL_LAST)}

    def relay(gi, after):
        wstarted[gi] = _gather_relay(wstarted[gi], after, f"gather_relay_{gi}")
        return wstarted[gi]["token"]

    def gathered(gi, after):
        own, lands = _gather_wait(wstarted[gi], after, f"gather_wait_{gi}")
        for key, o, z in zip(wgroups[gi], own, lands):
            G[key] = z if wstarted[gi]["own_slot"] else jnp.where(
                is_me.reshape((N_DEV,) + (1,) * o.ndim), o[None], z)

    def rows(k, l):
        a = G[(k, l)]
        return a.reshape(-1, a.shape[-1])

    bias = _attn_bias(S, min(256, S))
    row1 = lambda a: a.reshape(1, -1)

    def ffn1_params(l):
        return dict(g1=row1(ffn1_norm[l]), wg1=rows("ffn1_w_gate", l), wu1=rows("ffn1_w_up", l),
                    wd1=rows("ffn1_w_down", l))

    def out_params(l):
        return dict(wout=rows("w_out", l), g2=row1(ffn2_norm[l]), wg2=rows("ffn2_w_gate", l),
                    wu2=rows("ffn2_w_up", l), wd2=rows("ffn2_w_down", l))

    def mix_params(l):
        cw = jnp.transpose(G[("conv_w", None)][:, l], (1, 0, 2)).reshape(SSD_CONV, -1)
        return dict(
            gm=row1(mix_norm[l]), win=_perm_cols(rows("w_in", l)),
            cw=jnp.pad(cw, ((0, SUBLANE - SSD_CONV), (0, 0))), cb=row1(conv_b[l]),
            par=jnp.pad(jnp.stack([jnp.repeat(dt_bias[l], HEAD_DIM), jnp.repeat(a_log[l], HEAD_DIM),
                                   jnp.repeat(d_skip[l], HEAD_DIM), ssd_norm[l]]), ((0, SUBLANE - 4), (0, 0))),
            ln=jnp.pad(jnp.stack([sgu_ln_g[l], sgu_ln_b[l]]), ((0, SUBLANE - 2), (0, 0))),
            sw=sgu_w[l], bst=jnp.pad(sgu_b[l].T, ((0, 0), (0, SUBLANE - SGU_GROUPS))))

    xc = x.reshape(T, D)
    saved, lay = [], []
    tie = lambda a, tok: lax.optimization_barrier((a, tok))[0]
    for l in range(L):
        gathered(0 if l == 0 else l + 2, relay(0, order) if l == 0 else xc)
        p = ffn1_params(l)
        x1, *ffn1_saved = _ffn_fwd(xc, p["g1"], p["wg1"], p["wu1"], p["wd1"], f"ffn1_fwd_{l}")
        if l == 0:
            gathered(1, relay(1, x1))
        p.update(mix_params(l))
        lay.append(p)
        proj, ht = _norm_mm(x1, p["gm"], p["win"], f"in_proj_{l}")
        if l == 0:
            proj = tie(proj, relay(2, proj))
        o_att, lse = _attn_fwd(proj, bias, B, S, f"attn_fwd_{l}")
        pre = _conv_fwd(proj, p["cw"], p["cb"], B, S, f"conv_fwd_{l}")
        y_ssd, sall = _ssd_fwd(pre, proj, p["par"], B, S, f"ssd_fwd_{l}")
        y_sgu = _sgu_fwd(proj, p["ln"], p["sw"], p["bst"], B, S, f"sgu_fwd_{l}")
        ycat = jnp.concatenate([o_att.astype(BF16), y_ssd, y_sgu], axis=1)
        if l + 1 < L:
            ycat = tie(ycat, relay(l + 3, ycat))
        if l == 0:
            gathered(2, ycat)
        p.update(out_params(l))
        x2 = _mm(ycat, p["wout"], "nn", f"out_proj_{l}", residual=x1)
        x3, *ffn2_saved = _ffn_fwd(x2, p["g2"], p["wg2"], p["wu2"], p["wd2"], f"ffn2_fwd_{l}")
        saved.append(dict(x0=xc, ffn1=ffn1_saved, x1=x1, ht=ht, proj=proj, o_att=o_att, lse=lse, pre=pre,
                          sall=sall, ycat=ycat, x2=x2, ffn2=ffn2_saved))
        xc = x3
    loss_part, dx, dgf = _final_loss(xc, row1(final_norm), loss_target.reshape(T, D), "final_loss")

    gl = [dict() for _ in range(L)]
    gstarted, gorder = [], [order]

    def to_blocks(k, a):
        return a.reshape(N_DEV, -1, a.shape[-1]).astype(BF16)

    def send_grads(keys, l, extra, tag, small_names=None):
        xs = [to_blocks(k, gl[l][k]) for k in keys] + extra
        flags = [True] * len(keys) + [False] * len(extra)
        st = _xchg_start(xs, flags, gorder[0], f"grads_start_{tag}")
        gorder[0] = st[-1]
        gstarted.append((keys, l, st, flags, tag, small_names))

    def small_grads(names):
        sm = {}
        for k in names:
            if k == "final_norm":
                sm[k] = dgf.reshape(-1)
            elif k == "loss":
                sm[k] = loss_part[0, :1]
            else:
                sm[k] = jnp.stack([gl[l][k] for l in range(L)])
        return [_pack(sm, names)]

    def behind(a):
        return lax.optimization_barrier((a, gorder[0]))[0]

    for l in reversed(range(L)):
        p, s, g = lay[l], saved[l], gl[l]
        gfac, ufac, actt = s["ffn2"]
        dx2, dgt, dut, xn, dacc, g["ffn2_norm"] = _ffn_bwd_dx(
            dx, s["x2"], p["g2"], gfac, ufac, p["wg2"], p["wu2"], p["wd2"], f"ffn2_bwd_{l}")
        g["ffn2_w_gate"], g["ffn2_w_up"], g["ffn2_w_down"] = _ffn_dw(dgt, dut, actt, xn, dacc, f"ffn2_dw_{l}")
        if l == 0:
            send_grads(FFN2, 0, [], "l0f")
            dx2 = behind(dx2)
        dycat = _mm(dx2, p["wout"], "nt", f"out_proj_dx_{l}")
        g["w_out"] = _mm(s["ycat"], dx2, "tn", f"out_proj_dw_{l}", out_dtype=BF16, tm_cap=1024, tk_cap=512)
        dq, dk, dv = _attn_bwd(s["proj"], s["o_att"], s["lse"], dycat, bias, B, S, f"attn_bwd_{l}")
        dpre, dz, ddt, dpar = _ssd_bwd(s["pre"], s["proj"], s["sall"], dycat, p["par"], B, S, f"ssd_bwd_{l}")
        dxbc, dwb = _conv_bwd(dpre, s["proj"], p["cw"], B, S, f"conv_bwd_{l}")
        du, dvs, dln, dsw, dbst = _sgu_bwd(s["proj"], dycat, p["ln"], p["sw"], p["bst"], B, S, f"sgu_bwd_{l}")
        hsum = lambda r: r.reshape(SSD_HEADS, HEAD_DIM).sum(-1)
        g["conv_w"], g["conv_b"] = dwb[:SSD_CONV], dwb[SSD_CONV]
        g["dt_bias"], g["a_log"], g["d_skip"], g["ssd_norm"] = hsum(dpar[0]), hsum(dpar[1]), hsum(dpar[2]), dpar[3]
        g["sgu_ln_g"], g["sgu_ln_b"], g["sgu_w"], g["sgu_b"] = dln[0], dln[1], dsw, dbst[:, :SGU_GROUPS].T
        dproj = jnp.concatenate([dq, dk, dv, dz, du, dxbc, ddt, dvs], axis=1)
        g["w_in"] = _unperm_cols(_mm_resident_lhs(s["ht"], dproj, f"in_proj_dw_{l}"))
        dx1, g["mix_norm"] = _norm_mm_bwd(dproj, s["x1"], p["gm"], p["win"], dx2, f"in_proj_bwd_{l}")
        if l == 0:
            send_grads(MIX, 0, small_grads(_SMALL_EARLY), "l0a", _SMALL_EARLY)
            dx1, small_packs = lax.optimization_barrier((behind(dx1), small_packs))
        gfac, ufac, actt = s["ffn1"]
        dx, dgt, dut, xn, dacc, g["ffn1_norm"] = _ffn_bwd_dx(
            dx1, s["x0"], p["g1"], gfac, ufac, p["wg1"], p["wu1"], p["wd1"], f"ffn1_bwd_{l}")
        g["ffn1_w_gate"], g["ffn1_w_up"], g["ffn1_w_down"] = _ffn_dw(dgt, dut, actt, xn, dacc, f"ffn1_dw_{l}")
        if l > 0:
            send_grads(big, l, [], f"l{l}")
            dx = behind(dx)
    grad_x = dx.reshape(B, S, D)
    send_grads(FFN1, 0, small_grads(_SMALL_LAST), "l0b", _SMALL_LAST)

    res, after = {}, gorder[0]
    small_out = [dict() for _ in range(4)]
    me1 = me.reshape(1).astype(jnp.int32)
    for keys, l, st, flags, tag, names in gstarted:
        own, lands = _xchg_wait(st, flags, after, f"grads_wait_{tag}")
        for k, mine, pk in zip(keys, own, lands):
            res[k] = _adamw(me1, W[k], M[k], V[k], pk, mine, f"adamw_{k}_{l}", layer=l, into=res.get(k))
        done = [res[k][0] for k in keys]
        if names:
            outs = _adamw(me1, *small_packs[names], lands[-1], own[-1][None], f"adamw_small_{tag}")
            for d, o in zip(small_out, outs):
                u = _unpack(o, full_shapes, names)
                if "conv_w" in u:
                    u["conv_w"] = lax.dynamic_slice(u["conv_w"], (0, 0, me * cs), (L, SSD_CONV, cs))
                d.update(u)
                done.extend(u.values())
        after = lax.optimization_barrier(tuple(done))[0]
    back = lambda k, a: jnp.swapaxes(a, 1, 2) if col_sharded(k) else a
    grads, deltas, new_m, new_v = [dict({k: back(k, res[k][i]) for k in big}, **small_out[i]) for i in range(4)]

    names = ("ffn1_norm", "ffn1_w_gate", "ffn1_w_up", "ffn1_w_down", "mix_norm", "w_in", "conv_w", "conv_b",
             "dt_bias", "a_log", "d_skip", "ssd_norm", "sgu_ln_g", "sgu_ln_b", "sgu_w", "sgu_b", "w_out",
             "ffn2_norm", "ffn2_w_gate", "ffn2_w_up", "ffn2_w_down", "final_norm")
    loss = grads["loss"][0]
    return (loss, grad_x, *[grads[n] for n in names], *[deltas[n] for n in names],
            *[new_m[n] for n in names], *[new_v[n] for n in names])
```

```python
import functools

import numpy as np
import jax
import jax.numpy as jnp
from jax import lax
from jax.experimental import pallas as pl
from jax.experimental.pallas import tpu as pltpu

F32, BF16 = jnp.float32, jnp.bfloat16
HI = lax.Precision.HIGH
MESH = pl.DeviceIdType.MESH
N_DEV = 8
VMEM_LIMIT_BYTES = 56 * 1024 * 1024
LANE, SUBLANE = 128, 8

HEAD_DIM = 64
ATT_W = 384
SSD_W = 384
SSD_HEADS = 6
SSD_STATE = 128
SSD_CONV = 4
CHUNK = 128
SSD_CONV_DIM = 896
SGU_W = 256
SGU_GROUPS = 4
D_IN = 2950
RMS_EPS = 1e-6
LN_EPS = 1e-5
NEG = -1e30

PW = 3072
Q0, K0, V0, Z0, U0, XBC0, DT0, VS0 = 0, 384, 768, 1152, 1536, 1792, 2688, 2816

ADAM_LR, ADAM_B1, ADAM_B2, ADAM_EPS, ADAM_WD, ADAM_STEP = 0.001, 0.9, 0.999, 1e-08, 0.01, 10


def _cp(sem=None):
    return pltpu.CompilerParams(dimension_semantics=sem, vmem_limit_bytes=VMEM_LIMIT_BYTES)


def _tile(n, cap, mult=LANE):
    best = None
    t = mult
    while t <= min(n, cap):
        if n % t == 0:
            best = t
        t += mult
    return best if best is not None else n


def _dot(a, b, prec=None):
    return jnp.dot(a, b, preferred_element_type=F32, precision=prec)


def _dot_nt(a, b, prec=None):
    return lax.dot_general(a, b, (((1,), (1,)), ((), ())), preferred_element_type=F32, precision=prec)


def _dot_tn(a, b, prec=None):
    return lax.dot_general(a, b, (((0,), (0,)), ((), ())), preferred_element_type=F32, precision=prec)


def _sigmoid(x):
    return 1.0 / (1.0 + jnp.exp(-x))


def _silu(x):
    return x * _sigmoid(x)


def _gelu(x):
    return 0.5 * x * (1.0 + lax.erf(x * 0.7071067811865476))


def _softplus(x):
    return jnp.maximum(x, 0.0) + jnp.log(1.0 + jnp.exp(-jnp.abs(x)))


def _rms_fwd(x, g):
    rstd = lax.rsqrt(jnp.mean(x * x, axis=-1, keepdims=True) + RMS_EPS)
    xhat = x * rstd
    return xhat * g, xhat, rstd


def _rms_bwd(dy, xhat, rstd, g):
    dxhat = dy * g
    dx = rstd * (dxhat - xhat * jnp.mean(dxhat * xhat, axis=-1, keepdims=True))
    return dx, dy * xhat


def _resident(shape):
    return pl.BlockSpec(shape, lambda *_: (0,) * len(shape), pipeline_mode=pl.Buffered(1))


def _mm(a, b, mode, name, out_dtype=F32, residual=None, tm_cap=512, tn_cap=1024, tk_cap=1024):
    if mode == "nn":
        (M, K), (_, N) = a.shape, b.shape
    elif mode == "nt":
        (M, K), (N, _) = a.shape, b.shape
    else:
        (K, M), (_, N) = a.shape, b.shape
    tm, tn, tk = _tile(M, tm_cap), _tile(N, tn_cap), _tile(K, tk_cap)
    nk = K // tk
    if mode == "tn":
        a_spec = pl.BlockSpec((tk, tm), lambda i, j, k: (k, i))
    else:
        a_spec = pl.BlockSpec((tm, tk), lambda i, j, k: (i, k))
    if mode == "nt":
        b_spec = pl.BlockSpec((tn, tk), lambda i, j, k: (j, k))
    else:
        b_spec = pl.BlockSpec((tk, tn), lambda i, j, k: (k, j))
    o_spec = pl.BlockSpec((tm, tn), lambda i, j, k: (i, j))
    has_res = residual is not None

    def prod(a_ref, b_ref):
        av = a_ref[...].astype(BF16)
        bv = b_ref[...].astype(BF16)
        if mode == "nn":
            return _dot(av, bv)
        if mode == "nt":
            return _dot_nt(av, bv)
        return _dot_tn(av, bv)

    def body(*refs):
        a_ref, b_ref = refs[:2]
        r_ref = refs[2] if has_res else None
        o_ref = refs[2 + has_res]
        if nk == 1:
            o = prod(a_ref, b_ref)
            if has_res:
                o = r_ref[...] + o
            o_ref[...] = o.astype(out_dtype)
            return
        acc = refs[3 + has_res]
        k = pl.program_id(2)

        @pl.when(k == 0)
        def _():
            acc[...] = jnp.zeros_like(acc)

        acc[...] += prod(a_ref, b_ref)

        @pl.when(k == nk - 1)
        def _():
            o = acc[...]
            if has_res:
                o = r_ref[...] + o
            o_ref[...] = o.astype(out_dtype)

    ins = [a, b] + ([residual] if has_res else [])
    in_specs = [a_spec, b_spec] + ([o_spec] if has_res else [])
    return pl.pallas_call(
        body, name=name, grid=(M // tm, N // tn, nk),
        in_specs=in_specs, out_specs=o_spec,
        out_shape=jax.ShapeDtypeStruct((M, N), out_dtype),
        scratch_shapes=[pltpu.VMEM((tm, tn), F32)] if nk > 1 else [],
        compiler_params=_cp(("parallel", "parallel", "arbitrary")),
    )(*ins)


def _after(after):
    return ([after], [_ANY]) if after is not None else ([], [])


def _ffn_fwd(x, g, wgt, wut, wd, name, after=None):
    T, D = x.shape
    F = wgt.shape[0]
    tm = _tile(T, 256)

    def body(x_ref, g_ref, wg_ref, wu_ref, wd_ref, *rest):
        out_ref, dgf_ref, sl_ref, actt_ref = rest[-4:]
        xv = x_ref[...]
        xn = _rms_fwd(xv, g_ref[...])[0].astype(BF16)
        gate = _dot_nt(xn, wg_ref[...])
        up = _dot_nt(xn, wu_ref[...])
        sig = _sigmoid(gate)
        sl = gate * sig
        dgf_ref[...] = (up * (sig * (1.0 + gate * (1.0 - sig)))).astype(BF16)
        sl_ref[...] = sl.astype(BF16)
        act = (sl * up).astype(BF16)
        actt_ref[...] = act.T
        out_ref[...] = xv + 0.5 * _dot(act, wd_ref[...])

    row = lambda w: pl.BlockSpec((tm, w), lambda i: (i, 0))
    extra, extra_specs = _after(after)
    return pl.pallas_call(
        body, name=name, grid=(T // tm,),
        in_specs=[row(D), _resident((1, D)), _resident((F, D)), _resident((F, D)), _resident((F, D))] + extra_specs,
        out_specs=[row(D), row(F), row(F), pl.BlockSpec((F, tm), lambda i: (0, i))],
        out_shape=[jax.ShapeDtypeStruct((T, D), F32),
                   jax.ShapeDtypeStruct((T, F), BF16),
                   jax.ShapeDtypeStruct((T, F), BF16),
                   jax.ShapeDtypeStruct((F, T), BF16)],
        compiler_params=_cp(("parallel",)),
    )(x, g, wgt, wut, wd, *extra)


def _ffn_bwd_dx(dout, x, g, dgf, sl, wg, wu, wd, name):
    T, D = x.shape
    F = wg.shape[0]
    tm = _tile(T, 256)

    def body(dout_ref, x_ref, g_ref, dgf_ref, sl_ref, wg_ref, wu_ref, wd_ref,
             dx_ref, dgt_ref, dut_ref, xn_ref, dacc_ref, dg_ref):
        @pl.when(pl.program_id(0) == 0)
        def _():
            dg_ref[...] = jnp.zeros_like(dg_ref)

        gv = g_ref[...]
        dout_v = dout_ref[...]
        xn, xhat, rstd = _rms_fwd(x_ref[...], gv)
        xn_ref[...] = xn.astype(BF16)
        dacc = (0.5 * dout_v).astype(BF16)
        dacc_ref[...] = dacc
        dact = _dot_nt(dacc, wd_ref[...])
        dgate = (dact * dgf_ref[...].astype(F32)).astype(BF16)
        dup = (dact * sl_ref[...].astype(F32)).astype(BF16)
        dgt_ref[...] = dgate.T
        dut_ref[...] = dup.T
        dxn = _dot(dgate, wg_ref[...]) + _dot(dup, wu_ref[...])
        dx, dgrow = _rms_bwd(dxn, xhat, rstd, gv)
        dx_ref[...] = dout_v + dx
        dg_ref[...] += jnp.sum(dgrow, axis=0, keepdims=True)

    row = lambda w: pl.BlockSpec((tm, w), lambda i: (i, 0))
    tr = pl.BlockSpec((F, tm), lambda i: (0, i))
    return pl.pallas_call(
        body, name=name, grid=(T // tm,),
        in_specs=[row(D), row(D), _resident((1, D)), row(F), row(F),
                  _resident((F, D)), _resident((F, D)), _resident((F, D))],
        out_specs=[row(D), tr, tr, row(D), row(D), pl.BlockSpec((1, D), lambda i: (0, 0))],
        out_shape=[jax.ShapeDtypeStruct((T, D), F32)] + [jax.ShapeDtypeStruct((F, T), BF16)] * 2
        + [jax.ShapeDtypeStruct((T, D), BF16)] * 2 + [jax.ShapeDtypeStruct((1, D), F32)],
        compiler_params=_cp(("arbitrary",)),
    )(dout, x, g, dgf, sl, wg, wu, wd)


def _ffn_dw(dgt, dut, actt, xn, dacc, name):
    F, T = dgt.shape
    D = xn.shape[1]
    th = _tile(F, 256)

    def body(dg_ref, du_ref, a_ref, xn_ref, dacc_ref, dwg_ref, dwu_ref, dwd_ref):
        xv = xn_ref[...]
        dwg_ref[...] = _dot(dg_ref[...], xv).astype(BF16)
        dwu_ref[...] = _dot(du_ref[...], xv).astype(BF16)
        dwd_ref[...] = _dot(a_ref[...], dacc_ref[...]).astype(BF16)

    tile = pl.BlockSpec((th, T), lambda j: (j, 0))
    out = pl.BlockSpec((th, D), lambda j: (j, 0))
    return pl.pallas_call(
        body, name=name, grid=(F // th,),
        in_specs=[tile, tile, tile, _resident((T, D)), _resident((T, D))],
        out_specs=[out, out, out], out_shape=[jax.ShapeDtypeStruct((F, D), BF16)] * 3,
        compiler_params=_cp(("parallel",)),
    )(dgt, dut, actt, xn, dacc)


def _norm_mm(x, g, w, name):
    T, D = x.shape
    N = w.shape[1]
    tm = _tile(T, 512)

    def body(x_ref, g_ref, w_ref, o_ref, ht_ref):
        xn = _rms_fwd(x_ref[...], g_ref[...])[0]
        ht_ref[...] = xn.T.astype(BF16)
        o_ref[...] = _dot(xn.astype(BF16), w_ref[...])

    return pl.pallas_call(
        body, name=name, grid=(T // tm,),
        in_specs=[pl.BlockSpec((tm, D), lambda i: (i, 0)), _resident((1, D)), _resident((D, N))],
        out_specs=[pl.BlockSpec((tm, N), lambda i: (i, 0)), pl.BlockSpec((D, tm), lambda i: (0, i))],
        out_shape=[jax.ShapeDtypeStruct((T, N), F32), jax.ShapeDtypeStruct((D, T), BF16)],
        compiler_params=_cp(("parallel",)),
    )(x, g, w)


def _norm_mm_bwd(dproj, x, g, w, dres, name):
    T, D = x.shape
    N = w.shape[1]
    tm = _tile(T, 512)

    def body(dp_ref, x_ref, g_ref, w_ref, dres_ref, dx_ref, dg_ref):
        @pl.when(pl.program_id(0) == 0)
        def _():
            dg_ref[...] = jnp.zeros_like(dg_ref)

        gv = g_ref[...]
        dh = _dot_nt(dp_ref[...], w_ref[...])
        _, xhat, rstd = _rms_fwd(x_ref[...], gv)
        dx, dgrow = _rms_bwd(dh, xhat, rstd, gv)
        dx_ref[...] = dres_ref[...] + dx
        dg_ref[...] += jnp.sum(dgrow, axis=0, keepdims=True)

    row = pl.BlockSpec((tm, D), lambda i: (i, 0))
    one = pl.BlockSpec((1, D), lambda i: (0, 0))
    return pl.pallas_call(
        body, name=name, grid=(T // tm,),
        in_specs=[pl.BlockSpec((tm, N), lambda i: (i, 0)), row, _resident((1, D)), _resident((D, N)), row],
        out_specs=[row, one],
        out_shape=[jax.ShapeDtypeStruct((T, D), F32), jax.ShapeDtypeStruct((1, D), F32)],
        compiler_params=_cp(("arbitrary",)),
    )(dproj, x, g, w, dres)


def _mm_resident_lhs(at, b, name, tn_cap=512):
    M, K = at.shape
    N = b.shape[1]
    tn = _tile(N, tn_cap)

    def body(a_ref, b_ref, o_ref):
        o_ref[...] = _dot(a_ref[...], b_ref[...]).astype(BF16)

    return pl.pallas_call(
        body, name=name, grid=(N // tn,),
        in_specs=[_resident((M, K)), pl.BlockSpec((K, tn), lambda j: (0, j))],
        out_specs=pl.BlockSpec((M, tn), lambda j: (0, j)),
        out_shape=jax.ShapeDtypeStruct((M, N), BF16),
        compiler_params=_cp(("parallel",)),
    )(at, b)


def _final_loss(x, g, target, name):
    T, D = x.shape
    tm = _tile(T, 512)

    def body(x_ref, g_ref, t_ref, loss_ref, dx_ref, dg_ref):
        @pl.when(pl.program_id(0) == 0)
        def _():
            dg_ref[...] = jnp.zeros_like(dg_ref)
            loss_ref[...] = jnp.zeros_like(loss_ref)

        gv = g_ref[...]
        y, xhat, rstd = _rms_fwd(x_ref[...], gv)
        err = y - t_ref[...]
        part = 0.5 * jnp.sum(jnp.mean(err * err, axis=-1, keepdims=True), axis=0, keepdims=True)
        loss_ref[...] += jnp.broadcast_to(part, loss_ref.shape)
        dy = err * (1.0 / D)
        dx, dgrow = _rms_bwd(dy, xhat, rstd, gv)
        dx_ref[...] = dx
        dg_ref[...] += jnp.sum(dgrow, axis=0, keepdims=True)

    row = pl.BlockSpec((tm, D), lambda i: (i, 0))
    one = pl.BlockSpec((1, D), lambda i: (0, 0))
    return pl.pallas_call(
        body, name=name, grid=(T // tm,),
        in_specs=[row, one, row],
        out_specs=[pl.BlockSpec((1, LANE), lambda i: (0, 0)), row, one],
        out_shape=[jax.ShapeDtypeStruct((1, LANE), F32), jax.ShapeDtypeStruct((T, D), F32),
                   jax.ShapeDtypeStruct((1, D), F32)],
        compiler_params=_cp(("arbitrary",)),
    )(x, g, target)


def _attn_bias(S, bq):
    d = jnp.arange(bq)[:, None] - jnp.arange(S)[None, :] + (S // bq - 1) * bq
    ok = d >= 0
    mult = ((ok & (d <= 128)).astype(F32) + (ok & (d % 4 == 0) & (d <= 512)).astype(F32)
            + (ok & (d % 16 == 0) & (d <= 2048)).astype(F32))
    return jnp.where(mult > 0, jnp.log(jnp.maximum(mult, 1.0)), NEG).astype(F32)


def _attn_fwd(proj, bias, B, S, name):
    T = B * S
    bq = bias.shape[0]
    nb = S // bq
    qcol, kcol, vcol = Q0 // LANE, K0 // LANE, V0 // LANE

    def body(q_ref, k_ref, v_ref, t_ref, o_ref, lse_ref, ks, vs):
        for hh in range(2):
            sl = slice(HEAD_DIM * hh, HEAD_DIM * (hh + 1))
            ks[hh] = k_ref[:, sl].astype(BF16)
            vs[hh] = v_ref[:, sl].astype(BF16)
        for hh in range(2):
            sl = slice(HEAD_DIM * hh, HEAD_DIM * (hh + 1))
            for qb in range(nb):
                w, off, rows = bq * (qb + 1), (nb - 1 - qb) * bq, slice(qb * bq, (qb + 1) * bq)
                q = (q_ref[rows, sl] * 0.125).astype(BF16)
                s = _dot_nt(q, ks[hh, 0:w, :]) + t_ref[:, off:off + w]
                m = jnp.max(s, axis=-1, keepdims=True)
                p = jnp.exp(s - m)
                l = jnp.sum(p, axis=-1, keepdims=True)
                o_ref[rows, sl] = _dot(p.astype(BF16), vs[hh, 0:w, :]) / l
                lse_ref[rows, hh:hh + 1] = m + jnp.log(l)

    blk = lambda c0: pl.BlockSpec((S, LANE), lambda b, p: (b, c0 + p))
    return pl.pallas_call(
        body, name=name, grid=(B, ATT_W // LANE),
        in_specs=[blk(qcol), blk(kcol), blk(vcol), _resident((bq, S))],
        out_specs=[pl.BlockSpec((S, LANE), lambda b, p: (b, p)),
                   pl.BlockSpec((None, None, S, 2), lambda b, p: (b, p, 0, 0))],
        out_shape=[jax.ShapeDtypeStruct((T, ATT_W), F32),
                   jax.ShapeDtypeStruct((B, ATT_W // LANE, S, 2), F32)],
        scratch_shapes=[pltpu.VMEM((2, S, HEAD_DIM), BF16)] * 2,
        compiler_params=_cp(("parallel", "parallel")),
    )(proj, proj, proj, bias)


def _attn_bwd(proj, o, lse, dy, bias, B, S, name):
    T = B * S
    bq = bias.shape[0]
    nb = S // bq
    qcol, kcol, vcol = Q0 // LANE, K0 // LANE, V0 // LANE

    def body(q_ref, k_ref, v_ref, o_ref, lse_ref, do_ref, t_ref, dq_ref, dk_ref, dv_ref, ks, vs, dks, dvs):
        for hh in range(2):
            sl = slice(HEAD_DIM * hh, HEAD_DIM * (hh + 1))
            ks[hh] = k_ref[:, sl].astype(BF16)
            vs[hh] = v_ref[:, sl].astype(BF16)
        dks[...] = jnp.zeros_like(dks)
        dvs[...] = jnp.zeros_like(dvs)
        for hh in range(2):
            sl = slice(HEAD_DIM * hh, HEAD_DIM * (hh + 1))
            for qb in range(nb):
                w, off, rows = bq * (qb + 1), (nb - 1 - qb) * bq, slice(qb * bq, (qb + 1) * bq)
                q = (q_ref[rows, sl] * 0.125).astype(BF16)
                do = do_ref[rows, sl]
                dob = do.astype(BF16)
                delta = jnp.sum(do * o_ref[rows, sl], axis=-1, keepdims=True)
                k, v = ks[hh, 0:w, :], vs[hh, 0:w, :]
                s = _dot_nt(q, k) + t_ref[:, off:off + w]
                p = jnp.exp(s - lse_ref[rows, hh:hh + 1])
                ds = (p * (_dot_nt(dob, v) - delta)).astype(BF16)
                dq_ref[rows, sl] = (_dot(ds, k) * 0.125).astype(dq_ref.dtype)
                dks[hh, 0:w, :] += _dot_tn(ds, q)
                dvs[hh, 0:w, :] += _dot_tn(p.astype(BF16), dob)
            dk_ref[:, sl] = dks[hh].astype(dk_ref.dtype)
            dv_ref[:, sl] = dvs[hh].astype(dv_ref.dtype)

    blk = lambda c0: pl.BlockSpec((S, LANE), lambda b, p: (b, c0 + p))
    own = pl.BlockSpec((S, LANE), lambda b, p: (b, p))
    return pl.pallas_call(
        body, name=name, grid=(B, ATT_W // LANE),
        in_specs=[blk(qcol), blk(kcol), blk(vcol), own,
                  pl.BlockSpec((None, None, S, 2), lambda b, p: (b, p, 0, 0)), own, _resident((bq, S))],
        out_specs=[own, own, own],
        out_shape=[jax.ShapeDtypeStruct((T, ATT_W), BF16)] * 3,
        scratch_shapes=[pltpu.VMEM((2, S, HEAD_DIM), BF16)] * 2 + [pltpu.VMEM((2, S, HEAD_DIM), F32)] * 2,
        compiler_params=_cp(("parallel", "parallel")),
    )(proj, proj, proj, o, lse, dy, bias)


def _conv_fwd(proj, cw, cb, B, S, name, after=None):
    T = B * S
    nc = SSD_CONV_DIM // LANE
    c0 = XBC0 // LANE
    extra, extra_specs = _after(after)

    def body(x_ref, w_ref, b_ref, *rest):
        o_ref = rest[-1]
        x = x_ref[...]
        t = lax.broadcasted_iota(jnp.int32, (S, 1), 0)
        acc = b_ref[...] + w_ref[SSD_CONV - 1:SSD_CONV, :] * x
        for k in range(SSD_CONV - 1):
            sh = SSD_CONV - 1 - k
            xs = jnp.where(t >= sh, pltpu.roll(x, sh, 0), 0.0)
            acc = acc + w_ref[k:k + 1, :] * xs
        o_ref[...] = acc

    return pl.pallas_call(
        body, name=name, grid=(B, nc),
        in_specs=[pl.BlockSpec((S, LANE), lambda b, j: (b, c0 + j)),
                  pl.BlockSpec((SUBLANE, LANE), lambda b, j: (0, j)),
                  pl.BlockSpec((1, LANE), lambda b, j: (0, j))] + extra_specs,
        out_specs=pl.BlockSpec((S, LANE), lambda b, j: (b, j)),
        out_shape=jax.ShapeDtypeStruct((T, SSD_CONV_DIM), F32),
        compiler_params=_cp(("parallel", "parallel")),
    )(proj, cw, cb, *extra)


def _conv_bwd(dpre, proj, cw, B, S, name):
    T = B * S
    nc = SSD_CONV_DIM // LANE
    c0 = XBC0 // LANE

    def body(d_ref, x_ref, w_ref, dx_ref, dwb_ref):
        @pl.when(pl.program_id(1) == 0)
        def _():
            dwb_ref[...] = jnp.zeros_like(dwb_ref)

        d = d_ref[...]
        x = x_ref[...]
        t = lax.broadcasted_iota(jnp.int32, (S, 1), 0)
        dx = w_ref[SSD_CONV - 1:SSD_CONV, :] * d
        rows = [None] * SUBLANE
        rows[SSD_CONV - 1] = jnp.sum(d * x, axis=0, keepdims=True)
        for k in range(SSD_CONV - 1):
            sh = SSD_CONV - 1 - k
            dx = dx + w_ref[k:k + 1, :] * jnp.where(t < S - sh, pltpu.roll(d, S - sh, 0), 0.0)
            xs = jnp.where(t >= sh, pltpu.roll(x, sh, 0), 0.0)
            rows[k] = jnp.sum(d * xs, axis=0, keepdims=True)
        rows[SSD_CONV] = jnp.sum(d, axis=0, keepdims=True)
        dx_ref[...] = dx.astype(BF16)
        r = lax.broadcasted_iota(jnp.int32, (SUBLANE, LANE), 0)
        upd = jnp.zeros((SUBLANE, LANE), F32)
        for k in range(SSD_CONV + 1):
            upd = upd + jnp.where(r == k, rows[k], 0.0)
        dwb_ref[...] += upd

    return pl.pallas_call(
        body, name=name, grid=(nc, B),
        in_specs=[pl.BlockSpec((S, LANE), lambda j, b: (b, j)),
                  pl.BlockSpec((S, LANE), lambda j, b: (b, c0 + j)),
                  pl.BlockSpec((SUBLANE, LANE), lambda j, b: (0, j))],
        out_specs=[pl.BlockSpec((S, LANE), lambda j, b: (b, j)),
                   pl.BlockSpec((SUBLANE, LANE), lambda j, b: (0, j))],
        out_shape=[jax.ShapeDtypeStruct((T, SSD_CONV_DIM), BF16),
                   jax.ShapeDtypeStruct((SUBLANE, SSD_CONV_DIM), F32)],
        compiler_params=_cp(("parallel", "arbitrary")),
    )(dpre, proj, cw)


def _ssd_consts():
    e = np.zeros((LANE, SSD_W), np.float32)
    p = np.zeros((SUBLANE, SSD_W), np.float32)
    for h in range(SSD_HEADS):
        e[h, HEAD_DIM * h:HEAD_DIM * (h + 1)] = 1.0
        p[h, HEAD_DIM * h] = 1.0
    return jnp.asarray(e), jnp.asarray(p)


def _ssd_chunk(pre, z, dtr, sprev, par, e_mat, psel):
    L = CHUNK
    xc = _silu(pre)
    xs, bm, cm = xc[:, :SSD_W], xc[:, SSD_W:SSD_W + 2 * SSD_STATE], xc[:, SSD_W + 2 * SSD_STATE:]
    dtb, alog, dskip, ng = par[0:1], par[1:2], par[2:3], par[3:4]
    dt = _softplus(_dot(dtr, e_mat, HI) + dtb)
    a = dt * (-jnp.exp(alog))
    X = xs * dt
    ri = lax.broadcasted_iota(jnp.int32, (L, L), 0)
    ci = lax.broadcasted_iota(jnp.int32, (L, L), 1)
    tril = ri >= ci
    acs = _dot(tril.astype(F32), a, HI)
    acs_t = _dot_nt(psel, acs, HI)
    ecs = jnp.exp(acs)
    alast = acs[L - 1:L, :]
    xd = (X * jnp.exp(alast - acs)).astype(BF16)
    xb = X.astype(BF16)
    col = lax.broadcasted_iota(jnp.int32, (1, SSD_W), 1)
    sb = sprev.astype(BF16)
    bgs = [bm[:, SSD_STATE * g:SSD_STATE * (g + 1)].astype(BF16) for g in range(2)]
    cgs = [cm[:, SSD_STATE * g:SSD_STATE * (g + 1)].astype(BF16) for g in range(2)]
    cbs = [_dot_nt(cgs[g], bgs[g]) for g in range(2)]
    first = lax.broadcasted_iota(jnp.int32, (1, LANE), 1) < HEAD_DIM
    y_tiles, s_tiles = [], []
    for t in range(SSD_W // LANE):
        cl = slice(LANE * t, LANE * (t + 1))
        xb_t, xd_t, sb_t = xb[:, cl], xd[:, cl], sb[:, cl]
        per_head = []
        for h in (2 * t, 2 * t + 1):
            seg = acs[:, HEAD_DIM * h:HEAD_DIM * h + 1] - acs_t[h:h + 1, :]
            dec = jnp.exp(jnp.where(tril, seg, NEG))
            per_head.append(_dot((cbs[h // 3] * dec).astype(BF16), xb_t))
        y_t = jnp.where(first, per_head[0], per_head[1])
        ga, gb = (2 * t) // 3, (2 * t + 1) // 3
        if ga == gb:
            y_off, s_add = _dot(cgs[ga], sb_t), _dot_tn(bgs[ga], xd_t)
        else:
            y_off = jnp.where(first, _dot(cgs[ga], sb_t), _dot(cgs[gb], sb_t))
            s_add = jnp.where(first, _dot_tn(bgs[ga], xd_t), _dot_tn(bgs[gb], xd_t))
        y_tiles.append(y_t + y_off * ecs[:, cl])
        s_tiles.append(s_add)
    y = dskip * xs + jnp.concatenate(y_tiles, axis=1)
    snew = sprev * jnp.exp(alast) + jnp.concatenate(s_tiles, axis=1)
    yg = y * _silu(z)
    sq = yg * yg
    g0 = col < SSD_W // 2
    ms0 = jnp.sum(jnp.where(g0, sq, 0.0), axis=-1, keepdims=True) * (2.0 / SSD_W)
    ms1 = jnp.sum(jnp.where(g0, 0.0, sq), axis=-1, keepdims=True) * (2.0 / SSD_W)
    r = jnp.where(g0, lax.rsqrt(ms0 + RMS_EPS), lax.rsqrt(ms1 + RMS_EPS))
    return yg * r * ng, snew


SSD_CHUNKS_PER_STEP = 2


def _ssd_chunks_per_step(S):
    k = SSD_CHUNKS_PER_STEP
    while (S // CHUNK) % k:
        k //= 2
    return k


def _ssd_fwd(pre, proj, par, B, S, name):
    T = B * S
    k = _ssd_chunks_per_step(S)
    nc, rows = S // (CHUNK * k), CHUNK * k
    e_mat, psel = _ssd_consts()

    def body(pre_ref, z_ref, dt_ref, par_ref, e_ref, p_ref, y_ref, sall_ref, st):
        @pl.when(pl.program_id(1) == 0)
        def _():
            st[...] = jnp.zeros_like(st)

        sprev = st[...]
        for i in range(k):
            r = slice(CHUNK * i, CHUNK * (i + 1))
            sall_ref[i] = sprev
            y, sprev = _ssd_chunk(pre_ref[r, :], z_ref[r, :], dt_ref[r, :], sprev, par_ref[...], e_ref[...],
                                  p_ref[...])
            y_ref[r, :] = y.astype(BF16)
        st[...] = sprev

    row = lambda b, c: b * nc + c
    full = lambda shp: pl.BlockSpec(shp, lambda b, c: (0, 0))
    return pl.pallas_call(
        body, name=name, grid=(B, nc),
        in_specs=[pl.BlockSpec((rows, SSD_CONV_DIM), lambda b, c: (row(b, c), 0)),
                  pl.BlockSpec((rows, SSD_W), lambda b, c: (row(b, c), Z0 // SSD_W)),
                  pl.BlockSpec((rows, LANE), lambda b, c: (row(b, c), DT0 // LANE)),
                  full((SUBLANE, SSD_W)), full((LANE, SSD_W)), full((SUBLANE, SSD_W))],
        out_specs=[pl.BlockSpec((rows, SSD_W), lambda b, c: (row(b, c), 0)),
                   pl.BlockSpec((k, SSD_STATE, SSD_W), lambda b, c: (row(b, c), 0, 0))],
        out_shape=[jax.ShapeDtypeStruct((T, SSD_W), BF16),
                   jax.ShapeDtypeStruct((B * nc * k, SSD_STATE, SSD_W), F32)],
        scratch_shapes=[pltpu.VMEM((SSD_STATE, SSD_W), F32)],
        compiler_params=_cp(("parallel", "arbitrary")),
    )(pre, proj, proj, par, e_mat, psel)


def _ssd_bwd(pre, proj, sall, dy, par, B, S, name):
    T = B * S
    k = _ssd_chunks_per_step(S)
    nc, rows = S // (CHUNK * k), CHUNK * k
    e_mat, psel = _ssd_consts()

    def body(pre_ref, z_ref, dt_ref, sall_ref, dy_ref, par_ref, e_ref, p_ref,
             dpre_ref, dz_ref, ddt_ref, dpar_ref, ds):
        b, c = pl.program_id(0), pl.program_id(1)

        @pl.when(c == 0)
        def _():
            ds[...] = jnp.zeros_like(ds)

        @pl.when((b == 0) & (c == 0))
        def _():
            dpar_ref[...] = jnp.zeros_like(dpar_ref)

        e_v, p_v = e_ref[...], p_ref[...]
        fn = lambda pre, z, dtr, sprev, par: _ssd_chunk(pre, z, dtr, sprev, par, e_v, p_v)
        dstate, dpar_sum = ds[...], None
        for i in reversed(range(k)):
            r = slice(CHUNK * i, CHUNK * (i + 1))
            _, vjp = jax.vjp(fn, pre_ref[r, :], z_ref[r, :], dt_ref[r, :], sall_ref[i], par_ref[...])
            dpre, dz, ddt, dstate, dpar = vjp((dy_ref[r, :], dstate))
            dpre_ref[r, :] = dpre
            dz_ref[r, :] = dz.astype(BF16)
            ddt_ref[r, :] = ddt.astype(BF16)
            dpar_sum = dpar if dpar_sum is None else dpar_sum + dpar
        dpar_ref[...] += dpar_sum
        ds[...] = dstate

    row = lambda b, c: b * nc + (nc - 1 - c)
    full = lambda shp: pl.BlockSpec(shp, lambda b, c: (0, 0))
    return pl.pallas_call(
        body, name=name, grid=(B, nc),
        in_specs=[pl.BlockSpec((rows, SSD_CONV_DIM), lambda b, c: (row(b, c), 0)),
                  pl.BlockSpec((rows, SSD_W), lambda b, c: (row(b, c), Z0 // SSD_W)),
                  pl.BlockSpec((rows, LANE), lambda b, c: (row(b, c), DT0 // LANE)),
                  pl.BlockSpec((k, SSD_STATE, SSD_W), lambda b, c: (row(b, c), 0, 0)),
                  pl.BlockSpec((rows, SSD_W), lambda b, c: (row(b, c), ATT_W // SSD_W)),
                  full((SUBLANE, SSD_W)), full((LANE, SSD_W)), full((SUBLANE, SSD_W))],
        out_specs=[pl.BlockSpec((rows, SSD_CONV_DIM), lambda b, c: (row(b, c), 0)),
                   pl.BlockSpec((rows, SSD_W), lambda b, c: (row(b, c), 0)),
                   pl.BlockSpec((rows, LANE), lambda b, c: (row(b, c), 0)),
                   full((SUBLANE, SSD_W))],
        out_shape=[jax.ShapeDtypeStruct((T, SSD_CONV_DIM), F32),
                   jax.ShapeDtypeStruct((T, SSD_W), BF16),
                   jax.ShapeDtypeStruct((T, LANE), BF16),
                   jax.ShapeDtypeStruct((SUBLANE, SSD_W), F32)],
        scratch_shapes=[pltpu.VMEM((SSD_STATE, SSD_W), F32)],
        compiler_params=_cp(("arbitrary", "arbitrary")),
    )(pre, proj, proj, sall, dy, par, e_mat, psel)


def _sgu_consts():
    e = np.zeros((SUBLANE, SGU_W), np.float32)
    for g in range(SGU_GROUPS):
        e[g, HEAD_DIM * g:HEAD_DIM * (g + 1)] = 1.0
    return jnp.asarray(e)


def _sgu_chunk(u_raw, v_raw, ln, w, bst, e4):
    L = CHUNK
    u = _gelu(u_raw)
    v = _gelu(v_raw)
    mu = jnp.mean(v, axis=-1, keepdims=True)
    vc = v - mu
    var = jnp.mean(vc * vc, axis=-1, keepdims=True)
    vn = vc * lax.rsqrt(var + LN_EPS) * ln[0:1] + ln[1:2]
    vb = vn.astype(BF16)
    ri = lax.broadcasted_iota(jnp.int32, (L, L), 0)
    ci = lax.broadcasted_iota(jnp.int32, (L, L), 1)
    tril = ri >= ci
    col = lax.broadcasted_iota(jnp.int32, (1, SGU_W), 1)
    mixed = _dot(bst, e4, HI)
    for g in range(SGU_GROUPS):
        wc = jnp.where(tril, w[g], 0.0).astype(BF16)
        gm = (col >= HEAD_DIM * g) & (col < HEAD_DIM * (g + 1))
        mixed = mixed + jnp.where(gm, _dot(wc, vb), 0.0)
    return u * mixed


def _sgu_fwd(proj, ln, w, bst, B, S, name, after=None):
    T = B * S
    e4 = _sgu_consts()
    extra, extra_specs = _after(after)

    def body(u_ref, v_ref, ln_ref, w_ref, b_ref, e_ref, *rest):
        y_ref = rest[-1]
        y_ref[...] = _sgu_chunk(u_ref[...], v_ref[...], ln_ref[...], w_ref[...], b_ref[...], e_ref[...]).astype(BF16)

    return pl.pallas_call(
        body, name=name, grid=(T // CHUNK,),
        in_specs=[pl.BlockSpec((CHUNK, SGU_W), lambda i: (i, U0 // SGU_W)),
                  pl.BlockSpec((CHUNK, SGU_W), lambda i: (i, VS0 // SGU_W)),
                  pl.BlockSpec((SUBLANE, SGU_W), lambda i: (0, 0)),
                  pl.BlockSpec((SGU_GROUPS, CHUNK, CHUNK), lambda i: (0, 0, 0)),
                  pl.BlockSpec((CHUNK, SUBLANE), lambda i: (0, 0)),
                  pl.BlockSpec((SUBLANE, SGU_W), lambda i: (0, 0))] + extra_specs,
        out_specs=pl.BlockSpec((CHUNK, SGU_W), lambda i: (i, 0)),
        out_shape=jax.ShapeDtypeStruct((T, SGU_W), BF16),
        compiler_params=_cp(("parallel",)),
    )(proj, proj, ln, w, bst, e4, *extra)


def _sgu_bwd(proj, dy, ln, w, bst, B, S, name):
    T = B * S
    e4 = _sgu_consts()
    ycol = (ATT_W + SSD_W) // SGU_W

    def body(u_ref, v_ref, dy_ref, ln_ref, w_ref, b_ref, e_ref, du_ref, dv_ref, dln_ref, dw_ref, db_ref):
        @pl.when(pl.program_id(0) == 0)
        def _():
            dln_ref[...] = jnp.zeros_like(dln_ref)
            dw_ref[...] = jnp.zeros_like(dw_ref)
            db_ref[...] = jnp.zeros_like(db_ref)

        e_v = e_ref[...]
        fn = lambda u, v, ln, w, b: _sgu_chunk(u, v, ln, w, b, e_v)
        _, vjp = jax.vjp(fn, u_ref[...], v_ref[...], ln_ref[...], w_ref[...], b_ref[...])
        du, dv, dln, dw, db = vjp(dy_ref[...])
        du_ref[...] = du.astype(BF16)
        dv_ref[...] = dv.astype(BF16)
        dln_ref[...] += dln
        dw_ref[...] += dw
        db_ref[...] += db

    c_ln = pl.BlockSpec((SUBLANE, SGU_W), lambda i: (0, 0))
    c_w = pl.BlockSpec((SGU_GROUPS, CHUNK, CHUNK), lambda i: (0, 0, 0))
    c_b = pl.BlockSpec((CHUNK, SUBLANE), lambda i: (0, 0))
    return pl.pallas_call(
        body, name=name, grid=(T // CHUNK,),
        in_specs=[pl.BlockSpec((CHUNK, SGU_W), lambda i: (i, U0 // SGU_W)),
                  pl.BlockSpec((CHUNK, SGU_W), lambda i: (i, VS0 // SGU_W)),
                  pl.BlockSpec((CHUNK, SGU_W), lambda i: (i, ycol)),
                  c_ln, c_w, c_b, pl.BlockSpec((SUBLANE, SGU_W), lambda i: (0, 0))],
        out_specs=[pl.BlockSpec((CHUNK, SGU_W), lambda i: (i, 0)),
                   pl.BlockSpec((CHUNK, SGU_W), lambda i: (i, 0)), c_ln, c_w, c_b],
        out_shape=[jax.ShapeDtypeStruct((T, SGU_W), BF16), jax.ShapeDtypeStruct((T, SGU_W), BF16),
                   jax.ShapeDtypeStruct((SUBLANE, SGU_W), F32),
                   jax.ShapeDtypeStruct((SGU_GROUPS, CHUNK, CHUNK), F32),
                   jax.ShapeDtypeStruct((CHUNK, SUBLANE), F32)],
        compiler_params=_cp(("arbitrary",)),
    )(proj, proj, dy, ln, w, bst, e4)


_HBM = pl.BlockSpec(memory_space=pltpu.HBM)
_SEM = pl.BlockSpec(memory_space=pltpu.SEMAPHORE)
_ANY = pl.BlockSpec(memory_space=pl.ANY)
_EFFECT = pltpu.SideEffectType.DATAFLOW_SIDE_EFFECTING


def _peers():
    x, y, c = lax.axis_index("x"), lax.axis_index("y"), lax.axis_index("c")
    out = []
    for p in range(1, N_DEV):
        px, py, pc = x ^ ((p >> 2) & 1), y ^ ((p >> 1) & 1), c ^ (p & 1)
        out.append(((px, py, pc), 4 * px + 2 * py + pc))
    return 4 * x + 2 * y + c, out


def _xchg_start(xs, a2a, order, name):
    n = len(xs)
    lands = [lax.empty(a.shape if f else (N_DEV,) + a.shape, a.dtype) for a, f in zip(xs, a2a)]

    def body(*refs):
        ins, zones = refs[:n], refs[n:2 * n]
        send_sems, recv_sems = refs[2 * n + 1], refs[2 * n + 2]
        token = refs[-1]
        me, peers = _peers()
        for p, (dev, peer) in enumerate(peers):
            for t in range(n):
                pltpu.make_async_remote_copy(
                    src_ref=ins[t].at[peer] if a2a[t] else ins[t], dst_ref=zones[t].at[me],
                    send_sem=send_sems.at[p * n + t], recv_sem=recv_sems.at[p * n + t],
                    device_id=dev, device_id_type=MESH).start()
        token[...] = jnp.zeros_like(token)

    hbm = lambda a: pltpu.HBM(a.shape, a.dtype)
    sems = pltpu.SemaphoreType.DMA(((N_DEV - 1) * n,))
    out = pl.pallas_call(
        body, name=name,
        in_specs=[_HBM] * (2 * n) + [_ANY],
        out_specs=[_SEM, _SEM] + [_HBM] * (2 * n) + [pl.BlockSpec(memory_space=pltpu.VMEM)],
        out_shape=[sems, sems] + [hbm(a) for a in xs] + [hbm(a) for a in lands]
        + [jax.ShapeDtypeStruct((SUBLANE, LANE), F32)],
        input_output_aliases={t: 2 + t for t in range(2 * n)},
        compiler_params=pltpu.CompilerParams(has_side_effects=_EFFECT),
    )(*[pltpu.with_memory_space_constraint(a, pltpu.HBM) for a in list(xs) + list(lands)], order)
    return out[0], out[1], out[2:2 + n], out[2 + n:2 + 2 * n], out[-1]


def _xchg_wait(started, a2a, after, name):
    send_sems, recv_sems, xs, lands, _ = started
    n = len(xs)

    def body(*refs):
        ins, zones = refs[:n], refs[n:2 * n]
        send_s, recv_s = refs[2 * n], refs[2 * n + 1]
        me, peers = _peers()
        cps = []
        for p, (dev, peer) in enumerate(peers):
            for t in range(n):
                cps.append(pltpu.make_async_remote_copy(
                    src_ref=ins[t].at[peer] if a2a[t] else ins[t], dst_ref=zones[t].at[peer],
                    send_sem=send_s.at[p * n + t], recv_sem=recv_s.at[p * n + t],
                    device_id=dev, device_id_type=MESH))
        for cp in cps:
            cp.wait_recv()
        for cp in cps:
            cp.wait_send()

    hbm = lambda a: pltpu.HBM(a.shape, a.dtype)
    out = pl.pallas_call(
        body, name=name,
        in_specs=[_HBM] * (2 * n) + [_SEM, _SEM, _ANY],
        out_specs=[_HBM] * (2 * n),
        out_shape=[hbm(a) for a in xs] + [hbm(a) for a in lands],
        input_output_aliases={t: t for t in range(2 * n)},
        compiler_params=pltpu.CompilerParams(has_side_effects=_EFFECT),
    )(*xs, *lands, send_sems, recv_sems, after)
    return out[:n], out[n:]


def _chip_peers():
    x, y, c = lax.axis_index("x"), lax.axis_index("y"), lax.axis_index("c")
    chips = [(1 - x, y), (x, 1 - y), (1 - x, 1 - y)]
    slot = lambda px, py, pc: 4 * px + 2 * py + pc
    return (x, y, c), chips, slot


def _gather_start(xs, order, name, own_slot):
    n = len(xs)
    lands = [lax.empty((N_DEV,) + a.shape, a.dtype) for a in xs]

    def body(*refs):
        ins, zones = refs[:n], refs[n:2 * n]
        send_sems, d2d_sems, ici_sems = refs[2 * n + 1:2 * n + 4]
        token = refs[-1]
        (x, y, c), chips, slot = _chip_peers()
        me = slot(x, y, c)
        for t in range(n):
            if own_slot:
                pltpu.make_async_copy(ins[t], zones[t].at[me], d2d_sems.at[n + t]).start()
            for j, (px, py) in enumerate(chips):
                pltpu.make_async_remote_copy(
                    src_ref=ins[t], dst_ref=zones[t].at[me], send_sem=send_sems.at[(1 + j) * n + t],
                    recv_sem=ici_sems.at[j * n + t], device_id=(px, py, c), device_id_type=MESH).start()
            pltpu.make_async_remote_copy(
                src_ref=ins[t], dst_ref=zones[t].at[me], send_sem=send_sems.at[t],
                recv_sem=d2d_sems.at[t], device_id=(x, y, 1 - c), device_id_type=MESH).start()
        token[...] = jnp.zeros_like(token)

    hbm = lambda a: pltpu.HBM(a.shape, a.dtype)
    dma = lambda k: pltpu.SemaphoreType.DMA((k,))
    out = pl.pallas_call(
        body, name=name,
        in_specs=[_HBM] * (2 * n) + [_ANY],
        out_specs=[_SEM, _SEM, _SEM] + [_HBM] * (2 * n) + [pl.BlockSpec(memory_space=pltpu.VMEM)],
        out_shape=[dma(4 * n), dma(2 * n), dma(3 * n)] + [hbm(a) for a in xs] + [hbm(a) for a in lands]
        + [jax.ShapeDtypeStruct((SUBLANE, LANE), F32)],
        input_output_aliases={t: 3 + t for t in range(2 * n)},
        compiler_params=pltpu.CompilerParams(has_side_effects=_EFFECT),
    )(*[pltpu.with_memory_space_constraint(a, pltpu.HBM) for a in list(xs) + list(lands)], order)
    return dict(send=out[0], d2d=out[1], ici=out[2], xs=out[3:3 + n], lands=out[3 + n:3 + 2 * n], token=out[-1],
                own_slot=own_slot)


def _gather_relay(st, after, name):
    n = len(st["xs"])

    def body(*refs):
        zones, ici_sems = refs[:n], refs[n]
        fsend, frecv = refs[n + 2], refs[n + 3]
        token = refs[-1]
        (x, y, c), chips, slot = _chip_peers()
        for t in range(n):
            for j, (px, py) in enumerate(chips):
                blk = zones[t].at[slot(px, py, c)]
                fwd = pltpu.make_async_remote_copy(
                    src_ref=blk, dst_ref=blk, send_sem=fsend.at[j * n + t], recv_sem=ici_sems.at[j * n + t],
                    device_id=(x, y, 1 - c), device_id_type=MESH)
                fwd.wait_recv()
                pltpu.make_async_remote_copy(
                    src_ref=blk, dst_ref=blk, send_sem=fsend.at[j * n + t], recv_sem=frecv.at[j * n + t],
                    device_id=(x, y, 1 - c), device_id_type=MESH).start()
        token[...] = jnp.zeros_like(token)

    hbm = lambda a: pltpu.HBM(a.shape, a.dtype)
    dma = lambda k: pltpu.SemaphoreType.DMA((k,))
    out = pl.pallas_call(
        body, name=name,
        in_specs=[_HBM] * n + [_SEM, _ANY],
        out_specs=[_SEM, _SEM] + [_HBM] * n + [pl.BlockSpec(memory_space=pltpu.VMEM)],
        out_shape=[dma(3 * n), dma(3 * n)] + [hbm(a) for a in st["lands"]]
        + [jax.ShapeDtypeStruct((SUBLANE, LANE), F32)],
        input_output_aliases={t: 2 + t for t in range(n)},
        compiler_params=pltpu.CompilerParams(has_side_effects=_EFFECT),
    )(*st["lands"], st["ici"], after)
    return dict(st, fsend=out[0], frecv=out[1], lands=out[2:2 + n], token=out[-1])


def _gather_wait(st, after, name):
    n = len(st["xs"])

    def body(*refs):
        ins, zones = refs[:n], refs[n:2 * n]
        send_sems, d2d_sems, fsend, frecv = refs[2 * n:2 * n + 4]
        (x, y, c), chips, slot = _chip_peers()
        sib = (x, y, 1 - c)
        for t in range(n):
            if st["own_slot"]:
                pltpu.make_async_copy(ins[t], zones[t].at[slot(x, y, c)], d2d_sems.at[n + t]).wait()
            mine = lambda s, r, dst: pltpu.make_async_remote_copy(
                src_ref=ins[t], dst_ref=dst, send_sem=s, recv_sem=r, device_id=sib, device_id_type=MESH)
            direct = mine(send_sems.at[t], d2d_sems.at[t], zones[t].at[slot(x, y, 1 - c)])
            direct.wait_recv()
            direct.wait_send()
            for j, (px, py) in enumerate(chips):
                mine(send_sems.at[(1 + j) * n + t], d2d_sems.at[t], zones[t].at[slot(px, py, c)]).wait_send()
                relayed = mine(fsend.at[j * n + t], frecv.at[j * n + t], zones[t].at[slot(px, py, 1 - c)])
                relayed.wait_recv()
                relayed.wait_send()

    hbm = lambda a: pltpu.HBM(a.shape, a.dtype)
    out = pl.pallas_call(
        body, name=name,
        in_specs=[_HBM] * (2 * n) + [_SEM] * 4 + [_ANY],
        out_specs=[_HBM] * (2 * n),
        out_shape=[hbm(a) for a in st["xs"]] + [hbm(a) for a in st["lands"]],
        input_output_aliases={t: t for t in range(2 * n)},
        compiler_params=pltpu.CompilerParams(has_side_effects=_EFFECT),
    )(*st["xs"], *st["lands"], st["send"], st["d2d"], st["fsend"], st["frecv"], after)
    return out[:n], out[n:]


def _cast_layers(pairs, name):
    def body(*refs):
        n = len(refs) // 2
        for i in range(n):
            refs[n + i][...] = refs[i][...].astype(BF16)

    in_specs = [pl.BlockSpec((None,) + w.shape[1:], functools.partial(lambda l, i: (l, 0, 0), l),
                             pipeline_mode=pl.Buffered(1)) for w, l in pairs]
    return pl.pallas_call(
        body, name=name, grid=(1,), in_specs=in_specs,
        out_specs=[pl.BlockSpec(w.shape[1:], lambda i: (0, 0)) for w, _ in pairs],
        out_shape=[jax.ShapeDtypeStruct(w.shape[1:], BF16) for w, _ in pairs],
        compiler_params=_cp(("arbitrary",)),
    )(*[w for w, _ in pairs])


ADAMW_BLOCK_ELEMS = 256 * 1024


def _adam_step(me, w, m, v, parts_ref, mine, out_refs):
    g = None
    for p in range(N_DEV):
        term = jnp.where(me == p, mine.astype(F32), parts_ref[p].astype(F32))
        g = term if g is None else g + term
    mn = ADAM_B1 * m + (1.0 - ADAM_B1) * g
    vn = ADAM_B2 * v + (1.0 - ADAM_B2) * (g * g)
    m_hat = mn / (1.0 - ADAM_B1 ** ADAM_STEP)
    v_hat = vn / (1.0 - ADAM_B2 ** ADAM_STEP)
    g_ref, d_ref, mo_ref, vo_ref = out_refs
    g_ref[...] = g
    d_ref[...] = -ADAM_LR * (m_hat / (jnp.sqrt(v_hat) + ADAM_EPS) + ADAM_WD * w)
    mo_ref[...] = mn
    vo_ref[...] = vn


def _adamw(me, w, m, v, parts, own, name, layer=0, into=None):
    L, R, C = w.shape
    P = parts.shape[0]
    tr = R
    t = 16
    while t <= R:
        if R % t == 0 and t * C <= ADAMW_BLOCK_ELEMS:
            tr = t
        t += 16
    if tr == R and R * C > ADAMW_BLOCK_ELEMS and R % 16 == 0:
        tr = 16
    own_all = own.shape[0] == P

    def body(me_ref, w_ref, m_ref, v_ref, p_ref, own_ref, *rest):
        _adam_step(me_ref[0], w_ref[...], m_ref[...], v_ref[...], p_ref, own_ref[...], rest[-4:])

    blk = pl.BlockSpec((None, tr, C), lambda i, me_ref: (layer, i, 0))
    own_blk = pl.BlockSpec((None, tr, C), lambda i, me_ref: (me_ref[0] if own_all else 0, i, 0))
    prev = list(into) if into is not None else []
    return pl.pallas_call(
        body, name=name,
        grid_spec=pltpu.PrefetchScalarGridSpec(
            num_scalar_prefetch=1, grid=(R // tr,),
            in_specs=[blk, blk, blk, pl.BlockSpec((P, tr, C), lambda i, me_ref: (0, i, 0)), own_blk]
            + [_ANY] * len(prev),
            out_specs=[blk] * 4),
        out_shape=[jax.ShapeDtypeStruct((L, R, C), F32)] * 4,
        input_output_aliases={6 + i: i for i in range(len(prev))},
        compiler_params=_cp(("parallel",)),
    )(me, w, m, v, parts, own, *prev)


def _perm_cols(w):
    pad = jnp.zeros((w.shape[0], LANE - SSD_HEADS), w.dtype)
    return jnp.concatenate([w[:, 0:1536], w[:, 2438:2694], w[:, 1536:2432], w[:, 2432:2438], pad,
                            w[:, 2694:2950]], axis=1)


def _unperm_cols(w):
    return jnp.concatenate([w[:, 0:1536], w[:, XBC0:XBC0 + SSD_CONV_DIM], w[:, DT0:DT0 + SSD_HEADS],
                            w[:, U0:U0 + SGU_W], w[:, VS0:VS0 + SGU_W]], axis=1)


_SMALL = ("ffn1_norm", "mix_norm", "conv_w", "conv_b", "dt_bias", "a_log", "d_skip", "ssd_norm",
          "sgu_ln_g", "sgu_ln_b", "sgu_w", "sgu_b", "ffn2_norm", "final_norm", "loss")


_SMALL_LAST = ("ffn1_norm",)
_SMALL_EARLY = tuple(k for k in _SMALL if k not in _SMALL_LAST)


def _pack(d, names):
    v = jnp.concatenate([d[k].astype(F32).reshape(-1) for k in names])
    n = v.shape[0]
    npad = -(-n // (LANE * 16)) * (LANE * 16)
    return jnp.pad(v, (0, npad - n)).reshape(npad // LANE, LANE)


def _unpack(p, shapes, names):
    v = p.reshape(-1)
    out, o = {}, 0
    for k in names:
        n = int(np.prod(shapes[k]))
        out[k] = v[o:o + n].reshape(shapes[k])
        o += n
    return out


def kernel(x, ffn1_norm, ffn1_w_gate, ffn1_w_up, ffn1_w_down, mix_norm, w_in, conv_w, conv_b, dt_bias, a_log, d_skip, ssd_norm, sgu_ln_g, sgu_ln_b, sgu_w, sgu_b, w_out, ffn2_norm, ffn2_w_gate, ffn2_w_up, ffn2_w_down, final_norm, loss_target, m_ffn1_norm, m_ffn1_w_gate, m_ffn1_w_up, m_ffn1_w_down, m_mix_norm, m_w_in, m_conv_w, m_conv_b, m_dt_bias, m_a_log, m_d_skip, m_ssd_norm, m_sgu_ln_g, m_sgu_ln_b, m_sgu_w, m_sgu_b, m_w_out, m_ffn2_norm, m_ffn2_w_gate, m_ffn2_w_up, m_ffn2_w_down, m_final_norm, v_ffn1_norm, v_ffn1_w_gate, v_ffn1_w_up, v_ffn1_w_down, v_mix_norm, v_w_in, v_conv_w, v_conv_b, v_dt_bias, v_a_log, v_d_skip, v_ssd_norm, v_sgu_ln_g, v_sgu_ln_b, v_sgu_w, v_sgu_b, v_w_out, v_ffn2_norm, v_ffn2_w_gate, v_ffn2_w_up, v_ffn2_w_down, v_final_norm):
    B, S, D = x.shape
    T = B * S
    L = ffn1_norm.shape[0]
    me = 4 * lax.axis_index("x") + 2 * lax.axis_index("y") + lax.axis_index("c")
    cs = conv_w.shape[2]
    W = dict(ffn1_norm=ffn1_norm, ffn1_w_gate=ffn1_w_gate, ffn1_w_up=ffn1_w_up, ffn1_w_down=ffn1_w_down,
             mix_norm=mix_norm, w_in=w_in, conv_w=conv_w, conv_b=conv_b, dt_bias=dt_bias, a_log=a_log,
             d_skip=d_skip, ssd_norm=ssd_norm, sgu_ln_g=sgu_ln_g, sgu_ln_b=sgu_ln_b, sgu_w=sgu_w, sgu_b=sgu_b,
             w_out=w_out, ffn2_norm=ffn2_norm, ffn2_w_gate=ffn2_w_gate, ffn2_w_up=ffn2_w_up,
             ffn2_w_down=ffn2_w_down, final_norm=final_norm)
    M = dict(ffn1_norm=m_ffn1_norm, ffn1_w_gate=m_ffn1_w_gate, ffn1_w_up=m_ffn1_w_up, ffn1_w_down=m_ffn1_w_down,
             mix_norm=m_mix_norm, w_in=m_w_in, conv_w=m_conv_w, conv_b=m_conv_b, dt_bias=m_dt_bias, a_log=m_a_log,
             d_skip=m_d_skip, ssd_norm=m_ssd_norm, sgu_ln_g=m_sgu_ln_g, sgu_ln_b=m_sgu_ln_b, sgu_w=m_sgu_w,
             sgu_b=m_sgu_b, w_out=m_w_out, ffn2_norm=m_ffn2_norm, ffn2_w_gate=m_ffn2_w_gate,
             ffn2_w_up=m_ffn2_w_up, ffn2_w_down=m_ffn2_w_down, final_norm=m_final_norm)
    V = dict(ffn1_norm=v_ffn1_norm, ffn1_w_gate=v_ffn1_w_gate, ffn1_w_up=v_ffn1_w_up, ffn1_w_down=v_ffn1_w_down,
             mix_norm=v_mix_norm, w_in=v_w_in, conv_w=v_conv_w, conv_b=v_conv_b, dt_bias=v_dt_bias, a_log=v_a_log,
             d_skip=v_d_skip, ssd_norm=v_ssd_norm, sgu_ln_g=v_sgu_ln_g, sgu_ln_b=v_sgu_ln_b, sgu_w=v_sgu_w,
             sgu_b=v_sgu_b, w_out=v_w_out, ffn2_norm=v_ffn2_norm, ffn2_w_gate=v_ffn2_w_gate,
             ffn2_w_up=v_ffn2_w_up, ffn2_w_down=v_ffn2_w_down, final_norm=v_final_norm)
    FFN1 = ("ffn1_w_gate", "ffn1_w_up", "ffn1_w_down")
    FFN2 = ("ffn2_w_gate", "ffn2_w_up", "ffn2_w_down")
    MIX = ("w_in", "w_out")
    big = FFN1 + MIX + FFN2
    col_sharded = lambda k: k.endswith("w_gate") or k.endswith("w_up")
    for dct in (W, M, V):
        for k in big:
            if col_sharded(k):
                dct[k] = jnp.swapaxes(dct[k], 1, 2)

    wgroups = [[(k, 0) for k in FFN1], [("w_in", 0), ("conv_w", None)], [("w_out", 0)] + [(k, 0) for k in FFN2]]
    for l in range(1, L):
        wgroups += [[(k, l) for k in FFN1] + [("w_in", l)], [("w_out", l)] + [(k, l) for k in FFN2]]
    wstarted, order = [], x
    later = [kl for grp in wgroups[1:] for kl in grp if kl[0] != "conv_w"]
    cast = dict(zip(wgroups[0], _cast_layers([(W[k], l) for k, l in wgroups[0]], "cast_first")))
    for gi, grp in enumerate(wgroups):
        if gi == 1:
            first = lax.optimization_barrier((W[later[0][0]], order))[0]
            srcs = [(first if i == 0 else W[k], l) for i, (k, l) in enumerate(later)]
            cast.update(zip(later, _cast_layers(srcs, "cast_rest")))
        xs = [conv_w if k == "conv_w" else cast[(k, l)] for k, l in grp]
        st = _gather_start(xs, order, f"gather_start_{gi}", own_slot=True)
        order = st["token"]
        wstarted.append(st)
    G = {}
    is_me = (jnp.arange(N_DEV) == me)

    zero1 = jnp.zeros((1,), F32)
    W["loss"], M["loss"], V["loss"] = zero1, zero1, zero1
    full_shapes = {k: (W[k].shape if k != "conv_w" else (L, SSD_CONV, SSD_CONV_DIM)) for k in _SMALL}
    embed = lambda a, k: a if k != "conv_w" else lax.dynamic_update_slice(
        jnp.zeros(full_shapes[k], F32), a, (0, 0, me * cs))
    small_packs = {names: [_pack({k: embed(d[k], k) for k in names}, names)[None] for d in (W, M, V)]
                   for names in (_SMALL_EARLY, _SMALL_LAST)}

    def relay(gi, after):
        wstarted[gi] = _gather_relay(wstarted[gi], after, f"gather_relay_{gi}")
        return wstarted[gi]["token"]

    def gathered(gi, after):
        own, lands = _gather_wait(wstarted[gi], after, f"gather_wait_{gi}")
        for key, o, z in zip(wgroups[gi], own, lands):
            G[key] = z if wstarted[gi]["own_slot"] else jnp.where(
                is_me.reshape((N_DEV,) + (1,) * o.ndim), o[None], z)

    def rows(k, l):
        a = G[(k, l)]
        return a.reshape(-1, a.shape[-1])

    bias = _attn_bias(S, min(256, S))
    row1 = lambda a: a.reshape(1, -1)

    def ffn1_params(l):
        return dict(g1=row1(ffn1_norm[l]), wg1=rows("ffn1_w_gate", l), wu1=rows("ffn1_w_up", l),
                    wd1=rows("ffn1_w_down", l))

    def out_params(l):
        return dict(wout=rows("w_out", l), g2=row1(ffn2_norm[l]), wg2=rows("ffn2_w_gate", l),
                    wu2=rows("ffn2_w_up", l), wd2=rows("ffn2_w_down", l))

    def mix_params(l):
        cw = jnp.transpose(G[("conv_w", None)][:, l], (1, 0, 2)).reshape(SSD_CONV, -1)
        return dict(
            gm=row1(mix_norm[l]), win=_perm_cols(rows("w_in", l)),
            cw=jnp.pad(cw, ((0, SUBLANE - SSD_CONV), (0, 0))), cb=row1(conv_b[l]),
            par=jnp.pad(jnp.stack([jnp.repeat(dt_bias[l], HEAD_DIM), jnp.repeat(a_log[l], HEAD_DIM),
                                   jnp.repeat(d_skip[l], HEAD_DIM), ssd_norm[l]]), ((0, SUBLANE - 4), (0, 0))),
            ln=jnp.pad(jnp.stack([sgu_ln_g[l], sgu_ln_b[l]]), ((0, SUBLANE - 2), (0, 0))),
            sw=sgu_w[l], bst=jnp.pad(sgu_b[l].T, ((0, 0), (0, SUBLANE - SGU_GROUPS))))

    xc = x.reshape(T, D)
    saved, lay = [], []
    for l in range(L):
        if l == 0:
            gathered(0, relay(0, order))
        else:
            gathered(1 + 2 * l, xc)
        p = ffn1_params(l)
        x1, *ffn1_saved = _ffn_fwd(xc, p["g1"], p["wg1"], p["wu1"], p["wd1"], f"ffn1_fwd_{l}")
        if l == 0:
            gathered(1, relay(1, x1))
        p.update(mix_params(l))
        lay.append(p)
        proj, ht = _norm_mm(x1, p["gm"], p["win"], f"in_proj_{l}")
        o_att, lse = _attn_fwd(proj, bias, B, S, f"attn_fwd_{l}")
        tok = relay(2 + 2 * l, o_att)
        pre = _conv_fwd(proj, p["cw"], p["cb"], B, S, f"conv_fwd_{l}", after=tok)
        y_ssd, sall = _ssd_fwd(pre, proj, p["par"], B, S, f"ssd_fwd_{l}")
        y_sgu = _sgu_fwd(proj, p["ln"], p["sw"], p["bst"], B, S, f"sgu_fwd_{l}", after=tok)
        ycat = jnp.concatenate([o_att.astype(BF16), y_ssd, y_sgu], axis=1)
        gathered(2 + 2 * l, ycat)
        p.update(out_params(l))
        x2 = _mm(ycat, p["wout"], "nn", f"out_proj_{l}", residual=x1)
        tok = relay(3 + 2 * l, x2) if l + 1 < L else None
        x3, *ffn2_saved = _ffn_fwd(x2, p["g2"], p["wg2"], p["wu2"], p["wd2"], f"ffn2_fwd_{l}", after=tok)
        saved.append(dict(x0=xc, ffn1=ffn1_saved, x1=x1, ht=ht, proj=proj, o_att=o_att, lse=lse, pre=pre,
                          sall=sall, ycat=ycat, x2=x2, ffn2=ffn2_saved))
        xc = x3
    loss_part, dx, dgf = _final_loss(xc, row1(final_norm), loss_target.reshape(T, D), "final_loss")

    gl = [dict() for _ in range(L)]
    gstarted, gorder = [], [order]

    def to_blocks(k, a):
        return a.reshape(N_DEV, -1, a.shape[-1]).astype(BF16)

    def send_grads(keys, l, extra, tag, small_names=None):
        xs = [to_blocks(k, gl[l][k]) for k in keys] + extra
        flags = [True] * len(keys) + [False] * len(extra)
        st = _xchg_start(xs, flags, gorder[0], f"grads_start_{tag}")
        gorder[0] = st[-1]
        gstarted.append((keys, l, st, flags, tag, small_names))

    def small_grads(names):
        sm = {}
        for k in names:
            if k == "final_norm":
                sm[k] = dgf.reshape(-1)
            elif k == "loss":
                sm[k] = loss_part[0, :1]
            else:
                sm[k] = jnp.stack([gl[l][k] for l in range(L)])
        return [_pack(sm, names)]

    def behind(a):
        return lax.optimization_barrier((a, gorder[0]))[0]

    for l in reversed(range(L)):
        p, s, g = lay[l], saved[l], gl[l]
        gfac, ufac, actt = s["ffn2"]
        dx2, dgt, dut, xn, dacc, g["ffn2_norm"] = _ffn_bwd_dx(
            dx, s["x2"], p["g2"], gfac, ufac, p["wg2"], p["wu2"], p["wd2"], f"ffn2_bwd_{l}")
        g["ffn2_w_gate"], g["ffn2_w_up"], g["ffn2_w_down"] = _ffn_dw(dgt, dut, actt, xn, dacc, f"ffn2_dw_{l}")
        if l == 0:
            send_grads(FFN2, 0, [], "l0f")
            dx2 = behind(dx2)
        dycat = _mm(dx2, p["wout"], "nt", f"out_proj_dx_{l}")
        g["w_out"] = _mm(s["ycat"], dx2, "tn", f"out_proj_dw_{l}", out_dtype=BF16, tm_cap=1024, tk_cap=512)
        dq, dk, dv = _attn_bwd(s["proj"], s["o_att"], s["lse"], dycat, bias, B, S, f"attn_bwd_{l}")
        dpre, dz, ddt, dpar = _ssd_bwd(s["pre"], s["proj"], s["sall"], dycat, p["par"], B, S, f"ssd_bwd_{l}")
        dxbc, dwb = _conv_bwd(dpre, s["proj"], p["cw"], B, S, f"conv_bwd_{l}")
        du, dvs, dln, dsw, dbst = _sgu_bwd(s["proj"], dycat, p["ln"], p["sw"], p["bst"], B, S, f"sgu_bwd_{l}")
        hsum = lambda r: r.reshape(SSD_HEADS, HEAD_DIM).sum(-1)
        g["conv_w"], g["conv_b"] = dwb[:SSD_CONV], dwb[SSD_CONV]
        g["dt_bias"], g["a_log"], g["d_skip"], g["ssd_norm"] = hsum(dpar[0]), hsum(dpar[1]), hsum(dpar[2]), dpar[3]
        g["sgu_ln_g"], g["sgu_ln_b"], g["sgu_w"], g["sgu_b"] = dln[0], dln[1], dsw, dbst[:, :SGU_GROUPS].T
        dproj = jnp.concatenate([dq, dk, dv, dz, du, dxbc, ddt, dvs], axis=1)
        g["w_in"] = _unperm_cols(_mm_resident_lhs(s["ht"], dproj, f"in_proj_dw_{l}"))
        dx1, g["mix_norm"] = _norm_mm_bwd(dproj, s["x1"], p["gm"], p["win"], dx2, f"in_proj_bwd_{l}")
        if l == 0:
            send_grads(MIX, 0, small_grads(_SMALL_EARLY), "l0a", _SMALL_EARLY)
            dx1, small_packs = lax.optimization_barrier((behind(dx1), small_packs))
        gfac, ufac, actt = s["ffn1"]
        dx, dgt, dut, xn, dacc, g["ffn1_norm"] = _ffn_bwd_dx(
            dx1, s["x0"], p["g1"], gfac, ufac, p["wg1"], p["wu1"], p["wd1"], f"ffn1_bwd_{l}")
        g["ffn1_w_gate"], g["ffn1_w_up"], g["ffn1_w_down"] = _ffn_dw(dgt, dut, actt, xn, dacc, f"ffn1_dw_{l}")
        if l > 0:
            send_grads(big, l, [], f"l{l}")
            dx = behind(dx)
    grad_x = dx.reshape(B, S, D)
    send_grads(FFN1, 0, small_grads(_SMALL_LAST), "l0b", _SMALL_LAST)

    res, after = {}, gorder[0]
    small_out = [dict() for _ in range(4)]
    me1 = me.reshape(1).astype(jnp.int32)
    for keys, l, st, flags, tag, names in gstarted:
        own, lands = _xchg_wait(st, flags, after, f"grads_wait_{tag}")
        for k, mine, pk in zip(keys, own, lands):
            res[k] = _adamw(me1, W[k], M[k], V[k], pk, mine, f"adamw_{k}_{l}", layer=l, into=res.get(k))
        done = [res[k][0] for k in keys]
        if names:
            outs = _adamw(me1, *small_packs[names], lands[-1], own[-1][None], f"adamw_small_{tag}")
            for d, o in zip(small_out, outs):
                u = _unpack(o, full_shapes, names)
                if "conv_w" in u:
                    u["conv_w"] = lax.dynamic_slice(u["conv_w"], (0, 0, me * cs), (L, SSD_CONV, cs))
                d.update(u)
                done.extend(u.values())
        after = lax.optimization_barrier(tuple(done))[0]
    back = lambda k, a: jnp.swapaxes(a, 1, 2) if col_sharded(k) else a
    grads, deltas, new_m, new_v = [dict({k: back(k, res[k][i]) for k in big}, **small_out[i]) for i in range(4)]

    names = ("ffn1_norm", "ffn1_w_gate", "ffn1_w_up", "ffn1_w_down", "mix_norm", "w_in", "conv_w", "conv_b",
             "dt_bias", "a_log", "d_skip", "ssd_norm", "sgu_ln_g", "sgu_ln_b", "sgu_w", "sgu_b", "w_out",
             "ffn2_norm", "ffn2_w_gate", "ffn2_w_up", "ffn2_w_down", "final_norm")
    loss = grads["loss"][0]
    return (loss, grad_x, *[grads[n] for n in names], *[deltas[n] for n in names],
            *[new_m[n] for n in names], *[new_v[n] for n in names])
```

```python
import functools

import numpy as np
import jax
import jax.numpy as jnp
from jax import lax
from jax.experimental import pallas as pl
from jax.experimental.pallas import tpu as pltpu

F32, BF16 = jnp.float32, jnp.bfloat16
HI = lax.Precision.HIGH
MESH = pl.DeviceIdType.MESH
N_DEV = 8
VMEM_LIMIT_BYTES = 56 * 1024 * 1024
LANE, SUBLANE = 128, 8

HEAD_DIM = 64
ATT_W = 384
SSD_W = 384
SSD_HEADS = 6
SSD_STATE = 128
SSD_CONV = 4
CHUNK = 128
SSD_CONV_DIM = 896
SGU_W = 256
SGU_GROUPS = 4
D_IN = 2950
RMS_EPS = 1e-6
LN_EPS = 1e-5
NEG = -1e30

PW = 3072
Q0, K0, V0, Z0, U0, XBC0, DT0, VS0 = 0, 384, 768, 1152, 1536, 1792, 2688, 2816

ADAM_LR, ADAM_B1, ADAM_B2, ADAM_EPS, ADAM_WD, ADAM_STEP = 0.001, 0.9, 0.999, 1e-08, 0.01, 10


def _cp(sem=None):
    return pltpu.CompilerParams(dimension_semantics=sem, vmem_limit_bytes=VMEM_LIMIT_BYTES)


def _tile(n, cap, mult=LANE):
    best = None
    t = mult
    while t <= min(n, cap):
        if n % t == 0:
            best = t
        t += mult
    return best if best is not None else n


def _dot(a, b, prec=None):
    return jnp.dot(a, b, preferred_element_type=F32, precision=prec)


def _dot_nt(a, b, prec=None):
    return lax.dot_general(a, b, (((1,), (1,)), ((), ())), preferred_element_type=F32, precision=prec)


def _dot_tn(a, b, prec=None):
    return lax.dot_general(a, b, (((0,), (0,)), ((), ())), preferred_element_type=F32, precision=prec)


def _sigmoid(x):
    return 1.0 / (1.0 + jnp.exp(-x))


def _silu(x):
    return x * _sigmoid(x)


def _gelu(x):
    return 0.5 * x * (1.0 + lax.erf(x * 0.7071067811865476))


def _softplus(x):
    return jnp.maximum(x, 0.0) + jnp.log(1.0 + jnp.exp(-jnp.abs(x)))


def _rms_fwd(x, g):
    rstd = lax.rsqrt(jnp.mean(x * x, axis=-1, keepdims=True) + RMS_EPS)
    xhat = x * rstd
    return xhat * g, xhat, rstd


def _rms_bwd(dy, xhat, rstd, g):
    dxhat = dy * g
    dx = rstd * (dxhat - xhat * jnp.mean(dxhat * xhat, axis=-1, keepdims=True))
    return dx, dy * xhat


def _resident(shape):
    return pl.BlockSpec(shape, lambda *_: (0,) * len(shape), pipeline_mode=pl.Buffered(1))


def _mm(a, b, mode, name, out_dtype=F32, residual=None, tm_cap=512, tn_cap=1024, tk_cap=1024):
    if mode == "nn":
        (M, K), (_, N) = a.shape, b.shape
    elif mode == "nt":
        (M, K), (N, _) = a.shape, b.shape
    else:
        (K, M), (_, N) = a.shape, b.shape
    tm, tn, tk = _tile(M, tm_cap), _tile(N, tn_cap), _tile(K, tk_cap)
    nk = K // tk
    if mode == "tn":
        a_spec = pl.BlockSpec((tk, tm), lambda i, j, k: (k, i))
    else:
        a_spec = pl.BlockSpec((tm, tk), lambda i, j, k: (i, k))
    if mode == "nt":
        b_spec = pl.BlockSpec((tn, tk), lambda i, j, k: (j, k))
    else:
        b_spec = pl.BlockSpec((tk, tn), lambda i, j, k: (k, j))
    o_spec = pl.BlockSpec((tm, tn), lambda i, j, k: (i, j))
    has_res = residual is not None

    def prod(a_ref, b_ref):
        av = a_ref[...].astype(BF16)
        bv = b_ref[...].astype(BF16)
        if mode == "nn":
            return _dot(av, bv)
        if mode == "nt":
            return _dot_nt(av, bv)
        return _dot_tn(av, bv)

    def body(*refs):
        a_ref, b_ref = refs[:2]
        r_ref = refs[2] if has_res else None
        o_ref = refs[2 + has_res]
        if nk == 1:
            o = prod(a_ref, b_ref)
            if has_res:
                o = r_ref[...] + o
            o_ref[...] = o.astype(out_dtype)
            return
        acc = refs[3 + has_res]
        k = pl.program_id(2)

        @pl.when(k == 0)
        def _():
            acc[...] = jnp.zeros_like(acc)

        acc[...] += prod(a_ref, b_ref)

        @pl.when(k == nk - 1)
        def _():
            o = acc[...]
            if has_res:
                o = r_ref[...] + o
            o_ref[...] = o.astype(out_dtype)

    ins = [a, b] + ([residual] if has_res else [])
    in_specs = [a_spec, b_spec] + ([o_spec] if has_res else [])
    return pl.pallas_call(
        body, name=name, grid=(M // tm, N // tn, nk),
        in_specs=in_specs, out_specs=o_spec,
        out_shape=jax.ShapeDtypeStruct((M, N), out_dtype),
        scratch_shapes=[pltpu.VMEM((tm, tn), F32)] if nk > 1 else [],
        compiler_params=_cp(("parallel", "parallel", "arbitrary")),
    )(*ins)


def _after(after):
    return ([after], [_ANY]) if after is not None else ([], [])


def _ffn_fwd(x, g, wgt, wut, wd, name, after=None):
    T, D = x.shape
    F = wgt.shape[0]
    tm = _tile(T, 256)

    def body(x_ref, g_ref, wg_ref, wu_ref, wd_ref, *rest):
        out_ref, dgf_ref, sl_ref, actt_ref = rest[-4:]
        xv = x_ref[...]
        xn = _rms_fwd(xv, g_ref[...])[0].astype(BF16)
        gate = _dot_nt(xn, wg_ref[...])
        up = _dot_nt(xn, wu_ref[...])
        sig = _sigmoid(gate)
        sl = gate * sig
        dgf_ref[...] = (up * (sig * (1.0 + gate * (1.0 - sig)))).astype(BF16)
        sl_ref[...] = sl.astype(BF16)
        act = (sl * up).astype(BF16)
        actt_ref[...] = act.T
        out_ref[...] = xv + 0.5 * _dot(act, wd_ref[...])

    row = lambda w: pl.BlockSpec((tm, w), lambda i: (i, 0))
    extra, extra_specs = _after(after)
    return pl.pallas_call(
        body, name=name, grid=(T // tm,),
        in_specs=[row(D), _resident((1, D)), _resident((F, D)), _resident((F, D)), _resident((F, D))] + extra_specs,
        out_specs=[row(D), row(F), row(F), pl.BlockSpec((F, tm), lambda i: (0, i))],
        out_shape=[jax.ShapeDtypeStruct((T, D), F32),
                   jax.ShapeDtypeStruct((T, F), BF16),
                   jax.ShapeDtypeStruct((T, F), BF16),
                   jax.ShapeDtypeStruct((F, T), BF16)],
        compiler_params=_cp(("parallel",)),
    )(x, g, wgt, wut, wd, *extra)


def _ffn_bwd_dx(dout, x, g, dgf, sl, wg, wu, wd, name):
    T, D = x.shape
    F = wg.shape[0]
    tm = _tile(T, 256)

    def body(dout_ref, x_ref, g_ref, dgf_ref, sl_ref, wg_ref, wu_ref, wd_ref,
             dx_ref, dgt_ref, dut_ref, xn_ref, dacc_ref, dg_ref):
        @pl.when(pl.program_id(0) == 0)
        def _():
            dg_ref[...] = jnp.zeros_like(dg_ref)

        gv = g_ref[...]
        dout_v = dout_ref[...]
        xn, xhat, rstd = _rms_fwd(x_ref[...], gv)
        xn_ref[...] = xn.astype(BF16)
        dacc = (0.5 * dout_v).astype(BF16)
        dacc_ref[...] = dacc
        dact = _dot_nt(dacc, wd_ref[...])
        dgate = (dact * dgf_ref[...].astype(F32)).astype(BF16)
        dup = (dact * sl_ref[...].astype(F32)).astype(BF16)
        dgt_ref[...] = dgate.T
        dut_ref[...] = dup.T
        dxn = _dot(dgate, wg_ref[...]) + _dot(dup, wu_ref[...])
        dx, dgrow = _rms_bwd(dxn, xhat, rstd, gv)
        dx_ref[...] = dout_v + dx
        dg_ref[...] += jnp.sum(dgrow, axis=0, keepdims=True)

    row = lambda w: pl.BlockSpec((tm, w), lambda i: (i, 0))
    tr = pl.BlockSpec((F, tm), lambda i: (0, i))
    return pl.pallas_call(
        body, name=name, grid=(T // tm,),
        in_specs=[row(D), row(D), _resident((1, D)), row(F), row(F),
                  _resident((F, D)), _resident((F, D)), _resident((F, D))],
        out_specs=[row(D), tr, tr, row(D), row(D), pl.BlockSpec((1, D), lambda i: (0, 0))],
        out_shape=[jax.ShapeDtypeStruct((T, D), F32)] + [jax.ShapeDtypeStruct((F, T), BF16)] * 2
        + [jax.ShapeDtypeStruct((T, D), BF16)] * 2 + [jax.ShapeDtypeStruct((1, D), F32)],
        compiler_params=_cp(("arbitrary",)),
    )(dout, x, g, dgf, sl, wg, wu, wd)


def _ffn_dw(dgt, dut, actt, xn, dacc, name):
    F, T = dgt.shape
    D = xn.shape[1]
    th = _tile(F, 256)

    def body(dg_ref, du_ref, a_ref, xn_ref, dacc_ref, dwg_ref, dwu_ref, dwd_ref):
        xv = xn_ref[...]
        dwg_ref[...] = _dot(dg_ref[...], xv).astype(BF16)
        dwu_ref[...] = _dot(du_ref[...], xv).astype(BF16)
        dwd_ref[...] = _dot(a_ref[...], dacc_ref[...]).astype(BF16)

    tile = pl.BlockSpec((th, T), lambda j: (j, 0))
    out = pl.BlockSpec((th, D), lambda j: (j, 0))
    return pl.pallas_call(
        body, name=name, grid=(F // th,),
        in_specs=[tile, tile, tile, _resident((T, D)), _resident((T, D))],
        out_specs=[out, out, out], out_shape=[jax.ShapeDtypeStruct((F, D), BF16)] * 3,
        compiler_params=_cp(("parallel",)),
    )(dgt, dut, actt, xn, dacc)


def _dw_one(lt, r, name, after=None):
    F, T = lt.shape
    D = r.shape[1]
    th = _tile(F, 256)
    extra, extra_specs = _after(after)

    def body(l_ref, r_ref, *rest):
        rest[-1][...] = _dot(l_ref[...], r_ref[...]).astype(BF16)

    return pl.pallas_call(
        body, name=name, grid=(F // th,),
        in_specs=[pl.BlockSpec((th, T), lambda j: (j, 0)), _resident((T, D))] + extra_specs,
        out_specs=pl.BlockSpec((th, D), lambda j: (j, 0)),
        out_shape=jax.ShapeDtypeStruct((F, D), BF16),
        compiler_params=_cp(("parallel",)),
    )(lt, r, *extra)


def _norm_mm(x, g, w, name):
    T, D = x.shape
    N = w.shape[1]
    tm = _tile(T, 512)

    def body(x_ref, g_ref, w_ref, o_ref, ht_ref):
        xn = _rms_fwd(x_ref[...], g_ref[...])[0]
        ht_ref[...] = xn.T.astype(BF16)
        o_ref[...] = _dot(xn.astype(BF16), w_ref[...])

    return pl.pallas_call(
        body, name=name, grid=(T // tm,),
        in_specs=[pl.BlockSpec((tm, D), lambda i: (i, 0)), _resident((1, D)), _resident((D, N))],
        out_specs=[pl.BlockSpec((tm, N), lambda i: (i, 0)), pl.BlockSpec((D, tm), lambda i: (0, i))],
        out_shape=[jax.ShapeDtypeStruct((T, N), F32), jax.ShapeDtypeStruct((D, T), BF16)],
        compiler_params=_cp(("parallel",)),
    )(x, g, w)


def _norm_mm_bwd(dproj, x, g, w, dres, name):
    T, D = x.shape
    N = w.shape[1]
    tm = _tile(T, 512)

    def body(dp_ref, x_ref, g_ref, w_ref, dres_ref, dx_ref, dg_ref):
        @pl.when(pl.program_id(0) == 0)
        def _():
            dg_ref[...] = jnp.zeros_like(dg_ref)

        gv = g_ref[...]
        dh = _dot_nt(dp_ref[...], w_ref[...])
        _, xhat, rstd = _rms_fwd(x_ref[...], gv)
        dx, dgrow = _rms_bwd(dh, xhat, rstd, gv)
        dx_ref[...] = dres_ref[...] + dx
        dg_ref[...] += jnp.sum(dgrow, axis=0, keepdims=True)

    row = pl.BlockSpec((tm, D), lambda i: (i, 0))
    one = pl.BlockSpec((1, D), lambda i: (0, 0))
    return pl.pallas_call(
        body, name=name, grid=(T // tm,),
        in_specs=[pl.BlockSpec((tm, N), lambda i: (i, 0)), row, _resident((1, D)), _resident((D, N)), row],
        out_specs=[row, one],
        out_shape=[jax.ShapeDtypeStruct((T, D), F32), jax.ShapeDtypeStruct((1, D), F32)],
        compiler_params=_cp(("arbitrary",)),
    )(dproj, x, g, w, dres)


def _mm_resident_lhs(at, b, name, tn_cap=512):
    M, K = at.shape
    N = b.shape[1]
    tn = _tile(N, tn_cap)

    def body(a_ref, b_ref, o_ref):
        o_ref[...] = _dot(a_ref[...], b_ref[...]).astype(BF16)

    return pl.pallas_call(
        body, name=name, grid=(N // tn,),
        in_specs=[_resident((M, K)), pl.BlockSpec((K, tn), lambda j: (0, j))],
        out_specs=pl.BlockSpec((M, tn), lambda j: (0, j)),
        out_shape=jax.ShapeDtypeStruct((M, N), BF16),
        compiler_params=_cp(("parallel",)),
    )(at, b)


def _final_loss(x, g, target, name):
    T, D = x.shape
    tm = _tile(T, 512)

    def body(x_ref, g_ref, t_ref, loss_ref, dx_ref, dg_ref):
        @pl.when(pl.program_id(0) == 0)
        def _():
            dg_ref[...] = jnp.zeros_like(dg_ref)
            loss_ref[...] = jnp.zeros_like(loss_ref)

        gv = g_ref[...]
        y, xhat, rstd = _rms_fwd(x_ref[...], gv)
        err = y - t_ref[...]
        part = 0.5 * jnp.sum(jnp.mean(err * err, axis=-1, keepdims=True), axis=0, keepdims=True)
        loss_ref[...] += jnp.broadcast_to(part, loss_ref.shape)
        dy = err * (1.0 / D)
        dx, dgrow = _rms_bwd(dy, xhat, rstd, gv)
        dx_ref[...] = dx
        dg_ref[...] += jnp.sum(dgrow, axis=0, keepdims=True)

    row = pl.BlockSpec((tm, D), lambda i: (i, 0))
    one = pl.BlockSpec((1, D), lambda i: (0, 0))
    return pl.pallas_call(
        body, name=name, grid=(T // tm,),
        in_specs=[row, one, row],
        out_specs=[pl.BlockSpec((1, LANE), lambda i: (0, 0)), row, one],
        out_shape=[jax.ShapeDtypeStruct((1, LANE), F32), jax.ShapeDtypeStruct((T, D), F32),
                   jax.ShapeDtypeStruct((1, D), F32)],
        compiler_params=_cp(("arbitrary",)),
    )(x, g, target)


def _attn_bias(S, bq):
    d = jnp.arange(bq)[:, None] - jnp.arange(S)[None, :] + (S // bq - 1) * bq
    ok = d >= 0
    mult = ((ok & (d <= 128)).astype(F32) + (ok & (d % 4 == 0) & (d <= 512)).astype(F32)
            + (ok & (d % 16 == 0) & (d <= 2048)).astype(F32))
    return jnp.where(mult > 0, jnp.log(jnp.maximum(mult, 1.0)), NEG).astype(F32)


def _attn_fwd(proj, bias, B, S, name):
    T = B * S
    bq = bias.shape[0]
    nb = S // bq
    qcol, kcol, vcol = Q0 // LANE, K0 // LANE, V0 // LANE

    def body(q_ref, k_ref, v_ref, t_ref, o_ref, lse_ref, ks, vs):
        for hh in range(2):
            sl = slice(HEAD_DIM * hh, HEAD_DIM * (hh + 1))
            ks[hh] = k_ref[:, sl].astype(BF16)
            vs[hh] = v_ref[:, sl].astype(BF16)
        for hh in range(2):
            sl = slice(HEAD_DIM * hh, HEAD_DIM * (hh + 1))
            for qb in range(nb):
                w, off, rows = bq * (qb + 1), (nb - 1 - qb) * bq, slice(qb * bq, (qb + 1) * bq)
                q = (q_ref[rows, sl] * 0.125).astype(BF16)
                s = _dot_nt(q, ks[hh, 0:w, :]) + t_ref[:, off:off + w]
                m = jnp.max(s, axis=-1, keepdims=True)
                p = jnp.exp(s - m)
                l = jnp.sum(p, axis=-1, keepdims=True)
                o_ref[rows, sl] = _dot(p.astype(BF16), vs[hh, 0:w, :]) / l
                lse_ref[rows, hh:hh + 1] = m + jnp.log(l)

    blk = lambda c0: pl.BlockSpec((S, LANE), lambda b, p: (b, c0 + p))
    return pl.pallas_call(
        body, name=name, grid=(B, ATT_W // LANE),
        in_specs=[blk(qcol), blk(kcol), blk(vcol), _resident((bq, S))],
        out_specs=[pl.BlockSpec((S, LANE), lambda b, p: (b, p)),
                   pl.BlockSpec((None, None, S, 2), lambda b, p: (b, p, 0, 0))],
        out_shape=[jax.ShapeDtypeStruct((T, ATT_W), F32),
                   jax.ShapeDtypeStruct((B, ATT_W // LANE, S, 2), F32)],
        scratch_shapes=[pltpu.VMEM((2, S, HEAD_DIM), BF16)] * 2,
        compiler_params=_cp(("parallel", "parallel")),
    )(proj, proj, proj, bias)


def _attn_bwd(proj, o, lse, dy, bias, B, S, name):
    T = B * S
    bq = bias.shape[0]
    nb = S // bq
    qcol, kcol, vcol = Q0 // LANE, K0 // LANE, V0 // LANE

    def body(q_ref, k_ref, v_ref, o_ref, lse_ref, do_ref, t_ref, dq_ref, dk_ref, dv_ref, ks, vs, dks, dvs):
        for hh in range(2):
            sl = slice(HEAD_DIM * hh, HEAD_DIM * (hh + 1))
            ks[hh] = k_ref[:, sl].astype(BF16)
            vs[hh] = v_ref[:, sl].astype(BF16)
        dks[...] = jnp.zeros_like(dks)
        dvs[...] = jnp.zeros_like(dvs)
        for hh in range(2):
            sl = slice(HEAD_DIM * hh, HEAD_DIM * (hh + 1))
            for qb in range(nb):
                w, off, rows = bq * (qb + 1), (nb - 1 - qb) * bq, slice(qb * bq, (qb + 1) * bq)
                q = (q_ref[rows, sl] * 0.125).astype(BF16)
                do = do_ref[rows, sl]
                dob = do.astype(BF16)
                delta = jnp.sum(do * o_ref[rows, sl], axis=-1, keepdims=True)
                k, v = ks[hh, 0:w, :], vs[hh, 0:w, :]
                s = _dot_nt(q, k) + t_ref[:, off:off + w]
                p = jnp.exp(s - lse_ref[rows, hh:hh + 1])
                ds = (p * (_dot_nt(dob, v) - delta)).astype(BF16)
                dq_ref[rows, sl] = (_dot(ds, k) * 0.125).astype(dq_ref.dtype)
                dks[hh, 0:w, :] += _dot_tn(ds, q)
                dvs[hh, 0:w, :] += _dot_tn(p.astype(BF16), dob)
            dk_ref[:, sl] = dks[hh].astype(dk_ref.dtype)
            dv_ref[:, sl] = dvs[hh].astype(dv_ref.dtype)

    blk = lambda c0: pl.BlockSpec((S, LANE), lambda b, p: (b, c0 + p))
    own = pl.BlockSpec((S, LANE), lambda b, p: (b, p))
    return pl.pallas_call(
        body, name=name, grid=(B, ATT_W // LANE),
        in_specs=[blk(qcol), blk(kcol), blk(vcol), own,
                  pl.BlockSpec((None, None, S, 2), lambda b, p: (b, p, 0, 0)), own, _resident((bq, S))],
        out_specs=[own, own, own],
        out_shape=[jax.ShapeDtypeStruct((T, ATT_W), BF16)] * 3,
        scratch_shapes=[pltpu.VMEM((2, S, HEAD_DIM), BF16)] * 2 + [pltpu.VMEM((2, S, HEAD_DIM), F32)] * 2,
        compiler_params=_cp(("parallel", "parallel")),
    )(proj, proj, proj, o, lse, dy, bias)


def _conv_fwd(proj, cw, cb, B, S, name, after=None):
    T = B * S
    nc = SSD_CONV_DIM // LANE
    c0 = XBC0 // LANE
    extra, extra_specs = _after(after)

    def body(x_ref, w_ref, b_ref, *rest):
        o_ref = rest[-1]
        x = x_ref[...]
        t = lax.broadcasted_iota(jnp.int32, (S, 1), 0)
        acc = b_ref[...] + w_ref[SSD_CONV - 1:SSD_CONV, :] * x
        for k in range(SSD_CONV - 1):
            sh = SSD_CONV - 1 - k
            xs = jnp.where(t >= sh, pltpu.roll(x, sh, 0), 0.0)
            acc = acc + w_ref[k:k + 1, :] * xs
        o_ref[...] = acc

    return pl.pallas_call(
        body, name=name, grid=(B, nc),
        in_specs=[pl.BlockSpec((S, LANE), lambda b, j: (b, c0 + j)),
                  pl.BlockSpec((SUBLANE, LANE), lambda b, j: (0, j)),
                  pl.BlockSpec((1, LANE), lambda b, j: (0, j))] + extra_specs,
        out_specs=pl.BlockSpec((S, LANE), lambda b, j: (b, j)),
        out_shape=jax.ShapeDtypeStruct((T, SSD_CONV_DIM), F32),
        compiler_params=_cp(("parallel", "parallel")),
    )(proj, cw, cb, *extra)


def _conv_bwd(dpre, proj, cw, B, S, name):
    T = B * S
    nc = SSD_CONV_DIM // LANE
    c0 = XBC0 // LANE

    def body(d_ref, x_ref, w_ref, dx_ref, dwb_ref):
        @pl.when(pl.program_id(1) == 0)
        def _():
            dwb_ref[...] = jnp.zeros_like(dwb_ref)

        d = d_ref[...]
        x = x_ref[...]
        t = lax.broadcasted_iota(jnp.int32, (S, 1), 0)
        dx = w_ref[SSD_CONV - 1:SSD_CONV, :] * d
        rows = [None] * SUBLANE
        rows[SSD_CONV - 1] = jnp.sum(d * x, axis=0, keepdims=True)
        for k in range(SSD_CONV - 1):
            sh = SSD_CONV - 1 - k
            dx = dx + w_ref[k:k + 1, :] * jnp.where(t < S - sh, pltpu.roll(d, S - sh, 0), 0.0)
            xs = jnp.where(t >= sh, pltpu.roll(x, sh, 0), 0.0)
            rows[k] = jnp.sum(d * xs, axis=0, keepdims=True)
        rows[SSD_CONV] = jnp.sum(d, axis=0, keepdims=True)
        dx_ref[...] = dx.astype(BF16)
        r = lax.broadcasted_iota(jnp.int32, (SUBLANE, LANE), 0)
        upd = jnp.zeros((SUBLANE, LANE), F32)
        for k in range(SSD_CONV + 1):
            upd = upd + jnp.where(r == k, rows[k], 0.0)
        dwb_ref[...] += upd

    return pl.pallas_call(
        body, name=name, grid=(nc, B),
        in_specs=[pl.BlockSpec((S, LANE), lambda j, b: (b, j)),
                  pl.BlockSpec((S, LANE), lambda j, b: (b, c0 + j)),
                  pl.BlockSpec((SUBLANE, LANE), lambda j, b: (0, j))],
        out_specs=[pl.BlockSpec((S, LANE), lambda j, b: (b, j)),
                   pl.BlockSpec((SUBLANE, LANE), lambda j, b: (0, j))],
        out_shape=[jax.ShapeDtypeStruct((T, SSD_CONV_DIM), BF16),
                   jax.ShapeDtypeStruct((SUBLANE, SSD_CONV_DIM), F32)],
        compiler_params=_cp(("parallel", "arbitrary")),
    )(dpre, proj, cw)


def _ssd_consts():
    e = np.zeros((LANE, SSD_W), np.float32)
    p = np.zeros((SUBLANE, SSD_W), np.float32)
    for h in range(SSD_HEADS):
        e[h, HEAD_DIM * h:HEAD_DIM * (h + 1)] = 1.0
        p[h, HEAD_DIM * h] = 1.0
    return jnp.asarray(e), jnp.asarray(p)


def _ssd_chunk(pre, z, dtr, sprev, par, e_mat, psel):
    L = CHUNK
    xc = _silu(pre)
    xs, bm, cm = xc[:, :SSD_W], xc[:, SSD_W:SSD_W + 2 * SSD_STATE], xc[:, SSD_W + 2 * SSD_STATE:]
    dtb, alog, dskip, ng = par[0:1], par[1:2], par[2:3], par[3:4]
    dt = _softplus(_dot(dtr, e_mat, HI) + dtb)
    a = dt * (-jnp.exp(alog))
    X = xs * dt
    ri = lax.broadcasted_iota(jnp.int32, (L, L), 0)
    ci = lax.broadcasted_iota(jnp.int32, (L, L), 1)
    tril = ri >= ci
    acs = _dot(tril.astype(F32), a, HI)
    acs_t = _dot_nt(psel, acs, HI)
    ecs = jnp.exp(acs)
    alast = acs[L - 1:L, :]
    xd = (X * jnp.exp(alast - acs)).astype(BF16)
    xb = X.astype(BF16)
    col = lax.broadcasted_iota(jnp.int32, (1, SSD_W), 1)
    sb = sprev.astype(BF16)
    bgs = [bm[:, SSD_STATE * g:SSD_STATE * (g + 1)].astype(BF16) for g in range(2)]
    cgs = [cm[:, SSD_STATE * g:SSD_STATE * (g + 1)].astype(BF16) for g in range(2)]
    cbs = [_dot_nt(cgs[g], bgs[g]) for g in range(2)]
    first = lax.broadcasted_iota(jnp.int32, (1, LANE), 1) < HEAD_DIM
    y_tiles, s_tiles = [], []
    for t in range(SSD_W // LANE):
        cl = slice(LANE * t, LANE * (t + 1))
        xb_t, xd_t, sb_t = xb[:, cl], xd[:, cl], sb[:, cl]
        per_head = []
        for h in (2 * t, 2 * t + 1):
            seg = acs[:, HEAD_DIM * h:HEAD_DIM * h + 1] - acs_t[h:h + 1, :]
            dec = jnp.exp(jnp.where(tril, seg, NEG))
            per_head.append(_dot((cbs[h // 3] * dec).astype(BF16), xb_t))
        y_t = jnp.where(first, per_head[0], per_head[1])
        ga, gb = (2 * t) // 3, (2 * t + 1) // 3
        if ga == gb:
            y_off, s_add = _dot(cgs[ga], sb_t), _dot_tn(bgs[ga], xd_t)
        else:
            y_off = jnp.where(first, _dot(cgs[ga], sb_t), _dot(cgs[gb], sb_t))
            s_add = jnp.where(first, _dot_tn(bgs[ga], xd_t), _dot_tn(bgs[gb], xd_t))
        y_tiles.append(y_t + y_off * ecs[:, cl])
        s_tiles.append(s_add)
    y = dskip * xs + jnp.concatenate(y_tiles, axis=1)
    snew = sprev * jnp.exp(alast) + jnp.concatenate(s_tiles, axis=1)
    yg = y * _silu(z)
    sq = yg * yg
    g0 = col < SSD_W // 2
    ms0 = jnp.sum(jnp.where(g0, sq, 0.0), axis=-1, keepdims=True) * (2.0 / SSD_W)
    ms1 = jnp.sum(jnp.where(g0, 0.0, sq), axis=-1, keepdims=True) * (2.0 / SSD_W)
    r = jnp.where(g0, lax.rsqrt(ms0 + RMS_EPS), lax.rsqrt(ms1 + RMS_EPS))
    return yg * r * ng, snew


SSD_CHUNKS_PER_STEP = 2


def _ssd_chunks_per_step(S):
    k = SSD_CHUNKS_PER_STEP
    while (S // CHUNK) % k:
        k //= 2
    return k


def _ssd_fwd(pre, proj, par, B, S, name):
    T = B * S
    k = _ssd_chunks_per_step(S)
    nc, rows = S // (CHUNK * k), CHUNK * k
    e_mat, psel = _ssd_consts()

    def body(pre_ref, z_ref, dt_ref, par_ref, e_ref, p_ref, y_ref, sall_ref, st):
        @pl.when(pl.program_id(1) == 0)
        def _():
            st[...] = jnp.zeros_like(st)

        sprev = st[...]
        for i in range(k):
            r = slice(CHUNK * i, CHUNK * (i + 1))
            sall_ref[i] = sprev
            y, sprev = _ssd_chunk(pre_ref[r, :], z_ref[r, :], dt_ref[r, :], sprev, par_ref[...], e_ref[...],
                                  p_ref[...])
            y_ref[r, :] = y.astype(BF16)
        st[...] = sprev

    row = lambda b, c: b * nc + c
    full = lambda shp: pl.BlockSpec(shp, lambda b, c: (0, 0))
    return pl.pallas_call(
        body, name=name, grid=(B, nc),
        in_specs=[pl.BlockSpec((rows, SSD_CONV_DIM), lambda b, c: (row(b, c), 0)),
                  pl.BlockSpec((rows, SSD_W), lambda b, c: (row(b, c), Z0 // SSD_W)),
                  pl.BlockSpec((rows, LANE), lambda b, c: (row(b, c), DT0 // LANE)),
                  full((SUBLANE, SSD_W)), full((LANE, SSD_W)), full((SUBLANE, SSD_W))],
        out_specs=[pl.BlockSpec((rows, SSD_W), lambda b, c: (row(b, c), 0)),
                   pl.BlockSpec((k, SSD_STATE, SSD_W), lambda b, c: (row(b, c), 0, 0))],
        out_shape=[jax.ShapeDtypeStruct((T, SSD_W), BF16),
                   jax.ShapeDtypeStruct((B * nc * k, SSD_STATE, SSD_W), F32)],
        scratch_shapes=[pltpu.VMEM((SSD_STATE, SSD_W), F32)],
        compiler_params=_cp(("parallel", "arbitrary")),
    )(pre, proj, proj, par, e_mat, psel)


def _ssd_bwd(pre, proj, sall, dy, par, B, S, name):
    T = B * S
    k = _ssd_chunks_per_step(S)
    nc, rows = S // (CHUNK * k), CHUNK * k
    e_mat, psel = _ssd_consts()

    def body(pre_ref, z_ref, dt_ref, sall_ref, dy_ref, par_ref, e_ref, p_ref,
             dpre_ref, dz_ref, ddt_ref, dpar_ref, ds):
        b, c = pl.program_id(0), pl.program_id(1)

        @pl.when(c == 0)
        def _():
            ds[...] = jnp.zeros_like(ds)

        @pl.when((b == 0) & (c == 0))
        def _():
            dpar_ref[...] = jnp.zeros_like(dpar_ref)

        e_v, p_v = e_ref[...], p_ref[...]
        fn = lambda pre, z, dtr, sprev, par: _ssd_chunk(pre, z, dtr, sprev, par, e_v, p_v)
        dstate, dpar_sum = ds[...], None
        for i in reversed(range(k)):
            r = slice(CHUNK * i, CHUNK * (i + 1))
            _, vjp = jax.vjp(fn, pre_ref[r, :], z_ref[r, :], dt_ref[r, :], sall_ref[i], par_ref[...])
            dpre, dz, ddt, dstate, dpar = vjp((dy_ref[r, :], dstate))
            dpre_ref[r, :] = dpre
            dz_ref[r, :] = dz.astype(BF16)
            ddt_ref[r, :] = ddt.astype(BF16)
            dpar_sum = dpar if dpar_sum is None else dpar_sum + dpar
        dpar_ref[...] += dpar_sum
        ds[...] = dstate

    row = lambda b, c: b * nc + (nc - 1 - c)
    full = lambda shp: pl.BlockSpec(shp, lambda b, c: (0, 0))
    return pl.pallas_call(
        body, name=name, grid=(B, nc),
        in_specs=[pl.BlockSpec((rows, SSD_CONV_DIM), lambda b, c: (row(b, c), 0)),
                  pl.BlockSpec((rows, SSD_W), lambda b, c: (row(b, c), Z0 // SSD_W)),
                  pl.BlockSpec((rows, LANE), lambda b, c: (row(b, c), DT0 // LANE)),
                  pl.BlockSpec((k, SSD_STATE, SSD_W), lambda b, c: (row(b, c), 0, 0)),
                  pl.BlockSpec((rows, SSD_W), lambda b, c: (row(b, c), ATT_W // SSD_W)),
                  full((SUBLANE, SSD_W)), full((LANE, SSD_W)), full((SUBLANE, SSD_W))],
        out_specs=[pl.BlockSpec((rows, SSD_CONV_DIM), lambda b, c: (row(b, c), 0)),
                   pl.BlockSpec((rows, SSD_W), lambda b, c: (row(b, c), 0)),
                   pl.BlockSpec((rows, LANE), lambda b, c: (row(b, c), 0)),
                   full((SUBLANE, SSD_W))],
        out_shape=[jax.ShapeDtypeStruct((T, SSD_CONV_DIM), F32),
                   jax.ShapeDtypeStruct((T, SSD_W), BF16),
                   jax.ShapeDtypeStruct((T, LANE), BF16),
                   jax.ShapeDtypeStruct((SUBLANE, SSD_W), F32)],
        scratch_shapes=[pltpu.VMEM((SSD_STATE, SSD_W), F32)],
        compiler_params=_cp(("arbitrary", "arbitrary")),
    )(pre, proj, proj, sall, dy, par, e_mat, psel)


def _sgu_consts():
    e = np.zeros((SUBLANE, SGU_W), np.float32)
    for g in range(SGU_GROUPS):
        e[g, HEAD_DIM * g:HEAD_DIM * (g + 1)] = 1.0
    return jnp.asarray(e)


def _sgu_chunk(u_raw, v_raw, ln, w, bst, e4):
    L = CHUNK
    u = _gelu(u_raw)
    v = _gelu(v_raw)
    mu = jnp.mean(v, axis=-1, keepdims=True)
    vc = v - mu
    var = jnp.mean(vc * vc, axis=-1, keepdims=True)
    vn = vc * lax.rsqrt(var + LN_EPS) * ln[0:1] + ln[1:2]
    vb = vn.astype(BF16)
    ri = lax.broadcasted_iota(jnp.int32, (L, L), 0)
    ci = lax.broadcasted_iota(jnp.int32, (L, L), 1)
    tril = ri >= ci
    col = lax.broadcasted_iota(jnp.int32, (1, SGU_W), 1)
    mixed = _dot(bst, e4, HI)
    for g in range(SGU_GROUPS):
        wc = jnp.where(tril, w[g], 0.0).astype(BF16)
        gm = (col >= HEAD_DIM * g) & (col < HEAD_DIM * (g + 1))
        mixed = mixed + jnp.where(gm, _dot(wc, vb), 0.0)
    return u * mixed


def _sgu_fwd(proj, ln, w, bst, B, S, name, after=None):
    T = B * S
    e4 = _sgu_consts()
    extra, extra_specs = _after(after)

    def body(u_ref, v_ref, ln_ref, w_ref, b_ref, e_ref, *rest):
        y_ref = rest[-1]
        y_ref[...] = _sgu_chunk(u_ref[...], v_ref[...], ln_ref[...], w_ref[...], b_ref[...], e_ref[...]).astype(BF16)

    return pl.pallas_call(
        body, name=name, grid=(T // CHUNK,),
        in_specs=[pl.BlockSpec((CHUNK, SGU_W), lambda i: (i, U0 // SGU_W)),
                  pl.BlockSpec((CHUNK, SGU_W), lambda i: (i, VS0 // SGU_W)),
                  pl.BlockSpec((SUBLANE, SGU_W), lambda i: (0, 0)),
                  pl.BlockSpec((SGU_GROUPS, CHUNK, CHUNK), lambda i: (0, 0, 0)),
                  pl.BlockSpec((CHUNK, SUBLANE), lambda i: (0, 0)),
                  pl.BlockSpec((SUBLANE, SGU_W), lambda i: (0, 0))] + extra_specs,
        out_specs=pl.BlockSpec((CHUNK, SGU_W), lambda i: (i, 0)),
        out_shape=jax.ShapeDtypeStruct((T, SGU_W), BF16),
        compiler_params=_cp(("parallel",)),
    )(proj, proj, ln, w, bst, e4, *extra)


def _sgu_bwd(proj, dy, ln, w, bst, B, S, name):
    T = B * S
    e4 = _sgu_consts()
    ycol = (ATT_W + SSD_W) // SGU_W

    def body(u_ref, v_ref, dy_ref, ln_ref, w_ref, b_ref, e_ref, du_ref, dv_ref, dln_ref, dw_ref, db_ref):
        @pl.when(pl.program_id(0) == 0)
        def _():
            dln_ref[...] = jnp.zeros_like(dln_ref)
            dw_ref[...] = jnp.zeros_like(dw_ref)
            db_ref[...] = jnp.zeros_like(db_ref)

        e_v = e_ref[...]
        fn = lambda u, v, ln, w, b: _sgu_chunk(u, v, ln, w, b, e_v)
        _, vjp = jax.vjp(fn, u_ref[...], v_ref[...], ln_ref[...], w_ref[...], b_ref[...])
        du, dv, dln, dw, db = vjp(dy_ref[...])
        du_ref[...] = du.astype(BF16)
        dv_ref[...] = dv.astype(BF16)
        dln_ref[...] += dln
        dw_ref[...] += dw
        db_ref[...] += db

    c_ln = pl.BlockSpec((SUBLANE, SGU_W), lambda i: (0, 0))
    c_w = pl.BlockSpec((SGU_GROUPS, CHUNK, CHUNK), lambda i: (0, 0, 0))
    c_b = pl.BlockSpec((CHUNK, SUBLANE), lambda i: (0, 0))
    return pl.pallas_call(
        body, name=name, grid=(T // CHUNK,),
        in_specs=[pl.BlockSpec((CHUNK, SGU_W), lambda i: (i, U0 // SGU_W)),
                  pl.BlockSpec((CHUNK, SGU_W), lambda i: (i, VS0 // SGU_W)),
                  pl.BlockSpec((CHUNK, SGU_W), lambda i: (i, ycol)),
                  c_ln, c_w, c_b, pl.BlockSpec((SUBLANE, SGU_W), lambda i: (0, 0))],
        out_specs=[pl.BlockSpec((CHUNK, SGU_W), lambda i: (i, 0)),
                   pl.BlockSpec((CHUNK, SGU_W), lambda i: (i, 0)), c_ln, c_w, c_b],
        out_shape=[jax.ShapeDtypeStruct((T, SGU_W), BF16), jax.ShapeDtypeStruct((T, SGU_W), BF16),
                   jax.ShapeDtypeStruct((SUBLANE, SGU_W), F32),
                   jax.ShapeDtypeStruct((SGU_GROUPS, CHUNK, CHUNK), F32),
                   jax.ShapeDtypeStruct((CHUNK, SUBLANE), F32)],
        compiler_params=_cp(("arbitrary",)),
    )(proj, proj, dy, ln, w, bst, e4)


_HBM = pl.BlockSpec(memory_space=pltpu.HBM)
_SEM = pl.BlockSpec(memory_space=pltpu.SEMAPHORE)
_ANY = pl.BlockSpec(memory_space=pl.ANY)
_EFFECT = pltpu.SideEffectType.DATAFLOW_SIDE_EFFECTING


def _peers():
    x, y, c = lax.axis_index("x"), lax.axis_index("y"), lax.axis_index("c")
    out = []
    for p in range(1, N_DEV):
        px, py, pc = x ^ ((p >> 2) & 1), y ^ ((p >> 1) & 1), c ^ (p & 1)
        out.append(((px, py, pc), 4 * px + 2 * py + pc))
    return 4 * x + 2 * y + c, out


def _xchg_start(xs, a2a, order, name):
    n = len(xs)
    lands = [lax.empty(a.shape if f else (N_DEV,) + a.shape, a.dtype) for a, f in zip(xs, a2a)]

    def body(*refs):
        ins, zones = refs[:n], refs[n:2 * n]
        send_sems, recv_sems = refs[2 * n + 1], refs[2 * n + 2]
        token = refs[-1]
        me, peers = _peers()
        for p, (dev, peer) in enumerate(peers):
            for t in range(n):
                pltpu.make_async_remote_copy(
                    src_ref=ins[t].at[peer] if a2a[t] else ins[t], dst_ref=zones[t].at[me],
                    send_sem=send_sems.at[p * n + t], recv_sem=recv_sems.at[p * n + t],
                    device_id=dev, device_id_type=MESH).start()
        token[...] = jnp.zeros_like(token)

    hbm = lambda a: pltpu.HBM(a.shape, a.dtype)
    sems = pltpu.SemaphoreType.DMA(((N_DEV - 1) * n,))
    out = pl.pallas_call(
        body, name=name,
        in_specs=[_HBM] * (2 * n) + [_ANY],
        out_specs=[_SEM, _SEM] + [_HBM] * (2 * n) + [pl.BlockSpec(memory_space=pltpu.VMEM)],
        out_shape=[sems, sems] + [hbm(a) for a in xs] + [hbm(a) for a in lands]
        + [jax.ShapeDtypeStruct((SUBLANE, LANE), F32)],
        input_output_aliases={t: 2 + t for t in range(2 * n)},
        compiler_params=pltpu.CompilerParams(has_side_effects=_EFFECT),
    )(*[pltpu.with_memory_space_constraint(a, pltpu.HBM) for a in list(xs) + list(lands)], order)
    return out[0], out[1], out[2:2 + n], out[2 + n:2 + 2 * n], out[-1]


def _xchg_wait(started, a2a, after, name):
    send_sems, recv_sems, xs, lands, _ = started
    n = len(xs)

    def body(*refs):
        ins, zones = refs[:n], refs[n:2 * n]
        send_s, recv_s = refs[2 * n], refs[2 * n + 1]
        me, peers = _peers()
        cps = []
        for p, (dev, peer) in enumerate(peers):
            for t in range(n):
                cps.append(pltpu.make_async_remote_copy(
                    src_ref=ins[t].at[peer] if a2a[t] else ins[t], dst_ref=zones[t].at[peer],
                    send_sem=send_s.at[p * n + t], recv_sem=recv_s.at[p * n + t],
                    device_id=dev, device_id_type=MESH))
        for cp in cps:
            cp.wait_recv()
        for cp in cps:
            cp.wait_send()

    hbm = lambda a: pltpu.HBM(a.shape, a.dtype)
    out = pl.pallas_call(
        body, name=name,
        in_specs=[_HBM] * (2 * n) + [_SEM, _SEM, _ANY],
        out_specs=[_HBM] * (2 * n),
        out_shape=[hbm(a) for a in xs] + [hbm(a) for a in lands],
        input_output_aliases={t: t for t in range(2 * n)},
        compiler_params=pltpu.CompilerParams(has_side_effects=_EFFECT),
    )(*xs, *lands, send_sems, recv_sems, after)
    return out[:n], out[n:]


def _chip_peers():
    x, y, c = lax.axis_index("x"), lax.axis_index("y"), lax.axis_index("c")
    chips = [(1 - x, y), (x, 1 - y), (1 - x, 1 - y)]
    slot = lambda px, py, pc: 4 * px + 2 * py + pc
    return (x, y, c), chips, slot


def _gather_start(xs, order, name, own_slot):
    n = len(xs)
    lands = [lax.empty((N_DEV,) + a.shape, a.dtype) for a in xs]

    def body(*refs):
        ins, zones = refs[:n], refs[n:2 * n]
        send_sems, d2d_sems, ici_sems = refs[2 * n + 1:2 * n + 4]
        token = refs[-1]
        (x, y, c), chips, slot = _chip_peers()
        me = slot(x, y, c)
        for t in range(n):
            if own_slot:
                pltpu.make_async_copy(ins[t], zones[t].at[me], d2d_sems.at[n + t]).start()
            for j, (px, py) in enumerate(chips):
                pltpu.make_async_remote_copy(
                    src_ref=ins[t], dst_ref=zones[t].at[me], send_sem=send_sems.at[(1 + j) * n + t],
                    recv_sem=ici_sems.at[j * n + t], device_id=(px, py, c), device_id_type=MESH).start()
            pltpu.make_async_remote_copy(
                src_ref=ins[t], dst_ref=zones[t].at[me], send_sem=send_sems.at[t],
                recv_sem=d2d_sems.at[t], device_id=(x, y, 1 - c), device_id_type=MESH).start()
        token[...] = jnp.zeros_like(token)

    hbm = lambda a: pltpu.HBM(a.shape, a.dtype)
    dma = lambda k: pltpu.SemaphoreType.DMA((k,))
    out = pl.pallas_call(
        body, name=name,
        in_specs=[_HBM] * (2 * n) + [_ANY],
        out_specs=[_SEM, _SEM, _SEM] + [_HBM] * (2 * n) + [pl.BlockSpec(memory_space=pltpu.VMEM)],
        out_shape=[dma(4 * n), dma(2 * n), dma(3 * n)] + [hbm(a) for a in xs] + [hbm(a) for a in lands]
        + [jax.ShapeDtypeStruct((SUBLANE, LANE), F32)],
        input_output_aliases={t: 3 + t for t in range(2 * n)},
        compiler_params=pltpu.CompilerParams(has_side_effects=_EFFECT),
    )(*[pltpu.with_memory_space_constraint(a, pltpu.HBM) for a in list(xs) + list(lands)], order)
    return dict(send=out[0], d2d=out[1], ici=out[2], xs=out[3:3 + n], lands=out[3 + n:3 + 2 * n], token=out[-1],
                own_slot=own_slot)


def _gather_relay(st, after, name):
    n = len(st["xs"])

    def body(*refs):
        zones, ici_sems = refs[:n], refs[n]
        fsend, frecv = refs[n + 2], refs[n + 3]
        token = refs[-1]
        (x, y, c), chips, slot = _chip_peers()
        for t in range(n):
            for j, (px, py) in enumerate(chips):
                blk = zones[t].at[slot(px, py, c)]
                fwd = pltpu.make_async_remote_copy(
                    src_ref=blk, dst_ref=blk, send_sem=fsend.at[j * n + t], recv_sem=ici_sems.at[j * n + t],
                    device_id=(x, y, 1 - c), device_id_type=MESH)
                fwd.wait_recv()
                pltpu.make_async_remote_copy(
                    src_ref=blk, dst_ref=blk, send_sem=fsend.at[j * n + t], recv_sem=frecv.at[j * n + t],
                    device_id=(x, y, 1 - c), device_id_type=MESH).start()
        token[...] = jnp.zeros_like(token)

    hbm = lambda a: pltpu.HBM(a.shape, a.dtype)
    dma = lambda k: pltpu.SemaphoreType.DMA((k,))
    out = pl.pallas_call(
        body, name=name,
        in_specs=[_HBM] * n + [_SEM, _ANY],
        out_specs=[_SEM, _SEM] + [_HBM] * n + [pl.BlockSpec(memory_space=pltpu.VMEM)],
        out_shape=[dma(3 * n), dma(3 * n)] + [hbm(a) for a in st["lands"]]
        + [jax.ShapeDtypeStruct((SUBLANE, LANE), F32)],
        input_output_aliases={t: 2 + t for t in range(n)},
        compiler_params=pltpu.CompilerParams(has_side_effects=_EFFECT),
    )(*st["lands"], st["ici"], after)
    return dict(st, fsend=out[0], frecv=out[1], lands=out[2:2 + n], token=out[-1])


def _gather_wait(st, after, name):
    n = len(st["xs"])

    def body(*refs):
        ins, zones = refs[:n], refs[n:2 * n]
        send_sems, d2d_sems, fsend, frecv = refs[2 * n:2 * n + 4]
        (x, y, c), chips, slot = _chip_peers()
        sib = (x, y, 1 - c)
        for t in range(n):
            if st["own_slot"]:
                pltpu.make_async_copy(ins[t], zones[t].at[slot(x, y, c)], d2d_sems.at[n + t]).wait()
            mine = lambda s, r, dst: pltpu.make_async_remote_copy(
                src_ref=ins[t], dst_ref=dst, send_sem=s, recv_sem=r, device_id=sib, device_id_type=MESH)
            direct = mine(send_sems.at[t], d2d_sems.at[t], zones[t].at[slot(x, y, 1 - c)])
            direct.wait_recv()
            direct.wait_send()
            for j, (px, py) in enumerate(chips):
                mine(send_sems.at[(1 + j) * n + t], d2d_sems.at[t], zones[t].at[slot(px, py, c)]).wait_send()
                relayed = mine(fsend.at[j * n + t], frecv.at[j * n + t], zones[t].at[slot(px, py, 1 - c)])
                relayed.wait_recv()
                relayed.wait_send()

    hbm = lambda a: pltpu.HBM(a.shape, a.dtype)
    out = pl.pallas_call(
        body, name=name,
        in_specs=[_HBM] * (2 * n) + [_SEM] * 4 + [_ANY],
        out_specs=[_HBM] * (2 * n),
        out_shape=[hbm(a) for a in st["xs"]] + [hbm(a) for a in st["lands"]],
        input_output_aliases={t: t for t in range(2 * n)},
        compiler_params=pltpu.CompilerParams(has_side_effects=_EFFECT),
    )(*st["xs"], *st["lands"], st["send"], st["d2d"], st["fsend"], st["frecv"], after)
    return out[:n], out[n:]


def _cast_layers(pairs, name):
    def body(*refs):
        n = len(refs) // 2
        for i in range(n):
            refs[n + i][...] = refs[i][...].astype(BF16)

    in_specs = [pl.BlockSpec((None,) + w.shape[1:], functools.partial(lambda l, i: (l, 0, 0), l),
                             pipeline_mode=pl.Buffered(1)) for w, l in pairs]
    return pl.pallas_call(
        body, name=name, grid=(1,), in_specs=in_specs,
        out_specs=[pl.BlockSpec(w.shape[1:], lambda i: (0, 0)) for w, _ in pairs],
        out_shape=[jax.ShapeDtypeStruct(w.shape[1:], BF16) for w, _ in pairs],
        compiler_params=_cp(("arbitrary",)),
    )(*[w for w, _ in pairs])


ADAMW_BLOCK_ELEMS = 256 * 1024


def _adam_step(me, w, m, v, parts_ref, mine, out_refs):
    g = None
    for p in range(N_DEV):
        term = jnp.where(me == p, mine.astype(F32), parts_ref[p].astype(F32))
        g = term if g is None else g + term
    mn = ADAM_B1 * m + (1.0 - ADAM_B1) * g
    vn = ADAM_B2 * v + (1.0 - ADAM_B2) * (g * g)
    m_hat = mn / (1.0 - ADAM_B1 ** ADAM_STEP)
    v_hat = vn / (1.0 - ADAM_B2 ** ADAM_STEP)
    g_ref, d_ref, mo_ref, vo_ref = out_refs
    g_ref[...] = g
    d_ref[...] = -ADAM_LR * (m_hat / (jnp.sqrt(v_hat) + ADAM_EPS) + ADAM_WD * w)
    mo_ref[...] = mn
    vo_ref[...] = vn


def _adamw(me, w, m, v, parts, own, name, layer=0, into=None):
    L, R, C = w.shape
    P = parts.shape[0]
    tr = R
    t = 16
    while t <= R:
        if R % t == 0 and t * C <= ADAMW_BLOCK_ELEMS:
            tr = t
        t += 16
    if tr == R and R * C > ADAMW_BLOCK_ELEMS and R % 16 == 0:
        tr = 16
    own_all = own.shape[0] == P

    def body(me_ref, w_ref, m_ref, v_ref, p_ref, own_ref, *rest):
        _adam_step(me_ref[0], w_ref[...], m_ref[...], v_ref[...], p_ref, own_ref[...], rest[-4:])

    blk = pl.BlockSpec((None, tr, C), lambda i, me_ref: (layer, i, 0))
    own_blk = pl.BlockSpec((None, tr, C), lambda i, me_ref: (me_ref[0] if own_all else 0, i, 0))
    prev = list(into) if into is not None else []
    return pl.pallas_call(
        body, name=name,
        grid_spec=pltpu.PrefetchScalarGridSpec(
            num_scalar_prefetch=1, grid=(R // tr,),
            in_specs=[blk, blk, blk, pl.BlockSpec((P, tr, C), lambda i, me_ref: (0, i, 0)), own_blk]
            + [_ANY] * len(prev),
            out_specs=[blk] * 4),
        out_shape=[jax.ShapeDtypeStruct((L, R, C), F32)] * 4,
        input_output_aliases={6 + i: i for i in range(len(prev))},
        compiler_params=_cp(("parallel",)),
    )(me, w, m, v, parts, own, *prev)


def _perm_cols(w):
    pad = jnp.zeros((w.shape[0], LANE - SSD_HEADS), w.dtype)
    return jnp.concatenate([w[:, 0:1536], w[:, 2438:2694], w[:, 1536:2432], w[:, 2432:2438], pad,
                            w[:, 2694:2950]], axis=1)


def _unperm_cols(w):
    return jnp.concatenate([w[:, 0:1536], w[:, XBC0:XBC0 + SSD_CONV_DIM], w[:, DT0:DT0 + SSD_HEADS],
                            w[:, U0:U0 + SGU_W], w[:, VS0:VS0 + SGU_W]], axis=1)


_SMALL = ("ffn1_norm", "mix_norm", "conv_w", "conv_b", "dt_bias", "a_log", "d_skip", "ssd_norm",
          "sgu_ln_g", "sgu_ln_b", "sgu_w", "sgu_b", "ffn2_norm", "final_norm", "loss")


_SMALL_LAST = ("ffn1_norm",)
_SMALL_EARLY = tuple(k for k in _SMALL if k not in _SMALL_LAST)


def _pack(d, names):
    v = jnp.concatenate([d[k].astype(F32).reshape(-1) for k in names])
    n = v.shape[0]
    npad = -(-n // (LANE * 16)) * (LANE * 16)
    return jnp.pad(v, (0, npad - n)).reshape(npad // LANE, LANE)


def _unpack(p, shapes, names):
    v = p.reshape(-1)
    out, o = {}, 0
    for k in names:
        n = int(np.prod(shapes[k]))
        out[k] = v[o:o + n].reshape(shapes[k])
        o += n
    return out


def kernel(x, ffn1_norm, ffn1_w_gate, ffn1_w_up, ffn1_w_down, mix_norm, w_in, conv_w, conv_b, dt_bias, a_log, d_skip, ssd_norm, sgu_ln_g, sgu_ln_b, sgu_w, sgu_b, w_out, ffn2_norm, ffn2_w_gate, ffn2_w_up, ffn2_w_down, final_norm, loss_target, m_ffn1_norm, m_ffn1_w_gate, m_ffn1_w_up, m_ffn1_w_down, m_mix_norm, m_w_in, m_conv_w, m_conv_b, m_dt_bias, m_a_log, m_d_skip, m_ssd_norm, m_sgu_ln_g, m_sgu_ln_b, m_sgu_w, m_sgu_b, m_w_out, m_ffn2_norm, m_ffn2_w_gate, m_ffn2_w_up, m_ffn2_w_down, m_final_norm, v_ffn1_norm, v_ffn1_w_gate, v_ffn1_w_up, v_ffn1_w_down, v_mix_norm, v_w_in, v_conv_w, v_conv_b, v_dt_bias, v_a_log, v_d_skip, v_ssd_norm, v_sgu_ln_g, v_sgu_ln_b, v_sgu_w, v_sgu_b, v_w_out, v_ffn2_norm, v_ffn2_w_gate, v_ffn2_w_up, v_ffn2_w_down, v_final_norm):
    B, S, D = x.shape
    T = B * S
    L = ffn1_norm.shape[0]
    me = 4 * lax.axis_index("x") + 2 * lax.axis_index("y") + lax.axis_index("c")
    cs = conv_w.shape[2]
    W = dict(ffn1_norm=ffn1_norm, ffn1_w_gate=ffn1_w_gate, ffn1_w_up=ffn1_w_up, ffn1_w_down=ffn1_w_down,
             mix_norm=mix_norm, w_in=w_in, conv_w=conv_w, conv_b=conv_b, dt_bias=dt_bias, a_log=a_log,
             d_skip=d_skip, ssd_norm=ssd_norm, sgu_ln_g=sgu_ln_g, sgu_ln_b=sgu_ln_b, sgu_w=sgu_w, sgu_b=sgu_b,
             w_out=w_out, ffn2_norm=ffn2_norm, ffn2_w_gate=ffn2_w_gate, ffn2_w_up=ffn2_w_up,
             ffn2_w_down=ffn2_w_down, final_norm=final_norm)
    M = dict(ffn1_norm=m_ffn1_norm, ffn1_w_gate=m_ffn1_w_gate, ffn1_w_up=m_ffn1_w_up, ffn1_w_down=m_ffn1_w_down,
             mix_norm=m_mix_norm, w_in=m_w_in, conv_w=m_conv_w, conv_b=m_conv_b, dt_bias=m_dt_bias, a_log=m_a_log,
             d_skip=m_d_skip, ssd_norm=m_ssd_norm, sgu_ln_g=m_sgu_ln_g, sgu_ln_b=m_sgu_ln_b, sgu_w=m_sgu_w,
             sgu_b=m_sgu_b, w_out=m_w_out, ffn2_norm=m_ffn2_norm, ffn2_w_gate=m_ffn2_w_gate,
             ffn2_w_up=m_ffn2_w_up, ffn2_w_down=m_ffn2_w_down, final_norm=m_final_norm)
    V = dict(ffn1_norm=v_ffn1_norm, ffn1_w_gate=v_ffn1_w_gate, ffn1_w_up=v_ffn1_w_up, ffn1_w_down=v_ffn1_w_down,
             mix_norm=v_mix_norm, w_in=v_w_in, conv_w=v_conv_w, conv_b=v_conv_b, dt_bias=v_dt_bias, a_log=v_a_log,
             d_skip=v_d_skip, ssd_norm=v_ssd_norm, sgu_ln_g=v_sgu_ln_g, sgu_ln_b=v_sgu_ln_b, sgu_w=v_sgu_w,
             sgu_b=v_sgu_b, w_out=v_w_out, ffn2_norm=v_ffn2_norm, ffn2_w_gate=v_ffn2_w_gate,
             ffn2_w_up=v_ffn2_w_up, ffn2_w_down=v_ffn2_w_down, final_norm=v_final_norm)
    FFN1 = ("ffn1_w_gate", "ffn1_w_up", "ffn1_w_down")
    FFN2 = ("ffn2_w_gate", "ffn2_w_up", "ffn2_w_down")
    MIX = ("w_in", "w_out")
    big = FFN1 + MIX + FFN2
    col_sharded = lambda k: k.endswith("w_gate") or k.endswith("w_up")
    for dct in (W, M, V):
        for k in big:
            if col_sharded(k):
                dct[k] = jnp.swapaxes(dct[k], 1, 2)

    wgroups = [[(k, 0) for k in FFN1], [("w_in", 0), ("conv_w", None)], [("w_out", 0)] + [(k, 0) for k in FFN2]]
    for l in range(1, L):
        wgroups += [[(k, l) for k in FFN1] + [("w_in", l)], [("w_out", l)] + [(k, l) for k in FFN2]]
    wstarted, order = [], x
    later = [kl for grp in wgroups[1:] for kl in grp if kl[0] != "conv_w"]
    cast = dict(zip(wgroups[0], _cast_layers([(W[k], l) for k, l in wgroups[0]], "cast_first")))
    for gi, grp in enumerate(wgroups):
        if gi == 1:
            first = lax.optimization_barrier((W[later[0][0]], order))[0]
            srcs = [(first if i == 0 else W[k], l) for i, (k, l) in enumerate(later)]
            cast.update(zip(later, _cast_layers(srcs, "cast_rest")))
        xs = [conv_w if k == "conv_w" else cast[(k, l)] for k, l in grp]
        st = _gather_start(xs, order, f"gather_start_{gi}", own_slot=True)
        order = st["token"]
        wstarted.append(st)
    G = {}
    is_me = (jnp.arange(N_DEV) == me)

    zero1 = jnp.zeros((1,), F32)
    W["loss"], M["loss"], V["loss"] = zero1, zero1, zero1
    full_shapes = {k: (W[k].shape if k != "conv_w" else (L, SSD_CONV, SSD_CONV_DIM)) for k in _SMALL}
    embed = lambda a, k: a if k != "conv_w" else lax.dynamic_update_slice(
        jnp.zeros(full_shapes[k], F32), a, (0, 0, me * cs))
    small_packs = {names: [_pack({k: embed(d[k], k) for k in names}, names)[None] for d in (W, M, V)]
                   for names in (_SMALL_EARLY, _SMALL_LAST)}

    def relay(gi, after):
        wstarted[gi] = _gather_relay(wstarted[gi], after, f"gather_relay_{gi}")
        return wstarted[gi]["token"]

    def gathered(gi, after):
        own, lands = _gather_wait(wstarted[gi], after, f"gather_wait_{gi}")
        for key, o, z in zip(wgroups[gi], own, lands):
            G[key] = z if wstarted[gi]["own_slot"] else jnp.where(
                is_me.reshape((N_DEV,) + (1,) * o.ndim), o[None], z)

    def rows(k, l):
        a = G[(k, l)]
        return a.reshape(-1, a.shape[-1])

    bias = _attn_bias(S, min(256, S))
    row1 = lambda a: a.reshape(1, -1)

    def ffn1_params(l):
        return dict(g1=row1(ffn1_norm[l]), wg1=rows("ffn1_w_gate", l), wu1=rows("ffn1_w_up", l),
                    wd1=rows("ffn1_w_down", l))

    def out_params(l):
        return dict(wout=rows("w_out", l), g2=row1(ffn2_norm[l]), wg2=rows("ffn2_w_gate", l),
                    wu2=rows("ffn2_w_up", l), wd2=rows("ffn2_w_down", l))

    def mix_params(l):
        cw = jnp.transpose(G[("conv_w", None)][:, l], (1, 0, 2)).reshape(SSD_CONV, -1)
        return dict(
            gm=row1(mix_norm[l]), win=_perm_cols(rows("w_in", l)),
            cw=jnp.pad(cw, ((0, SUBLANE - SSD_CONV), (0, 0))), cb=row1(conv_b[l]),
            par=jnp.pad(jnp.stack([jnp.repeat(dt_bias[l], HEAD_DIM), jnp.repeat(a_log[l], HEAD_DIM),
                                   jnp.repeat(d_skip[l], HEAD_DIM), ssd_norm[l]]), ((0, SUBLANE - 4), (0, 0))),
            ln=jnp.pad(jnp.stack([sgu_ln_g[l], sgu_ln_b[l]]), ((0, SUBLANE - 2), (0, 0))),
            sw=sgu_w[l], bst=jnp.pad(sgu_b[l].T, ((0, 0), (0, SUBLANE - SGU_GROUPS))))

    xc = x.reshape(T, D)
    saved, lay = [], []
    for l in range(L):
        if l == 0:
            gathered(0, relay(0, order))
        else:
            gathered(1 + 2 * l, xc)
        p = ffn1_params(l)
        x1, *ffn1_saved = _ffn_fwd(xc, p["g1"], p["wg1"], p["wu1"], p["wd1"], f"ffn1_fwd_{l}")
        if l == 0:
            gathered(1, relay(1, x1))
        p.update(mix_params(l))
        lay.append(p)
        proj, ht = _norm_mm(x1, p["gm"], p["win"], f"in_proj_{l}")
        o_att, lse = _attn_fwd(proj, bias, B, S, f"attn_fwd_{l}")
        tok = relay(2 + 2 * l, o_att)
        pre = _conv_fwd(proj, p["cw"], p["cb"], B, S, f"conv_fwd_{l}", after=tok)
        y_ssd, sall = _ssd_fwd(pre, proj, p["par"], B, S, f"ssd_fwd_{l}")
        y_sgu = _sgu_fwd(proj, p["ln"], p["sw"], p["bst"], B, S, f"sgu_fwd_{l}", after=tok)
        ycat = jnp.concatenate([o_att.astype(BF16), y_ssd, y_sgu], axis=1)
        gathered(2 + 2 * l, ycat)
        p.update(out_params(l))
        x2 = _mm(ycat, p["wout"], "nn", f"out_proj_{l}", residual=x1)
        tok = relay(3 + 2 * l, x2) if l + 1 < L else None
        x3, *ffn2_saved = _ffn_fwd(x2, p["g2"], p["wg2"], p["wu2"], p["wd2"], f"ffn2_fwd_{l}", after=tok)
        saved.append(dict(x0=xc, ffn1=ffn1_saved, x1=x1, ht=ht, proj=proj, o_att=o_att, lse=lse, pre=pre,
                          sall=sall, ycat=ycat, x2=x2, ffn2=ffn2_saved))
        xc = x3
    loss_part, dx, dgf = _final_loss(xc, row1(final_norm), loss_target.reshape(T, D), "final_loss")

    gl = [dict() for _ in range(L)]
    gstarted, gorder = [], [order]

    def to_blocks(k, a):
        return a.reshape(N_DEV, -1, a.shape[-1]).astype(BF16)

    def send_grads(keys, l, extra, tag, small_names=None):
        xs = [to_blocks(k, gl[l][k]) for k in keys] + extra
        flags = [True] * len(keys) + [False] * len(extra)
        st = _xchg_start(xs, flags, gorder[0], f"grads_start_{tag}")
        gorder[0] = st[-1]
        gstarted.append((keys, l, st, flags, tag, small_names))

    def small_grads(names):
        sm = {}
        for k in names:
            if k == "final_norm":
                sm[k] = dgf.reshape(-1)
            elif k == "loss":
                sm[k] = loss_part[0, :1]
            else:
                sm[k] = jnp.stack([gl[l][k] for l in range(L)])
        return [_pack(sm, names)]

    def behind(a):
        return lax.optimization_barrier((a, gorder[0]))[0]

    for l in reversed(range(L)):
        p, s, g = lay[l], saved[l], gl[l]
        gfac, ufac, actt = s["ffn2"]
        dx2, dgt, dut, xn, dacc, g["ffn2_norm"] = _ffn_bwd_dx(
            dx, s["x2"], p["g2"], gfac, ufac, p["wg2"], p["wu2"], p["wd2"], f"ffn2_bwd_{l}")
        g["ffn2_w_gate"], g["ffn2_w_up"], g["ffn2_w_down"] = _ffn_dw(dgt, dut, actt, xn, dacc, f"ffn2_dw_{l}")
        if l == 0:
            send_grads(FFN2, 0, [], "l0f")
            dx2 = behind(dx2)
        dycat = _mm(dx2, p["wout"], "nt", f"out_proj_dx_{l}")
        g["w_out"] = _mm(s["ycat"], dx2, "tn", f"out_proj_dw_{l}", out_dtype=BF16, tm_cap=1024, tk_cap=512)
        dq, dk, dv = _attn_bwd(s["proj"], s["o_att"], s["lse"], dycat, bias, B, S, f"attn_bwd_{l}")
        dpre, dz, ddt, dpar = _ssd_bwd(s["pre"], s["proj"], s["sall"], dycat, p["par"], B, S, f"ssd_bwd_{l}")
        dxbc, dwb = _conv_bwd(dpre, s["proj"], p["cw"], B, S, f"conv_bwd_{l}")
        du, dvs, dln, dsw, dbst = _sgu_bwd(s["proj"], dycat, p["ln"], p["sw"], p["bst"], B, S, f"sgu_bwd_{l}")
        hsum = lambda r: r.reshape(SSD_HEADS, HEAD_DIM).sum(-1)
        g["conv_w"], g["conv_b"] = dwb[:SSD_CONV], dwb[SSD_CONV]
        g["dt_bias"], g["a_log"], g["d_skip"], g["ssd_norm"] = hsum(dpar[0]), hsum(dpar[1]), hsum(dpar[2]), dpar[3]
        g["sgu_ln_g"], g["sgu_ln_b"], g["sgu_w"], g["sgu_b"] = dln[0], dln[1], dsw, dbst[:, :SGU_GROUPS].T
        dproj = jnp.concatenate([dq, dk, dv, dz, du, dxbc, ddt, dvs], axis=1)
        g["w_in"] = _unperm_cols(_mm_resident_lhs(s["ht"], dproj, f"in_proj_dw_{l}"))
        dx1, g["mix_norm"] = _norm_mm_bwd(dproj, s["x1"], p["gm"], p["win"], dx2, f"in_proj_bwd_{l}")
        if l == 0:
            send_grads(MIX, 0, small_grads(_SMALL_EARLY), "l0a", _SMALL_EARLY)
            dx1, small_packs = lax.optimization_barrier((behind(dx1), small_packs))
        gfac, ufac, actt = s["ffn1"]
        dx, dgt, dut, xn, dacc, g["ffn1_norm"] = _ffn_bwd_dx(
            dx1, s["x0"], p["g1"], gfac, ufac, p["wg1"], p["wu1"], p["wd1"], f"ffn1_bwd_{l}")
        if l > 0:
            g["ffn1_w_gate"], g["ffn1_w_up"], g["ffn1_w_down"] = _ffn_dw(dgt, dut, actt, xn, dacc,
                                                                        f"ffn1_dw_{l}")
            send_grads(big, l, [], f"l{l}")
            dx = behind(dx)
        else:
            g["ffn1_w_gate"] = _dw_one(dgt, xn, "ffn1_dwg_0")
            send_grads(("ffn1_w_gate",), 0, [], "l0b1")
            g["ffn1_w_up"] = _dw_one(dut, xn, "ffn1_dwu_0", after=gorder[0])
            send_grads(("ffn1_w_up",), 0, [], "l0b2")
            g["ffn1_w_down"] = _dw_one(actt, dacc, "ffn1_dwd_0", after=gorder[0])
    grad_x = dx.reshape(B, S, D)
    send_grads(("ffn1_w_down",), 0, small_grads(_SMALL_LAST), "l0b", _SMALL_LAST)

    res, after = {}, gorder[0]
    small_out = [dict() for _ in range(4)]
    me1 = me.reshape(1).astype(jnp.int32)
    for keys, l, st, flags, tag, names in gstarted:
        own, lands = _xchg_wait(st, flags, after, f"grads_wait_{tag}")
        for k, mine, pk in zip(keys, own, lands):
            res[k] = _adamw(me1, W[k], M[k], V[k], pk, mine, f"adamw_{k}_{l}", layer=l, into=res.get(k))
        done = [res[k][0] for k in keys]
        if names:
            outs = _adamw(me1, *small_packs[names], lands[-1], own[-1][None], f"adamw_small_{tag}")
            for d, o in zip(small_out, outs):
                u = _unpack(o, full_shapes, names)
                if "conv_w" in u:
                    u["conv_w"] = lax.dynamic_slice(u["conv_w"], (0, 0, me * cs), (L, SSD_CONV, cs))
                d.update(u)
                done.extend(u.values())
        after = lax.optimization_barrier(tuple(done))[0]
    back = lambda k, a: jnp.swapaxes(a, 1, 2) if col_sharded(k) else a
    grads, deltas, new_m, new_v = [dict({k: back(k, res[k][i]) for k in big}, **small_out[i]) for i in range(4)]

    names = ("ffn1_norm", "ffn1_w_gate", "ffn1_w_up", "ffn1_w_down", "mix_norm", "w_in", "conv_w", "conv_b",
             "dt_bias", "a_log", "d_skip", "ssd_norm", "sgu_ln_g", "sgu_ln_b", "sgu_w", "sgu_b", "w_out",
             "ffn2_norm", "ffn2_w_gate", "ffn2_w_up", "ffn2_w_down", "final_norm")
    loss = grads["loss"][0]
    return (loss, grad_x, *[grads[n] for n in names], *[deltas[n] for n in names],
            *[new_m[n] for n in names], *[new_v[n] for n in names])
```

```python
import functools

import numpy as np
import jax
import jax.numpy as jnp
from jax import lax
from jax.experimental import pallas as pl
from jax.experimental.pallas import tpu as pltpu

F32, BF16 = jnp.float32, jnp.bfloat16
HI = lax.Precision.HIGH
MESH = pl.DeviceIdType.MESH
N_DEV = 8
VMEM_LIMIT_BYTES = 56 * 1024 * 1024
LANE, SUBLANE = 128, 8

HEAD_DIM = 64
ATT_W = 384
SSD_W = 384
SSD_HEADS = 6
SSD_STATE = 128
SSD_CONV = 4
CHUNK = 128
SSD_CONV_DIM = 896
SGU_W = 256
SGU_GROUPS = 4
D_IN = 2950
RMS_EPS = 1e-6
LN_EPS = 1e-5
NEG = -1e30

PW = 3072
Q0, K0, V0, Z0, U0, XBC0, DT0, VS0 = 0, 384, 768, 1152, 1536, 1792, 2688, 2816

ADAM_LR, ADAM_B1, ADAM_B2, ADAM_EPS, ADAM_WD, ADAM_STEP = 0.001, 0.9, 0.999, 1e-08, 0.01, 10


def _cp(sem=None):
    return pltpu.CompilerParams(dimension_semantics=sem, vmem_limit_bytes=VMEM_LIMIT_BYTES)


def _tile(n, cap, mult=LANE):
    best = None
    t = mult
    while t <= min(n, cap):
        if n % t == 0:
            best = t
        t += mult
    return best if best is not None else n


def _dot(a, b, prec=None):
    return jnp.dot(a, b, preferred_element_type=F32, precision=prec)


def _dot_nt(a, b, prec=None):
    return lax.dot_general(a, b, (((1,), (1,)), ((), ())), preferred_element_type=F32, precision=prec)


def _dot_tn(a, b, prec=None):
    return lax.dot_general(a, b, (((0,), (0,)), ((), ())), preferred_element_type=F32, precision=prec)


def _sigmoid(x):
    return 1.0 / (1.0 + jnp.exp(-x))


def _silu(x):
    return x * _sigmoid(x)


def _gelu(x):
    return 0.5 * x * (1.0 + lax.erf(x * 0.7071067811865476))


def _softplus(x):
    return jnp.maximum(x, 0.0) + jnp.log(1.0 + jnp.exp(-jnp.abs(x)))


def _rms_fwd(x, g):
    rstd = lax.rsqrt(jnp.mean(x * x, axis=-1, keepdims=True) + RMS_EPS)
    xhat = x * rstd
    return xhat * g, xhat, rstd


def _rms_bwd(dy, xhat, rstd, g):
    dxhat = dy * g
    dx = rstd * (dxhat - xhat * jnp.mean(dxhat * xhat, axis=-1, keepdims=True))
    return dx, dy * xhat


def _resident(shape):
    return pl.BlockSpec(shape, lambda *_: (0,) * len(shape), pipeline_mode=pl.Buffered(1))


def _mm(a, b, mode, name, out_dtype=F32, residual=None, tm_cap=512, tn_cap=1024, tk_cap=1024):
    if mode == "nn":
        (M, K), (_, N) = a.shape, b.shape
    elif mode == "nt":
        (M, K), (N, _) = a.shape, b.shape
    else:
        (K, M), (_, N) = a.shape, b.shape
    tm, tn, tk = _tile(M, tm_cap), _tile(N, tn_cap), _tile(K, tk_cap)
    nk = K // tk
    if mode == "tn":
        a_spec = pl.BlockSpec((tk, tm), lambda i, j, k: (k, i))
    else:
        a_spec = pl.BlockSpec((tm, tk), lambda i, j, k: (i, k))
    if mode == "nt":
        b_spec = pl.BlockSpec((tn, tk), lambda i, j, k: (j, k))
    else:
        b_spec = pl.BlockSpec((tk, tn), lambda i, j, k: (k, j))
    o_spec = pl.BlockSpec((tm, tn), lambda i, j, k: (i, j))
    has_res = residual is not None

    def prod(a_ref, b_ref):
        av = a_ref[...].astype(BF16)
        bv = b_ref[...].astype(BF16)
        if mode == "nn":
            return _dot(av, bv)
        if mode == "nt":
            return _dot_nt(av, bv)
        return _dot_tn(av, bv)

    def body(*refs):
        a_ref, b_ref = refs[:2]
        r_ref = refs[2] if has_res else None
        o_ref = refs[2 + has_res]
        if nk == 1:
            o = prod(a_ref, b_ref)
            if has_res:
                o = r_ref[...] + o
            o_ref[...] = o.astype(out_dtype)
            return
        acc = refs[3 + has_res]
        k = pl.program_id(2)

        @pl.when(k == 0)
        def _():
            acc[...] = jnp.zeros_like(acc)

        acc[...] += prod(a_ref, b_ref)

        @pl.when(k == nk - 1)
        def _():
            o = acc[...]
            if has_res:
                o = r_ref[...] + o
            o_ref[...] = o.astype(out_dtype)

    ins = [a, b] + ([residual] if has_res else [])
    in_specs = [a_spec, b_spec] + ([o_spec] if has_res else [])
    return pl.pallas_call(
        body, name=name, grid=(M // tm, N // tn, nk),
        in_specs=in_specs, out_specs=o_spec,
        out_shape=jax.ShapeDtypeStruct((M, N), out_dtype),
        scratch_shapes=[pltpu.VMEM((tm, tn), F32)] if nk > 1 else [],
        compiler_params=_cp(("parallel", "parallel", "arbitrary")),
    )(*ins)


def _after(after):
    return ([after], [_ANY]) if after is not None else ([], [])


def _ffn_fwd(x, g, wgt, wut, wd, name, after=None):
    T, D = x.shape
    F = wgt.shape[0]
    tm = _tile(T, 256)

    def body(x_ref, g_ref, wg_ref, wu_ref, wd_ref, *rest):
        out_ref, dgf_ref, sl_ref, actt_ref = rest[-4:]
        xv = x_ref[...]
        xn = _rms_fwd(xv, g_ref[...])[0].astype(BF16)
        gate = _dot_nt(xn, wg_ref[...])
        up = _dot_nt(xn, wu_ref[...])
        sig = _sigmoid(gate)
        sl = gate * sig
        dgf_ref[...] = (up * (sig * (1.0 + gate * (1.0 - sig)))).astype(BF16)
        sl_ref[...] = sl.astype(BF16)
        act = (sl * up).astype(BF16)
        actt_ref[...] = act.T
        out_ref[...] = xv + 0.5 * _dot(act, wd_ref[...])

    row = lambda w: pl.BlockSpec((tm, w), lambda i: (i, 0))
    extra, extra_specs = _after(after)
    return pl.pallas_call(
        body, name=name, grid=(T // tm,),
        in_specs=[row(D), _resident((1, D)), _resident((F, D)), _resident((F, D)), _resident((F, D))] + extra_specs,
        out_specs=[row(D), row(F), row(F), pl.BlockSpec((F, tm), lambda i: (0, i))],
        out_shape=[jax.ShapeDtypeStruct((T, D), F32),
                   jax.ShapeDtypeStruct((T, F), BF16),
                   jax.ShapeDtypeStruct((T, F), BF16),
                   jax.ShapeDtypeStruct((F, T), BF16)],
        compiler_params=_cp(("parallel",)),
    )(x, g, wgt, wut, wd, *extra)


def _ffn_bwd_dx(dout, x, g, dgf, sl, wg, wu, wd, name):
    T, D = x.shape
    F = wg.shape[0]
    tm = _tile(T, 256)

    def body(dout_ref, x_ref, g_ref, dgf_ref, sl_ref, wg_ref, wu_ref, wd_ref,
             dx_ref, dgt_ref, dut_ref, xn_ref, dacc_ref, dg_ref):
        @pl.when(pl.program_id(0) == 0)
        def _():
            dg_ref[...] = jnp.zeros_like(dg_ref)

        gv = g_ref[...]
        dout_v = dout_ref[...]
        xn, xhat, rstd = _rms_fwd(x_ref[...], gv)
        xn_ref[...] = xn.astype(BF16)
        dacc = (0.5 * dout_v).astype(BF16)
        dacc_ref[...] = dacc
        dact = _dot_nt(dacc, wd_ref[...])
        dgate = (dact * dgf_ref[...].astype(F32)).astype(BF16)
        dup = (dact * sl_ref[...].astype(F32)).astype(BF16)
        dgt_ref[...] = dgate.T
        dut_ref[...] = dup.T
        dxn = _dot(dgate, wg_ref[...]) + _dot(dup, wu_ref[...])
        dx, dgrow = _rms_bwd(dxn, xhat, rstd, gv)
        dx_ref[...] = dout_v + dx
        dg_ref[...] += jnp.sum(dgrow, axis=0, keepdims=True)

    row = lambda w: pl.BlockSpec((tm, w), lambda i: (i, 0))
    tr = pl.BlockSpec((F, tm), lambda i: (0, i))
    return pl.pallas_call(
        body, name=name, grid=(T // tm,),
        in_specs=[row(D), row(D), _resident((1, D)), row(F), row(F),
                  _resident((F, D)), _resident((F, D)), _resident((F, D))],
        out_specs=[row(D), tr, tr, row(D), row(D), pl.BlockSpec((1, D), lambda i: (0, 0))],
        out_shape=[jax.ShapeDtypeStruct((T, D), F32)] + [jax.ShapeDtypeStruct((F, T), BF16)] * 2
        + [jax.ShapeDtypeStruct((T, D), BF16)] * 2 + [jax.ShapeDtypeStruct((1, D), F32)],
        compiler_params=_cp(("arbitrary",)),
    )(dout, x, g, dgf, sl, wg, wu, wd)


def _ffn_dw(dgt, dut, actt, xn, dacc, name):
    F, T = dgt.shape
    D = xn.shape[1]
    th = _tile(F, 256)

    def body(dg_ref, du_ref, a_ref, xn_ref, dacc_ref, dwg_ref, dwu_ref, dwd_ref):
        xv = xn_ref[...]
        dwg_ref[...] = _dot(dg_ref[...], xv).astype(BF16)
        dwu_ref[...] = _dot(du_ref[...], xv).astype(BF16)
        dwd_ref[...] = _dot(a_ref[...], dacc_ref[...]).astype(BF16)

    tile = pl.BlockSpec((th, T), lambda j: (j, 0))
    out = pl.BlockSpec((th, D), lambda j: (j, 0))
    return pl.pallas_call(
        body, name=name, grid=(F // th,),
        in_specs=[tile, tile, tile, _resident((T, D)), _resident((T, D))],
        out_specs=[out, out, out], out_shape=[jax.ShapeDtypeStruct((F, D), BF16)] * 3,
        compiler_params=_cp(("parallel",)),
    )(dgt, dut, actt, xn, dacc)


def _dw_one(lt, r, name, after=None):
    F, T = lt.shape
    D = r.shape[1]
    th = _tile(F, 256)
    extra, extra_specs = _after(after)

    def body(l_ref, r_ref, *rest):
        rest[-1][...] = _dot(l_ref[...], r_ref[...]).astype(BF16)

    return pl.pallas_call(
        body, name=name, grid=(F // th,),
        in_specs=[pl.BlockSpec((th, T), lambda j: (j, 0)), _resident((T, D))] + extra_specs,
        out_specs=pl.BlockSpec((th, D), lambda j: (j, 0)),
        out_shape=jax.ShapeDtypeStruct((F, D), BF16),
        compiler_params=_cp(("parallel",)),
    )(lt, r, *extra)


def _norm_mm(x, g, w, name):
    T, D = x.shape
    N = w.shape[1]
    tm = _tile(T, 512)

    def body(x_ref, g_ref, w_ref, o_ref, ht_ref):
        xn = _rms_fwd(x_ref[...], g_ref[...])[0]
        ht_ref[...] = xn.T.astype(BF16)
        o_ref[...] = _dot(xn.astype(BF16), w_ref[...])

    return pl.pallas_call(
        body, name=name, grid=(T // tm,),
        in_specs=[pl.BlockSpec((tm, D), lambda i: (i, 0)), _resident((1, D)), _resident((D, N))],
        out_specs=[pl.BlockSpec((tm, N), lambda i: (i, 0)), pl.BlockSpec((D, tm), lambda i: (0, i))],
        out_shape=[jax.ShapeDtypeStruct((T, N), F32), jax.ShapeDtypeStruct((D, T), BF16)],
        compiler_params=_cp(("parallel",)),
    )(x, g, w)


def _norm_mm_bwd(dproj, x, g, w, dres, name):
    T, D = x.shape
    N = w.shape[1]
    tm = _tile(T, 512)

    def body(dp_ref, x_ref, g_ref, w_ref, dres_ref, dx_ref, dg_ref):
        @pl.when(pl.program_id(0) == 0)
        def _():
            dg_ref[...] = jnp.zeros_like(dg_ref)

        gv = g_ref[...]
        dh = _dot_nt(dp_ref[...], w_ref[...])
        _, xhat, rstd = _rms_fwd(x_ref[...], gv)
        dx, dgrow = _rms_bwd(dh, xhat, rstd, gv)
        dx_ref[...] = dres_ref[...] + dx
        dg_ref[...] += jnp.sum(dgrow, axis=0, keepdims=True)

    row = pl.BlockSpec((tm, D), lambda i: (i, 0))
    one = pl.BlockSpec((1, D), lambda i: (0, 0))
    return pl.pallas_call(
        body, name=name, grid=(T // tm,),
        in_specs=[pl.BlockSpec((tm, N), lambda i: (i, 0)), row, _resident((1, D)), _resident((D, N)), row],
        out_specs=[row, one],
        out_shape=[jax.ShapeDtypeStruct((T, D), F32), jax.ShapeDtypeStruct((1, D), F32)],
        compiler_params=_cp(("arbitrary",)),
    )(dproj, x, g, w, dres)


def _mm_resident_lhs(at, b, name, tn_cap=512):
    M, K = at.shape
    N = b.shape[1]
    tn = _tile(N, tn_cap)

    def body(a_ref, b_ref, o_ref):
        o_ref[...] = _dot(a_ref[...], b_ref[...]).astype(BF16)

    return pl.pallas_call(
        body, name=name, grid=(N // tn,),
        in_specs=[_resident((M, K)), pl.BlockSpec((K, tn), lambda j: (0, j))],
        out_specs=pl.BlockSpec((M, tn), lambda j: (0, j)),
        out_shape=jax.ShapeDtypeStruct((M, N), BF16),
        compiler_params=_cp(("parallel",)),
    )(at, b)


def _final_loss(x, g, target, name):
    T, D = x.shape
    tm = _tile(T, 512)

    def body(x_ref, g_ref, t_ref, loss_ref, dx_ref, dg_ref):
        @pl.when(pl.program_id(0) == 0)
        def _():
            dg_ref[...] = jnp.zeros_like(dg_ref)
            loss_ref[...] = jnp.zeros_like(loss_ref)

        gv = g_ref[...]
        y, xhat, rstd = _rms_fwd(x_ref[...], gv)
        err = y - t_ref[...]
        part = 0.5 * jnp.sum(jnp.mean(err * err, axis=-1, keepdims=True), axis=0, keepdims=True)
        loss_ref[...] += jnp.broadcast_to(part, loss_ref.shape)
        dy = err * (1.0 / D)
        dx, dgrow = _rms_bwd(dy, xhat, rstd, gv)
        dx_ref[...] = dx
        dg_ref[...] += jnp.sum(dgrow, axis=0, keepdims=True)

    row = pl.BlockSpec((tm, D), lambda i: (i, 0))
    one = pl.BlockSpec((1, D), lambda i: (0, 0))
    return pl.pallas_call(
        body, name=name, grid=(T // tm,),
        in_specs=[row, one, row],
        out_specs=[pl.BlockSpec((1, LANE), lambda i: (0, 0)), row, one],
        out_shape=[jax.ShapeDtypeStruct((1, LANE), F32), jax.ShapeDtypeStruct((T, D), F32),
                   jax.ShapeDtypeStruct((1, D), F32)],
        compiler_params=_cp(("arbitrary",)),
    )(x, g, target)


def _attn_bias(S, bq):
    d = np.arange(bq)[:, None] - np.arange(S)[None, :] + (S // bq - 1) * bq
    ok = d >= 0
    mult = ((ok & (d <= 128)).astype(np.float32) + (ok & (d % 4 == 0) & (d <= 512))
            + (ok & (d % 16 == 0) & (d <= 2048)))
    return jnp.asarray(np.where(mult > 0, np.log(np.maximum(mult, 1.0)), NEG).astype(np.float32))


def _attn_fwd(proj, bias, B, S, name):
    T = B * S
    bq = bias.shape[0]
    nb = S // bq
    qcol, kcol, vcol = Q0 // LANE, K0 // LANE, V0 // LANE

    def body(q_ref, k_ref, v_ref, t_ref, o_ref, lse_ref, ks, vs):
        for hh in range(2):
            sl = slice(HEAD_DIM * hh, HEAD_DIM * (hh + 1))
            ks[hh] = k_ref[:, sl].astype(BF16)
            vs[hh] = v_ref[:, sl].astype(BF16)
        for hh in range(2):
            sl = slice(HEAD_DIM * hh, HEAD_DIM * (hh + 1))
            for qb in range(nb):
                w, off, rows = bq * (qb + 1), (nb - 1 - qb) * bq, slice(qb * bq, (qb + 1) * bq)
                q = (q_ref[rows, sl] * 0.125).astype(BF16)
                s = _dot_nt(q, ks[hh, 0:w, :]) + t_ref[:, off:off + w]
                m = jnp.max(s, axis=-1, keepdims=True)
                p = jnp.exp(s - m)
                l = jnp.sum(p, axis=-1, keepdims=True)
                o_ref[rows, sl] = _dot(p.astype(BF16), vs[hh, 0:w, :]) / l
                lse_ref[rows, hh:hh + 1] = m + jnp.log(l)

    blk = lambda c0: pl.BlockSpec((S, LANE), lambda b, p: (b, c0 + p))
    return pl.pallas_call(
        body, name=name, grid=(B, ATT_W // LANE),
        in_specs=[blk(qcol), blk(kcol), blk(vcol), _resident((bq, S))],
        out_specs=[pl.BlockSpec((S, LANE), lambda b, p: (b, p)),
                   pl.BlockSpec((None, None, S, 2), lambda b, p: (b, p, 0, 0))],
        out_shape=[jax.ShapeDtypeStruct((T, ATT_W), F32),
                   jax.ShapeDtypeStruct((B, ATT_W // LANE, S, 2), F32)],
        scratch_shapes=[pltpu.VMEM((2, S, HEAD_DIM), BF16)] * 2,
        compiler_params=_cp(("parallel", "parallel")),
    )(proj, proj, proj, bias)


def _attn_bwd(proj, o, lse, dy, bias, B, S, name):
    T = B * S
    bq = bias.shape[0]
    nb = S // bq
    qcol, kcol, vcol = Q0 // LANE, K0 // LANE, V0 // LANE

    def body(q_ref, k_ref, v_ref, o_ref, lse_ref, do_ref, t_ref, dq_ref, dk_ref, dv_ref, ks, vs, dks, dvs):
        for hh in range(2):
            sl = slice(HEAD_DIM * hh, HEAD_DIM * (hh + 1))
            ks[hh] = k_ref[:, sl].astype(BF16)
            vs[hh] = v_ref[:, sl].astype(BF16)
        dks[...] = jnp.zeros_like(dks)
        dvs[...] = jnp.zeros_like(dvs)
        for hh in range(2):
            sl = slice(HEAD_DIM * hh, HEAD_DIM * (hh + 1))
            for qb in range(nb):
                w, off, rows = bq * (qb + 1), (nb - 1 - qb) * bq, slice(qb * bq, (qb + 1) * bq)
                q = (q_ref[rows, sl] * 0.125).astype(BF16)
                do = do_ref[rows, sl]
                dob = do.astype(BF16)
                delta = jnp.sum(do * o_ref[rows, sl], axis=-1, keepdims=True)
                k, v = ks[hh, 0:w, :], vs[hh, 0:w, :]
                s = _dot_nt(q, k) + t_ref[:, off:off + w]
                p = jnp.exp(s - lse_ref[rows, hh:hh + 1])
                ds = (p * (_dot_nt(dob, v) - delta)).astype(BF16)
                dq_ref[rows, sl] = (_dot(ds, k) * 0.125).astype(dq_ref.dtype)
                dks[hh, 0:w, :] += _dot_tn(ds, q)
                dvs[hh, 0:w, :] += _dot_tn(p.astype(BF16), dob)
            dk_ref[:, sl] = dks[hh].astype(dk_ref.dtype)
            dv_ref[:, sl] = dvs[hh].astype(dv_ref.dtype)

    blk = lambda c0: pl.BlockSpec((S, LANE), lambda b, p: (b, c0 + p))
    own = pl.BlockSpec((S, LANE), lambda b, p: (b, p))
    return pl.pallas_call(
        body, name=name, grid=(B, ATT_W // LANE),
        in_specs=[blk(qcol), blk(kcol), blk(vcol), own,
                  pl.BlockSpec((None, None, S, 2), lambda b, p: (b, p, 0, 0)), own, _resident((bq, S))],
        out_specs=[own, own, own],
        out_shape=[jax.ShapeDtypeStruct((T, ATT_W), BF16)] * 3,
        scratch_shapes=[pltpu.VMEM((2, S, HEAD_DIM), BF16)] * 2 + [pltpu.VMEM((2, S, HEAD_DIM), F32)] * 2,
        compiler_params=_cp(("parallel", "parallel")),
    )(proj, proj, proj, o, lse, dy, bias)


def _conv_fwd(proj, cw, cb, B, S, name, after=None):
    T = B * S
    nc = SSD_CONV_DIM // LANE
    c0 = XBC0 // LANE
    extra, extra_specs = _after(after)

    def body(x_ref, w_ref, b_ref, *rest):
        o_ref = rest[-1]
        x = x_ref[...]
        t = lax.broadcasted_iota(jnp.int32, (S, 1), 0)
        acc = b_ref[...] + w_ref[SSD_CONV - 1:SSD_CONV, :] * x
        for k in range(SSD_CONV - 1):
            sh = SSD_CONV - 1 - k
            xs = jnp.where(t >= sh, pltpu.roll(x, sh, 0), 0.0)
            acc = acc + w_ref[k:k + 1, :] * xs
        o_ref[...] = acc

    return pl.pallas_call(
        body, name=name, grid=(B, nc),
        in_specs=[pl.BlockSpec((S, LANE), lambda b, j: (b, c0 + j)),
                  pl.BlockSpec((SUBLANE, LANE), lambda b, j: (0, j)),
                  pl.BlockSpec((1, LANE), lambda b, j: (0, j))] + extra_specs,
        out_specs=pl.BlockSpec((S, LANE), lambda b, j: (b, j)),
        out_shape=jax.ShapeDtypeStruct((T, SSD_CONV_DIM), F32),
        compiler_params=_cp(("parallel", "parallel")),
    )(proj, cw, cb, *extra)


def _conv_bwd(dpre, proj, cw, B, S, name):
    T = B * S
    nc = SSD_CONV_DIM // LANE
    c0 = XBC0 // LANE

    def body(d_ref, x_ref, w_ref, dx_ref, dwb_ref):
        @pl.when(pl.program_id(1) == 0)
        def _():
            dwb_ref[...] = jnp.zeros_like(dwb_ref)

        d = d_ref[...]
        x = x_ref[...]
        t = lax.broadcasted_iota(jnp.int32, (S, 1), 0)
        dx = w_ref[SSD_CONV - 1:SSD_CONV, :] * d
        rows = [None] * SUBLANE
        rows[SSD_CONV - 1] = jnp.sum(d * x, axis=0, keepdims=True)
        for k in range(SSD_CONV - 1):
            sh = SSD_CONV - 1 - k
            dx = dx + w_ref[k:k + 1, :] * jnp.where(t < S - sh, pltpu.roll(d, S - sh, 0), 0.0)
            xs = jnp.where(t >= sh, pltpu.roll(x, sh, 0), 0.0)
            rows[k] = jnp.sum(d * xs, axis=0, keepdims=True)
        rows[SSD_CONV] = jnp.sum(d, axis=0, keepdims=True)
        dx_ref[...] = dx.astype(BF16)
        r = lax.broadcasted_iota(jnp.int32, (SUBLANE, LANE), 0)
        upd = jnp.zeros((SUBLANE, LANE), F32)
        for k in range(SSD_CONV + 1):
            upd = upd + jnp.where(r == k, rows[k], 0.0)
        dwb_ref[...] += upd

    return pl.pallas_call(
        body, name=name, grid=(nc, B),
        in_specs=[pl.BlockSpec((S, LANE), lambda j, b: (b, j)),
                  pl.BlockSpec((S, LANE), lambda j, b: (b, c0 + j)),
                  pl.BlockSpec((SUBLANE, LANE), lambda j, b: (0, j))],
        out_specs=[pl.BlockSpec((S, LANE), lambda j, b: (b, j)),
                   pl.BlockSpec((SUBLANE, LANE), lambda j, b: (0, j))],
        out_shape=[jax.ShapeDtypeStruct((T, SSD_CONV_DIM), BF16),
                   jax.ShapeDtypeStruct((SUBLANE, SSD_CONV_DIM), F32)],
        compiler_params=_cp(("parallel", "arbitrary")),
    )(dpre, proj, cw)


def _ssd_consts():
    e = np.zeros((LANE, SSD_W), np.float32)
    p = np.zeros((SUBLANE, SSD_W), np.float32)
    for h in range(SSD_HEADS):
        e[h, HEAD_DIM * h:HEAD_DIM * (h + 1)] = 1.0
        p[h, HEAD_DIM * h] = 1.0
    return jnp.asarray(e), jnp.asarray(p)


def _ssd_chunk(pre, z, dtr, sprev, par, e_mat, psel):
    L = CHUNK
    xc = _silu(pre)
    xs, bm, cm = xc[:, :SSD_W], xc[:, SSD_W:SSD_W + 2 * SSD_STATE], xc[:, SSD_W + 2 * SSD_STATE:]
    dtb, alog, dskip, ng = par[0:1], par[1:2], par[2:3], par[3:4]
    dt = _softplus(_dot(dtr, e_mat, HI) + dtb)
    a = dt * (-jnp.exp(alog))
    X = xs * dt
    ri = lax.broadcasted_iota(jnp.int32, (L, L), 0)
    ci = lax.broadcasted_iota(jnp.int32, (L, L), 1)
    tril = ri >= ci
    acs = _dot(tril.astype(F32), a, HI)
    acs_t = _dot_nt(psel, acs, HI)
    ecs = jnp.exp(acs)
    alast = acs[L - 1:L, :]
    xd = (X * jnp.exp(alast - acs)).astype(BF16)
    xb = X.astype(BF16)
    col = lax.broadcasted_iota(jnp.int32, (1, SSD_W), 1)
    sb = sprev.astype(BF16)
    bgs = [bm[:, SSD_STATE * g:SSD_STATE * (g + 1)].astype(BF16) for g in range(2)]
    cgs = [cm[:, SSD_STATE * g:SSD_STATE * (g + 1)].astype(BF16) for g in range(2)]
    cbs = [_dot_nt(cgs[g], bgs[g]) for g in range(2)]
    first = lax.broadcasted_iota(jnp.int32, (1, LANE), 1) < HEAD_DIM
    y_tiles, s_tiles = [], []
    for t in range(SSD_W // LANE):
        cl = slice(LANE * t, LANE * (t + 1))
        xb_t, xd_t, sb_t = xb[:, cl], xd[:, cl], sb[:, cl]
        per_head = []
        for h in (2 * t, 2 * t + 1):
            seg = acs[:, HEAD_DIM * h:HEAD_DIM * h + 1] - acs_t[h:h + 1, :]
            dec = jnp.exp(jnp.where(tril, seg, NEG))
            per_head.append(_dot((cbs[h // 3] * dec).astype(BF16), xb_t))
        y_t = jnp.where(first, per_head[0], per_head[1])
        ga, gb = (2 * t) // 3, (2 * t + 1) // 3
        if ga == gb:
            y_off, s_add = _dot(cgs[ga], sb_t), _dot_tn(bgs[ga], xd_t)
        else:
            y_off = jnp.where(first, _dot(cgs[ga], sb_t), _dot(cgs[gb], sb_t))
            s_add = jnp.where(first, _dot_tn(bgs[ga], xd_t), _dot_tn(bgs[gb], xd_t))
        y_tiles.append(y_t + y_off * ecs[:, cl])
        s_tiles.append(s_add)
    y = dskip * xs + jnp.concatenate(y_tiles, axis=1)
    snew = sprev * jnp.exp(alast) + jnp.concatenate(s_tiles, axis=1)
    yg = y * _silu(z)
    sq = yg * yg
    g0 = col < SSD_W // 2
    ms0 = jnp.sum(jnp.where(g0, sq, 0.0), axis=-1, keepdims=True) * (2.0 / SSD_W)
    ms1 = jnp.sum(jnp.where(g0, 0.0, sq), axis=-1, keepdims=True) * (2.0 / SSD_W)
    r = jnp.where(g0, lax.rsqrt(ms0 + RMS_EPS), lax.rsqrt(ms1 + RMS_EPS))
    return yg * r * ng, snew


SSD_CHUNKS_PER_STEP = 2


def _ssd_chunks_per_step(S):
    k = SSD_CHUNKS_PER_STEP
    while (S // CHUNK) % k:
        k //= 2
    return k


def _ssd_fwd(pre, proj, par, B, S, name):
    T = B * S
    k = _ssd_chunks_per_step(S)
    nc, rows = S // (CHUNK * k), CHUNK * k
    e_mat, psel = _ssd_consts()

    def body(pre_ref, z_ref, dt_ref, par_ref, e_ref, p_ref, y_ref, sall_ref, st):
        @pl.when(pl.program_id(1) == 0)
        def _():
            st[...] = jnp.zeros_like(st)

        sprev = st[...]
        for i in range(k):
            r = slice(CHUNK * i, CHUNK * (i + 1))
            sall_ref[i] = sprev
            y, sprev = _ssd_chunk(pre_ref[r, :], z_ref[r, :], dt_ref[r, :], sprev, par_ref[...], e_ref[...],
                                  p_ref[...])
            y_ref[r, :] = y.astype(BF16)
        st[...] = sprev

    row = lambda b, c: b * nc + c
    full = lambda shp: pl.BlockSpec(shp, lambda b, c: (0, 0))
    return pl.pallas_call(
        body, name=name, grid=(B, nc),
        in_specs=[pl.BlockSpec((rows, SSD_CONV_DIM), lambda b, c: (row(b, c), 0)),
                  pl.BlockSpec((rows, SSD_W), lambda b, c: (row(b, c), Z0 // SSD_W)),
                  pl.BlockSpec((rows, LANE), lambda b, c: (row(b, c), DT0 // LANE)),
                  full((SUBLANE, SSD_W)), full((LANE, SSD_W)), full((SUBLANE, SSD_W))],
        out_specs=[pl.BlockSpec((rows, SSD_W), lambda b, c: (row(b, c), 0)),
                   pl.BlockSpec((k, SSD_STATE, SSD_W), lambda b, c: (row(b, c), 0, 0))],
        out_shape=[jax.ShapeDtypeStruct((T, SSD_W), BF16),
                   jax.ShapeDtypeStruct((B * nc * k, SSD_STATE, SSD_W), F32)],
        scratch_shapes=[pltpu.VMEM((SSD_STATE, SSD_W), F32)],
        compiler_params=_cp(("parallel", "arbitrary")),
    )(pre, proj, proj, par, e_mat, psel)


def _ssd_bwd(pre, proj, sall, dy, par, B, S, name):
    T = B * S
    k = _ssd_chunks_per_step(S)
    nc, rows = S // (CHUNK * k), CHUNK * k
    e_mat, psel = _ssd_consts()

    def body(pre_ref, z_ref, dt_ref, sall_ref, dy_ref, par_ref, e_ref, p_ref,
             dpre_ref, dz_ref, ddt_ref, dpar_ref, ds):
        b, c = pl.program_id(0), pl.program_id(1)

        @pl.when(c == 0)
        def _():
            ds[...] = jnp.zeros_like(ds)

        @pl.when((b == 0) & (c == 0))
        def _():
            dpar_ref[...] = jnp.zeros_like(dpar_ref)

        e_v, p_v = e_ref[...], p_ref[...]
        fn = lambda pre, z, dtr, sprev, par: _ssd_chunk(pre, z, dtr, sprev, par, e_v, p_v)
        dstate, dpar_sum = ds[...], None
        for i in reversed(range(k)):
            r = slice(CHUNK * i, CHUNK * (i + 1))
            _, vjp = jax.vjp(fn, pre_ref[r, :], z_ref[r, :], dt_ref[r, :], sall_ref[i], par_ref[...])
            dpre, dz, ddt, dstate, dpar = vjp((dy_ref[r, :], dstate))
            dpre_ref[r, :] = dpre
            dz_ref[r, :] = dz.astype(BF16)
            ddt_ref[r, :] = ddt.astype(BF16)
            dpar_sum = dpar if dpar_sum is None else dpar_sum + dpar
        dpar_ref[...] += dpar_sum
        ds[...] = dstate

    row = lambda b, c: b * nc + (nc - 1 - c)
    full = lambda shp: pl.BlockSpec(shp, lambda b, c: (0, 0))
    return pl.pallas_call(
        body, name=name, grid=(B, nc),
        in_specs=[pl.BlockSpec((rows, SSD_CONV_DIM), lambda b, c: (row(b, c), 0)),
                  pl.BlockSpec((rows, SSD_W), lambda b, c: (row(b, c), Z0 // SSD_W)),
                  pl.BlockSpec((rows, LANE), lambda b, c: (row(b, c), DT0 // LANE)),
                  pl.BlockSpec((k, SSD_STATE, SSD_W), lambda b, c: (row(b, c), 0, 0)),
                  pl.BlockSpec((rows, SSD_W), lambda b, c: (row(b, c), ATT_W // SSD_W)),
                  full((SUBLANE, SSD_W)), full((LANE, SSD_W)), full((SUBLANE, SSD_W))],
        out_specs=[pl.BlockSpec((rows, SSD_CONV_DIM), lambda b, c: (row(b, c), 0)),
                   pl.BlockSpec((rows, SSD_W), lambda b, c: (row(b, c), 0)),
                   pl.BlockSpec((rows, LANE), lambda b, c: (row(b, c), 0)),
                   full((SUBLANE, SSD_W))],
        out_shape=[jax.ShapeDtypeStruct((T, SSD_CONV_DIM), F32),
                   jax.ShapeDtypeStruct((T, SSD_W), BF16),
                   jax.ShapeDtypeStruct((T, LANE), BF16),
                   jax.ShapeDtypeStruct((SUBLANE, SSD_W), F32)],
        scratch_shapes=[pltpu.VMEM((SSD_STATE, SSD_W), F32)],
        compiler_params=_cp(("arbitrary", "arbitrary")),
    )(pre, proj, proj, sall, dy, par, e_mat, psel)


def _sgu_consts():
    e = np.zeros((SUBLANE, SGU_W), np.float32)
    for g in range(SGU_GROUPS):
        e[g, HEAD_DIM * g:HEAD_DIM * (g + 1)] = 1.0
    return jnp.asarray(e)


def _sgu_chunk(u_raw, v_raw, ln, w, bst, e4):
    L = CHUNK
    u = _gelu(u_raw)
    v = _gelu(v_raw)
    mu = jnp.mean(v, axis=-1, keepdims=True)
    vc = v - mu
    var = jnp.mean(vc * vc, axis=-1, keepdims=True)
    vn = vc * lax.rsqrt(var + LN_EPS) * ln[0:1] + ln[1:2]
    vb = vn.astype(BF16)
    ri = lax.broadcasted_iota(jnp.int32, (L, L), 0)
    ci = lax.broadcasted_iota(jnp.int32, (L, L), 1)
    tril = ri >= ci
    col = lax.broadcasted_iota(jnp.int32, (1, SGU_W), 1)
    mixed = _dot(bst, e4, HI)
    for g in range(SGU_GROUPS):
        wc = jnp.where(tril, w[g], 0.0).astype(BF16)
        gm = (col >= HEAD_DIM * g) & (col < HEAD_DIM * (g + 1))
        mixed = mixed + jnp.where(gm, _dot(wc, vb), 0.0)
    return u * mixed


def _sgu_fwd(proj, ln, w, bst, B, S, name, after=None):
    T = B * S
    e4 = _sgu_consts()
    extra, extra_specs = _after(after)

    def body(u_ref, v_ref, ln_ref, w_ref, b_ref, e_ref, *rest):
        y_ref = rest[-1]
        y_ref[...] = _sgu_chunk(u_ref[...], v_ref[...], ln_ref[...], w_ref[...], b_ref[...], e_ref[...]).astype(BF16)

    return pl.pallas_call(
        body, name=name, grid=(T // CHUNK,),
        in_specs=[pl.BlockSpec((CHUNK, SGU_W), lambda i: (i, U0 // SGU_W)),
                  pl.BlockSpec((CHUNK, SGU_W), lambda i: (i, VS0 // SGU_W)),
                  pl.BlockSpec((SUBLANE, SGU_W), lambda i: (0, 0)),
                  pl.BlockSpec((SGU_GROUPS, CHUNK, CHUNK), lambda i: (0, 0, 0)),
                  pl.BlockSpec((CHUNK, SUBLANE), lambda i: (0, 0)),
                  pl.BlockSpec((SUBLANE, SGU_W), lambda i: (0, 0))] + extra_specs,
        out_specs=pl.BlockSpec((CHUNK, SGU_W), lambda i: (i, 0)),
        out_shape=jax.ShapeDtypeStruct((T, SGU_W), BF16),
        compiler_params=_cp(("parallel",)),
    )(proj, proj, ln, w, bst, e4, *extra)


def _sgu_bwd(proj, dy, ln, w, bst, B, S, name):
    T = B * S
    e4 = _sgu_consts()
    ycol = (ATT_W + SSD_W) // SGU_W

    def body(u_ref, v_ref, dy_ref, ln_ref, w_ref, b_ref, e_ref, du_ref, dv_ref, dln_ref, dw_ref, db_ref):
        @pl.when(pl.program_id(0) == 0)
        def _():
            dln_ref[...] = jnp.zeros_like(dln_ref)
            dw_ref[...] = jnp.zeros_like(dw_ref)
            db_ref[...] = jnp.zeros_like(db_ref)

        e_v = e_ref[...]
        fn = lambda u, v, ln, w, b: _sgu_chunk(u, v, ln, w, b, e_v)
        _, vjp = jax.vjp(fn, u_ref[...], v_ref[...], ln_ref[...], w_ref[...], b_ref[...])
        du, dv, dln, dw, db = vjp(dy_ref[...])
        du_ref[...] = du.astype(BF16)
        dv_ref[...] = dv.astype(BF16)
        dln_ref[...] += dln
        dw_ref[...] += dw
        db_ref[...] += db

    c_ln = pl.BlockSpec((SUBLANE, SGU_W), lambda i: (0, 0))
    c_w = pl.BlockSpec((SGU_GROUPS, CHUNK, CHUNK), lambda i: (0, 0, 0))
    c_b = pl.BlockSpec((CHUNK, SUBLANE), lambda i: (0, 0))
    return pl.pallas_call(
        body, name=name, grid=(T // CHUNK,),
        in_specs=[pl.BlockSpec((CHUNK, SGU_W), lambda i: (i, U0 // SGU_W)),
                  pl.BlockSpec((CHUNK, SGU_W), lambda i: (i, VS0 // SGU_W)),
                  pl.BlockSpec((CHUNK, SGU_W), lambda i: (i, ycol)),
                  c_ln, c_w, c_b, pl.BlockSpec((SUBLANE, SGU_W), lambda i: (0, 0))],
        out_specs=[pl.BlockSpec((CHUNK, SGU_W), lambda i: (i, 0)),
                   pl.BlockSpec((CHUNK, SGU_W), lambda i: (i, 0)), c_ln, c_w, c_b],
        out_shape=[jax.ShapeDtypeStruct((T, SGU_W), BF16), jax.ShapeDtypeStruct((T, SGU_W), BF16),
                   jax.ShapeDtypeStruct((SUBLANE, SGU_W), F32),
                   jax.ShapeDtypeStruct((SGU_GROUPS, CHUNK, CHUNK), F32),
                   jax.ShapeDtypeStruct((CHUNK, SUBLANE), F32)],
        compiler_params=_cp(("arbitrary",)),
    )(proj, proj, dy, ln, w, bst, e4)


_HBM = pl.BlockSpec(memory_space=pltpu.HBM)
_SEM = pl.BlockSpec(memory_space=pltpu.SEMAPHORE)
_ANY = pl.BlockSpec(memory_space=pl.ANY)
_EFFECT = pltpu.SideEffectType.DATAFLOW_SIDE_EFFECTING


def _peers():
    x, y, c = lax.axis_index("x"), lax.axis_index("y"), lax.axis_index("c")
    out = []
    for p in range(1, N_DEV):
        px, py, pc = x ^ ((p >> 2) & 1), y ^ ((p >> 1) & 1), c ^ (p & 1)
        out.append(((px, py, pc), 4 * px + 2 * py + pc))
    return 4 * x + 2 * y + c, out


def _xchg_start(xs, a2a, order, name):
    n = len(xs)
    lands = [lax.empty(a.shape if f else (N_DEV,) + a.shape, a.dtype) for a, f in zip(xs, a2a)]

    def body(*refs):
        ins, zones = refs[:n], refs[n:2 * n]
        send_sems, recv_sems = refs[2 * n + 1], refs[2 * n + 2]
        token = refs[-1]
        me, peers = _peers()
        for p, (dev, peer) in enumerate(peers):
            for t in range(n):
                pltpu.make_async_remote_copy(
                    src_ref=ins[t].at[peer] if a2a[t] else ins[t], dst_ref=zones[t].at[me],
                    send_sem=send_sems.at[p * n + t], recv_sem=recv_sems.at[p * n + t],
                    device_id=dev, device_id_type=MESH).start()
        token[...] = jnp.zeros_like(token)

    hbm = lambda a: pltpu.HBM(a.shape, a.dtype)
    sems = pltpu.SemaphoreType.DMA(((N_DEV - 1) * n,))
    out = pl.pallas_call(
        body, name=name,
        in_specs=[_HBM] * (2 * n) + [_ANY],
        out_specs=[_SEM, _SEM] + [_HBM] * (2 * n) + [pl.BlockSpec(memory_space=pltpu.VMEM)],
        out_shape=[sems, sems] + [hbm(a) for a in xs] + [hbm(a) for a in lands]
        + [jax.ShapeDtypeStruct((SUBLANE, LANE), F32)],
        input_output_aliases={t: 2 + t for t in range(2 * n)},
        compiler_params=pltpu.CompilerParams(has_side_effects=_EFFECT),
    )(*[pltpu.with_memory_space_constraint(a, pltpu.HBM) for a in list(xs) + list(lands)], order)
    return out[0], out[1], out[2:2 + n], out[2 + n:2 + 2 * n], out[-1]


def _xchg_wait(started, a2a, after, name):
    send_sems, recv_sems, xs, lands, _ = started
    n = len(xs)

    def body(*refs):
        ins, zones = refs[:n], refs[n:2 * n]
        send_s, recv_s = refs[2 * n], refs[2 * n + 1]
        me, peers = _peers()
        cps = []
        for p, (dev, peer) in enumerate(peers):
            for t in range(n):
                cps.append(pltpu.make_async_remote_copy(
                    src_ref=ins[t].at[peer] if a2a[t] else ins[t], dst_ref=zones[t].at[peer],
                    send_sem=send_s.at[p * n + t], recv_sem=recv_s.at[p * n + t],
                    device_id=dev, device_id_type=MESH))
        for cp in cps:
            cp.wait_recv()
        for cp in cps:
            cp.wait_send()

    hbm = lambda a: pltpu.HBM(a.shape, a.dtype)
    out = pl.pallas_call(
        body, name=name,
        in_specs=[_HBM] * (2 * n) + [_SEM, _SEM, _ANY],
        out_specs=[_HBM] * (2 * n),
        out_shape=[hbm(a) for a in xs] + [hbm(a) for a in lands],
        input_output_aliases={t: t for t in range(2 * n)},
        compiler_params=pltpu.CompilerParams(has_side_effects=_EFFECT),
    )(*xs, *lands, send_sems, recv_sems, after)
    return out[:n], out[n:]


def _chip_peers():
    x, y, c = lax.axis_index("x"), lax.axis_index("y"), lax.axis_index("c")
    chips = [(1 - x, y), (x, 1 - y), (1 - x, 1 - y)]
    slot = lambda px, py, pc: 4 * px + 2 * py + pc
    return (x, y, c), chips, slot


def _gather_start(xs, order, name, own_slot):
    n = len(xs)
    lands = [lax.empty((N_DEV,) + a.shape, a.dtype) for a in xs]

    def body(*refs):
        ins, zones = refs[:n], refs[n:2 * n]
        send_sems, d2d_sems, ici_sems = refs[2 * n + 1:2 * n + 4]
        token = refs[-1]
        (x, y, c), chips, slot = _chip_peers()
        me = slot(x, y, c)
        for t in range(n):
            if own_slot:
                pltpu.make_async_copy(ins[t], zones[t].at[me], d2d_sems.at[n + t]).start()
            for j, (px, py) in enumerate(chips):
                pltpu.make_async_remote_copy(
                    src_ref=ins[t], dst_ref=zones[t].at[me], send_sem=send_sems.at[(1 + j) * n + t],
                    recv_sem=ici_sems.at[j * n + t], device_id=(px, py, c), device_id_type=MESH).start()
            pltpu.make_async_remote_copy(
                src_ref=ins[t], dst_ref=zones[t].at[me], send_sem=send_sems.at[t],
                recv_sem=d2d_sems.at[t], device_id=(x, y, 1 - c), device_id_type=MESH).start()
        token[...] = jnp.zeros_like(token)

    hbm = lambda a: pltpu.HBM(a.shape, a.dtype)
    dma = lambda k: pltpu.SemaphoreType.DMA((k,))
    out = pl.pallas_call(
        body, name=name,
        in_specs=[_HBM] * (2 * n) + [_ANY],
        out_specs=[_SEM, _SEM, _SEM] + [_HBM] * (2 * n) + [pl.BlockSpec(memory_space=pltpu.VMEM)],
        out_shape=[dma(4 * n), dma(2 * n), dma(3 * n)] + [hbm(a) for a in xs] + [hbm(a) for a in lands]
        + [jax.ShapeDtypeStruct((SUBLANE, LANE), F32)],
        input_output_aliases={t: 3 + t for t in range(2 * n)},
        compiler_params=pltpu.CompilerParams(has_side_effects=_EFFECT),
    )(*[pltpu.with_memory_space_constraint(a, pltpu.HBM) for a in list(xs) + list(lands)], order)
    return dict(send=out[0], d2d=out[1], ici=out[2], xs=out[3:3 + n], lands=out[3 + n:3 + 2 * n], token=out[-1],
                own_slot=own_slot)


def _gather_relay(st, after, name):
    n = len(st["xs"])

    def body(*refs):
        zones, ici_sems = refs[:n], refs[n]
        fsend, frecv = refs[n + 2], refs[n + 3]
        token = refs[-1]
        (x, y, c), chips, slot = _chip_peers()
        for t in range(n):
            for j, (px, py) in enumerate(chips):
                blk = zones[t].at[slot(px, py, c)]
                fwd = pltpu.make_async_remote_copy(
                    src_ref=blk, dst_ref=blk, send_sem=fsend.at[j * n + t], recv_sem=ici_sems.at[j * n + t],
                    device_id=(x, y, 1 - c), device_id_type=MESH)
                fwd.wait_recv()
                pltpu.make_async_remote_copy(
                    src_ref=blk, dst_ref=blk, send_sem=fsend.at[j * n + t], recv_sem=frecv.at[j * n + t],
                    device_id=(x, y, 1 - c), device_id_type=MESH).start()
        token[...] = jnp.zeros_like(token)

    hbm = lambda a: pltpu.HBM(a.shape, a.dtype)
    dma = lambda k: pltpu.SemaphoreType.DMA((k,))
    out = pl.pallas_call(
        body, name=name,
        in_specs=[_HBM] * n + [_SEM, _ANY],
        out_specs=[_SEM, _SEM] + [_HBM] * n + [pl.BlockSpec(memory_space=pltpu.VMEM)],
        out_shape=[dma(3 * n), dma(3 * n)] + [hbm(a) for a in st["lands"]]
        + [jax.ShapeDtypeStruct((SUBLANE, LANE), F32)],
        input_output_aliases={t: 2 + t for t in range(n)},
        compiler_params=pltpu.CompilerParams(has_side_effects=_EFFECT),
    )(*st["lands"], st["ici"], after)
    return dict(st, fsend=out[0], frecv=out[1], lands=out[2:2 + n], token=out[-1])


def _gather_wait(st, after, name):
    n = len(st["xs"])

    def body(*refs):
        ins, zones = refs[:n], refs[n:2 * n]
        send_sems, d2d_sems, fsend, frecv = refs[2 * n:2 * n + 4]
        (x, y, c), chips, slot = _chip_peers()
        sib = (x, y, 1 - c)
        for t in range(n):
            if st["own_slot"]:
                pltpu.make_async_copy(ins[t], zones[t].at[slot(x, y, c)], d2d_sems.at[n + t]).wait()
            mine = lambda s, r, dst: pltpu.make_async_remote_copy(
                src_ref=ins[t], dst_ref=dst, send_sem=s, recv_sem=r, device_id=sib, device_id_type=MESH)
            direct = mine(send_sems.at[t], d2d_sems.at[t], zones[t].at[slot(x, y, 1 - c)])
            direct.wait_recv()
            direct.wait_send()
            for j, (px, py) in enumerate(chips):
                mine(send_sems.at[(1 + j) * n + t], d2d_sems.at[t], zones[t].at[slot(px, py, c)]).wait_send()
                relayed = mine(fsend.at[j * n + t], frecv.at[j * n + t], zones[t].at[slot(px, py, 1 - c)])
                relayed.wait_recv()
                relayed.wait_send()

    hbm = lambda a: pltpu.HBM(a.shape, a.dtype)
    out = pl.pallas_call(
        body, name=name,
        in_specs=[_HBM] * (2 * n) + [_SEM] * 4 + [_ANY],
        out_specs=[_HBM] * (2 * n),
        out_shape=[hbm(a) for a in st["xs"]] + [hbm(a) for a in st["lands"]],
        input_output_aliases={t: t for t in range(2 * n)},
        compiler_params=pltpu.CompilerParams(has_side_effects=_EFFECT),
    )(*st["xs"], *st["lands"], st["send"], st["d2d"], st["fsend"], st["frecv"], after)
    return out[:n], out[n:]


def _cast_layers(pairs, name):
    def body(*refs):
        n = len(refs) // 2
        for i in range(n):
            refs[n + i][...] = refs[i][...].astype(BF16)

    in_specs = [pl.BlockSpec((None,) + w.shape[1:], functools.partial(lambda l, i: (l, 0, 0), l),
                             pipeline_mode=pl.Buffered(1)) for w, l in pairs]
    return pl.pallas_call(
        body, name=name, grid=(1,), in_specs=in_specs,
        out_specs=[pl.BlockSpec(w.shape[1:], lambda i: (0, 0)) for w, _ in pairs],
        out_shape=[jax.ShapeDtypeStruct(w.shape[1:], BF16) for w, _ in pairs],
        compiler_params=_cp(("arbitrary",)),
    )(*[w for w, _ in pairs])


ADAMW_BLOCK_ELEMS = 256 * 1024


def _adam_step(me, w, m, v, parts_ref, mine, out_refs):
    g = None
    for p in range(N_DEV):
        term = jnp.where(me == p, mine.astype(F32), parts_ref[p].astype(F32))
        g = term if g is None else g + term
    mn = ADAM_B1 * m + (1.0 - ADAM_B1) * g
    vn = ADAM_B2 * v + (1.0 - ADAM_B2) * (g * g)
    m_hat = mn / (1.0 - ADAM_B1 ** ADAM_STEP)
    v_hat = vn / (1.0 - ADAM_B2 ** ADAM_STEP)
    g_ref, d_ref, mo_ref, vo_ref = out_refs
    g_ref[...] = g
    d_ref[...] = -ADAM_LR * (m_hat / (jnp.sqrt(v_hat) + ADAM_EPS) + ADAM_WD * w)
    mo_ref[...] = mn
    vo_ref[...] = vn


def _adamw(me, w, m, v, parts, own, name, layer=0, into=None):
    L, R, C = w.shape
    P = parts.shape[0]
    tr = R
    t = 16
    while t <= R:
        if R % t == 0 and t * C <= ADAMW_BLOCK_ELEMS:
            tr = t
        t += 16
    if tr == R and R * C > ADAMW_BLOCK_ELEMS and R % 16 == 0:
        tr = 16
    own_all = own.shape[0] == P

    def body(me_ref, w_ref, m_ref, v_ref, p_ref, own_ref, *rest):
        _adam_step(me_ref[0], w_ref[...], m_ref[...], v_ref[...], p_ref, own_ref[...], rest[-4:])

    blk = pl.BlockSpec((None, tr, C), lambda i, me_ref: (layer, i, 0))
    own_blk = pl.BlockSpec((None, tr, C), lambda i, me_ref: (me_ref[0] if own_all else 0, i, 0))
    prev = list(into) if into is not None else []
    return pl.pallas_call(
        body, name=name,
        grid_spec=pltpu.PrefetchScalarGridSpec(
            num_scalar_prefetch=1, grid=(R // tr,),
            in_specs=[blk, blk, blk, pl.BlockSpec((P, tr, C), lambda i, me_ref: (0, i, 0)), own_blk]
            + [_ANY] * len(prev),
            out_specs=[blk] * 4),
        out_shape=[jax.ShapeDtypeStruct((L, R, C), F32)] * 4,
        input_output_aliases={6 + i: i for i in range(len(prev))},
        compiler_params=_cp(("parallel",)),
    )(me, w, m, v, parts, own, *prev)


def _perm_cols(w):
    pad = jnp.zeros(w.shape[:-1] + (LANE - SSD_HEADS,), w.dtype)
    return jnp.concatenate([w[..., 0:1536], w[..., 2438:2694], w[..., 1536:2432], w[..., 2432:2438], pad,
                            w[..., 2694:2950]], axis=-1)


def _unperm_cols(w):
    return jnp.concatenate([w[..., 0:1536], w[..., XBC0:XBC0 + SSD_CONV_DIM], w[..., DT0:DT0 + SSD_HEADS],
                            w[..., U0:U0 + SGU_W], w[..., VS0:VS0 + SGU_W]], axis=-1)


_SMALL = ("ffn1_norm", "mix_norm", "conv_w", "conv_b", "dt_bias", "a_log", "d_skip", "ssd_norm",
          "sgu_ln_g", "sgu_ln_b", "sgu_w", "sgu_b", "ffn2_norm", "final_norm", "loss")


_SMALL_LAST = ("ffn1_norm",)
_SMALL_EARLY = tuple(k for k in _SMALL if k not in _SMALL_LAST)


def _pack(d, names):
    v = jnp.concatenate([d[k].astype(F32).reshape(-1) for k in names])
    n = v.shape[0]
    npad = -(-n // (LANE * 16)) * (LANE * 16)
    return jnp.pad(v, (0, npad - n)).reshape(npad // LANE, LANE)


def _unpack(p, shapes, names):
    v = p.reshape(-1)
    out, o = {}, 0
    for k in names:
        n = int(np.prod(shapes[k]))
        out[k] = v[o:o + n].reshape(shapes[k])
        o += n
    return out


def kernel(x, ffn1_norm, ffn1_w_gate, ffn1_w_up, ffn1_w_down, mix_norm, w_in, conv_w, conv_b, dt_bias, a_log, d_skip, ssd_norm, sgu_ln_g, sgu_ln_b, sgu_w, sgu_b, w_out, ffn2_norm, ffn2_w_gate, ffn2_w_up, ffn2_w_down, final_norm, loss_target, m_ffn1_norm, m_ffn1_w_gate, m_ffn1_w_up, m_ffn1_w_down, m_mix_norm, m_w_in, m_conv_w, m_conv_b, m_dt_bias, m_a_log, m_d_skip, m_ssd_norm, m_sgu_ln_g, m_sgu_ln_b, m_sgu_w, m_sgu_b, m_w_out, m_ffn2_norm, m_ffn2_w_gate, m_ffn2_w_up, m_ffn2_w_down, m_final_norm, v_ffn1_norm, v_ffn1_w_gate, v_ffn1_w_up, v_ffn1_w_down, v_mix_norm, v_w_in, v_conv_w, v_conv_b, v_dt_bias, v_a_log, v_d_skip, v_ssd_norm, v_sgu_ln_g, v_sgu_ln_b, v_sgu_w, v_sgu_b, v_w_out, v_ffn2_norm, v_ffn2_w_gate, v_ffn2_w_up, v_ffn2_w_down, v_final_norm):
    B, S, D = x.shape
    T = B * S
    L = ffn1_norm.shape[0]
    me = 4 * lax.axis_index("x") + 2 * lax.axis_index("y") + lax.axis_index("c")
    cs = conv_w.shape[2]
    W = dict(ffn1_norm=ffn1_norm, ffn1_w_gate=ffn1_w_gate, ffn1_w_up=ffn1_w_up, ffn1_w_down=ffn1_w_down,
             mix_norm=mix_norm, w_in=w_in, conv_w=conv_w, conv_b=conv_b, dt_bias=dt_bias, a_log=a_log,
             d_skip=d_skip, ssd_norm=ssd_norm, sgu_ln_g=sgu_ln_g, sgu_ln_b=sgu_ln_b, sgu_w=sgu_w, sgu_b=sgu_b,
             w_out=w_out, ffn2_norm=ffn2_norm, ffn2_w_gate=ffn2_w_gate, ffn2_w_up=ffn2_w_up,
             ffn2_w_down=ffn2_w_down, final_norm=final_norm)
    M = dict(ffn1_norm=m_ffn1_norm, ffn1_w_gate=m_ffn1_w_gate, ffn1_w_up=m_ffn1_w_up, ffn1_w_down=m_ffn1_w_down,
             mix_norm=m_mix_norm, w_in=m_w_in, conv_w=m_conv_w, conv_b=m_conv_b, dt_bias=m_dt_bias, a_log=m_a_log,
             d_skip=m_d_skip, ssd_norm=m_ssd_norm, sgu_ln_g=m_sgu_ln_g, sgu_ln_b=m_sgu_ln_b, sgu_w=m_sgu_w,
             sgu_b=m_sgu_b, w_out=m_w_out, ffn2_norm=m_ffn2_norm, ffn2_w_gate=m_ffn2_w_gate,
             ffn2_w_up=m_ffn2_w_up, ffn2_w_down=m_ffn2_w_down, final_norm=m_final_norm)
    V = dict(ffn1_norm=v_ffn1_norm, ffn1_w_gate=v_ffn1_w_gate, ffn1_w_up=v_ffn1_w_up, ffn1_w_down=v_ffn1_w_down,
             mix_norm=v_mix_norm, w_in=v_w_in, conv_w=v_conv_w, conv_b=v_conv_b, dt_bias=v_dt_bias, a_log=v_a_log,
             d_skip=v_d_skip, ssd_norm=v_ssd_norm, sgu_ln_g=v_sgu_ln_g, sgu_ln_b=v_sgu_ln_b, sgu_w=v_sgu_w,
             sgu_b=v_sgu_b, w_out=v_w_out, ffn2_norm=v_ffn2_norm, ffn2_w_gate=v_ffn2_w_gate,
             ffn2_w_up=v_ffn2_w_up, ffn2_w_down=v_ffn2_w_down, final_norm=v_final_norm)
    FFN1 = ("ffn1_w_gate", "ffn1_w_up", "ffn1_w_down")
    FFN2 = ("ffn2_w_gate", "ffn2_w_up", "ffn2_w_down")
    MIX = ("w_in", "w_out")
    big = FFN1 + MIX + FFN2
    col_sharded = lambda k: k.endswith("w_gate") or k.endswith("w_up")
    for dct in (W, M, V):
        for k in big:
            if col_sharded(k):
                dct[k] = jnp.swapaxes(dct[k], 1, 2)
        dct["w_in"] = _perm_cols(dct["w_in"])

    wgroups = [[(k, 0) for k in FFN1], [("w_in", 0), ("conv_w", None)], [("w_out", 0)] + [(k, 0) for k in FFN2]]
    for l in range(1, L):
        wgroups += [[(k, l) for k in FFN1] + [("w_in", l)], [("w_out", l)] + [(k, l) for k in FFN2]]
    wstarted, order = [], x
    later = [kl for grp in wgroups[1:] for kl in grp if kl[0] != "conv_w"]
    cast = dict(zip(wgroups[0], _cast_layers([(W[k], l) for k, l in wgroups[0]], "cast_first")))
    for gi, grp in enumerate(wgroups):
        if gi == 1:
            first = lax.optimization_barrier((W[later[0][0]], order))[0]
            srcs = [(first if i == 0 else W[k], l) for i, (k, l) in enumerate(later)]
            cast.update(zip(later, _cast_layers(srcs, "cast_rest")))
        xs = [conv_w if k == "conv_w" else cast[(k, l)] for k, l in grp]
        st = _gather_start(xs, order, f"gather_start_{gi}", own_slot=True)
        order = st["token"]
        wstarted.append(st)
    G = {}
    is_me = (jnp.arange(N_DEV) == me)

    zero1 = jnp.zeros((1,), F32)
    W["loss"], M["loss"], V["loss"] = zero1, zero1, zero1
    full_shapes = {k: (W[k].shape if k != "conv_w" else (L, SSD_CONV, SSD_CONV_DIM)) for k in _SMALL}
    embed = lambda a, k: a if k != "conv_w" else lax.dynamic_update_slice(
        jnp.zeros(full_shapes[k], F32), a, (0, 0, me * cs))
    small_packs = {names: [_pack({k: embed(d[k], k) for k in names}, names)[None] for d in (W, M, V)]
                   for names in (_SMALL_EARLY, _SMALL_LAST)}

    def relay(gi, after):
        wstarted[gi] = _gather_relay(wstarted[gi], after, f"gather_relay_{gi}")
        return wstarted[gi]["token"]

    def gathered(gi, after):
        own, lands = _gather_wait(wstarted[gi], after, f"gather_wait_{gi}")
        for key, o, z in zip(wgroups[gi], own, lands):
            G[key] = z if wstarted[gi]["own_slot"] else jnp.where(
                is_me.reshape((N_DEV,) + (1,) * o.ndim), o[None], z)

    def rows(k, l):
        a = G[(k, l)]
        return a.reshape(-1, a.shape[-1])

    bias = _attn_bias(S, min(256, S))
    row1 = lambda a: a.reshape(1, -1)

    def ffn1_params(l):
        return dict(g1=row1(ffn1_norm[l]), wg1=rows("ffn1_w_gate", l), wu1=rows("ffn1_w_up", l),
                    wd1=rows("ffn1_w_down", l))

    def out_params(l):
        return dict(wout=rows("w_out", l), g2=row1(ffn2_norm[l]), wg2=rows("ffn2_w_gate", l),
                    wu2=rows("ffn2_w_up", l), wd2=rows("ffn2_w_down", l))

    def mix_params(l):
        cw = jnp.transpose(G[("conv_w", None)][:, l], (1, 0, 2)).reshape(SSD_CONV, -1)
        return dict(
            gm=row1(mix_norm[l]), win=rows("w_in", l),
            cw=jnp.pad(cw, ((0, SUBLANE - SSD_CONV), (0, 0))), cb=row1(conv_b[l]),
            par=jnp.pad(jnp.stack([jnp.repeat(dt_bias[l], HEAD_DIM), jnp.repeat(a_log[l], HEAD_DIM),
                                   jnp.repeat(d_skip[l], HEAD_DIM), ssd_norm[l]]), ((0, SUBLANE - 4), (0, 0))),
            ln=jnp.pad(jnp.stack([sgu_ln_g[l], sgu_ln_b[l]]), ((0, SUBLANE - 2), (0, 0))),
            sw=sgu_w[l], bst=jnp.pad(sgu_b[l].T, ((0, 0), (0, SUBLANE - SGU_GROUPS))))

    xc = x.reshape(T, D)
    saved, lay = [], []
    for l in range(L):
        if l == 0:
            gathered(0, relay(0, order))
        else:
            gathered(1 + 2 * l, xc)
        p = ffn1_params(l)
        x1, *ffn1_saved = _ffn_fwd(xc, p["g1"], p["wg1"], p["wu1"], p["wd1"], f"ffn1_fwd_{l}")
        if l == 0:
            gathered(1, relay(1, x1))
        p.update(mix_params(l))
        lay.append(p)
        proj, ht = _norm_mm(x1, p["gm"], p["win"], f"in_proj_{l}")
        o_att, lse = _attn_fwd(proj, bias, B, S, f"attn_fwd_{l}")
        tok = relay(2 + 2 * l, o_att)
        pre = _conv_fwd(proj, p["cw"], p["cb"], B, S, f"conv_fwd_{l}", after=tok)
        y_ssd, sall = _ssd_fwd(pre, proj, p["par"], B, S, f"ssd_fwd_{l}")
        y_sgu = _sgu_fwd(proj, p["ln"], p["sw"], p["bst"], B, S, f"sgu_fwd_{l}", after=tok)
        ycat = jnp.concatenate([o_att.astype(BF16), y_ssd, y_sgu], axis=1)
        gathered(2 + 2 * l, ycat)
        p.update(out_params(l))
        x2 = _mm(ycat, p["wout"], "nn", f"out_proj_{l}", residual=x1)
        tok = relay(3 + 2 * l, x2) if l + 1 < L else None
        x3, *ffn2_saved = _ffn_fwd(x2, p["g2"], p["wg2"], p["wu2"], p["wd2"], f"ffn2_fwd_{l}", after=tok)
        saved.append(dict(x0=xc, ffn1=ffn1_saved, x1=x1, ht=ht, proj=proj, o_att=o_att, lse=lse, pre=pre,
                          sall=sall, ycat=ycat, x2=x2, ffn2=ffn2_saved))
        xc = x3
    loss_part, dx, dgf = _final_loss(xc, row1(final_norm), loss_target.reshape(T, D), "final_loss")

    gl = [dict() for _ in range(L)]
    gstarted, gorder = [], [order]

    def to_blocks(k, a):
        return a.reshape(N_DEV, -1, a.shape[-1]).astype(BF16)

    def send_grads(keys, l, extra, tag, small_names=None):
        xs = [to_blocks(k, gl[l][k]) for k in keys] + extra
        flags = [True] * len(keys) + [False] * len(extra)
        st = _xchg_start(xs, flags, gorder[0], f"grads_start_{tag}")
        gorder[0] = st[-1]
        gstarted.append((keys, l, st, flags, tag, small_names))

    def small_grads(names):
        sm = {}
        for k in names:
            if k == "final_norm":
                sm[k] = dgf.reshape(-1)
            elif k == "loss":
                sm[k] = loss_part[0, :1]
            else:
                sm[k] = jnp.stack([gl[l][k] for l in range(L)])
        return [_pack(sm, names)]

    def behind(a):
        return lax.optimization_barrier((a, gorder[0]))[0]

    for l in reversed(range(L)):
        p, s, g = lay[l], saved[l], gl[l]
        gfac, ufac, actt = s["ffn2"]
        dx2, dgt, dut, xn, dacc, g["ffn2_norm"] = _ffn_bwd_dx(
            dx, s["x2"], p["g2"], gfac, ufac, p["wg2"], p["wu2"], p["wd2"], f"ffn2_bwd_{l}")
        g["ffn2_w_gate"], g["ffn2_w_up"], g["ffn2_w_down"] = _ffn_dw(dgt, dut, actt, xn, dacc, f"ffn2_dw_{l}")
        if l == 0:
            send_grads(FFN2, 0, [], "l0f")
            dx2 = behind(dx2)
        dycat = _mm(dx2, p["wout"], "nt", f"out_proj_dx_{l}")
        g["w_out"] = _mm(s["ycat"], dx2, "tn", f"out_proj_dw_{l}", out_dtype=BF16, tm_cap=1024, tk_cap=512)
        dq, dk, dv = _attn_bwd(s["proj"], s["o_att"], s["lse"], dycat, bias, B, S, f"attn_bwd_{l}")
        dpre, dz, ddt, dpar = _ssd_bwd(s["pre"], s["proj"], s["sall"], dycat, p["par"], B, S, f"ssd_bwd_{l}")
        dxbc, dwb = _conv_bwd(dpre, s["proj"], p["cw"], B, S, f"conv_bwd_{l}")
        du, dvs, dln, dsw, dbst = _sgu_bwd(s["proj"], dycat, p["ln"], p["sw"], p["bst"], B, S, f"sgu_bwd_{l}")
        hsum = lambda r: r.reshape(SSD_HEADS, HEAD_DIM).sum(-1)
        g["conv_w"], g["conv_b"] = dwb[:SSD_CONV], dwb[SSD_CONV]
        g["dt_bias"], g["a_log"], g["d_skip"], g["ssd_norm"] = hsum(dpar[0]), hsum(dpar[1]), hsum(dpar[2]), dpar[3]
        g["sgu_ln_g"], g["sgu_ln_b"], g["sgu_w"], g["sgu_b"] = dln[0], dln[1], dsw, dbst[:, :SGU_GROUPS].T
        dproj = jnp.concatenate([dq, dk, dv, dz, du, dxbc, ddt, dvs], axis=1)
        g["w_in"] = _mm_resident_lhs(s["ht"], dproj, f"in_proj_dw_{l}")
        dx1, g["mix_norm"] = _norm_mm_bwd(dproj, s["x1"], p["gm"], p["win"], dx2, f"in_proj_bwd_{l}")
        if l == 0:
            send_grads(MIX, 0, small_grads(_SMALL_EARLY), "l0a", _SMALL_EARLY)
            dx1, small_packs = lax.optimization_barrier((behind(dx1), small_packs))
        gfac, ufac, actt = s["ffn1"]
        dx, dgt, dut, xn, dacc, g["ffn1_norm"] = _ffn_bwd_dx(
            dx1, s["x0"], p["g1"], gfac, ufac, p["wg1"], p["wu1"], p["wd1"], f"ffn1_bwd_{l}")
        if l > 0:
            g["ffn1_w_gate"], g["ffn1_w_up"], g["ffn1_w_down"] = _ffn_dw(dgt, dut, actt, xn, dacc,
                                                                        f"ffn1_dw_{l}")
            send_grads(big, l, [], f"l{l}")
            dx = behind(dx)
        else:
            g["ffn1_w_gate"] = _dw_one(dgt, xn, "ffn1_dwg_0")
            send_grads(("ffn1_w_gate",), 0, [], "l0b1")
            g["ffn1_w_up"] = _dw_one(dut, xn, "ffn1_dwu_0", after=gorder[0])
            send_grads(("ffn1_w_up",), 0, [], "l0b2")
            g["ffn1_w_down"] = _dw_one(actt, dacc, "ffn1_dwd_0", after=gorder[0])
    grad_x = dx.reshape(B, S, D)
    send_grads(("ffn1_w_down",), 0, small_grads(_SMALL_LAST), "l0b", _SMALL_LAST)

    res, after = {}, gorder[0]
    small_out = [dict() for _ in range(4)]
    me1 = me.reshape(1).astype(jnp.int32)
    for keys, l, st, flags, tag, names in gstarted:
        own, lands = _xchg_wait(st, flags, after, f"grads_wait_{tag}")
        for k, mine, pk in zip(keys, own, lands):
            res[k] = _adamw(me1, W[k], M[k], V[k], pk, mine, f"adamw_{k}_{l}", layer=l, into=res.get(k))
        done = [res[k][0] for k in keys]
        if names:
            outs = _adamw(me1, *small_packs[names], lands[-1], own[-1][None], f"adamw_small_{tag}")
            for d, o in zip(small_out, outs):
                u = _unpack(o, full_shapes, names)
                if "conv_w" in u:
                    u["conv_w"] = lax.dynamic_slice(u["conv_w"], (0, 0, me * cs), (L, SSD_CONV, cs))
                d.update(u)
                done.extend(u.values())
        after = lax.optimization_barrier(tuple(done))[0]
    back = lambda k, a: jnp.swapaxes(a, 1, 2) if col_sharded(k) else _unperm_cols(a) if k == "w_in" else a
    grads, deltas, new_m, new_v = [dict({k: back(k, res[k][i]) for k in big}, **small_out[i]) for i in range(4)]

    names = ("ffn1_norm", "ffn1_w_gate", "ffn1_w_up", "ffn1_w_down", "mix_norm", "w_in", "conv_w", "conv_b",
             "dt_bias", "a_log", "d_skip", "ssd_norm", "sgu_ln_g", "sgu_ln_b", "sgu_w", "sgu_b", "w_out",
             "ffn2_norm", "ffn2_w_gate", "ffn2_w_up", "ffn2_w_down", "final_norm")
    loss = grads["loss"][0]
    return (loss, grad_x, *[grads[n] for n in names], *[deltas[n] for n in names],
            *[new_m[n] for n in names], *[new_v[n] for n in names])
```

```python
import functools

import numpy as np
import jax
import jax.numpy as jnp
from jax import lax
from jax.experimental import pallas as pl
from jax.experimental.pallas import tpu as pltpu

F32, BF16 = jnp.float32, jnp.bfloat16
HI = lax.Precision.HIGH
MESH = pl.DeviceIdType.MESH
N_DEV = 8
VMEM_LIMIT_BYTES = 56 * 1024 * 1024
LANE, SUBLANE = 128, 8

HEAD_DIM = 64
ATT_W = 384
SSD_W = 384
SSD_HEADS = 6
SSD_STATE = 128
SSD_CONV = 4
CHUNK = 128
SSD_CONV_DIM = 896
SGU_W = 256
SGU_GROUPS = 4
D_IN = 2950
RMS_EPS = 1e-6
LN_EPS = 1e-5
NEG = -1e30

PW = 3072
Q0, K0, V0, Z0, U0, XBC0, DT0, VS0 = 0, 384, 768, 1152, 1536, 1792, 2688, 2816

ADAM_LR, ADAM_B1, ADAM_B2, ADAM_EPS, ADAM_WD, ADAM_STEP = 0.001, 0.9, 0.999, 1e-08, 0.01, 10


def _cp(sem=None):
    return pltpu.CompilerParams(dimension_semantics=sem, vmem_limit_bytes=VMEM_LIMIT_BYTES)


def _tile(n, cap, mult=LANE):
    best = None
    t = mult
    while t <= min(n, cap):
        if n % t == 0:
            best = t
        t += mult
    return best if best is not None else n


def _dot(a, b, prec=None):
    return jnp.dot(a, b, preferred_element_type=F32, precision=prec)


def _dot_nt(a, b, prec=None):
    return lax.dot_general(a, b, (((1,), (1,)), ((), ())), preferred_element_type=F32, precision=prec)


def _dot_tn(a, b, prec=None):
    return lax.dot_general(a, b, (((0,), (0,)), ((), ())), preferred_element_type=F32, precision=prec)


def _sigmoid(x):
    return 1.0 / (1.0 + jnp.exp(-x))


def _silu(x):
    return x * _sigmoid(x)


def _gelu(x):
    return 0.5 * x * (1.0 + lax.erf(x * 0.7071067811865476))


def _softplus(x):
    return jnp.maximum(x, 0.0) + jnp.log(1.0 + jnp.exp(-jnp.abs(x)))


def _rms_fwd(x, g):
    rstd = lax.rsqrt(jnp.mean(x * x, axis=-1, keepdims=True) + RMS_EPS)
    xhat = x * rstd
    return xhat * g, xhat, rstd


def _rms_bwd(dy, xhat, rstd, g):
    dxhat = dy * g
    dx = rstd * (dxhat - xhat * jnp.mean(dxhat * xhat, axis=-1, keepdims=True))
    return dx, dy * xhat


def _resident(shape):
    return pl.BlockSpec(shape, lambda *_: (0,) * len(shape), pipeline_mode=pl.Buffered(1))


def _mm(a, b, mode, name, out_dtype=F32, residual=None, tm_cap=512, tn_cap=1024, tk_cap=1024):
    if mode == "nn":
        (M, K), (_, N) = a.shape, b.shape
    elif mode == "nt":
        (M, K), (N, _) = a.shape, b.shape
    else:
        (K, M), (_, N) = a.shape, b.shape
    tm, tn, tk = _tile(M, tm_cap), _tile(N, tn_cap), _tile(K, tk_cap)
    nk = K // tk
    if mode == "tn":
        a_spec = pl.BlockSpec((tk, tm), lambda i, j, k: (k, i))
    else:
        a_spec = pl.BlockSpec((tm, tk), lambda i, j, k: (i, k))
    if mode == "nt":
        b_spec = pl.BlockSpec((tn, tk), lambda i, j, k: (j, k))
    else:
        b_spec = pl.BlockSpec((tk, tn), lambda i, j, k: (k, j))
    o_spec = pl.BlockSpec((tm, tn), lambda i, j, k: (i, j))
    has_res = residual is not None

    def prod(a_ref, b_ref):
        av = a_ref[...].astype(BF16)
        bv = b_ref[...].astype(BF16)
        if mode == "nn":
            return _dot(av, bv)
        if mode == "nt":
            return _dot_nt(av, bv)
        return _dot_tn(av, bv)

    def body(*refs):
        a_ref, b_ref = refs[:2]
        r_ref = refs[2] if has_res else None
        o_ref = refs[2 + has_res]
        if nk == 1:
            o = prod(a_ref, b_ref)
            if has_res:
                o = r_ref[...] + o
            o_ref[...] = o.astype(out_dtype)
            return
        acc = refs[3 + has_res]
        k = pl.program_id(2)

        @pl.when(k == 0)
        def _():
            acc[...] = jnp.zeros_like(acc)

        acc[...] += prod(a_ref, b_ref)

        @pl.when(k == nk - 1)
        def _():
            o = acc[...]
            if has_res:
                o = r_ref[...] + o
            o_ref[...] = o.astype(out_dtype)

    ins = [a, b] + ([residual] if has_res else [])
    in_specs = [a_spec, b_spec] + ([o_spec] if has_res else [])
    return pl.pallas_call(
        body, name=name, grid=(M // tm, N // tn, nk),
        in_specs=in_specs, out_specs=o_spec,
        out_shape=jax.ShapeDtypeStruct((M, N), out_dtype),
        scratch_shapes=[pltpu.VMEM((tm, tn), F32)] if nk > 1 else [],
        compiler_params=_cp(("parallel", "parallel", "arbitrary")),
    )(*ins)


def _after(after):
    return ([after], [_ANY]) if after is not None else ([], [])


def _ffn_fwd(x, g, wgt, wut, wd, name, after=None):
    T, D = x.shape
    F = wgt.shape[0]
    tm = _tile(T, 256)

    def body(x_ref, g_ref, wg_ref, wu_ref, wd_ref, *rest):
        out_ref, dgf_ref, sl_ref, actt_ref = rest[-4:]
        xv = x_ref[...]
        xn = _rms_fwd(xv, g_ref[...])[0].astype(BF16)
        gate = _dot_nt(xn, wg_ref[...])
        up = _dot_nt(xn, wu_ref[...])
        sig = _sigmoid(gate)
        sl = gate * sig
        dgf_ref[...] = (up * (sig * (1.0 + gate * (1.0 - sig)))).astype(BF16)
        sl_ref[...] = sl.astype(BF16)
        act = (sl * up).astype(BF16)
        actt_ref[...] = act.T
        out_ref[...] = xv + 0.5 * _dot(act, wd_ref[...])

    row = lambda w: pl.BlockSpec((tm, w), lambda i: (i, 0))
    extra, extra_specs = _after(after)
    return pl.pallas_call(
        body, name=name, grid=(T // tm,),
        in_specs=[row(D), _resident((1, D)), _resident((F, D)), _resident((F, D)), _resident((F, D))] + extra_specs,
        out_specs=[row(D), row(F), row(F), pl.BlockSpec((F, tm), lambda i: (0, i))],
        out_shape=[jax.ShapeDtypeStruct((T, D), F32),
                   jax.ShapeDtypeStruct((T, F), BF16),
                   jax.ShapeDtypeStruct((T, F), BF16),
                   jax.ShapeDtypeStruct((F, T), BF16)],
        compiler_params=_cp(("parallel",)),
    )(x, g, wgt, wut, wd, *extra)


def _ffn_bwd_dx(dout, x, g, dgf, sl, wg, wu, wd, name):
    T, D = x.shape
    F = wg.shape[0]
    tm = _tile(T, 256)

    def body(dout_ref, x_ref, g_ref, dgf_ref, sl_ref, wg_ref, wu_ref, wd_ref,
             dx_ref, dgt_ref, dut_ref, xn_ref, dacc_ref, dg_ref):
        @pl.when(pl.program_id(0) == 0)
        def _():
            dg_ref[...] = jnp.zeros_like(dg_ref)

        gv = g_ref[...]
        dout_v = dout_ref[...]
        xn, xhat, rstd = _rms_fwd(x_ref[...], gv)
        xn_ref[...] = xn.astype(BF16)
        dacc = (0.5 * dout_v).astype(BF16)
        dacc_ref[...] = dacc
        dact = _dot_nt(dacc, wd_ref[...])
        dgate = (dact * dgf_ref[...].astype(F32)).astype(BF16)
        dup = (dact * sl_ref[...].astype(F32)).astype(BF16)
        dgt_ref[...] = dgate.T
        dut_ref[...] = dup.T
        dxn = _dot(dgate, wg_ref[...]) + _dot(dup, wu_ref[...])
        dx, dgrow = _rms_bwd(dxn, xhat, rstd, gv)
        dx_ref[...] = dout_v + dx
        dg_ref[...] += jnp.sum(dgrow, axis=0, keepdims=True)

    row = lambda w: pl.BlockSpec((tm, w), lambda i: (i, 0))
    tr = pl.BlockSpec((F, tm), lambda i: (0, i))
    return pl.pallas_call(
        body, name=name, grid=(T // tm,),
        in_specs=[row(D), row(D), _resident((1, D)), row(F), row(F),
                  _resident((F, D)), _resident((F, D)), _resident((F, D))],
        out_specs=[row(D), tr, tr, row(D), row(D), pl.BlockSpec((1, D), lambda i: (0, 0))],
        out_shape=[jax.ShapeDtypeStruct((T, D), F32)] + [jax.ShapeDtypeStruct((F, T), BF16)] * 2
        + [jax.ShapeDtypeStruct((T, D), BF16)] * 2 + [jax.ShapeDtypeStruct((1, D), F32)],
        compiler_params=_cp(("arbitrary",)),
    )(dout, x, g, dgf, sl, wg, wu, wd)


def _ffn_dw(dgt, dut, actt, xn, dacc, name):
    F, T = dgt.shape
    D = xn.shape[1]
    th = _tile(F, 256)

    def body(dg_ref, du_ref, a_ref, xn_ref, dacc_ref, dwg_ref, dwu_ref, dwd_ref):
        xv = xn_ref[...]
        dwg_ref[...] = _dot(dg_ref[...], xv).astype(BF16)
        dwu_ref[...] = _dot(du_ref[...], xv).astype(BF16)
        dwd_ref[...] = _dot(a_ref[...], dacc_ref[...]).astype(BF16)

    tile = pl.BlockSpec((th, T), lambda j: (j, 0))
    out = pl.BlockSpec((th, D), lambda j: (j, 0))
    return pl.pallas_call(
        body, name=name, grid=(F // th,),
        in_specs=[tile, tile, tile, _resident((T, D)), _resident((T, D))],
        out_specs=[out, out, out], out_shape=[jax.ShapeDtypeStruct((F, D), BF16)] * 3,
        compiler_params=_cp(("parallel",)),
    )(dgt, dut, actt, xn, dacc)


def _dw_one(lt, r, name, after=None):
    F, T = lt.shape
    D = r.shape[1]
    th = _tile(F, 256)
    extra, extra_specs = _after(after)

    def body(l_ref, r_ref, *rest):
        rest[-1][...] = _dot(l_ref[...], r_ref[...]).astype(BF16)

    return pl.pallas_call(
        body, name=name, grid=(F // th,),
        in_specs=[pl.BlockSpec((th, T), lambda j: (j, 0)), _resident((T, D))] + extra_specs,
        out_specs=pl.BlockSpec((th, D), lambda j: (j, 0)),
        out_shape=jax.ShapeDtypeStruct((F, D), BF16),
        compiler_params=_cp(("parallel",)),
    )(lt, r, *extra)


def _norm_mm(x, g, w, name):
    T, D = x.shape
    N = w.shape[1]
    tm = _tile(T, 512)

    def body(x_ref, g_ref, w_ref, o_ref, ht_ref):
        xn = _rms_fwd(x_ref[...], g_ref[...])[0]
        ht_ref[...] = xn.T.astype(BF16)
        o_ref[...] = _dot(xn.astype(BF16), w_ref[...])

    return pl.pallas_call(
        body, name=name, grid=(T // tm,),
        in_specs=[pl.BlockSpec((tm, D), lambda i: (i, 0)), _resident((1, D)), _resident((D, N))],
        out_specs=[pl.BlockSpec((tm, N), lambda i: (i, 0)), pl.BlockSpec((D, tm), lambda i: (0, i))],
        out_shape=[jax.ShapeDtypeStruct((T, N), F32), jax.ShapeDtypeStruct((D, T), BF16)],
        compiler_params=_cp(("parallel",)),
    )(x, g, w)


def _norm_mm_bwd(dproj, x, g, w, dres, name):
    T, D = x.shape
    N = w.shape[1]
    tm = _tile(T, 512)

    def body(dp_ref, x_ref, g_ref, w_ref, dres_ref, dx_ref, dg_ref):
        @pl.when(pl.program_id(0) == 0)
        def _():
            dg_ref[...] = jnp.zeros_like(dg_ref)

        gv = g_ref[...]
        dh = _dot_nt(dp_ref[...], w_ref[...])
        _, xhat, rstd = _rms_fwd(x_ref[...], gv)
        dx, dgrow = _rms_bwd(dh, xhat, rstd, gv)
        dx_ref[...] = dres_ref[...] + dx
        dg_ref[...] += jnp.sum(dgrow, axis=0, keepdims=True)

    row = pl.BlockSpec((tm, D), lambda i: (i, 0))
    one = pl.BlockSpec((1, D), lambda i: (0, 0))
    return pl.pallas_call(
        body, name=name, grid=(T // tm,),
        in_specs=[pl.BlockSpec((tm, N), lambda i: (i, 0)), row, _resident((1, D)), _resident((D, N)), row],
        out_specs=[row, one],
        out_shape=[jax.ShapeDtypeStruct((T, D), F32), jax.ShapeDtypeStruct((1, D), F32)],
        compiler_params=_cp(("arbitrary",)),
    )(dproj, x, g, w, dres)


def _mm_resident_lhs(at, b, name, tn_cap=512):
    M, K = at.shape
    N = b.shape[1]
    tn = _tile(N, tn_cap)

    def body(a_ref, b_ref, o_ref):
        o_ref[...] = _dot(a_ref[...], b_ref[...]).astype(BF16)

    return pl.pallas_call(
        body, name=name, grid=(N // tn,),
        in_specs=[_resident((M, K)), pl.BlockSpec((K, tn), lambda j: (0, j))],
        out_specs=pl.BlockSpec((M, tn), lambda j: (0, j)),
        out_shape=jax.ShapeDtypeStruct((M, N), BF16),
        compiler_params=_cp(("parallel",)),
    )(at, b)


def _final_loss(x, g, target, name):
    T, D = x.shape
    tm = _tile(T, 512)

    def body(x_ref, g_ref, t_ref, loss_ref, dx_ref, dg_ref):
        @pl.when(pl.program_id(0) == 0)
        def _():
            dg_ref[...] = jnp.zeros_like(dg_ref)
            loss_ref[...] = jnp.zeros_like(loss_ref)

        gv = g_ref[...]
        y, xhat, rstd = _rms_fwd(x_ref[...], gv)
        err = y - t_ref[...]
        part = 0.5 * jnp.sum(jnp.mean(err * err, axis=-1, keepdims=True), axis=0, keepdims=True)
        loss_ref[...] += jnp.broadcast_to(part, loss_ref.shape)
        dy = err * (1.0 / D)
        dx, dgrow = _rms_bwd(dy, xhat, rstd, gv)
        dx_ref[...] = dx
        dg_ref[...] += jnp.sum(dgrow, axis=0, keepdims=True)

    row = pl.BlockSpec((tm, D), lambda i: (i, 0))
    one = pl.BlockSpec((1, D), lambda i: (0, 0))
    return pl.pallas_call(
        body, name=name, grid=(T // tm,),
        in_specs=[row, one, row],
        out_specs=[pl.BlockSpec((1, LANE), lambda i: (0, 0)), row, one],
        out_shape=[jax.ShapeDtypeStruct((1, LANE), F32), jax.ShapeDtypeStruct((T, D), F32),
                   jax.ShapeDtypeStruct((1, D), F32)],
        compiler_params=_cp(("arbitrary",)),
    )(x, g, target)


def _attn_bias(S, bq):
    d = np.arange(bq)[:, None] - np.arange(S)[None, :] + (S // bq - 1) * bq
    ok = d >= 0
    mult = ((ok & (d <= 128)).astype(np.float32) + (ok & (d % 4 == 0) & (d <= 512))
            + (ok & (d % 16 == 0) & (d <= 2048)))
    return jnp.asarray(np.where(mult > 0, np.log(np.maximum(mult, 1.0)), NEG).astype(np.float32))


def _attn_fwd(proj, bias, B, S, name):
    T = B * S
    bq = bias.shape[0]
    nb = S // bq
    qcol, kcol, vcol = Q0 // LANE, K0 // LANE, V0 // LANE

    def body(q_ref, k_ref, v_ref, t_ref, o_ref, lse_ref, ks, vs):
        for hh in range(2):
            sl = slice(HEAD_DIM * hh, HEAD_DIM * (hh + 1))
            ks[hh] = k_ref[:, sl].astype(BF16)
            vs[hh] = v_ref[:, sl].astype(BF16)
        for hh in range(2):
            sl = slice(HEAD_DIM * hh, HEAD_DIM * (hh + 1))
            for qb in range(nb):
                w, off, rows = bq * (qb + 1), (nb - 1 - qb) * bq, slice(qb * bq, (qb + 1) * bq)
                q = (q_ref[rows, sl] * 0.125).astype(BF16)
                s = _dot_nt(q, ks[hh, 0:w, :]) + t_ref[:, off:off + w]
                m = jnp.max(s, axis=-1, keepdims=True)
                p = jnp.exp(s - m)
                l = jnp.sum(p, axis=-1, keepdims=True)
                o_ref[rows, sl] = _dot(p.astype(BF16), vs[hh, 0:w, :]) / l
                lse_ref[rows, hh:hh + 1] = m + jnp.log(l)

    blk = lambda c0: pl.BlockSpec((S, LANE), lambda b, p: (b, c0 + p))
    return pl.pallas_call(
        body, name=name, grid=(B, ATT_W // LANE),
        in_specs=[blk(qcol), blk(kcol), blk(vcol), _resident((bq, S))],
        out_specs=[pl.BlockSpec((S, LANE), lambda b, p: (b, p)),
                   pl.BlockSpec((None, None, S, 2), lambda b, p: (b, p, 0, 0))],
        out_shape=[jax.ShapeDtypeStruct((T, ATT_W), F32),
                   jax.ShapeDtypeStruct((B, ATT_W // LANE, S, 2), F32)],
        scratch_shapes=[pltpu.VMEM((2, S, HEAD_DIM), BF16)] * 2,
        compiler_params=_cp(("parallel", "parallel")),
    )(proj, proj, proj, bias)


def _attn_bwd(proj, o, lse, dy, bias, B, S, name):
    T = B * S
    bq = bias.shape[0]
    nb = S // bq
    qcol, kcol, vcol = Q0 // LANE, K0 // LANE, V0 // LANE

    def body(q_ref, k_ref, v_ref, o_ref, lse_ref, do_ref, t_ref, dq_ref, dk_ref, dv_ref, ks, vs, dks, dvs):
        for hh in range(2):
            sl = slice(HEAD_DIM * hh, HEAD_DIM * (hh + 1))
            ks[hh] = k_ref[:, sl].astype(BF16)
            vs[hh] = v_ref[:, sl].astype(BF16)
        dks[...] = jnp.zeros_like(dks)
        dvs[...] = jnp.zeros_like(dvs)
        for hh in range(2):
            sl = slice(HEAD_DIM * hh, HEAD_DIM * (hh + 1))
            for qb in range(nb):
                w, off, rows = bq * (qb + 1), (nb - 1 - qb) * bq, slice(qb * bq, (qb + 1) * bq)
                q = (q_ref[rows, sl] * 0.125).astype(BF16)
                do = do_ref[rows, sl]
                dob = do.astype(BF16)
                delta = jnp.sum(do * o_ref[rows, sl], axis=-1, keepdims=True)
                k, v = ks[hh, 0:w, :], vs[hh, 0:w, :]
                s = _dot_nt(q, k) + t_ref[:, off:off + w]
                p = jnp.exp(s - lse_ref[rows, hh:hh + 1])
                ds = (p * (_dot_nt(dob, v) - delta)).astype(BF16)
                dq_ref[rows, sl] = (_dot(ds, k) * 0.125).astype(dq_ref.dtype)
                dks[hh, 0:w, :] += _dot_tn(ds, q)
                dvs[hh, 0:w, :] += _dot_tn(p.astype(BF16), dob)
            dk_ref[:, sl] = dks[hh].astype(dk_ref.dtype)
            dv_ref[:, sl] = dvs[hh].astype(dv_ref.dtype)

    blk = lambda c0: pl.BlockSpec((S, LANE), lambda b, p: (b, c0 + p))
    own = pl.BlockSpec((S, LANE), lambda b, p: (b, p))
    return pl.pallas_call(
        body, name=name, grid=(B, ATT_W // LANE),
        in_specs=[blk(qcol), blk(kcol), blk(vcol), own,
                  pl.BlockSpec((None, None, S, 2), lambda b, p: (b, p, 0, 0)), own, _resident((bq, S))],
        out_specs=[own, own, own],
        out_shape=[jax.ShapeDtypeStruct((T, ATT_W), BF16)] * 3,
        scratch_shapes=[pltpu.VMEM((2, S, HEAD_DIM), BF16)] * 2 + [pltpu.VMEM((2, S, HEAD_DIM), F32)] * 2,
        compiler_params=_cp(("parallel", "parallel")),
    )(proj, proj, proj, o, lse, dy, bias)


def _conv_fwd(proj, cw, cb, B, S, name, after=None):
    T = B * S
    nc = SSD_CONV_DIM // LANE
    c0 = XBC0 // LANE
    extra, extra_specs = _after(after)

    def body(x_ref, w_ref, b_ref, *rest):
        o_ref = rest[-1]
        x = x_ref[...]
        t = lax.broadcasted_iota(jnp.int32, (S, 1), 0)
        acc = b_ref[...] + w_ref[SSD_CONV - 1:SSD_CONV, :] * x
        for k in range(SSD_CONV - 1):
            sh = SSD_CONV - 1 - k
            xs = jnp.where(t >= sh, pltpu.roll(x, sh, 0), 0.0)
            acc = acc + w_ref[k:k + 1, :] * xs
        o_ref[...] = acc

    return pl.pallas_call(
        body, name=name, grid=(B, nc),
        in_specs=[pl.BlockSpec((S, LANE), lambda b, j: (b, c0 + j)),
                  pl.BlockSpec((SUBLANE, LANE), lambda b, j: (0, j)),
                  pl.BlockSpec((1, LANE), lambda b, j: (0, j))] + extra_specs,
        out_specs=pl.BlockSpec((S, LANE), lambda b, j: (b, j)),
        out_shape=jax.ShapeDtypeStruct((T, SSD_CONV_DIM), F32),
        compiler_params=_cp(("parallel", "parallel")),
    )(proj, cw, cb, *extra)


def _conv_bwd(dpre, proj, cw, B, S, name):
    T = B * S
    nc = SSD_CONV_DIM // LANE
    c0 = XBC0 // LANE

    def body(d_ref, x_ref, w_ref, dx_ref, dwb_ref):
        @pl.when(pl.program_id(1) == 0)
        def _():
            dwb_ref[...] = jnp.zeros_like(dwb_ref)

        d = d_ref[...]
        x = x_ref[...]
        t = lax.broadcasted_iota(jnp.int32, (S, 1), 0)
        dx = w_ref[SSD_CONV - 1:SSD_CONV, :] * d
        rows = [None] * SUBLANE
        rows[SSD_CONV - 1] = jnp.sum(d * x, axis=0, keepdims=True)
        for k in range(SSD_CONV - 1):
            sh = SSD_CONV - 1 - k
            dx = dx + w_ref[k:k + 1, :] * jnp.where(t < S - sh, pltpu.roll(d, S - sh, 0), 0.0)
            xs = jnp.where(t >= sh, pltpu.roll(x, sh, 0), 0.0)
            rows[k] = jnp.sum(d * xs, axis=0, keepdims=True)
        rows[SSD_CONV] = jnp.sum(d, axis=0, keepdims=True)
        dx_ref[...] = dx.astype(BF16)
        r = lax.broadcasted_iota(jnp.int32, (SUBLANE, LANE), 0)
        upd = jnp.zeros((SUBLANE, LANE), F32)
        for k in range(SSD_CONV + 1):
            upd = upd + jnp.where(r == k, rows[k], 0.0)
        dwb_ref[...] += upd

    return pl.pallas_call(
        body, name=name, grid=(nc, B),
        in_specs=[pl.BlockSpec((S, LANE), lambda j, b: (b, j)),
                  pl.BlockSpec((S, LANE), lambda j, b: (b, c0 + j)),
                  pl.BlockSpec((SUBLANE, LANE), lambda j, b: (0, j))],
        out_specs=[pl.BlockSpec((S, LANE), lambda j, b: (b, j)),
                   pl.BlockSpec((SUBLANE, LANE), lambda j, b: (0, j))],
        out_shape=[jax.ShapeDtypeStruct((T, SSD_CONV_DIM), BF16),
                   jax.ShapeDtypeStruct((SUBLANE, SSD_CONV_DIM), F32)],
        compiler_params=_cp(("parallel", "arbitrary")),
    )(dpre, proj, cw)


def _ssd_consts():
    e = np.zeros((LANE, SSD_W), np.float32)
    p = np.zeros((SUBLANE, SSD_W), np.float32)
    for h in range(SSD_HEADS):
        e[h, HEAD_DIM * h:HEAD_DIM * (h + 1)] = 1.0
        p[h, HEAD_DIM * h] = 1.0
    return jnp.asarray(e), jnp.asarray(p)


def _ssd_chunk(pre, z, dtr, sprev, par, e_mat, psel):
    L = CHUNK
    xc = _silu(pre)
    xs, bm, cm = xc[:, :SSD_W], xc[:, SSD_W:SSD_W + 2 * SSD_STATE], xc[:, SSD_W + 2 * SSD_STATE:]
    dtb, alog, dskip, ng = par[0:1], par[1:2], par[2:3], par[3:4]
    dt = _softplus(_dot(dtr, e_mat, HI) + dtb)
    a = dt * (-jnp.exp(alog))
    X = xs * dt
    ri = lax.broadcasted_iota(jnp.int32, (L, L), 0)
    ci = lax.broadcasted_iota(jnp.int32, (L, L), 1)
    tril = ri >= ci
    acs = _dot(tril.astype(F32), a, HI)
    acs_t = _dot_nt(psel, acs, HI)
    ecs = jnp.exp(acs)
    alast = acs[L - 1:L, :]
    xd = (X * jnp.exp(alast - acs)).astype(BF16)
    xb = X.astype(BF16)
    col = lax.broadcasted_iota(jnp.int32, (1, SSD_W), 1)
    sb = sprev.astype(BF16)
    bgs = [bm[:, SSD_STATE * g:SSD_STATE * (g + 1)].astype(BF16) for g in range(2)]
    cgs = [cm[:, SSD_STATE * g:SSD_STATE * (g + 1)].astype(BF16) for g in range(2)]
    cbs = [_dot_nt(cgs[g], bgs[g]) for g in range(2)]
    first = lax.broadcasted_iota(jnp.int32, (1, LANE), 1) < HEAD_DIM
    y_tiles, s_tiles = [], []
    for t in range(SSD_W // LANE):
        cl = slice(LANE * t, LANE * (t + 1))
        xb_t, xd_t, sb_t = xb[:, cl], xd[:, cl], sb[:, cl]
        per_head = []
        for h in (2 * t, 2 * t + 1):
            seg = acs[:, HEAD_DIM * h:HEAD_DIM * h + 1] - acs_t[h:h + 1, :]
            dec = jnp.exp(jnp.where(tril, seg, NEG))
            per_head.append(_dot((cbs[h // 3] * dec).astype(BF16), xb_t))
        y_t = jnp.where(first, per_head[0], per_head[1])
        ga, gb = (2 * t) // 3, (2 * t + 1) // 3
        if ga == gb:
            y_off, s_add = _dot(cgs[ga], sb_t), _dot_tn(bgs[ga], xd_t)
        else:
            y_off = jnp.where(first, _dot(cgs[ga], sb_t), _dot(cgs[gb], sb_t))
            s_add = jnp.where(first, _dot_tn(bgs[ga], xd_t), _dot_tn(bgs[gb], xd_t))
        y_tiles.append(y_t + y_off * ecs[:, cl])
        s_tiles.append(s_add)
    y = dskip * xs + jnp.concatenate(y_tiles, axis=1)
    snew = sprev * jnp.exp(alast) + jnp.concatenate(s_tiles, axis=1)
    yg = y * _silu(z)
    sq = yg * yg
    g0 = col < SSD_W // 2
    ms0 = jnp.sum(jnp.where(g0, sq, 0.0), axis=-1, keepdims=True) * (2.0 / SSD_W)
    ms1 = jnp.sum(jnp.where(g0, 0.0, sq), axis=-1, keepdims=True) * (2.0 / SSD_W)
    r = jnp.where(g0, lax.rsqrt(ms0 + RMS_EPS), lax.rsqrt(ms1 + RMS_EPS))
    return yg * r * ng, snew


SSD_CHUNKS_PER_STEP = 2


def _ssd_chunks_per_step(S):
    k = SSD_CHUNKS_PER_STEP
    while (S // CHUNK) % k:
        k //= 2
    return k


def _ssd_fwd(pre, proj, par, B, S, name):
    T = B * S
    k = _ssd_chunks_per_step(S)
    nc, rows = S // (CHUNK * k), CHUNK * k
    e_mat, psel = _ssd_consts()

    def body(pre_ref, z_ref, dt_ref, par_ref, e_ref, p_ref, y_ref, sall_ref, st):
        @pl.when(pl.program_id(1) == 0)
        def _():
            st[...] = jnp.zeros_like(st)

        sprev = st[...]
        for i in range(k):
            r = slice(CHUNK * i, CHUNK * (i + 1))
            sall_ref[i] = sprev
            y, sprev = _ssd_chunk(pre_ref[r, :], z_ref[r, :], dt_ref[r, :], sprev, par_ref[...], e_ref[...],
                                  p_ref[...])
            y_ref[r, :] = y.astype(BF16)
        st[...] = sprev

    row = lambda b, c: b * nc + c
    full = lambda shp: pl.BlockSpec(shp, lambda b, c: (0, 0))
    return pl.pallas_call(
        body, name=name, grid=(B, nc),
        in_specs=[pl.BlockSpec((rows, SSD_CONV_DIM), lambda b, c: (row(b, c), 0)),
                  pl.BlockSpec((rows, SSD_W), lambda b, c: (row(b, c), Z0 // SSD_W)),
                  pl.BlockSpec((rows, LANE), lambda b, c: (row(b, c), DT0 // LANE)),
                  full((SUBLANE, SSD_W)), full((LANE, SSD_W)), full((SUBLANE, SSD_W))],
        out_specs=[pl.BlockSpec((rows, SSD_W), lambda b, c: (row(b, c), 0)),
                   pl.BlockSpec((k, SSD_STATE, SSD_W), lambda b, c: (row(b, c), 0, 0))],
        out_shape=[jax.ShapeDtypeStruct((T, SSD_W), BF16),
                   jax.ShapeDtypeStruct((B * nc * k, SSD_STATE, SSD_W), F32)],
        scratch_shapes=[pltpu.VMEM((SSD_STATE, SSD_W), F32)],
        compiler_params=_cp(("parallel", "arbitrary")),
    )(pre, proj, proj, par, e_mat, psel)


def _ssd_bwd(pre, proj, sall, dy, par, B, S, name):
    T = B * S
    k = _ssd_chunks_per_step(S)
    nc, rows = S // (CHUNK * k), CHUNK * k
    e_mat, psel = _ssd_consts()

    def body(pre_ref, z_ref, dt_ref, sall_ref, dy_ref, par_ref, e_ref, p_ref,
             dpre_ref, dz_ref, ddt_ref, dpar_ref, ds):
        b, c = pl.program_id(0), pl.program_id(1)

        @pl.when(c == 0)
        def _():
            ds[...] = jnp.zeros_like(ds)

        @pl.when((b == 0) & (c == 0))
        def _():
            dpar_ref[...] = jnp.zeros_like(dpar_ref)

        e_v, p_v = e_ref[...], p_ref[...]
        fn = lambda pre, z, dtr, sprev, par: _ssd_chunk(pre, z, dtr, sprev, par, e_v, p_v)
        dstate, dpar_sum = ds[...], None
        for i in reversed(range(k)):
            r = slice(CHUNK * i, CHUNK * (i + 1))
            _, vjp = jax.vjp(fn, pre_ref[r, :], z_ref[r, :], dt_ref[r, :], sall_ref[i], par_ref[...])
            dpre, dz, ddt, dstate, dpar = vjp((dy_ref[r, :], dstate))
            dpre_ref[r, :] = dpre
            dz_ref[r, :] = dz.astype(BF16)
            ddt_ref[r, :] = ddt.astype(BF16)
            dpar_sum = dpar if dpar_sum is None else dpar_sum + dpar
        dpar_ref[...] += dpar_sum
        ds[...] = dstate

    row = lambda b, c: b * nc + (nc - 1 - c)
    full = lambda shp: pl.BlockSpec(shp, lambda b, c: (0, 0))
    return pl.pallas_call(
        body, name=name, grid=(B, nc),
        in_specs=[pl.BlockSpec((rows, SSD_CONV_DIM), lambda b, c: (row(b, c), 0)),
                  pl.BlockSpec((rows, SSD_W), lambda b, c: (row(b, c), Z0 // SSD_W)),
                  pl.BlockSpec((rows, LANE), lambda b, c: (row(b, c), DT0 // LANE)),
                  pl.BlockSpec((k, SSD_STATE, SSD_W), lambda b, c: (row(b, c), 0, 0)),
                  pl.BlockSpec((rows, SSD_W), lambda b, c: (row(b, c), ATT_W // SSD_W)),
                  full((SUBLANE, SSD_W)), full((LANE, SSD_W)), full((SUBLANE, SSD_W))],
        out_specs=[pl.BlockSpec((rows, SSD_CONV_DIM), lambda b, c: (row(b, c), 0)),
                   pl.BlockSpec((rows, SSD_W), lambda b, c: (row(b, c), 0)),
                   pl.BlockSpec((rows, LANE), lambda b, c: (row(b, c), 0)),
                   full((SUBLANE, SSD_W))],
        out_shape=[jax.ShapeDtypeStruct((T, SSD_CONV_DIM), F32),
                   jax.ShapeDtypeStruct((T, SSD_W), BF16),
                   jax.ShapeDtypeStruct((T, LANE), BF16),
                   jax.ShapeDtypeStruct((SUBLANE, SSD_W), F32)],
        scratch_shapes=[pltpu.VMEM((SSD_STATE, SSD_W), F32)],
        compiler_params=_cp(("arbitrary", "arbitrary")),
    )(pre, proj, proj, sall, dy, par, e_mat, psel)


def _sgu_consts():
    e = np.zeros((SUBLANE, SGU_W), np.float32)
    for g in range(SGU_GROUPS):
        e[g, HEAD_DIM * g:HEAD_DIM * (g + 1)] = 1.0
    return jnp.asarray(e)


def _sgu_chunk(u_raw, v_raw, ln, w, bst, e4):
    L = CHUNK
    u = _gelu(u_raw)
    v = _gelu(v_raw)
    mu = jnp.mean(v, axis=-1, keepdims=True)
    vc = v - mu
    var = jnp.mean(vc * vc, axis=-1, keepdims=True)
    vn = vc * lax.rsqrt(var + LN_EPS) * ln[0:1] + ln[1:2]
    vb = vn.astype(BF16)
    ri = lax.broadcasted_iota(jnp.int32, (L, L), 0)
    ci = lax.broadcasted_iota(jnp.int32, (L, L), 1)
    tril = ri >= ci
    col = lax.broadcasted_iota(jnp.int32, (1, SGU_W), 1)
    mixed = _dot(bst, e4, HI)
    for g in range(SGU_GROUPS):
        wc = jnp.where(tril, w[g], 0.0).astype(BF16)
        gm = (col >= HEAD_DIM * g) & (col < HEAD_DIM * (g + 1))
        mixed = mixed + jnp.where(gm, _dot(wc, vb), 0.0)
    return u * mixed


def _sgu_fwd(proj, ln, w, bst, B, S, name, after=None):
    T = B * S
    e4 = _sgu_consts()
    extra, extra_specs = _after(after)
    k = 2 if (T // CHUNK) % 2 == 0 else 1
    rows = k * CHUNK

    def body(u_ref, v_ref, ln_ref, w_ref, b_ref, e_ref, *rest):
        y_ref = rest[-1]
        for i in range(k):
            r = slice(CHUNK * i, CHUNK * (i + 1))
            y_ref[r, :] = _sgu_chunk(u_ref[r, :], v_ref[r, :], ln_ref[...], w_ref[...], b_ref[...],
                                     e_ref[...]).astype(BF16)

    return pl.pallas_call(
        body, name=name, grid=(T // rows,),
        in_specs=[pl.BlockSpec((rows, SGU_W), lambda i: (i, U0 // SGU_W)),
                  pl.BlockSpec((rows, SGU_W), lambda i: (i, VS0 // SGU_W)),
                  pl.BlockSpec((SUBLANE, SGU_W), lambda i: (0, 0)),
                  pl.BlockSpec((SGU_GROUPS, CHUNK, CHUNK), lambda i: (0, 0, 0)),
                  pl.BlockSpec((CHUNK, SUBLANE), lambda i: (0, 0)),
                  pl.BlockSpec((SUBLANE, SGU_W), lambda i: (0, 0))] + extra_specs,
        out_specs=pl.BlockSpec((rows, SGU_W), lambda i: (i, 0)),
        out_shape=jax.ShapeDtypeStruct((T, SGU_W), BF16),
        compiler_params=_cp(("parallel",)),
    )(proj, proj, ln, w, bst, e4, *extra)


def _sgu_bwd(proj, dy, ln, w, bst, B, S, name):
    T = B * S
    e4 = _sgu_consts()
    ycol = (ATT_W + SSD_W) // SGU_W
    k = 2 if (T // CHUNK) % 2 == 0 else 1
    rows = k * CHUNK

    def body(u_ref, v_ref, dy_ref, ln_ref, w_ref, b_ref, e_ref, du_ref, dv_ref, dln_ref, dw_ref, db_ref):
        @pl.when(pl.program_id(0) == 0)
        def _():
            dln_ref[...] = jnp.zeros_like(dln_ref)
            dw_ref[...] = jnp.zeros_like(dw_ref)
            db_ref[...] = jnp.zeros_like(db_ref)

        e_v = e_ref[...]
        fn = lambda u, v, ln, w, b: _sgu_chunk(u, v, ln, w, b, e_v)
        acc = None
        for i in range(k):
            r = slice(CHUNK * i, CHUNK * (i + 1))
            _, vjp = jax.vjp(fn, u_ref[r, :], v_ref[r, :], ln_ref[...], w_ref[...], b_ref[...])
            du, dv, *dpar = vjp(dy_ref[r, :])
            du_ref[r, :] = du.astype(BF16)
            dv_ref[r, :] = dv.astype(BF16)
            acc = dpar if acc is None else [a + d for a, d in zip(acc, dpar)]
        dln_ref[...] += acc[0]
        dw_ref[...] += acc[1]
        db_ref[...] += acc[2]

    c_ln = pl.BlockSpec((SUBLANE, SGU_W), lambda i: (0, 0))
    c_w = pl.BlockSpec((SGU_GROUPS, CHUNK, CHUNK), lambda i: (0, 0, 0))
    c_b = pl.BlockSpec((CHUNK, SUBLANE), lambda i: (0, 0))
    return pl.pallas_call(
        body, name=name, grid=(T // rows,),
        in_specs=[pl.BlockSpec((rows, SGU_W), lambda i: (i, U0 // SGU_W)),
                  pl.BlockSpec((rows, SGU_W), lambda i: (i, VS0 // SGU_W)),
                  pl.BlockSpec((rows, SGU_W), lambda i: (i, ycol)),
                  c_ln, c_w, c_b, pl.BlockSpec((SUBLANE, SGU_W), lambda i: (0, 0))],
        out_specs=[pl.BlockSpec((rows, SGU_W), lambda i: (i, 0)),
                   pl.BlockSpec((rows, SGU_W), lambda i: (i, 0)), c_ln, c_w, c_b],
        out_shape=[jax.ShapeDtypeStruct((T, SGU_W), BF16), jax.ShapeDtypeStruct((T, SGU_W), BF16),
                   jax.ShapeDtypeStruct((SUBLANE, SGU_W), F32),
                   jax.ShapeDtypeStruct((SGU_GROUPS, CHUNK, CHUNK), F32),
                   jax.ShapeDtypeStruct((CHUNK, SUBLANE), F32)],
        compiler_params=_cp(("arbitrary",)),
    )(proj, proj, dy, ln, w, bst, e4)


_HBM = pl.BlockSpec(memory_space=pltpu.HBM)
_SEM = pl.BlockSpec(memory_space=pltpu.SEMAPHORE)
_ANY = pl.BlockSpec(memory_space=pl.ANY)
_EFFECT = pltpu.SideEffectType.DATAFLOW_SIDE_EFFECTING


def _peers():
    x, y, c = lax.axis_index("x"), lax.axis_index("y"), lax.axis_index("c")
    out = []
    for p in range(1, N_DEV):
        px, py, pc = x ^ ((p >> 2) & 1), y ^ ((p >> 1) & 1), c ^ (p & 1)
        out.append(((px, py, pc), 4 * px + 2 * py + pc))
    return 4 * x + 2 * y + c, out


def _xchg_start(xs, a2a, order, name):
    n = len(xs)
    lands = [lax.empty(a.shape if f else (N_DEV,) + a.shape, a.dtype) for a, f in zip(xs, a2a)]

    def body(*refs):
        ins, zones = refs[:n], refs[n:2 * n]
        send_sems, recv_sems = refs[2 * n + 1], refs[2 * n + 2]
        token = refs[-1]
        me, peers = _peers()
        for p, (dev, peer) in enumerate(peers):
            for t in range(n):
                pltpu.make_async_remote_copy(
                    src_ref=ins[t].at[peer] if a2a[t] else ins[t], dst_ref=zones[t].at[me],
                    send_sem=send_sems.at[p * n + t], recv_sem=recv_sems.at[p * n + t],
                    device_id=dev, device_id_type=MESH).start()
        token[...] = jnp.zeros_like(token)

    hbm = lambda a: pltpu.HBM(a.shape, a.dtype)
    sems = pltpu.SemaphoreType.DMA(((N_DEV - 1) * n,))
    out = pl.pallas_call(
        body, name=name,
        in_specs=[_HBM] * (2 * n) + [_ANY],
        out_specs=[_SEM, _SEM] + [_HBM] * (2 * n) + [pl.BlockSpec(memory_space=pltpu.VMEM)],
        out_shape=[sems, sems] + [hbm(a) for a in xs] + [hbm(a) for a in lands]
        + [jax.ShapeDtypeStruct((SUBLANE, LANE), F32)],
        input_output_aliases={t: 2 + t for t in range(2 * n)},
        compiler_params=pltpu.CompilerParams(has_side_effects=_EFFECT),
    )(*[pltpu.with_memory_space_constraint(a, pltpu.HBM) for a in list(xs) + list(lands)], order)
    return out[0], out[1], out[2:2 + n], out[2 + n:2 + 2 * n], out[-1]


def _xchg_wait(started, a2a, after, name):
    send_sems, recv_sems, xs, lands, _ = started
    n = len(xs)

    def body(*refs):
        ins, zones = refs[:n], refs[n:2 * n]
        send_s, recv_s = refs[2 * n], refs[2 * n + 1]
        me, peers = _peers()
        cps = []
        for p, (dev, peer) in enumerate(peers):
            for t in range(n):
                cps.append(pltpu.make_async_remote_copy(
                    src_ref=ins[t].at[peer] if a2a[t] else ins[t], dst_ref=zones[t].at[peer],
                    send_sem=send_s.at[p * n + t], recv_sem=recv_s.at[p * n + t],
                    device_id=dev, device_id_type=MESH))
        for cp in cps:
            cp.wait_recv()
        for cp in cps:
            cp.wait_send()

    hbm = lambda a: pltpu.HBM(a.shape, a.dtype)
    out = pl.pallas_call(
        body, name=name,
        in_specs=[_HBM] * (2 * n) + [_SEM, _SEM, _ANY],
        out_specs=[_HBM] * (2 * n),
        out_shape=[hbm(a) for a in xs] + [hbm(a) for a in lands],
        input_output_aliases={t: t for t in range(2 * n)},
        compiler_params=pltpu.CompilerParams(has_side_effects=_EFFECT),
    )(*xs, *lands, send_sems, recv_sems, after)
    return out[:n], out[n:]


def _chip_peers():
    x, y, c = lax.axis_index("x"), lax.axis_index("y"), lax.axis_index("c")
    chips = [(1 - x, y), (x, 1 - y), (1 - x, 1 - y)]
    slot = lambda px, py, pc: 4 * px + 2 * py + pc
    return (x, y, c), chips, slot


def _gather_start(xs, order, name):
    n = len(xs)
    lands = [lax.empty((N_DEV,) + a.shape, a.dtype) for a in xs]

    def body(*refs):
        ins, zones = refs[:n], refs[n:2 * n]
        send_sems, d2d_sems, ici_sems = refs[2 * n + 1:2 * n + 4]
        token = refs[-1]
        (x, y, c), chips, slot = _chip_peers()
        me = slot(x, y, c)
        for t in range(n):
            pltpu.make_async_copy(ins[t], zones[t].at[me], d2d_sems.at[n + t]).start()
            for j, (px, py) in enumerate(chips):
                pltpu.make_async_remote_copy(
                    src_ref=ins[t], dst_ref=zones[t].at[me], send_sem=send_sems.at[(1 + j) * n + t],
                    recv_sem=ici_sems.at[j * n + t], device_id=(px, py, c), device_id_type=MESH).start()
            pltpu.make_async_remote_copy(
                src_ref=ins[t], dst_ref=zones[t].at[me], send_sem=send_sems.at[t],
                recv_sem=d2d_sems.at[t], device_id=(x, y, 1 - c), device_id_type=MESH).start()
        token[...] = jnp.zeros_like(token)

    hbm = lambda a: pltpu.HBM(a.shape, a.dtype)
    dma = lambda k: pltpu.SemaphoreType.DMA((k,))
    out = pl.pallas_call(
        body, name=name,
        in_specs=[_HBM] * (2 * n) + [_ANY],
        out_specs=[_SEM, _SEM, _SEM] + [_HBM] * (2 * n) + [pl.BlockSpec(memory_space=pltpu.VMEM)],
        out_shape=[dma(4 * n), dma(2 * n), dma(3 * n)] + [hbm(a) for a in xs] + [hbm(a) for a in lands]
        + [jax.ShapeDtypeStruct((SUBLANE, LANE), F32)],
        input_output_aliases={t: 3 + t for t in range(2 * n)},
        compiler_params=pltpu.CompilerParams(has_side_effects=_EFFECT),
    )(*[pltpu.with_memory_space_constraint(a, pltpu.HBM) for a in list(xs) + list(lands)], order)
    return dict(send=out[0], d2d=out[1], ici=out[2], xs=out[3:3 + n], lands=out[3 + n:3 + 2 * n], token=out[-1])


def _gather_relay(st, after, name):
    n = len(st["xs"])

    def body(*refs):
        zones, ici_sems = refs[:n], refs[n]
        fsend, frecv = refs[n + 2], refs[n + 3]
        token = refs[-1]
        (x, y, c), chips, slot = _chip_peers()
        for t in range(n):
            for j, (px, py) in enumerate(chips):
                blk = zones[t].at[slot(px, py, c)]
                fwd = pltpu.make_async_remote_copy(
                    src_ref=blk, dst_ref=blk, send_sem=fsend.at[j * n + t], recv_sem=ici_sems.at[j * n + t],
                    device_id=(x, y, 1 - c), device_id_type=MESH)
                fwd.wait_recv()
                pltpu.make_async_remote_copy(
                    src_ref=blk, dst_ref=blk, send_sem=fsend.at[j * n + t], recv_sem=frecv.at[j * n + t],
                    device_id=(x, y, 1 - c), device_id_type=MESH).start()
        token[...] = jnp.zeros_like(token)

    hbm = lambda a: pltpu.HBM(a.shape, a.dtype)
    dma = lambda k: pltpu.SemaphoreType.DMA((k,))
    out = pl.pallas_call(
        body, name=name,
        in_specs=[_HBM] * n + [_SEM, _ANY],
        out_specs=[_SEM, _SEM] + [_HBM] * n + [pl.BlockSpec(memory_space=pltpu.VMEM)],
        out_shape=[dma(3 * n), dma(3 * n)] + [hbm(a) for a in st["lands"]]
        + [jax.ShapeDtypeStruct((SUBLANE, LANE), F32)],
        input_output_aliases={t: 2 + t for t in range(n)},
        compiler_params=pltpu.CompilerParams(has_side_effects=_EFFECT),
    )(*st["lands"], st["ici"], after)
    return dict(st, fsend=out[0], frecv=out[1], lands=out[2:2 + n], token=out[-1])


def _gather_wait(st, after, name):
    n = len(st["xs"])

    def body(*refs):
        ins, zones = refs[:n], refs[n:2 * n]
        send_sems, d2d_sems, fsend, frecv = refs[2 * n:2 * n + 4]
        (x, y, c), chips, slot = _chip_peers()
        sib = (x, y, 1 - c)
        for t in range(n):
            pltpu.make_async_copy(ins[t], zones[t].at[slot(x, y, c)], d2d_sems.at[n + t]).wait()
            mine = lambda s, r, dst: pltpu.make_async_remote_copy(
                src_ref=ins[t], dst_ref=dst, send_sem=s, recv_sem=r, device_id=sib, device_id_type=MESH)
            direct = mine(send_sems.at[t], d2d_sems.at[t], zones[t].at[slot(x, y, 1 - c)])
            direct.wait_recv()
            direct.wait_send()
            for j, (px, py) in enumerate(chips):
                mine(send_sems.at[(1 + j) * n + t], d2d_sems.at[t], zones[t].at[slot(px, py, c)]).wait_send()
                relayed = mine(fsend.at[j * n + t], frecv.at[j * n + t], zones[t].at[slot(px, py, 1 - c)])
                relayed.wait_recv()
                relayed.wait_send()

    hbm = lambda a: pltpu.HBM(a.shape, a.dtype)
    out = pl.pallas_call(
        body, name=name,
        in_specs=[_HBM] * (2 * n) + [_SEM] * 4 + [_ANY],
        out_specs=[_HBM] * (2 * n),
        out_shape=[hbm(a) for a in st["xs"]] + [hbm(a) for a in st["lands"]],
        input_output_aliases={t: t for t in range(2 * n)},
        compiler_params=pltpu.CompilerParams(has_side_effects=_EFFECT),
    )(*st["xs"], *st["lands"], st["send"], st["d2d"], st["fsend"], st["frecv"], after)
    return out[:n], out[n:]


def _cast_layers(pairs, name):
    def body(*refs):
        n = len(refs) // 2
        for i in range(n):
            refs[n + i][...] = refs[i][...].astype(BF16)

    in_specs = [pl.BlockSpec((None,) + w.shape[1:], functools.partial(lambda l, i: (l, 0, 0), l),
                             pipeline_mode=pl.Buffered(1)) for w, l in pairs]
    return pl.pallas_call(
        body, name=name, grid=(1,), in_specs=in_specs,
        out_specs=[pl.BlockSpec(w.shape[1:], lambda i: (0, 0)) for w, _ in pairs],
        out_shape=[jax.ShapeDtypeStruct(w.shape[1:], BF16) for w, _ in pairs],
        compiler_params=_cp(("arbitrary",)),
    )(*[w for w, _ in pairs])


ADAMW_BLOCK_ELEMS = 256 * 1024


def _adam_step(me, w, m, v, parts_ref, mine, out_refs):
    g = None
    for p in range(N_DEV):
        term = jnp.where(me == p, mine.astype(F32), parts_ref[p].astype(F32))
        g = term if g is None else g + term
    mn = ADAM_B1 * m + (1.0 - ADAM_B1) * g
    vn = ADAM_B2 * v + (1.0 - ADAM_B2) * (g * g)
    m_hat = mn / (1.0 - ADAM_B1 ** ADAM_STEP)
    v_hat = vn / (1.0 - ADAM_B2 ** ADAM_STEP)
    g_ref, d_ref, mo_ref, vo_ref = out_refs
    g_ref[...] = g
    d_ref[...] = -ADAM_LR * (m_hat / (jnp.sqrt(v_hat) + ADAM_EPS) + ADAM_WD * w)
    mo_ref[...] = mn
    vo_ref[...] = vn


def _adamw(me, w, m, v, parts, own, name, layer=0, into=None):
    L, R, C = w.shape
    P = parts.shape[0]
    tr = R
    t = 16
    while t <= R:
        if R % t == 0 and t * C <= ADAMW_BLOCK_ELEMS:
            tr = t
        t += 16
    if tr == R and R * C > ADAMW_BLOCK_ELEMS and R % 16 == 0:
        tr = 16
    own_all = own.shape[0] == P

    def body(me_ref, w_ref, m_ref, v_ref, p_ref, own_ref, *rest):
        _adam_step(me_ref[0], w_ref[...], m_ref[...], v_ref[...], p_ref, own_ref[...], rest[-4:])

    blk = pl.BlockSpec((None, tr, C), lambda i, me_ref: (layer, i, 0))
    own_blk = pl.BlockSpec((None, tr, C), lambda i, me_ref: (me_ref[0] if own_all else 0, i, 0))
    prev = list(into) if into is not None else []
    return pl.pallas_call(
        body, name=name,
        grid_spec=pltpu.PrefetchScalarGridSpec(
            num_scalar_prefetch=1, grid=(R // tr,),
            in_specs=[blk, blk, blk, pl.BlockSpec((P, tr, C), lambda i, me_ref: (0, i, 0)), own_blk]
            + [_ANY] * len(prev),
            out_specs=[blk] * 4),
        out_shape=[jax.ShapeDtypeStruct((L, R, C), F32)] * 4,
        input_output_aliases={6 + i: i for i in range(len(prev))},
        compiler_params=_cp(("parallel",)),
    )(me, w, m, v, parts, own, *prev)


def _perm_cols(w):
    pad = jnp.zeros(w.shape[:-1] + (LANE - SSD_HEADS,), w.dtype)
    return jnp.concatenate([w[..., 0:1536], w[..., 2438:2694], w[..., 1536:2432], w[..., 2432:2438], pad,
                            w[..., 2694:2950]], axis=-1)


def _unperm_cols(w):
    return jnp.concatenate([w[..., 0:1536], w[..., XBC0:XBC0 + SSD_CONV_DIM], w[..., DT0:DT0 + SSD_HEADS],
                            w[..., U0:U0 + SGU_W], w[..., VS0:VS0 + SGU_W]], axis=-1)


_SMALL = ("ffn1_norm", "mix_norm", "conv_w", "conv_b", "dt_bias", "a_log", "d_skip", "ssd_norm",
          "sgu_ln_g", "sgu_ln_b", "sgu_w", "sgu_b", "ffn2_norm", "final_norm", "loss")


_SMALL_LAST = ("ffn1_norm",)
_SMALL_EARLY = tuple(k for k in _SMALL if k not in _SMALL_LAST)


def _pack(d, names):
    v = jnp.concatenate([d[k].astype(F32).reshape(-1) for k in names])
    n = v.shape[0]
    npad = -(-n // (LANE * 16)) * (LANE * 16)
    return jnp.pad(v, (0, npad - n)).reshape(npad // LANE, LANE)


def _unpack(p, shapes, names):
    v = p.reshape(-1)
    out, o = {}, 0
    for k in names:
        n = int(np.prod(shapes[k]))
        out[k] = v[o:o + n].reshape(shapes[k])
        o += n
    return out


def kernel(x, ffn1_norm, ffn1_w_gate, ffn1_w_up, ffn1_w_down, mix_norm, w_in, conv_w, conv_b, dt_bias, a_log, d_skip, ssd_norm, sgu_ln_g, sgu_ln_b, sgu_w, sgu_b, w_out, ffn2_norm, ffn2_w_gate, ffn2_w_up, ffn2_w_down, final_norm, loss_target, m_ffn1_norm, m_ffn1_w_gate, m_ffn1_w_up, m_ffn1_w_down, m_mix_norm, m_w_in, m_conv_w, m_conv_b, m_dt_bias, m_a_log, m_d_skip, m_ssd_norm, m_sgu_ln_g, m_sgu_ln_b, m_sgu_w, m_sgu_b, m_w_out, m_ffn2_norm, m_ffn2_w_gate, m_ffn2_w_up, m_ffn2_w_down, m_final_norm, v_ffn1_norm, v_ffn1_w_gate, v_ffn1_w_up, v_ffn1_w_down, v_mix_norm, v_w_in, v_conv_w, v_conv_b, v_dt_bias, v_a_log, v_d_skip, v_ssd_norm, v_sgu_ln_g, v_sgu_ln_b, v_sgu_w, v_sgu_b, v_w_out, v_ffn2_norm, v_ffn2_w_gate, v_ffn2_w_up, v_ffn2_w_down, v_final_norm):
    B, S, D = x.shape
    T = B * S
    L = ffn1_norm.shape[0]
    me = 4 * lax.axis_index("x") + 2 * lax.axis_index("y") + lax.axis_index("c")
    cs = conv_w.shape[2]
    W = dict(ffn1_norm=ffn1_norm, ffn1_w_gate=ffn1_w_gate, ffn1_w_up=ffn1_w_up, ffn1_w_down=ffn1_w_down,
             mix_norm=mix_norm, w_in=w_in, conv_w=conv_w, conv_b=conv_b, dt_bias=dt_bias, a_log=a_log,
             d_skip=d_skip, ssd_norm=ssd_norm, sgu_ln_g=sgu_ln_g, sgu_ln_b=sgu_ln_b, sgu_w=sgu_w, sgu_b=sgu_b,
             w_out=w_out, ffn2_norm=ffn2_norm, ffn2_w_gate=ffn2_w_gate, ffn2_w_up=ffn2_w_up,
             ffn2_w_down=ffn2_w_down, final_norm=final_norm)
    M = dict(ffn1_norm=m_ffn1_norm, ffn1_w_gate=m_ffn1_w_gate, ffn1_w_up=m_ffn1_w_up, ffn1_w_down=m_ffn1_w_down,
             mix_norm=m_mix_norm, w_in=m_w_in, conv_w=m_conv_w, conv_b=m_conv_b, dt_bias=m_dt_bias, a_log=m_a_log,
             d_skip=m_d_skip, ssd_norm=m_ssd_norm, sgu_ln_g=m_sgu_ln_g, sgu_ln_b=m_sgu_ln_b, sgu_w=m_sgu_w,
             sgu_b=m_sgu_b, w_out=m_w_out, ffn2_norm=m_ffn2_norm, ffn2_w_gate=m_ffn2_w_gate,
             ffn2_w_up=m_ffn2_w_up, ffn2_w_down=m_ffn2_w_down, final_norm=m_final_norm)
    V = dict(ffn1_norm=v_ffn1_norm, ffn1_w_gate=v_ffn1_w_gate, ffn1_w_up=v_ffn1_w_up, ffn1_w_down=v_ffn1_w_down,
             mix_norm=v_mix_norm, w_in=v_w_in, conv_w=v_conv_w, conv_b=v_conv_b, dt_bias=v_dt_bias, a_log=v_a_log,
             d_skip=v_d_skip, ssd_norm=v_ssd_norm, sgu_ln_g=v_sgu_ln_g, sgu_ln_b=v_sgu_ln_b, sgu_w=v_sgu_w,
             sgu_b=v_sgu_b, w_out=v_w_out, ffn2_norm=v_ffn2_norm, ffn2_w_gate=v_ffn2_w_gate,
             ffn2_w_up=v_ffn2_w_up, ffn2_w_down=v_ffn2_w_down, final_norm=v_final_norm)
    FFN1 = ("ffn1_w_gate", "ffn1_w_up", "ffn1_w_down")
    FFN2 = ("ffn2_w_gate", "ffn2_w_up", "ffn2_w_down")
    MIX = ("w_in", "w_out")
    big = FFN1 + MIX + FFN2
    col_sharded = lambda k: k.endswith("w_gate") or k.endswith("w_up")
    for dct in (W, M, V):
        for k in big:
            if col_sharded(k):
                dct[k] = jnp.swapaxes(dct[k], 1, 2)
        dct["w_in"] = _perm_cols(dct["w_in"])

    wgroups = [[(k, 0) for k in FFN1], [("w_in", 0), ("conv_w", None)], [("w_out", 0)] + [(k, 0) for k in FFN2]]
    for l in range(1, L):
        wgroups += [[(k, l) for k in FFN1] + [("w_in", l)], [("w_out", l)] + [(k, l) for k in FFN2]]
    wstarted, order = [], x
    later = [kl for grp in wgroups[1:] for kl in grp if kl[0] != "conv_w"]
    cast = dict(zip(wgroups[0], _cast_layers([(W[k], l) for k, l in wgroups[0]], "cast_first")))
    for gi, grp in enumerate(wgroups):
        if gi == 1:
            first = lax.optimization_barrier((W[later[0][0]], order))[0]
            srcs = [(first if i == 0 else W[k], l) for i, (k, l) in enumerate(later)]
            cast.update(zip(later, _cast_layers(srcs, "cast_rest")))
        xs = [conv_w if k == "conv_w" else cast[(k, l)] for k, l in grp]
        st = _gather_start(xs, order, f"gather_start_{gi}")
        order = st["token"]
        wstarted.append(st)
    G = {}

    zero1 = jnp.zeros((1,), F32)
    W["loss"], M["loss"], V["loss"] = zero1, zero1, zero1
    full_shapes = {k: (W[k].shape if k != "conv_w" else (L, SSD_CONV, SSD_CONV_DIM)) for k in _SMALL}
    embed = lambda a, k: a if k != "conv_w" else lax.dynamic_update_slice(
        jnp.zeros(full_shapes[k], F32), a, (0, 0, me * cs))
    small_packs = {names: [_pack({k: embed(d[k], k) for k in names}, names)[None] for d in (W, M, V)]
                   for names in (_SMALL_EARLY, _SMALL_LAST)}

    def relay(gi, after):
        wstarted[gi] = _gather_relay(wstarted[gi], after, f"gather_relay_{gi}")
        return wstarted[gi]["token"]

    def gathered(gi, after):
        _, lands = _gather_wait(wstarted[gi], after, f"gather_wait_{gi}")
        G.update(zip(wgroups[gi], lands))

    def rows(k, l):
        a = G[(k, l)]
        return a.reshape(-1, a.shape[-1])

    bias = _attn_bias(S, min(256, S))
    row1 = lambda a: a.reshape(1, -1)

    def ffn1_params(l):
        return dict(g1=row1(ffn1_norm[l]), wg1=rows("ffn1_w_gate", l), wu1=rows("ffn1_w_up", l),
                    wd1=rows("ffn1_w_down", l))

    def out_params(l):
        return dict(wout=rows("w_out", l), g2=row1(ffn2_norm[l]), wg2=rows("ffn2_w_gate", l),
                    wu2=rows("ffn2_w_up", l), wd2=rows("ffn2_w_down", l))

    def mix_params(l):
        cw = jnp.transpose(G[("conv_w", None)][:, l], (1, 0, 2)).reshape(SSD_CONV, -1)
        return dict(
            gm=row1(mix_norm[l]), win=rows("w_in", l),
            cw=jnp.pad(cw, ((0, SUBLANE - SSD_CONV), (0, 0))), cb=row1(conv_b[l]),
            par=jnp.pad(jnp.stack([jnp.repeat(dt_bias[l], HEAD_DIM), jnp.repeat(a_log[l], HEAD_DIM),
                                   jnp.repeat(d_skip[l], HEAD_DIM), ssd_norm[l]]), ((0, SUBLANE - 4), (0, 0))),
            ln=jnp.pad(jnp.stack([sgu_ln_g[l], sgu_ln_b[l]]), ((0, SUBLANE - 2), (0, 0))),
            sw=sgu_w[l], bst=jnp.pad(sgu_b[l].T, ((0, 0), (0, SUBLANE - SGU_GROUPS))))

    xc = x.reshape(T, D)
    saved, lay = [], []
    for l in range(L):
        if l == 0:
            gathered(0, relay(0, order))
        else:
            gathered(1 + 2 * l, xc)
        p = ffn1_params(l)
        x1, *ffn1_saved = _ffn_fwd(xc, p["g1"], p["wg1"], p["wu1"], p["wd1"], f"ffn1_fwd_{l}")
        if l == 0:
            gathered(1, relay(1, x1))
        p.update(mix_params(l))
        lay.append(p)
        proj, ht = _norm_mm(x1, p["gm"], p["win"], f"in_proj_{l}")
        o_att, lse = _attn_fwd(proj, bias, B, S, f"attn_fwd_{l}")
        tok = relay(2 + 2 * l, o_att)
        pre = _conv_fwd(proj, p["cw"], p["cb"], B, S, f"conv_fwd_{l}", after=tok)
        y_ssd, sall = _ssd_fwd(pre, proj, p["par"], B, S, f"ssd_fwd_{l}")
        y_sgu = _sgu_fwd(proj, p["ln"], p["sw"], p["bst"], B, S, f"sgu_fwd_{l}", after=tok)
        ycat = jnp.concatenate([o_att.astype(BF16), y_ssd, y_sgu], axis=1)
        gathered(2 + 2 * l, ycat)
        p.update(out_params(l))
        x2 = _mm(ycat, p["wout"], "nn", f"out_proj_{l}", residual=x1)
        tok = relay(3 + 2 * l, x2) if l + 1 < L else None
        x3, *ffn2_saved = _ffn_fwd(x2, p["g2"], p["wg2"], p["wu2"], p["wd2"], f"ffn2_fwd_{l}", after=tok)
        saved.append(dict(x0=xc, ffn1=ffn1_saved, x1=x1, ht=ht, proj=proj, o_att=o_att, lse=lse, pre=pre,
                          sall=sall, ycat=ycat, x2=x2, ffn2=ffn2_saved))
        xc = x3
    loss_part, dx, dgf = _final_loss(xc, row1(final_norm), loss_target.reshape(T, D), "final_loss")

    gl = [dict() for _ in range(L)]
    gstarted, gorder = [], [order]

    def to_blocks(k, a):
        return a.reshape(N_DEV, -1, a.shape[-1]).astype(BF16)

    def send_grads(keys, l, extra, tag, small_names=None):
        xs = [to_blocks(k, gl[l][k]) for k in keys] + extra
        flags = [True] * len(keys) + [False] * len(extra)
        st = _xchg_start(xs, flags, gorder[0], f"grads_start_{tag}")
        gorder[0] = st[-1]
        gstarted.append((keys, l, st, flags, tag, small_names))

    def small_grads(names):
        sm = {}
        for k in names:
            if k == "final_norm":
                sm[k] = dgf.reshape(-1)
            elif k == "loss":
                sm[k] = loss_part[0, :1]
            else:
                sm[k] = jnp.stack([gl[l][k] for l in range(L)])
        return [_pack(sm, names)]

    def behind(a):
        return lax.optimization_barrier((a, gorder[0]))[0]

    for l in reversed(range(L)):
        p, s, g = lay[l], saved[l], gl[l]
        gfac, ufac, actt = s["ffn2"]
        dx2, dgt, dut, xn, dacc, g["ffn2_norm"] = _ffn_bwd_dx(
            dx, s["x2"], p["g2"], gfac, ufac, p["wg2"], p["wu2"], p["wd2"], f"ffn2_bwd_{l}")
        g["ffn2_w_gate"], g["ffn2_w_up"], g["ffn2_w_down"] = _ffn_dw(dgt, dut, actt, xn, dacc, f"ffn2_dw_{l}")
        if l == 0:
            send_grads(FFN2, 0, [], "l0f")
            dx2 = behind(dx2)
        dycat = _mm(dx2, p["wout"], "nt", f"out_proj_dx_{l}")
        g["w_out"] = _mm(s["ycat"], dx2, "tn", f"out_proj_dw_{l}", out_dtype=BF16, tm_cap=1024, tk_cap=512)
        dq, dk, dv = _attn_bwd(s["proj"], s["o_att"], s["lse"], dycat, bias, B, S, f"attn_bwd_{l}")
        dpre, dz, ddt, dpar = _ssd_bwd(s["pre"], s["proj"], s["sall"], dycat, p["par"], B, S, f"ssd_bwd_{l}")
        dxbc, dwb = _conv_bwd(dpre, s["proj"], p["cw"], B, S, f"conv_bwd_{l}")
        du, dvs, dln, dsw, dbst = _sgu_bwd(s["proj"], dycat, p["ln"], p["sw"], p["bst"], B, S, f"sgu_bwd_{l}")
        hsum = lambda r: r.reshape(SSD_HEADS, HEAD_DIM).sum(-1)
        g["conv_w"], g["conv_b"] = dwb[:SSD_CONV], dwb[SSD_CONV]
        g["dt_bias"], g["a_log"], g["d_skip"], g["ssd_norm"] = hsum(dpar[0]), hsum(dpar[1]), hsum(dpar[2]), dpar[3]
        g["sgu_ln_g"], g["sgu_ln_b"], g["sgu_w"], g["sgu_b"] = dln[0], dln[1], dsw, dbst[:, :SGU_GROUPS].T
        dproj = jnp.concatenate([dq, dk, dv, dz, du, dxbc, ddt, dvs], axis=1)
        g["w_in"] = _mm_resident_lhs(s["ht"], dproj, f"in_proj_dw_{l}")
        dx1, g["mix_norm"] = _norm_mm_bwd(dproj, s["x1"], p["gm"], p["win"], dx2, f"in_proj_bwd_{l}")
        if l == 0:
            send_grads(MIX, 0, small_grads(_SMALL_EARLY), "l0a", _SMALL_EARLY)
            dx1, small_packs = lax.optimization_barrier((behind(dx1), small_packs))
        gfac, ufac, actt = s["ffn1"]
        dx, dgt, dut, xn, dacc, g["ffn1_norm"] = _ffn_bwd_dx(
            dx1, s["x0"], p["g1"], gfac, ufac, p["wg1"], p["wu1"], p["wd1"], f"ffn1_bwd_{l}")
        if l > 0:
            g["ffn1_w_gate"], g["ffn1_w_up"], g["ffn1_w_down"] = _ffn_dw(dgt, dut, actt, xn, dacc,
                                                                        f"ffn1_dw_{l}")
            send_grads(big, l, [], f"l{l}")
            dx = behind(dx)
        else:
            g["ffn1_w_gate"] = _dw_one(dgt, xn, "ffn1_dwg_0")
            send_grads(("ffn1_w_gate",), 0, [], "l0b1")
            g["ffn1_w_up"] = _dw_one(dut, xn, "ffn1_dwu_0", after=gorder[0])
            send_grads(("ffn1_w_up",), 0, [], "l0b2")
            g["ffn1_w_down"] = _dw_one(actt, dacc, "ffn1_dwd_0", after=gorder[0])
    grad_x = dx.reshape(B, S, D)
    send_grads(("ffn1_w_down",), 0, small_grads(_SMALL_LAST), "l0b", _SMALL_LAST)

    res, after = {}, gorder[0]
    small_out = [dict() for _ in range(4)]
    me1 = me.reshape(1).astype(jnp.int32)
    for keys, l, st, flags, tag, names in gstarted:
        own, lands = _xchg_wait(st, flags, after, f"grads_wait_{tag}")
        for k, mine, pk in zip(keys, own, lands):
            res[k] = _adamw(me1, W[k], M[k], V[k], pk, mine, f"adamw_{k}_{l}", layer=l, into=res.get(k))
        done = [res[k][0] for k in keys]
        if names:
            outs = _adamw(me1, *small_packs[names], lands[-1], own[-1][None], f"adamw_small_{tag}")
            for d, o in zip(small_out, outs):
                u = _unpack(o, full_shapes, names)
                if "conv_w" in u:
                    u["conv_w"] = lax.dynamic_slice(u["conv_w"], (0, 0, me * cs), (L, SSD_CONV, cs))
                d.update(u)
                done.extend(u.values())
        after = lax.optimization_barrier(tuple(done))[0]
    back = lambda k, a: jnp.swapaxes(a, 1, 2) if col_sharded(k) else _unperm_cols(a) if k == "w_in" else a
    grads, deltas, new_m, new_v = [dict({k: back(k, res[k][i]) for k in big}, **small_out[i]) for i in range(4)]

    names = ("ffn1_norm", "ffn1_w_gate", "ffn1_w_up", "ffn1_w_down", "mix_norm", "w_in", "conv_w", "conv_b",
             "dt_bias", "a_log", "d_skip", "ssd_norm", "sgu_ln_g", "sgu_ln_b", "sgu_w", "sgu_b", "w_out",
             "ffn2_norm", "ffn2_w_gate", "ffn2_w_up", "ffn2_w_down", "final_norm")
    loss = grads["loss"][0]
    return (loss, grad_x, *[grads[n] for n in names], *[deltas[n] for n in names],
            *[new_m[n] for n in names], *[new_v[n] for n in names])
```

```python
import functools

import numpy as np
import jax
import jax.numpy as jnp
from jax import lax
from jax.experimental import pallas as pl
from jax.experimental.pallas import tpu as pltpu

F32, BF16 = jnp.float32, jnp.bfloat16
HI = lax.Precision.HIGH
MESH = pl.DeviceIdType.MESH
N_DEV = 8
VMEM_LIMIT_BYTES = 56 * 1024 * 1024
LANE, SUBLANE = 128, 8

HEAD_DIM = 64
ATT_W = 384
SSD_W = 384
SSD_HEADS = 6
SSD_STATE = 128
SSD_CONV = 4
CHUNK = 128
SSD_CONV_DIM = 896
SGU_W = 256
SGU_GROUPS = 4
D_IN = 2950
RMS_EPS = 1e-6
LN_EPS = 1e-5
NEG = -1e30

PW = 3072
Q0, K0, V0, Z0, U0, XBC0, DT0, VS0 = 0, 384, 768, 1152, 1536, 1792, 2688, 2816

ADAM_LR, ADAM_B1, ADAM_B2, ADAM_EPS, ADAM_WD, ADAM_STEP = 0.001, 0.9, 0.999, 1e-08, 0.01, 10


def _cp(sem=None):
    return pltpu.CompilerParams(dimension_semantics=sem, vmem_limit_bytes=VMEM_LIMIT_BYTES)


def _tile(n, cap, mult=LANE):
    best = None
    t = mult
    while t <= min(n, cap):
        if n % t == 0:
            best = t
        t += mult
    return best if best is not None else n


def _dot(a, b, prec=None):
    return jnp.dot(a, b, preferred_element_type=F32, precision=prec)


def _dot_nt(a, b, prec=None):
    return lax.dot_general(a, b, (((1,), (1,)), ((), ())), preferred_element_type=F32, precision=prec)


def _dot_tn(a, b, prec=None):
    return lax.dot_general(a, b, (((0,), (0,)), ((), ())), preferred_element_type=F32, precision=prec)


def _sigmoid(x):
    return 1.0 / (1.0 + jnp.exp(-x))


def _silu(x):
    return x * _sigmoid(x)


def _gelu(x):
    return 0.5 * x * (1.0 + lax.erf(x * 0.7071067811865476))


def _softplus(x):
    return jnp.maximum(x, 0.0) + jnp.log(1.0 + jnp.exp(-jnp.abs(x)))


def _rms_fwd(x, g):
    rstd = lax.rsqrt(jnp.mean(x * x, axis=-1, keepdims=True) + RMS_EPS)
    xhat = x * rstd
    return xhat * g, xhat, rstd


def _rms_bwd(dy, xhat, rstd, g):
    dxhat = dy * g
    dx = rstd * (dxhat - xhat * jnp.mean(dxhat * xhat, axis=-1, keepdims=True))
    return dx, dy * xhat


def _resident(shape):
    return pl.BlockSpec(shape, lambda *_: (0,) * len(shape), pipeline_mode=pl.Buffered(1))


def _mm(a, b, mode, name, out_dtype=F32, residual=None, tm_cap=512, tn_cap=1024, tk_cap=1024):
    if mode == "nn":
        (M, K), (_, N) = a.shape, b.shape
    elif mode == "nt":
        (M, K), (N, _) = a.shape, b.shape
    else:
        (K, M), (_, N) = a.shape, b.shape
    tm, tn, tk = _tile(M, tm_cap), _tile(N, tn_cap), _tile(K, tk_cap)
    nk = K // tk
    if mode == "tn":
        a_spec = pl.BlockSpec((tk, tm), lambda i, j, k: (k, i))
    else:
        a_spec = pl.BlockSpec((tm, tk), lambda i, j, k: (i, k))
    if mode == "nt":
        b_spec = pl.BlockSpec((tn, tk), lambda i, j, k: (j, k))
    else:
        b_spec = pl.BlockSpec((tk, tn), lambda i, j, k: (k, j))
    o_spec = pl.BlockSpec((tm, tn), lambda i, j, k: (i, j))
    has_res = residual is not None

    def prod(a_ref, b_ref):
        av = a_ref[...].astype(BF16)
        bv = b_ref[...].astype(BF16)
        if mode == "nn":
            return _dot(av, bv)
        if mode == "nt":
            return _dot_nt(av, bv)
        return _dot_tn(av, bv)

    def body(*refs):
        a_ref, b_ref = refs[:2]
        r_ref = refs[2] if has_res else None
        o_ref = refs[2 + has_res]
        if nk == 1:
            o = prod(a_ref, b_ref)
            if has_res:
                o = r_ref[...] + o
            o_ref[...] = o.astype(out_dtype)
            return
        acc = refs[3 + has_res]
        k = pl.program_id(2)

        @pl.when(k == 0)
        def _():
            acc[...] = jnp.zeros_like(acc)

        acc[...] += prod(a_ref, b_ref)

        @pl.when(k == nk - 1)
        def _():
            o = acc[...]
            if has_res:
                o = r_ref[...] + o
            o_ref[...] = o.astype(out_dtype)

    ins = [a, b] + ([residual] if has_res else [])
    in_specs = [a_spec, b_spec] + ([o_spec] if has_res else [])
    return pl.pallas_call(
        body, name=name, grid=(M // tm, N // tn, nk),
        in_specs=in_specs, out_specs=o_spec,
        out_shape=jax.ShapeDtypeStruct((M, N), out_dtype),
        scratch_shapes=[pltpu.VMEM((tm, tn), F32)] if nk > 1 else [],
        compiler_params=_cp(("parallel", "parallel", "arbitrary")),
    )(*ins)


def _after(after):
    return ([after], [_ANY]) if after is not None else ([], [])


def _ffn_fwd(x, g, wgt, wut, wd, name, after=None):
    T, D = x.shape
    F = wgt.shape[0]
    tm = _tile(T, 256)

    def body(x_ref, g_ref, wg_ref, wu_ref, wd_ref, *rest):
        out_ref, dgf_ref, sl_ref, actt_ref = rest[-4:]
        xv = x_ref[...]
        xn = _rms_fwd(xv, g_ref[...])[0].astype(BF16)
        gate = _dot_nt(xn, wg_ref[...])
        up = _dot_nt(xn, wu_ref[...])
        sig = _sigmoid(gate)
        sl = gate * sig
        dgf_ref[...] = (up * (sig * (1.0 + gate * (1.0 - sig)))).astype(BF16)
        sl_ref[...] = sl.astype(BF16)
        act = (sl * up).astype(BF16)
        actt_ref[...] = act.T
        out_ref[...] = xv + 0.5 * _dot(act, wd_ref[...])

    row = lambda w: pl.BlockSpec((tm, w), lambda i: (i, 0))
    extra, extra_specs = _after(after)
    return pl.pallas_call(
        body, name=name, grid=(T // tm,),
        in_specs=[row(D), _resident((1, D)), _resident((F, D)), _resident((F, D)), _resident((F, D))] + extra_specs,
        out_specs=[row(D), row(F), row(F), pl.BlockSpec((F, tm), lambda i: (0, i))],
        out_shape=[jax.ShapeDtypeStruct((T, D), F32),
                   jax.ShapeDtypeStruct((T, F), BF16),
                   jax.ShapeDtypeStruct((T, F), BF16),
                   jax.ShapeDtypeStruct((F, T), BF16)],
        compiler_params=_cp(("parallel",)),
    )(x, g, wgt, wut, wd, *extra)


def _ffn_bwd_dx(dout, x, g, dgf, sl, wg, wu, wd, name):
    T, D = x.shape
    F = wg.shape[0]
    tm = _tile(T, 256)

    def body(dout_ref, x_ref, g_ref, dgf_ref, sl_ref, wg_ref, wu_ref, wd_ref,
             dx_ref, dgt_ref, dut_ref, xn_ref, dacc_ref, dg_ref):
        @pl.when(pl.program_id(0) == 0)
        def _():
            dg_ref[...] = jnp.zeros_like(dg_ref)

        gv = g_ref[...]
        dout_v = dout_ref[...]
        xn, xhat, rstd = _rms_fwd(x_ref[...], gv)
        xn_ref[...] = xn.astype(BF16)
        dacc = (0.5 * dout_v).astype(BF16)
        dacc_ref[...] = dacc
        dact = _dot_nt(dacc, wd_ref[...])
        dgate = (dact * dgf_ref[...].astype(F32)).astype(BF16)
        dup = (dact * sl_ref[...].astype(F32)).astype(BF16)
        dgt_ref[...] = dgate.T
        dut_ref[...] = dup.T
        dxn = _dot(dgate, wg_ref[...]) + _dot(dup, wu_ref[...])
        dx, dgrow = _rms_bwd(dxn, xhat, rstd, gv)
        dx_ref[...] = dout_v + dx
        dg_ref[...] += jnp.sum(dgrow, axis=0, keepdims=True)

    row = lambda w: pl.BlockSpec((tm, w), lambda i: (i, 0))
    tr = pl.BlockSpec((F, tm), lambda i: (0, i))
    return pl.pallas_call(
        body, name=name, grid=(T // tm,),
        in_specs=[row(D), row(D), _resident((1, D)), row(F), row(F),
                  _resident((F, D)), _resident((F, D)), _resident((F, D))],
        out_specs=[row(D), tr, tr, row(D), row(D), pl.BlockSpec((1, D), lambda i: (0, 0))],
        out_shape=[jax.ShapeDtypeStruct((T, D), F32)] + [jax.ShapeDtypeStruct((F, T), BF16)] * 2
        + [jax.ShapeDtypeStruct((T, D), BF16)] * 2 + [jax.ShapeDtypeStruct((1, D), F32)],
        compiler_params=_cp(("arbitrary",)),
    )(dout, x, g, dgf, sl, wg, wu, wd)


def _ffn_dw(dgt, dut, actt, xn, dacc, name):
    F, T = dgt.shape
    D = xn.shape[1]
    th = _tile(F, 256)

    def body(dg_ref, du_ref, a_ref, xn_ref, dacc_ref, dwg_ref, dwu_ref, dwd_ref):
        xv = xn_ref[...]
        dwg_ref[...] = _dot(dg_ref[...], xv).astype(BF16)
        dwu_ref[...] = _dot(du_ref[...], xv).astype(BF16)
        dwd_ref[...] = _dot(a_ref[...], dacc_ref[...]).astype(BF16)

    tile = pl.BlockSpec((th, T), lambda j: (j, 0))
    out = pl.BlockSpec((th, D), lambda j: (j, 0))
    return pl.pallas_call(
        body, name=name, grid=(F // th,),
        in_specs=[tile, tile, tile, _resident((T, D)), _resident((T, D))],
        out_specs=[out, out, out], out_shape=[jax.ShapeDtypeStruct((F, D), BF16)] * 3,
        compiler_params=_cp(("parallel",)),
    )(dgt, dut, actt, xn, dacc)


def _dw_one(lt, r, name, after=None):
    F, T = lt.shape
    D = r.shape[1]
    th = _tile(F, 256)
    extra, extra_specs = _after(after)

    def body(l_ref, r_ref, *rest):
        rest[-1][...] = _dot(l_ref[...], r_ref[...]).astype(BF16)

    return pl.pallas_call(
        body, name=name, grid=(F // th,),
        in_specs=[pl.BlockSpec((th, T), lambda j: (j, 0)), _resident((T, D))] + extra_specs,
        out_specs=pl.BlockSpec((th, D), lambda j: (j, 0)),
        out_shape=jax.ShapeDtypeStruct((F, D), BF16),
        compiler_params=_cp(("parallel",)),
    )(lt, r, *extra)


def _norm_mm(x, g, w, name):
    T, D = x.shape
    N = w.shape[1]
    tm = _tile(T, 512)

    def body(x_ref, g_ref, w_ref, o_ref, ht_ref):
        xn = _rms_fwd(x_ref[...], g_ref[...])[0]
        ht_ref[...] = xn.T.astype(BF16)
        o_ref[...] = _dot(xn.astype(BF16), w_ref[...])

    return pl.pallas_call(
        body, name=name, grid=(T // tm,),
        in_specs=[pl.BlockSpec((tm, D), lambda i: (i, 0)), _resident((1, D)), _resident((D, N))],
        out_specs=[pl.BlockSpec((tm, N), lambda i: (i, 0)), pl.BlockSpec((D, tm), lambda i: (0, i))],
        out_shape=[jax.ShapeDtypeStruct((T, N), F32), jax.ShapeDtypeStruct((D, T), BF16)],
        compiler_params=_cp(("parallel",)),
    )(x, g, w)


def _norm_mm_bwd(dproj, x, g, w, dres, name):
    T, D = x.shape
    N = w.shape[1]
    tm = _tile(T, 512)

    def body(dp_ref, x_ref, g_ref, w_ref, dres_ref, dx_ref, dg_ref):
        @pl.when(pl.program_id(0) == 0)
        def _():
            dg_ref[...] = jnp.zeros_like(dg_ref)

        gv = g_ref[...]
        dh = _dot_nt(dp_ref[...], w_ref[...])
        _, xhat, rstd = _rms_fwd(x_ref[...], gv)
        dx, dgrow = _rms_bwd(dh, xhat, rstd, gv)
        dx_ref[...] = dres_ref[...] + dx
        dg_ref[...] += jnp.sum(dgrow, axis=0, keepdims=True)

    row = pl.BlockSpec((tm, D), lambda i: (i, 0))
    one = pl.BlockSpec((1, D), lambda i: (0, 0))
    return pl.pallas_call(
        body, name=name, grid=(T // tm,),
        in_specs=[pl.BlockSpec((tm, N), lambda i: (i, 0)), row, _resident((1, D)), _resident((D, N)), row],
        out_specs=[row, one],
        out_shape=[jax.ShapeDtypeStruct((T, D), F32), jax.ShapeDtypeStruct((1, D), F32)],
        compiler_params=_cp(("arbitrary",)),
    )(dproj, x, g, w, dres)


def _mm_resident_lhs(at, b, name, tn_cap=512):
    M, K = at.shape
    N = b.shape[1]
    tn = _tile(N, tn_cap)

    def body(a_ref, b_ref, o_ref):
        o_ref[...] = _dot(a_ref[...], b_ref[...]).astype(BF16)

    return pl.pallas_call(
        body, name=name, grid=(N // tn,),
        in_specs=[_resident((M, K)), pl.BlockSpec((K, tn), lambda j: (0, j))],
        out_specs=pl.BlockSpec((M, tn), lambda j: (0, j)),
        out_shape=jax.ShapeDtypeStruct((M, N), BF16),
        compiler_params=_cp(("parallel",)),
    )(at, b)


def _final_loss(x, g, target, name):
    T, D = x.shape
    tm = _tile(T, 512)

    def body(x_ref, g_ref, t_ref, loss_ref, dx_ref, dg_ref):
        @pl.when(pl.program_id(0) == 0)
        def _():
            dg_ref[...] = jnp.zeros_like(dg_ref)
            loss_ref[...] = jnp.zeros_like(loss_ref)

        gv = g_ref[...]
        y, xhat, rstd = _rms_fwd(x_ref[...], gv)
        err = y - t_ref[...]
        part = 0.5 * jnp.sum(jnp.mean(err * err, axis=-1, keepdims=True), axis=0, keepdims=True)
        loss_ref[...] += jnp.broadcast_to(part, loss_ref.shape)
        dy = err * (1.0 / D)
        dx, dgrow = _rms_bwd(dy, xhat, rstd, gv)
        dx_ref[...] = dx
        dg_ref[...] += jnp.sum(dgrow, axis=0, keepdims=True)

    row = pl.BlockSpec((tm, D), lambda i: (i, 0))
    one = pl.BlockSpec((1, D), lambda i: (0, 0))
    return pl.pallas_call(
        body, name=name, grid=(T // tm,),
        in_specs=[row, one, row],
        out_specs=[pl.BlockSpec((1, LANE), lambda i: (0, 0)), row, one],
        out_shape=[jax.ShapeDtypeStruct((1, LANE), F32), jax.ShapeDtypeStruct((T, D), F32),
                   jax.ShapeDtypeStruct((1, D), F32)],
        compiler_params=_cp(("arbitrary",)),
    )(x, g, target)


def _attn_bias(S, bq):
    d = np.arange(bq)[:, None] - np.arange(S)[None, :] + (S // bq - 1) * bq
    ok = d >= 0
    mult = ((ok & (d <= 128)).astype(np.float32) + (ok & (d % 4 == 0) & (d <= 512))
            + (ok & (d % 16 == 0) & (d <= 2048)))
    return jnp.asarray(np.where(mult > 0, np.log(np.maximum(mult, 1.0)), NEG).astype(np.float32))


def _attn_fwd(proj, bias, B, S, name):
    T = B * S
    bq = bias.shape[0]
    nb = S // bq
    qcol, kcol, vcol = Q0 // LANE, K0 // LANE, V0 // LANE

    def body(q_ref, k_ref, v_ref, t_ref, o_ref, lse_ref, ks, vs):
        for hh in range(2):
            sl = slice(HEAD_DIM * hh, HEAD_DIM * (hh + 1))
            ks[hh] = k_ref[:, sl].astype(BF16)
            vs[hh] = v_ref[:, sl].astype(BF16)
        for hh in range(2):
            sl = slice(HEAD_DIM * hh, HEAD_DIM * (hh + 1))
            for qb in range(nb):
                w, off, rows = bq * (qb + 1), (nb - 1 - qb) * bq, slice(qb * bq, (qb + 1) * bq)
                q = (q_ref[rows, sl] * 0.125).astype(BF16)
                s = _dot_nt(q, ks[hh, 0:w, :]) + t_ref[:, off:off + w]
                m = jnp.max(s, axis=-1, keepdims=True)
                p = jnp.exp(s - m)
                l = jnp.sum(p, axis=-1, keepdims=True)
                o_ref[rows, sl] = _dot(p.astype(BF16), vs[hh, 0:w, :]) / l
                lse_ref[rows, hh:hh + 1] = m + jnp.log(l)

    blk = lambda c0: pl.BlockSpec((S, LANE), lambda b, p: (b, c0 + p))
    return pl.pallas_call(
        body, name=name, grid=(B, ATT_W // LANE),
        in_specs=[blk(qcol), blk(kcol), blk(vcol), _resident((bq, S))],
        out_specs=[pl.BlockSpec((S, LANE), lambda b, p: (b, p)),
                   pl.BlockSpec((None, None, S, 2), lambda b, p: (b, p, 0, 0))],
        out_shape=[jax.ShapeDtypeStruct((T, ATT_W), F32),
                   jax.ShapeDtypeStruct((B, ATT_W // LANE, S, 2), F32)],
        scratch_shapes=[pltpu.VMEM((2, S, HEAD_DIM), BF16)] * 2,
        compiler_params=_cp(("parallel", "parallel")),
    )(proj, proj, proj, bias)


def _attn_bwd(proj, o, lse, dy, bias, B, S, name):
    T = B * S
    bq = bias.shape[0]
    nb = S // bq
    qcol, kcol, vcol = Q0 // LANE, K0 // LANE, V0 // LANE

    def body(q_ref, k_ref, v_ref, o_ref, lse_ref, do_ref, t_ref, dq_ref, dk_ref, dv_ref, ks, vs, dks, dvs):
        for hh in range(2):
            sl = slice(HEAD_DIM * hh, HEAD_DIM * (hh + 1))
            ks[hh] = k_ref[:, sl].astype(BF16)
            vs[hh] = v_ref[:, sl].astype(BF16)
        dks[...] = jnp.zeros_like(dks)
        dvs[...] = jnp.zeros_like(dvs)
        for hh in range(2):
            sl = slice(HEAD_DIM * hh, HEAD_DIM * (hh + 1))
            for qb in range(nb):
                w, off, rows = bq * (qb + 1), (nb - 1 - qb) * bq, slice(qb * bq, (qb + 1) * bq)
                q = (q_ref[rows, sl] * 0.125).astype(BF16)
                do = do_ref[rows, sl]
                dob = do.astype(BF16)
                delta = jnp.sum(do * o_ref[rows, sl], axis=-1, keepdims=True)
                k, v = ks[hh, 0:w, :], vs[hh, 0:w, :]
                s = _dot_nt(q, k) + t_ref[:, off:off + w]
                p = jnp.exp(s - lse_ref[rows, hh:hh + 1])
                ds = (p * (_dot_nt(dob, v) - delta)).astype(BF16)
                dq_ref[rows, sl] = (_dot(ds, k) * 0.125).astype(dq_ref.dtype)
                dks[hh, 0:w, :] += _dot_tn(ds, q)
                dvs[hh, 0:w, :] += _dot_tn(p.astype(BF16), dob)
            dk_ref[:, sl] = dks[hh].astype(dk_ref.dtype)
            dv_ref[:, sl] = dvs[hh].astype(dv_ref.dtype)

    blk = lambda c0: pl.BlockSpec((S, LANE), lambda b, p: (b, c0 + p))
    own = pl.BlockSpec((S, LANE), lambda b, p: (b, p))
    return pl.pallas_call(
        body, name=name, grid=(B, ATT_W // LANE),
        in_specs=[blk(qcol), blk(kcol), blk(vcol), own,
                  pl.BlockSpec((None, None, S, 2), lambda b, p: (b, p, 0, 0)), own, _resident((bq, S))],
        out_specs=[own, own, own],
        out_shape=[jax.ShapeDtypeStruct((T, ATT_W), BF16)] * 3,
        scratch_shapes=[pltpu.VMEM((2, S, HEAD_DIM), BF16)] * 2 + [pltpu.VMEM((2, S, HEAD_DIM), F32)] * 2,
        compiler_params=_cp(("parallel", "parallel")),
    )(proj, proj, proj, o, lse, dy, bias)


def _conv_fwd(proj, cw, cb, B, S, name, after=None):
    T = B * S
    nc = SSD_CONV_DIM // LANE
    c0 = XBC0 // LANE
    extra, extra_specs = _after(after)

    def body(x_ref, w_ref, b_ref, *rest):
        o_ref = rest[-1]
        x = x_ref[...]
        t = lax.broadcasted_iota(jnp.int32, (S, 1), 0)
        acc = b_ref[...] + w_ref[SSD_CONV - 1:SSD_CONV, :] * x
        for k in range(SSD_CONV - 1):
            sh = SSD_CONV - 1 - k
            xs = jnp.where(t >= sh, pltpu.roll(x, sh, 0), 0.0)
            acc = acc + w_ref[k:k + 1, :] * xs
        o_ref[...] = acc

    return pl.pallas_call(
        body, name=name, grid=(B, nc),
        in_specs=[pl.BlockSpec((S, LANE), lambda b, j: (b, c0 + j)),
                  pl.BlockSpec((SUBLANE, LANE), lambda b, j: (0, j)),
                  pl.BlockSpec((1, LANE), lambda b, j: (0, j))] + extra_specs,
        out_specs=pl.BlockSpec((S, LANE), lambda b, j: (b, j)),
        out_shape=jax.ShapeDtypeStruct((T, SSD_CONV_DIM), F32),
        compiler_params=_cp(("parallel", "parallel")),
    )(proj, cw, cb, *extra)


def _conv_bwd(dpre, proj, cw, B, S, name):
    T = B * S
    nc = SSD_CONV_DIM // LANE
    c0 = XBC0 // LANE

    def body(d_ref, x_ref, w_ref, dx_ref, dwb_ref):
        @pl.when(pl.program_id(1) == 0)
        def _():
            dwb_ref[...] = jnp.zeros_like(dwb_ref)

        d = d_ref[...]
        x = x_ref[...]
        t = lax.broadcasted_iota(jnp.int32, (S, 1), 0)
        dx = w_ref[SSD_CONV - 1:SSD_CONV, :] * d
        rows = [None] * SUBLANE
        rows[SSD_CONV - 1] = jnp.sum(d * x, axis=0, keepdims=True)
        for k in range(SSD_CONV - 1):
            sh = SSD_CONV - 1 - k
            dx = dx + w_ref[k:k + 1, :] * jnp.where(t < S - sh, pltpu.roll(d, S - sh, 0), 0.0)
            xs = jnp.where(t >= sh, pltpu.roll(x, sh, 0), 0.0)
            rows[k] = jnp.sum(d * xs, axis=0, keepdims=True)
        rows[SSD_CONV] = jnp.sum(d, axis=0, keepdims=True)
        dx_ref[...] = dx.astype(BF16)
        r = lax.broadcasted_iota(jnp.int32, (SUBLANE, LANE), 0)
        upd = jnp.zeros((SUBLANE, LANE), F32)
        for k in range(SSD_CONV + 1):
            upd = upd + jnp.where(r == k, rows[k], 0.0)
        dwb_ref[...] += upd

    return pl.pallas_call(
        body, name=name, grid=(nc, B),
        in_specs=[pl.BlockSpec((S, LANE), lambda j, b: (b, j)),
                  pl.BlockSpec((S, LANE), lambda j, b: (b, c0 + j)),
                  pl.BlockSpec((SUBLANE, LANE), lambda j, b: (0, j))],
        out_specs=[pl.BlockSpec((S, LANE), lambda j, b: (b, j)),
                   pl.BlockSpec((SUBLANE, LANE), lambda j, b: (0, j))],
        out_shape=[jax.ShapeDtypeStruct((T, SSD_CONV_DIM), BF16),
                   jax.ShapeDtypeStruct((SUBLANE, SSD_CONV_DIM), F32)],
        compiler_params=_cp(("parallel", "arbitrary")),
    )(dpre, proj, cw)


def _ssd_consts():
    e = np.zeros((LANE, SSD_W), np.float32)
    p = np.zeros((SUBLANE, SSD_W), np.float32)
    for h in range(SSD_HEADS):
        e[h, HEAD_DIM * h:HEAD_DIM * (h + 1)] = 1.0
        p[h, HEAD_DIM * h] = 1.0
    return jnp.asarray(e), jnp.asarray(p)


def _ssd_chunk(pre, z, dtr, sprev, par, e_mat, psel):
    L = CHUNK
    xc = _silu(pre)
    xs, bm, cm = xc[:, :SSD_W], xc[:, SSD_W:SSD_W + 2 * SSD_STATE], xc[:, SSD_W + 2 * SSD_STATE:]
    dtb, alog, dskip, ng = par[0:1], par[1:2], par[2:3], par[3:4]
    dt = _softplus(_dot(dtr, e_mat, HI) + dtb)
    a = dt * (-jnp.exp(alog))
    X = xs * dt
    ri = lax.broadcasted_iota(jnp.int32, (L, L), 0)
    ci = lax.broadcasted_iota(jnp.int32, (L, L), 1)
    tril = ri >= ci
    acs = _dot(tril.astype(F32), a, HI)
    acs_t = _dot_nt(psel, acs, HI)
    ecs = jnp.exp(acs)
    alast = acs[L - 1:L, :]
    xd = (X * jnp.exp(alast - acs)).astype(BF16)
    xb = X.astype(BF16)
    col = lax.broadcasted_iota(jnp.int32, (1, SSD_W), 1)
    sb = sprev.astype(BF16)
    bgs = [bm[:, SSD_STATE * g:SSD_STATE * (g + 1)].astype(BF16) for g in range(2)]
    cgs = [cm[:, SSD_STATE * g:SSD_STATE * (g + 1)].astype(BF16) for g in range(2)]
    cbs = [_dot_nt(cgs[g], bgs[g]) for g in range(2)]
    first = lax.broadcasted_iota(jnp.int32, (1, LANE), 1) < HEAD_DIM
    y_tiles, s_tiles = [], []
    for t in range(SSD_W // LANE):
        cl = slice(LANE * t, LANE * (t + 1))
        xb_t, xd_t, sb_t = xb[:, cl], xd[:, cl], sb[:, cl]
        per_head = []
        for h in (2 * t, 2 * t + 1):
            seg = acs[:, HEAD_DIM * h:HEAD_DIM * h + 1] - acs_t[h:h + 1, :]
            dec = jnp.exp(jnp.where(tril, seg, NEG))
            per_head.append(_dot((cbs[h // 3] * dec).astype(BF16), xb_t))
        y_t = jnp.where(first, per_head[0], per_head[1])
        ga, gb = (2 * t) // 3, (2 * t + 1) // 3
        if ga == gb:
            y_off, s_add = _dot(cgs[ga], sb_t), _dot_tn(bgs[ga], xd_t)
        else:
            y_off = jnp.where(first, _dot(cgs[ga], sb_t), _dot(cgs[gb], sb_t))
            s_add = jnp.where(first, _dot_tn(bgs[ga], xd_t), _dot_tn(bgs[gb], xd_t))
        y_tiles.append(y_t + y_off * ecs[:, cl])
        s_tiles.append(s_add)
    y = dskip * xs + jnp.concatenate(y_tiles, axis=1)
    snew = sprev * jnp.exp(alast) + jnp.concatenate(s_tiles, axis=1)
    yg = y * _silu(z)
    sq = yg * yg
    g0 = col < SSD_W // 2
    ms0 = jnp.sum(jnp.where(g0, sq, 0.0), axis=-1, keepdims=True) * (2.0 / SSD_W)
    ms1 = jnp.sum(jnp.where(g0, 0.0, sq), axis=-1, keepdims=True) * (2.0 / SSD_W)
    r = jnp.where(g0, lax.rsqrt(ms0 + RMS_EPS), lax.rsqrt(ms1 + RMS_EPS))
    return yg * r * ng, snew


SSD_CHUNKS_PER_STEP = 2


def _ssd_chunks_per_step(S):
    k = SSD_CHUNKS_PER_STEP
    while (S // CHUNK) % k:
        k //= 2
    return k


def _ssd_fwd(pre, proj, par, B, S, name):
    T = B * S
    k = _ssd_chunks_per_step(S)
    nc, rows = S // (CHUNK * k), CHUNK * k
    e_mat, psel = _ssd_consts()

    def body(pre_ref, z_ref, dt_ref, par_ref, e_ref, p_ref, y_ref, sall_ref, st):
        @pl.when(pl.program_id(1) == 0)
        def _():
            st[...] = jnp.zeros_like(st)

        sprev = st[...]
        for i in range(k):
            r = slice(CHUNK * i, CHUNK * (i + 1))
            sall_ref[i] = sprev
            y, sprev = _ssd_chunk(pre_ref[r, :], z_ref[r, :], dt_ref[r, :], sprev, par_ref[...], e_ref[...],
                                  p_ref[...])
            y_ref[r, :] = y.astype(BF16)
        st[...] = sprev

    row = lambda b, c: b * nc + c
    full = lambda shp: pl.BlockSpec(shp, lambda b, c: (0, 0))
    return pl.pallas_call(
        body, name=name, grid=(B, nc),
        in_specs=[pl.BlockSpec((rows, SSD_CONV_DIM), lambda b, c: (row(b, c), 0)),
                  pl.BlockSpec((rows, SSD_W), lambda b, c: (row(b, c), Z0 // SSD_W)),
                  pl.BlockSpec((rows, LANE), lambda b, c: (row(b, c), DT0 // LANE)),
                  full((SUBLANE, SSD_W)), full((LANE, SSD_W)), full((SUBLANE, SSD_W))],
        out_specs=[pl.BlockSpec((rows, SSD_W), lambda b, c: (row(b, c), 0)),
                   pl.BlockSpec((k, SSD_STATE, SSD_W), lambda b, c: (row(b, c), 0, 0))],
        out_shape=[jax.ShapeDtypeStruct((T, SSD_W), BF16),
                   jax.ShapeDtypeStruct((B * nc * k, SSD_STATE, SSD_W), F32)],
        scratch_shapes=[pltpu.VMEM((SSD_STATE, SSD_W), F32)],
        compiler_params=_cp(("parallel", "arbitrary")),
    )(pre, proj, proj, par, e_mat, psel)


def _ssd_bwd(pre, proj, sall, dy, par, B, S, name):
    T = B * S
    k = _ssd_chunks_per_step(S)
    nc, rows = S // (CHUNK * k), CHUNK * k
    e_mat, psel = _ssd_consts()

    def body(pre_ref, z_ref, dt_ref, sall_ref, dy_ref, par_ref, e_ref, p_ref,
             dpre_ref, dz_ref, ddt_ref, dpar_ref, ds):
        b, c = pl.program_id(0), pl.program_id(1)

        @pl.when(c == 0)
        def _():
            ds[...] = jnp.zeros_like(ds)

        @pl.when((b == 0) & (c == 0))
        def _():
            dpar_ref[...] = jnp.zeros_like(dpar_ref)

        e_v, p_v = e_ref[...], p_ref[...]
        fn = lambda pre, z, dtr, sprev, par: _ssd_chunk(pre, z, dtr, sprev, par, e_v, p_v)
        dstate, dpar_sum = ds[...], None
        for i in reversed(range(k)):
            r = slice(CHUNK * i, CHUNK * (i + 1))
            _, vjp = jax.vjp(fn, pre_ref[r, :], z_ref[r, :], dt_ref[r, :], sall_ref[i], par_ref[...])
            dpre, dz, ddt, dstate, dpar = vjp((dy_ref[r, :], dstate))
            dpre_ref[r, :] = dpre
            dz_ref[r, :] = dz.astype(BF16)
            ddt_ref[r, :] = ddt.astype(BF16)
            dpar_sum = dpar if dpar_sum is None else dpar_sum + dpar
        dpar_ref[...] += dpar_sum
        ds[...] = dstate

    row = lambda b, c: b * nc + (nc - 1 - c)
    full = lambda shp: pl.BlockSpec(shp, lambda b, c: (0, 0))
    return pl.pallas_call(
        body, name=name, grid=(B, nc),
        in_specs=[pl.BlockSpec((rows, SSD_CONV_DIM), lambda b, c: (row(b, c), 0)),
                  pl.BlockSpec((rows, SSD_W), lambda b, c: (row(b, c), Z0 // SSD_W)),
                  pl.BlockSpec((rows, LANE), lambda b, c: (row(b, c), DT0 // LANE)),
                  pl.BlockSpec((k, SSD_STATE, SSD_W), lambda b, c: (row(b, c), 0, 0)),
                  pl.BlockSpec((rows, SSD_W), lambda b, c: (row(b, c), ATT_W // SSD_W)),
                  full((SUBLANE, SSD_W)), full((LANE, SSD_W)), full((SUBLANE, SSD_W))],
        out_specs=[pl.BlockSpec((rows, SSD_CONV_DIM), lambda b, c: (row(b, c), 0)),
                   pl.BlockSpec((rows, SSD_W), lambda b, c: (row(b, c), 0)),
                   pl.BlockSpec((rows, LANE), lambda b, c: (row(b, c), 0)),
                   full((SUBLANE, SSD_W))],
        out_shape=[jax.ShapeDtypeStruct((T, SSD_CONV_DIM), F32),
                   jax.ShapeDtypeStruct((T, SSD_W), BF16),
                   jax.ShapeDtypeStruct((T, LANE), BF16),
                   jax.ShapeDtypeStruct((SUBLANE, SSD_W), F32)],
        scratch_shapes=[pltpu.VMEM((SSD_STATE, SSD_W), F32)],
        compiler_params=_cp(("arbitrary", "arbitrary")),
    )(pre, proj, proj, sall, dy, par, e_mat, psel)


def _sgu_consts():
    e = np.zeros((SUBLANE, SGU_W), np.float32)
    for g in range(SGU_GROUPS):
        e[g, HEAD_DIM * g:HEAD_DIM * (g + 1)] = 1.0
    return jnp.asarray(e)


def _sgu_chunk(u_raw, v_raw, ln, w, bst, e4):
    L = CHUNK
    u = _gelu(u_raw)
    v = _gelu(v_raw)
    mu = jnp.mean(v, axis=-1, keepdims=True)
    vc = v - mu
    var = jnp.mean(vc * vc, axis=-1, keepdims=True)
    vn = vc * lax.rsqrt(var + LN_EPS) * ln[0:1] + ln[1:2]
    vb = vn.astype(BF16)
    ri = lax.broadcasted_iota(jnp.int32, (L, L), 0)
    ci = lax.broadcasted_iota(jnp.int32, (L, L), 1)
    tril = ri >= ci
    col = lax.broadcasted_iota(jnp.int32, (1, SGU_W), 1)
    mixed = _dot(bst, e4, HI)
    for g in range(SGU_GROUPS):
        wc = jnp.where(tril, w[g], 0.0).astype(BF16)
        gm = (col >= HEAD_DIM * g) & (col < HEAD_DIM * (g + 1))
        mixed = mixed + jnp.where(gm, _dot(wc, vb), 0.0)
    return u * mixed


def _sgu_fwd(proj, ln, w, bst, B, S, name, after=None):
    T = B * S
    e4 = _sgu_consts()
    extra, extra_specs = _after(after)
    k = max(d for d in (4, 2, 1) if (T // CHUNK) % d == 0)
    rows = k * CHUNK

    def body(u_ref, v_ref, ln_ref, w_ref, b_ref, e_ref, *rest):
        y_ref = rest[-1]
        for i in range(k):
            r = slice(CHUNK * i, CHUNK * (i + 1))
            y_ref[r, :] = _sgu_chunk(u_ref[r, :], v_ref[r, :], ln_ref[...], w_ref[...], b_ref[...],
                                     e_ref[...]).astype(BF16)

    return pl.pallas_call(
        body, name=name, grid=(T // rows,),
        in_specs=[pl.BlockSpec((rows, SGU_W), lambda i: (i, U0 // SGU_W)),
                  pl.BlockSpec((rows, SGU_W), lambda i: (i, VS0 // SGU_W)),
                  pl.BlockSpec((SUBLANE, SGU_W), lambda i: (0, 0)),
                  pl.BlockSpec((SGU_GROUPS, CHUNK, CHUNK), lambda i: (0, 0, 0)),
                  pl.BlockSpec((CHUNK, SUBLANE), lambda i: (0, 0)),
                  pl.BlockSpec((SUBLANE, SGU_W), lambda i: (0, 0))] + extra_specs,
        out_specs=pl.BlockSpec((rows, SGU_W), lambda i: (i, 0)),
        out_shape=jax.ShapeDtypeStruct((T, SGU_W), BF16),
        compiler_params=_cp(("parallel",)),
    )(proj, proj, ln, w, bst, e4, *extra)


def _sgu_bwd(proj, dy, ln, w, bst, B, S, name):
    T = B * S
    e4 = _sgu_consts()
    ycol = (ATT_W + SSD_W) // SGU_W
    k = max(d for d in (4, 2, 1) if (T // CHUNK) % d == 0)
    rows = k * CHUNK

    def body(u_ref, v_ref, dy_ref, ln_ref, w_ref, b_ref, e_ref, du_ref, dv_ref, dln_ref, dw_ref, db_ref):
        @pl.when(pl.program_id(0) == 0)
        def _():
            dln_ref[...] = jnp.zeros_like(dln_ref)
            dw_ref[...] = jnp.zeros_like(dw_ref)
            db_ref[...] = jnp.zeros_like(db_ref)

        e_v = e_ref[...]
        fn = lambda u, v, ln, w, b: _sgu_chunk(u, v, ln, w, b, e_v)
        acc = None
        for i in range(k):
            r = slice(CHUNK * i, CHUNK * (i + 1))
            _, vjp = jax.vjp(fn, u_ref[r, :], v_ref[r, :], ln_ref[...], w_ref[...], b_ref[...])
            du, dv, *dpar = vjp(dy_ref[r, :])
            du_ref[r, :] = du.astype(BF16)
            dv_ref[r, :] = dv.astype(BF16)
            acc = dpar if acc is None else [a + d for a, d in zip(acc, dpar)]
        dln_ref[...] += acc[0]
        dw_ref[...] += acc[1]
        db_ref[...] += acc[2]

    c_ln = pl.BlockSpec((SUBLANE, SGU_W), lambda i: (0, 0))
    c_w = pl.BlockSpec((SGU_GROUPS, CHUNK, CHUNK), lambda i: (0, 0, 0))
    c_b = pl.BlockSpec((CHUNK, SUBLANE), lambda i: (0, 0))
    return pl.pallas_call(
        body, name=name, grid=(T // rows,),
        in_specs=[pl.BlockSpec((rows, SGU_W), lambda i: (i, U0 // SGU_W)),
                  pl.BlockSpec((rows, SGU_W), lambda i: (i, VS0 // SGU_W)),
                  pl.BlockSpec((rows, SGU_W), lambda i: (i, ycol)),
                  c_ln, c_w, c_b, pl.BlockSpec((SUBLANE, SGU_W), lambda i: (0, 0))],
        out_specs=[pl.BlockSpec((rows, SGU_W), lambda i: (i, 0)),
                   pl.BlockSpec((rows, SGU_W), lambda i: (i, 0)), c_ln, c_w, c_b],
        out_shape=[jax.ShapeDtypeStruct((T, SGU_W), BF16), jax.ShapeDtypeStruct((T, SGU_W), BF16),
                   jax.ShapeDtypeStruct((SUBLANE, SGU_W), F32),
                   jax.ShapeDtypeStruct((SGU_GROUPS, CHUNK, CHUNK), F32),
                   jax.ShapeDtypeStruct((CHUNK, SUBLANE), F32)],
        compiler_params=_cp(("arbitrary",)),
    )(proj, proj, dy, ln, w, bst, e4)


_HBM = pl.BlockSpec(memory_space=pltpu.HBM)
_SEM = pl.BlockSpec(memory_space=pltpu.SEMAPHORE)
_ANY = pl.BlockSpec(memory_space=pl.ANY)
_EFFECT = pltpu.SideEffectType.DATAFLOW_SIDE_EFFECTING


def _peers():
    x, y, c = lax.axis_index("x"), lax.axis_index("y"), lax.axis_index("c")
    out = []
    for p in range(1, N_DEV):
        px, py, pc = x ^ ((p >> 2) & 1), y ^ ((p >> 1) & 1), c ^ (p & 1)
        out.append(((px, py, pc), 4 * px + 2 * py + pc))
    return 4 * x + 2 * y + c, out


def _xchg_start(xs, a2a, order, name):
    n = len(xs)
    lands = [lax.empty(a.shape if f else (N_DEV,) + a.shape, a.dtype) for a, f in zip(xs, a2a)]

    def body(*refs):
        ins, zones = refs[:n], refs[n:2 * n]
        send_sems, recv_sems = refs[2 * n + 1], refs[2 * n + 2]
        token = refs[-1]
        me, peers = _peers()
        for p, (dev, peer) in enumerate(peers):
            for t in range(n):
                pltpu.make_async_remote_copy(
                    src_ref=ins[t].at[peer] if a2a[t] else ins[t], dst_ref=zones[t].at[me],
                    send_sem=send_sems.at[p * n + t], recv_sem=recv_sems.at[p * n + t],
                    device_id=dev, device_id_type=MESH).start()
        token[...] = jnp.zeros_like(token)

    hbm = lambda a: pltpu.HBM(a.shape, a.dtype)
    sems = pltpu.SemaphoreType.DMA(((N_DEV - 1) * n,))
    out = pl.pallas_call(
        body, name=name,
        in_specs=[_HBM] * (2 * n) + [_ANY],
        out_specs=[_SEM, _SEM] + [_HBM] * (2 * n) + [pl.BlockSpec(memory_space=pltpu.VMEM)],
        out_shape=[sems, sems] + [hbm(a) for a in xs] + [hbm(a) for a in lands]
        + [jax.ShapeDtypeStruct((SUBLANE, LANE), F32)],
        input_output_aliases={t: 2 + t for t in range(2 * n)},
        compiler_params=pltpu.CompilerParams(has_side_effects=_EFFECT),
    )(*[pltpu.with_memory_space_constraint(a, pltpu.HBM) for a in list(xs) + list(lands)], order)
    return out[0], out[1], out[2:2 + n], out[2 + n:2 + 2 * n], out[-1]


def _xchg_wait(started, a2a, after, name):
    send_sems, recv_sems, xs, lands, _ = started
    n = len(xs)

    def body(*refs):
        ins, zones = refs[:n], refs[n:2 * n]
        send_s, recv_s = refs[2 * n], refs[2 * n + 1]
        me, peers = _peers()
        cps = []
        for p, (dev, peer) in enumerate(peers):
            for t in range(n):
                cps.append(pltpu.make_async_remote_copy(
                    src_ref=ins[t].at[peer] if a2a[t] else ins[t], dst_ref=zones[t].at[peer],
                    send_sem=send_s.at[p * n + t], recv_sem=recv_s.at[p * n + t],
                    device_id=dev, device_id_type=MESH))
        for cp in cps:
            cp.wait_recv()
        for cp in cps:
            cp.wait_send()

    hbm = lambda a: pltpu.HBM(a.shape, a.dtype)
    out = pl.pallas_call(
        body, name=name,
        in_specs=[_HBM] * (2 * n) + [_SEM, _SEM, _ANY],
        out_specs=[_HBM] * (2 * n),
        out_shape=[hbm(a) for a in xs] + [hbm(a) for a in lands],
        input_output_aliases={t: t for t in range(2 * n)},
        compiler_params=pltpu.CompilerParams(has_side_effects=_EFFECT),
    )(*xs, *lands, send_sems, recv_sems, after)
    return out[:n], out[n:]


def _chip_peers():
    x, y, c = lax.axis_index("x"), lax.axis_index("y"), lax.axis_index("c")
    chips = [(1 - x, y), (x, 1 - y), (1 - x, 1 - y)]
    slot = lambda px, py, pc: 4 * px + 2 * py + pc
    return (x, y, c), chips, slot


def _gather_start(xs, order, name):
    n = len(xs)
    lands = [lax.empty((N_DEV,) + a.shape, a.dtype) for a in xs]

    def body(*refs):
        ins, zones = refs[:n], refs[n:2 * n]
        send_sems, d2d_sems, ici_sems = refs[2 * n + 1:2 * n + 4]
        token = refs[-1]
        (x, y, c), chips, slot = _chip_peers()
        me = slot(x, y, c)
        for t in range(n):
            pltpu.make_async_copy(ins[t], zones[t].at[me], d2d_sems.at[n + t]).start()
            for j, (px, py) in enumerate(chips):
                pltpu.make_async_remote_copy(
                    src_ref=ins[t], dst_ref=zones[t].at[me], send_sem=send_sems.at[(1 + j) * n + t],
                    recv_sem=ici_sems.at[j * n + t], device_id=(px, py, c), device_id_type=MESH).start()
            pltpu.make_async_remote_copy(
                src_ref=ins[t], dst_ref=zones[t].at[me], send_sem=send_sems.at[t],
                recv_sem=d2d_sems.at[t], device_id=(x, y, 1 - c), device_id_type=MESH).start()
        token[...] = jnp.zeros_like(token)

    hbm = lambda a: pltpu.HBM(a.shape, a.dtype)
    dma = lambda k: pltpu.SemaphoreType.DMA((k,))
    out = pl.pallas_call(
        body, name=name,
        in_specs=[_HBM] * (2 * n) + [_ANY],
        out_specs=[_SEM, _SEM, _SEM] + [_HBM] * (2 * n) + [pl.BlockSpec(memory_space=pltpu.VMEM)],
        out_shape=[dma(4 * n), dma(2 * n), dma(3 * n)] + [hbm(a) for a in xs] + [hbm(a) for a in lands]
        + [jax.ShapeDtypeStruct((SUBLANE, LANE), F32)],
        input_output_aliases={t: 3 + t for t in range(2 * n)},
        compiler_params=pltpu.CompilerParams(has_side_effects=_EFFECT),
    )(*[pltpu.with_memory_space_constraint(a, pltpu.HBM) for a in list(xs) + list(lands)], order)
    return dict(send=out[0], d2d=out[1], ici=out[2], xs=out[3:3 + n], lands=out[3 + n:3 + 2 * n], token=out[-1])


def _gather_relay(st, after, name):
    n = len(st["xs"])

    def body(*refs):
        zones, ici_sems = refs[:n], refs[n]
        fsend, frecv = refs[n + 2], refs[n + 3]
        token = refs[-1]
        (x, y, c), chips, slot = _chip_peers()
        for t in range(n):
            for j, (px, py) in enumerate(chips):
                blk = zones[t].at[slot(px, py, c)]
                fwd = pltpu.make_async_remote_copy(
                    src_ref=blk, dst_ref=blk, send_sem=fsend.at[j * n + t], recv_sem=ici_sems.at[j * n + t],
                    device_id=(x, y, 1 - c), device_id_type=MESH)
                fwd.wait_recv()
                pltpu.make_async_remote_copy(
                    src_ref=blk, dst_ref=blk, send_sem=fsend.at[j * n + t], recv_sem=frecv.at[j * n + t],
                    device_id=(x, y, 1 - c), device_id_type=MESH).start()
        token[...] = jnp.zeros_like(token)

    hbm = lambda a: pltpu.HBM(a.shape, a.dtype)
    dma = lambda k: pltpu.SemaphoreType.DMA((k,))
    out = pl.pallas_call(
        body, name=name,
        in_specs=[_HBM] * n + [_SEM, _ANY],
        out_specs=[_SEM, _SEM] + [_HBM] * n + [pl.BlockSpec(memory_space=pltpu.VMEM)],
        out_shape=[dma(3 * n), dma(3 * n)] + [hbm(a) for a in st["lands"]]
        + [jax.ShapeDtypeStruct((SUBLANE, LANE), F32)],
        input_output_aliases={t: 2 + t for t in range(n)},
        compiler_params=pltpu.CompilerParams(has_side_effects=_EFFECT),
    )(*st["lands"], st["ici"], after)
    return dict(st, fsend=out[0], frecv=out[1], lands=out[2:2 + n], token=out[-1])


def _gather_wait(st, after, name):
    n = len(st["xs"])

    def body(*refs):
        ins, zones = refs[:n], refs[n:2 * n]
        send_sems, d2d_sems, fsend, frecv = refs[2 * n:2 * n + 4]
        (x, y, c), chips, slot = _chip_peers()
        sib = (x, y, 1 - c)
        for t in range(n):
            pltpu.make_async_copy(ins[t], zones[t].at[slot(x, y, c)], d2d_sems.at[n + t]).wait()
            mine = lambda s, r, dst: pltpu.make_async_remote_copy(
                src_ref=ins[t], dst_ref=dst, send_sem=s, recv_sem=r, device_id=sib, device_id_type=MESH)
            direct = mine(send_sems.at[t], d2d_sems.at[t], zones[t].at[slot(x, y, 1 - c)])
            direct.wait_recv()
            direct.wait_send()
            for j, (px, py) in enumerate(chips):
                mine(send_sems.at[(1 + j) * n + t], d2d_sems.at[t], zones[t].at[slot(px, py, c)]).wait_send()
                relayed = mine(fsend.at[j * n + t], frecv.at[j * n + t], zones[t].at[slot(px, py, 1 - c)])
                relayed.wait_recv()
                relayed.wait_send()

    hbm = lambda a: pltpu.HBM(a.shape, a.dtype)
    out = pl.pallas_call(
        body, name=name,
        in_specs=[_HBM] * (2 * n) + [_SEM] * 4 + [_ANY],
        out_specs=[_HBM] * (2 * n),
        out_shape=[hbm(a) for a in st["xs"]] + [hbm(a) for a in st["lands"]],
        input_output_aliases={t: t for t in range(2 * n)},
        compiler_params=pltpu.CompilerParams(has_side_effects=_EFFECT),
    )(*st["xs"], *st["lands"], st["send"], st["d2d"], st["fsend"], st["frecv"], after)
    return out[:n], out[n:]


def _cast_layers(pairs, name):
    def body(*refs):
        n = len(refs) // 2
        for i in range(n):
            refs[n + i][...] = refs[i][...].astype(BF16)

    in_specs = [pl.BlockSpec((None,) + w.shape[1:], functools.partial(lambda l, i: (l, 0, 0), l),
                             pipeline_mode=pl.Buffered(1)) for w, l in pairs]
    return pl.pallas_call(
        body, name=name, grid=(1,), in_specs=in_specs,
        out_specs=[pl.BlockSpec(w.shape[1:], lambda i: (0, 0)) for w, _ in pairs],
        out_shape=[jax.ShapeDtypeStruct(w.shape[1:], BF16) for w, _ in pairs],
        compiler_params=_cp(("arbitrary",)),
    )(*[w for w, _ in pairs])


ADAMW_BLOCK_ELEMS = 256 * 1024


def _adam_step(me, w, m, v, parts_ref, mine, out_refs):
    g = None
    for p in range(N_DEV):
        term = jnp.where(me == p, mine.astype(F32), parts_ref[p].astype(F32))
        g = term if g is None else g + term
    mn = ADAM_B1 * m + (1.0 - ADAM_B1) * g
    vn = ADAM_B2 * v + (1.0 - ADAM_B2) * (g * g)
    m_hat = mn / (1.0 - ADAM_B1 ** ADAM_STEP)
    v_hat = vn / (1.0 - ADAM_B2 ** ADAM_STEP)
    g_ref, d_ref, mo_ref, vo_ref = out_refs
    g_ref[...] = g
    d_ref[...] = -ADAM_LR * (m_hat / (jnp.sqrt(v_hat) + ADAM_EPS) + ADAM_WD * w)
    mo_ref[...] = mn
    vo_ref[...] = vn


def _adamw(me, w, m, v, parts, own, name, layer=0, into=None):
    L, R, C = w.shape
    P = parts.shape[0]
    tr = R
    t = 16
    while t <= R:
        if R % t == 0 and t * C <= ADAMW_BLOCK_ELEMS:
            tr = t
        t += 16
    if tr == R and R * C > ADAMW_BLOCK_ELEMS and R % 16 == 0:
        tr = 16
    own_all = own.shape[0] == P

    def body(me_ref, w_ref, m_ref, v_ref, p_ref, own_ref, *rest):
        _adam_step(me_ref[0], w_ref[...], m_ref[...], v_ref[...], p_ref, own_ref[...], rest[-4:])

    blk = pl.BlockSpec((None, tr, C), lambda i, me_ref: (layer, i, 0))
    own_blk = pl.BlockSpec((None, tr, C), lambda i, me_ref: (me_ref[0] if own_all else 0, i, 0))
    prev = list(into) if into is not None else []
    return pl.pallas_call(
        body, name=name,
        grid_spec=pltpu.PrefetchScalarGridSpec(
            num_scalar_prefetch=1, grid=(R // tr,),
            in_specs=[blk, blk, blk, pl.BlockSpec((P, tr, C), lambda i, me_ref: (0, i, 0)), own_blk]
            + [_ANY] * len(prev),
            out_specs=[blk] * 4),
        out_shape=[jax.ShapeDtypeStruct((L, R, C), F32)] * 4,
        input_output_aliases={6 + i: i for i in range(len(prev))},
        compiler_params=_cp(("parallel",)),
    )(me, w, m, v, parts, own, *prev)


def _perm_cols(w):
    pad = jnp.zeros(w.shape[:-1] + (LANE - SSD_HEADS,), w.dtype)
    return jnp.concatenate([w[..., 0:1536], w[..., 2438:2694], w[..., 1536:2432], w[..., 2432:2438], pad,
                            w[..., 2694:2950]], axis=-1)


def _unperm_cols(w):
    return jnp.concatenate([w[..., 0:1536], w[..., XBC0:XBC0 + SSD_CONV_DIM], w[..., DT0:DT0 + SSD_HEADS],
                            w[..., U0:U0 + SGU_W], w[..., VS0:VS0 + SGU_W]], axis=-1)


_SMALL = ("ffn1_norm", "mix_norm", "conv_w", "conv_b", "dt_bias", "a_log", "d_skip", "ssd_norm",
          "sgu_ln_g", "sgu_ln_b", "sgu_w", "sgu_b", "ffn2_norm", "final_norm", "loss")


_SMALL_LAST = ("ffn1_norm",)
_SMALL_EARLY = tuple(k for k in _SMALL if k not in _SMALL_LAST)


def _pack(d, names):
    v = jnp.concatenate([d[k].astype(F32).reshape(-1) for k in names])
    n = v.shape[0]
    npad = -(-n // (LANE * 16)) * (LANE * 16)
    return jnp.pad(v, (0, npad - n)).reshape(npad // LANE, LANE)


def _unpack(p, shapes, names):
    v = p.reshape(-1)
    out, o = {}, 0
    for k in names:
        n = int(np.prod(shapes[k]))
        out[k] = v[o:o + n].reshape(shapes[k])
        o += n
    return out


def kernel(x, ffn1_norm, ffn1_w_gate, ffn1_w_up, ffn1_w_down, mix_norm, w_in, conv_w, conv_b, dt_bias, a_log, d_skip, ssd_norm, sgu_ln_g, sgu_ln_b, sgu_w, sgu_b, w_out, ffn2_norm, ffn2_w_gate, ffn2_w_up, ffn2_w_down, final_norm, loss_target, m_ffn1_norm, m_ffn1_w_gate, m_ffn1_w_up, m_ffn1_w_down, m_mix_norm, m_w_in, m_conv_w, m_conv_b, m_dt_bias, m_a_log, m_d_skip, m_ssd_norm, m_sgu_ln_g, m_sgu_ln_b, m_sgu_w, m_sgu_b, m_w_out, m_ffn2_norm, m_ffn2_w_gate, m_ffn2_w_up, m_ffn2_w_down, m_final_norm, v_ffn1_norm, v_ffn1_w_gate, v_ffn1_w_up, v_ffn1_w_down, v_mix_norm, v_w_in, v_conv_w, v_conv_b, v_dt_bias, v_a_log, v_d_skip, v_ssd_norm, v_sgu_ln_g, v_sgu_ln_b, v_sgu_w, v_sgu_b, v_w_out, v_ffn2_norm, v_ffn2_w_gate, v_ffn2_w_up, v_ffn2_w_down, v_final_norm):
    B, S, D = x.shape
    T = B * S
    L = ffn1_norm.shape[0]
    me = 4 * lax.axis_index("x") + 2 * lax.axis_index("y") + lax.axis_index("c")
    cs = conv_w.shape[2]
    W = dict(ffn1_norm=ffn1_norm, ffn1_w_gate=ffn1_w_gate, ffn1_w_up=ffn1_w_up, ffn1_w_down=ffn1_w_down,
             mix_norm=mix_norm, w_in=w_in, conv_w=conv_w, conv_b=conv_b, dt_bias=dt_bias, a_log=a_log,
             d_skip=d_skip, ssd_norm=ssd_norm, sgu_ln_g=sgu_ln_g, sgu_ln_b=sgu_ln_b, sgu_w=sgu_w, sgu_b=sgu_b,
             w_out=w_out, ffn2_norm=ffn2_norm, ffn2_w_gate=ffn2_w_gate, ffn2_w_up=ffn2_w_up,
             ffn2_w_down=ffn2_w_down, final_norm=final_norm)
    M = dict(ffn1_norm=m_ffn1_norm, ffn1_w_gate=m_ffn1_w_gate, ffn1_w_up=m_ffn1_w_up, ffn1_w_down=m_ffn1_w_down,
             mix_norm=m_mix_norm, w_in=m_w_in, conv_w=m_conv_w, conv_b=m_conv_b, dt_bias=m_dt_bias, a_log=m_a_log,
             d_skip=m_d_skip, ssd_norm=m_ssd_norm, sgu_ln_g=m_sgu_ln_g, sgu_ln_b=m_sgu_ln_b, sgu_w=m_sgu_w,
             sgu_b=m_sgu_b, w_out=m_w_out, ffn2_norm=m_ffn2_norm, ffn2_w_gate=m_ffn2_w_gate,
             ffn2_w_up=m_ffn2_w_up, ffn2_w_down=m_ffn2_w_down, final_norm=m_final_norm)
    V = dict(ffn1_norm=v_ffn1_norm, ffn1_w_gate=v_ffn1_w_gate, ffn1_w_up=v_ffn1_w_up, ffn1_w_down=v_ffn1_w_down,
             mix_norm=v_mix_norm, w_in=v_w_in, conv_w=v_conv_w, conv_b=v_conv_b, dt_bias=v_dt_bias, a_log=v_a_log,
             d_skip=v_d_skip, ssd_norm=v_ssd_norm, sgu_ln_g=v_sgu_ln_g, sgu_ln_b=v_sgu_ln_b, sgu_w=v_sgu_w,
             sgu_b=v_sgu_b, w_out=v_w_out, ffn2_norm=v_ffn2_norm, ffn2_w_gate=v_ffn2_w_gate,
             ffn2_w_up=v_ffn2_w_up, ffn2_w_down=v_ffn2_w_down, final_norm=v_final_norm)
    FFN1 = ("ffn1_w_gate", "ffn1_w_up", "ffn1_w_down")
    FFN2 = ("ffn2_w_gate", "ffn2_w_up", "ffn2_w_down")
    MIX = ("w_in", "w_out")
    big = FFN1 + MIX + FFN2
    col_sharded = lambda k: k.endswith("w_gate") or k.endswith("w_up")
    for dct in (W, M, V):
        for k in big:
            if col_sharded(k):
                dct[k] = jnp.swapaxes(dct[k], 1, 2)
        dct["w_in"] = _perm_cols(dct["w_in"])

    wgroups = [[(k, 0) for k in FFN1], [("w_in", 0), ("conv_w", None)], [("w_out", 0)] + [(k, 0) for k in FFN2]]
    for l in range(1, L):
        wgroups += [[(k, l) for k in FFN1] + [("w_in", l)], [("w_out", l)] + [(k, l) for k in FFN2]]
    wstarted, order = [], x
    later = [kl for grp in wgroups[1:] for kl in grp if kl[0] != "conv_w"]
    cast = dict(zip(wgroups[0], _cast_layers([(W[k], l) for k, l in wgroups[0]], "cast_first")))
    for gi, grp in enumerate(wgroups):
        if gi == 1:
            first = lax.optimization_barrier((W[later[0][0]], order))[0]
            srcs = [(first if i == 0 else W[k], l) for i, (k, l) in enumerate(later)]
            cast.update(zip(later, _cast_layers(srcs, "cast_rest")))
        xs = [conv_w if k == "conv_w" else cast[(k, l)] for k, l in grp]
        st = _gather_start(xs, order, f"gather_start_{gi}")
        order = st["token"]
        wstarted.append(st)
    G = {}

    zero1 = jnp.zeros((1,), F32)
    W["loss"], M["loss"], V["loss"] = zero1, zero1, zero1
    full_shapes = {k: (W[k].shape if k != "conv_w" else (L, SSD_CONV, SSD_CONV_DIM)) for k in _SMALL}
    embed = lambda a, k: a if k != "conv_w" else lax.dynamic_update_slice(
        jnp.zeros(full_shapes[k], F32), a, (0, 0, me * cs))
    small_packs = {names: [_pack({k: embed(d[k], k) for k in names}, names)[None] for d in (W, M, V)]
                   for names in (_SMALL_EARLY, _SMALL_LAST)}

    def relay(gi, after):
        wstarted[gi] = _gather_relay(wstarted[gi], after, f"gather_relay_{gi}")
        return wstarted[gi]["token"]

    def gathered(gi, after):
        _, lands = _gather_wait(wstarted[gi], after, f"gather_wait_{gi}")
        G.update(zip(wgroups[gi], lands))

    def rows(k, l):
        a = G[(k, l)]
        return a.reshape(-1, a.shape[-1])

    bias = _attn_bias(S, min(256, S))
    row1 = lambda a: a.reshape(1, -1)

    def ffn1_params(l):
        return dict(g1=row1(ffn1_norm[l]), wg1=rows("ffn1_w_gate", l), wu1=rows("ffn1_w_up", l),
                    wd1=rows("ffn1_w_down", l))

    def out_params(l):
        return dict(wout=rows("w_out", l), g2=row1(ffn2_norm[l]), wg2=rows("ffn2_w_gate", l),
                    wu2=rows("ffn2_w_up", l), wd2=rows("ffn2_w_down", l))

    def mix_params(l):
        cw = jnp.transpose(G[("conv_w", None)][:, l], (1, 0, 2)).reshape(SSD_CONV, -1)
        return dict(
            gm=row1(mix_norm[l]), win=rows("w_in", l),
            cw=jnp.pad(cw, ((0, SUBLANE - SSD_CONV), (0, 0))), cb=row1(conv_b[l]),
            par=jnp.pad(jnp.stack([jnp.repeat(dt_bias[l], HEAD_DIM), jnp.repeat(a_log[l], HEAD_DIM),
                                   jnp.repeat(d_skip[l], HEAD_DIM), ssd_norm[l]]), ((0, SUBLANE - 4), (0, 0))),
            ln=jnp.pad(jnp.stack([sgu_ln_g[l], sgu_ln_b[l]]), ((0, SUBLANE - 2), (0, 0))),
            sw=sgu_w[l], bst=jnp.pad(sgu_b[l].T, ((0, 0), (0, SUBLANE - SGU_GROUPS))))

    xc = x.reshape(T, D)
    saved, lay = [], []
    for l in range(L):
        if l == 0:
            gathered(0, relay(0, order))
        else:
            gathered(1 + 2 * l, xc)
        p = ffn1_params(l)
        x1, *ffn1_saved = _ffn_fwd(xc, p["g1"], p["wg1"], p["wu1"], p["wd1"], f"ffn1_fwd_{l}")
        if l == 0:
            gathered(1, relay(1, x1))
        p.update(mix_params(l))
        lay.append(p)
        proj, ht = _norm_mm(x1, p["gm"], p["win"], f"in_proj_{l}")
        o_att, lse = _attn_fwd(proj, bias, B, S, f"attn_fwd_{l}")
        tok = relay(2 + 2 * l, o_att)
        pre = _conv_fwd(proj, p["cw"], p["cb"], B, S, f"conv_fwd_{l}", after=tok)
        y_ssd, sall = _ssd_fwd(pre, proj, p["par"], B, S, f"ssd_fwd_{l}")
        y_sgu = _sgu_fwd(proj, p["ln"], p["sw"], p["bst"], B, S, f"sgu_fwd_{l}", after=tok)
        ycat = jnp.concatenate([o_att.astype(BF16), y_ssd, y_sgu], axis=1)
        gathered(2 + 2 * l, ycat)
        p.update(out_params(l))
        x2 = _mm(ycat, p["wout"], "nn", f"out_proj_{l}", residual=x1)
        tok = relay(3 + 2 * l, x2) if l + 1 < L else None
        x3, *ffn2_saved = _ffn_fwd(x2, p["g2"], p["wg2"], p["wu2"], p["wd2"], f"ffn2_fwd_{l}", after=tok)
        saved.append(dict(x0=xc, ffn1=ffn1_saved, x1=x1, ht=ht, proj=proj, o_att=o_att, lse=lse, pre=pre,
                          sall=sall, ycat=ycat, x2=x2, ffn2=ffn2_saved))
        xc = x3
    loss_part, dx, dgf = _final_loss(xc, row1(final_norm), loss_target.reshape(T, D), "final_loss")

    gl = [dict() for _ in range(L)]
    gstarted, gorder = [], [order]

    def to_blocks(k, a):
        return a.reshape(N_DEV, -1, a.shape[-1]).astype(BF16)

    def send_grads(keys, l, extra, tag, small_names=None):
        xs = [to_blocks(k, gl[l][k]) for k in keys] + extra
        flags = [True] * len(keys) + [False] * len(extra)
        st = _xchg_start(xs, flags, gorder[0], f"grads_start_{tag}")
        gorder[0] = st[-1]
        gstarted.append((keys, l, st, flags, tag, small_names))

    def small_grads(names):
        sm = {}
        for k in names:
            if k == "final_norm":
                sm[k] = dgf.reshape(-1)
            elif k == "loss":
                sm[k] = loss_part[0, :1]
            else:
                sm[k] = jnp.stack([gl[l][k] for l in range(L)])
        return [_pack(sm, names)]

    def behind(a):
        return lax.optimization_barrier((a, gorder[0]))[0]

    for l in reversed(range(L)):
        p, s, g = lay[l], saved[l], gl[l]
        gfac, ufac, actt = s["ffn2"]
        dx2, dgt, dut, xn, dacc, g["ffn2_norm"] = _ffn_bwd_dx(
            dx, s["x2"], p["g2"], gfac, ufac, p["wg2"], p["wu2"], p["wd2"], f"ffn2_bwd_{l}")
        g["ffn2_w_gate"], g["ffn2_w_up"], g["ffn2_w_down"] = _ffn_dw(dgt, dut, actt, xn, dacc, f"ffn2_dw_{l}")
        if l == 0:
            send_grads(FFN2, 0, [], "l0f")
            dx2 = behind(dx2)
        dycat = _mm(dx2, p["wout"], "nt", f"out_proj_dx_{l}")
        g["w_out"] = _mm(s["ycat"], dx2, "tn", f"out_proj_dw_{l}", out_dtype=BF16, tm_cap=1024, tk_cap=512)
        dq, dk, dv = _attn_bwd(s["proj"], s["o_att"], s["lse"], dycat, bias, B, S, f"attn_bwd_{l}")
        dpre, dz, ddt, dpar = _ssd_bwd(s["pre"], s["proj"], s["sall"], dycat, p["par"], B, S, f"ssd_bwd_{l}")
        dxbc, dwb = _conv_bwd(dpre, s["proj"], p["cw"], B, S, f"conv_bwd_{l}")
        du, dvs, dln, dsw, dbst = _sgu_bwd(s["proj"], dycat, p["ln"], p["sw"], p["bst"], B, S, f"sgu_bwd_{l}")
        hsum = lambda r: r.reshape(SSD_HEADS, HEAD_DIM).sum(-1)
        g["conv_w"], g["conv_b"] = dwb[:SSD_CONV], dwb[SSD_CONV]
        g["dt_bias"], g["a_log"], g["d_skip"], g["ssd_norm"] = hsum(dpar[0]), hsum(dpar[1]), hsum(dpar[2]), dpar[3]
        g["sgu_ln_g"], g["sgu_ln_b"], g["sgu_w"], g["sgu_b"] = dln[0], dln[1], dsw, dbst[:, :SGU_GROUPS].T
        dproj = jnp.concatenate([dq, dk, dv, dz, du, dxbc, ddt, dvs], axis=1)
        g["w_in"] = _mm_resident_lhs(s["ht"], dproj, f"in_proj_dw_{l}")
        dx1, g["mix_norm"] = _norm_mm_bwd(dproj, s["x1"], p["gm"], p["win"], dx2, f"in_proj_bwd_{l}")
        if l == 0:
            send_grads(MIX, 0, small_grads(_SMALL_EARLY), "l0a", _SMALL_EARLY)
            dx1, small_packs = lax.optimization_barrier((behind(dx1), small_packs))
        gfac, ufac, actt = s["ffn1"]
        dx, dgt, dut, xn, dacc, g["ffn1_norm"] = _ffn_bwd_dx(
            dx1, s["x0"], p["g1"], gfac, ufac, p["wg1"], p["wu1"], p["wd1"], f"ffn1_bwd_{l}")
        if l > 0:
            g["ffn1_w_gate"], g["ffn1_w_up"], g["ffn1_w_down"] = _ffn_dw(dgt, dut, actt, xn, dacc,
                                                                        f"ffn1_dw_{l}")
            send_grads(big, l, [], f"l{l}")
            dx = behind(dx)
        else:
            g["ffn1_w_gate"] = _dw_one(dgt, xn, "ffn1_dwg_0")
            send_grads(("ffn1_w_gate",), 0, [], "l0b1")
            g["ffn1_w_up"] = _dw_one(dut, xn, "ffn1_dwu_0", after=gorder[0])
            send_grads(("ffn1_w_up",), 0, [], "l0b2")
            g["ffn1_w_down"] = _dw_one(actt, dacc, "ffn1_dwd_0", after=gorder[0])
    grad_x = dx.reshape(B, S, D)
    send_grads(("ffn1_w_down",), 0, small_grads(_SMALL_LAST), "l0b", _SMALL_LAST)

    res, after = {}, gorder[0]
    small_out = [dict() for _ in range(4)]
    me1 = me.reshape(1).astype(jnp.int32)
    for keys, l, st, flags, tag, names in gstarted:
        own, lands = _xchg_wait(st, flags, after, f"grads_wait_{tag}")
        for k, mine, pk in zip(keys, own, lands):
            res[k] = _adamw(me1, W[k], M[k], V[k], pk, mine, f"adamw_{k}_{l}", layer=l, into=res.get(k))
        done = [res[k][0] for k in keys]
        if names:
            outs = _adamw(me1, *small_packs[names], lands[-1], own[-1][None], f"adamw_small_{tag}")
            for d, o in zip(small_out, outs):
                u = _unpack(o, full_shapes, names)
                if "conv_w" in u:
                    u["conv_w"] = lax.dynamic_slice(u["conv_w"], (0, 0, me * cs), (L, SSD_CONV, cs))
                d.update(u)
                done.extend(u.values())
        after = lax.optimization_barrier(tuple(done))[0]
    back = lambda k, a: jnp.swapaxes(a, 1, 2) if col_sharded(k) else _unperm_cols(a) if k == "w_in" else a
    grads, deltas, new_m, new_v = [dict({k: back(k, res[k][i]) for k in big}, **small_out[i]) for i in range(4)]

    names = ("ffn1_norm", "ffn1_w_gate", "ffn1_w_up", "ffn1_w_down", "mix_norm", "w_in", "conv_w", "conv_b",
             "dt_bias", "a_log", "d_skip", "ssd_norm", "sgu_ln_g", "sgu_ln_b", "sgu_w", "sgu_b", "w_out",
             "ffn2_norm", "ffn2_w_gate", "ffn2_w_up", "ffn2_w_down", "final_norm")
    loss = grads["loss"][0]
    return (loss, grad_x, *[grads[n] for n in names], *[deltas[n] for n in names],
            *[new_m[n] for n in names], *[new_v[n] for n in names])
```

```python
import functools

import numpy as np
import jax
import jax.numpy as jnp
from jax import lax
from jax.experimental import pallas as pl
from jax.experimental.pallas import tpu as pltpu

F32, BF16 = jnp.float32, jnp.bfloat16
HI = lax.Precision.HIGH
MESH = pl.DeviceIdType.MESH
N_DEV = 8
VMEM_LIMIT_BYTES = 56 * 1024 * 1024
LANE, SUBLANE = 128, 8

HEAD_DIM = 64
ATT_W = 384
SSD_W = 384
SSD_HEADS = 6
SSD_STATE = 128
SSD_CONV = 4
CHUNK = 128
SSD_CONV_DIM = 896
SGU_W = 256
SGU_GROUPS = 4
D_IN = 2950
RMS_EPS = 1e-6
LN_EPS = 1e-5
NEG = -1e30

PW = 3072
Q0, K0, V0, Z0, U0, XBC0, DT0, VS0 = 0, 384, 768, 1152, 1536, 1792, 2688, 2816

ADAM_LR, ADAM_B1, ADAM_B2, ADAM_EPS, ADAM_WD, ADAM_STEP = 0.001, 0.9, 0.999, 1e-08, 0.01, 10


def _cp(sem=None):
    return pltpu.CompilerParams(dimension_semantics=sem, vmem_limit_bytes=VMEM_LIMIT_BYTES)


def _tile(n, cap, mult=LANE):
    best = None
    t = mult
    while t <= min(n, cap):
        if n % t == 0:
            best = t
        t += mult
    return best if best is not None else n


def _dot(a, b, prec=None):
    return jnp.dot(a, b, preferred_element_type=F32, precision=prec)


def _dot_nt(a, b, prec=None):
    return lax.dot_general(a, b, (((1,), (1,)), ((), ())), preferred_element_type=F32, precision=prec)


def _dot_tn(a, b, prec=None):
    return lax.dot_general(a, b, (((0,), (0,)), ((), ())), preferred_element_type=F32, precision=prec)


def _sigmoid(x):
    return 1.0 / (1.0 + jnp.exp(-x))


def _silu(x):
    return x * _sigmoid(x)


def _gelu(x):
    return 0.5 * x * (1.0 + lax.erf(x * 0.7071067811865476))


def _softplus(x):
    return jnp.maximum(x, 0.0) + jnp.log(1.0 + jnp.exp(-jnp.abs(x)))


def _rms_fwd(x, g):
    rstd = lax.rsqrt(jnp.mean(x * x, axis=-1, keepdims=True) + RMS_EPS)
    xhat = x * rstd
    return xhat * g, xhat, rstd


def _rms_bwd(dy, xhat, rstd, g):
    dxhat = dy * g
    dx = rstd * (dxhat - xhat * jnp.mean(dxhat * xhat, axis=-1, keepdims=True))
    return dx, dy * xhat


def _resident(shape):
    return pl.BlockSpec(shape, lambda *_: (0,) * len(shape), pipeline_mode=pl.Buffered(1))


def _mm(a, b, mode, name, out_dtype=F32, residual=None, tm_cap=512, tn_cap=1024, tk_cap=1024):
    if mode == "nn":
        (M, K), (_, N) = a.shape, b.shape
    elif mode == "nt":
        (M, K), (N, _) = a.shape, b.shape
    else:
        (K, M), (_, N) = a.shape, b.shape
    tm, tn, tk = _tile(M, tm_cap), _tile(N, tn_cap), _tile(K, tk_cap)
    nk = K // tk
    if mode == "tn":
        a_spec = pl.BlockSpec((tk, tm), lambda i, j, k: (k, i))
    else:
        a_spec = pl.BlockSpec((tm, tk), lambda i, j, k: (i, k))
    if mode == "nt":
        b_spec = pl.BlockSpec((tn, tk), lambda i, j, k: (j, k))
    else:
        b_spec = pl.BlockSpec((tk, tn), lambda i, j, k: (k, j))
    o_spec = pl.BlockSpec((tm, tn), lambda i, j, k: (i, j))
    has_res = residual is not None

    def prod(a_ref, b_ref):
        av = a_ref[...].astype(BF16)
        bv = b_ref[...].astype(BF16)
        if mode == "nn":
            return _dot(av, bv)
        if mode == "nt":
            return _dot_nt(av, bv)
        return _dot_tn(av, bv)

    def body(*refs):
        a_ref, b_ref = refs[:2]
        r_ref = refs[2] if has_res else None
        o_ref = refs[2 + has_res]
        if nk == 1:
            o = prod(a_ref, b_ref)
            if has_res:
                o = r_ref[...] + o
            o_ref[...] = o.astype(out_dtype)
            return
        acc = refs[3 + has_res]
        k = pl.program_id(2)

        @pl.when(k == 0)
        def _():
            acc[...] = jnp.zeros_like(acc)

        acc[...] += prod(a_ref, b_ref)

        @pl.when(k == nk - 1)
        def _():
            o = acc[...]
            if has_res:
                o = r_ref[...] + o
            o_ref[...] = o.astype(out_dtype)

    ins = [a, b] + ([residual] if has_res else [])
    in_specs = [a_spec, b_spec] + ([o_spec] if has_res else [])
    return pl.pallas_call(
        body, name=name, grid=(M // tm, N // tn, nk),
        in_specs=in_specs, out_specs=o_spec,
        out_shape=jax.ShapeDtypeStruct((M, N), out_dtype),
        scratch_shapes=[pltpu.VMEM((tm, tn), F32)] if nk > 1 else [],
        compiler_params=_cp(("parallel", "parallel", "arbitrary")),
    )(*ins)


def _after(after):
    return ([after], [_ANY]) if after is not None else ([], [])


def _ffn_fwd(x, g, wgt, wut, wd, name, after=None):
    T, D = x.shape
    F = wgt.shape[0]
    tm = _tile(T, 256)

    def body(x_ref, g_ref, wg_ref, wu_ref, wd_ref, *rest):
        out_ref, dgf_ref, sl_ref, actt_ref = rest[-4:]
        xv = x_ref[...]
        xn = _rms_fwd(xv, g_ref[...])[0].astype(BF16)
        gate = _dot_nt(xn, wg_ref[...])
        up = _dot_nt(xn, wu_ref[...])
        sig = _sigmoid(gate)
        sl = gate * sig
        dgf_ref[...] = (up * (sig + sl * (1.0 - sig))).astype(BF16)
        sl_ref[...] = sl.astype(BF16)
        act = (sl * up).astype(BF16)
        actt_ref[...] = act.T
        out_ref[...] = xv + 0.5 * _dot(act, wd_ref[...])

    row = lambda w: pl.BlockSpec((tm, w), lambda i: (i, 0))
    extra, extra_specs = _after(after)
    return pl.pallas_call(
        body, name=name, grid=(T // tm,),
        in_specs=[row(D), _resident((1, D)), _resident((F, D)), _resident((F, D)), _resident((F, D))] + extra_specs,
        out_specs=[row(D), row(F), row(F), pl.BlockSpec((F, tm), lambda i: (0, i))],
        out_shape=[jax.ShapeDtypeStruct((T, D), F32),
                   jax.ShapeDtypeStruct((T, F), BF16),
                   jax.ShapeDtypeStruct((T, F), BF16),
                   jax.ShapeDtypeStruct((F, T), BF16)],
        compiler_params=_cp(("parallel",)),
    )(x, g, wgt, wut, wd, *extra)


def _ffn_bwd_dx(dout, x, g, dgf, sl, wg, wu, wd, name):
    T, D = x.shape
    F = wg.shape[0]
    tm = _tile(T, 256)

    def body(dout_ref, x_ref, g_ref, dgf_ref, sl_ref, wg_ref, wu_ref, wd_ref,
             dx_ref, dgt_ref, dut_ref, xn_ref, dacc_ref, dg_ref):
        @pl.when(pl.program_id(0) == 0)
        def _():
            dg_ref[...] = jnp.zeros_like(dg_ref)

        gv = g_ref[...]
        dout_v = dout_ref[...]
        xn, xhat, rstd = _rms_fwd(x_ref[...], gv)
        xn_ref[...] = xn.astype(BF16)
        dacc = (0.5 * dout_v).astype(BF16)
        dacc_ref[...] = dacc
        dact = _dot_nt(dacc, wd_ref[...])
        dgate = (dact * dgf_ref[...].astype(F32)).astype(BF16)
        dup = (dact * sl_ref[...].astype(F32)).astype(BF16)
        dgt_ref[...] = dgate.T
        dut_ref[...] = dup.T
        dxn = _dot(dgate, wg_ref[...]) + _dot(dup, wu_ref[...])
        dx, dgrow = _rms_bwd(dxn, xhat, rstd, gv)
        dx_ref[...] = dout_v + dx
        dg_ref[...] += jnp.sum(dgrow, axis=0, keepdims=True)

    row = lambda w: pl.BlockSpec((tm, w), lambda i: (i, 0))
    tr = pl.BlockSpec((F, tm), lambda i: (0, i))
    return pl.pallas_call(
        body, name=name, grid=(T // tm,),
        in_specs=[row(D), row(D), _resident((1, D)), row(F), row(F),
                  _resident((F, D)), _resident((F, D)), _resident((F, D))],
        out_specs=[row(D), tr, tr, row(D), row(D), pl.BlockSpec((1, D), lambda i: (0, 0))],
        out_shape=[jax.ShapeDtypeStruct((T, D), F32)] + [jax.ShapeDtypeStruct((F, T), BF16)] * 2
        + [jax.ShapeDtypeStruct((T, D), BF16)] * 2 + [jax.ShapeDtypeStruct((1, D), F32)],
        compiler_params=_cp(("arbitrary",)),
    )(dout, x, g, dgf, sl, wg, wu, wd)


def _ffn_dw(dgt, dut, actt, xn, dacc, name):
    F, T = dgt.shape
    D = xn.shape[1]
    th = _tile(F, 256)

    def body(dg_ref, du_ref, a_ref, xn_ref, dacc_ref, dwg_ref, dwu_ref, dwd_ref):
        xv = xn_ref[...]
        dwg_ref[...] = _dot(dg_ref[...], xv).astype(BF16)
        dwu_ref[...] = _dot(du_ref[...], xv).astype(BF16)
        dwd_ref[...] = _dot(a_ref[...], dacc_ref[...]).astype(BF16)

    tile = pl.BlockSpec((th, T), lambda j: (j, 0))
    out = pl.BlockSpec((th, D), lambda j: (j, 0))
    return pl.pallas_call(
        body, name=name, grid=(F // th,),
        in_specs=[tile, tile, tile, _resident((T, D)), _resident((T, D))],
        out_specs=[out, out, out], out_shape=[jax.ShapeDtypeStruct((F, D), BF16)] * 3,
        compiler_params=_cp(("parallel",)),
    )(dgt, dut, actt, xn, dacc)


def _dw_one(lt, r, name, after=None):
    F, T = lt.shape
    D = r.shape[1]
    th = _tile(F, 256)
    extra, extra_specs = _after(after)

    def body(l_ref, r_ref, *rest):
        rest[-1][...] = _dot(l_ref[...], r_ref[...]).astype(BF16)

    return pl.pallas_call(
        body, name=name, grid=(F // th,),
        in_specs=[pl.BlockSpec((th, T), lambda j: (j, 0)), _resident((T, D))] + extra_specs,
        out_specs=pl.BlockSpec((th, D), lambda j: (j, 0)),
        out_shape=jax.ShapeDtypeStruct((F, D), BF16),
        compiler_params=_cp(("parallel",)),
    )(lt, r, *extra)


def _norm_mm(x, g, w, name):
    T, D = x.shape
    N = w.shape[1]
    tm = _tile(T, 512)

    def body(x_ref, g_ref, w_ref, o_ref, ht_ref):
        xn = _rms_fwd(x_ref[...], g_ref[...])[0]
        ht_ref[...] = xn.T.astype(BF16)
        o_ref[...] = _dot(xn.astype(BF16), w_ref[...])

    return pl.pallas_call(
        body, name=name, grid=(T // tm,),
        in_specs=[pl.BlockSpec((tm, D), lambda i: (i, 0)), _resident((1, D)), _resident((D, N))],
        out_specs=[pl.BlockSpec((tm, N), lambda i: (i, 0)), pl.BlockSpec((D, tm), lambda i: (0, i))],
        out_shape=[jax.ShapeDtypeStruct((T, N), F32), jax.ShapeDtypeStruct((D, T), BF16)],
        compiler_params=_cp(("parallel",)),
    )(x, g, w)


def _norm_mm_bwd(dproj, x, g, w, dres, name):
    T, D = x.shape
    N = w.shape[1]
    tm = _tile(T, 512)

    def body(dp_ref, x_ref, g_ref, w_ref, dres_ref, dx_ref, dg_ref):
        @pl.when(pl.program_id(0) == 0)
        def _():
            dg_ref[...] = jnp.zeros_like(dg_ref)

        gv = g_ref[...]
        dh = _dot_nt(dp_ref[...], w_ref[...])
        _, xhat, rstd = _rms_fwd(x_ref[...], gv)
        dx, dgrow = _rms_bwd(dh, xhat, rstd, gv)
        dx_ref[...] = dres_ref[...] + dx
        dg_ref[...] += jnp.sum(dgrow, axis=0, keepdims=True)

    row = pl.BlockSpec((tm, D), lambda i: (i, 0))
    one = pl.BlockSpec((1, D), lambda i: (0, 0))
    return pl.pallas_call(
        body, name=name, grid=(T // tm,),
        in_specs=[pl.BlockSpec((tm, N), lambda i: (i, 0)), row, _resident((1, D)), _resident((D, N)), row],
        out_specs=[row, one],
        out_shape=[jax.ShapeDtypeStruct((T, D), F32), jax.ShapeDtypeStruct((1, D), F32)],
        compiler_params=_cp(("arbitrary",)),
    )(dproj, x, g, w, dres)


def _mm_resident_lhs(at, b, name, tn_cap=512):
    M, K = at.shape
    N = b.shape[1]
    tn = _tile(N, tn_cap)

    def body(a_ref, b_ref, o_ref):
        o_ref[...] = _dot(a_ref[...], b_ref[...]).astype(BF16)

    return pl.pallas_call(
        body, name=name, grid=(N // tn,),
        in_specs=[_resident((M, K)), pl.BlockSpec((K, tn), lambda j: (0, j))],
        out_specs=pl.BlockSpec((M, tn), lambda j: (0, j)),
        out_shape=jax.ShapeDtypeStruct((M, N), BF16),
        compiler_params=_cp(("parallel",)),
    )(at, b)


def _final_loss(x, g, target, name):
    T, D = x.shape
    tm = _tile(T, 512)

    def body(x_ref, g_ref, t_ref, loss_ref, dx_ref, dg_ref):
        @pl.when(pl.program_id(0) == 0)
        def _():
            dg_ref[...] = jnp.zeros_like(dg_ref)
            loss_ref[...] = jnp.zeros_like(loss_ref)

        gv = g_ref[...]
        y, xhat, rstd = _rms_fwd(x_ref[...], gv)
        err = y - t_ref[...]
        part = 0.5 * jnp.sum(jnp.mean(err * err, axis=-1, keepdims=True), axis=0, keepdims=True)
        loss_ref[...] += jnp.broadcast_to(part, loss_ref.shape)
        dy = err * (1.0 / D)
        dx, dgrow = _rms_bwd(dy, xhat, rstd, gv)
        dx_ref[...] = dx
        dg_ref[...] += jnp.sum(dgrow, axis=0, keepdims=True)

    row = pl.BlockSpec((tm, D), lambda i: (i, 0))
    one = pl.BlockSpec((1, D), lambda i: (0, 0))
    return pl.pallas_call(
        body, name=name, grid=(T // tm,),
        in_specs=[row, one, row],
        out_specs=[pl.BlockSpec((1, LANE), lambda i: (0, 0)), row, one],
        out_shape=[jax.ShapeDtypeStruct((1, LANE), F32), jax.ShapeDtypeStruct((T, D), F32),
                   jax.ShapeDtypeStruct((1, D), F32)],
        compiler_params=_cp(("arbitrary",)),
    )(x, g, target)


def _attn_bias(S, bq):
    d = np.arange(bq)[:, None] - np.arange(S)[None, :] + (S // bq - 1) * bq
    ok = d >= 0
    mult = ((ok & (d <= 128)).astype(np.float32) + (ok & (d % 4 == 0) & (d <= 512))
            + (ok & (d % 16 == 0) & (d <= 2048)))
    return jnp.asarray(np.where(mult > 0, np.log(np.maximum(mult, 1.0)), NEG).astype(np.float32))


def _attn_fwd(proj, bias, B, S, name):
    T = B * S
    bq = bias.shape[0]
    nb = S // bq
    qcol, kcol, vcol = Q0 // LANE, K0 // LANE, V0 // LANE

    def body(q_ref, k_ref, v_ref, t_ref, o_ref, lse_ref, ks, vs):
        for hh in range(2):
            sl = slice(HEAD_DIM * hh, HEAD_DIM * (hh + 1))
            ks[hh] = k_ref[:, sl].astype(BF16)
            vs[hh] = v_ref[:, sl].astype(BF16)
        for hh in range(2):
            sl = slice(HEAD_DIM * hh, HEAD_DIM * (hh + 1))
            for qb in range(nb):
                w, off, rows = bq * (qb + 1), (nb - 1 - qb) * bq, slice(qb * bq, (qb + 1) * bq)
                q = (q_ref[rows, sl] * 0.125).astype(BF16)
                s = _dot_nt(q, ks[hh, 0:w, :]) + t_ref[:, off:off + w]
                m = jnp.max(s, axis=-1, keepdims=True)
                p = jnp.exp(s - m)
                l = jnp.sum(p, axis=-1, keepdims=True)
                o_ref[rows, sl] = _dot(p.astype(BF16), vs[hh, 0:w, :]) / l
                lse_ref[rows, hh:hh + 1] = m + jnp.log(l)

    blk = lambda c0: pl.BlockSpec((S, LANE), lambda b, p: (b, c0 + p))
    return pl.pallas_call(
        body, name=name, grid=(B, ATT_W // LANE),
        in_specs=[blk(qcol), blk(kcol), blk(vcol), _resident((bq, S))],
        out_specs=[pl.BlockSpec((S, LANE), lambda b, p: (b, p)),
                   pl.BlockSpec((None, None, S, 2), lambda b, p: (b, p, 0, 0))],
        out_shape=[jax.ShapeDtypeStruct((T, ATT_W), F32),
                   jax.ShapeDtypeStruct((B, ATT_W // LANE, S, 2), F32)],
        scratch_shapes=[pltpu.VMEM((2, S, HEAD_DIM), BF16)] * 2,
        compiler_params=_cp(("parallel", "parallel")),
    )(proj, proj, proj, bias)


def _attn_bwd(proj, o, lse, dy, bias, B, S, name):
    T = B * S
    bq = bias.shape[0]
    nb = S // bq
    qcol, kcol, vcol = Q0 // LANE, K0 // LANE, V0 // LANE

    def body(q_ref, k_ref, v_ref, o_ref, lse_ref, do_ref, t_ref, dq_ref, dk_ref, dv_ref, ks, vs, dks, dvs):
        for hh in range(2):
            sl = slice(HEAD_DIM * hh, HEAD_DIM * (hh + 1))
            ks[hh] = k_ref[:, sl].astype(BF16)
            vs[hh] = v_ref[:, sl].astype(BF16)
        dks[...] = jnp.zeros_like(dks)
        dvs[...] = jnp.zeros_like(dvs)
        for hh in range(2):
            sl = slice(HEAD_DIM * hh, HEAD_DIM * (hh + 1))
            for qb in range(nb):
                w, off, rows = bq * (qb + 1), (nb - 1 - qb) * bq, slice(qb * bq, (qb + 1) * bq)
                q = (q_ref[rows, sl] * 0.125).astype(BF16)
                do = do_ref[rows, sl]
                dob = do.astype(BF16)
                delta = jnp.sum(do * o_ref[rows, sl], axis=-1, keepdims=True)
                k, v = ks[hh, 0:w, :], vs[hh, 0:w, :]
                s = _dot_nt(q, k) + t_ref[:, off:off + w]
                p = jnp.exp(s - lse_ref[rows, hh:hh + 1])
                ds = (p * (_dot_nt(dob, v) - delta)).astype(BF16)
                dq_ref[rows, sl] = (_dot(ds, k) * 0.125).astype(dq_ref.dtype)
                dks[hh, 0:w, :] += _dot_tn(ds, q)
                dvs[hh, 0:w, :] += _dot_tn(p.astype(BF16), dob)
            dk_ref[:, sl] = dks[hh].astype(dk_ref.dtype)
            dv_ref[:, sl] = dvs[hh].astype(dv_ref.dtype)

    blk = lambda c0: pl.BlockSpec((S, LANE), lambda b, p: (b, c0 + p))
    own = pl.BlockSpec((S, LANE), lambda b, p: (b, p))
    return pl.pallas_call(
        body, name=name, grid=(B, ATT_W // LANE),
        in_specs=[blk(qcol), blk(kcol), blk(vcol), own,
                  pl.BlockSpec((None, None, S, 2), lambda b, p: (b, p, 0, 0)), own, _resident((bq, S))],
        out_specs=[own, own, own],
        out_shape=[jax.ShapeDtypeStruct((T, ATT_W), BF16)] * 3,
        scratch_shapes=[pltpu.VMEM((2, S, HEAD_DIM), BF16)] * 2 + [pltpu.VMEM((2, S, HEAD_DIM), F32)] * 2,
        compiler_params=_cp(("parallel", "parallel")),
    )(proj, proj, proj, o, lse, dy, bias)


def _conv_fwd(proj, cw, cb, B, S, name, after=None):
    T = B * S
    nc = SSD_CONV_DIM // LANE
    c0 = XBC0 // LANE
    extra, extra_specs = _after(after)

    def body(x_ref, w_ref, b_ref, *rest):
        o_ref = rest[-1]
        x = x_ref[...]
        t = lax.broadcasted_iota(jnp.int32, (S, 1), 0)
        acc = b_ref[...] + w_ref[SSD_CONV - 1:SSD_CONV, :] * x
        for k in range(SSD_CONV - 1):
            sh = SSD_CONV - 1 - k
            xs = jnp.where(t >= sh, pltpu.roll(x, sh, 0), 0.0)
            acc = acc + w_ref[k:k + 1, :] * xs
        o_ref[...] = acc

    return pl.pallas_call(
        body, name=name, grid=(B, nc),
        in_specs=[pl.BlockSpec((S, LANE), lambda b, j: (b, c0 + j)),
                  pl.BlockSpec((SUBLANE, LANE), lambda b, j: (0, j)),
                  pl.BlockSpec((1, LANE), lambda b, j: (0, j))] + extra_specs,
        out_specs=pl.BlockSpec((S, LANE), lambda b, j: (b, j)),
        out_shape=jax.ShapeDtypeStruct((T, SSD_CONV_DIM), F32),
        compiler_params=_cp(("parallel", "parallel")),
    )(proj, cw, cb, *extra)


def _conv_bwd(dpre, proj, cw, B, S, name):
    T = B * S
    nc = SSD_CONV_DIM // LANE
    c0 = XBC0 // LANE

    def body(d_ref, x_ref, w_ref, dx_ref, dwb_ref):
        @pl.when(pl.program_id(1) == 0)
        def _():
            dwb_ref[...] = jnp.zeros_like(dwb_ref)

        d = d_ref[...]
        x = x_ref[...]
        t = lax.broadcasted_iota(jnp.int32, (S, 1), 0)
        dx = w_ref[SSD_CONV - 1:SSD_CONV, :] * d
        rows = [None] * SUBLANE
        rows[SSD_CONV - 1] = jnp.sum(d * x, axis=0, keepdims=True)
        for k in range(SSD_CONV - 1):
            sh = SSD_CONV - 1 - k
            dx = dx + w_ref[k:k + 1, :] * jnp.where(t < S - sh, pltpu.roll(d, S - sh, 0), 0.0)
            xs = jnp.where(t >= sh, pltpu.roll(x, sh, 0), 0.0)
            rows[k] = jnp.sum(d * xs, axis=0, keepdims=True)
        rows[SSD_CONV] = jnp.sum(d, axis=0, keepdims=True)
        dx_ref[...] = dx.astype(BF16)
        r = lax.broadcasted_iota(jnp.int32, (SUBLANE, LANE), 0)
        upd = jnp.zeros((SUBLANE, LANE), F32)
        for k in range(SSD_CONV + 1):
            upd = upd + jnp.where(r == k, rows[k], 0.0)
        dwb_ref[...] += upd

    return pl.pallas_call(
        body, name=name, grid=(nc, B),
        in_specs=[pl.BlockSpec((S, LANE), lambda j, b: (b, j)),
                  pl.BlockSpec((S, LANE), lambda j, b: (b, c0 + j)),
                  pl.BlockSpec((SUBLANE, LANE), lambda j, b: (0, j))],
        out_specs=[pl.BlockSpec((S, LANE), lambda j, b: (b, j)),
                   pl.BlockSpec((SUBLANE, LANE), lambda j, b: (0, j))],
        out_shape=[jax.ShapeDtypeStruct((T, SSD_CONV_DIM), BF16),
                   jax.ShapeDtypeStruct((SUBLANE, SSD_CONV_DIM), F32)],
        compiler_params=_cp(("parallel", "arbitrary")),
    )(dpre, proj, cw)


def _ssd_consts():
    e = np.zeros((LANE, SSD_W), np.float32)
    p = np.zeros((SUBLANE, SSD_W), np.float32)
    for h in range(SSD_HEADS):
        e[h, HEAD_DIM * h:HEAD_DIM * (h + 1)] = 1.0
        p[h, HEAD_DIM * h] = 1.0
    return jnp.asarray(e), jnp.asarray(p)


def _ssd_chunk(pre, z, dtr, sprev, par, e_mat, psel):
    L = CHUNK
    xc = _silu(pre)
    xs, bm, cm = xc[:, :SSD_W], xc[:, SSD_W:SSD_W + 2 * SSD_STATE], xc[:, SSD_W + 2 * SSD_STATE:]
    dtb, alog, dskip, ng = par[0:1], par[1:2], par[2:3], par[3:4]
    dt = _softplus(_dot(dtr, e_mat, HI) + dtb)
    a = dt * (-jnp.exp(alog))
    X = xs * dt
    ri = lax.broadcasted_iota(jnp.int32, (L, L), 0)
    ci = lax.broadcasted_iota(jnp.int32, (L, L), 1)
    tril = ri >= ci
    acs = _dot(tril.astype(F32), a, HI)
    acs_t = _dot_nt(psel, acs, HI)
    ecs = jnp.exp(acs)
    alast = acs[L - 1:L, :]
    xd = (X * jnp.exp(alast - acs)).astype(BF16)
    xb = X.astype(BF16)
    col = lax.broadcasted_iota(jnp.int32, (1, SSD_W), 1)
    sb = sprev.astype(BF16)
    bgs = [bm[:, SSD_STATE * g:SSD_STATE * (g + 1)].astype(BF16) for g in range(2)]
    cgs = [cm[:, SSD_STATE * g:SSD_STATE * (g + 1)].astype(BF16) for g in range(2)]
    cbs = [_dot_nt(cgs[g], bgs[g]) for g in range(2)]
    first = lax.broadcasted_iota(jnp.int32, (1, LANE), 1) < HEAD_DIM
    y_tiles, s_tiles = [], []
    for t in range(SSD_W // LANE):
        cl = slice(LANE * t, LANE * (t + 1))
        xb_t, xd_t, sb_t = xb[:, cl], xd[:, cl], sb[:, cl]
        per_head = []
        for h in (2 * t, 2 * t + 1):
            seg = acs[:, HEAD_DIM * h:HEAD_DIM * h + 1] - acs_t[h:h + 1, :]
            dec = jnp.exp(jnp.where(tril, seg, NEG))
            per_head.append(_dot((cbs[h // 3] * dec).astype(BF16), xb_t))
        y_t = jnp.where(first, per_head[0], per_head[1])
        ga, gb = (2 * t) // 3, (2 * t + 1) // 3
        if ga == gb:
            y_off, s_add = _dot(cgs[ga], sb_t), _dot_tn(bgs[ga], xd_t)
        else:
            y_off = jnp.where(first, _dot(cgs[ga], sb_t), _dot(cgs[gb], sb_t))
            s_add = jnp.where(first, _dot_tn(bgs[ga], xd_t), _dot_tn(bgs[gb], xd_t))
        y_tiles.append(y_t + y_off * ecs[:, cl])
        s_tiles.append(s_add)
    y = dskip * xs + jnp.concatenate(y_tiles, axis=1)
    snew = sprev * jnp.exp(alast) + jnp.concatenate(s_tiles, axis=1)
    yg = y * _silu(z)
    sq = yg * yg
    g0 = col < SSD_W // 2
    ms0 = jnp.sum(jnp.where(g0, sq, 0.0), axis=-1, keepdims=True) * (2.0 / SSD_W)
    ms1 = jnp.sum(jnp.where(g0, 0.0, sq), axis=-1, keepdims=True) * (2.0 / SSD_W)
    r = jnp.where(g0, lax.rsqrt(ms0 + RMS_EPS), lax.rsqrt(ms1 + RMS_EPS))
    return yg * r * ng, snew


SSD_CHUNKS_PER_STEP = 4


def _ssd_chunks_per_step(S):
    k = SSD_CHUNKS_PER_STEP
    while (S // CHUNK) % k:
        k //= 2
    return k


def _ssd_fwd(pre, proj, par, B, S, name):
    T = B * S
    k = _ssd_chunks_per_step(S)
    nc, rows = S // (CHUNK * k), CHUNK * k
    e_mat, psel = _ssd_consts()

    def body(pre_ref, z_ref, dt_ref, par_ref, e_ref, p_ref, y_ref, sall_ref, st):
        @pl.when(pl.program_id(1) == 0)
        def _():
            st[...] = jnp.zeros_like(st)

        sprev = st[...]
        for i in range(k):
            r = slice(CHUNK * i, CHUNK * (i + 1))
            sall_ref[i] = sprev
            y, sprev = _ssd_chunk(pre_ref[r, :], z_ref[r, :], dt_ref[r, :], sprev, par_ref[...], e_ref[...],
                                  p_ref[...])
            y_ref[r, :] = y.astype(BF16)
        st[...] = sprev

    row = lambda b, c: b * nc + c
    full = lambda shp: pl.BlockSpec(shp, lambda b, c: (0, 0))
    return pl.pallas_call(
        body, name=name, grid=(B, nc),
        in_specs=[pl.BlockSpec((rows, SSD_CONV_DIM), lambda b, c: (row(b, c), 0)),
                  pl.BlockSpec((rows, SSD_W), lambda b, c: (row(b, c), Z0 // SSD_W)),
                  pl.BlockSpec((rows, LANE), lambda b, c: (row(b, c), DT0 // LANE)),
                  full((SUBLANE, SSD_W)), full((LANE, SSD_W)), full((SUBLANE, SSD_W))],
        out_specs=[pl.BlockSpec((rows, SSD_W), lambda b, c: (row(b, c), 0)),
                   pl.BlockSpec((k, SSD_STATE, SSD_W), lambda b, c: (row(b, c), 0, 0))],
        out_shape=[jax.ShapeDtypeStruct((T, SSD_W), BF16),
                   jax.ShapeDtypeStruct((B * nc * k, SSD_STATE, SSD_W), F32)],
        scratch_shapes=[pltpu.VMEM((SSD_STATE, SSD_W), F32)],
        compiler_params=_cp(("parallel", "arbitrary")),
    )(pre, proj, proj, par, e_mat, psel)


def _ssd_bwd(pre, proj, sall, dy, par, B, S, name):
    T = B * S
    k = _ssd_chunks_per_step(S)
    nc, rows = S // (CHUNK * k), CHUNK * k
    e_mat, psel = _ssd_consts()

    def body(pre_ref, z_ref, dt_ref, sall_ref, dy_ref, par_ref, e_ref, p_ref,
             dpre_ref, dz_ref, ddt_ref, dpar_ref, ds):
        b, c = pl.program_id(0), pl.program_id(1)

        @pl.when(c == 0)
        def _():
            ds[...] = jnp.zeros_like(ds)

        @pl.when((b == 0) & (c == 0))
        def _():
            dpar_ref[...] = jnp.zeros_like(dpar_ref)

        e_v, p_v = e_ref[...], p_ref[...]
        fn = lambda pre, z, dtr, sprev, par: _ssd_chunk(pre, z, dtr, sprev, par, e_v, p_v)
        dstate, dpar_sum = ds[...], None
        for i in reversed(range(k)):
            r = slice(CHUNK * i, CHUNK * (i + 1))
            _, vjp = jax.vjp(fn, pre_ref[r, :], z_ref[r, :], dt_ref[r, :], sall_ref[i], par_ref[...])
            dpre, dz, ddt, dstate, dpar = vjp((dy_ref[r, :], dstate))
            dpre_ref[r, :] = dpre
            dz_ref[r, :] = dz.astype(BF16)
            ddt_ref[r, :] = ddt.astype(BF16)
            dpar_sum = dpar if dpar_sum is None else dpar_sum + dpar
        dpar_ref[...] += dpar_sum
        ds[...] = dstate

    row = lambda b, c: b * nc + (nc - 1 - c)
    full = lambda shp: pl.BlockSpec(shp, lambda b, c: (0, 0))
    return pl.pallas_call(
        body, name=name, grid=(B, nc),
        in_specs=[pl.BlockSpec((rows, SSD_CONV_DIM), lambda b, c: (row(b, c), 0)),
                  pl.BlockSpec((rows, SSD_W), lambda b, c: (row(b, c), Z0 // SSD_W)),
                  pl.BlockSpec((rows, LANE), lambda b, c: (row(b, c), DT0 // LANE)),
                  pl.BlockSpec((k, SSD_STATE, SSD_W), lambda b, c: (row(b, c), 0, 0)),
                  pl.BlockSpec((rows, SSD_W), lambda b, c: (row(b, c), ATT_W // SSD_W)),
                  full((SUBLANE, SSD_W)), full((LANE, SSD_W)), full((SUBLANE, SSD_W))],
        out_specs=[pl.BlockSpec((rows, SSD_CONV_DIM), lambda b, c: (row(b, c), 0)),
                   pl.BlockSpec((rows, SSD_W), lambda b, c: (row(b, c), 0)),
                   pl.BlockSpec((rows, LANE), lambda b, c: (row(b, c), 0)),
                   full((SUBLANE, SSD_W))],
        out_shape=[jax.ShapeDtypeStruct((T, SSD_CONV_DIM), F32),
                   jax.ShapeDtypeStruct((T, SSD_W), BF16),
                   jax.ShapeDtypeStruct((T, LANE), BF16),
                   jax.ShapeDtypeStruct((SUBLANE, SSD_W), F32)],
        scratch_shapes=[pltpu.VMEM((SSD_STATE, SSD_W), F32)],
        compiler_params=_cp(("arbitrary", "arbitrary")),
    )(pre, proj, proj, sall, dy, par, e_mat, psel)


def _sgu_consts():
    e = np.zeros((SUBLANE, SGU_W), np.float32)
    for g in range(SGU_GROUPS):
        e[g, HEAD_DIM * g:HEAD_DIM * (g + 1)] = 1.0
    return jnp.asarray(e)


def _sgu_chunk(u_raw, v_raw, ln, w, bst, e4):
    L = CHUNK
    u = _gelu(u_raw)
    v = _gelu(v_raw)
    mu = jnp.mean(v, axis=-1, keepdims=True)
    vc = v - mu
    var = jnp.mean(vc * vc, axis=-1, keepdims=True)
    vn = vc * lax.rsqrt(var + LN_EPS) * ln[0:1] + ln[1:2]
    vb = vn.astype(BF16)
    ri = lax.broadcasted_iota(jnp.int32, (L, L), 0)
    ci = lax.broadcasted_iota(jnp.int32, (L, L), 1)
    tril = ri >= ci
    col = lax.broadcasted_iota(jnp.int32, (1, SGU_W), 1)
    mixed = _dot(bst, e4, HI)
    for g in range(SGU_GROUPS):
        wc = jnp.where(tril, w[g], 0.0).astype(BF16)
        gm = (col >= HEAD_DIM * g) & (col < HEAD_DIM * (g + 1))
        mixed = mixed + jnp.where(gm, _dot(wc, vb), 0.0)
    return u * mixed


def _sgu_fwd(proj, ln, w, bst, B, S, name, after=None):
    T = B * S
    e4 = _sgu_consts()
    extra, extra_specs = _after(after)
    k = max(d for d in (4, 2, 1) if (T // CHUNK) % d == 0)
    rows = k * CHUNK

    def body(u_ref, v_ref, ln_ref, w_ref, b_ref, e_ref, *rest):
        y_ref = rest[-1]
        for i in range(k):
            r = slice(CHUNK * i, CHUNK * (i + 1))
            y_ref[r, :] = _sgu_chunk(u_ref[r, :], v_ref[r, :], ln_ref[...], w_ref[...], b_ref[...],
                                     e_ref[...]).astype(BF16)

    return pl.pallas_call(
        body, name=name, grid=(T // rows,),
        in_specs=[pl.BlockSpec((rows, SGU_W), lambda i: (i, U0 // SGU_W)),
                  pl.BlockSpec((rows, SGU_W), lambda i: (i, VS0 // SGU_W)),
                  pl.BlockSpec((SUBLANE, SGU_W), lambda i: (0, 0)),
                  pl.BlockSpec((SGU_GROUPS, CHUNK, CHUNK), lambda i: (0, 0, 0)),
                  pl.BlockSpec((CHUNK, SUBLANE), lambda i: (0, 0)),
                  pl.BlockSpec((SUBLANE, SGU_W), lambda i: (0, 0))] + extra_specs,
        out_specs=pl.BlockSpec((rows, SGU_W), lambda i: (i, 0)),
        out_shape=jax.ShapeDtypeStruct((T, SGU_W), BF16),
        compiler_params=_cp(("parallel",)),
    )(proj, proj, ln, w, bst, e4, *extra)


def _sgu_bwd(proj, dy, ln, w, bst, B, S, name):
    T = B * S
    e4 = _sgu_consts()
    ycol = (ATT_W + SSD_W) // SGU_W
    k = max(d for d in (4, 2, 1) if (T // CHUNK) % d == 0)
    rows = k * CHUNK

    def body(u_ref, v_ref, dy_ref, ln_ref, w_ref, b_ref, e_ref, du_ref, dv_ref, dln_ref, dw_ref, db_ref):
        @pl.when(pl.program_id(0) == 0)
        def _():
            dln_ref[...] = jnp.zeros_like(dln_ref)
            dw_ref[...] = jnp.zeros_like(dw_ref)
            db_ref[...] = jnp.zeros_like(db_ref)

        e_v = e_ref[...]
        fn = lambda u, v, ln, w, b: _sgu_chunk(u, v, ln, w, b, e_v)
        acc = None
        for i in range(k):
            r = slice(CHUNK * i, CHUNK * (i + 1))
            _, vjp = jax.vjp(fn, u_ref[r, :], v_ref[r, :], ln_ref[...], w_ref[...], b_ref[...])
            du, dv, *dpar = vjp(dy_ref[r, :])
            du_ref[r, :] = du.astype(BF16)
            dv_ref[r, :] = dv.astype(BF16)
            acc = dpar if acc is None else [a + d for a, d in zip(acc, dpar)]
        dln_ref[...] += acc[0]
        dw_ref[...] += acc[1]
        db_ref[...] += acc[2]

    c_ln = pl.BlockSpec((SUBLANE, SGU_W), lambda i: (0, 0))
    c_w = pl.BlockSpec((SGU_GROUPS, CHUNK, CHUNK), lambda i: (0, 0, 0))
    c_b = pl.BlockSpec((CHUNK, SUBLANE), lambda i: (0, 0))
    return pl.pallas_call(
        body, name=name, grid=(T // rows,),
        in_specs=[pl.BlockSpec((rows, SGU_W), lambda i: (i, U0 // SGU_W)),
                  pl.BlockSpec((rows, SGU_W), lambda i: (i, VS0 // SGU_W)),
                  pl.BlockSpec((rows, SGU_W), lambda i: (i, ycol)),
                  c_ln, c_w, c_b, pl.BlockSpec((SUBLANE, SGU_W), lambda i: (0, 0))],
        out_specs=[pl.BlockSpec((rows, SGU_W), lambda i: (i, 0)),
                   pl.BlockSpec((rows, SGU_W), lambda i: (i, 0)), c_ln, c_w, c_b],
        out_shape=[jax.ShapeDtypeStruct((T, SGU_W), BF16), jax.ShapeDtypeStruct((T, SGU_W), BF16),
                   jax.ShapeDtypeStruct((SUBLANE, SGU_W), F32),
                   jax.ShapeDtypeStruct((SGU_GROUPS, CHUNK, CHUNK), F32),
                   jax.ShapeDtypeStruct((CHUNK, SUBLANE), F32)],
        compiler_params=_cp(("arbitrary",)),
    )(proj, proj, dy, ln, w, bst, e4)


_HBM = pl.BlockSpec(memory_space=pltpu.HBM)
_SEM = pl.BlockSpec(memory_space=pltpu.SEMAPHORE)
_ANY = pl.BlockSpec(memory_space=pl.ANY)
_EFFECT = pltpu.SideEffectType.DATAFLOW_SIDE_EFFECTING


def _peers():
    x, y, c = lax.axis_index("x"), lax.axis_index("y"), lax.axis_index("c")
    out = []
    for p in range(1, N_DEV):
        px, py, pc = x ^ ((p >> 2) & 1), y ^ ((p >> 1) & 1), c ^ (p & 1)
        out.append(((px, py, pc), 4 * px + 2 * py + pc))
    return 4 * x + 2 * y + c, out


def _xchg_start(xs, a2a, order, name):
    n = len(xs)
    lands = [lax.empty(a.shape if f else (N_DEV,) + a.shape, a.dtype) for a, f in zip(xs, a2a)]

    def body(*refs):
        ins, zones = refs[:n], refs[n:2 * n]
        send_sems, recv_sems = refs[2 * n + 1], refs[2 * n + 2]
        token = refs[-1]
        me, peers = _peers()
        for p, (dev, peer) in enumerate(peers):
            for t in range(n):
                pltpu.make_async_remote_copy(
                    src_ref=ins[t].at[peer] if a2a[t] else ins[t], dst_ref=zones[t].at[me],
                    send_sem=send_sems.at[p * n + t], recv_sem=recv_sems.at[p * n + t],
                    device_id=dev, device_id_type=MESH).start()
        token[...] = jnp.zeros_like(token)

    hbm = lambda a: pltpu.HBM(a.shape, a.dtype)
    sems = pltpu.SemaphoreType.DMA(((N_DEV - 1) * n,))
    out = pl.pallas_call(
        body, name=name,
        in_specs=[_HBM] * (2 * n) + [_ANY],
        out_specs=[_SEM, _SEM] + [_HBM] * (2 * n) + [pl.BlockSpec(memory_space=pltpu.VMEM)],
        out_shape=[sems, sems] + [hbm(a) for a in xs] + [hbm(a) for a in lands]
        + [jax.ShapeDtypeStruct((SUBLANE, LANE), F32)],
        input_output_aliases={t: 2 + t for t in range(2 * n)},
        compiler_params=pltpu.CompilerParams(has_side_effects=_EFFECT),
    )(*[pltpu.with_memory_space_constraint(a, pltpu.HBM) for a in list(xs) + list(lands)], order)
    return out[0], out[1], out[2:2 + n], out[2 + n:2 + 2 * n], out[-1]


def _xchg_wait(started, a2a, after, name):
    send_sems, recv_sems, xs, lands, _ = started
    n = len(xs)

    def body(*refs):
        ins, zones = refs[:n], refs[n:2 * n]
        send_s, recv_s = refs[2 * n], refs[2 * n + 1]
        me, peers = _peers()
        cps = []
        for p, (dev, peer) in enumerate(peers):
            for t in range(n):
                cps.append(pltpu.make_async_remote_copy(
                    src_ref=ins[t].at[peer] if a2a[t] else ins[t], dst_ref=zones[t].at[peer],
                    send_sem=send_s.at[p * n + t], recv_sem=recv_s.at[p * n + t],
                    device_id=dev, device_id_type=MESH))
        for cp in cps:
            cp.wait_recv()
        for cp in cps:
            cp.wait_send()

    hbm = lambda a: pltpu.HBM(a.shape, a.dtype)
    out = pl.pallas_call(
        body, name=name,
        in_specs=[_HBM] * (2 * n) + [_SEM, _SEM, _ANY],
        out_specs=[_HBM] * (2 * n),
        out_shape=[hbm(a) for a in xs] + [hbm(a) for a in lands],
        input_output_aliases={t: t for t in range(2 * n)},
        compiler_params=pltpu.CompilerParams(has_side_effects=_EFFECT),
    )(*xs, *lands, send_sems, recv_sems, after)
    return out[:n], out[n:]


def _chip_peers():
    x, y, c = lax.axis_index("x"), lax.axis_index("y"), lax.axis_index("c")
    chips = [(1 - x, y), (x, 1 - y), (1 - x, 1 - y)]
    slot = lambda px, py, pc: 4 * px + 2 * py + pc
    return (x, y, c), chips, slot


def _gather_start(xs, order, name):
    n = len(xs)
    lands = [lax.empty((N_DEV,) + a.shape, a.dtype) for a in xs]

    def body(*refs):
        ins, zones = refs[:n], refs[n:2 * n]
        send_sems, d2d_sems, ici_sems = refs[2 * n + 1:2 * n + 4]
        token = refs[-1]
        (x, y, c), chips, slot = _chip_peers()
        me = slot(x, y, c)
        for t in range(n):
            pltpu.make_async_copy(ins[t], zones[t].at[me], d2d_sems.at[n + t]).start()
            for j, (px, py) in enumerate(chips):
                pltpu.make_async_remote_copy(
                    src_ref=ins[t], dst_ref=zones[t].at[me], send_sem=send_sems.at[(1 + j) * n + t],
                    recv_sem=ici_sems.at[j * n + t], device_id=(px, py, c), device_id_type=MESH).start()
            pltpu.make_async_remote_copy(
                src_ref=ins[t], dst_ref=zones[t].at[me], send_sem=send_sems.at[t],
                recv_sem=d2d_sems.at[t], device_id=(x, y, 1 - c), device_id_type=MESH).start()
        token[...] = jnp.zeros_like(token)

    hbm = lambda a: pltpu.HBM(a.shape, a.dtype)
    dma = lambda k: pltpu.SemaphoreType.DMA((k,))
    out = pl.pallas_call(
        body, name=name,
        in_specs=[_HBM] * (2 * n) + [_ANY],
        out_specs=[_SEM, _SEM, _SEM] + [_HBM] * (2 * n) + [pl.BlockSpec(memory_space=pltpu.VMEM)],
        out_shape=[dma(4 * n), dma(2 * n), dma(3 * n)] + [hbm(a) for a in xs] + [hbm(a) for a in lands]
        + [jax.ShapeDtypeStruct((SUBLANE, LANE), F32)],
        input_output_aliases={t: 3 + t for t in range(2 * n)},
        compiler_params=pltpu.CompilerParams(has_side_effects=_EFFECT),
    )(*[pltpu.with_memory_space_constraint(a, pltpu.HBM) for a in list(xs) + list(lands)], order)
    return dict(send=out[0], d2d=out[1], ici=out[2], xs=out[3:3 + n], lands=out[3 + n:3 + 2 * n], token=out[-1])


def _gather_relay(st, after, name):
    n = len(st["xs"])

    def body(*refs):
        zones, ici_sems = refs[:n], refs[n]
        fsend, frecv = refs[n + 2], refs[n + 3]
        token = refs[-1]
        (x, y, c), chips, slot = _chip_peers()
        for t in range(n):
            for j, (px, py) in enumerate(chips):
                blk = zones[t].at[slot(px, py, c)]
                fwd = pltpu.make_async_remote_copy(
                    src_ref=blk, dst_ref=blk, send_sem=fsend.at[j * n + t], recv_sem=ici_sems.at[j * n + t],
                    device_id=(x, y, 1 - c), device_id_type=MESH)
                fwd.wait_recv()
                pltpu.make_async_remote_copy(
                    src_ref=blk, dst_ref=blk, send_sem=fsend.at[j * n + t], recv_sem=frecv.at[j * n + t],
                    device_id=(x, y, 1 - c), device_id_type=MESH).start()
        token[...] = jnp.zeros_like(token)

    hbm = lambda a: pltpu.HBM(a.shape, a.dtype)
    dma = lambda k: pltpu.SemaphoreType.DMA((k,))
    out = pl.pallas_call(
        body, name=name,
        in_specs=[_HBM] * n + [_SEM, _ANY],
        out_specs=[_SEM, _SEM] + [_HBM] * n + [pl.BlockSpec(memory_space=pltpu.VMEM)],
        out_shape=[dma(3 * n), dma(3 * n)] + [hbm(a) for a in st["lands"]]
        + [jax.ShapeDtypeStruct((SUBLANE, LANE), F32)],
        input_output_aliases={t: 2 + t for t in range(n)},
        compiler_params=pltpu.CompilerParams(has_side_effects=_EFFECT),
    )(*st["lands"], st["ici"], after)
    return dict(st, fsend=out[0], frecv=out[1], lands=out[2:2 + n], token=out[-1])


def _gather_wait(st, after, name):
    n = len(st["xs"])

    def body(*refs):
        ins, zones = refs[:n], refs[n:2 * n]
        send_sems, d2d_sems, fsend, frecv = refs[2 * n:2 * n + 4]
        (x, y, c), chips, slot = _chip_peers()
        sib = (x, y, 1 - c)
        for t in range(n):
            pltpu.make_async_copy(ins[t], zones[t].at[slot(x, y, c)], d2d_sems.at[n + t]).wait()
            mine = lambda s, r, dst: pltpu.make_async_remote_copy(
                src_ref=ins[t], dst_ref=dst, send_sem=s, recv_sem=r, device_id=sib, device_id_type=MESH)
            direct = mine(send_sems.at[t], d2d_sems.at[t], zones[t].at[slot(x, y, 1 - c)])
            direct.wait_recv()
            direct.wait_send()
            for j, (px, py) in enumerate(chips):
                mine(send_sems.at[(1 + j) * n + t], d2d_sems.at[t], zones[t].at[slot(px, py, c)]).wait_send()
                relayed = mine(fsend.at[j * n + t], frecv.at[j * n + t], zones[t].at[slot(px, py, 1 - c)])
                relayed.wait_recv()
                relayed.wait_send()

    hbm = lambda a: pltpu.HBM(a.shape, a.dtype)
    out = pl.pallas_call(
        body, name=name,
        in_specs=[_HBM] * (2 * n) + [_SEM] * 4 + [_ANY],
        out_specs=[_HBM] * (2 * n),
        out_shape=[hbm(a) for a in st["xs"]] + [hbm(a) for a in st["lands"]],
        input_output_aliases={t: t for t in range(2 * n)},
        compiler_params=pltpu.CompilerParams(has_side_effects=_EFFECT),
    )(*st["xs"], *st["lands"], st["send"], st["d2d"], st["fsend"], st["frecv"], after)
    return out[:n], out[n:]


def _cast_layers(pairs, name):
    def body(*refs):
        n = len(refs) // 2
        for i in range(n):
            refs[n + i][...] = refs[i][...].astype(BF16)

    in_specs = [pl.BlockSpec((None,) + w.shape[1:], functools.partial(lambda l, i: (l, 0, 0), l),
                             pipeline_mode=pl.Buffered(1)) for w, l in pairs]
    return pl.pallas_call(
        body, name=name, grid=(1,), in_specs=in_specs,
        out_specs=[pl.BlockSpec(w.shape[1:], lambda i: (0, 0)) for w, _ in pairs],
        out_shape=[jax.ShapeDtypeStruct(w.shape[1:], BF16) for w, _ in pairs],
        compiler_params=_cp(("arbitrary",)),
    )(*[w for w, _ in pairs])


ADAMW_BLOCK_ELEMS = 256 * 1024


def _adam_step(me, w, m, v, parts_ref, mine, out_refs):
    g = None
    for p in range(N_DEV):
        term = jnp.where(me == p, mine.astype(F32), parts_ref[p].astype(F32))
        g = term if g is None else g + term
    mn = ADAM_B1 * m + (1.0 - ADAM_B1) * g
    vn = ADAM_B2 * v + (1.0 - ADAM_B2) * (g * g)
    m_hat = mn / (1.0 - ADAM_B1 ** ADAM_STEP)
    v_hat = vn / (1.0 - ADAM_B2 ** ADAM_STEP)
    g_ref, d_ref, mo_ref, vo_ref = out_refs
    g_ref[...] = g
    d_ref[...] = -ADAM_LR * (m_hat / (jnp.sqrt(v_hat) + ADAM_EPS) + ADAM_WD * w)
    mo_ref[...] = mn
    vo_ref[...] = vn


def _adamw(me, w, m, v, parts, own, name, layer=0, into=None):
    L, R, C = w.shape
    P = parts.shape[0]
    tr = R
    t = 16
    while t <= R:
        if R % t == 0 and t * C <= ADAMW_BLOCK_ELEMS:
            tr = t
        t += 16
    if tr == R and R * C > ADAMW_BLOCK_ELEMS and R % 16 == 0:
        tr = 16
    own_all = own.shape[0] == P

    def body(me_ref, w_ref, m_ref, v_ref, p_ref, own_ref, *rest):
        _adam_step(me_ref[0], w_ref[...], m_ref[...], v_ref[...], p_ref, own_ref[...], rest[-4:])

    blk = pl.BlockSpec((None, tr, C), lambda i, me_ref: (layer, i, 0))
    own_blk = pl.BlockSpec((None, tr, C), lambda i, me_ref: (me_ref[0] if own_all else 0, i, 0))
    prev = list(into) if into is not None else []
    return pl.pallas_call(
        body, name=name,
        grid_spec=pltpu.PrefetchScalarGridSpec(
            num_scalar_prefetch=1, grid=(R // tr,),
            in_specs=[blk, blk, blk, pl.BlockSpec((P, tr, C), lambda i, me_ref: (0, i, 0)), own_blk]
            + [_ANY] * len(prev),
            out_specs=[blk] * 4),
        out_shape=[jax.ShapeDtypeStruct((L, R, C), F32)] * 4,
        input_output_aliases={6 + i: i for i in range(len(prev))},
        compiler_params=_cp(("parallel",)),
    )(me, w, m, v, parts, own, *prev)


def _perm_cols(w):
    pad = jnp.zeros(w.shape[:-1] + (LANE - SSD_HEADS,), w.dtype)
    return jnp.concatenate([w[..., 0:1536], w[..., 2438:2694], w[..., 1536:2432], w[..., 2432:2438], pad,
                            w[..., 2694:2950]], axis=-1)


def _unperm_cols(w):
    return jnp.concatenate([w[..., 0:1536], w[..., XBC0:XBC0 + SSD_CONV_DIM], w[..., DT0:DT0 + SSD_HEADS],
                            w[..., U0:U0 + SGU_W], w[..., VS0:VS0 + SGU_W]], axis=-1)


_SMALL = ("ffn1_norm", "mix_norm", "conv_w", "conv_b", "dt_bias", "a_log", "d_skip", "ssd_norm",
          "sgu_ln_g", "sgu_ln_b", "sgu_w", "sgu_b", "ffn2_norm", "final_norm", "loss")


_SMALL_LAST = ("ffn1_norm",)
_SMALL_EARLY = tuple(k for k in _SMALL if k not in _SMALL_LAST)


def _pack(d, names):
    v = jnp.concatenate([d[k].astype(F32).reshape(-1) for k in names])
    n = v.shape[0]
    npad = -(-n // (LANE * 16)) * (LANE * 16)
    return jnp.pad(v, (0, npad - n)).reshape(npad // LANE, LANE)


def _unpack(p, shapes, names):
    v = p.reshape(-1)
    out, o = {}, 0
    for k in names:
        n = int(np.prod(shapes[k]))
        out[k] = v[o:o + n].reshape(shapes[k])
        o += n
    return out


def kernel(x, ffn1_norm, ffn1_w_gate, ffn1_w_up, ffn1_w_down, mix_norm, w_in, conv_w, conv_b, dt_bias, a_log, d_skip, ssd_norm, sgu_ln_g, sgu_ln_b, sgu_w, sgu_b, w_out, ffn2_norm, ffn2_w_gate, ffn2_w_up, ffn2_w_down, final_norm, loss_target, m_ffn1_norm, m_ffn1_w_gate, m_ffn1_w_up, m_ffn1_w_down, m_mix_norm, m_w_in, m_conv_w, m_conv_b, m_dt_bias, m_a_log, m_d_skip, m_ssd_norm, m_sgu_ln_g, m_sgu_ln_b, m_sgu_w, m_sgu_b, m_w_out, m_ffn2_norm, m_ffn2_w_gate, m_ffn2_w_up, m_ffn2_w_down, m_final_norm, v_ffn1_norm, v_ffn1_w_gate, v_ffn1_w_up, v_ffn1_w_down, v_mix_norm, v_w_in, v_conv_w, v_conv_b, v_dt_bias, v_a_log, v_d_skip, v_ssd_norm, v_sgu_ln_g, v_sgu_ln_b, v_sgu_w, v_sgu_b, v_w_out, v_ffn2_norm, v_ffn2_w_gate, v_ffn2_w_up, v_ffn2_w_down, v_final_norm):
    B, S, D = x.shape
    T = B * S
    L = ffn1_norm.shape[0]
    me = 4 * lax.axis_index("x") + 2 * lax.axis_index("y") + lax.axis_index("c")
    cs = conv_w.shape[2]
    W = dict(ffn1_norm=ffn1_norm, ffn1_w_gate=ffn1_w_gate, ffn1_w_up=ffn1_w_up, ffn1_w_down=ffn1_w_down,
             mix_norm=mix_norm, w_in=w_in, conv_w=conv_w, conv_b=conv_b, dt_bias=dt_bias, a_log=a_log,
             d_skip=d_skip, ssd_norm=ssd_norm, sgu_ln_g=sgu_ln_g, sgu_ln_b=sgu_ln_b, sgu_w=sgu_w, sgu_b=sgu_b,
             w_out=w_out, ffn2_norm=ffn2_norm, ffn2_w_gate=ffn2_w_gate, ffn2_w_up=ffn2_w_up,
             ffn2_w_down=ffn2_w_down, final_norm=final_norm)
    M = dict(ffn1_norm=m_ffn1_norm, ffn1_w_gate=m_ffn1_w_gate, ffn1_w_up=m_ffn1_w_up, ffn1_w_down=m_ffn1_w_down,
             mix_norm=m_mix_norm, w_in=m_w_in, conv_w=m_conv_w, conv_b=m_conv_b, dt_bias=m_dt_bias, a_log=m_a_log,
             d_skip=m_d_skip, ssd_norm=m_ssd_norm, sgu_ln_g=m_sgu_ln_g, sgu_ln_b=m_sgu_ln_b, sgu_w=m_sgu_w,
             sgu_b=m_sgu_b, w_out=m_w_out, ffn2_norm=m_ffn2_norm, ffn2_w_gate=m_ffn2_w_gate,
             ffn2_w_up=m_ffn2_w_up, ffn2_w_down=m_ffn2_w_down, final_norm=m_final_norm)
    V = dict(ffn1_norm=v_ffn1_norm, ffn1_w_gate=v_ffn1_w_gate, ffn1_w_up=v_ffn1_w_up, ffn1_w_down=v_ffn1_w_down,
             mix_norm=v_mix_norm, w_in=v_w_in, conv_w=v_conv_w, conv_b=v_conv_b, dt_bias=v_dt_bias, a_log=v_a_log,
             d_skip=v_d_skip, ssd_norm=v_ssd_norm, sgu_ln_g=v_sgu_ln_g, sgu_ln_b=v_sgu_ln_b, sgu_w=v_sgu_w,
             sgu_b=v_sgu_b, w_out=v_w_out, ffn2_norm=v_ffn2_norm, ffn2_w_gate=v_ffn2_w_gate,
             ffn2_w_up=v_ffn2_w_up, ffn2_w_down=v_ffn2_w_down, final_norm=v_final_norm)
    FFN1 = ("ffn1_w_gate", "ffn1_w_up", "ffn1_w_down")
    FFN2 = ("ffn2_w_gate", "ffn2_w_up", "ffn2_w_down")
    MIX = ("w_in", "w_out")
    big = FFN1 + MIX + FFN2
    col_sharded = lambda k: k.endswith("w_gate") or k.endswith("w_up")
    for dct in (W, M, V):
        for k in big:
            if col_sharded(k):
                dct[k] = jnp.swapaxes(dct[k], 1, 2)
        dct["w_in"] = _perm_cols(dct["w_in"])

    wgroups = [[(k, 0) for k in FFN1], [("w_in", 0), ("conv_w", None)], [("w_out", 0)] + [(k, 0) for k in FFN2]]
    for l in range(1, L):
        wgroups += [[(k, l) for k in FFN1] + [("w_in", l)], [("w_out", l)] + [(k, l) for k in FFN2]]
    wstarted, order = [], x
    later = [kl for grp in wgroups[1:] for kl in grp if kl[0] != "conv_w"]
    cast = dict(zip(wgroups[0], _cast_layers([(W[k], l) for k, l in wgroups[0]], "cast_first")))
    for gi, grp in enumerate(wgroups):
        if gi == 1:
            first = lax.optimization_barrier((W[later[0][0]], order))[0]
            srcs = [(first if i == 0 else W[k], l) for i, (k, l) in enumerate(later)]
            cast.update(zip(later, _cast_layers(srcs, "cast_rest")))
        xs = [conv_w if k == "conv_w" else cast[(k, l)] for k, l in grp]
        st = _gather_start(xs, order, f"gather_start_{gi}")
        order = st["token"]
        wstarted.append(st)
    G = {}

    zero1 = jnp.zeros((1,), F32)
    W["loss"], M["loss"], V["loss"] = zero1, zero1, zero1
    full_shapes = {k: (W[k].shape if k != "conv_w" else (L, SSD_CONV, SSD_CONV_DIM)) for k in _SMALL}
    embed = lambda a, k: a if k != "conv_w" else lax.dynamic_update_slice(
        jnp.zeros(full_shapes[k], F32), a, (0, 0, me * cs))
    small_packs = {names: [_pack({k: embed(d[k], k) for k in names}, names)[None] for d in (W, M, V)]
                   for names in (_SMALL_EARLY, _SMALL_LAST)}

    def relay(gi, after):
        wstarted[gi] = _gather_relay(wstarted[gi], after, f"gather_relay_{gi}")
        return wstarted[gi]["token"]

    def gathered(gi, after):
        _, lands = _gather_wait(wstarted[gi], after, f"gather_wait_{gi}")
        G.update(zip(wgroups[gi], lands))

    def rows(k, l):
        a = G[(k, l)]
        return a.reshape(-1, a.shape[-1])

    bias = _attn_bias(S, min(256, S))
    row1 = lambda a: a.reshape(1, -1)

    def ffn1_params(l):
        return dict(g1=row1(ffn1_norm[l]), wg1=rows("ffn1_w_gate", l), wu1=rows("ffn1_w_up", l),
                    wd1=rows("ffn1_w_down", l))

    def out_params(l):
        return dict(wout=rows("w_out", l), g2=row1(ffn2_norm[l]), wg2=rows("ffn2_w_gate", l),
                    wu2=rows("ffn2_w_up", l), wd2=rows("ffn2_w_down", l))

    def mix_params(l):
        cw = jnp.transpose(G[("conv_w", None)][:, l], (1, 0, 2)).reshape(SSD_CONV, -1)
        return dict(
            gm=row1(mix_norm[l]), win=rows("w_in", l),
            cw=jnp.pad(cw, ((0, SUBLANE - SSD_CONV), (0, 0))), cb=row1(conv_b[l]),
            par=jnp.pad(jnp.stack([jnp.repeat(dt_bias[l], HEAD_DIM), jnp.repeat(a_log[l], HEAD_DIM),
                                   jnp.repeat(d_skip[l], HEAD_DIM), ssd_norm[l]]), ((0, SUBLANE - 4), (0, 0))),
            ln=jnp.pad(jnp.stack([sgu_ln_g[l], sgu_ln_b[l]]), ((0, SUBLANE - 2), (0, 0))),
            sw=sgu_w[l], bst=jnp.pad(sgu_b[l].T, ((0, 0), (0, SUBLANE - SGU_GROUPS))))

    xc = x.reshape(T, D)
    saved, lay = [], []
    for l in range(L):
        if l == 0:
            gathered(0, relay(0, order))
        else:
            gathered(1 + 2 * l, xc)
        p = ffn1_params(l)
        x1, *ffn1_saved = _ffn_fwd(xc, p["g1"], p["wg1"], p["wu1"], p["wd1"], f"ffn1_fwd_{l}")
        if l == 0:
            gathered(1, relay(1, x1))
        p.update(mix_params(l))
        lay.append(p)
        proj, ht = _norm_mm(x1, p["gm"], p["win"], f"in_proj_{l}")
        o_att, lse = _attn_fwd(proj, bias, B, S, f"attn_fwd_{l}")
        tok = relay(2 + 2 * l, o_att)
        pre = _conv_fwd(proj, p["cw"], p["cb"], B, S, f"conv_fwd_{l}", after=tok)
        y_ssd, sall = _ssd_fwd(pre, proj, p["par"], B, S, f"ssd_fwd_{l}")
        y_sgu = _sgu_fwd(proj, p["ln"], p["sw"], p["bst"], B, S, f"sgu_fwd_{l}", after=tok)
        ycat = jnp.concatenate([o_att.astype(BF16), y_ssd, y_sgu], axis=1)
        gathered(2 + 2 * l, ycat)
        p.update(out_params(l))
        x2 = _mm(ycat, p["wout"], "nn", f"out_proj_{l}", residual=x1)
        tok = relay(3 + 2 * l, x2) if l + 1 < L else None
        x3, *ffn2_saved = _ffn_fwd(x2, p["g2"], p["wg2"], p["wu2"], p["wd2"], f"ffn2_fwd_{l}", after=tok)
        saved.append(dict(x0=xc, ffn1=ffn1_saved, x1=x1, ht=ht, proj=proj, o_att=o_att, lse=lse, pre=pre,
                          sall=sall, ycat=ycat, x2=x2, ffn2=ffn2_saved))
        xc = x3
    loss_part, dx, dgf = _final_loss(xc, row1(final_norm), loss_target.reshape(T, D), "final_loss")

    gl = [dict() for _ in range(L)]
    gstarted, gorder = [], [order]

    def to_blocks(k, a):
        return a.reshape(N_DEV, -1, a.shape[-1]).astype(BF16)

    def send_grads(keys, l, extra, tag, small_names=None):
        xs = [to_blocks(k, gl[l][k]) for k in keys] + extra
        flags = [True] * len(keys) + [False] * len(extra)
        st = _xchg_start(xs, flags, gorder[0], f"grads_start_{tag}")
        gorder[0] = st[-1]
        gstarted.append((keys, l, st, flags, tag, small_names))

    def small_grads(names):
        sm = {}
        for k in names:
            if k == "final_norm":
                sm[k] = dgf.reshape(-1)
            elif k == "loss":
                sm[k] = loss_part[0, :1]
            else:
                sm[k] = jnp.stack([gl[l][k] for l in range(L)])
        return [_pack(sm, names)]

    def behind(a):
        return lax.optimization_barrier((a, gorder[0]))[0]

    for l in reversed(range(L)):
        p, s, g = lay[l], saved[l], gl[l]
        gfac, ufac, actt = s["ffn2"]
        dx2, dgt, dut, xn, dacc, g["ffn2_norm"] = _ffn_bwd_dx(
            dx, s["x2"], p["g2"], gfac, ufac, p["wg2"], p["wu2"], p["wd2"], f"ffn2_bwd_{l}")
        g["ffn2_w_gate"], g["ffn2_w_up"], g["ffn2_w_down"] = _ffn_dw(dgt, dut, actt, xn, dacc, f"ffn2_dw_{l}")
        if l == 0:
            send_grads(FFN2, 0, [], "l0f")
            dx2 = behind(dx2)
        dycat = _mm(dx2, p["wout"], "nt", f"out_proj_dx_{l}")
        g["w_out"] = _mm(s["ycat"], dx2, "tn", f"out_proj_dw_{l}", out_dtype=BF16, tm_cap=1024, tk_cap=512)
        dq, dk, dv = _attn_bwd(s["proj"], s["o_att"], s["lse"], dycat, bias, B, S, f"attn_bwd_{l}")
        dpre, dz, ddt, dpar = _ssd_bwd(s["pre"], s["proj"], s["sall"], dycat, p["par"], B, S, f"ssd_bwd_{l}")
        dxbc, dwb = _conv_bwd(dpre, s["proj"], p["cw"], B, S, f"conv_bwd_{l}")
        du, dvs, dln, dsw, dbst = _sgu_bwd(s["proj"], dycat, p["ln"], p["sw"], p["bst"], B, S, f"sgu_bwd_{l}")
        hsum = lambda r: r.reshape(SSD_HEADS, HEAD_DIM).sum(-1)
        g["conv_w"], g["conv_b"] = dwb[:SSD_CONV], dwb[SSD_CONV]
        g["dt_bias"], g["a_log"], g["d_skip"], g["ssd_norm"] = hsum(dpar[0]), hsum(dpar[1]), hsum(dpar[2]), dpar[3]
        g["sgu_ln_g"], g["sgu_ln_b"], g["sgu_w"], g["sgu_b"] = dln[0], dln[1], dsw, dbst[:, :SGU_GROUPS].T
        dproj = jnp.concatenate([dq, dk, dv, dz, du, dxbc, ddt, dvs], axis=1)
        g["w_in"] = _mm_resident_lhs(s["ht"], dproj, f"in_proj_dw_{l}")
        dx1, g["mix_norm"] = _norm_mm_bwd(dproj, s["x1"], p["gm"], p["win"], dx2, f"in_proj_bwd_{l}")
        if l == 0:
            send_grads(MIX, 0, small_grads(_SMALL_EARLY), "l0a", _SMALL_EARLY)
            dx1, small_packs = lax.optimization_barrier((behind(dx1), small_packs))
        gfac, ufac, actt = s["ffn1"]
        dx, dgt, dut, xn, dacc, g["ffn1_norm"] = _ffn_bwd_dx(
            dx1, s["x0"], p["g1"], gfac, ufac, p["wg1"], p["wu1"], p["wd1"], f"ffn1_bwd_{l}")
        if l > 0:
            g["ffn1_w_gate"], g["ffn1_w_up"], g["ffn1_w_down"] = _ffn_dw(dgt, dut, actt, xn, dacc,
                                                                        f"ffn1_dw_{l}")
            send_grads(big, l, [], f"l{l}")
            dx = behind(dx)
        else:
            g["ffn1_w_gate"] = _dw_one(dgt, xn, "ffn1_dwg_0")
            send_grads(("ffn1_w_gate",), 0, [], "l0b1")
            g["ffn1_w_up"] = _dw_one(dut, xn, "ffn1_dwu_0", after=gorder[0])
            send_grads(("ffn1_w_up",), 0, [], "l0b2")
            g["ffn1_w_down"] = _dw_one(actt, dacc, "ffn1_dwd_0", after=gorder[0])
    grad_x = dx.reshape(B, S, D)
    send_grads(("ffn1_w_down",), 0, small_grads(_SMALL_LAST), "l0b", _SMALL_LAST)

    res, after = {}, gorder[0]
    small_out = [dict() for _ in range(4)]
    me1 = me.reshape(1).astype(jnp.int32)
    for keys, l, st, flags, tag, names in gstarted:
        own, lands = _xchg_wait(st, flags, after, f"grads_wait_{tag}")
        for k, mine, pk in zip(keys, own, lands):
            res[k] = _adamw(me1, W[k], M[k], V[k], pk, mine, f"adamw_{k}_{l}", layer=l, into=res.get(k))
        done = [res[k][0] for k in keys]
        if names:
            outs = _adamw(me1, *small_packs[names], lands[-1], own[-1][None], f"adamw_small_{tag}")
            for d, o in zip(small_out, outs):
                u = _unpack(o, full_shapes, names)
                if "conv_w" in u:
                    u["conv_w"] = lax.dynamic_slice(u["conv_w"], (0, 0, me * cs), (L, SSD_CONV, cs))
                d.update(u)
                done.extend(u.values())
        after = lax.optimization_barrier(tuple(done))[0]
    back = lambda k, a: jnp.swapaxes(a, 1, 2) if col_sharded(k) else _unperm_cols(a) if k == "w_in" else a
    grads, deltas, new_m, new_v = [dict({k: back(k, res[k][i]) for k in big}, **small_out[i]) for i in range(4)]

    names = ("ffn1_norm", "ffn1_w_gate", "ffn1_w_up", "ffn1_w_down", "mix_norm", "w_in", "conv_w", "conv_b",
             "dt_bias", "a_log", "d_skip", "ssd_norm", "sgu_ln_g", "sgu_ln_b", "sgu_w", "sgu_b", "w_out",
             "ffn2_norm", "ffn2_w_gate", "ffn2_w_up", "ffn2_w_down", "final_norm")
    loss = grads["loss"][0]
    return (loss, grad_x, *[grads[n] for n in names], *[deltas[n] for n in names],
            *[new_m[n] for n in names], *[new_v[n] for n in names])
```

```python
import functools

import numpy as np
import jax
import jax.numpy as jnp
from jax import lax
from jax.experimental import pallas as pl
from jax.experimental.pallas import tpu as pltpu

F32, BF16 = jnp.float32, jnp.bfloat16
HI = lax.Precision.HIGH
MESH = pl.DeviceIdType.MESH
N_DEV = 8
VMEM_LIMIT_BYTES = 56 * 1024 * 1024
LANE, SUBLANE = 128, 8

HEAD_DIM = 64
ATT_W = 384
SSD_W = 384
SSD_HEADS = 6
SSD_STATE = 128
SSD_CONV = 4
CHUNK = 128
SSD_CONV_DIM = 896
SGU_W = 256
SGU_GROUPS = 4
D_IN = 2950
RMS_EPS = 1e-6
LN_EPS = 1e-5
NEG = -1e30

PW = 3072
Q0, K0, V0, Z0, U0, XBC0, DT0, VS0 = 0, 384, 768, 1152, 1536, 1792, 2688, 2816

ADAM_LR, ADAM_B1, ADAM_B2, ADAM_EPS, ADAM_WD, ADAM_STEP = 0.001, 0.9, 0.999, 1e-08, 0.01, 10


def _cp(sem=None):
    return pltpu.CompilerParams(dimension_semantics=sem, vmem_limit_bytes=VMEM_LIMIT_BYTES)


def _tile(n, cap, mult=LANE):
    best = None
    t = mult
    while t <= min(n, cap):
        if n % t == 0:
            best = t
        t += mult
    return best if best is not None else n


def _dot(a, b, prec=None):
    return jnp.dot(a, b, preferred_element_type=F32, precision=prec)


def _dot_nt(a, b, prec=None):
    return lax.dot_general(a, b, (((1,), (1,)), ((), ())), preferred_element_type=F32, precision=prec)


def _dot_tn(a, b, prec=None):
    return lax.dot_general(a, b, (((0,), (0,)), ((), ())), preferred_element_type=F32, precision=prec)


def _sigmoid(x):
    return 1.0 / (1.0 + jnp.exp(-x))


def _silu(x):
    return x * _sigmoid(x)


def _gelu(x):
    return 0.5 * x * (1.0 + lax.erf(x * 0.7071067811865476))


def _softplus(x):
    return jnp.maximum(x, 0.0) + jnp.log(1.0 + jnp.exp(-jnp.abs(x)))


def _rms_fwd(x, g):
    rstd = lax.rsqrt(jnp.mean(x * x, axis=-1, keepdims=True) + RMS_EPS)
    xhat = x * rstd
    return xhat * g, xhat, rstd


def _rms_bwd(dy, xhat, rstd, g):
    dxhat = dy * g
    dx = rstd * (dxhat - xhat * jnp.mean(dxhat * xhat, axis=-1, keepdims=True))
    return dx, dy * xhat


def _resident(shape):
    return pl.BlockSpec(shape, lambda *_: (0,) * len(shape), pipeline_mode=pl.Buffered(1))


def _mm(a, b, mode, name, out_dtype=F32, residual=None, tm_cap=1024, tn_cap=1024, tk_cap=1024):
    if mode == "nn":
        (M, K), (_, N) = a.shape, b.shape
    elif mode == "nt":
        (M, K), (N, _) = a.shape, b.shape
    else:
        (K, M), (_, N) = a.shape, b.shape
    tm, tn, tk = _tile(M, tm_cap), _tile(N, tn_cap), _tile(K, tk_cap)
    nk = K // tk
    if mode == "tn":
        a_spec = pl.BlockSpec((tk, tm), lambda i, j, k: (k, i))
    else:
        a_spec = pl.BlockSpec((tm, tk), lambda i, j, k: (i, k))
    if mode == "nt":
        b_spec = pl.BlockSpec((tn, tk), lambda i, j, k: (j, k))
    else:
        b_spec = pl.BlockSpec((tk, tn), lambda i, j, k: (k, j))
    o_spec = pl.BlockSpec((tm, tn), lambda i, j, k: (i, j))
    has_res = residual is not None

    def prod(a_ref, b_ref):
        av = a_ref[...].astype(BF16)
        bv = b_ref[...].astype(BF16)
        if mode == "nn":
            return _dot(av, bv)
        if mode == "nt":
            return _dot_nt(av, bv)
        return _dot_tn(av, bv)

    def body(*refs):
        a_ref, b_ref = refs[:2]
        r_ref = refs[2] if has_res else None
        o_ref = refs[2 + has_res]
        if nk == 1:
            o = prod(a_ref, b_ref)
            if has_res:
                o = r_ref[...] + o
            o_ref[...] = o.astype(out_dtype)
            return
        acc = refs[3 + has_res]
        k = pl.program_id(2)

        @pl.when(k == 0)
        def _():
            acc[...] = jnp.zeros_like(acc)

        acc[...] += prod(a_ref, b_ref)

        @pl.when(k == nk - 1)
        def _():
            o = acc[...]
            if has_res:
                o = r_ref[...] + o
            o_ref[...] = o.astype(out_dtype)

    ins = [a, b] + ([residual] if has_res else [])
    in_specs = [a_spec, b_spec] + ([o_spec] if has_res else [])
    return pl.pallas_call(
        body, name=name, grid=(M // tm, N // tn, nk),
        in_specs=in_specs, out_specs=o_spec,
        out_shape=jax.ShapeDtypeStruct((M, N), out_dtype),
        scratch_shapes=[pltpu.VMEM((tm, tn), F32)] if nk > 1 else [],
        compiler_params=_cp(("parallel", "parallel", "arbitrary")),
    )(*ins)


def _after(after):
    return ([after], [_ANY]) if after is not None else ([], [])


def _ffn_fwd(x, g, wgt, wut, wd, name, after=None):
    T, D = x.shape
    F = wgt.shape[0]
    tm = _tile(T, 256)

    def body(x_ref, g_ref, wg_ref, wu_ref, wd_ref, *rest):
        out_ref, dgf_ref, sl_ref, actt_ref = rest[-4:]
        xv = x_ref[...]
        xn = _rms_fwd(xv, g_ref[...])[0].astype(BF16)
        gate = _dot_nt(xn, wg_ref[...])
        up = _dot_nt(xn, wu_ref[...])
        sig = _sigmoid(gate)
        sl = gate * sig
        dgf_ref[...] = (up * (sig + sl * (1.0 - sig))).astype(BF16)
        sl_ref[...] = sl.astype(BF16)
        act = (sl * up).astype(BF16)
        actt_ref[...] = act.T
        out_ref[...] = xv + 0.5 * _dot(act, wd_ref[...])

    row = lambda w: pl.BlockSpec((tm, w), lambda i: (i, 0))
    extra, extra_specs = _after(after)
    return pl.pallas_call(
        body, name=name, grid=(T // tm,),
        in_specs=[row(D), _resident((1, D)), _resident((F, D)), _resident((F, D)), _resident((F, D))] + extra_specs,
        out_specs=[row(D), row(F), row(F), pl.BlockSpec((F, tm), lambda i: (0, i))],
        out_shape=[jax.ShapeDtypeStruct((T, D), F32),
                   jax.ShapeDtypeStruct((T, F), BF16),
                   jax.ShapeDtypeStruct((T, F), BF16),
                   jax.ShapeDtypeStruct((F, T), BF16)],
        compiler_params=_cp(("parallel",)),
    )(x, g, wgt, wut, wd, *extra)


def _ffn_bwd_dx(dout, x, g, dgf, sl, wg, wu, wd, name):
    T, D = x.shape
    F = wg.shape[0]
    tm = _tile(T, 256)

    def body(dout_ref, x_ref, g_ref, dgf_ref, sl_ref, wg_ref, wu_ref, wd_ref,
             dx_ref, dgt_ref, dut_ref, xn_ref, dacc_ref, dg_ref):
        @pl.when(pl.program_id(0) == 0)
        def _():
            dg_ref[...] = jnp.zeros_like(dg_ref)

        gv = g_ref[...]
        dout_v = dout_ref[...]
        xn, xhat, rstd = _rms_fwd(x_ref[...], gv)
        xn_ref[...] = xn.astype(BF16)
        dacc = (0.5 * dout_v).astype(BF16)
        dacc_ref[...] = dacc
        dact = _dot_nt(dacc, wd_ref[...])
        dgate = (dact * dgf_ref[...].astype(F32)).astype(BF16)
        dup = (dact * sl_ref[...].astype(F32)).astype(BF16)
        dgt_ref[...] = dgate.T
        dut_ref[...] = dup.T
        dxn = _dot(dgate, wg_ref[...]) + _dot(dup, wu_ref[...])
        dx, dgrow = _rms_bwd(dxn, xhat, rstd, gv)
        dx_ref[...] = dout_v + dx
        dg_ref[...] += jnp.sum(dgrow, axis=0, keepdims=True)

    row = lambda w: pl.BlockSpec((tm, w), lambda i: (i, 0))
    tr = pl.BlockSpec((F, tm), lambda i: (0, i))
    return pl.pallas_call(
        body, name=name, grid=(T // tm,),
        in_specs=[row(D), row(D), _resident((1, D)), row(F), row(F),
                  _resident((F, D)), _resident((F, D)), _resident((F, D))],
        out_specs=[row(D), tr, tr, row(D), row(D), pl.BlockSpec((1, D), lambda i: (0, 0))],
        out_shape=[jax.ShapeDtypeStruct((T, D), F32)] + [jax.ShapeDtypeStruct((F, T), BF16)] * 2
        + [jax.ShapeDtypeStruct((T, D), BF16)] * 2 + [jax.ShapeDtypeStruct((1, D), F32)],
        compiler_params=_cp(("arbitrary",)),
    )(dout, x, g, dgf, sl, wg, wu, wd)


def _ffn_dw(dgt, dut, actt, xn, dacc, name):
    F, T = dgt.shape
    D = xn.shape[1]
    th = _tile(F, 256)

    def body(dg_ref, du_ref, a_ref, xn_ref, dacc_ref, dwg_ref, dwu_ref, dwd_ref):
        xv = xn_ref[...]
        dwg_ref[...] = _dot(dg_ref[...], xv).astype(BF16)
        dwu_ref[...] = _dot(du_ref[...], xv).astype(BF16)
        dwd_ref[...] = _dot(a_ref[...], dacc_ref[...]).astype(BF16)

    tile = pl.BlockSpec((th, T), lambda j: (j, 0))
    out = pl.BlockSpec((th, D), lambda j: (j, 0))
    return pl.pallas_call(
        body, name=name, grid=(F // th,),
        in_specs=[tile, tile, tile, _resident((T, D)), _resident((T, D))],
        out_specs=[out, out, out], out_shape=[jax.ShapeDtypeStruct((F, D), BF16)] * 3,
        compiler_params=_cp(("parallel",)),
    )(dgt, dut, actt, xn, dacc)


def _dw_one(lt, r, name, after=None):
    F, T = lt.shape
    D = r.shape[1]
    th = _tile(F, 256)
    extra, extra_specs = _after(after)

    def body(l_ref, r_ref, *rest):
        rest[-1][...] = _dot(l_ref[...], r_ref[...]).astype(BF16)

    return pl.pallas_call(
        body, name=name, grid=(F // th,),
        in_specs=[pl.BlockSpec((th, T), lambda j: (j, 0)), _resident((T, D))] + extra_specs,
        out_specs=pl.BlockSpec((th, D), lambda j: (j, 0)),
        out_shape=jax.ShapeDtypeStruct((F, D), BF16),
        compiler_params=_cp(("parallel",)),
    )(lt, r, *extra)


def _norm_mm(x, g, w, name):
    T, D = x.shape
    N = w.shape[1]
    tm = _tile(T, 512)

    def body(x_ref, g_ref, w_ref, o_ref, ht_ref):
        xn = _rms_fwd(x_ref[...], g_ref[...])[0]
        ht_ref[...] = xn.T.astype(BF16)
        o_ref[...] = _dot(xn.astype(BF16), w_ref[...])

    return pl.pallas_call(
        body, name=name, grid=(T // tm,),
        in_specs=[pl.BlockSpec((tm, D), lambda i: (i, 0)), _resident((1, D)), _resident((D, N))],
        out_specs=[pl.BlockSpec((tm, N), lambda i: (i, 0)), pl.BlockSpec((D, tm), lambda i: (0, i))],
        out_shape=[jax.ShapeDtypeStruct((T, N), F32), jax.ShapeDtypeStruct((D, T), BF16)],
        compiler_params=_cp(("parallel",)),
    )(x, g, w)


def _norm_mm_bwd(dproj, x, g, w, dres, name):
    T, D = x.shape
    N = w.shape[1]
    tm = _tile(T, 512)

    def body(dp_ref, x_ref, g_ref, w_ref, dres_ref, dx_ref, dg_ref):
        @pl.when(pl.program_id(0) == 0)
        def _():
            dg_ref[...] = jnp.zeros_like(dg_ref)

        gv = g_ref[...]
        dh = _dot_nt(dp_ref[...], w_ref[...])
        _, xhat, rstd = _rms_fwd(x_ref[...], gv)
        dx, dgrow = _rms_bwd(dh, xhat, rstd, gv)
        dx_ref[...] = dres_ref[...] + dx
        dg_ref[...] += jnp.sum(dgrow, axis=0, keepdims=True)

    row = pl.BlockSpec((tm, D), lambda i: (i, 0))
    one = pl.BlockSpec((1, D), lambda i: (0, 0))
    return pl.pallas_call(
        body, name=name, grid=(T // tm,),
        in_specs=[pl.BlockSpec((tm, N), lambda i: (i, 0)), row, _resident((1, D)), _resident((D, N)), row],
        out_specs=[row, one],
        out_shape=[jax.ShapeDtypeStruct((T, D), F32), jax.ShapeDtypeStruct((1, D), F32)],
        compiler_params=_cp(("arbitrary",)),
    )(dproj, x, g, w, dres)


def _mm_resident_lhs(at, b, name, tn_cap=512):
    M, K = at.shape
    N = b.shape[1]
    tn = _tile(N, tn_cap)

    def body(a_ref, b_ref, o_ref):
        o_ref[...] = _dot(a_ref[...], b_ref[...]).astype(BF16)

    return pl.pallas_call(
        body, name=name, grid=(N // tn,),
        in_specs=[_resident((M, K)), pl.BlockSpec((K, tn), lambda j: (0, j))],
        out_specs=pl.BlockSpec((M, tn), lambda j: (0, j)),
        out_shape=jax.ShapeDtypeStruct((M, N), BF16),
        compiler_params=_cp(("parallel",)),
    )(at, b)


def _final_loss(x, g, target, name):
    T, D = x.shape
    tm = _tile(T, 512)

    def body(x_ref, g_ref, t_ref, loss_ref, dx_ref, dg_ref):
        @pl.when(pl.program_id(0) == 0)
        def _():
            dg_ref[...] = jnp.zeros_like(dg_ref)
            loss_ref[...] = jnp.zeros_like(loss_ref)

        gv = g_ref[...]
        y, xhat, rstd = _rms_fwd(x_ref[...], gv)
        err = y - t_ref[...]
        part = 0.5 * jnp.sum(jnp.mean(err * err, axis=-1, keepdims=True), axis=0, keepdims=True)
        loss_ref[...] += jnp.broadcast_to(part, loss_ref.shape)
        dy = err * (1.0 / D)
        dx, dgrow = _rms_bwd(dy, xhat, rstd, gv)
        dx_ref[...] = dx
        dg_ref[...] += jnp.sum(dgrow, axis=0, keepdims=True)

    row = pl.BlockSpec((tm, D), lambda i: (i, 0))
    one = pl.BlockSpec((1, D), lambda i: (0, 0))
    return pl.pallas_call(
        body, name=name, grid=(T // tm,),
        in_specs=[row, one, row],
        out_specs=[pl.BlockSpec((1, LANE), lambda i: (0, 0)), row, one],
        out_shape=[jax.ShapeDtypeStruct((1, LANE), F32), jax.ShapeDtypeStruct((T, D), F32),
                   jax.ShapeDtypeStruct((1, D), F32)],
        compiler_params=_cp(("arbitrary",)),
    )(x, g, target)


def _attn_bias(S, bq):
    d = np.arange(bq)[:, None] - np.arange(S)[None, :] + (S // bq - 1) * bq
    ok = d >= 0
    mult = ((ok & (d <= 128)).astype(np.float32) + (ok & (d % 4 == 0) & (d <= 512))
            + (ok & (d % 16 == 0) & (d <= 2048)))
    return jnp.asarray(np.where(mult > 0, np.log(np.maximum(mult, 1.0)), NEG).astype(np.float32))


def _attn_fwd(proj, bias, B, S, name):
    T = B * S
    bq = bias.shape[0]
    nb = S // bq
    qcol, kcol, vcol = Q0 // LANE, K0 // LANE, V0 // LANE

    def body(q_ref, k_ref, v_ref, t_ref, o_ref, lse_ref, ks, vs):
        for hh in range(2):
            sl = slice(HEAD_DIM * hh, HEAD_DIM * (hh + 1))
            ks[hh] = k_ref[:, sl].astype(BF16)
            vs[hh] = v_ref[:, sl].astype(BF16)
        for hh in range(2):
            sl = slice(HEAD_DIM * hh, HEAD_DIM * (hh + 1))
            for qb in range(nb):
                w, off, rows = bq * (qb + 1), (nb - 1 - qb) * bq, slice(qb * bq, (qb + 1) * bq)
                q = (q_ref[rows, sl] * 0.125).astype(BF16)
                s = _dot_nt(q, ks[hh, 0:w, :]) + t_ref[:, off:off + w]
                m = jnp.max(s, axis=-1, keepdims=True)
                p = jnp.exp(s - m)
                l = jnp.sum(p, axis=-1, keepdims=True)
                o_ref[rows, sl] = _dot(p.astype(BF16), vs[hh, 0:w, :]) / l
                lse_ref[rows, hh:hh + 1] = m + jnp.log(l)

    blk = lambda c0: pl.BlockSpec((S, LANE), lambda b, p: (b, c0 + p))
    return pl.pallas_call(
        body, name=name, grid=(B, ATT_W // LANE),
        in_specs=[blk(qcol), blk(kcol), blk(vcol), _resident((bq, S))],
        out_specs=[pl.BlockSpec((S, LANE), lambda b, p: (b, p)),
                   pl.BlockSpec((None, None, S, 2), lambda b, p: (b, p, 0, 0))],
        out_shape=[jax.ShapeDtypeStruct((T, ATT_W), F32),
                   jax.ShapeDtypeStruct((B, ATT_W // LANE, S, 2), F32)],
        scratch_shapes=[pltpu.VMEM((2, S, HEAD_DIM), BF16)] * 2,
        compiler_params=_cp(("parallel", "parallel")),
    )(proj, proj, proj, bias)


def _attn_bwd(proj, o, lse, dy, bias, B, S, name):
    T = B * S
    bq = bias.shape[0]
    nb = S // bq
    qcol, kcol, vcol = Q0 // LANE, K0 // LANE, V0 // LANE

    def body(q_ref, k_ref, v_ref, o_ref, lse_ref, do_ref, t_ref, dq_ref, dk_ref, dv_ref, ks, vs, dks, dvs):
        for hh in range(2):
            sl = slice(HEAD_DIM * hh, HEAD_DIM * (hh + 1))
            ks[hh] = k_ref[:, sl].astype(BF16)
            vs[hh] = v_ref[:, sl].astype(BF16)
        dks[...] = jnp.zeros_like(dks)
        dvs[...] = jnp.zeros_like(dvs)
        for hh in range(2):
            sl = slice(HEAD_DIM * hh, HEAD_DIM * (hh + 1))
            for qb in range(nb):
                w, off, rows = bq * (qb + 1), (nb - 1 - qb) * bq, slice(qb * bq, (qb + 1) * bq)
                q = (q_ref[rows, sl] * 0.125).astype(BF16)
                do = do_ref[rows, sl]
                dob = do.astype(BF16)
                delta = jnp.sum(do * o_ref[rows, sl], axis=-1, keepdims=True)
                k, v = ks[hh, 0:w, :], vs[hh, 0:w, :]
                s = _dot_nt(q, k) + t_ref[:, off:off + w]
                p = jnp.exp(s - lse_ref[rows, hh:hh + 1])
                ds = (p * (_dot_nt(dob, v) - delta)).astype(BF16)
                dq_ref[rows, sl] = (_dot(ds, k) * 0.125).astype(dq_ref.dtype)
                dks[hh, 0:w, :] += _dot_tn(ds, q)
                dvs[hh, 0:w, :] += _dot_tn(p.astype(BF16), dob)
            dk_ref[:, sl] = dks[hh].astype(dk_ref.dtype)
            dv_ref[:, sl] = dvs[hh].astype(dv_ref.dtype)

    blk = lambda c0: pl.BlockSpec((S, LANE), lambda b, p: (b, c0 + p))
    own = pl.BlockSpec((S, LANE), lambda b, p: (b, p))
    return pl.pallas_call(
        body, name=name, grid=(B, ATT_W // LANE),
        in_specs=[blk(qcol), blk(kcol), blk(vcol), own,
                  pl.BlockSpec((None, None, S, 2), lambda b, p: (b, p, 0, 0)), own, _resident((bq, S))],
        out_specs=[own, own, own],
        out_shape=[jax.ShapeDtypeStruct((T, ATT_W), BF16)] * 3,
        scratch_shapes=[pltpu.VMEM((2, S, HEAD_DIM), BF16)] * 2 + [pltpu.VMEM((2, S, HEAD_DIM), F32)] * 2,
        compiler_params=_cp(("parallel", "parallel")),
    )(proj, proj, proj, o, lse, dy, bias)


def _conv_fwd(proj, cw, cb, B, S, name, after=None):
    T = B * S
    nc = SSD_CONV_DIM // LANE
    c0 = XBC0 // LANE
    extra, extra_specs = _after(after)

    def body(x_ref, w_ref, b_ref, *rest):
        o_ref = rest[-1]
        x = x_ref[...]
        t = lax.broadcasted_iota(jnp.int32, (S, 1), 0)
        acc = b_ref[...] + w_ref[SSD_CONV - 1:SSD_CONV, :] * x
        for k in range(SSD_CONV - 1):
            sh = SSD_CONV - 1 - k
            xs = jnp.where(t >= sh, pltpu.roll(x, sh, 0), 0.0)
            acc = acc + w_ref[k:k + 1, :] * xs
        o_ref[...] = acc

    return pl.pallas_call(
        body, name=name, grid=(B, nc),
        in_specs=[pl.BlockSpec((S, LANE), lambda b, j: (b, c0 + j)),
                  pl.BlockSpec((SUBLANE, LANE), lambda b, j: (0, j)),
                  pl.BlockSpec((1, LANE), lambda b, j: (0, j))] + extra_specs,
        out_specs=pl.BlockSpec((S, LANE), lambda b, j: (b, j)),
        out_shape=jax.ShapeDtypeStruct((T, SSD_CONV_DIM), F32),
        compiler_params=_cp(("parallel", "parallel")),
    )(proj, cw, cb, *extra)


def _conv_bwd(dpre, proj, cw, B, S, name):
    T = B * S
    nc = SSD_CONV_DIM // LANE
    c0 = XBC0 // LANE

    def body(d_ref, x_ref, w_ref, dx_ref, dwb_ref):
        @pl.when(pl.program_id(1) == 0)
        def _():
            dwb_ref[...] = jnp.zeros_like(dwb_ref)

        d = d_ref[...]
        x = x_ref[...]
        t = lax.broadcasted_iota(jnp.int32, (S, 1), 0)
        dx = w_ref[SSD_CONV - 1:SSD_CONV, :] * d
        rows = [None] * SUBLANE
        rows[SSD_CONV - 1] = jnp.sum(d * x, axis=0, keepdims=True)
        for k in range(SSD_CONV - 1):
            sh = SSD_CONV - 1 - k
            dx = dx + w_ref[k:k + 1, :] * jnp.where(t < S - sh, pltpu.roll(d, S - sh, 0), 0.0)
            xs = jnp.where(t >= sh, pltpu.roll(x, sh, 0), 0.0)
            rows[k] = jnp.sum(d * xs, axis=0, keepdims=True)
        rows[SSD_CONV] = jnp.sum(d, axis=0, keepdims=True)
        dx_ref[...] = dx.astype(BF16)
        r = lax.broadcasted_iota(jnp.int32, (SUBLANE, LANE), 0)
        upd = jnp.zeros((SUBLANE, LANE), F32)
        for k in range(SSD_CONV + 1):
            upd = upd + jnp.where(r == k, rows[k], 0.0)
        dwb_ref[...] += upd

    return pl.pallas_call(
        body, name=name, grid=(nc, B),
        in_specs=[pl.BlockSpec((S, LANE), lambda j, b: (b, j)),
                  pl.BlockSpec((S, LANE), lambda j, b: (b, c0 + j)),
                  pl.BlockSpec((SUBLANE, LANE), lambda j, b: (0, j))],
        out_specs=[pl.BlockSpec((S, LANE), lambda j, b: (b, j)),
                   pl.BlockSpec((SUBLANE, LANE), lambda j, b: (0, j))],
        out_shape=[jax.ShapeDtypeStruct((T, SSD_CONV_DIM), BF16),
                   jax.ShapeDtypeStruct((SUBLANE, SSD_CONV_DIM), F32)],
        compiler_params=_cp(("parallel", "arbitrary")),
    )(dpre, proj, cw)


def _ssd_consts():
    e = np.zeros((LANE, SSD_W), np.float32)
    p = np.zeros((SUBLANE, SSD_W), np.float32)
    for h in range(SSD_HEADS):
        e[h, HEAD_DIM * h:HEAD_DIM * (h + 1)] = 1.0
        p[h, HEAD_DIM * h] = 1.0
    return jnp.asarray(e), jnp.asarray(p)


def _ssd_chunk(pre, z, dtr, sprev, par, e_mat, psel):
    L = CHUNK
    xc = _silu(pre)
    xs, bm, cm = xc[:, :SSD_W], xc[:, SSD_W:SSD_W + 2 * SSD_STATE], xc[:, SSD_W + 2 * SSD_STATE:]
    dtb, alog, dskip, ng = par[0:1], par[1:2], par[2:3], par[3:4]
    dt = _softplus(_dot(dtr, e_mat, HI) + dtb)
    a = dt * (-jnp.exp(alog))
    X = xs * dt
    ri = lax.broadcasted_iota(jnp.int32, (L, L), 0)
    ci = lax.broadcasted_iota(jnp.int32, (L, L), 1)
    tril = ri >= ci
    acs = _dot(tril.astype(F32), a, HI)
    acs_t = _dot_nt(psel, acs, HI)
    ecs = jnp.exp(acs)
    alast = acs[L - 1:L, :]
    xd = (X * jnp.exp(alast - acs)).astype(BF16)
    xb = X.astype(BF16)
    col = lax.broadcasted_iota(jnp.int32, (1, SSD_W), 1)
    sb = sprev.astype(BF16)
    bgs = [bm[:, SSD_STATE * g:SSD_STATE * (g + 1)].astype(BF16) for g in range(2)]
    cgs = [cm[:, SSD_STATE * g:SSD_STATE * (g + 1)].astype(BF16) for g in range(2)]
    cbs = [_dot_nt(cgs[g], bgs[g]) for g in range(2)]
    first = lax.broadcasted_iota(jnp.int32, (1, LANE), 1) < HEAD_DIM
    y_tiles, s_tiles = [], []
    for t in range(SSD_W // LANE):
        cl = slice(LANE * t, LANE * (t + 1))
        xb_t, xd_t, sb_t = xb[:, cl], xd[:, cl], sb[:, cl]
        per_head = []
        for h in (2 * t, 2 * t + 1):
            seg = acs[:, HEAD_DIM * h:HEAD_DIM * h + 1] - acs_t[h:h + 1, :]
            dec = jnp.exp(jnp.where(tril, seg, NEG))
            per_head.append(_dot((cbs[h // 3] * dec).astype(BF16), xb_t))
        y_t = jnp.where(first, per_head[0], per_head[1])
        ga, gb = (2 * t) // 3, (2 * t + 1) // 3
        if ga == gb:
            y_off, s_add = _dot(cgs[ga], sb_t), _dot_tn(bgs[ga], xd_t)
        else:
            y_off = jnp.where(first, _dot(cgs[ga], sb_t), _dot(cgs[gb], sb_t))
            s_add = jnp.where(first, _dot_tn(bgs[ga], xd_t), _dot_tn(bgs[gb], xd_t))
        y_tiles.append(y_t + y_off * ecs[:, cl])
        s_tiles.append(s_add)
    y = dskip * xs + jnp.concatenate(y_tiles, axis=1)
    snew = sprev * jnp.exp(alast) + jnp.concatenate(s_tiles, axis=1)
    yg = y * _silu(z)
    sq = yg * yg
    g0 = col < SSD_W // 2
    ms0 = jnp.sum(jnp.where(g0, sq, 0.0), axis=-1, keepdims=True) * (2.0 / SSD_W)
    ms1 = jnp.sum(jnp.where(g0, 0.0, sq), axis=-1, keepdims=True) * (2.0 / SSD_W)
    r = jnp.where(g0, lax.rsqrt(ms0 + RMS_EPS), lax.rsqrt(ms1 + RMS_EPS))
    return yg * r * ng, snew


SSD_CHUNKS_PER_STEP = 4


def _ssd_chunks_per_step(S):
    k = SSD_CHUNKS_PER_STEP
    while (S // CHUNK) % k:
        k //= 2
    return k


def _ssd_fwd(pre, proj, par, B, S, name):
    T = B * S
    k = _ssd_chunks_per_step(S)
    nc, rows = S // (CHUNK * k), CHUNK * k
    e_mat, psel = _ssd_consts()

    def body(pre_ref, z_ref, dt_ref, par_ref, e_ref, p_ref, y_ref, sall_ref, st):
        @pl.when(pl.program_id(1) == 0)
        def _():
            st[...] = jnp.zeros_like(st)

        sprev = st[...]
        for i in range(k):
            r = slice(CHUNK * i, CHUNK * (i + 1))
            sall_ref[i] = sprev
            y, sprev = _ssd_chunk(pre_ref[r, :], z_ref[r, :], dt_ref[r, :], sprev, par_ref[...], e_ref[...],
                                  p_ref[...])
            y_ref[r, :] = y.astype(BF16)
        st[...] = sprev

    row = lambda b, c: b * nc + c
    full = lambda shp: pl.BlockSpec(shp, lambda b, c: (0, 0))
    return pl.pallas_call(
        body, name=name, grid=(B, nc),
        in_specs=[pl.BlockSpec((rows, SSD_CONV_DIM), lambda b, c: (row(b, c), 0)),
                  pl.BlockSpec((rows, SSD_W), lambda b, c: (row(b, c), Z0 // SSD_W)),
                  pl.BlockSpec((rows, LANE), lambda b, c: (row(b, c), DT0 // LANE)),
                  full((SUBLANE, SSD_W)), full((LANE, SSD_W)), full((SUBLANE, SSD_W))],
        out_specs=[pl.BlockSpec((rows, SSD_W), lambda b, c: (row(b, c), 0)),
                   pl.BlockSpec((k, SSD_STATE, SSD_W), lambda b, c: (row(b, c), 0, 0))],
        out_shape=[jax.ShapeDtypeStruct((T, SSD_W), BF16),
                   jax.ShapeDtypeStruct((B * nc * k, SSD_STATE, SSD_W), F32)],
        scratch_shapes=[pltpu.VMEM((SSD_STATE, SSD_W), F32)],
        compiler_params=_cp(("parallel", "arbitrary")),
    )(pre, proj, proj, par, e_mat, psel)


def _ssd_bwd(pre, proj, sall, dy, par, B, S, name):
    T = B * S
    k = _ssd_chunks_per_step(S)
    nc, rows = S // (CHUNK * k), CHUNK * k
    e_mat, psel = _ssd_consts()

    def body(pre_ref, z_ref, dt_ref, sall_ref, dy_ref, par_ref, e_ref, p_ref,
             dpre_ref, dz_ref, ddt_ref, dpar_ref, ds):
        b, c = pl.program_id(0), pl.program_id(1)

        @pl.when(c == 0)
        def _():
            ds[...] = jnp.zeros_like(ds)

        @pl.when((b == 0) & (c == 0))
        def _():
            dpar_ref[...] = jnp.zeros_like(dpar_ref)

        e_v, p_v = e_ref[...], p_ref[...]
        fn = lambda pre, z, dtr, sprev, par: _ssd_chunk(pre, z, dtr, sprev, par, e_v, p_v)
        dstate, dpar_sum = ds[...], None
        for i in reversed(range(k)):
            r = slice(CHUNK * i, CHUNK * (i + 1))
            _, vjp = jax.vjp(fn, pre_ref[r, :], z_ref[r, :], dt_ref[r, :], sall_ref[i], par_ref[...])
            dpre, dz, ddt, dstate, dpar = vjp((dy_ref[r, :], dstate))
            dpre_ref[r, :] = dpre
            dz_ref[r, :] = dz.astype(BF16)
            ddt_ref[r, :] = ddt.astype(BF16)
            dpar_sum = dpar if dpar_sum is None else dpar_sum + dpar
        dpar_ref[...] += dpar_sum
        ds[...] = dstate

    row = lambda b, c: b * nc + (nc - 1 - c)
    full = lambda shp: pl.BlockSpec(shp, lambda b, c: (0, 0))
    return pl.pallas_call(
        body, name=name, grid=(B, nc),
        in_specs=[pl.BlockSpec((rows, SSD_CONV_DIM), lambda b, c: (row(b, c), 0)),
                  pl.BlockSpec((rows, SSD_W), lambda b, c: (row(b, c), Z0 // SSD_W)),
                  pl.BlockSpec((rows, LANE), lambda b, c: (row(b, c), DT0 // LANE)),
                  pl.BlockSpec((k, SSD_STATE, SSD_W), lambda b, c: (row(b, c), 0, 0)),
                  pl.BlockSpec((rows, SSD_W), lambda b, c: (row(b, c), ATT_W // SSD_W)),
                  full((SUBLANE, SSD_W)), full((LANE, SSD_W)), full((SUBLANE, SSD_W))],
        out_specs=[pl.BlockSpec((rows, SSD_CONV_DIM), lambda b, c: (row(b, c), 0)),
                   pl.BlockSpec((rows, SSD_W), lambda b, c: (row(b, c), 0)),
                   pl.BlockSpec((rows, LANE), lambda b, c: (row(b, c), 0)),
                   full((SUBLANE, SSD_W))],
        out_shape=[jax.ShapeDtypeStruct((T, SSD_CONV_DIM), F32),
                   jax.ShapeDtypeStruct((T, SSD_W), BF16),
                   jax.ShapeDtypeStruct((T, LANE), BF16),
                   jax.ShapeDtypeStruct((SUBLANE, SSD_W), F32)],
        scratch_shapes=[pltpu.VMEM((SSD_STATE, SSD_W), F32)],
        compiler_params=_cp(("arbitrary", "arbitrary")),
    )(pre, proj, proj, sall, dy, par, e_mat, psel)


def _sgu_consts():
    e = np.zeros((SUBLANE, SGU_W), np.float32)
    for g in range(SGU_GROUPS):
        e[g, HEAD_DIM * g:HEAD_DIM * (g + 1)] = 1.0
    return jnp.asarray(e)


def _sgu_chunk(u_raw, v_raw, ln, w, bst, e4):
    L = CHUNK
    u = _gelu(u_raw)
    v = _gelu(v_raw)
    mu = jnp.mean(v, axis=-1, keepdims=True)
    vc = v - mu
    var = jnp.mean(vc * vc, axis=-1, keepdims=True)
    vn = vc * lax.rsqrt(var + LN_EPS) * ln[0:1] + ln[1:2]
    vb = vn.astype(BF16)
    ri = lax.broadcasted_iota(jnp.int32, (L, L), 0)
    ci = lax.broadcasted_iota(jnp.int32, (L, L), 1)
    tril = ri >= ci
    col = lax.broadcasted_iota(jnp.int32, (1, SGU_W), 1)
    mixed = _dot(bst, e4, HI)
    for g in range(SGU_GROUPS):
        wc = jnp.where(tril, w[g], 0.0).astype(BF16)
        gm = (col >= HEAD_DIM * g) & (col < HEAD_DIM * (g + 1))
        mixed = mixed + jnp.where(gm, _dot(wc, vb), 0.0)
    return u * mixed


def _sgu_fwd(proj, ln, w, bst, B, S, name, after=None):
    T = B * S
    e4 = _sgu_consts()
    extra, extra_specs = _after(after)
    k = max(d for d in (8, 4, 2, 1) if (T // CHUNK) % d == 0)
    rows = k * CHUNK

    def body(u_ref, v_ref, ln_ref, w_ref, b_ref, e_ref, *rest):
        y_ref = rest[-1]
        for i in range(k):
            r = slice(CHUNK * i, CHUNK * (i + 1))
            y_ref[r, :] = _sgu_chunk(u_ref[r, :], v_ref[r, :], ln_ref[...], w_ref[...], b_ref[...],
                                     e_ref[...]).astype(BF16)

    return pl.pallas_call(
        body, name=name, grid=(T // rows,),
        in_specs=[pl.BlockSpec((rows, SGU_W), lambda i: (i, U0 // SGU_W)),
                  pl.BlockSpec((rows, SGU_W), lambda i: (i, VS0 // SGU_W)),
                  pl.BlockSpec((SUBLANE, SGU_W), lambda i: (0, 0)),
                  pl.BlockSpec((SGU_GROUPS, CHUNK, CHUNK), lambda i: (0, 0, 0)),
                  pl.BlockSpec((CHUNK, SUBLANE), lambda i: (0, 0)),
                  pl.BlockSpec((SUBLANE, SGU_W), lambda i: (0, 0))] + extra_specs,
        out_specs=pl.BlockSpec((rows, SGU_W), lambda i: (i, 0)),
        out_shape=jax.ShapeDtypeStruct((T, SGU_W), BF16),
        compiler_params=_cp(("parallel",)),
    )(proj, proj, ln, w, bst, e4, *extra)


def _sgu_bwd(proj, dy, ln, w, bst, B, S, name):
    T = B * S
    e4 = _sgu_consts()
    ycol = (ATT_W + SSD_W) // SGU_W
    k = max(d for d in (4, 2, 1) if (T // CHUNK) % d == 0)
    rows = k * CHUNK

    def body(u_ref, v_ref, dy_ref, ln_ref, w_ref, b_ref, e_ref, du_ref, dv_ref, dln_ref, dw_ref, db_ref):
        @pl.when(pl.program_id(0) == 0)
        def _():
            dln_ref[...] = jnp.zeros_like(dln_ref)
            dw_ref[...] = jnp.zeros_like(dw_ref)
            db_ref[...] = jnp.zeros_like(db_ref)

        e_v = e_ref[...]
        fn = lambda u, v, ln, w, b: _sgu_chunk(u, v, ln, w, b, e_v)
        acc = None
        for i in range(k):
            r = slice(CHUNK * i, CHUNK * (i + 1))
            _, vjp = jax.vjp(fn, u_ref[r, :], v_ref[r, :], ln_ref[...], w_ref[...], b_ref[...])
            du, dv, *dpar = vjp(dy_ref[r, :])
            du_ref[r, :] = du.astype(BF16)
            dv_ref[r, :] = dv.astype(BF16)
            acc = dpar if acc is None else [a + d for a, d in zip(acc, dpar)]
        dln_ref[...] += acc[0]
        dw_ref[...] += acc[1]
        db_ref[...] += acc[2]

    c_ln = pl.BlockSpec((SUBLANE, SGU_W), lambda i: (0, 0))
    c_w = pl.BlockSpec((SGU_GROUPS, CHUNK, CHUNK), lambda i: (0, 0, 0))
    c_b = pl.BlockSpec((CHUNK, SUBLANE), lambda i: (0, 0))
    return pl.pallas_call(
        body, name=name, grid=(T // rows,),
        in_specs=[pl.BlockSpec((rows, SGU_W), lambda i: (i, U0 // SGU_W)),
                  pl.BlockSpec((rows, SGU_W), lambda i: (i, VS0 // SGU_W)),
                  pl.BlockSpec((rows, SGU_W), lambda i: (i, ycol)),
                  c_ln, c_w, c_b, pl.BlockSpec((SUBLANE, SGU_W), lambda i: (0, 0))],
        out_specs=[pl.BlockSpec((rows, SGU_W), lambda i: (i, 0)),
                   pl.BlockSpec((rows, SGU_W), lambda i: (i, 0)), c_ln, c_w, c_b],
        out_shape=[jax.ShapeDtypeStruct((T, SGU_W), BF16), jax.ShapeDtypeStruct((T, SGU_W), BF16),
                   jax.ShapeDtypeStruct((SUBLANE, SGU_W), F32),
                   jax.ShapeDtypeStruct((SGU_GROUPS, CHUNK, CHUNK), F32),
                   jax.ShapeDtypeStruct((CHUNK, SUBLANE), F32)],
        compiler_params=_cp(("arbitrary",)),
    )(proj, proj, dy, ln, w, bst, e4)


_HBM = pl.BlockSpec(memory_space=pltpu.HBM)
_SEM = pl.BlockSpec(memory_space=pltpu.SEMAPHORE)
_ANY = pl.BlockSpec(memory_space=pl.ANY)
_EFFECT = pltpu.SideEffectType.DATAFLOW_SIDE_EFFECTING


def _peers():
    x, y, c = lax.axis_index("x"), lax.axis_index("y"), lax.axis_index("c")
    out = []
    for p in range(1, N_DEV):
        px, py, pc = x ^ ((p >> 2) & 1), y ^ ((p >> 1) & 1), c ^ (p & 1)
        out.append(((px, py, pc), 4 * px + 2 * py + pc))
    return 4 * x + 2 * y + c, out


def _xchg_start(xs, a2a, order, name):
    n = len(xs)
    lands = [lax.empty(a.shape if f else (N_DEV,) + a.shape, a.dtype) for a, f in zip(xs, a2a)]

    def body(*refs):
        ins, zones = refs[:n], refs[n:2 * n]
        send_sems, recv_sems = refs[2 * n + 1], refs[2 * n + 2]
        token = refs[-1]
        me, peers = _peers()
        for p, (dev, peer) in enumerate(peers):
            for t in range(n):
                pltpu.make_async_remote_copy(
                    src_ref=ins[t].at[peer] if a2a[t] else ins[t], dst_ref=zones[t].at[me],
                    send_sem=send_sems.at[p * n + t], recv_sem=recv_sems.at[p * n + t],
                    device_id=dev, device_id_type=MESH).start()
        token[...] = jnp.zeros_like(token)

    hbm = lambda a: pltpu.HBM(a.shape, a.dtype)
    sems = pltpu.SemaphoreType.DMA(((N_DEV - 1) * n,))
    out = pl.pallas_call(
        body, name=name,
        in_specs=[_HBM] * (2 * n) + [_ANY],
        out_specs=[_SEM, _SEM] + [_HBM] * (2 * n) + [pl.BlockSpec(memory_space=pltpu.VMEM)],
        out_shape=[sems, sems] + [hbm(a) for a in xs] + [hbm(a) for a in lands]
        + [jax.ShapeDtypeStruct((SUBLANE, LANE), F32)],
        input_output_aliases={t: 2 + t for t in range(2 * n)},
        compiler_params=pltpu.CompilerParams(has_side_effects=_EFFECT),
    )(*[pltpu.with_memory_space_constraint(a, pltpu.HBM) for a in list(xs) + list(lands)], order)
    return out[0], out[1], out[2:2 + n], out[2 + n:2 + 2 * n], out[-1]


def _xchg_wait(started, a2a, after, name):
    send_sems, recv_sems, xs, lands, _ = started
    n = len(xs)

    def body(*refs):
        ins, zones = refs[:n], refs[n:2 * n]
        send_s, recv_s = refs[2 * n], refs[2 * n + 1]
        me, peers = _peers()
        cps = []
        for p, (dev, peer) in enumerate(peers):
            for t in range(n):
                cps.append(pltpu.make_async_remote_copy(
                    src_ref=ins[t].at[peer] if a2a[t] else ins[t], dst_ref=zones[t].at[peer],
                    send_sem=send_s.at[p * n + t], recv_sem=recv_s.at[p * n + t],
                    device_id=dev, device_id_type=MESH))
        for cp in cps:
            cp.wait_recv()
        for cp in cps:
            cp.wait_send()

    hbm = lambda a: pltpu.HBM(a.shape, a.dtype)
    out = pl.pallas_call(
        body, name=name,
        in_specs=[_HBM] * (2 * n) + [_SEM, _SEM, _ANY],
        out_specs=[_HBM] * (2 * n),
        out_shape=[hbm(a) for a in xs] + [hbm(a) for a in lands],
        input_output_aliases={t: t for t in range(2 * n)},
        compiler_params=pltpu.CompilerParams(has_side_effects=_EFFECT),
    )(*xs, *lands, send_sems, recv_sems, after)
    return out[:n], out[n:]


def _chip_peers():
    x, y, c = lax.axis_index("x"), lax.axis_index("y"), lax.axis_index("c")
    chips = [(1 - x, y), (x, 1 - y), (1 - x, 1 - y)]
    slot = lambda px, py, pc: 4 * px + 2 * py + pc
    return (x, y, c), chips, slot


def _gather_start(xs, order, name):
    n = len(xs)
    lands = [lax.empty((N_DEV,) + a.shape, a.dtype) for a in xs]

    def body(*refs):
        ins, zones = refs[:n], refs[n:2 * n]
        send_sems, d2d_sems, ici_sems = refs[2 * n + 1:2 * n + 4]
        token = refs[-1]
        (x, y, c), chips, slot = _chip_peers()
        me = slot(x, y, c)
        for t in range(n):
            pltpu.make_async_copy(ins[t], zones[t].at[me], d2d_sems.at[n + t]).start()
            for j, (px, py) in enumerate(chips):
                pltpu.make_async_remote_copy(
                    src_ref=ins[t], dst_ref=zones[t].at[me], send_sem=send_sems.at[(1 + j) * n + t],
                    recv_sem=ici_sems.at[j * n + t], device_id=(px, py, c), device_id_type=MESH).start()
            pltpu.make_async_remote_copy(
                src_ref=ins[t], dst_ref=zones[t].at[me], send_sem=send_sems.at[t],
                recv_sem=d2d_sems.at[t], device_id=(x, y, 1 - c), device_id_type=MESH).start()
        token[...] = jnp.zeros_like(token)

    hbm = lambda a: pltpu.HBM(a.shape, a.dtype)
    dma = lambda k: pltpu.SemaphoreType.DMA((k,))
    out = pl.pallas_call(
        body, name=name,
        in_specs=[_HBM] * (2 * n) + [_ANY],
        out_specs=[_SEM, _SEM, _SEM] + [_HBM] * (2 * n) + [pl.BlockSpec(memory_space=pltpu.VMEM)],
        out_shape=[dma(4 * n), dma(2 * n), dma(3 * n)] + [hbm(a) for a in xs] + [hbm(a) for a in lands]
        + [jax.ShapeDtypeStruct((SUBLANE, LANE), F32)],
        input_output_aliases={t: 3 + t for t in range(2 * n)},
        compiler_params=pltpu.CompilerParams(has_side_effects=_EFFECT),
    )(*[pltpu.with_memory_space_constraint(a, pltpu.HBM) for a in list(xs) + list(lands)], order)
    return dict(send=out[0], d2d=out[1], ici=out[2], xs=out[3:3 + n], lands=out[3 + n:3 + 2 * n], token=out[-1])


def _gather_relay(st, after, name):
    n = len(st["xs"])

    def body(*refs):
        zones, ici_sems = refs[:n], refs[n]
        fsend, frecv = refs[n + 2], refs[n + 3]
        token = refs[-1]
        (x, y, c), chips, slot = _chip_peers()
        for t in range(n):
            for j, (px, py) in enumerate(chips):
                blk = zones[t].at[slot(px, py, c)]
                fwd = pltpu.make_async_remote_copy(
                    src_ref=blk, dst_ref=blk, send_sem=fsend.at[j * n + t], recv_sem=ici_sems.at[j * n + t],
                    device_id=(x, y, 1 - c), device_id_type=MESH)
                fwd.wait_recv()
                pltpu.make_async_remote_copy(
                    src_ref=blk, dst_ref=blk, send_sem=fsend.at[j * n + t], recv_sem=frecv.at[j * n + t],
                    device_id=(x, y, 1 - c), device_id_type=MESH).start()
        token[...] = jnp.zeros_like(token)

    hbm = lambda a: pltpu.HBM(a.shape, a.dtype)
    dma = lambda k: pltpu.SemaphoreType.DMA((k,))
    out = pl.pallas_call(
        body, name=name,
        in_specs=[_HBM] * n + [_SEM, _ANY],
        out_specs=[_SEM, _SEM] + [_HBM] * n + [pl.BlockSpec(memory_space=pltpu.VMEM)],
        out_shape=[dma(3 * n), dma(3 * n)] + [hbm(a) for a in st["lands"]]
        + [jax.ShapeDtypeStruct((SUBLANE, LANE), F32)],
        input_output_aliases={t: 2 + t for t in range(n)},
        compiler_params=pltpu.CompilerParams(has_side_effects=_EFFECT),
    )(*st["lands"], st["ici"], after)
    return dict(st, fsend=out[0], frecv=out[1], lands=out[2:2 + n], token=out[-1])


def _gather_wait(st, after, name):
    n = len(st["xs"])

    def body(*refs):
        ins, zones = refs[:n], refs[n:2 * n]
        send_sems, d2d_sems, fsend, frecv = refs[2 * n:2 * n + 4]
        (x, y, c), chips, slot = _chip_peers()
        sib = (x, y, 1 - c)
        for t in range(n):
            pltpu.make_async_copy(ins[t], zones[t].at[slot(x, y, c)], d2d_sems.at[n + t]).wait()
            mine = lambda s, r, dst: pltpu.make_async_remote_copy(
                src_ref=ins[t], dst_ref=dst, send_sem=s, recv_sem=r, device_id=sib, device_id_type=MESH)
            direct = mine(send_sems.at[t], d2d_sems.at[t], zones[t].at[slot(x, y, 1 - c)])
            direct.wait_recv()
            direct.wait_send()
            for j, (px, py) in enumerate(chips):
                mine(send_sems.at[(1 + j) * n + t], d2d_sems.at[t], zones[t].at[slot(px, py, c)]).wait_send()
                relayed = mine(fsend.at[j * n + t], frecv.at[j * n + t], zones[t].at[slot(px, py, 1 - c)])
                relayed.wait_recv()
                relayed.wait_send()

    hbm = lambda a: pltpu.HBM(a.shape, a.dtype)
    out = pl.pallas_call(
        body, name=name,
        in_specs=[_HBM] * (2 * n) + [_SEM] * 4 + [_ANY],
        out_specs=[_HBM] * (2 * n),
        out_shape=[hbm(a) for a in st["xs"]] + [hbm(a) for a in st["lands"]],
        input_output_aliases={t: t for t in range(2 * n)},
        compiler_params=pltpu.CompilerParams(has_side_effects=_EFFECT),
    )(*st["xs"], *st["lands"], st["send"], st["d2d"], st["fsend"], st["frecv"], after)
    return out[:n], out[n:]


def _cast_layers(pairs, name):
    def body(*refs):
        n = len(refs) // 2
        for i in range(n):
            refs[n + i][...] = refs[i][...].astype(BF16)

    in_specs = [pl.BlockSpec((None,) + w.shape[1:], functools.partial(lambda l, i: (l, 0, 0), l),
                             pipeline_mode=pl.Buffered(1)) for w, l in pairs]
    return pl.pallas_call(
        body, name=name, grid=(1,), in_specs=in_specs,
        out_specs=[pl.BlockSpec(w.shape[1:], lambda i: (0, 0)) for w, _ in pairs],
        out_shape=[jax.ShapeDtypeStruct(w.shape[1:], BF16) for w, _ in pairs],
        compiler_params=_cp(("arbitrary",)),
    )(*[w for w, _ in pairs])


ADAMW_BLOCK_ELEMS = 256 * 1024


def _adam_step(me, w, m, v, parts_ref, mine, out_refs):
    g = None
    for p in range(N_DEV):
        term = jnp.where(me == p, mine.astype(F32), parts_ref[p].astype(F32))
        g = term if g is None else g + term
    mn = ADAM_B1 * m + (1.0 - ADAM_B1) * g
    vn = ADAM_B2 * v + (1.0 - ADAM_B2) * (g * g)
    m_hat = mn / (1.0 - ADAM_B1 ** ADAM_STEP)
    v_hat = vn / (1.0 - ADAM_B2 ** ADAM_STEP)
    g_ref, d_ref, mo_ref, vo_ref = out_refs
    g_ref[...] = g
    d_ref[...] = -ADAM_LR * (m_hat / (jnp.sqrt(v_hat) + ADAM_EPS) + ADAM_WD * w)
    mo_ref[...] = mn
    vo_ref[...] = vn


def _adamw(me, w, m, v, parts, own, name, layer=0, into=None):
    L, R, C = w.shape
    P = parts.shape[0]
    tr = R
    t = 16
    while t <= R:
        if R % t == 0 and t * C <= ADAMW_BLOCK_ELEMS:
            tr = t
        t += 16
    if tr == R and R * C > ADAMW_BLOCK_ELEMS and R % 16 == 0:
        tr = 16
    own_all = own.shape[0] == P

    def body(me_ref, w_ref, m_ref, v_ref, p_ref, own_ref, *rest):
        _adam_step(me_ref[0], w_ref[...], m_ref[...], v_ref[...], p_ref, own_ref[...], rest[-4:])

    blk = pl.BlockSpec((None, tr, C), lambda i, me_ref: (layer, i, 0))
    own_blk = pl.BlockSpec((None, tr, C), lambda i, me_ref: (me_ref[0] if own_all else 0, i, 0))
    prev = list(into) if into is not None else []
    return pl.pallas_call(
        body, name=name,
        grid_spec=pltpu.PrefetchScalarGridSpec(
            num_scalar_prefetch=1, grid=(R // tr,),
            in_specs=[blk, blk, blk, pl.BlockSpec((P, tr, C), lambda i, me_ref: (0, i, 0)), own_blk]
            + [_ANY] * len(prev),
            out_specs=[blk] * 4),
        out_shape=[jax.ShapeDtypeStruct((L, R, C), F32)] * 4,
        input_output_aliases={6 + i: i for i in range(len(prev))},
        compiler_params=_cp(("parallel",)),
    )(me, w, m, v, parts, own, *prev)


def _perm_cols(w):
    pad = jnp.zeros(w.shape[:-1] + (LANE - SSD_HEADS,), w.dtype)
    return jnp.concatenate([w[..., 0:1536], w[..., 2438:2694], w[..., 1536:2432], w[..., 2432:2438], pad,
                            w[..., 2694:2950]], axis=-1)


def _unperm_cols(w):
    return jnp.concatenate([w[..., 0:1536], w[..., XBC0:XBC0 + SSD_CONV_DIM], w[..., DT0:DT0 + SSD_HEADS],
                            w[..., U0:U0 + SGU_W], w[..., VS0:VS0 + SGU_W]], axis=-1)


_SMALL = ("ffn1_norm", "mix_norm", "conv_w", "conv_b", "dt_bias", "a_log", "d_skip", "ssd_norm",
          "sgu_ln_g", "sgu_ln_b", "sgu_w", "sgu_b", "ffn2_norm", "final_norm", "loss")


_SMALL_LAST = ("ffn1_norm",)
_SMALL_EARLY = tuple(k for k in _SMALL if k not in _SMALL_LAST)


def _pack(d, names):
    v = jnp.concatenate([d[k].astype(F32).reshape(-1) for k in names])
    n = v.shape[0]
    npad = -(-n // (LANE * 16)) * (LANE * 16)
    return jnp.pad(v, (0, npad - n)).reshape(npad // LANE, LANE)


def _unpack(p, shapes, names):
    v = p.reshape(-1)
    out, o = {}, 0
    for k in names:
        n = int(np.prod(shapes[k]))
        out[k] = v[o:o + n].reshape(shapes[k])
        o += n
    return out


def kernel(x, ffn1_norm, ffn1_w_gate, ffn1_w_up, ffn1_w_down, mix_norm, w_in, conv_w, conv_b, dt_bias, a_log, d_skip, ssd_norm, sgu_ln_g, sgu_ln_b, sgu_w, sgu_b, w_out, ffn2_norm, ffn2_w_gate, ffn2_w_up, ffn2_w_down, final_norm, loss_target, m_ffn1_norm, m_ffn1_w_gate, m_ffn1_w_up, m_ffn1_w_down, m_mix_norm, m_w_in, m_conv_w, m_conv_b, m_dt_bias, m_a_log, m_d_skip, m_ssd_norm, m_sgu_ln_g, m_sgu_ln_b, m_sgu_w, m_sgu_b, m_w_out, m_ffn2_norm, m_ffn2_w_gate, m_ffn2_w_up, m_ffn2_w_down, m_final_norm, v_ffn1_norm, v_ffn1_w_gate, v_ffn1_w_up, v_ffn1_w_down, v_mix_norm, v_w_in, v_conv_w, v_conv_b, v_dt_bias, v_a_log, v_d_skip, v_ssd_norm, v_sgu_ln_g, v_sgu_ln_b, v_sgu_w, v_sgu_b, v_w_out, v_ffn2_norm, v_ffn2_w_gate, v_ffn2_w_up, v_ffn2_w_down, v_final_norm):
    B, S, D = x.shape
    T = B * S
    L = ffn1_norm.shape[0]
    me = 4 * lax.axis_index("x") + 2 * lax.axis_index("y") + lax.axis_index("c")
    cs = conv_w.shape[2]
    W = dict(ffn1_norm=ffn1_norm, ffn1_w_gate=ffn1_w_gate, ffn1_w_up=ffn1_w_up, ffn1_w_down=ffn1_w_down,
             mix_norm=mix_norm, w_in=w_in, conv_w=conv_w, conv_b=conv_b, dt_bias=dt_bias, a_log=a_log,
             d_skip=d_skip, ssd_norm=ssd_norm, sgu_ln_g=sgu_ln_g, sgu_ln_b=sgu_ln_b, sgu_w=sgu_w, sgu_b=sgu_b,
             w_out=w_out, ffn2_norm=ffn2_norm, ffn2_w_gate=ffn2_w_gate, ffn2_w_up=ffn2_w_up,
             ffn2_w_down=ffn2_w_down, final_norm=final_norm)
    M = dict(ffn1_norm=m_ffn1_norm, ffn1_w_gate=m_ffn1_w_gate, ffn1_w_up=m_ffn1_w_up, ffn1_w_down=m_ffn1_w_down,
             mix_norm=m_mix_norm, w_in=m_w_in, conv_w=m_conv_w, conv_b=m_conv_b, dt_bias=m_dt_bias, a_log=m_a_log,
             d_skip=m_d_skip, ssd_norm=m_ssd_norm, sgu_ln_g=m_sgu_ln_g, sgu_ln_b=m_sgu_ln_b, sgu_w=m_sgu_w,
             sgu_b=m_sgu_b, w_out=m_w_out, ffn2_norm=m_ffn2_norm, ffn2_w_gate=m_ffn2_w_gate,
             ffn2_w_up=m_ffn2_w_up, ffn2_w_down=m_ffn2_w_down, final_norm=m_final_norm)
    V = dict(ffn1_norm=v_ffn1_norm, ffn1_w_gate=v_ffn1_w_gate, ffn1_w_up=v_ffn1_w_up, ffn1_w_down=v_ffn1_w_down,
             mix_norm=v_mix_norm, w_in=v_w_in, conv_w=v_conv_w, conv_b=v_conv_b, dt_bias=v_dt_bias, a_log=v_a_log,
             d_skip=v_d_skip, ssd_norm=v_ssd_norm, sgu_ln_g=v_sgu_ln_g, sgu_ln_b=v_sgu_ln_b, sgu_w=v_sgu_w,
             sgu_b=v_sgu_b, w_out=v_w_out, ffn2_norm=v_ffn2_norm, ffn2_w_gate=v_ffn2_w_gate,
             ffn2_w_up=v_ffn2_w_up, ffn2_w_down=v_ffn2_w_down, final_norm=v_final_norm)
    FFN1 = ("ffn1_w_gate", "ffn1_w_up", "ffn1_w_down")
    FFN2 = ("ffn2_w_gate", "ffn2_w_up", "ffn2_w_down")
    MIX = ("w_in", "w_out")
    big = FFN1 + MIX + FFN2
    col_sharded = lambda k: k.endswith("w_gate") or k.endswith("w_up")
    for dct in (W, M, V):
        for k in big:
            if col_sharded(k):
                dct[k] = jnp.swapaxes(dct[k], 1, 2)
        dct["w_in"] = _perm_cols(dct["w_in"])

    wgroups = [[(k, 0) for k in FFN1], [("w_in", 0), ("conv_w", None)], [("w_out", 0)] + [(k, 0) for k in FFN2]]
    for l in range(1, L):
        wgroups += [[(k, l) for k in FFN1] + [("w_in", l)], [("w_out", l)] + [(k, l) for k in FFN2]]
    wstarted, order = [], x
    later = [kl for grp in wgroups[1:] for kl in grp if kl[0] != "conv_w"]
    cast = dict(zip(wgroups[0], _cast_layers([(W[k], l) for k, l in wgroups[0]], "cast_first")))
    for gi, grp in enumerate(wgroups):
        if gi == 1:
            first = lax.optimization_barrier((W[later[0][0]], order))[0]
            srcs = [(first if i == 0 else W[k], l) for i, (k, l) in enumerate(later)]
            cast.update(zip(later, _cast_layers(srcs, "cast_rest")))
        xs = [conv_w if k == "conv_w" else cast[(k, l)] for k, l in grp]
        st = _gather_start(xs, order, f"gather_start_{gi}")
        order = st["token"]
        wstarted.append(st)
    G = {}

    zero1 = jnp.zeros((1,), F32)
    W["loss"], M["loss"], V["loss"] = zero1, zero1, zero1
    full_shapes = {k: (W[k].shape if k != "conv_w" else (L, SSD_CONV, SSD_CONV_DIM)) for k in _SMALL}
    embed = lambda a, k: a if k != "conv_w" else lax.dynamic_update_slice(
        jnp.zeros(full_shapes[k], F32), a, (0, 0, me * cs))
    small_packs = {names: [_pack({k: embed(d[k], k) for k in names}, names)[None] for d in (W, M, V)]
                   for names in (_SMALL_EARLY, _SMALL_LAST)}

    def relay(gi, after):
        wstarted[gi] = _gather_relay(wstarted[gi], after, f"gather_relay_{gi}")
        return wstarted[gi]["token"]

    def gathered(gi, after):
        _, lands = _gather_wait(wstarted[gi], after, f"gather_wait_{gi}")
        G.update(zip(wgroups[gi], lands))

    def rows(k, l):
        a = G[(k, l)]
        return a.reshape(-1, a.shape[-1])

    bias = _attn_bias(S, min(256, S))
    row1 = lambda a: a.reshape(1, -1)

    def ffn1_params(l):
        return dict(g1=row1(ffn1_norm[l]), wg1=rows("ffn1_w_gate", l), wu1=rows("ffn1_w_up", l),
                    wd1=rows("ffn1_w_down", l))

    def out_params(l):
        return dict(wout=rows("w_out", l), g2=row1(ffn2_norm[l]), wg2=rows("ffn2_w_gate", l),
                    wu2=rows("ffn2_w_up", l), wd2=rows("ffn2_w_down", l))

    def mix_params(l):
        cw = jnp.transpose(G[("conv_w", None)][:, l], (1, 0, 2)).reshape(SSD_CONV, -1)
        return dict(
            gm=row1(mix_norm[l]), win=rows("w_in", l),
            cw=jnp.pad(cw, ((0, SUBLANE - SSD_CONV), (0, 0))), cb=row1(conv_b[l]),
            par=jnp.pad(jnp.stack([jnp.repeat(dt_bias[l], HEAD_DIM), jnp.repeat(a_log[l], HEAD_DIM),
                                   jnp.repeat(d_skip[l], HEAD_DIM), ssd_norm[l]]), ((0, SUBLANE - 4), (0, 0))),
            ln=jnp.pad(jnp.stack([sgu_ln_g[l], sgu_ln_b[l]]), ((0, SUBLANE - 2), (0, 0))),
            sw=sgu_w[l], bst=jnp.pad(sgu_b[l].T, ((0, 0), (0, SUBLANE - SGU_GROUPS))))

    xc = x.reshape(T, D)
    saved, lay = [], []
    for l in range(L):
        if l == 0:
            gathered(0, relay(0, order))
        else:
            gathered(1 + 2 * l, xc)
        p = ffn1_params(l)
        x1, *ffn1_saved = _ffn_fwd(xc, p["g1"], p["wg1"], p["wu1"], p["wd1"], f"ffn1_fwd_{l}")
        if l == 0:
            gathered(1, relay(1, x1))
        p.update(mix_params(l))
        lay.append(p)
        proj, ht = _norm_mm(x1, p["gm"], p["win"], f"in_proj_{l}")
        o_att, lse = _attn_fwd(proj, bias, B, S, f"attn_fwd_{l}")
        tok = relay(2 + 2 * l, o_att)
        pre = _conv_fwd(proj, p["cw"], p["cb"], B, S, f"conv_fwd_{l}", after=tok)
        y_ssd, sall = _ssd_fwd(pre, proj, p["par"], B, S, f"ssd_fwd_{l}")
        y_sgu = _sgu_fwd(proj, p["ln"], p["sw"], p["bst"], B, S, f"sgu_fwd_{l}", after=tok)
        ycat = jnp.concatenate([o_att.astype(BF16), y_ssd, y_sgu], axis=1)
        gathered(2 + 2 * l, ycat)
        p.update(out_params(l))
        x2 = _mm(ycat, p["wout"], "nn", f"out_proj_{l}", residual=x1)
        tok = relay(3 + 2 * l, x2) if l + 1 < L else None
        x3, *ffn2_saved = _ffn_fwd(x2, p["g2"], p["wg2"], p["wu2"], p["wd2"], f"ffn2_fwd_{l}", after=tok)
        saved.append(dict(x0=xc, ffn1=ffn1_saved, x1=x1, ht=ht, proj=proj, o_att=o_att, lse=lse, pre=pre,
                          sall=sall, ycat=ycat, x2=x2, ffn2=ffn2_saved))
        xc = x3
    loss_part, dx, dgf = _final_loss(xc, row1(final_norm), loss_target.reshape(T, D), "final_loss")

    gl = [dict() for _ in range(L)]
    gstarted, gorder = [], [order]

    def to_blocks(k, a):
        return a.reshape(N_DEV, -1, a.shape[-1]).astype(BF16)

    def send_grads(keys, l, extra, tag, small_names=None):
        xs = [to_blocks(k, gl[l][k]) for k in keys] + extra
        flags = [True] * len(keys) + [False] * len(extra)
        st = _xchg_start(xs, flags, gorder[0], f"grads_start_{tag}")
        gorder[0] = st[-1]
        gstarted.append((keys, l, st, flags, tag, small_names))

    def small_grads(names):
        sm = {}
        for k in names:
            if k == "final_norm":
                sm[k] = dgf.reshape(-1)
            elif k == "loss":
                sm[k] = loss_part[0, :1]
            else:
                sm[k] = jnp.stack([gl[l][k] for l in range(L)])
        return [_pack(sm, names)]

    def behind(a):
        return lax.optimization_barrier((a, gorder[0]))[0]

    for l in reversed(range(L)):
        p, s, g = lay[l], saved[l], gl[l]
        gfac, ufac, actt = s["ffn2"]
        dx2, dgt, dut, xn, dacc, g["ffn2_norm"] = _ffn_bwd_dx(
            dx, s["x2"], p["g2"], gfac, ufac, p["wg2"], p["wu2"], p["wd2"], f"ffn2_bwd_{l}")
        g["ffn2_w_gate"], g["ffn2_w_up"], g["ffn2_w_down"] = _ffn_dw(dgt, dut, actt, xn, dacc, f"ffn2_dw_{l}")
        if l == 0:
            send_grads(FFN2, 0, [], "l0f")
            dx2 = behind(dx2)
        dycat = _mm(dx2, p["wout"], "nt", f"out_proj_dx_{l}")
        g["w_out"] = _mm(s["ycat"], dx2, "tn", f"out_proj_dw_{l}", out_dtype=BF16, tm_cap=1024, tk_cap=512)
        dq, dk, dv = _attn_bwd(s["proj"], s["o_att"], s["lse"], dycat, bias, B, S, f"attn_bwd_{l}")
        dpre, dz, ddt, dpar = _ssd_bwd(s["pre"], s["proj"], s["sall"], dycat, p["par"], B, S, f"ssd_bwd_{l}")
        dxbc, dwb = _conv_bwd(dpre, s["proj"], p["cw"], B, S, f"conv_bwd_{l}")
        du, dvs, dln, dsw, dbst = _sgu_bwd(s["proj"], dycat, p["ln"], p["sw"], p["bst"], B, S, f"sgu_bwd_{l}")
        hsum = lambda r: r.reshape(SSD_HEADS, HEAD_DIM).sum(-1)
        g["conv_w"], g["conv_b"] = dwb[:SSD_CONV], dwb[SSD_CONV]
        g["dt_bias"], g["a_log"], g["d_skip"], g["ssd_norm"] = hsum(dpar[0]), hsum(dpar[1]), hsum(dpar[2]), dpar[3]
        g["sgu_ln_g"], g["sgu_ln_b"], g["sgu_w"], g["sgu_b"] = dln[0], dln[1], dsw, dbst[:, :SGU_GROUPS].T
        dproj = jnp.concatenate([dq, dk, dv, dz, du, dxbc, ddt, dvs], axis=1)
        g["w_in"] = _mm_resident_lhs(s["ht"], dproj, f"in_proj_dw_{l}")
        dx1, g["mix_norm"] = _norm_mm_bwd(dproj, s["x1"], p["gm"], p["win"], dx2, f"in_proj_bwd_{l}")
        if l == 0:
            send_grads(MIX, 0, small_grads(_SMALL_EARLY), "l0a", _SMALL_EARLY)
            dx1, small_packs = lax.optimization_barrier((behind(dx1), small_packs))
        gfac, ufac, actt = s["ffn1"]
        dx, dgt, dut, xn, dacc, g["ffn1_norm"] = _ffn_bwd_dx(
            dx1, s["x0"], p["g1"], gfac, ufac, p["wg1"], p["wu1"], p["wd1"], f"ffn1_bwd_{l}")
        if l > 0:
            g["ffn1_w_gate"], g["ffn1_w_up"], g["ffn1_w_down"] = _ffn_dw(dgt, dut, actt, xn, dacc,
                                                                        f"ffn1_dw_{l}")
            send_grads(big, l, [], f"l{l}")
            dx = behind(dx)
        else:
            g["ffn1_w_gate"] = _dw_one(dgt, xn, "ffn1_dwg_0")
            send_grads(("ffn1_w_gate",), 0, [], "l0b1")
            g["ffn1_w_up"] = _dw_one(dut, xn, "ffn1_dwu_0", after=gorder[0])
            send_grads(("ffn1_w_up",), 0, [], "l0b2")
            g["ffn1_w_down"] = _dw_one(actt, dacc, "ffn1_dwd_0", after=gorder[0])
    grad_x = dx.reshape(B, S, D)
    send_grads(("ffn1_w_down",), 0, small_grads(_SMALL_LAST), "l0b", _SMALL_LAST)

    res, after = {}, gorder[0]
    small_out = [dict() for _ in range(4)]
    me1 = me.reshape(1).astype(jnp.int32)
    for keys, l, st, flags, tag, names in gstarted:
        own, lands = _xchg_wait(st, flags, after, f"grads_wait_{tag}")
        for k, mine, pk in zip(keys, own, lands):
            res[k] = _adamw(me1, W[k], M[k], V[k], pk, mine, f"adamw_{k}_{l}", layer=l, into=res.get(k))
        done = [res[k][0] for k in keys]
        if names:
            outs = _adamw(me1, *small_packs[names], lands[-1], own[-1][None], f"adamw_small_{tag}")
            for d, o in zip(small_out, outs):
                u = _unpack(o, full_shapes, names)
                if "conv_w" in u:
                    u["conv_w"] = lax.dynamic_slice(u["conv_w"], (0, 0, me * cs), (L, SSD_CONV, cs))
                d.update(u)
                done.extend(u.values())
        after = lax.optimization_barrier(tuple(done))[0]
    back = lambda k, a: jnp.swapaxes(a, 1, 2) if col_sharded(k) else _unperm_cols(a) if k == "w_in" else a
    grads, deltas, new_m, new_v = [dict({k: back(k, res[k][i]) for k in big}, **small_out[i]) for i in range(4)]

    names = ("ffn1_norm", "ffn1_w_gate", "ffn1_w_up", "ffn1_w_down", "mix_norm", "w_in", "conv_w", "conv_b",
             "dt_bias", "a_log", "d_skip", "ssd_norm", "sgu_ln_g", "sgu_ln_b", "sgu_w", "sgu_b", "w_out",
             "ffn2_norm", "ffn2_w_gate", "ffn2_w_up", "ffn2_w_down", "final_norm")
    loss = grads["loss"][0]
    return (loss, grad_x, *[grads[n] for n in names], *[deltas[n] for n in names],
            *[new_m[n] for n in names], *[new_v[n] for n in names])
```

```python
import functools

import numpy as np
import jax
import jax.numpy as jnp
from jax import lax
from jax.experimental import pallas as pl
from jax.experimental.pallas import tpu as pltpu

F32, BF16 = jnp.float32, jnp.bfloat16
HI = lax.Precision.HIGH
MESH = pl.DeviceIdType.MESH
N_DEV = 8
VMEM_LIMIT_BYTES = 56 * 1024 * 1024
LANE, SUBLANE = 128, 8

HEAD_DIM = 64
ATT_W = 384
SSD_W = 384
SSD_HEADS = 6
SSD_STATE = 128
SSD_CONV = 4
CHUNK = 128
SSD_CONV_DIM = 896
SGU_W = 256
SGU_GROUPS = 4
D_IN = 2950
RMS_EPS = 1e-6
LN_EPS = 1e-5
NEG = -1e30

PW = 3072
Q0, K0, V0, Z0, U0, XBC0, DT0, VS0 = 0, 384, 768, 1152, 1536, 1792, 2688, 2816

ADAM_LR, ADAM_B1, ADAM_B2, ADAM_EPS, ADAM_WD, ADAM_STEP = 0.001, 0.9, 0.999, 1e-08, 0.01, 10


def _cp(sem=None):
    return pltpu.CompilerParams(dimension_semantics=sem, vmem_limit_bytes=VMEM_LIMIT_BYTES)


def _tile(n, cap, mult=LANE):
    best = None
    t = mult
    while t <= min(n, cap):
        if n % t == 0:
            best = t
        t += mult
    return best if best is not None else n


def _dot(a, b, prec=None):
    return jnp.dot(a, b, preferred_element_type=F32, precision=prec)


def _dot_nt(a, b, prec=None):
    return lax.dot_general(a, b, (((1,), (1,)), ((), ())), preferred_element_type=F32, precision=prec)


def _dot_tn(a, b, prec=None):
    return lax.dot_general(a, b, (((0,), (0,)), ((), ())), preferred_element_type=F32, precision=prec)


def _sigmoid(x):
    return 1.0 / (1.0 + jnp.exp(-x))


def _silu(x):
    return x * _sigmoid(x)


def _gelu(x):
    return 0.5 * x * (1.0 + lax.erf(x * 0.7071067811865476))


def _softplus(x):
    return jnp.maximum(x, 0.0) + jnp.log(1.0 + jnp.exp(-jnp.abs(x)))


def _rms_fwd(x, g):
    rstd = lax.rsqrt(jnp.mean(x * x, axis=-1, keepdims=True) + RMS_EPS)
    xhat = x * rstd
    return xhat * g, xhat, rstd


def _rms_bwd(dy, xhat, rstd, g):
    dxhat = dy * g
    dx = rstd * (dxhat - xhat * jnp.mean(dxhat * xhat, axis=-1, keepdims=True))
    return dx, dy * xhat


def _resident(shape):
    return pl.BlockSpec(shape, lambda *_: (0,) * len(shape), pipeline_mode=pl.Buffered(1))


def _mm(a, b, mode, name, out_dtype=F32, residual=None, tm_cap=1024, tn_cap=1024, tk_cap=1024):
    if mode == "nn":
        (M, K), (_, N) = a.shape, b.shape
    elif mode == "nt":
        (M, K), (N, _) = a.shape, b.shape
    else:
        (K, M), (_, N) = a.shape, b.shape
    tm, tn, tk = _tile(M, tm_cap), _tile(N, tn_cap), _tile(K, tk_cap)
    nk = K // tk
    if mode == "tn":
        a_spec = pl.BlockSpec((tk, tm), lambda i, j, k: (k, i))
    else:
        a_spec = pl.BlockSpec((tm, tk), lambda i, j, k: (i, k))
    if mode == "nt":
        b_spec = pl.BlockSpec((tn, tk), lambda i, j, k: (j, k))
    else:
        b_spec = pl.BlockSpec((tk, tn), lambda i, j, k: (k, j))
    o_spec = pl.BlockSpec((tm, tn), lambda i, j, k: (i, j))
    has_res = residual is not None

    def prod(a_ref, b_ref):
        av = a_ref[...].astype(BF16)
        bv = b_ref[...].astype(BF16)
        if mode == "nn":
            return _dot(av, bv)
        if mode == "nt":
            return _dot_nt(av, bv)
        return _dot_tn(av, bv)

    def body(*refs):
        a_ref, b_ref = refs[:2]
        r_ref = refs[2] if has_res else None
        o_ref = refs[2 + has_res]
        if nk == 1:
            o = prod(a_ref, b_ref)
            if has_res:
                o = r_ref[...] + o
            o_ref[...] = o.astype(out_dtype)
            return
        acc = refs[3 + has_res]
        k = pl.program_id(2)

        @pl.when(k == 0)
        def _():
            acc[...] = jnp.zeros_like(acc)

        acc[...] += prod(a_ref, b_ref)

        @pl.when(k == nk - 1)
        def _():
            o = acc[...]
            if has_res:
                o = r_ref[...] + o
            o_ref[...] = o.astype(out_dtype)

    ins = [a, b] + ([residual] if has_res else [])
    in_specs = [a_spec, b_spec] + ([o_spec] if has_res else [])
    return pl.pallas_call(
        body, name=name, grid=(M // tm, N // tn, nk),
        in_specs=in_specs, out_specs=o_spec,
        out_shape=jax.ShapeDtypeStruct((M, N), out_dtype),
        scratch_shapes=[pltpu.VMEM((tm, tn), F32)] if nk > 1 else [],
        compiler_params=_cp(("parallel", "parallel", "arbitrary")),
    )(*ins)


def _after(after):
    return ([after], [_ANY]) if after is not None else ([], [])


def _ffn_fwd(x, g, wgt, wut, wd, name, after=None):
    T, D = x.shape
    F = wgt.shape[0]
    tm = _tile(T, 256)

    def body(x_ref, g_ref, wg_ref, wu_ref, wd_ref, *rest):
        out_ref, dgf_ref, sl_ref, actt_ref = rest[-4:]
        xv = x_ref[...]
        xn = _rms_fwd(xv, g_ref[...])[0].astype(BF16)
        gate = _dot_nt(xn, wg_ref[...])
        up = _dot_nt(xn, wu_ref[...])
        sig = _sigmoid(gate)
        sl = gate * sig
        dgf_ref[...] = (up * (sig + sl * (1.0 - sig))).astype(BF16)
        sl_ref[...] = sl.astype(BF16)
        act = (sl * up).astype(BF16)
        actt_ref[...] = act.T
        out_ref[...] = xv + 0.5 * _dot(act, wd_ref[...])

    row = lambda w: pl.BlockSpec((tm, w), lambda i: (i, 0))
    extra, extra_specs = _after(after)
    return pl.pallas_call(
        body, name=name, grid=(T // tm,),
        in_specs=[row(D), _resident((1, D)), _resident((F, D)), _resident((F, D)), _resident((F, D))] + extra_specs,
        out_specs=[row(D), row(F), row(F), pl.BlockSpec((F, tm), lambda i: (0, i))],
        out_shape=[jax.ShapeDtypeStruct((T, D), F32),
                   jax.ShapeDtypeStruct((T, F), BF16),
                   jax.ShapeDtypeStruct((T, F), BF16),
                   jax.ShapeDtypeStruct((F, T), BF16)],
        compiler_params=_cp(("parallel",)),
    )(x, g, wgt, wut, wd, *extra)


def _ffn_bwd_dx(dout, x, g, dgf, sl, wg, wu, wd, name):
    T, D = x.shape
    F = wg.shape[0]
    tm = _tile(T, 256)

    def body(dout_ref, x_ref, g_ref, dgf_ref, sl_ref, wg_ref, wu_ref, wd_ref,
             dx_ref, dgt_ref, dut_ref, xn_ref, dacc_ref, dg_ref):
        @pl.when(pl.program_id(0) == 0)
        def _():
            dg_ref[...] = jnp.zeros_like(dg_ref)

        gv = g_ref[...]
        dout_v = dout_ref[...]
        xn, xhat, rstd = _rms_fwd(x_ref[...], gv)
        xn_ref[...] = xn.astype(BF16)
        dacc = (0.5 * dout_v).astype(BF16)
        dacc_ref[...] = dacc
        dact = _dot_nt(dacc, wd_ref[...])
        dgate = (dact * dgf_ref[...].astype(F32)).astype(BF16)
        dup = (dact * sl_ref[...].astype(F32)).astype(BF16)
        dgt_ref[...] = dgate.T
        dut_ref[...] = dup.T
        dxn = _dot(dgate, wg_ref[...]) + _dot(dup, wu_ref[...])
        dx, dgrow = _rms_bwd(dxn, xhat, rstd, gv)
        dx_ref[...] = dout_v + dx
        dg_ref[...] += jnp.sum(dgrow, axis=0, keepdims=True)

    row = lambda w: pl.BlockSpec((tm, w), lambda i: (i, 0))
    tr = pl.BlockSpec((F, tm), lambda i: (0, i))
    return pl.pallas_call(
        body, name=name, grid=(T // tm,),
        in_specs=[row(D), row(D), _resident((1, D)), row(F), row(F),
                  _resident((F, D)), _resident((F, D)), _resident((F, D))],
        out_specs=[row(D), tr, tr, row(D), row(D), pl.BlockSpec((1, D), lambda i: (0, 0))],
        out_shape=[jax.ShapeDtypeStruct((T, D), F32)] + [jax.ShapeDtypeStruct((F, T), BF16)] * 2
        + [jax.ShapeDtypeStruct((T, D), BF16)] * 2 + [jax.ShapeDtypeStruct((1, D), F32)],
        compiler_params=_cp(("arbitrary",)),
    )(dout, x, g, dgf, sl, wg, wu, wd)


def _ffn_dw(dgt, dut, actt, xn, dacc, name):
    F, T = dgt.shape
    D = xn.shape[1]
    th = _tile(F, 256)

    def body(dg_ref, du_ref, a_ref, xn_ref, dacc_ref, dwg_ref, dwu_ref, dwd_ref):
        xv = xn_ref[...]
        dwg_ref[...] = _dot(dg_ref[...], xv).astype(BF16)
        dwu_ref[...] = _dot(du_ref[...], xv).astype(BF16)
        dwd_ref[...] = _dot(a_ref[...], dacc_ref[...]).astype(BF16)

    tile = pl.BlockSpec((th, T), lambda j: (j, 0))
    out = pl.BlockSpec((th, D), lambda j: (j, 0))
    return pl.pallas_call(
        body, name=name, grid=(F // th,),
        in_specs=[tile, tile, tile, _resident((T, D)), _resident((T, D))],
        out_specs=[out, out, out], out_shape=[jax.ShapeDtypeStruct((F, D), BF16)] * 3,
        compiler_params=_cp(("parallel",)),
    )(dgt, dut, actt, xn, dacc)


def _dw_one(lt, r, name, after=None):
    F, T = lt.shape
    D = r.shape[1]
    th = _tile(F, 256)
    extra, extra_specs = _after(after)

    def body(l_ref, r_ref, *rest):
        rest[-1][...] = _dot(l_ref[...], r_ref[...]).astype(BF16)

    return pl.pallas_call(
        body, name=name, grid=(F // th,),
        in_specs=[pl.BlockSpec((th, T), lambda j: (j, 0)), _resident((T, D))] + extra_specs,
        out_specs=pl.BlockSpec((th, D), lambda j: (j, 0)),
        out_shape=jax.ShapeDtypeStruct((F, D), BF16),
        compiler_params=_cp(("parallel",)),
    )(lt, r, *extra)


def _norm_mm(x, g, w, name):
    T, D = x.shape
    N = w.shape[1]
    tm = _tile(T, 512)

    def body(x_ref, g_ref, w_ref, o_ref, ht_ref):
        xn = _rms_fwd(x_ref[...], g_ref[...])[0]
        ht_ref[...] = xn.T.astype(BF16)
        o_ref[...] = _dot(xn.astype(BF16), w_ref[...])

    return pl.pallas_call(
        body, name=name, grid=(T // tm,),
        in_specs=[pl.BlockSpec((tm, D), lambda i: (i, 0)), _resident((1, D)), _resident((D, N))],
        out_specs=[pl.BlockSpec((tm, N), lambda i: (i, 0)), pl.BlockSpec((D, tm), lambda i: (0, i))],
        out_shape=[jax.ShapeDtypeStruct((T, N), F32), jax.ShapeDtypeStruct((D, T), BF16)],
        compiler_params=_cp(("parallel",)),
    )(x, g, w)


def _norm_mm_bwd(dproj, x, g, w, dres, name):
    T, D = x.shape
    N = w.shape[1]
    tm = _tile(T, 512)

    def body(dp_ref, x_ref, g_ref, w_ref, dres_ref, dx_ref, dg_ref):
        @pl.when(pl.program_id(0) == 0)
        def _():
            dg_ref[...] = jnp.zeros_like(dg_ref)

        gv = g_ref[...]
        dh = _dot_nt(dp_ref[...], w_ref[...])
        _, xhat, rstd = _rms_fwd(x_ref[...], gv)
        dx, dgrow = _rms_bwd(dh, xhat, rstd, gv)
        dx_ref[...] = dres_ref[...] + dx
        dg_ref[...] += jnp.sum(dgrow, axis=0, keepdims=True)

    row = pl.BlockSpec((tm, D), lambda i: (i, 0))
    one = pl.BlockSpec((1, D), lambda i: (0, 0))
    return pl.pallas_call(
        body, name=name, grid=(T // tm,),
        in_specs=[pl.BlockSpec((tm, N), lambda i: (i, 0)), row, _resident((1, D)), _resident((D, N)), row],
        out_specs=[row, one],
        out_shape=[jax.ShapeDtypeStruct((T, D), F32), jax.ShapeDtypeStruct((1, D), F32)],
        compiler_params=_cp(("arbitrary",)),
    )(dproj, x, g, w, dres)


def _mm_resident_lhs(at, b, name, tn_cap=1024):
    M, K = at.shape
    N = b.shape[1]
    tn = _tile(N, tn_cap)

    def body(a_ref, b_ref, o_ref):
        o_ref[...] = _dot(a_ref[...], b_ref[...]).astype(BF16)

    return pl.pallas_call(
        body, name=name, grid=(N // tn,),
        in_specs=[_resident((M, K)), pl.BlockSpec((K, tn), lambda j: (0, j))],
        out_specs=pl.BlockSpec((M, tn), lambda j: (0, j)),
        out_shape=jax.ShapeDtypeStruct((M, N), BF16),
        compiler_params=_cp(("parallel",)),
    )(at, b)


def _final_loss(x, g, target, name):
    T, D = x.shape
    tm = _tile(T, 1024)

    def body(x_ref, g_ref, t_ref, loss_ref, dx_ref, dg_ref):
        @pl.when(pl.program_id(0) == 0)
        def _():
            dg_ref[...] = jnp.zeros_like(dg_ref)
            loss_ref[...] = jnp.zeros_like(loss_ref)

        gv = g_ref[...]
        y, xhat, rstd = _rms_fwd(x_ref[...], gv)
        err = y - t_ref[...]
        part = 0.5 * jnp.sum(jnp.mean(err * err, axis=-1, keepdims=True), axis=0, keepdims=True)
        loss_ref[...] += jnp.broadcast_to(part, loss_ref.shape)
        dy = err * (1.0 / D)
        dx, dgrow = _rms_bwd(dy, xhat, rstd, gv)
        dx_ref[...] = dx
        dg_ref[...] += jnp.sum(dgrow, axis=0, keepdims=True)

    row = pl.BlockSpec((tm, D), lambda i: (i, 0))
    one = pl.BlockSpec((1, D), lambda i: (0, 0))
    return pl.pallas_call(
        body, name=name, grid=(T // tm,),
        in_specs=[row, one, row],
        out_specs=[pl.BlockSpec((1, LANE), lambda i: (0, 0)), row, one],
        out_shape=[jax.ShapeDtypeStruct((1, LANE), F32), jax.ShapeDtypeStruct((T, D), F32),
                   jax.ShapeDtypeStruct((1, D), F32)],
        compiler_params=_cp(("arbitrary",)),
    )(x, g, target)


def _attn_bias(S, bq):
    d = np.arange(bq)[:, None] - np.arange(S)[None, :] + (S // bq - 1) * bq
    ok = d >= 0
    mult = ((ok & (d <= 128)).astype(np.float32) + (ok & (d % 4 == 0) & (d <= 512))
            + (ok & (d % 16 == 0) & (d <= 2048)))
    return jnp.asarray(np.where(mult > 0, np.log(np.maximum(mult, 1.0)), NEG).astype(np.float32))


def _attn_fwd(proj, bias, B, S, name):
    T = B * S
    bq = bias.shape[0]
    nb = S // bq
    qcol, kcol, vcol = Q0 // LANE, K0 // LANE, V0 // LANE

    def body(q_ref, k_ref, v_ref, t_ref, o_ref, lse_ref, ks, vs):
        for hh in range(2):
            sl = slice(HEAD_DIM * hh, HEAD_DIM * (hh + 1))
            ks[hh] = k_ref[:, sl].astype(BF16)
            vs[hh] = v_ref[:, sl].astype(BF16)
        for hh in range(2):
            sl = slice(HEAD_DIM * hh, HEAD_DIM * (hh + 1))
            for qb in range(nb):
                w, off, rows = bq * (qb + 1), (nb - 1 - qb) * bq, slice(qb * bq, (qb + 1) * bq)
                q = (q_ref[rows, sl] * 0.125).astype(BF16)
                s = _dot_nt(q, ks[hh, 0:w, :]) + t_ref[:, off:off + w]
                m = jnp.max(s, axis=-1, keepdims=True)
                p = jnp.exp(s - m)
                l = jnp.sum(p, axis=-1, keepdims=True)
                o_ref[rows, sl] = _dot(p.astype(BF16), vs[hh, 0:w, :]) / l
                lse_ref[rows, hh:hh + 1] = m + jnp.log(l)

    blk = lambda c0: pl.BlockSpec((S, LANE), lambda b, p: (b, c0 + p))
    return pl.pallas_call(
        body, name=name, grid=(B, ATT_W // LANE),
        in_specs=[blk(qcol), blk(kcol), blk(vcol), _resident((bq, S))],
        out_specs=[pl.BlockSpec((S, LANE), lambda b, p: (b, p)),
                   pl.BlockSpec((None, None, S, 2), lambda b, p: (b, p, 0, 0))],
        out_shape=[jax.ShapeDtypeStruct((T, ATT_W), F32),
                   jax.ShapeDtypeStruct((B, ATT_W // LANE, S, 2), F32)],
        scratch_shapes=[pltpu.VMEM((2, S, HEAD_DIM), BF16)] * 2,
        compiler_params=_cp(("parallel", "parallel")),
    )(proj, proj, proj, bias)


def _attn_bwd(proj, o, lse, dy, bias, B, S, name):
    T = B * S
    bq = bias.shape[0]
    nb = S // bq
    qcol, kcol, vcol = Q0 // LANE, K0 // LANE, V0 // LANE

    def body(q_ref, k_ref, v_ref, o_ref, lse_ref, do_ref, t_ref, dq_ref, dk_ref, dv_ref, ks, vs, dks, dvs):
        for hh in range(2):
            sl = slice(HEAD_DIM * hh, HEAD_DIM * (hh + 1))
            ks[hh] = k_ref[:, sl].astype(BF16)
            vs[hh] = v_ref[:, sl].astype(BF16)
        dks[...] = jnp.zeros_like(dks)
        dvs[...] = jnp.zeros_like(dvs)
        for hh in range(2):
            sl = slice(HEAD_DIM * hh, HEAD_DIM * (hh + 1))
            for qb in range(nb):
                w, off, rows = bq * (qb + 1), (nb - 1 - qb) * bq, slice(qb * bq, (qb + 1) * bq)
                q = (q_ref[rows, sl] * 0.125).astype(BF16)
                do = do_ref[rows, sl]
                dob = do.astype(BF16)
                delta = jnp.sum(do * o_ref[rows, sl], axis=-1, keepdims=True)
                k, v = ks[hh, 0:w, :], vs[hh, 0:w, :]
                s = _dot_nt(q, k) + t_ref[:, off:off + w]
                p = jnp.exp(s - lse_ref[rows, hh:hh + 1])
                ds = (p * (_dot_nt(dob, v) - delta)).astype(BF16)
                dq_ref[rows, sl] = (_dot(ds, k) * 0.125).astype(dq_ref.dtype)
                dks[hh, 0:w, :] += _dot_tn(ds, q)
                dvs[hh, 0:w, :] += _dot_tn(p.astype(BF16), dob)
            dk_ref[:, sl] = dks[hh].astype(dk_ref.dtype)
            dv_ref[:, sl] = dvs[hh].astype(dv_ref.dtype)

    blk = lambda c0: pl.BlockSpec((S, LANE), lambda b, p: (b, c0 + p))
    own = pl.BlockSpec((S, LANE), lambda b, p: (b, p))
    return pl.pallas_call(
        body, name=name, grid=(B, ATT_W // LANE),
        in_specs=[blk(qcol), blk(kcol), blk(vcol), own,
                  pl.BlockSpec((None, None, S, 2), lambda b, p: (b, p, 0, 0)), own, _resident((bq, S))],
        out_specs=[own, own, own],
        out_shape=[jax.ShapeDtypeStruct((T, ATT_W), BF16)] * 3,
        scratch_shapes=[pltpu.VMEM((2, S, HEAD_DIM), BF16)] * 2 + [pltpu.VMEM((2, S, HEAD_DIM), F32)] * 2,
        compiler_params=_cp(("parallel", "parallel")),
    )(proj, proj, proj, o, lse, dy, bias)


def _conv_fwd(proj, cw, cb, B, S, name, after=None):
    T = B * S
    nc = SSD_CONV_DIM // LANE
    c0 = XBC0 // LANE
    extra, extra_specs = _after(after)

    def body(x_ref, w_ref, b_ref, *rest):
        o_ref = rest[-1]
        x = x_ref[...]
        t = lax.broadcasted_iota(jnp.int32, (S, 1), 0)
        acc = b_ref[...] + w_ref[SSD_CONV - 1:SSD_CONV, :] * x
        for k in range(SSD_CONV - 1):
            sh = SSD_CONV - 1 - k
            xs = jnp.where(t >= sh, pltpu.roll(x, sh, 0), 0.0)
            acc = acc + w_ref[k:k + 1, :] * xs
        o_ref[...] = acc

    return pl.pallas_call(
        body, name=name, grid=(B, nc),
        in_specs=[pl.BlockSpec((S, LANE), lambda b, j: (b, c0 + j)),
                  pl.BlockSpec((SUBLANE, LANE), lambda b, j: (0, j)),
                  pl.BlockSpec((1, LANE), lambda b, j: (0, j))] + extra_specs,
        out_specs=pl.BlockSpec((S, LANE), lambda b, j: (b, j)),
        out_shape=jax.ShapeDtypeStruct((T, SSD_CONV_DIM), F32),
        compiler_params=_cp(("parallel", "parallel")),
    )(proj, cw, cb, *extra)


def _conv_bwd(dpre, proj, cw, B, S, name):
    T = B * S
    nc = SSD_CONV_DIM // LANE
    c0 = XBC0 // LANE

    def body(d_ref, x_ref, w_ref, dx_ref, dwb_ref):
        @pl.when(pl.program_id(1) == 0)
        def _():
            dwb_ref[...] = jnp.zeros_like(dwb_ref)

        d = d_ref[...]
        x = x_ref[...]
        t = lax.broadcasted_iota(jnp.int32, (S, 1), 0)
        dx = w_ref[SSD_CONV - 1:SSD_CONV, :] * d
        rows = [None] * SUBLANE
        rows[SSD_CONV - 1] = jnp.sum(d * x, axis=0, keepdims=True)
        for k in range(SSD_CONV - 1):
            sh = SSD_CONV - 1 - k
            dx = dx + w_ref[k:k + 1, :] * jnp.where(t < S - sh, pltpu.roll(d, S - sh, 0), 0.0)
            xs = jnp.where(t >= sh, pltpu.roll(x, sh, 0), 0.0)
            rows[k] = jnp.sum(d * xs, axis=0, keepdims=True)
        rows[SSD_CONV] = jnp.sum(d, axis=0, keepdims=True)
        dx_ref[...] = dx.astype(BF16)
        r = lax.broadcasted_iota(jnp.int32, (SUBLANE, LANE), 0)
        upd = jnp.zeros((SUBLANE, LANE), F32)
        for k in range(SSD_CONV + 1):
            upd = upd + jnp.where(r == k, rows[k], 0.0)
        dwb_ref[...] += upd

    return pl.pallas_call(
        body, name=name, grid=(nc, B),
        in_specs=[pl.BlockSpec((S, LANE), lambda j, b: (b, j)),
                  pl.BlockSpec((S, LANE), lambda j, b: (b, c0 + j)),
                  pl.BlockSpec((SUBLANE, LANE), lambda j, b: (0, j))],
        out_specs=[pl.BlockSpec((S, LANE), lambda j, b: (b, j)),
                   pl.BlockSpec((SUBLANE, LANE), lambda j, b: (0, j))],
        out_shape=[jax.ShapeDtypeStruct((T, SSD_CONV_DIM), BF16),
                   jax.ShapeDtypeStruct((SUBLANE, SSD_CONV_DIM), F32)],
        compiler_params=_cp(("parallel", "arbitrary")),
    )(dpre, proj, cw)


def _ssd_consts():
    e = np.zeros((LANE, SSD_W), np.float32)
    p = np.zeros((SUBLANE, SSD_W), np.float32)
    for h in range(SSD_HEADS):
        e[h, HEAD_DIM * h:HEAD_DIM * (h + 1)] = 1.0
        p[h, HEAD_DIM * h] = 1.0
    return jnp.asarray(e), jnp.asarray(p)


def _ssd_chunk(pre, z, dtr, sprev, par, e_mat, psel):
    L = CHUNK
    xc = _silu(pre)
    xs, bm, cm = xc[:, :SSD_W], xc[:, SSD_W:SSD_W + 2 * SSD_STATE], xc[:, SSD_W + 2 * SSD_STATE:]
    dtb, alog, dskip, ng = par[0:1], par[1:2], par[2:3], par[3:4]
    dt = _softplus(_dot(dtr, e_mat, HI) + dtb)
    a = dt * (-jnp.exp(alog))
    X = xs * dt
    ri = lax.broadcasted_iota(jnp.int32, (L, L), 0)
    ci = lax.broadcasted_iota(jnp.int32, (L, L), 1)
    tril = ri >= ci
    acs = _dot(tril.astype(F32), a, HI)
    acs_t = _dot_nt(psel, acs, HI)
    ecs = jnp.exp(acs)
    alast = acs[L - 1:L, :]
    xd = (X * jnp.exp(alast - acs)).astype(BF16)
    xb = X.astype(BF16)
    col = lax.broadcasted_iota(jnp.int32, (1, SSD_W), 1)
    sb = sprev.astype(BF16)
    bgs = [bm[:, SSD_STATE * g:SSD_STATE * (g + 1)].astype(BF16) for g in range(2)]
    cgs = [cm[:, SSD_STATE * g:SSD_STATE * (g + 1)].astype(BF16) for g in range(2)]
    cbs = [_dot_nt(cgs[g], bgs[g]) for g in range(2)]
    first = lax.broadcasted_iota(jnp.int32, (1, LANE), 1) < HEAD_DIM
    y_tiles, s_tiles = [], []
    for t in range(SSD_W // LANE):
        cl = slice(LANE * t, LANE * (t + 1))
        xb_t, xd_t, sb_t = xb[:, cl], xd[:, cl], sb[:, cl]
        per_head = []
        for h in (2 * t, 2 * t + 1):
            seg = acs[:, HEAD_DIM * h:HEAD_DIM * h + 1] - acs_t[h:h + 1, :]
            dec = jnp.exp(jnp.where(tril, seg, NEG))
            per_head.append(_dot((cbs[h // 3] * dec).astype(BF16), xb_t))
        y_t = jnp.where(first, per_head[0], per_head[1])
        ga, gb = (2 * t) // 3, (2 * t + 1) // 3
        if ga == gb:
            y_off, s_add = _dot(cgs[ga], sb_t), _dot_tn(bgs[ga], xd_t)
        else:
            y_off = jnp.where(first, _dot(cgs[ga], sb_t), _dot(cgs[gb], sb_t))
            s_add = jnp.where(first, _dot_tn(bgs[ga], xd_t), _dot_tn(bgs[gb], xd_t))
        y_tiles.append(y_t + y_off * ecs[:, cl])
        s_tiles.append(s_add)
    y = dskip * xs + jnp.concatenate(y_tiles, axis=1)
    snew = sprev * jnp.exp(alast) + jnp.concatenate(s_tiles, axis=1)
    yg = y * _silu(z)
    sq = yg * yg
    g0 = col < SSD_W // 2
    ms0 = jnp.sum(jnp.where(g0, sq, 0.0), axis=-1, keepdims=True) * (2.0 / SSD_W)
    ms1 = jnp.sum(jnp.where(g0, 0.0, sq), axis=-1, keepdims=True) * (2.0 / SSD_W)
    r = jnp.where(g0, lax.rsqrt(ms0 + RMS_EPS), lax.rsqrt(ms1 + RMS_EPS))
    return yg * r * ng, snew


SSD_CHUNKS_PER_STEP = 4


def _ssd_chunks_per_step(S):
    k = SSD_CHUNKS_PER_STEP
    while (S // CHUNK) % k:
        k //= 2
    return k


def _ssd_fwd(pre, proj, par, B, S, name):
    T = B * S
    k = _ssd_chunks_per_step(S)
    nc, rows = S // (CHUNK * k), CHUNK * k
    e_mat, psel = _ssd_consts()

    def body(pre_ref, z_ref, dt_ref, par_ref, e_ref, p_ref, y_ref, sall_ref, st):
        @pl.when(pl.program_id(1) == 0)
        def _():
            st[...] = jnp.zeros_like(st)

        sprev = st[...]
        for i in range(k):
            r = slice(CHUNK * i, CHUNK * (i + 1))
            sall_ref[i] = sprev
            y, sprev = _ssd_chunk(pre_ref[r, :], z_ref[r, :], dt_ref[r, :], sprev, par_ref[...], e_ref[...],
                                  p_ref[...])
            y_ref[r, :] = y.astype(BF16)
        st[...] = sprev

    row = lambda b, c: b * nc + c
    full = lambda shp: pl.BlockSpec(shp, lambda b, c: (0, 0))
    return pl.pallas_call(
        body, name=name, grid=(B, nc),
        in_specs=[pl.BlockSpec((rows, SSD_CONV_DIM), lambda b, c: (row(b, c), 0)),
                  pl.BlockSpec((rows, SSD_W), lambda b, c: (row(b, c), Z0 // SSD_W)),
                  pl.BlockSpec((rows, LANE), lambda b, c: (row(b, c), DT0 // LANE)),
                  full((SUBLANE, SSD_W)), full((LANE, SSD_W)), full((SUBLANE, SSD_W))],
        out_specs=[pl.BlockSpec((rows, SSD_W), lambda b, c: (row(b, c), 0)),
                   pl.BlockSpec((k, SSD_STATE, SSD_W), lambda b, c: (row(b, c), 0, 0))],
        out_shape=[jax.ShapeDtypeStruct((T, SSD_W), BF16),
                   jax.ShapeDtypeStruct((B * nc * k, SSD_STATE, SSD_W), F32)],
        scratch_shapes=[pltpu.VMEM((SSD_STATE, SSD_W), F32)],
        compiler_params=_cp(("parallel", "arbitrary")),
    )(pre, proj, proj, par, e_mat, psel)


def _ssd_bwd(pre, proj, sall, dy, par, B, S, name):
    T = B * S
    k = _ssd_chunks_per_step(S)
    nc, rows = S // (CHUNK * k), CHUNK * k
    e_mat, psel = _ssd_consts()

    def body(pre_ref, z_ref, dt_ref, sall_ref, dy_ref, par_ref, e_ref, p_ref,
             dpre_ref, dz_ref, ddt_ref, dpar_ref, ds):
        b, c = pl.program_id(0), pl.program_id(1)

        @pl.when(c == 0)
        def _():
            ds[...] = jnp.zeros_like(ds)

        @pl.when((b == 0) & (c == 0))
        def _():
            dpar_ref[...] = jnp.zeros_like(dpar_ref)

        e_v, p_v = e_ref[...], p_ref[...]
        fn = lambda pre, z, dtr, sprev, par: _ssd_chunk(pre, z, dtr, sprev, par, e_v, p_v)
        dstate, dpar_sum = ds[...], None
        for i in reversed(range(k)):
            r = slice(CHUNK * i, CHUNK * (i + 1))
            _, vjp = jax.vjp(fn, pre_ref[r, :], z_ref[r, :], dt_ref[r, :], sall_ref[i], par_ref[...])
            dpre, dz, ddt, dstate, dpar = vjp((dy_ref[r, :], dstate))
            dpre_ref[r, :] = dpre
            dz_ref[r, :] = dz.astype(BF16)
            ddt_ref[r, :] = ddt.astype(BF16)
            dpar_sum = dpar if dpar_sum is None else dpar_sum + dpar
        dpar_ref[...] += dpar_sum
        ds[...] = dstate

    row = lambda b, c: b * nc + (nc - 1 - c)
    full = lambda shp: pl.BlockSpec(shp, lambda b, c: (0, 0))
    return pl.pallas_call(
        body, name=name, grid=(B, nc),
        in_specs=[pl.BlockSpec((rows, SSD_CONV_DIM), lambda b, c: (row(b, c), 0)),
                  pl.BlockSpec((rows, SSD_W), lambda b, c: (row(b, c), Z0 // SSD_W)),
                  pl.BlockSpec((rows, LANE), lambda b, c: (row(b, c), DT0 // LANE)),
                  pl.BlockSpec((k, SSD_STATE, SSD_W), lambda b, c: (row(b, c), 0, 0)),
                  pl.BlockSpec((rows, SSD_W), lambda b, c: (row(b, c), ATT_W // SSD_W)),
                  full((SUBLANE, SSD_W)), full((LANE, SSD_W)), full((SUBLANE, SSD_W))],
        out_specs=[pl.BlockSpec((rows, SSD_CONV_DIM), lambda b, c: (row(b, c), 0)),
                   pl.BlockSpec((rows, SSD_W), lambda b, c: (row(b, c), 0)),
                   pl.BlockSpec((rows, LANE), lambda b, c: (row(b, c), 0)),
                   full((SUBLANE, SSD_W))],
        out_shape=[jax.ShapeDtypeStruct((T, SSD_CONV_DIM), F32),
                   jax.ShapeDtypeStruct((T, SSD_W), BF16),
                   jax.ShapeDtypeStruct((T, LANE), BF16),
                   jax.ShapeDtypeStruct((SUBLANE, SSD_W), F32)],
        scratch_shapes=[pltpu.VMEM((SSD_STATE, SSD_W), F32)],
        compiler_params=_cp(("arbitrary", "arbitrary")),
    )(pre, proj, proj, sall, dy, par, e_mat, psel)


def _sgu_consts():
    e = np.zeros((SUBLANE, SGU_W), np.float32)
    for g in range(SGU_GROUPS):
        e[g, HEAD_DIM * g:HEAD_DIM * (g + 1)] = 1.0
    return jnp.asarray(e)


def _sgu_chunk(u_raw, v_raw, ln, w, bst, e4):
    L = CHUNK
    u = _gelu(u_raw)
    v = _gelu(v_raw)
    mu = jnp.mean(v, axis=-1, keepdims=True)
    vc = v - mu
    var = jnp.mean(vc * vc, axis=-1, keepdims=True)
    vn = vc * lax.rsqrt(var + LN_EPS) * ln[0:1] + ln[1:2]
    vb = vn.astype(BF16)
    ri = lax.broadcasted_iota(jnp.int32, (L, L), 0)
    ci = lax.broadcasted_iota(jnp.int32, (L, L), 1)
    tril = ri >= ci
    col = lax.broadcasted_iota(jnp.int32, (1, SGU_W), 1)
    mixed = _dot(bst, e4, HI)
    for g in range(SGU_GROUPS):
        wc = jnp.where(tril, w[g], 0.0).astype(BF16)
        gm = (col >= HEAD_DIM * g) & (col < HEAD_DIM * (g + 1))
        mixed = mixed + jnp.where(gm, _dot(wc, vb), 0.0)
    return u * mixed


def _sgu_fwd(proj, ln, w, bst, B, S, name, after=None):
    T = B * S
    e4 = _sgu_consts()
    extra, extra_specs = _after(after)
    k = max(d for d in (8, 4, 2, 1) if (T // CHUNK) % d == 0)
    rows = k * CHUNK

    def body(u_ref, v_ref, ln_ref, w_ref, b_ref, e_ref, *rest):
        y_ref = rest[-1]
        for i in range(k):
            r = slice(CHUNK * i, CHUNK * (i + 1))
            y_ref[r, :] = _sgu_chunk(u_ref[r, :], v_ref[r, :], ln_ref[...], w_ref[...], b_ref[...],
                                     e_ref[...]).astype(BF16)

    return pl.pallas_call(
        body, name=name, grid=(T // rows,),
        in_specs=[pl.BlockSpec((rows, SGU_W), lambda i: (i, U0 // SGU_W)),
                  pl.BlockSpec((rows, SGU_W), lambda i: (i, VS0 // SGU_W)),
                  pl.BlockSpec((SUBLANE, SGU_W), lambda i: (0, 0)),
                  pl.BlockSpec((SGU_GROUPS, CHUNK, CHUNK), lambda i: (0, 0, 0)),
                  pl.BlockSpec((CHUNK, SUBLANE), lambda i: (0, 0)),
                  pl.BlockSpec((SUBLANE, SGU_W), lambda i: (0, 0))] + extra_specs,
        out_specs=pl.BlockSpec((rows, SGU_W), lambda i: (i, 0)),
        out_shape=jax.ShapeDtypeStruct((T, SGU_W), BF16),
        compiler_params=_cp(("parallel",)),
    )(proj, proj, ln, w, bst, e4, *extra)


def _sgu_bwd(proj, dy, ln, w, bst, B, S, name):
    T = B * S
    e4 = _sgu_consts()
    ycol = (ATT_W + SSD_W) // SGU_W
    k = max(d for d in (4, 2, 1) if (T // CHUNK) % d == 0)
    rows = k * CHUNK

    def body(u_ref, v_ref, dy_ref, ln_ref, w_ref, b_ref, e_ref, du_ref, dv_ref, dln_ref, dw_ref, db_ref):
        @pl.when(pl.program_id(0) == 0)
        def _():
            dln_ref[...] = jnp.zeros_like(dln_ref)
            dw_ref[...] = jnp.zeros_like(dw_ref)
            db_ref[...] = jnp.zeros_like(db_ref)

        e_v = e_ref[...]
        fn = lambda u, v, ln, w, b: _sgu_chunk(u, v, ln, w, b, e_v)
        acc = None
        for i in range(k):
            r = slice(CHUNK * i, CHUNK * (i + 1))
            _, vjp = jax.vjp(fn, u_ref[r, :], v_ref[r, :], ln_ref[...], w_ref[...], b_ref[...])
            du, dv, *dpar = vjp(dy_ref[r, :])
            du_ref[r, :] = du.astype(BF16)
            dv_ref[r, :] = dv.astype(BF16)
            acc = dpar if acc is None else [a + d for a, d in zip(acc, dpar)]
        dln_ref[...] += acc[0]
        dw_ref[...] += acc[1]
        db_ref[...] += acc[2]

    c_ln = pl.BlockSpec((SUBLANE, SGU_W), lambda i: (0, 0))
    c_w = pl.BlockSpec((SGU_GROUPS, CHUNK, CHUNK), lambda i: (0, 0, 0))
    c_b = pl.BlockSpec((CHUNK, SUBLANE), lambda i: (0, 0))
    return pl.pallas_call(
        body, name=name, grid=(T // rows,),
        in_specs=[pl.BlockSpec((rows, SGU_W), lambda i: (i, U0 // SGU_W)),
                  pl.BlockSpec((rows, SGU_W), lambda i: (i, VS0 // SGU_W)),
                  pl.BlockSpec((rows, SGU_W), lambda i: (i, ycol)),
                  c_ln, c_w, c_b, pl.BlockSpec((SUBLANE, SGU_W), lambda i: (0, 0))],
        out_specs=[pl.BlockSpec((rows, SGU_W), lambda i: (i, 0)),
                   pl.BlockSpec((rows, SGU_W), lambda i: (i, 0)), c_ln, c_w, c_b],
        out_shape=[jax.ShapeDtypeStruct((T, SGU_W), BF16), jax.ShapeDtypeStruct((T, SGU_W), BF16),
                   jax.ShapeDtypeStruct((SUBLANE, SGU_W), F32),
                   jax.ShapeDtypeStruct((SGU_GROUPS, CHUNK, CHUNK), F32),
                   jax.ShapeDtypeStruct((CHUNK, SUBLANE), F32)],
        compiler_params=_cp(("arbitrary",)),
    )(proj, proj, dy, ln, w, bst, e4)


_HBM = pl.BlockSpec(memory_space=pltpu.HBM)
_SEM = pl.BlockSpec(memory_space=pltpu.SEMAPHORE)
_ANY = pl.BlockSpec(memory_space=pl.ANY)
_EFFECT = pltpu.SideEffectType.DATAFLOW_SIDE_EFFECTING


def _peers():
    x, y, c = lax.axis_index("x"), lax.axis_index("y"), lax.axis_index("c")
    out = []
    for p in range(1, N_DEV):
        px, py, pc = x ^ ((p >> 2) & 1), y ^ ((p >> 1) & 1), c ^ (p & 1)
        out.append(((px, py, pc), 4 * px + 2 * py + pc))
    return 4 * x + 2 * y + c, out


def _xchg_start(xs, a2a, order, name):
    n = len(xs)
    lands = [lax.empty(a.shape if f else (N_DEV,) + a.shape, a.dtype) for a, f in zip(xs, a2a)]

    def body(*refs):
        ins, zones = refs[:n], refs[n:2 * n]
        send_sems, recv_sems = refs[2 * n + 1], refs[2 * n + 2]
        token = refs[-1]
        me, peers = _peers()
        for p, (dev, peer) in enumerate(peers):
            for t in range(n):
                pltpu.make_async_remote_copy(
                    src_ref=ins[t].at[peer] if a2a[t] else ins[t], dst_ref=zones[t].at[me],
                    send_sem=send_sems.at[p * n + t], recv_sem=recv_sems.at[p * n + t],
                    device_id=dev, device_id_type=MESH).start()
        token[...] = jnp.zeros_like(token)

    hbm = lambda a: pltpu.HBM(a.shape, a.dtype)
    sems = pltpu.SemaphoreType.DMA(((N_DEV - 1) * n,))
    out = pl.pallas_call(
        body, name=name,
        in_specs=[_HBM] * (2 * n) + [_ANY],
        out_specs=[_SEM, _SEM] + [_HBM] * (2 * n) + [pl.BlockSpec(memory_space=pltpu.VMEM)],
        out_shape=[sems, sems] + [hbm(a) for a in xs] + [hbm(a) for a in lands]
        + [jax.ShapeDtypeStruct((SUBLANE, LANE), F32)],
        input_output_aliases={t: 2 + t for t in range(2 * n)},
        compiler_params=pltpu.CompilerParams(has_side_effects=_EFFECT),
    )(*[pltpu.with_memory_space_constraint(a, pltpu.HBM) for a in list(xs) + list(lands)], order)
    return out[0], out[1], out[2:2 + n], out[2 + n:2 + 2 * n], out[-1]


def _xchg_wait(started, a2a, after, name):
    send_sems, recv_sems, xs, lands, _ = started
    n = len(xs)

    def body(*refs):
        ins, zones = refs[:n], refs[n:2 * n]
        send_s, recv_s = refs[2 * n], refs[2 * n + 1]
        me, peers = _peers()
        cps = []
        for p, (dev, peer) in enumerate(peers):
            for t in range(n):
                cps.append(pltpu.make_async_remote_copy(
                    src_ref=ins[t].at[peer] if a2a[t] else ins[t], dst_ref=zones[t].at[peer],
                    send_sem=send_s.at[p * n + t], recv_sem=recv_s.at[p * n + t],
                    device_id=dev, device_id_type=MESH))
        for cp in cps:
            cp.wait_recv()
        for cp in cps:
            cp.wait_send()

    hbm = lambda a: pltpu.HBM(a.shape, a.dtype)
    out = pl.pallas_call(
        body, name=name,
        in_specs=[_HBM] * (2 * n) + [_SEM, _SEM, _ANY],
        out_specs=[_HBM] * (2 * n),
        out_shape=[hbm(a) for a in xs] + [hbm(a) for a in lands],
        input_output_aliases={t: t for t in range(2 * n)},
        compiler_params=pltpu.CompilerParams(has_side_effects=_EFFECT),
    )(*xs, *lands, send_sems, recv_sems, after)
    return out[:n], out[n:]


def _chip_peers():
    x, y, c = lax.axis_index("x"), lax.axis_index("y"), lax.axis_index("c")
    chips = [(1 - x, y), (x, 1 - y), (1 - x, 1 - y)]
    slot = lambda px, py, pc: 4 * px + 2 * py + pc
    return (x, y, c), chips, slot


def _gather_start(xs, order, name):
    n = len(xs)
    lands = [lax.empty((N_DEV,) + a.shape, a.dtype) for a in xs]

    def body(*refs):
        ins, zones = refs[:n], refs[n:2 * n]
        send_sems, d2d_sems, ici_sems = refs[2 * n + 1:2 * n + 4]
        token = refs[-1]
        (x, y, c), chips, slot = _chip_peers()
        me = slot(x, y, c)
        for t in range(n):
            pltpu.make_async_copy(ins[t], zones[t].at[me], d2d_sems.at[n + t]).start()
            for j, (px, py) in enumerate(chips):
                pltpu.make_async_remote_copy(
                    src_ref=ins[t], dst_ref=zones[t].at[me], send_sem=send_sems.at[(1 + j) * n + t],
                    recv_sem=ici_sems.at[j * n + t], device_id=(px, py, c), device_id_type=MESH).start()
            pltpu.make_async_remote_copy(
                src_ref=ins[t], dst_ref=zones[t].at[me], send_sem=send_sems.at[t],
                recv_sem=d2d_sems.at[t], device_id=(x, y, 1 - c), device_id_type=MESH).start()
        token[...] = jnp.zeros_like(token)

    hbm = lambda a: pltpu.HBM(a.shape, a.dtype)
    dma = lambda k: pltpu.SemaphoreType.DMA((k,))
    out = pl.pallas_call(
        body, name=name,
        in_specs=[_HBM] * (2 * n) + [_ANY],
        out_specs=[_SEM, _SEM, _SEM] + [_HBM] * (2 * n) + [pl.BlockSpec(memory_space=pltpu.VMEM)],
        out_shape=[dma(4 * n), dma(2 * n), dma(3 * n)] + [hbm(a) for a in xs] + [hbm(a) for a in lands]
        + [jax.ShapeDtypeStruct((SUBLANE, LANE), F32)],
        input_output_aliases={t: 3 + t for t in range(2 * n)},
        compiler_params=pltpu.CompilerParams(has_side_effects=_EFFECT),
    )(*[pltpu.with_memory_space_constraint(a, pltpu.HBM) for a in list(xs) + list(lands)], order)
    return dict(send=out[0], d2d=out[1], ici=out[2], xs=out[3:3 + n], lands=out[3 + n:3 + 2 * n], token=out[-1])


def _gather_relay(st, after, name):
    n = len(st["xs"])

    def body(*refs):
        zones, ici_sems = refs[:n], refs[n]
        fsend, frecv = refs[n + 2], refs[n + 3]
        token = refs[-1]
        (x, y, c), chips, slot = _chip_peers()
        for t in range(n):
            for j, (px, py) in enumerate(chips):
                blk = zones[t].at[slot(px, py, c)]
                fwd = pltpu.make_async_remote_copy(
                    src_ref=blk, dst_ref=blk, send_sem=fsend.at[j * n + t], recv_sem=ici_sems.at[j * n + t],
                    device_id=(x, y, 1 - c), device_id_type=MESH)
                fwd.wait_recv()
                pltpu.make_async_remote_copy(
                    src_ref=blk, dst_ref=blk, send_sem=fsend.at[j * n + t], recv_sem=frecv.at[j * n + t],
                    device_id=(x, y, 1 - c), device_id_type=MESH).start()
        token[...] = jnp.zeros_like(token)

    hbm = lambda a: pltpu.HBM(a.shape, a.dtype)
    dma = lambda k: pltpu.SemaphoreType.DMA((k,))
    out = pl.pallas_call(
        body, name=name,
        in_specs=[_HBM] * n + [_SEM, _ANY],
        out_specs=[_SEM, _SEM] + [_HBM] * n + [pl.BlockSpec(memory_space=pltpu.VMEM)],
        out_shape=[dma(3 * n), dma(3 * n)] + [hbm(a) for a in st["lands"]]
        + [jax.ShapeDtypeStruct((SUBLANE, LANE), F32)],
        input_output_aliases={t: 2 + t for t in range(n)},
        compiler_params=pltpu.CompilerParams(has_side_effects=_EFFECT),
    )(*st["lands"], st["ici"], after)
    return dict(st, fsend=out[0], frecv=out[1], lands=out[2:2 + n], token=out[-1])


def _gather_wait(st, after, name):
    n = len(st["xs"])

    def body(*refs):
        ins, zones = refs[:n], refs[n:2 * n]
        send_sems, d2d_sems, fsend, frecv = refs[2 * n:2 * n + 4]
        (x, y, c), chips, slot = _chip_peers()
        sib = (x, y, 1 - c)
        for t in range(n):
            pltpu.make_async_copy(ins[t], zones[t].at[slot(x, y, c)], d2d_sems.at[n + t]).wait()
            mine = lambda s, r, dst: pltpu.make_async_remote_copy(
                src_ref=ins[t], dst_ref=dst, send_sem=s, recv_sem=r, device_id=sib, device_id_type=MESH)
            direct = mine(send_sems.at[t], d2d_sems.at[t], zones[t].at[slot(x, y, 1 - c)])
            direct.wait_recv()
            direct.wait_send()
            for j, (px, py) in enumerate(chips):
                mine(send_sems.at[(1 + j) * n + t], d2d_sems.at[t], zones[t].at[slot(px, py, c)]).wait_send()
                relayed = mine(fsend.at[j * n + t], frecv.at[j * n + t], zones[t].at[slot(px, py, 1 - c)])
                relayed.wait_recv()
                relayed.wait_send()

    hbm = lambda a: pltpu.HBM(a.shape, a.dtype)
    out = pl.pallas_call(
        body, name=name,
        in_specs=[_HBM] * (2 * n) + [_SEM] * 4 + [_ANY],
        out_specs=[_HBM] * (2 * n),
        out_shape=[hbm(a) for a in st["xs"]] + [hbm(a) for a in st["lands"]],
        input_output_aliases={t: t for t in range(2 * n)},
        compiler_params=pltpu.CompilerParams(has_side_effects=_EFFECT),
    )(*st["xs"], *st["lands"], st["send"], st["d2d"], st["fsend"], st["frecv"], after)
    return out[:n], out[n:]


def _cast_layers(pairs, name):
    def body(*refs):
        n = len(refs) // 2
        for i in range(n):
            refs[n + i][...] = refs[i][...].astype(BF16)

    in_specs = [pl.BlockSpec((None,) + w.shape[1:], functools.partial(lambda l, i: (l, 0, 0), l),
                             pipeline_mode=pl.Buffered(1)) for w, l in pairs]
    return pl.pallas_call(
        body, name=name, grid=(1,), in_specs=in_specs,
        out_specs=[pl.BlockSpec(w.shape[1:], lambda i: (0, 0)) for w, _ in pairs],
        out_shape=[jax.ShapeDtypeStruct(w.shape[1:], BF16) for w, _ in pairs],
        compiler_params=_cp(("arbitrary",)),
    )(*[w for w, _ in pairs])


ADAMW_BLOCK_ELEMS = 256 * 1024


def _adam_step(me, w, m, v, parts_ref, mine, out_refs):
    g = None
    for p in range(N_DEV):
        term = jnp.where(me == p, mine.astype(F32), parts_ref[p].astype(F32))
        g = term if g is None else g + term
    mn = ADAM_B1 * m + (1.0 - ADAM_B1) * g
    vn = ADAM_B2 * v + (1.0 - ADAM_B2) * (g * g)
    m_hat = mn / (1.0 - ADAM_B1 ** ADAM_STEP)
    v_hat = vn / (1.0 - ADAM_B2 ** ADAM_STEP)
    g_ref, d_ref, mo_ref, vo_ref = out_refs
    g_ref[...] = g
    d_ref[...] = -ADAM_LR * (m_hat / (jnp.sqrt(v_hat) + ADAM_EPS) + ADAM_WD * w)
    mo_ref[...] = mn
    vo_ref[...] = vn


def _adamw(me, w, m, v, parts, own, name, layer=0, into=None):
    L, R, C = w.shape
    P = parts.shape[0]
    tr = R
    t = 16
    while t <= R:
        if R % t == 0 and t * C <= ADAMW_BLOCK_ELEMS:
            tr = t
        t += 16
    if tr == R and R * C > ADAMW_BLOCK_ELEMS and R % 16 == 0:
        tr = 16
    own_all = own.shape[0] == P

    def body(me_ref, w_ref, m_ref, v_ref, p_ref, own_ref, *rest):
        _adam_step(me_ref[0], w_ref[...], m_ref[...], v_ref[...], p_ref, own_ref[...], rest[-4:])

    blk = pl.BlockSpec((None, tr, C), lambda i, me_ref: (layer, i, 0))
    own_blk = pl.BlockSpec((None, tr, C), lambda i, me_ref: (me_ref[0] if own_all else 0, i, 0))
    prev = list(into) if into is not None else []
    return pl.pallas_call(
        body, name=name,
        grid_spec=pltpu.PrefetchScalarGridSpec(
            num_scalar_prefetch=1, grid=(R // tr,),
            in_specs=[blk, blk, blk, pl.BlockSpec((P, tr, C), lambda i, me_ref: (0, i, 0)), own_blk]
            + [_ANY] * len(prev),
            out_specs=[blk] * 4),
        out_shape=[jax.ShapeDtypeStruct((L, R, C), F32)] * 4,
        input_output_aliases={6 + i: i for i in range(len(prev))},
        compiler_params=_cp(("parallel",)),
    )(me, w, m, v, parts, own, *prev)


def _perm_cols(w):
    pad = jnp.zeros(w.shape[:-1] + (LANE - SSD_HEADS,), w.dtype)
    return jnp.concatenate([w[..., 0:1536], w[..., 2438:2694], w[..., 1536:2432], w[..., 2432:2438], pad,
                            w[..., 2694:2950]], axis=-1)


def _unperm_cols(w):
    return jnp.concatenate([w[..., 0:1536], w[..., XBC0:XBC0 + SSD_CONV_DIM], w[..., DT0:DT0 + SSD_HEADS],
                            w[..., U0:U0 + SGU_W], w[..., VS0:VS0 + SGU_W]], axis=-1)


_SMALL = ("ffn1_norm", "mix_norm", "conv_w", "conv_b", "dt_bias", "a_log", "d_skip", "ssd_norm",
          "sgu_ln_g", "sgu_ln_b", "sgu_w", "sgu_b", "ffn2_norm", "final_norm", "loss")


_SMALL_LAST = ("ffn1_norm",)
_SMALL_EARLY = tuple(k for k in _SMALL if k not in _SMALL_LAST)


def _pack(d, names):
    v = jnp.concatenate([d[k].astype(F32).reshape(-1) for k in names])
    n = v.shape[0]
    npad = -(-n // (LANE * 16)) * (LANE * 16)
    return jnp.pad(v, (0, npad - n)).reshape(npad // LANE, LANE)


def _unpack(p, shapes, names):
    v = p.reshape(-1)
    out, o = {}, 0
    for k in names:
        n = int(np.prod(shapes[k]))
        out[k] = v[o:o + n].reshape(shapes[k])
        o += n
    return out


def kernel(x, ffn1_norm, ffn1_w_gate, ffn1_w_up, ffn1_w_down, mix_norm, w_in, conv_w, conv_b, dt_bias, a_log, d_skip, ssd_norm, sgu_ln_g, sgu_ln_b, sgu_w, sgu_b, w_out, ffn2_norm, ffn2_w_gate, ffn2_w_up, ffn2_w_down, final_norm, loss_target, m_ffn1_norm, m_ffn1_w_gate, m_ffn1_w_up, m_ffn1_w_down, m_mix_norm, m_w_in, m_conv_w, m_conv_b, m_dt_bias, m_a_log, m_d_skip, m_ssd_norm, m_sgu_ln_g, m_sgu_ln_b, m_sgu_w, m_sgu_b, m_w_out, m_ffn2_norm, m_ffn2_w_gate, m_ffn2_w_up, m_ffn2_w_down, m_final_norm, v_ffn1_norm, v_ffn1_w_gate, v_ffn1_w_up, v_ffn1_w_down, v_mix_norm, v_w_in, v_conv_w, v_conv_b, v_dt_bias, v_a_log, v_d_skip, v_ssd_norm, v_sgu_ln_g, v_sgu_ln_b, v_sgu_w, v_sgu_b, v_w_out, v_ffn2_norm, v_ffn2_w_gate, v_ffn2_w_up, v_ffn2_w_down, v_final_norm):
    B, S, D = x.shape
    T = B * S
    L = ffn1_norm.shape[0]
    me = 4 * lax.axis_index("x") + 2 * lax.axis_index("y") + lax.axis_index("c")
    cs = conv_w.shape[2]
    W = dict(ffn1_norm=ffn1_norm, ffn1_w_gate=ffn1_w_gate, ffn1_w_up=ffn1_w_up, ffn1_w_down=ffn1_w_down,
             mix_norm=mix_norm, w_in=w_in, conv_w=conv_w, conv_b=conv_b, dt_bias=dt_bias, a_log=a_log,
             d_skip=d_skip, ssd_norm=ssd_norm, sgu_ln_g=sgu_ln_g, sgu_ln_b=sgu_ln_b, sgu_w=sgu_w, sgu_b=sgu_b,
             w_out=w_out, ffn2_norm=ffn2_norm, ffn2_w_gate=ffn2_w_gate, ffn2_w_up=ffn2_w_up,
             ffn2_w_down=ffn2_w_down, final_norm=final_norm)
    M = dict(ffn1_norm=m_ffn1_norm, ffn1_w_gate=m_ffn1_w_gate, ffn1_w_up=m_ffn1_w_up, ffn1_w_down=m_ffn1_w_down,
             mix_norm=m_mix_norm, w_in=m_w_in, conv_w=m_conv_w, conv_b=m_conv_b, dt_bias=m_dt_bias, a_log=m_a_log,
             d_skip=m_d_skip, ssd_norm=m_ssd_norm, sgu_ln_g=m_sgu_ln_g, sgu_ln_b=m_sgu_ln_b, sgu_w=m_sgu_w,
             sgu_b=m_sgu_b, w_out=m_w_out, ffn2_norm=m_ffn2_norm, ffn2_w_gate=m_ffn2_w_gate,
             ffn2_w_up=m_ffn2_w_up, ffn2_w_down=m_ffn2_w_down, final_norm=m_final_norm)
    V = dict(ffn1_norm=v_ffn1_norm, ffn1_w_gate=v_ffn1_w_gate, ffn1_w_up=v_ffn1_w_up, ffn1_w_down=v_ffn1_w_down,
             mix_norm=v_mix_norm, w_in=v_w_in, conv_w=v_conv_w, conv_b=v_conv_b, dt_bias=v_dt_bias, a_log=v_a_log,
             d_skip=v_d_skip, ssd_norm=v_ssd_norm, sgu_ln_g=v_sgu_ln_g, sgu_ln_b=v_sgu_ln_b, sgu_w=v_sgu_w,
             sgu_b=v_sgu_b, w_out=v_w_out, ffn2_norm=v_ffn2_norm, ffn2_w_gate=v_ffn2_w_gate,
             ffn2_w_up=v_ffn2_w_up, ffn2_w_down=v_ffn2_w_down, final_norm=v_final_norm)
    FFN1 = ("ffn1_w_gate", "ffn1_w_up", "ffn1_w_down")
    FFN2 = ("ffn2_w_gate", "ffn2_w_up", "ffn2_w_down")
    MIX = ("w_in", "w_out")
    big = FFN1 + MIX + FFN2
    col_sharded = lambda k: k.endswith("w_gate") or k.endswith("w_up")
    for dct in (W, M, V):
        for k in big:
            if col_sharded(k):
                dct[k] = jnp.swapaxes(dct[k], 1, 2)
        dct["w_in"] = _perm_cols(dct["w_in"])

    wgroups = [[(k, 0) for k in FFN1], [("w_in", 0), ("conv_w", None)], [("w_out", 0)] + [(k, 0) for k in FFN2]]
    for l in range(1, L):
        wgroups += [[(k, l) for k in FFN1] + [("w_in", l)], [("w_out", l)] + [(k, l) for k in FFN2]]
    wstarted, order = [], x
    later = [kl for grp in wgroups[1:] for kl in grp if kl[0] != "conv_w"]
    cast = dict(zip(wgroups[0], _cast_layers([(W[k], l) for k, l in wgroups[0]], "cast_first")))
    for gi, grp in enumerate(wgroups):
        if gi == 1:
            first = lax.optimization_barrier((W[later[0][0]], order))[0]
            srcs = [(first if i == 0 else W[k], l) for i, (k, l) in enumerate(later)]
            cast.update(zip(later, _cast_layers(srcs, "cast_rest")))
        xs = [conv_w if k == "conv_w" else cast[(k, l)] for k, l in grp]
        st = _gather_start(xs, order, f"gather_start_{gi}")
        order = st["token"]
        wstarted.append(st)
    G = {}

    zero1 = jnp.zeros((1,), F32)
    W["loss"], M["loss"], V["loss"] = zero1, zero1, zero1
    full_shapes = {k: (W[k].shape if k != "conv_w" else (L, SSD_CONV, SSD_CONV_DIM)) for k in _SMALL}
    embed = lambda a, k: a if k != "conv_w" else lax.dynamic_update_slice(
        jnp.zeros(full_shapes[k], F32), a, (0, 0, me * cs))
    small_packs = {names: [_pack({k: embed(d[k], k) for k in names}, names)[None] for d in (W, M, V)]
                   for names in (_SMALL_EARLY, _SMALL_LAST)}

    def relay(gi, after):
        wstarted[gi] = _gather_relay(wstarted[gi], after, f"gather_relay_{gi}")
        return wstarted[gi]["token"]

    def gathered(gi, after):
        _, lands = _gather_wait(wstarted[gi], after, f"gather_wait_{gi}")
        G.update(zip(wgroups[gi], lands))

    def rows(k, l):
        a = G[(k, l)]
        return a.reshape(-1, a.shape[-1])

    bias = _attn_bias(S, min(256, S))
    row1 = lambda a: a.reshape(1, -1)

    def ffn1_params(l):
        return dict(g1=row1(ffn1_norm[l]), wg1=rows("ffn1_w_gate", l), wu1=rows("ffn1_w_up", l),
                    wd1=rows("ffn1_w_down", l))

    def out_params(l):
        return dict(wout=rows("w_out", l), g2=row1(ffn2_norm[l]), wg2=rows("ffn2_w_gate", l),
                    wu2=rows("ffn2_w_up", l), wd2=rows("ffn2_w_down", l))

    def mix_params(l):
        cw = jnp.transpose(G[("conv_w", None)][:, l], (1, 0, 2)).reshape(SSD_CONV, -1)
        return dict(
            gm=row1(mix_norm[l]), win=rows("w_in", l),
            cw=jnp.pad(cw, ((0, SUBLANE - SSD_CONV), (0, 0))), cb=row1(conv_b[l]),
            par=jnp.pad(jnp.stack([jnp.repeat(dt_bias[l], HEAD_DIM), jnp.repeat(a_log[l], HEAD_DIM),
                                   jnp.repeat(d_skip[l], HEAD_DIM), ssd_norm[l]]), ((0, SUBLANE - 4), (0, 0))),
            ln=jnp.pad(jnp.stack([sgu_ln_g[l], sgu_ln_b[l]]), ((0, SUBLANE - 2), (0, 0))),
            sw=sgu_w[l], bst=jnp.pad(sgu_b[l].T, ((0, 0), (0, SUBLANE - SGU_GROUPS))))

    xc = x.reshape(T, D)
    saved, lay = [], []
    for l in range(L):
        if l == 0:
            gathered(0, relay(0, order))
        else:
            gathered(1 + 2 * l, xc)
        p = ffn1_params(l)
        x1, *ffn1_saved = _ffn_fwd(xc, p["g1"], p["wg1"], p["wu1"], p["wd1"], f"ffn1_fwd_{l}")
        if l == 0:
            gathered(1, relay(1, x1))
        p.update(mix_params(l))
        lay.append(p)
        proj, ht = _norm_mm(x1, p["gm"], p["win"], f"in_proj_{l}")
        o_att, lse = _attn_fwd(proj, bias, B, S, f"attn_fwd_{l}")
        tok = relay(2 + 2 * l, o_att)
        pre = _conv_fwd(proj, p["cw"], p["cb"], B, S, f"conv_fwd_{l}", after=tok)
        y_ssd, sall = _ssd_fwd(pre, proj, p["par"], B, S, f"ssd_fwd_{l}")
        y_sgu = _sgu_fwd(proj, p["ln"], p["sw"], p["bst"], B, S, f"sgu_fwd_{l}", after=tok)
        ycat = jnp.concatenate([o_att.astype(BF16), y_ssd, y_sgu], axis=1)
        gathered(2 + 2 * l, ycat)
        p.update(out_params(l))
        x2 = _mm(ycat, p["wout"], "nn", f"out_proj_{l}", residual=x1)
        tok = relay(3 + 2 * l, x2) if l + 1 < L else None
        x3, *ffn2_saved = _ffn_fwd(x2, p["g2"], p["wg2"], p["wu2"], p["wd2"], f"ffn2_fwd_{l}", after=tok)
        saved.append(dict(x0=xc, ffn1=ffn1_saved, x1=x1, ht=ht, proj=proj, o_att=o_att, lse=lse, pre=pre,
                          sall=sall, ycat=ycat, x2=x2, ffn2=ffn2_saved))
        xc = x3
    loss_part, dx, dgf = _final_loss(xc, row1(final_norm), loss_target.reshape(T, D), "final_loss")

    gl = [dict() for _ in range(L)]
    gstarted, gorder = [], [order]

    def to_blocks(k, a):
        return a.reshape(N_DEV, -1, a.shape[-1]).astype(BF16)

    def send_grads(keys, l, extra, tag, small_names=None):
        xs = [to_blocks(k, gl[l][k]) for k in keys] + extra
        flags = [True] * len(keys) + [False] * len(extra)
        st = _xchg_start(xs, flags, gorder[0], f"grads_start_{tag}")
        gorder[0] = st[-1]
        gstarted.append((keys, l, st, flags, tag, small_names))

    def small_grads(names):
        sm = {}
        for k in names:
            if k == "final_norm":
                sm[k] = dgf.reshape(-1)
            elif k == "loss":
                sm[k] = loss_part[0, :1]
            else:
                sm[k] = jnp.stack([gl[l][k] for l in range(L)])
        return [_pack(sm, names)]

    def behind(a):
        return lax.optimization_barrier((a, gorder[0]))[0]

    for l in reversed(range(L)):
        p, s, g = lay[l], saved[l], gl[l]
        gfac, ufac, actt = s["ffn2"]
        dx2, dgt, dut, xn, dacc, g["ffn2_norm"] = _ffn_bwd_dx(
            dx, s["x2"], p["g2"], gfac, ufac, p["wg2"], p["wu2"], p["wd2"], f"ffn2_bwd_{l}")
        g["ffn2_w_gate"], g["ffn2_w_up"], g["ffn2_w_down"] = _ffn_dw(dgt, dut, actt, xn, dacc, f"ffn2_dw_{l}")
        if l == 0:
            send_grads(FFN2, 0, [], "l0f")
            dx2 = behind(dx2)
        dycat = _mm(dx2, p["wout"], "nt", f"out_proj_dx_{l}")
        g["w_out"] = _mm(s["ycat"], dx2, "tn", f"out_proj_dw_{l}", out_dtype=BF16, tm_cap=1024, tk_cap=512)
        dq, dk, dv = _attn_bwd(s["proj"], s["o_att"], s["lse"], dycat, bias, B, S, f"attn_bwd_{l}")
        dpre, dz, ddt, dpar = _ssd_bwd(s["pre"], s["proj"], s["sall"], dycat, p["par"], B, S, f"ssd_bwd_{l}")
        dxbc, dwb = _conv_bwd(dpre, s["proj"], p["cw"], B, S, f"conv_bwd_{l}")
        du, dvs, dln, dsw, dbst = _sgu_bwd(s["proj"], dycat, p["ln"], p["sw"], p["bst"], B, S, f"sgu_bwd_{l}")
        hsum = lambda r: r.reshape(SSD_HEADS, HEAD_DIM).sum(-1)
        g["conv_w"], g["conv_b"] = dwb[:SSD_CONV], dwb[SSD_CONV]
        g["dt_bias"], g["a_log"], g["d_skip"], g["ssd_norm"] = hsum(dpar[0]), hsum(dpar[1]), hsum(dpar[2]), dpar[3]
        g["sgu_ln_g"], g["sgu_ln_b"], g["sgu_w"], g["sgu_b"] = dln[0], dln[1], dsw, dbst[:, :SGU_GROUPS].T
        dproj = jnp.concatenate([dq, dk, dv, dz, du, dxbc, ddt, dvs], axis=1)
        g["w_in"] = _mm_resident_lhs(s["ht"], dproj, f"in_proj_dw_{l}")
        dx1, g["mix_norm"] = _norm_mm_bwd(dproj, s["x1"], p["gm"], p["win"], dx2, f"in_proj_bwd_{l}")
        if l == 0:
            send_grads(MIX, 0, small_grads(_SMALL_EARLY), "l0a", _SMALL_EARLY)
            dx1, small_packs = lax.optimization_barrier((behind(dx1), small_packs))
        gfac, ufac, actt = s["ffn1"]
        dx, dgt, dut, xn, dacc, g["ffn1_norm"] = _ffn_bwd_dx(
            dx1, s["x0"], p["g1"], gfac, ufac, p["wg1"], p["wu1"], p["wd1"], f"ffn1_bwd_{l}")
        if l > 0:
            g["ffn1_w_gate"], g["ffn1_w_up"], g["ffn1_w_down"] = _ffn_dw(dgt, dut, actt, xn, dacc,
                                                                        f"ffn1_dw_{l}")
            send_grads(big, l, [], f"l{l}")
            dx = behind(dx)
        else:
            g["ffn1_w_gate"] = _dw_one(dgt, xn, "ffn1_dwg_0")
            send_grads(("ffn1_w_gate",), 0, [], "l0b1")
            g["ffn1_w_up"] = _dw_one(dut, xn, "ffn1_dwu_0", after=gorder[0])
            send_grads(("ffn1_w_up",), 0, [], "l0b2")
            g["ffn1_w_down"] = _dw_one(actt, dacc, "ffn1_dwd_0", after=gorder[0])
    grad_x = dx.reshape(B, S, D)
    send_grads(("ffn1_w_down",), 0, small_grads(_SMALL_LAST), "l0b", _SMALL_LAST)

    res, after = {}, gorder[0]
    small_out = [dict() for _ in range(4)]
    me1 = me.reshape(1).astype(jnp.int32)
    for keys, l, st, flags, tag, names in gstarted:
        own, lands = _xchg_wait(st, flags, after, f"grads_wait_{tag}")
        for k, mine, pk in zip(keys, own, lands):
            res[k] = _adamw(me1, W[k], M[k], V[k], pk, mine, f"adamw_{k}_{l}", layer=l, into=res.get(k))
        done = [res[k][0] for k in keys]
        if names:
            outs = _adamw(me1, *small_packs[names], lands[-1], own[-1][None], f"adamw_small_{tag}")
            for d, o in zip(small_out, outs):
                u = _unpack(o, full_shapes, names)
                if "conv_w" in u:
                    u["conv_w"] = lax.dynamic_slice(u["conv_w"], (0, 0, me * cs), (L, SSD_CONV, cs))
                d.update(u)
                done.extend(u.values())
        after = lax.optimization_barrier(tuple(done))[0]
    back = lambda k, a: jnp.swapaxes(a, 1, 2) if col_sharded(k) else _unperm_cols(a) if k == "w_in" else a
    grads, deltas, new_m, new_v = [dict({k: back(k, res[k][i]) for k in big}, **small_out[i]) for i in range(4)]

    names = ("ffn1_norm", "ffn1_w_gate", "ffn1_w_up", "ffn1_w_down", "mix_norm", "w_in", "conv_w", "conv_b",
             "dt_bias", "a_log", "d_skip", "ssd_norm", "sgu_ln_g", "sgu_ln_b", "sgu_w", "sgu_b", "w_out",
             "ffn2_norm", "ffn2_w_gate", "ffn2_w_up", "ffn2_w_down", "final_norm")
    loss = grads["loss"][0]
    return (loss, grad_x, *[grads[n] for n in names], *[deltas[n] for n in names],
            *[new_m[n] for n in names], *[new_v[n] for n in names])
```

```python
import functools

import numpy as np
import jax
import jax.numpy as jnp
from jax import lax
from jax.experimental import pallas as pl
from jax.experimental.pallas import tpu as pltpu

F32, BF16 = jnp.float32, jnp.bfloat16
HI = lax.Precision.HIGH
MESH = pl.DeviceIdType.MESH
N_DEV = 8
VMEM_LIMIT_BYTES = 56 * 1024 * 1024
LANE, SUBLANE = 128, 8

HEAD_DIM = 64
ATT_W = 384
SSD_W = 384
SSD_HEADS = 6
SSD_STATE = 128
SSD_CONV = 4
CHUNK = 128
SSD_CONV_DIM = 896
SGU_W = 256
SGU_GROUPS = 4
D_IN = 2950
RMS_EPS = 1e-6
LN_EPS = 1e-5
NEG = -1e30

PW = 3072
Q0, K0, V0, Z0, U0, XBC0, DT0, VS0 = 0, 384, 768, 1152, 1536, 1792, 2688, 2816

ADAM_LR, ADAM_B1, ADAM_B2, ADAM_EPS, ADAM_WD, ADAM_STEP = 0.001, 0.9, 0.999, 1e-08, 0.01, 10


def _cp(sem=None):
    return pltpu.CompilerParams(dimension_semantics=sem, vmem_limit_bytes=VMEM_LIMIT_BYTES)


def _tile(n, cap, mult=LANE):
    best = None
    t = mult
    while t <= min(n, cap):
        if n % t == 0:
            best = t
        t += mult
    return best if best is not None else n


def _dot(a, b, prec=None):
    return jnp.dot(a, b, preferred_element_type=F32, precision=prec)


def _dot_nt(a, b, prec=None):
    return lax.dot_general(a, b, (((1,), (1,)), ((), ())), preferred_element_type=F32, precision=prec)


def _dot_tn(a, b, prec=None):
    return lax.dot_general(a, b, (((0,), (0,)), ((), ())), preferred_element_type=F32, precision=prec)


def _sigmoid(x):
    return 1.0 / (1.0 + jnp.exp(-x))


def _silu(x):
    return x * _sigmoid(x)


def _gelu(x):
    return 0.5 * x * (1.0 + lax.erf(x * 0.7071067811865476))


def _softplus(x):
    return jnp.maximum(x, 0.0) + jnp.log(1.0 + jnp.exp(-jnp.abs(x)))


def _rms_fwd(x, g):
    rstd = lax.rsqrt(jnp.mean(x * x, axis=-1, keepdims=True) + RMS_EPS)
    xhat = x * rstd
    return xhat * g, xhat, rstd


def _rms_bwd(dy, xhat, rstd, g):
    dxhat = dy * g
    dx = rstd * (dxhat - xhat * jnp.mean(dxhat * xhat, axis=-1, keepdims=True))
    return dx, dy * xhat


def _resident(shape):
    return pl.BlockSpec(shape, lambda *_: (0,) * len(shape), pipeline_mode=pl.Buffered(1))


def _mm(a, b, mode, name, out_dtype=F32, residual=None, tm_cap=1024, tn_cap=1024, tk_cap=1024):
    if mode == "nn":
        (M, K), (_, N) = a.shape, b.shape
    elif mode == "nt":
        (M, K), (N, _) = a.shape, b.shape
    else:
        (K, M), (_, N) = a.shape, b.shape
    tm, tn, tk = _tile(M, tm_cap), _tile(N, tn_cap), _tile(K, tk_cap)
    nk = K // tk
    if mode == "tn":
        a_spec = pl.BlockSpec((tk, tm), lambda i, j, k: (k, i))
    else:
        a_spec = pl.BlockSpec((tm, tk), lambda i, j, k: (i, k))
    if mode == "nt":
        b_spec = pl.BlockSpec((tn, tk), lambda i, j, k: (j, k))
    else:
        b_spec = pl.BlockSpec((tk, tn), lambda i, j, k: (k, j))
    o_spec = pl.BlockSpec((tm, tn), lambda i, j, k: (i, j))
    has_res = residual is not None

    def prod(a_ref, b_ref):
        av = a_ref[...].astype(BF16)
        bv = b_ref[...].astype(BF16)
        if mode == "nn":
            return _dot(av, bv)
        if mode == "nt":
            return _dot_nt(av, bv)
        return _dot_tn(av, bv)

    def body(*refs):
        a_ref, b_ref = refs[:2]
        r_ref = refs[2] if has_res else None
        o_ref = refs[2 + has_res]
        if nk == 1:
            o = prod(a_ref, b_ref)
            if has_res:
                o = r_ref[...] + o
            o_ref[...] = o.astype(out_dtype)
            return
        acc = refs[3 + has_res]
        k = pl.program_id(2)

        @pl.when(k == 0)
        def _():
            acc[...] = jnp.zeros_like(acc)

        acc[...] += prod(a_ref, b_ref)

        @pl.when(k == nk - 1)
        def _():
            o = acc[...]
            if has_res:
                o = r_ref[...] + o
            o_ref[...] = o.astype(out_dtype)

    ins = [a, b] + ([residual] if has_res else [])
    in_specs = [a_spec, b_spec] + ([o_spec] if has_res else [])
    return pl.pallas_call(
        body, name=name, grid=(M // tm, N // tn, nk),
        in_specs=in_specs, out_specs=o_spec,
        out_shape=jax.ShapeDtypeStruct((M, N), out_dtype),
        scratch_shapes=[pltpu.VMEM((tm, tn), F32)] if nk > 1 else [],
        compiler_params=_cp(("parallel", "parallel", "arbitrary")),
    )(*ins)


def _after(after):
    return ([after], [_ANY]) if after is not None else ([], [])


def _ffn_fwd(x, g, wgt, wut, wd, name, after=None, proj=None):
    T, D = x.shape
    F = wgt.shape[0]
    tm = _tile(T, 256)

    def body(x_ref, g_ref, wg_ref, wu_ref, wd_ref, *rest):
        out_ref, dgf_ref, sl_ref, actt_ref = rest[-4:]
        xv = x_ref[...]
        if proj is not None:
            xv = xv + _dot(rest[0][...], rest[1][...])
            rest[-5][...] = xv
        xn = _rms_fwd(xv, g_ref[...])[0].astype(BF16)
        gate = _dot_nt(xn, wg_ref[...])
        up = _dot_nt(xn, wu_ref[...])
        sig = _sigmoid(gate)
        sl = gate * sig
        dgf_ref[...] = (up * (sig + sl * (1.0 - sig))).astype(BF16)
        sl_ref[...] = sl.astype(BF16)
        act = (sl * up).astype(BF16)
        actt_ref[...] = act.T
        out_ref[...] = xv + 0.5 * _dot(act, wd_ref[...])

    row = lambda w: pl.BlockSpec((tm, w), lambda i: (i, 0))
    extra, extra_specs = _after(after)
    pre, pre_specs, pre_out, pre_shape = [], [], [], []
    if proj is not None:
        pre, pre_specs = list(proj), [row(proj[0].shape[1]), _resident(proj[1].shape)]
        pre_out, pre_shape = [row(D)], [jax.ShapeDtypeStruct((T, D), F32)]
    return pl.pallas_call(
        body, name=name, grid=(T // tm,),
        in_specs=[row(D), _resident((1, D)), _resident((F, D)), _resident((F, D)), _resident((F, D))]
        + pre_specs + extra_specs,
        out_specs=pre_out + [row(D), row(F), row(F), pl.BlockSpec((F, tm), lambda i: (0, i))],
        out_shape=pre_shape + [jax.ShapeDtypeStruct((T, D), F32),
                               jax.ShapeDtypeStruct((T, F), BF16),
                               jax.ShapeDtypeStruct((T, F), BF16),
                               jax.ShapeDtypeStruct((F, T), BF16)],
        compiler_params=_cp(("parallel",)),
    )(x, g, wgt, wut, wd, *pre, *extra)


def _ffn_bwd_dx(dout, x, g, dgf, sl, wg, wu, wd, name):
    T, D = x.shape
    F = wg.shape[0]
    tm = _tile(T, 256)

    def body(dout_ref, x_ref, g_ref, dgf_ref, sl_ref, wg_ref, wu_ref, wd_ref,
             dx_ref, dgt_ref, dut_ref, xn_ref, dacc_ref, dg_ref):
        @pl.when(pl.program_id(0) == 0)
        def _():
            dg_ref[...] = jnp.zeros_like(dg_ref)

        gv = g_ref[...]
        dout_v = dout_ref[...]
        xn, xhat, rstd = _rms_fwd(x_ref[...], gv)
        xn_ref[...] = xn.astype(BF16)
        dacc = (0.5 * dout_v).astype(BF16)
        dacc_ref[...] = dacc
        dact = _dot_nt(dacc, wd_ref[...])
        dgate = (dact * dgf_ref[...].astype(F32)).astype(BF16)
        dup = (dact * sl_ref[...].astype(F32)).astype(BF16)
        dgt_ref[...] = dgate.T
        dut_ref[...] = dup.T
        dxn = _dot(dgate, wg_ref[...]) + _dot(dup, wu_ref[...])
        dx, dgrow = _rms_bwd(dxn, xhat, rstd, gv)
        dx_ref[...] = dout_v + dx
        dg_ref[...] += jnp.sum(dgrow, axis=0, keepdims=True)

    row = lambda w: pl.BlockSpec((tm, w), lambda i: (i, 0))
    tr = pl.BlockSpec((F, tm), lambda i: (0, i))
    return pl.pallas_call(
        body, name=name, grid=(T // tm,),
        in_specs=[row(D), row(D), _resident((1, D)), row(F), row(F),
                  _resident((F, D)), _resident((F, D)), _resident((F, D))],
        out_specs=[row(D), tr, tr, row(D), row(D), pl.BlockSpec((1, D), lambda i: (0, 0))],
        out_shape=[jax.ShapeDtypeStruct((T, D), F32)] + [jax.ShapeDtypeStruct((F, T), BF16)] * 2
        + [jax.ShapeDtypeStruct((T, D), BF16)] * 2 + [jax.ShapeDtypeStruct((1, D), F32)],
        compiler_params=_cp(("arbitrary",)),
    )(dout, x, g, dgf, sl, wg, wu, wd)


def _ffn_dw(dgt, dut, actt, xn, dacc, name):
    F, T = dgt.shape
    D = xn.shape[1]
    th = _tile(F, 256)

    def body(dg_ref, du_ref, a_ref, xn_ref, dacc_ref, dwg_ref, dwu_ref, dwd_ref):
        xv = xn_ref[...]
        dwg_ref[...] = _dot(dg_ref[...], xv).astype(BF16)
        dwu_ref[...] = _dot(du_ref[...], xv).astype(BF16)
        dwd_ref[...] = _dot(a_ref[...], dacc_ref[...]).astype(BF16)

    tile = pl.BlockSpec((th, T), lambda j: (j, 0))
    out = pl.BlockSpec((th, D), lambda j: (j, 0))
    return pl.pallas_call(
        body, name=name, grid=(F // th,),
        in_specs=[tile, tile, tile, _resident((T, D)), _resident((T, D))],
        out_specs=[out, out, out], out_shape=[jax.ShapeDtypeStruct((F, D), BF16)] * 3,
        compiler_params=_cp(("parallel",)),
    )(dgt, dut, actt, xn, dacc)


def _dw_one(lt, r, name, after=None):
    F, T = lt.shape
    D = r.shape[1]
    th = _tile(F, 256)
    extra, extra_specs = _after(after)

    def body(l_ref, r_ref, *rest):
        rest[-1][...] = _dot(l_ref[...], r_ref[...]).astype(BF16)

    return pl.pallas_call(
        body, name=name, grid=(F // th,),
        in_specs=[pl.BlockSpec((th, T), lambda j: (j, 0)), _resident((T, D))] + extra_specs,
        out_specs=pl.BlockSpec((th, D), lambda j: (j, 0)),
        out_shape=jax.ShapeDtypeStruct((F, D), BF16),
        compiler_params=_cp(("parallel",)),
    )(lt, r, *extra)


def _norm_mm(x, g, w, name):
    T, D = x.shape
    N = w.shape[1]
    tm = _tile(T, 512)

    def body(x_ref, g_ref, w_ref, o_ref, ht_ref):
        xn = _rms_fwd(x_ref[...], g_ref[...])[0]
        ht_ref[...] = xn.T.astype(BF16)
        o_ref[...] = _dot(xn.astype(BF16), w_ref[...])

    return pl.pallas_call(
        body, name=name, grid=(T // tm,),
        in_specs=[pl.BlockSpec((tm, D), lambda i: (i, 0)), _resident((1, D)), _resident((D, N))],
        out_specs=[pl.BlockSpec((tm, N), lambda i: (i, 0)), pl.BlockSpec((D, tm), lambda i: (0, i))],
        out_shape=[jax.ShapeDtypeStruct((T, N), F32), jax.ShapeDtypeStruct((D, T), BF16)],
        compiler_params=_cp(("parallel",)),
    )(x, g, w)


def _norm_mm_bwd(dproj, x, g, w, dres, name):
    T, D = x.shape
    N = w.shape[1]
    tm = _tile(T, 512)

    def body(dp_ref, x_ref, g_ref, w_ref, dres_ref, dx_ref, dg_ref):
        @pl.when(pl.program_id(0) == 0)
        def _():
            dg_ref[...] = jnp.zeros_like(dg_ref)

        gv = g_ref[...]
        dh = _dot_nt(dp_ref[...], w_ref[...])
        _, xhat, rstd = _rms_fwd(x_ref[...], gv)
        dx, dgrow = _rms_bwd(dh, xhat, rstd, gv)
        dx_ref[...] = dres_ref[...] + dx
        dg_ref[...] += jnp.sum(dgrow, axis=0, keepdims=True)

    row = pl.BlockSpec((tm, D), lambda i: (i, 0))
    one = pl.BlockSpec((1, D), lambda i: (0, 0))
    return pl.pallas_call(
        body, name=name, grid=(T // tm,),
        in_specs=[pl.BlockSpec((tm, N), lambda i: (i, 0)), row, _resident((1, D)), _resident((D, N)), row],
        out_specs=[row, one],
        out_shape=[jax.ShapeDtypeStruct((T, D), F32), jax.ShapeDtypeStruct((1, D), F32)],
        compiler_params=_cp(("arbitrary",)),
    )(dproj, x, g, w, dres)


def _mm_resident_lhs(at, b, name, tn_cap=512):
    M, K = at.shape
    N = b.shape[1]
    tn = _tile(N, tn_cap)

    def body(a_ref, b_ref, o_ref):
        o_ref[...] = _dot(a_ref[...], b_ref[...]).astype(BF16)

    return pl.pallas_call(
        body, name=name, grid=(N // tn,),
        in_specs=[_resident((M, K)), pl.BlockSpec((K, tn), lambda j: (0, j))],
        out_specs=pl.BlockSpec((M, tn), lambda j: (0, j)),
        out_shape=jax.ShapeDtypeStruct((M, N), BF16),
        compiler_params=_cp(("parallel",)),
    )(at, b)


def _final_loss(x, g, target, name):
    T, D = x.shape
    tm = _tile(T, 512)

    def body(x_ref, g_ref, t_ref, loss_ref, dx_ref, dg_ref):
        @pl.when(pl.program_id(0) == 0)
        def _():
            dg_ref[...] = jnp.zeros_like(dg_ref)
            loss_ref[...] = jnp.zeros_like(loss_ref)

        gv = g_ref[...]
        y, xhat, rstd = _rms_fwd(x_ref[...], gv)
        err = y - t_ref[...]
        part = 0.5 * jnp.sum(jnp.mean(err * err, axis=-1, keepdims=True), axis=0, keepdims=True)
        loss_ref[...] += jnp.broadcast_to(part, loss_ref.shape)
        dy = err * (1.0 / D)
        dx, dgrow = _rms_bwd(dy, xhat, rstd, gv)
        dx_ref[...] = dx
        dg_ref[...] += jnp.sum(dgrow, axis=0, keepdims=True)

    row = pl.BlockSpec((tm, D), lambda i: (i, 0))
    one = pl.BlockSpec((1, D), lambda i: (0, 0))
    return pl.pallas_call(
        body, name=name, grid=(T // tm,),
        in_specs=[row, one, row],
        out_specs=[pl.BlockSpec((1, LANE), lambda i: (0, 0)), row, one],
        out_shape=[jax.ShapeDtypeStruct((1, LANE), F32), jax.ShapeDtypeStruct((T, D), F32),
                   jax.ShapeDtypeStruct((1, D), F32)],
        compiler_params=_cp(("arbitrary",)),
    )(x, g, target)


def _attn_bias(S, bq):
    d = np.arange(bq)[:, None] - np.arange(S)[None, :] + (S // bq - 1) * bq
    ok = d >= 0
    mult = ((ok & (d <= 128)).astype(np.float32) + (ok & (d % 4 == 0) & (d <= 512))
            + (ok & (d % 16 == 0) & (d <= 2048)))
    return jnp.asarray(np.where(mult > 0, np.log(np.maximum(mult, 1.0)), NEG).astype(np.float32))


def _attn_fwd(proj, bias, B, S, name):
    T = B * S
    bq = bias.shape[0]
    nb = S // bq
    qcol, kcol, vcol = Q0 // LANE, K0 // LANE, V0 // LANE

    def body(q_ref, k_ref, v_ref, t_ref, o_ref, lse_ref, ks, vs):
        for hh in range(2):
            sl = slice(HEAD_DIM * hh, HEAD_DIM * (hh + 1))
            ks[hh] = k_ref[:, sl].astype(BF16)
            vs[hh] = v_ref[:, sl].astype(BF16)
        for hh in range(2):
            sl = slice(HEAD_DIM * hh, HEAD_DIM * (hh + 1))
            for qb in range(nb):
                w, off, rows = bq * (qb + 1), (nb - 1 - qb) * bq, slice(qb * bq, (qb + 1) * bq)
                q = (q_ref[rows, sl] * 0.125).astype(BF16)
                s = _dot_nt(q, ks[hh, 0:w, :]) + t_ref[:, off:off + w]
                m = jnp.max(s, axis=-1, keepdims=True)
                p = jnp.exp(s - m)
                l = jnp.sum(p, axis=-1, keepdims=True)
                o_ref[rows, sl] = _dot(p.astype(BF16), vs[hh, 0:w, :]) / l
                lse_ref[rows, hh:hh + 1] = m + jnp.log(l)

    blk = lambda c0: pl.BlockSpec((S, LANE), lambda b, p: (b, c0 + p))
    return pl.pallas_call(
        body, name=name, grid=(B, ATT_W // LANE),
        in_specs=[blk(qcol), blk(kcol), blk(vcol), _resident((bq, S))],
        out_specs=[pl.BlockSpec((S, LANE), lambda b, p: (b, p)),
                   pl.BlockSpec((None, None, S, 2), lambda b, p: (b, p, 0, 0))],
        out_shape=[jax.ShapeDtypeStruct((T, ATT_W), F32),
                   jax.ShapeDtypeStruct((B, ATT_W // LANE, S, 2), F32)],
        scratch_shapes=[pltpu.VMEM((2, S, HEAD_DIM), BF16)] * 2,
        compiler_params=_cp(("parallel", "parallel")),
    )(proj, proj, proj, bias)


def _attn_bwd(proj, o, lse, dy, bias, B, S, name):
    T = B * S
    bq = bias.shape[0]
    nb = S // bq
    qcol, kcol, vcol = Q0 // LANE, K0 // LANE, V0 // LANE

    def body(q_ref, k_ref, v_ref, o_ref, lse_ref, do_ref, t_ref, dq_ref, dk_ref, dv_ref, ks, vs, dks, dvs):
        for hh in range(2):
            sl = slice(HEAD_DIM * hh, HEAD_DIM * (hh + 1))
            ks[hh] = k_ref[:, sl].astype(BF16)
            vs[hh] = v_ref[:, sl].astype(BF16)
        dks[...] = jnp.zeros_like(dks)
        dvs[...] = jnp.zeros_like(dvs)
        for hh in range(2):
            sl = slice(HEAD_DIM * hh, HEAD_DIM * (hh + 1))
            for qb in range(nb):
                w, off, rows = bq * (qb + 1), (nb - 1 - qb) * bq, slice(qb * bq, (qb + 1) * bq)
                q = (q_ref[rows, sl] * 0.125).astype(BF16)
                do = do_ref[rows, sl]
                dob = do.astype(BF16)
                delta = jnp.sum(do * o_ref[rows, sl], axis=-1, keepdims=True)
                k, v = ks[hh, 0:w, :], vs[hh, 0:w, :]
                s = _dot_nt(q, k) + t_ref[:, off:off + w]
                p = jnp.exp(s - lse_ref[rows, hh:hh + 1])
                ds = (p * (_dot_nt(dob, v) - delta)).astype(BF16)
                dq_ref[rows, sl] = (_dot(ds, k) * 0.125).astype(dq_ref.dtype)
                dks[hh, 0:w, :] += _dot_tn(ds, q)
                dvs[hh, 0:w, :] += _dot_tn(p.astype(BF16), dob)
            dk_ref[:, sl] = dks[hh].astype(dk_ref.dtype)
            dv_ref[:, sl] = dvs[hh].astype(dv_ref.dtype)

    blk = lambda c0: pl.BlockSpec((S, LANE), lambda b, p: (b, c0 + p))
    own = pl.BlockSpec((S, LANE), lambda b, p: (b, p))
    return pl.pallas_call(
        body, name=name, grid=(B, ATT_W // LANE),
        in_specs=[blk(qcol), blk(kcol), blk(vcol), own,
                  pl.BlockSpec((None, None, S, 2), lambda b, p: (b, p, 0, 0)), own, _resident((bq, S))],
        out_specs=[own, own, own],
        out_shape=[jax.ShapeDtypeStruct((T, ATT_W), BF16)] * 3,
        scratch_shapes=[pltpu.VMEM((2, S, HEAD_DIM), BF16)] * 2 + [pltpu.VMEM((2, S, HEAD_DIM), F32)] * 2,
        compiler_params=_cp(("parallel", "parallel")),
    )(proj, proj, proj, o, lse, dy, bias)


def _conv_fwd(proj, cw, cb, B, S, name, after=None):
    T = B * S
    nc = SSD_CONV_DIM // LANE
    c0 = XBC0 // LANE
    extra, extra_specs = _after(after)

    def body(x_ref, w_ref, b_ref, *rest):
        o_ref = rest[-1]
        x = x_ref[...]
        t = lax.broadcasted_iota(jnp.int32, (S, 1), 0)
        acc = b_ref[...] + w_ref[SSD_CONV - 1:SSD_CONV, :] * x
        for k in range(SSD_CONV - 1):
            sh = SSD_CONV - 1 - k
            xs = jnp.where(t >= sh, pltpu.roll(x, sh, 0), 0.0)
            acc = acc + w_ref[k:k + 1, :] * xs
        o_ref[...] = acc

    return pl.pallas_call(
        body, name=name, grid=(B, nc),
        in_specs=[pl.BlockSpec((S, LANE), lambda b, j: (b, c0 + j)),
                  pl.BlockSpec((SUBLANE, LANE), lambda b, j: (0, j)),
                  pl.BlockSpec((1, LANE), lambda b, j: (0, j))] + extra_specs,
        out_specs=pl.BlockSpec((S, LANE), lambda b, j: (b, j)),
        out_shape=jax.ShapeDtypeStruct((T, SSD_CONV_DIM), F32),
        compiler_params=_cp(("parallel", "parallel")),
    )(proj, cw, cb, *extra)


def _conv_bwd(dpre, proj, cw, B, S, name):
    T = B * S
    nc = SSD_CONV_DIM // LANE
    c0 = XBC0 // LANE

    def body(d_ref, x_ref, w_ref, dx_ref, dwb_ref):
        @pl.when(pl.program_id(1) == 0)
        def _():
            dwb_ref[...] = jnp.zeros_like(dwb_ref)

        d = d_ref[...]
        x = x_ref[...]
        t = lax.broadcasted_iota(jnp.int32, (S, 1), 0)
        dx = w_ref[SSD_CONV - 1:SSD_CONV, :] * d
        rows = [None] * SUBLANE
        rows[SSD_CONV - 1] = jnp.sum(d * x, axis=0, keepdims=True)
        for k in range(SSD_CONV - 1):
            sh = SSD_CONV - 1 - k
            dx = dx + w_ref[k:k + 1, :] * jnp.where(t < S - sh, pltpu.roll(d, S - sh, 0), 0.0)
            xs = jnp.where(t >= sh, pltpu.roll(x, sh, 0), 0.0)
            rows[k] = jnp.sum(d * xs, axis=0, keepdims=True)
        rows[SSD_CONV] = jnp.sum(d, axis=0, keepdims=True)
        dx_ref[...] = dx.astype(BF16)
        r = lax.broadcasted_iota(jnp.int32, (SUBLANE, LANE), 0)
        upd = jnp.zeros((SUBLANE, LANE), F32)
        for k in range(SSD_CONV + 1):
            upd = upd + jnp.where(r == k, rows[k], 0.0)
        dwb_ref[...] += upd

    return pl.pallas_call(
        body, name=name, grid=(nc, B),
        in_specs=[pl.BlockSpec((S, LANE), lambda j, b: (b, j)),
                  pl.BlockSpec((S, LANE), lambda j, b: (b, c0 + j)),
                  pl.BlockSpec((SUBLANE, LANE), lambda j, b: (0, j))],
        out_specs=[pl.BlockSpec((S, LANE), lambda j, b: (b, j)),
                   pl.BlockSpec((SUBLANE, LANE), lambda j, b: (0, j))],
        out_shape=[jax.ShapeDtypeStruct((T, SSD_CONV_DIM), BF16),
                   jax.ShapeDtypeStruct((SUBLANE, SSD_CONV_DIM), F32)],
        compiler_params=_cp(("parallel", "arbitrary")),
    )(dpre, proj, cw)


def _ssd_consts():
    e = np.zeros((LANE, SSD_W), np.float32)
    p = np.zeros((SUBLANE, SSD_W), np.float32)
    for h in range(SSD_HEADS):
        e[h, HEAD_DIM * h:HEAD_DIM * (h + 1)] = 1.0
        p[h, HEAD_DIM * h] = 1.0
    return jnp.asarray(e), jnp.asarray(p)


def _ssd_chunk(pre, z, dtr, sprev, par, e_mat, psel):
    L = CHUNK
    xc = _silu(pre)
    xs, bm, cm = xc[:, :SSD_W], xc[:, SSD_W:SSD_W + 2 * SSD_STATE], xc[:, SSD_W + 2 * SSD_STATE:]
    dtb, alog, dskip, ng = par[0:1], par[1:2], par[2:3], par[3:4]
    dt = _softplus(_dot(dtr, e_mat, HI) + dtb)
    a = dt * (-jnp.exp(alog))
    X = xs * dt
    ri = lax.broadcasted_iota(jnp.int32, (L, L), 0)
    ci = lax.broadcasted_iota(jnp.int32, (L, L), 1)
    tril = ri >= ci
    acs = _dot(tril.astype(F32), a, HI)
    acs_t = _dot_nt(psel, acs, HI)
    ecs = jnp.exp(acs)
    alast = acs[L - 1:L, :]
    xd = (X * jnp.exp(alast - acs)).astype(BF16)
    xb = X.astype(BF16)
    col = lax.broadcasted_iota(jnp.int32, (1, SSD_W), 1)
    sb = sprev.astype(BF16)
    bgs = [bm[:, SSD_STATE * g:SSD_STATE * (g + 1)].astype(BF16) for g in range(2)]
    cgs = [cm[:, SSD_STATE * g:SSD_STATE * (g + 1)].astype(BF16) for g in range(2)]
    cbs = [_dot_nt(cgs[g], bgs[g]) for g in range(2)]
    first = lax.broadcasted_iota(jnp.int32, (1, LANE), 1) < HEAD_DIM
    y_tiles, s_tiles = [], []
    for t in range(SSD_W // LANE):
        cl = slice(LANE * t, LANE * (t + 1))
        xb_t, xd_t, sb_t = xb[:, cl], xd[:, cl], sb[:, cl]
        per_head = []
        for h in (2 * t, 2 * t + 1):
            seg = acs[:, HEAD_DIM * h:HEAD_DIM * h + 1] - acs_t[h:h + 1, :]
            dec = jnp.exp(jnp.where(tril, seg, NEG))
            per_head.append(_dot((cbs[h // 3] * dec).astype(BF16), xb_t))
        y_t = jnp.where(first, per_head[0], per_head[1])
        ga, gb = (2 * t) // 3, (2 * t + 1) // 3
        if ga == gb:
            y_off, s_add = _dot(cgs[ga], sb_t), _dot_tn(bgs[ga], xd_t)
        else:
            y_off = jnp.where(first, _dot(cgs[ga], sb_t), _dot(cgs[gb], sb_t))
            s_add = jnp.where(first, _dot_tn(bgs[ga], xd_t), _dot_tn(bgs[gb], xd_t))
        y_tiles.append(y_t + y_off * ecs[:, cl])
        s_tiles.append(s_add)
    y = dskip * xs + jnp.concatenate(y_tiles, axis=1)
    snew = sprev * jnp.exp(alast) + jnp.concatenate(s_tiles, axis=1)
    yg = y * _silu(z)
    sq = yg * yg
    g0 = col < SSD_W // 2
    ms0 = jnp.sum(jnp.where(g0, sq, 0.0), axis=-1, keepdims=True) * (2.0 / SSD_W)
    ms1 = jnp.sum(jnp.where(g0, 0.0, sq), axis=-1, keepdims=True) * (2.0 / SSD_W)
    r = jnp.where(g0, lax.rsqrt(ms0 + RMS_EPS), lax.rsqrt(ms1 + RMS_EPS))
    return yg * r * ng, snew


SSD_CHUNKS_PER_STEP = 4


def _ssd_chunks_per_step(S):
    k = SSD_CHUNKS_PER_STEP
    while (S // CHUNK) % k:
        k //= 2
    return k


def _ssd_fwd(pre, proj, par, B, S, name):
    T = B * S
    k = _ssd_chunks_per_step(S)
    nc, rows = S // (CHUNK * k), CHUNK * k
    e_mat, psel = _ssd_consts()

    def body(pre_ref, z_ref, dt_ref, par_ref, e_ref, p_ref, y_ref, sall_ref, st):
        @pl.when(pl.program_id(1) == 0)
        def _():
            st[...] = jnp.zeros_like(st)

        sprev = st[...]
        for i in range(k):
            r = slice(CHUNK * i, CHUNK * (i + 1))
            sall_ref[i] = sprev
            y, sprev = _ssd_chunk(pre_ref[r, :], z_ref[r, :], dt_ref[r, :], sprev, par_ref[...], e_ref[...],
                                  p_ref[...])
            y_ref[r, :] = y.astype(BF16)
        st[...] = sprev

    row = lambda b, c: b * nc + c
    full = lambda shp: pl.BlockSpec(shp, lambda b, c: (0, 0))
    return pl.pallas_call(
        body, name=name, grid=(B, nc),
        in_specs=[pl.BlockSpec((rows, SSD_CONV_DIM), lambda b, c: (row(b, c), 0)),
                  pl.BlockSpec((rows, SSD_W), lambda b, c: (row(b, c), Z0 // SSD_W)),
                  pl.BlockSpec((rows, LANE), lambda b, c: (row(b, c), DT0 // LANE)),
                  full((SUBLANE, SSD_W)), full((LANE, SSD_W)), full((SUBLANE, SSD_W))],
        out_specs=[pl.BlockSpec((rows, SSD_W), lambda b, c: (row(b, c), 0)),
                   pl.BlockSpec((k, SSD_STATE, SSD_W), lambda b, c: (row(b, c), 0, 0))],
        out_shape=[jax.ShapeDtypeStruct((T, SSD_W), BF16),
                   jax.ShapeDtypeStruct((B * nc * k, SSD_STATE, SSD_W), F32)],
        scratch_shapes=[pltpu.VMEM((SSD_STATE, SSD_W), F32)],
        compiler_params=_cp(("parallel", "arbitrary")),
    )(pre, proj, proj, par, e_mat, psel)


def _ssd_bwd(pre, proj, sall, dy, par, B, S, name):
    T = B * S
    k = _ssd_chunks_per_step(S)
    nc, rows = S // (CHUNK * k), CHUNK * k
    e_mat, psel = _ssd_consts()

    def body(pre_ref, z_ref, dt_ref, sall_ref, dy_ref, par_ref, e_ref, p_ref,
             dpre_ref, dz_ref, ddt_ref, dpar_ref, ds):
        b, c = pl.program_id(0), pl.program_id(1)

        @pl.when(c == 0)
        def _():
            ds[...] = jnp.zeros_like(ds)

        @pl.when((b == 0) & (c == 0))
        def _():
            dpar_ref[...] = jnp.zeros_like(dpar_ref)

        e_v, p_v = e_ref[...], p_ref[...]
        fn = lambda pre, z, dtr, sprev, par: _ssd_chunk(pre, z, dtr, sprev, par, e_v, p_v)
        dstate, dpar_sum = ds[...], None
        for i in reversed(range(k)):
            r = slice(CHUNK * i, CHUNK * (i + 1))
            _, vjp = jax.vjp(fn, pre_ref[r, :], z_ref[r, :], dt_ref[r, :], sall_ref[i], par_ref[...])
            dpre, dz, ddt, dstate, dpar = vjp((dy_ref[r, :], dstate))
            dpre_ref[r, :] = dpre
            dz_ref[r, :] = dz.astype(BF16)
            ddt_ref[r, :] = ddt.astype(BF16)
            dpar_sum = dpar if dpar_sum is None else dpar_sum + dpar
        dpar_ref[...] += dpar_sum
        ds[...] = dstate

    row = lambda b, c: b * nc + (nc - 1 - c)
    full = lambda shp: pl.BlockSpec(shp, lambda b, c: (0, 0))
    return pl.pallas_call(
        body, name=name, grid=(B, nc),
        in_specs=[pl.BlockSpec((rows, SSD_CONV_DIM), lambda b, c: (row(b, c), 0)),
                  pl.BlockSpec((rows, SSD_W), lambda b, c: (row(b, c), Z0 // SSD_W)),
                  pl.BlockSpec((rows, LANE), lambda b, c: (row(b, c), DT0 // LANE)),
                  pl.BlockSpec((k, SSD_STATE, SSD_W), lambda b, c: (row(b, c), 0, 0)),
                  pl.BlockSpec((rows, SSD_W), lambda b, c: (row(b, c), ATT_W // SSD_W)),
                  full((SUBLANE, SSD_W)), full((LANE, SSD_W)), full((SUBLANE, SSD_W))],
        out_specs=[pl.BlockSpec((rows, SSD_CONV_DIM), lambda b, c: (row(b, c), 0)),
                   pl.BlockSpec((rows, SSD_W), lambda b, c: (row(b, c), 0)),
                   pl.BlockSpec((rows, LANE), lambda b, c: (row(b, c), 0)),
                   full((SUBLANE, SSD_W))],
        out_shape=[jax.ShapeDtypeStruct((T, SSD_CONV_DIM), F32),
                   jax.ShapeDtypeStruct((T, SSD_W), BF16),
                   jax.ShapeDtypeStruct((T, LANE), BF16),
                   jax.ShapeDtypeStruct((SUBLANE, SSD_W), F32)],
        scratch_shapes=[pltpu.VMEM((SSD_STATE, SSD_W), F32)],
        compiler_params=_cp(("arbitrary", "arbitrary")),
    )(pre, proj, proj, sall, dy, par, e_mat, psel)


def _sgu_consts():
    e = np.zeros((SUBLANE, SGU_W), np.float32)
    for g in range(SGU_GROUPS):
        e[g, HEAD_DIM * g:HEAD_DIM * (g + 1)] = 1.0
    return jnp.asarray(e)


def _sgu_chunk(u_raw, v_raw, ln, w, bst, e4):
    L = CHUNK
    u = _gelu(u_raw)
    v = _gelu(v_raw)
    mu = jnp.mean(v, axis=-1, keepdims=True)
    vc = v - mu
    var = jnp.mean(vc * vc, axis=-1, keepdims=True)
    vn = vc * lax.rsqrt(var + LN_EPS) * ln[0:1] + ln[1:2]
    vb = vn.astype(BF16)
    ri = lax.broadcasted_iota(jnp.int32, (L, L), 0)
    ci = lax.broadcasted_iota(jnp.int32, (L, L), 1)
    tril = ri >= ci
    col = lax.broadcasted_iota(jnp.int32, (1, SGU_W), 1)
    mixed = _dot(bst, e4, HI)
    for g in range(SGU_GROUPS):
        wc = jnp.where(tril, w[g], 0.0).astype(BF16)
        gm = (col >= HEAD_DIM * g) & (col < HEAD_DIM * (g + 1))
        mixed = mixed + jnp.where(gm, _dot(wc, vb), 0.0)
    return u * mixed


def _sgu_fwd(proj, ln, w, bst, B, S, name, after=None):
    T = B * S
    e4 = _sgu_consts()
    extra, extra_specs = _after(after)
    k = max(d for d in (8, 4, 2, 1) if (T // CHUNK) % d == 0)
    rows = k * CHUNK

    def body(u_ref, v_ref, ln_ref, w_ref, b_ref, e_ref, *rest):
        y_ref = rest[-1]
        for i in range(k):
            r = slice(CHUNK * i, CHUNK * (i + 1))
            y_ref[r, :] = _sgu_chunk(u_ref[r, :], v_ref[r, :], ln_ref[...], w_ref[...], b_ref[...],
                                     e_ref[...]).astype(BF16)

    return pl.pallas_call(
        body, name=name, grid=(T // rows,),
        in_specs=[pl.BlockSpec((rows, SGU_W), lambda i: (i, U0 // SGU_W)),
                  pl.BlockSpec((rows, SGU_W), lambda i: (i, VS0 // SGU_W)),
                  pl.BlockSpec((SUBLANE, SGU_W), lambda i: (0, 0)),
                  pl.BlockSpec((SGU_GROUPS, CHUNK, CHUNK), lambda i: (0, 0, 0)),
                  pl.BlockSpec((CHUNK, SUBLANE), lambda i: (0, 0)),
                  pl.BlockSpec((SUBLANE, SGU_W), lambda i: (0, 0))] + extra_specs,
        out_specs=pl.BlockSpec((rows, SGU_W), lambda i: (i, 0)),
        out_shape=jax.ShapeDtypeStruct((T, SGU_W), BF16),
        compiler_params=_cp(("parallel",)),
    )(proj, proj, ln, w, bst, e4, *extra)


def _sgu_bwd(proj, dy, ln, w, bst, B, S, name):
    T = B * S
    e4 = _sgu_consts()
    ycol = (ATT_W + SSD_W) // SGU_W
    k = max(d for d in (4, 2, 1) if (T // CHUNK) % d == 0)
    rows = k * CHUNK

    def body(u_ref, v_ref, dy_ref, ln_ref, w_ref, b_ref, e_ref, du_ref, dv_ref, dln_ref, dw_ref, db_ref):
        @pl.when(pl.program_id(0) == 0)
        def _():
            dln_ref[...] = jnp.zeros_like(dln_ref)
            dw_ref[...] = jnp.zeros_like(dw_ref)
            db_ref[...] = jnp.zeros_like(db_ref)

        e_v = e_ref[...]
        fn = lambda u, v, ln, w, b: _sgu_chunk(u, v, ln, w, b, e_v)
        acc = None
        for i in range(k):
            r = slice(CHUNK * i, CHUNK * (i + 1))
            _, vjp = jax.vjp(fn, u_ref[r, :], v_ref[r, :], ln_ref[...], w_ref[...], b_ref[...])
            du, dv, *dpar = vjp(dy_ref[r, :])
            du_ref[r, :] = du.astype(BF16)
            dv_ref[r, :] = dv.astype(BF16)
            acc = dpar if acc is None else [a + d for a, d in zip(acc, dpar)]
        dln_ref[...] += acc[0]
        dw_ref[...] += acc[1]
        db_ref[...] += acc[2]

    c_ln = pl.BlockSpec((SUBLANE, SGU_W), lambda i: (0, 0))
    c_w = pl.BlockSpec((SGU_GROUPS, CHUNK, CHUNK), lambda i: (0, 0, 0))
    c_b = pl.BlockSpec((CHUNK, SUBLANE), lambda i: (0, 0))
    return pl.pallas_call(
        body, name=name, grid=(T // rows,),
        in_specs=[pl.BlockSpec((rows, SGU_W), lambda i: (i, U0 // SGU_W)),
                  pl.BlockSpec((rows, SGU_W), lambda i: (i, VS0 // SGU_W)),
                  pl.BlockSpec((rows, SGU_W), lambda i: (i, ycol)),
                  c_ln, c_w, c_b, pl.BlockSpec((SUBLANE, SGU_W), lambda i: (0, 0))],
        out_specs=[pl.BlockSpec((rows, SGU_W), lambda i: (i, 0)),
                   pl.BlockSpec((rows, SGU_W), lambda i: (i, 0)), c_ln, c_w, c_b],
        out_shape=[jax.ShapeDtypeStruct((T, SGU_W), BF16), jax.ShapeDtypeStruct((T, SGU_W), BF16),
                   jax.ShapeDtypeStruct((SUBLANE, SGU_W), F32),
                   jax.ShapeDtypeStruct((SGU_GROUPS, CHUNK, CHUNK), F32),
                   jax.ShapeDtypeStruct((CHUNK, SUBLANE), F32)],
        compiler_params=_cp(("arbitrary",)),
    )(proj, proj, dy, ln, w, bst, e4)


_HBM = pl.BlockSpec(memory_space=pltpu.HBM)
_SEM = pl.BlockSpec(memory_space=pltpu.SEMAPHORE)
_ANY = pl.BlockSpec(memory_space=pl.ANY)
_EFFECT = pltpu.SideEffectType.DATAFLOW_SIDE_EFFECTING


def _peers():
    x, y, c = lax.axis_index("x"), lax.axis_index("y"), lax.axis_index("c")
    out = []
    for p in range(1, N_DEV):
        px, py, pc = x ^ ((p >> 2) & 1), y ^ ((p >> 1) & 1), c ^ (p & 1)
        out.append(((px, py, pc), 4 * px + 2 * py + pc))
    return 4 * x + 2 * y + c, out


def _xchg_start(xs, a2a, order, name):
    n = len(xs)
    lands = [lax.empty(a.shape if f else (N_DEV,) + a.shape, a.dtype) for a, f in zip(xs, a2a)]

    def body(*refs):
        ins, zones = refs[:n], refs[n:2 * n]
        send_sems, recv_sems = refs[2 * n + 1], refs[2 * n + 2]
        token = refs[-1]
        me, peers = _peers()
        for p, (dev, peer) in enumerate(peers):
            for t in range(n):
                pltpu.make_async_remote_copy(
                    src_ref=ins[t].at[peer] if a2a[t] else ins[t], dst_ref=zones[t].at[me],
                    send_sem=send_sems.at[p * n + t], recv_sem=recv_sems.at[p * n + t],
                    device_id=dev, device_id_type=MESH).start()
        token[...] = jnp.zeros_like(token)

    hbm = lambda a: pltpu.HBM(a.shape, a.dtype)
    sems = pltpu.SemaphoreType.DMA(((N_DEV - 1) * n,))
    out = pl.pallas_call(
        body, name=name,
        in_specs=[_HBM] * (2 * n) + [_ANY],
        out_specs=[_SEM, _SEM] + [_HBM] * (2 * n) + [pl.BlockSpec(memory_space=pltpu.VMEM)],
        out_shape=[sems, sems] + [hbm(a) for a in xs] + [hbm(a) for a in lands]
        + [jax.ShapeDtypeStruct((SUBLANE, LANE), F32)],
        input_output_aliases={t: 2 + t for t in range(2 * n)},
        compiler_params=pltpu.CompilerParams(has_side_effects=_EFFECT),
    )(*[pltpu.with_memory_space_constraint(a, pltpu.HBM) for a in list(xs) + list(lands)], order)
    return out[0], out[1], out[2:2 + n], out[2 + n:2 + 2 * n], out[-1]


def _xchg_wait(started, a2a, after, name):
    send_sems, recv_sems, xs, lands, _ = started
    n = len(xs)

    def body(*refs):
        ins, zones = refs[:n], refs[n:2 * n]
        send_s, recv_s = refs[2 * n], refs[2 * n + 1]
        me, peers = _peers()
        cps = []
        for p, (dev, peer) in enumerate(peers):
            for t in range(n):
                cps.append(pltpu.make_async_remote_copy(
                    src_ref=ins[t].at[peer] if a2a[t] else ins[t], dst_ref=zones[t].at[peer],
                    send_sem=send_s.at[p * n + t], recv_sem=recv_s.at[p * n + t],
                    device_id=dev, device_id_type=MESH))
        for cp in cps:
            cp.wait_recv()
        for cp in cps:
            cp.wait_send()

    hbm = lambda a: pltpu.HBM(a.shape, a.dtype)
    out = pl.pallas_call(
        body, name=name,
        in_specs=[_HBM] * (2 * n) + [_SEM, _SEM, _ANY],
        out_specs=[_HBM] * (2 * n),
        out_shape=[hbm(a) for a in xs] + [hbm(a) for a in lands],
        input_output_aliases={t: t for t in range(2 * n)},
        compiler_params=pltpu.CompilerParams(has_side_effects=_EFFECT),
    )(*xs, *lands, send_sems, recv_sems, after)
    return out[:n], out[n:]


def _chip_peers():
    x, y, c = lax.axis_index("x"), lax.axis_index("y"), lax.axis_index("c")
    chips = [(1 - x, y), (x, 1 - y), (1 - x, 1 - y)]
    slot = lambda px, py, pc: 4 * px + 2 * py + pc
    return (x, y, c), chips, slot


def _gather_start(xs, order, name):
    n = len(xs)
    lands = [lax.empty((N_DEV,) + a.shape, a.dtype) for a in xs]

    def body(*refs):
        ins, zones = refs[:n], refs[n:2 * n]
        send_sems, d2d_sems, ici_sems = refs[2 * n + 1:2 * n + 4]
        token = refs[-1]
        (x, y, c), chips, slot = _chip_peers()
        me = slot(x, y, c)
        for t in range(n):
            pltpu.make_async_copy(ins[t], zones[t].at[me], d2d_sems.at[n + t]).start()
            for j, (px, py) in enumerate(chips):
                pltpu.make_async_remote_copy(
                    src_ref=ins[t], dst_ref=zones[t].at[me], send_sem=send_sems.at[(1 + j) * n + t],
                    recv_sem=ici_sems.at[j * n + t], device_id=(px, py, c), device_id_type=MESH).start()
            pltpu.make_async_remote_copy(
                src_ref=ins[t], dst_ref=zones[t].at[me], send_sem=send_sems.at[t],
                recv_sem=d2d_sems.at[t], device_id=(x, y, 1 - c), device_id_type=MESH).start()
        token[...] = jnp.zeros_like(token)

    hbm = lambda a: pltpu.HBM(a.shape, a.dtype)
    dma = lambda k: pltpu.SemaphoreType.DMA((k,))
    out = pl.pallas_call(
        body, name=name,
        in_specs=[_HBM] * (2 * n) + [_ANY],
        out_specs=[_SEM, _SEM, _SEM] + [_HBM] * (2 * n) + [pl.BlockSpec(memory_space=pltpu.VMEM)],
        out_shape=[dma(4 * n), dma(2 * n), dma(3 * n)] + [hbm(a) for a in xs] + [hbm(a) for a in lands]
        + [jax.ShapeDtypeStruct((SUBLANE, LANE), F32)],
        input_output_aliases={t: 3 + t for t in range(2 * n)},
        compiler_params=pltpu.CompilerParams(has_side_effects=_EFFECT),
    )(*[pltpu.with_memory_space_constraint(a, pltpu.HBM) for a in list(xs) + list(lands)], order)
    return dict(send=out[0], d2d=out[1], ici=out[2], xs=out[3:3 + n], lands=out[3 + n:3 + 2 * n], token=out[-1])


def _gather_relay(st, after, name):
    n = len(st["xs"])

    def body(*refs):
        zones, ici_sems = refs[:n], refs[n]
        fsend, frecv = refs[n + 2], refs[n + 3]
        token = refs[-1]
        (x, y, c), chips, slot = _chip_peers()
        for t in range(n):
            for j, (px, py) in enumerate(chips):
                blk = zones[t].at[slot(px, py, c)]
                fwd = pltpu.make_async_remote_copy(
                    src_ref=blk, dst_ref=blk, send_sem=fsend.at[j * n + t], recv_sem=ici_sems.at[j * n + t],
                    device_id=(x, y, 1 - c), device_id_type=MESH)
                fwd.wait_recv()
                pltpu.make_async_remote_copy(
                    src_ref=blk, dst_ref=blk, send_sem=fsend.at[j * n + t], recv_sem=frecv.at[j * n + t],
                    device_id=(x, y, 1 - c), device_id_type=MESH).start()
        token[...] = jnp.zeros_like(token)

    hbm = lambda a: pltpu.HBM(a.shape, a.dtype)
    dma = lambda k: pltpu.SemaphoreType.DMA((k,))
    out = pl.pallas_call(
        body, name=name,
        in_specs=[_HBM] * n + [_SEM, _ANY],
        out_specs=[_SEM, _SEM] + [_HBM] * n + [pl.BlockSpec(memory_space=pltpu.VMEM)],
        out_shape=[dma(3 * n), dma(3 * n)] + [hbm(a) for a in st["lands"]]
        + [jax.ShapeDtypeStruct((SUBLANE, LANE), F32)],
        input_output_aliases={t: 2 + t for t in range(n)},
        compiler_params=pltpu.CompilerParams(has_side_effects=_EFFECT),
    )(*st["lands"], st["ici"], after)
    return dict(st, fsend=out[0], frecv=out[1], lands=out[2:2 + n], token=out[-1])


def _gather_wait(st, after, name):
    n = len(st["xs"])

    def body(*refs):
        ins, zones = refs[:n], refs[n:2 * n]
        send_sems, d2d_sems, fsend, frecv = refs[2 * n:2 * n + 4]
        (x, y, c), chips, slot = _chip_peers()
        sib = (x, y, 1 - c)
        for t in range(n):
            pltpu.make_async_copy(ins[t], zones[t].at[slot(x, y, c)], d2d_sems.at[n + t]).wait()
            mine = lambda s, r, dst: pltpu.make_async_remote_copy(
                src_ref=ins[t], dst_ref=dst, send_sem=s, recv_sem=r, device_id=sib, device_id_type=MESH)
            direct = mine(send_sems.at[t], d2d_sems.at[t], zones[t].at[slot(x, y, 1 - c)])
            direct.wait_recv()
            direct.wait_send()
            for j, (px, py) in enumerate(chips):
                mine(send_sems.at[(1 + j) * n + t], d2d_sems.at[t], zones[t].at[slot(px, py, c)]).wait_send()
                relayed = mine(fsend.at[j * n + t], frecv.at[j * n + t], zones[t].at[slot(px, py, 1 - c)])
                relayed.wait_recv()
                relayed.wait_send()

    hbm = lambda a: pltpu.HBM(a.shape, a.dtype)
    out = pl.pallas_call(
        body, name=name,
        in_specs=[_HBM] * (2 * n) + [_SEM] * 4 + [_ANY],
        out_specs=[_HBM] * (2 * n),
        out_shape=[hbm(a) for a in st["xs"]] + [hbm(a) for a in st["lands"]],
        input_output_aliases={t: t for t in range(2 * n)},
        compiler_params=pltpu.CompilerParams(has_side_effects=_EFFECT),
    )(*st["xs"], *st["lands"], st["send"], st["d2d"], st["fsend"], st["frecv"], after)
    return out[:n], out[n:]


def _cast_layers(pairs, name):
    def body(*refs):
        n = len(refs) // 2
        for i in range(n):
            refs[n + i][...] = refs[i][...].astype(BF16)

    in_specs = [pl.BlockSpec((None,) + w.shape[1:], functools.partial(lambda l, i: (l, 0, 0), l),
                             pipeline_mode=pl.Buffered(1)) for w, l in pairs]
    return pl.pallas_call(
        body, name=name, grid=(1,), in_specs=in_specs,
        out_specs=[pl.BlockSpec(w.shape[1:], lambda i: (0, 0)) for w, _ in pairs],
        out_shape=[jax.ShapeDtypeStruct(w.shape[1:], BF16) for w, _ in pairs],
        compiler_params=_cp(("arbitrary",)),
    )(*[w for w, _ in pairs])


ADAMW_BLOCK_ELEMS = 256 * 1024


def _adam_step(me, w, m, v, parts_ref, mine, out_refs):
    g = None
    for p in range(N_DEV):
        term = jnp.where(me == p, mine.astype(F32), parts_ref[p].astype(F32))
        g = term if g is None else g + term
    mn = ADAM_B1 * m + (1.0 - ADAM_B1) * g
    vn = ADAM_B2 * v + (1.0 - ADAM_B2) * (g * g)
    m_hat = mn / (1.0 - ADAM_B1 ** ADAM_STEP)
    v_hat = vn / (1.0 - ADAM_B2 ** ADAM_STEP)
    g_ref, d_ref, mo_ref, vo_ref = out_refs
    g_ref[...] = g
    d_ref[...] = -ADAM_LR * (m_hat / (jnp.sqrt(v_hat) + ADAM_EPS) + ADAM_WD * w)
    mo_ref[...] = mn
    vo_ref[...] = vn


def _adamw(me, w, m, v, parts, own, name, layer=0, into=None):
    L, R, C = w.shape
    P = parts.shape[0]
    tr = R
    t = 16
    while t <= R:
        if R % t == 0 and t * C <= ADAMW_BLOCK_ELEMS:
            tr = t
        t += 16
    if tr == R and R * C > ADAMW_BLOCK_ELEMS and R % 16 == 0:
        tr = 16
    own_all = own.shape[0] == P

    def body(me_ref, w_ref, m_ref, v_ref, p_ref, own_ref, *rest):
        _adam_step(me_ref[0], w_ref[...], m_ref[...], v_ref[...], p_ref, own_ref[...], rest[-4:])

    blk = pl.BlockSpec((None, tr, C), lambda i, me_ref: (layer, i, 0))
    own_blk = pl.BlockSpec((None, tr, C), lambda i, me_ref: (me_ref[0] if own_all else 0, i, 0))
    prev = list(into) if into is not None else []
    return pl.pallas_call(
        body, name=name,
        grid_spec=pltpu.PrefetchScalarGridSpec(
            num_scalar_prefetch=1, grid=(R // tr,),
            in_specs=[blk, blk, blk, pl.BlockSpec((P, tr, C), lambda i, me_ref: (0, i, 0)), own_blk]
            + [_ANY] * len(prev),
            out_specs=[blk] * 4),
        out_shape=[jax.ShapeDtypeStruct((L, R, C), F32)] * 4,
        input_output_aliases={6 + i: i for i in range(len(prev))},
        compiler_params=_cp(("parallel",)),
    )(me, w, m, v, parts, own, *prev)


def _perm_cols(w):
    pad = jnp.zeros(w.shape[:-1] + (LANE - SSD_HEADS,), w.dtype)
    return jnp.concatenate([w[..., 0:1536], w[..., 2438:2694], w[..., 1536:2432], w[..., 2432:2438], pad,
                            w[..., 2694:2950]], axis=-1)


def _unperm_cols(w):
    return jnp.concatenate([w[..., 0:1536], w[..., XBC0:XBC0 + SSD_CONV_DIM], w[..., DT0:DT0 + SSD_HEADS],
                            w[..., U0:U0 + SGU_W], w[..., VS0:VS0 + SGU_W]], axis=-1)


_SMALL = ("ffn1_norm", "mix_norm", "conv_w", "conv_b", "dt_bias", "a_log", "d_skip", "ssd_norm",
          "sgu_ln_g", "sgu_ln_b", "sgu_w", "sgu_b", "ffn2_norm", "final_norm", "loss")


_SMALL_LAST = ("ffn1_norm",)
_SMALL_EARLY = tuple(k for k in _SMALL if k not in _SMALL_LAST)


def _pack(d, names):
    v = jnp.concatenate([d[k].astype(F32).reshape(-1) for k in names])
    n = v.shape[0]
    npad = -(-n // (LANE * 16)) * (LANE * 16)
    return jnp.pad(v, (0, npad - n)).reshape(npad // LANE, LANE)


def _unpack(p, shapes, names):
    v = p.reshape(-1)
    out, o = {}, 0
    for k in names:
        n = int(np.prod(shapes[k]))
        out[k] = v[o:o + n].reshape(shapes[k])
        o += n
    return out


def kernel(x, ffn1_norm, ffn1_w_gate, ffn1_w_up, ffn1_w_down, mix_norm, w_in, conv_w, conv_b, dt_bias, a_log, d_skip, ssd_norm, sgu_ln_g, sgu_ln_b, sgu_w, sgu_b, w_out, ffn2_norm, ffn2_w_gate, ffn2_w_up, ffn2_w_down, final_norm, loss_target, m_ffn1_norm, m_ffn1_w_gate, m_ffn1_w_up, m_ffn1_w_down, m_mix_norm, m_w_in, m_conv_w, m_conv_b, m_dt_bias, m_a_log, m_d_skip, m_ssd_norm, m_sgu_ln_g, m_sgu_ln_b, m_sgu_w, m_sgu_b, m_w_out, m_ffn2_norm, m_ffn2_w_gate, m_ffn2_w_up, m_ffn2_w_down, m_final_norm, v_ffn1_norm, v_ffn1_w_gate, v_ffn1_w_up, v_ffn1_w_down, v_mix_norm, v_w_in, v_conv_w, v_conv_b, v_dt_bias, v_a_log, v_d_skip, v_ssd_norm, v_sgu_ln_g, v_sgu_ln_b, v_sgu_w, v_sgu_b, v_w_out, v_ffn2_norm, v_ffn2_w_gate, v_ffn2_w_up, v_ffn2_w_down, v_final_norm):
    B, S, D = x.shape
    T = B * S
    L = ffn1_norm.shape[0]
    me = 4 * lax.axis_index("x") + 2 * lax.axis_index("y") + lax.axis_index("c")
    cs = conv_w.shape[2]
    W = dict(ffn1_norm=ffn1_norm, ffn1_w_gate=ffn1_w_gate, ffn1_w_up=ffn1_w_up, ffn1_w_down=ffn1_w_down,
             mix_norm=mix_norm, w_in=w_in, conv_w=conv_w, conv_b=conv_b, dt_bias=dt_bias, a_log=a_log,
             d_skip=d_skip, ssd_norm=ssd_norm, sgu_ln_g=sgu_ln_g, sgu_ln_b=sgu_ln_b, sgu_w=sgu_w, sgu_b=sgu_b,
             w_out=w_out, ffn2_norm=ffn2_norm, ffn2_w_gate=ffn2_w_gate, ffn2_w_up=ffn2_w_up,
             ffn2_w_down=ffn2_w_down, final_norm=final_norm)
    M = dict(ffn1_norm=m_ffn1_norm, ffn1_w_gate=m_ffn1_w_gate, ffn1_w_up=m_ffn1_w_up, ffn1_w_down=m_ffn1_w_down,
             mix_norm=m_mix_norm, w_in=m_w_in, conv_w=m_conv_w, conv_b=m_conv_b, dt_bias=m_dt_bias, a_log=m_a_log,
             d_skip=m_d_skip, ssd_norm=m_ssd_norm, sgu_ln_g=m_sgu_ln_g, sgu_ln_b=m_sgu_ln_b, sgu_w=m_sgu_w,
             sgu_b=m_sgu_b, w_out=m_w_out, ffn2_norm=m_ffn2_norm, ffn2_w_gate=m_ffn2_w_gate,
             ffn2_w_up=m_ffn2_w_up, ffn2_w_down=m_ffn2_w_down, final_norm=m_final_norm)
    V = dict(ffn1_norm=v_ffn1_norm, ffn1_w_gate=v_ffn1_w_gate, ffn1_w_up=v_ffn1_w_up, ffn1_w_down=v_ffn1_w_down,
             mix_norm=v_mix_norm, w_in=v_w_in, conv_w=v_conv_w, conv_b=v_conv_b, dt_bias=v_dt_bias, a_log=v_a_log,
             d_skip=v_d_skip, ssd_norm=v_ssd_norm, sgu_ln_g=v_sgu_ln_g, sgu_ln_b=v_sgu_ln_b, sgu_w=v_sgu_w,
             sgu_b=v_sgu_b, w_out=v_w_out, ffn2_norm=v_ffn2_norm, ffn2_w_gate=v_ffn2_w_gate,
             ffn2_w_up=v_ffn2_w_up, ffn2_w_down=v_ffn2_w_down, final_norm=v_final_norm)
    FFN1 = ("ffn1_w_gate", "ffn1_w_up", "ffn1_w_down")
    FFN2 = ("ffn2_w_gate", "ffn2_w_up", "ffn2_w_down")
    MIX = ("w_in", "w_out")
    big = FFN1 + MIX + FFN2
    col_sharded = lambda k: k.endswith("w_gate") or k.endswith("w_up")
    for dct in (W, M, V):
        for k in big:
            if col_sharded(k):
                dct[k] = jnp.swapaxes(dct[k], 1, 2)
        dct["w_in"] = _perm_cols(dct["w_in"])

    wgroups = [[(k, 0) for k in FFN1], [("w_in", 0), ("conv_w", None)], [("w_out", 0)] + [(k, 0) for k in FFN2]]
    for l in range(1, L):
        wgroups += [[(k, l) for k in FFN1] + [("w_in", l)], [("w_out", l)] + [(k, l) for k in FFN2]]
    wstarted, order = [], x
    later = [kl for grp in wgroups[1:] for kl in grp if kl[0] != "conv_w"]
    cast = dict(zip(wgroups[0], _cast_layers([(W[k], l) for k, l in wgroups[0]], "cast_first")))
    for gi, grp in enumerate(wgroups):
        if gi == 1:
            first = lax.optimization_barrier((W[later[0][0]], order))[0]
            srcs = [(first if i == 0 else W[k], l) for i, (k, l) in enumerate(later)]
            cast.update(zip(later, _cast_layers(srcs, "cast_rest")))
        xs = [conv_w if k == "conv_w" else cast[(k, l)] for k, l in grp]
        st = _gather_start(xs, order, f"gather_start_{gi}")
        order = st["token"]
        wstarted.append(st)
    G = {}

    zero1 = jnp.zeros((1,), F32)
    W["loss"], M["loss"], V["loss"] = zero1, zero1, zero1
    full_shapes = {k: (W[k].shape if k != "conv_w" else (L, SSD_CONV, SSD_CONV_DIM)) for k in _SMALL}
    embed = lambda a, k: a if k != "conv_w" else lax.dynamic_update_slice(
        jnp.zeros(full_shapes[k], F32), a, (0, 0, me * cs))
    small_packs = {names: [_pack({k: embed(d[k], k) for k in names}, names)[None] for d in (W, M, V)]
                   for names in (_SMALL_EARLY, _SMALL_LAST)}

    def relay(gi, after):
        wstarted[gi] = _gather_relay(wstarted[gi], after, f"gather_relay_{gi}")
        return wstarted[gi]["token"]

    def gathered(gi, after):
        _, lands = _gather_wait(wstarted[gi], after, f"gather_wait_{gi}")
        G.update(zip(wgroups[gi], lands))

    def rows(k, l):
        a = G[(k, l)]
        return a.reshape(-1, a.shape[-1])

    bias = _attn_bias(S, min(256, S))
    row1 = lambda a: a.reshape(1, -1)

    def ffn1_params(l):
        return dict(g1=row1(ffn1_norm[l]), wg1=rows("ffn1_w_gate", l), wu1=rows("ffn1_w_up", l),
                    wd1=rows("ffn1_w_down", l))

    def out_params(l):
        return dict(wout=rows("w_out", l), g2=row1(ffn2_norm[l]), wg2=rows("ffn2_w_gate", l),
                    wu2=rows("ffn2_w_up", l), wd2=rows("ffn2_w_down", l))

    def mix_params(l):
        cw = jnp.transpose(G[("conv_w", None)][:, l], (1, 0, 2)).reshape(SSD_CONV, -1)
        return dict(
            gm=row1(mix_norm[l]), win=rows("w_in", l),
            cw=jnp.pad(cw, ((0, SUBLANE - SSD_CONV), (0, 0))), cb=row1(conv_b[l]),
            par=jnp.pad(jnp.stack([jnp.repeat(dt_bias[l], HEAD_DIM), jnp.repeat(a_log[l], HEAD_DIM),
                                   jnp.repeat(d_skip[l], HEAD_DIM), ssd_norm[l]]), ((0, SUBLANE - 4), (0, 0))),
            ln=jnp.pad(jnp.stack([sgu_ln_g[l], sgu_ln_b[l]]), ((0, SUBLANE - 2), (0, 0))),
            sw=sgu_w[l], bst=jnp.pad(sgu_b[l].T, ((0, 0), (0, SUBLANE - SGU_GROUPS))))

    xc = x.reshape(T, D)
    saved, lay = [], []
    for l in range(L):
        if l == 0:
            gathered(0, relay(0, order))
        else:
            gathered(1 + 2 * l, xc)
        p = ffn1_params(l)
        x1, *ffn1_saved = _ffn_fwd(xc, p["g1"], p["wg1"], p["wu1"], p["wd1"], f"ffn1_fwd_{l}")
        if l == 0:
            gathered(1, relay(1, x1))
        p.update(mix_params(l))
        lay.append(p)
        proj, ht = _norm_mm(x1, p["gm"], p["win"], f"in_proj_{l}")
        o_att, lse = _attn_fwd(proj, bias, B, S, f"attn_fwd_{l}")
        tok = relay(2 + 2 * l, o_att)
        pre = _conv_fwd(proj, p["cw"], p["cb"], B, S, f"conv_fwd_{l}", after=tok)
        y_ssd, sall = _ssd_fwd(pre, proj, p["par"], B, S, f"ssd_fwd_{l}")
        y_sgu = _sgu_fwd(proj, p["ln"], p["sw"], p["bst"], B, S, f"sgu_fwd_{l}", after=tok)
        ycat = jnp.concatenate([o_att.astype(BF16), y_ssd, y_sgu], axis=1)
        gathered(2 + 2 * l, ycat)
        p.update(out_params(l))
        tok = relay(3 + 2 * l, ycat) if l + 1 < L else None
        x2, x3, *ffn2_saved = _ffn_fwd(x1, p["g2"], p["wg2"], p["wu2"], p["wd2"], f"ffn2_fwd_{l}", after=tok,
                                       proj=(ycat, p["wout"]))
        saved.append(dict(x0=xc, ffn1=ffn1_saved, x1=x1, ht=ht, proj=proj, o_att=o_att, lse=lse, pre=pre,
                          sall=sall, ycat=ycat, x2=x2, ffn2=ffn2_saved))
        xc = x3
    loss_part, dx, dgf = _final_loss(xc, row1(final_norm), loss_target.reshape(T, D), "final_loss")

    gl = [dict() for _ in range(L)]
    gstarted, gorder = [], [order]

    def to_blocks(k, a):
        return a.reshape(N_DEV, -1, a.shape[-1]).astype(BF16)

    def send_grads(keys, l, extra, tag, small_names=None):
        xs = [to_blocks(k, gl[l][k]) for k in keys] + extra
        flags = [True] * len(keys) + [False] * len(extra)
        st = _xchg_start(xs, flags, gorder[0], f"grads_start_{tag}")
        gorder[0] = st[-1]
        gstarted.append((keys, l, st, flags, tag, small_names))

    def small_grads(names):
        sm = {}
        for k in names:
            if k == "final_norm":
                sm[k] = dgf.reshape(-1)
            elif k == "loss":
                sm[k] = loss_part[0, :1]
            else:
                sm[k] = jnp.stack([gl[l][k] for l in range(L)])
        return [_pack(sm, names)]

    def behind(a):
        return lax.optimization_barrier((a, gorder[0]))[0]

    for l in reversed(range(L)):
        p, s, g = lay[l], saved[l], gl[l]
        gfac, ufac, actt = s["ffn2"]
        dx2, dgt, dut, xn, dacc, g["ffn2_norm"] = _ffn_bwd_dx(
            dx, s["x2"], p["g2"], gfac, ufac, p["wg2"], p["wu2"], p["wd2"], f"ffn2_bwd_{l}")
        g["ffn2_w_gate"], g["ffn2_w_up"], g["ffn2_w_down"] = _ffn_dw(dgt, dut, actt, xn, dacc, f"ffn2_dw_{l}")
        if l == 0:
            send_grads(FFN2, 0, [], "l0f")
            dx2 = behind(dx2)
        dycat = _mm(dx2, p["wout"], "nt", f"out_proj_dx_{l}")
        g["w_out"] = _mm(s["ycat"], dx2, "tn", f"out_proj_dw_{l}", out_dtype=BF16, tm_cap=1024, tk_cap=512)
        dq, dk, dv = _attn_bwd(s["proj"], s["o_att"], s["lse"], dycat, bias, B, S, f"attn_bwd_{l}")
        dpre, dz, ddt, dpar = _ssd_bwd(s["pre"], s["proj"], s["sall"], dycat, p["par"], B, S, f"ssd_bwd_{l}")
        dxbc, dwb = _conv_bwd(dpre, s["proj"], p["cw"], B, S, f"conv_bwd_{l}")
        du, dvs, dln, dsw, dbst = _sgu_bwd(s["proj"], dycat, p["ln"], p["sw"], p["bst"], B, S, f"sgu_bwd_{l}")
        hsum = lambda r: r.reshape(SSD_HEADS, HEAD_DIM).sum(-1)
        g["conv_w"], g["conv_b"] = dwb[:SSD_CONV], dwb[SSD_CONV]
        g["dt_bias"], g["a_log"], g["d_skip"], g["ssd_norm"] = hsum(dpar[0]), hsum(dpar[1]), hsum(dpar[2]), dpar[3]
        g["sgu_ln_g"], g["sgu_ln_b"], g["sgu_w"], g["sgu_b"] = dln[0], dln[1], dsw, dbst[:, :SGU_GROUPS].T
        dproj = jnp.concatenate([dq, dk, dv, dz, du, dxbc, ddt, dvs], axis=1)
        g["w_in"] = _mm_resident_lhs(s["ht"], dproj, f"in_proj_dw_{l}")
        dx1, g["mix_norm"] = _norm_mm_bwd(dproj, s["x1"], p["gm"], p["win"], dx2, f"in_proj_bwd_{l}")
        if l == 0:
            send_grads(MIX, 0, small_grads(_SMALL_EARLY), "l0a", _SMALL_EARLY)
            dx1, small_packs = lax.optimization_barrier((behind(dx1), small_packs))
        gfac, ufac, actt = s["ffn1"]
        dx, dgt, dut, xn, dacc, g["ffn1_norm"] = _ffn_bwd_dx(
            dx1, s["x0"], p["g1"], gfac, ufac, p["wg1"], p["wu1"], p["wd1"], f"ffn1_bwd_{l}")
        if l > 0:
            g["ffn1_w_gate"], g["ffn1_w_up"], g["ffn1_w_down"] = _ffn_dw(dgt, dut, actt, xn, dacc,
                                                                        f"ffn1_dw_{l}")
            send_grads(big, l, [], f"l{l}")
            dx = behind(dx)
        else:
            g["ffn1_w_gate"] = _dw_one(dgt, xn, "ffn1_dwg_0")
            send_grads(("ffn1_w_gate",), 0, [], "l0b1")
            g["ffn1_w_up"] = _dw_one(dut, xn, "ffn1_dwu_0", after=gorder[0])
            send_grads(("ffn1_w_up",), 0, [], "l0b2")
            g["ffn1_w_down"] = _dw_one(actt, dacc, "ffn1_dwd_0", after=gorder[0])
    grad_x = dx.reshape(B, S, D)
    send_grads(("ffn1_w_down",), 0, small_grads(_SMALL_LAST), "l0b", _SMALL_LAST)

    res, after = {}, gorder[0]
    small_out = [dict() for _ in range(4)]
    me1 = me.reshape(1).astype(jnp.int32)
    for keys, l, st, flags, tag, names in gstarted:
        own, lands = _xchg_wait(st, flags, after, f"grads_wait_{tag}")
        for k, mine, pk in zip(keys, own, lands):
            res[k] = _adamw(me1, W[k], M[k], V[k], pk, mine, f"adamw_{k}_{l}", layer=l, into=res.get(k))
        done = [res[k][0] for k in keys]
        if names:
            outs = _adamw(me1, *small_packs[names], lands[-1], own[-1][None], f"adamw_small_{tag}")
            for d, o in zip(small_out, outs):
                u = _unpack(o, full_shapes, names)
                if "conv_w" in u:
                    u["conv_w"] = lax.dynamic_slice(u["conv_w"], (0, 0, me * cs), (L, SSD_CONV, cs))
                d.update(u)
                done.extend(u.values())
        after = lax.optimization_barrier(tuple(done))[0]
    back = lambda k, a: jnp.swapaxes(a, 1, 2) if col_sharded(k) else _unperm_cols(a) if k == "w_in" else a
    grads, deltas, new_m, new_v = [dict({k: back(k, res[k][i]) for k in big}, **small_out[i]) for i in range(4)]

    names = ("ffn1_norm", "ffn1_w_gate", "ffn1_w_up", "ffn1_w_down", "mix_norm", "w_in", "conv_w", "conv_b",
             "dt_bias", "a_log", "d_skip", "ssd_norm", "sgu_ln_g", "sgu_ln_b", "sgu_w", "sgu_b", "w_out",
             "ffn2_norm", "ffn2_w_gate", "ffn2_w_up", "ffn2_w_down", "final_norm")
    loss = grads["loss"][0]
    return (loss, grad_x, *[grads[n] for n in names], *[deltas[n] for n in names],
            *[new_m[n] for n in names], *[new_v[n] for n in names])
```
